```python
import math
import jax, jax.numpy as jnp
from jax import lax
import numpy as np

D_MODEL = 2048
BATCH = 4
SEQ = 2048
DEPTH = 2

CHUNK = 64
EPS = 1e-6
NEG = -1e30
A_HEADS = 8
A_HEAD_DIM = 64
A_LEFT_CHUNKS = 8
A_MAX_REL = 128
B_HEADS = 8
B_HEAD_DIM = 64
B_LATENT = 128
IDX_HEADS = 8
IDX_DIM = 64
TOPK_MAX = 256
Q_BLOCK = 128
C_Q_HEADS = 16
C_KV_HEADS = 2
C_HEAD_DIM = 64
C_WINDOW = 128
C_LEFT_CHUNKS = C_WINDOW // CHUNK
T5_BUCKETS = 32
T5_MAX_DIST = 256
MEM_LEN = 256
MEM_HEADS = 4
MEM_HEAD_DIM = 128
MEM_W = MEM_HEADS * MEM_HEAD_DIM
D_FF = 5504
CONV_W = 3

A_W = A_HEADS * A_HEAD_DIM
B_W = B_HEADS * B_HEAD_DIM
C_W = C_Q_HEADS * C_HEAD_DIM
C_KV_W = C_KV_HEADS * C_HEAD_DIM
C_GROUP = C_Q_HEADS // C_KV_HEADS
N_BRANCH = 3
MIX_W = A_W + B_W + C_W
IN_SIZES = (A_W, A_W, A_W, B_W, B_LATENT, IDX_HEADS * IDX_DIM, IDX_DIM, IDX_HEADS, C_W, C_KV_W, C_KV_W, N_BRANCH * D_MODEL)
IN_W = 3 * A_W + B_W + B_LATENT + IDX_HEADS * IDX_DIM + IDX_DIM + IDX_HEADS + C_W + 2 * C_KV_W + N_BRANCH * D_MODEL

kernel_name = 'hybrid_chunk_streaming_encoder'


def rmsnorm(x, g):
    xf = x.astype(jnp.float32)
    y = xf * lax.rsqrt(jnp.mean(xf * xf, axis=-1, keepdims=True) + EPS)
    return (y * g.astype(jnp.float32)).astype(x.dtype)


def split_cols(t, sizes):
    offs = np.cumsum(sizes)[:-1].tolist()
    return jnp.split(t, offs, axis=-1)


def t5_bucket(rel):
    half = T5_BUCKETS // 2
    max_exact = half // 2
    sign = jnp.where(rel > 0, half, 0)
    d = jnp.abs(rel)
    d_f = jnp.maximum(d, 1).astype(jnp.float32)
    large = max_exact + (jnp.log(d_f / max_exact) / math.log(T5_MAX_DIST / max_exact) * (half - max_exact)).astype(jnp.int32)
    large = jnp.minimum(large, half - 1)
    return sign + jnp.where(d < max_exact, d, large)


def chunk_band(t, n_left):
    b, s = t.shape[:2]
    nc = s // CHUNK
    tc = t.reshape((b, nc, CHUNK) + t.shape[2:])
    tp = jnp.pad(tc, [(0, 0), (n_left, 0)] + [(0, 0)] * (tc.ndim - 2))
    return jnp.concatenate([tp[:, j:j + nc] for j in range(n_left + 1)], axis=2)


def band_valid(nc, n_left):
    band = (n_left + 1) * CHUNK
    key_chunk = jnp.arange(nc)[:, None] - n_left + (jnp.arange(band) // CHUNK)[None, :]
    return key_chunk >= 0


def chunk_attention_a(q, k, v, rel_bias):
    b, s = q.shape[:2]
    nc = s // CHUNK
    band = (A_LEFT_CHUNKS + 1) * CHUNK
    qc = q.reshape(b, nc, CHUNK, A_HEADS, A_HEAD_DIM)
    kb = chunk_band(k, A_LEFT_CHUNKS)
    vb = chunk_band(v, A_LEFT_CHUNKS)
    logits = jnp.einsum('bcihd,bcjhd->bchij', qc, kb).astype(jnp.float32) * (A_HEAD_DIM ** -0.5)
    rel = A_LEFT_CHUNKS * CHUNK + jnp.arange(CHUNK)[:, None] - jnp.arange(band)[None, :]
    idx = jnp.clip(rel, -A_MAX_REL, A_MAX_REL) + A_MAX_REL
    bias = jnp.transpose(rel_bias[idx], (2, 0, 1)).astype(jnp.float32)
    logits = logits + bias[None, None]
    valid = band_valid(nc, A_LEFT_CHUNKS)
    logits = jnp.where(valid[None, :, None, None, :], logits, NEG)
    p = jax.nn.softmax(logits, axis=-1).astype(v.dtype)
    o = jnp.einsum('bchij,bcjhd->bcihd', p, vb)
    return o.reshape(b, s, A_W)


def dsa_attention_b(q, ckv, qi, ki, wi, w_uk, w_uv, t5_b):
    b, s = q.shape[:2]
    nb = s // Q_BLOCK
    topk = min(TOPK_MAX, s // 4)
    q_lat = jnp.einsum('bshd,hcd->bshc', q, w_uk)
    wi_s = wi * (IDX_HEADS ** -0.5)
    kpos = jnp.arange(s, dtype=jnp.int32)

    def to_blocks(t):
        return jnp.moveaxis(t.reshape((b, nb, Q_BLOCK) + t.shape[2:]), 1, 0)

    def block(args):
        ql, qib, wib, qpos = args
        dots = jnp.einsum('bthd,bsd->bths', qib, ki).astype(jnp.float32) * (IDX_DIM ** -0.5)
        score = jnp.einsum('bths,bth->bts', jax.nn.relu(dots), wib.astype(jnp.float32))
        adm = (kpos // CHUNK)[None, :] <= (qpos // CHUNK)[:, None]
        score = jnp.where(adm[None], score, NEG)
        _, idx = lax.top_k(score, topk)
        valid = (idx // CHUNK) <= (qpos // CHUNK)[None, :, None]
        c_sel = jax.vmap(lambda c, i: c[i])(ckv, idx)
        bias = t5_b[t5_bucket(idx - qpos[None, :, None])]
        logits = jnp.einsum('bthc,btkc->bthk', ql, c_sel).astype(jnp.float32) * (B_HEAD_DIM ** -0.5)
        logits = logits + jnp.moveaxis(bias, -1, 2).astype(jnp.float32)
        logits = jnp.where(valid[:, :, None, :], logits, NEG)
        p = jax.nn.softmax(logits, axis=-1).astype(ckv.dtype)
        o_lat = jnp.einsum('bthk,btkc->bthc', p, c_sel)
        return jnp.einsum('bthc,hcd->bthd', o_lat, w_uv).reshape(b, Q_BLOCK, B_W)

    out = lax.map(block, (to_blocks(q_lat), to_blocks(qi), to_blocks(wi_s), kpos.reshape(nb, Q_BLOCK)))
    return jnp.moveaxis(out, 0, 1).reshape(b, s, B_W)


def sliding_sink_attention_c(q, k, v, t5_c, sinks):
    b, s = q.shape[:2]
    nc = s // CHUNK
    band = (C_LEFT_CHUNKS + 1) * CHUNK
    qc = q.reshape(b, nc, CHUNK, C_KV_HEADS, C_GROUP, C_HEAD_DIM)
    kb = chunk_band(k, C_LEFT_CHUNKS)
    vb = chunk_band(v, C_LEFT_CHUNKS)
    logits = jnp.einsum('bcikgd,bcjkd->bckgij', qc, kb).astype(jnp.float32) * (C_HEAD_DIM ** -0.5)
    rel = jnp.arange(band)[None, :] - (C_LEFT_CHUNKS * CHUNK + jnp.arange(CHUNK)[:, None])
    bias = jnp.transpose(t5_c[t5_bucket(rel)], (2, 0, 1)).reshape(C_KV_HEADS, C_GROUP, CHUNK, band)
    logits = logits + bias[None, None].astype(jnp.float32)
    valid = band_valid(nc, C_LEFT_CHUNKS)
    logits = jnp.where(valid[None, :, None, None, None, :], logits, NEG)
    sk = sinks.reshape(C_KV_HEADS, C_GROUP).astype(jnp.float32)[None, None, :, :, None, None]
    sk = jnp.broadcast_to(sk, logits.shape[:-1] + (1,))
    p = jax.nn.softmax(jnp.concatenate([logits, sk], axis=-1), axis=-1)[..., :-1].astype(v.dtype)
    o = jnp.einsum('bckgij,bcjkd->bcikgd', p, vb)
    return o.reshape(b, s, C_W)


def hybrid_mixer(h, t5_table, w_in, a_rel_bias, ckv_gain, w_uk, w_uv, sinks, w_branch, w_o):
    b, s, _ = h.shape
    qa, ka, va, qb, ckv, qi, ki, wi, qc, kc, vc, gl = split_cols(h @ w_in, IN_SIZES)
    oa = chunk_attention_a(qa.reshape(b, s, A_HEADS, A_HEAD_DIM), ka.reshape(b, s, A_HEADS, A_HEAD_DIM),
                           va.reshape(b, s, A_HEADS, A_HEAD_DIM), a_rel_bias)
    ob = dsa_attention_b(qb.reshape(b, s, B_HEADS, B_HEAD_DIM), rmsnorm(ckv, ckv_gain),
                         qi.reshape(b, s, IDX_HEADS, IDX_DIM), ki, wi, w_uk, w_uv, t5_table[:, :B_HEADS])
    oc = sliding_sink_attention_c(qc.reshape(b, s, C_Q_HEADS, C_HEAD_DIM), kc.reshape(b, s, C_KV_HEADS, C_HEAD_DIM),
                                  vc.reshape(b, s, C_KV_HEADS, C_HEAD_DIM), t5_table[:, B_HEADS:], sinks)
    gates = jax.nn.sigmoid(gl.astype(jnp.float32)).astype(h.dtype).reshape(b, s, N_BRANCH, D_MODEL)
    wa, wb, wc = jnp.split(w_branch, [A_W, A_W + B_W], axis=0)
    merged = gates[:, :, 0] * (oa @ wa) + gates[:, :, 1] * (ob @ wb) + gates[:, :, 2] * (oc @ wc)
    return merged @ w_o


def memory_xattn(h, mem_n, w_mq, w_mkv, w_mo):
    b, s, _ = h.shape
    m = mem_n.shape[1]
    q = (h @ w_mq).reshape(b, s, MEM_HEADS, MEM_HEAD_DIM)
    k, v = jnp.split((mem_n @ w_mkv).reshape(b, m, 2 * MEM_HEADS, MEM_HEAD_DIM), 2, axis=2)
    logits = jnp.einsum('bshd,bmhd->bhsm', q, k).astype(jnp.float32) * (MEM_HEAD_DIM ** -0.5)
    p = jax.nn.softmax(logits, axis=-1).astype(v.dtype)
    o = jnp.einsum('bhsm,bmhd->bshd', p, v).reshape(b, s, MEM_W)
    return o @ w_mo


def conv_ffn(h, w_up, conv_w, conv_b, w_down):
    s = h.shape[1]
    u = h @ w_up
    up = jnp.pad(u, ((0, 0), (CONV_W - 1, 0), (0, 0)))
    acc = conv_b + up[:, 0:s] * conv_w[0]
    for j in range(1, CONV_W):
        acc = acc + up[:, j:j + s] * conv_w[j]
    gate, val = jnp.split(acc, 2, axis=-1)
    return (jax.nn.gelu(gate) * val) @ w_down


def setup_inputs(seed: int = 0) -> dict:
    key = jax.random.key(seed)
    ks = jax.random.split(key, 22)
    f32 = jnp.float32

    def nrm(k, shape, scale):
        return jax.random.normal(k, shape, f32) * scale

    return {
        'x': nrm(ks[0], (BATCH, SEQ, D_MODEL), 1.0),
        'mem': nrm(ks[1], (BATCH, MEM_LEN, D_MODEL), 1.0),
        't5_table': nrm(ks[2], (T5_BUCKETS, B_HEADS + C_Q_HEADS), 0.1),
        'norm_gains': 1.0 + nrm(ks[3], (DEPTH, 6, D_MODEL), 0.05),
        'w_in': nrm(ks[4], (DEPTH, D_MODEL, IN_W), D_MODEL ** -0.5),
        'a_rel_bias': nrm(ks[5], (DEPTH, 2 * A_MAX_REL + 1, A_HEADS), 0.1),
        'ckv_gain': 1.0 + nrm(ks[6], (DEPTH, B_LATENT), 0.05),
        'w_uk': nrm(ks[7], (DEPTH, B_HEADS, B_LATENT, B_HEAD_DIM), B_LATENT ** -0.5),
        'w_uv': nrm(ks[8], (DEPTH, B_HEADS, B_LATENT, B_HEAD_DIM), B_LATENT ** -0.5),
        'sinks': nrm(ks[9], (DEPTH, C_Q_HEADS), 0.5),
        'w_branch': nrm(ks[10], (DEPTH, MIX_W, D_MODEL), A_W ** -0.5),
        'w_o': nrm(ks[11], (DEPTH, D_MODEL, D_MODEL), D_MODEL ** -0.5),
        'mem_gain': 1.0 + nrm(ks[12], (DEPTH, D_MODEL), 0.05),
        'w_mq': nrm(ks[13], (DEPTH, D_MODEL, MEM_W), D_MODEL ** -0.5),
        'w_mkv': nrm(ks[14], (DEPTH, D_MODEL, 2 * MEM_W), D_MODEL ** -0.5),
        'w_mo': nrm(ks[15], (DEPTH, MEM_W, D_MODEL), MEM_W ** -0.5),
        'w_up': nrm(ks[16], (DEPTH, D_MODEL, 2 * D_FF), D_MODEL ** -0.5),
        'conv_w': nrm(ks[17], (DEPTH, CONV_W, 2 * D_FF), CONV_W ** -0.5),
        'conv_b': nrm(ks[18], (DEPTH, 2 * D_FF), 0.02),
        'w_down': nrm(ks[19], (DEPTH, D_FF, D_MODEL), D_FF ** -0.5),
    }


def reference(x, mem, t5_table, norm_gains, w_in, a_rel_bias, ckv_gain, w_uk, w_uv, sinks, w_branch, w_o,
              mem_gain, w_mq, w_mkv, w_mo, w_up, conv_w, conv_b, w_down):
    for l in range(DEPTH):
        g = norm_gains[l]
        h = rmsnorm(x, g[0])
        y = hybrid_mixer(h, t5_table, w_in[l], a_rel_bias[l], ckv_gain[l], w_uk[l], w_uv[l], sinks[l], w_branch[l], w_o[l])
        x = x + rmsnorm(y, g[1])
        h = rmsnorm(x, g[2])
        y = memory_xattn(h, rmsnorm(mem, mem_gain[l]), w_mq[l], w_mkv[l], w_mo[l])
        x = x + rmsnorm(y, g[3])
        h = rmsnorm(x, g[4])
        y = conv_ffn(h, w_up[l], conv_w[l], conv_b[l], w_down[l])
        x = x + rmsnorm(y, g[5])
    return x
```

```python
import functools
import math

import numpy as np
import jax
import jax.numpy as jnp
from jax import lax
from jax.experimental import pallas as pl
from jax.experimental.pallas import tpu as pltpu

D_MODEL = 2048
BATCH = 4
SEQ = 2048
DEPTH = 2
TOKENS = BATCH * SEQ
CHUNK = 64
EPS = 1e-6
NEG = -1e30
A_HEADS = 8
A_LEFT_CHUNKS = 8
A_MAX_REL = 128
A_W = 512
B_HEADS = 8
B_W = 512
B_LATENT = 128
IDX_HEADS = 8
IDX_DIM = 64
TOPK = 256
C_Q_HEADS = 16
C_GROUP = 8
C_W = 1024
C_LEFT_CHUNKS = 2
T5_BUCKETS = 32
T5_MAX_DIST = 256
MEM_LEN = 256
MEM_HEADS = 4
MEM_HEAD_DIM = 128
MEM_W = 512
D_FF = 5504
CONV_W = 3

LANES = 128
SUBLANES = 8
HEAD_DIM = 64
QB = 128
KB = 256
VMEM_LIMIT = 56 * 1024 * 1024

P_QA, P_KA, P_VA, P_QB = 0, 512, 1024, 1536
P_QI = 2048
P_CKV = 2560
P_KI = 2688
P_WI = 2816
P_KC = 2944
P_QC = 3072
P_VC = 4096
P_REST = 4608
P_GATE = P_REST
IN_WP = P_GATE + 3 * D_MODEL
IN_BN = 1536
IN_BM = 1024
GATE_TILE0 = P_GATE // IN_BN

FF_P = 5632
FF_BN = 512
FF_BM = 1024
DOWN_BK = 1408

A_WIN = (A_LEFT_CHUNKS + 2) * CHUNK
C_WIN = (C_LEFT_CHUNKS + 2) * CHUNK
A_PAD = A_LEFT_CHUNKS * CHUNK
C_PAD = C_LEFT_CHUNKS * CHUNK
B_NEAR = 4
NKB = SEQ // KB

_NT = (((1,), (1,)), ((), ()))


def _cparams(sem):
    return pltpu.CompilerParams(dimension_semantics=sem, vmem_limit_bytes=VMEM_LIMIT)


def _t5_bucket(rel):
    half = T5_BUCKETS // 2
    max_exact = half // 2
    sign = jnp.where(rel > 0, half, 0)
    d = jnp.abs(rel)
    d_f = jnp.maximum(d, 1).astype(jnp.float32)
    large = max_exact + (jnp.log(d_f / max_exact) / math.log(T5_MAX_DIST / max_exact) * (half - max_exact)).astype(jnp.int32)
    large = jnp.minimum(large, half - 1)
    return sign + jnp.where(d < max_exact, d, large)


def _far_bucket_is_constant():
    d = np.arange(QB * B_NEAR - (KB - 1), SEQ, dtype=np.float32)
    assert d[0] > T5_MAX_DIST
    large = 8 + (np.log(d / 8) / math.log(T5_MAX_DIST / 8) * 8).astype(np.int32)
    return bool(np.all(np.minimum(large, 15) == 15))


assert _far_bucket_is_constant()


def _rms(v, g):
    return v * lax.rsqrt(jnp.mean(v * v, axis=-1, keepdims=True) + EPS) * g


def _norm_kernel(x_ref, g_ref, o_ref):
    o_ref[...] = _rms(x_ref[...], g_ref[...]).astype(o_ref.dtype)


def _norm(x, g, bm=1024):
    t, d = x.shape
    return pl.pallas_call(
        _norm_kernel,
        grid=(t // bm,),
        in_specs=[pl.BlockSpec((bm, d), lambda i: (i, 0)), pl.BlockSpec((1, d), lambda i: (0, 0))],
        out_specs=pl.BlockSpec((bm, d), lambda i: (i, 0)),
        out_shape=jax.ShapeDtypeStruct((t, d), jnp.bfloat16),
        compiler_params=_cparams(("parallel",)),
        name="rmsnorm",
    )(x, g)


def _inproj_kernel(h_ref, w_ref, o_ref):
    j = pl.program_id(0)
    acc = jnp.dot(h_ref[...], w_ref[...], preferred_element_type=jnp.float32)

    @pl.when(j < GATE_TILE0)
    def _():
        o_ref[...] = acc.astype(o_ref.dtype)

    @pl.when(j >= GATE_TILE0)
    def _():
        o_ref[...] = jax.nn.sigmoid(acc).astype(o_ref.dtype)


def _inproj(h, w):
    t, k = h.shape
    n = w.shape[1]
    return pl.pallas_call(
        _inproj_kernel,
        grid=(n // IN_BN, t // IN_BM),
        in_specs=[pl.BlockSpec((IN_BM, k), lambda j, i: (i, 0)), pl.BlockSpec((k, IN_BN), lambda j, i: (0, j))],
        out_specs=pl.BlockSpec((IN_BM, IN_BN), lambda j, i: (i, j)),
        out_shape=jax.ShapeDtypeStruct((t, n), jnp.bfloat16),
        compiler_params=_cparams(("parallel", "parallel")),
        name="in_proj",
    )(h, w)


def _lane_masks(dtype):
    lane = lax.broadcasted_iota(jnp.int32, (1, LANES), 1)
    lo = (lane < HEAD_DIM).astype(jnp.float32)
    return lo.astype(dtype), (1.0 - lo).astype(dtype)


def _attn_a_kernel(q_ref, k_ref, v_ref, bias_ref, o_ref):
    i = pl.program_id(1)
    start = pl.multiple_of(i * QB, QB)
    kw = k_ref[pl.ds(start, A_WIN), :]
    vw = v_ref[pl.ds(start, A_WIN), :]
    mlo, mhi = _lane_masks(jnp.bfloat16)
    lane = lax.broadcasted_iota(jnp.int32, (QB, LANES), 1)
    kchunk = lax.broadcasted_iota(jnp.int32, (QB, A_WIN), 1) // CHUNK
    pad_mask = jnp.where(kchunk >= A_LEFT_CHUNKS - 2 * i, 0.0, NEG)
    scale = HEAD_DIM ** -0.5
    for p in range(A_HEADS // 2):
        cols = slice(p * LANES, (p + 1) * LANES)
        qp = q_ref[:, cols]
        kp = kw[:, cols]
        vp = vw[:, cols]
        outs = []
        for half, msk in enumerate((mlo, mhi)):
            s = lax.dot_general(qp * msk, kp, _NT, preferred_element_type=jnp.float32)
            s = s * scale + bias_ref[2 * p + half] + pad_mask
            m = jnp.max(s, axis=-1, keepdims=True)
            e = jnp.exp(s - m)
            l = jnp.sum(e, axis=-1, keepdims=True)
            o = jnp.dot(e.astype(jnp.bfloat16), vp, preferred_element_type=jnp.float32)
            outs.append(o / l)
        o_ref[:, cols] = jnp.where(lane < HEAD_DIM, outs[0], outs[1]).astype(o_ref.dtype)


def _attn_a(proj3, kpad, vpad, bias):
    nq = SEQ // QB
    return pl.pallas_call(
        _attn_a_kernel,
        grid=(BATCH, nq),
        in_specs=[
            pl.BlockSpec((None, QB, A_W), lambda b, i: (b, i, P_QA // A_W)),
            pl.BlockSpec((None, SEQ + A_PAD, A_W), lambda b, i: (b, 0, 0)),
            pl.BlockSpec((None, SEQ + A_PAD, A_W), lambda b, i: (b, 0, 0)),
            pl.BlockSpec((A_HEADS, QB, A_WIN), lambda b, i: (0, 0, 0)),
        ],
        out_specs=pl.BlockSpec((None, QB, A_W), lambda b, i: (b, i, 0)),
        out_shape=jax.ShapeDtypeStruct((BATCH, SEQ, A_W), jnp.bfloat16),
        compiler_params=_cparams(("parallel", "parallel")),
        name="attn_a",
    )(proj3, kpad, vpad, bias)


def _attn_c_kernel(sink_ref, q_ref, k_ref, v_ref, bias_ref, o_ref):
    i = pl.program_id(1)
    start = pl.multiple_of(i * QB, QB)
    kw = k_ref[pl.ds(start, C_WIN), :]
    vw = v_ref[pl.ds(start, C_WIN), :]
    mlo, mhi = _lane_masks(jnp.bfloat16)
    lane = lax.broadcasted_iota(jnp.int32, (QB, LANES), 1)
    kchunk = lax.broadcasted_iota(jnp.int32, (QB, C_WIN), 1) // CHUNK
    pad_mask = jnp.where(kchunk >= C_LEFT_CHUNKS - 2 * i, 0.0, NEG)
    scale = HEAD_DIM ** -0.5
    qs = [q_ref[:, p * LANES:(p + 1) * LANES] for p in range(C_GROUP)]
    halves = []
    for g, msk in enumerate((mlo, mhi)):
        qg = jnp.concatenate([q * msk for q in qs], axis=0)
        s_all = lax.dot_general(qg, kw, _NT, preferred_element_type=jnp.float32)
        ps, ls = [], []
        for p in range(C_GROUP):
            h = g * C_GROUP + p
            rows = slice(p * QB, (p + 1) * QB)
            s = s_all[rows] * scale + bias_ref[h * QB:(h + 1) * QB, :] + pad_mask
            sink = sink_ref[h]
            m = jnp.maximum(jnp.max(s, axis=-1, keepdims=True), sink)
            e = jnp.exp(s - m)
            ls.append(jnp.sum(e, axis=-1, keepdims=True) + jnp.exp(sink - m))
            ps.append(e.astype(jnp.bfloat16))
        o_all = jnp.dot(jnp.concatenate(ps, axis=0), vw, preferred_element_type=jnp.float32)
        halves.append([o_all[p * QB:(p + 1) * QB] / ls[p] for p in range(C_GROUP)])
    for p in range(C_GROUP):
        o_ref[:, p * LANES:(p + 1) * LANES] = jnp.where(lane < HEAD_DIM, halves[0][p], halves[1][p]).astype(o_ref.dtype)


def _attn_c(sinks, proj3, kpad, vpad, bias):
    nq = SEQ // QB
    return pl.pallas_call(
        _attn_c_kernel,
        grid=(BATCH, nq),
        in_specs=[
            pl.BlockSpec(memory_space=pltpu.SMEM),
            pl.BlockSpec((None, QB, C_W), lambda b, i: (b, i, P_QC // C_W)),
            pl.BlockSpec((None, SEQ + C_PAD, LANES), lambda b, i: (b, 0, 0)),
            pl.BlockSpec((None, SEQ + C_PAD, LANES), lambda b, i: (b, 0, 0)),
            pl.BlockSpec((C_Q_HEADS * QB, C_WIN), lambda b, i: (0, 0)),
        ],
        out_specs=pl.BlockSpec((None, QB, C_W), lambda b, i: (b, i, 0)),
        out_shape=jax.ShapeDtypeStruct((BATCH, SEQ, C_W), jnp.bfloat16),
        compiler_params=_cparams(("parallel", "parallel")),
        name="attn_c",
    )(sinks, proj3, kpad, vpad, bias)


def _sort_key(x):
    bits = lax.bitcast_convert_type(x + 0.0, jnp.int32)
    return bits ^ ((bits >> 31) & jnp.int32(0x7FFFFFFF))


def _attn_b_kernel(qb_ref, qi_ref, wi_ref, ki_ref, ckv_ref, gain_ref, wuk_ref, wuv_ref, bias_ref, tri_ref,
                   o_ref, ckvn, ql, qi_all, wib, sk, m_s, l_s, acc_s, p_s):
    i = pl.program_id(1)
    nkb = i // 2 + 1
    mlo, mhi = _lane_masks(jnp.bfloat16)

    @pl.when(i == 0)
    def _():
        c = ckv_ref[...].astype(jnp.float32)
        ckvn[...] = _rms(c, gain_ref[...]).astype(ckvn.dtype)

    wi = wi_ref[...].astype(jnp.float32) * (IDX_HEADS ** -0.5 * IDX_DIM ** -0.5)
    for h in range(B_HEADS):
        rows = slice(h * QB, (h + 1) * QB)
        cols = slice((h // 2) * LANES, (h // 2 + 1) * LANES)
        qlat = jnp.dot(qb_ref[:, cols], wuk_ref[h], preferred_element_type=jnp.float32)
        ql[rows, :] = (qlat * HEAD_DIM ** -0.5).astype(ql.dtype)
        qi_all[rows, :] = qi_ref[:, cols] * (mhi if h % 2 else mlo)
        wib[h] = jnp.broadcast_to(wi[:, h:h + 1], (QB, KB))

    qpos_chunk = (i * QB + lax.broadcasted_iota(jnp.int32, (QB, KB), 0)) // CHUNK
    kloc = lax.broadcasted_iota(jnp.int32, (QB, KB), 1)

    def admissible(kb):
        return (kb * KB + kloc) // CHUNK <= qpos_chunk

    def score_block(kb, carry):
        kblk = ki_ref[pl.ds(pl.multiple_of(kb * KB, KB), KB), :]
        dots = lax.dot_general(qi_all[...], kblk, _NT, preferred_element_type=jnp.float32)
        score = jnp.zeros((QB, KB), jnp.float32)
        for h in range(IDX_HEADS):
            score = score + jnp.maximum(dots[h * QB:(h + 1) * QB], 0.0) * wib[h]
        score = jnp.where(admissible(kb), score, NEG)
        sk[kb] = _sort_key(score)
        return carry

    lax.fori_loop(0, nkb, score_block, 0)

    neg_key = _sort_key(jnp.full((1, 1), NEG, jnp.float32))
    n_rest = ((NKB - nkb) * KB).astype(jnp.float32)

    def count(pred_fn, v):
        def body(kb, acc):
            hit = jnp.where(pred_fn(sk[kb], v), 1.0, 0.0)
            return acc + hit[:, :LANES] + hit[:, LANES:]

        acc = lax.fori_loop(0, nkb, body, jnp.zeros((QB, LANES), jnp.float32))
        return jnp.sum(acc, axis=-1, keepdims=True) + jnp.where(pred_fn(neg_key, v), n_rest, 0.0)

    ge = lambda a, v: a >= v
    int_min = jnp.int32(-2 ** 31)
    v0 = jnp.where(count(ge, jnp.zeros((QB, 1), jnp.int32)) >= TOPK, jnp.int32(0), int_min)

    def bit_step(t, v):
        cand = v | (jnp.int32(1) << (30 - t))
        return jnp.where(count(ge, cand) >= TOPK, cand, v)

    thr = lax.fori_loop(0, 31, bit_step, v0)
    need = TOPK - count(lambda a, v: a > v, thr)

    m_s[...] = jnp.full(m_s.shape, 4 * NEG, jnp.float32)
    l_s[...] = jnp.zeros(l_s.shape, jnp.float32)
    acc_s[...] = jnp.zeros(acc_s.shape, jnp.float32)

    def attend(kb, eq_seen):
        keys = sk[kb]
        eq = keys == thr
        eq_f = jnp.where(eq, 1.0, 0.0)
        before = jnp.dot(eq_f.astype(jnp.bfloat16), tri_ref[...], preferred_element_type=jnp.float32)
        take_eq = jnp.where(eq, jnp.where(eq_seen + before < need, 1.0, 0.0), 0.0)
        take = jnp.where(keys > thr, 1.0, take_eq)
        mask_add = jnp.where(admissible(kb), jnp.where(take > 0.0, 0.0, NEG), NEG)
        cblk = ckvn[pl.ds(pl.multiple_of(kb * KB, KB), KB), :]
        dots = lax.dot_general(ql[...], cblk, _NT, preferred_element_type=jnp.float32)
        tile = jnp.minimum(i - 2 * kb, B_NEAR)
        for h in range(B_HEADS):
            rows = slice(h * QB, (h + 1) * QB)
            s = dots[rows] + bias_ref[tile, rows, :] + mask_add
            m_old = m_s[rows, :]
            m_new = jnp.maximum(m_old, jnp.max(s, axis=-1, keepdims=True))
            alpha = jnp.exp(m_old - m_new)
            e = jnp.exp(s - jnp.tile(m_new, (1, KB // LANES)))
            l_s[rows, :] = alpha * l_s[rows, :] + jnp.sum(e, axis=-1, keepdims=True)
            m_s[rows, :] = m_new
            acc_s[rows, :] = acc_s[rows, :] * alpha
            p_s[rows, :] = e.astype(p_s.dtype)
        acc_s[...] += jnp.dot(p_s[...], cblk, preferred_element_type=jnp.float32)
        return eq_seen + jnp.sum(eq_f, axis=-1, keepdims=True)

    lax.fori_loop(0, nkb, attend, jnp.zeros((QB, 1), jnp.float32))

    for p in range(B_HEADS // 2):
        out = jnp.zeros((QB, LANES), jnp.float32)
        for h in (2 * p, 2 * p + 1):
            rows = slice(h * QB, (h + 1) * QB)
            o_lat = (acc_s[rows, :] / l_s[rows, :]).astype(jnp.bfloat16)
            out = out + jnp.dot(o_lat, wuv_ref[h], preferred_element_type=jnp.float32)
        o_ref[:, p * LANES:(p + 1) * LANES] = out.astype(o_ref.dtype)


def _attn_b(proj3, gain, wuk, wuv, bias, tri):
    nq = SEQ // QB
    hq = B_HEADS * QB
    return pl.pallas_call(
        _attn_b_kernel,
        grid=(BATCH, nq),
        in_specs=[
            pl.BlockSpec((None, QB, B_W), lambda b, i: (b, i, P_QB // B_W)),
            pl.BlockSpec((None, QB, IDX_HEADS * IDX_DIM), lambda b, i: (b, i, P_QI // (IDX_HEADS * IDX_DIM))),
            pl.BlockSpec((None, QB, LANES), lambda b, i: (b, i, P_WI // LANES)),
            pl.BlockSpec((None, SEQ, LANES), lambda b, i: (b, 0, P_KI // LANES)),
            pl.BlockSpec((None, SEQ, LANES), lambda b, i: (b, 0, P_CKV // LANES)),
            pl.BlockSpec((1, B_LATENT), lambda b, i: (0, 0)),
            pl.BlockSpec((B_HEADS, LANES, B_LATENT), lambda b, i: (0, 0, 0)),
            pl.BlockSpec((B_HEADS, B_LATENT, LANES), lambda b, i: (0, 0, 0)),
            pl.BlockSpec((B_NEAR + 1, hq, KB), lambda b, i: (0, 0, 0)),
            pl.BlockSpec((KB, KB), lambda b, i: (0, 0)),
        ],
        out_specs=pl.BlockSpec((None, QB, B_W), lambda b, i: (b, i, 0)),
        out_shape=jax.ShapeDtypeStruct((BATCH, SEQ, B_W), jnp.bfloat16),
        scratch_shapes=[
            pltpu.VMEM((SEQ, B_LATENT), jnp.bfloat16),
            pltpu.VMEM((hq, B_LATENT), jnp.bfloat16),
            pltpu.VMEM((hq, LANES), jnp.bfloat16),
            pltpu.VMEM((IDX_HEADS, QB, KB), jnp.float32),
            pltpu.VMEM((NKB, QB, KB), jnp.int32),
            pltpu.VMEM((hq, LANES), jnp.float32),
            pltpu.VMEM((hq, LANES), jnp.float32),
            pltpu.VMEM((hq, B_LATENT), jnp.float32),
            pltpu.VMEM((hq, KB), jnp.bfloat16),
        ],
        compiler_params=_cparams(("parallel", "arbitrary")),
        name="attn_b",
    )(proj3, proj3, proj3, proj3, proj3, gain, wuk, wuv, bias, tri)


MERGE_BN = 512
MERGE_BM = 1024


def _merge_kernel(oa_ref, ob_ref, oc_ref, ga_ref, gb_ref, gc_ref, wa_ref, wb_ref, wc_ref, o_ref):
    f32 = jnp.float32
    m = ga_ref[...].astype(f32) * jnp.dot(oa_ref[...], wa_ref[...], preferred_element_type=f32)
    m = m + gb_ref[...].astype(f32) * jnp.dot(ob_ref[...], wb_ref[...], preferred_element_type=f32)
    m = m + gc_ref[...].astype(f32) * jnp.dot(oc_ref[...], wc_ref[...], preferred_element_type=f32)
    o_ref[...] = m.astype(o_ref.dtype)


def _merge(oa, ob, oc, proj, wa, wb, wc):
    t = oa.shape[0]
    bn, bm = MERGE_BN, MERGE_BM
    g0 = P_GATE // bn
    gstep = D_MODEL // bn
    return pl.pallas_call(
        _merge_kernel,
        grid=(D_MODEL // bn, t // bm),
        in_specs=[
            pl.BlockSpec((bm, A_W), lambda j, i: (i, 0)),
            pl.BlockSpec((bm, B_W), lambda j, i: (i, 0)),
            pl.BlockSpec((bm, C_W), lambda j, i: (i, 0)),
            pl.BlockSpec((bm, bn), lambda j, i: (i, g0 + j)),
            pl.BlockSpec((bm, bn), lambda j, i: (i, g0 + gstep + j)),
            pl.BlockSpec((bm, bn), lambda j, i: (i, g0 + 2 * gstep + j)),
            pl.BlockSpec((A_W, bn), lambda j, i: (0, j)),
            pl.BlockSpec((B_W, bn), lambda j, i: (0, j)),
            pl.BlockSpec((C_W, bn), lambda j, i: (0, j)),
        ],
        out_specs=pl.BlockSpec((bm, bn), lambda j, i: (i, j)),
        out_shape=jax.ShapeDtypeStruct((t, D_MODEL), jnp.bfloat16),
        compiler_params=_cparams(("parallel", "parallel")),
        name="merge",
    )(oa, ob, oc, proj, proj, proj, wa, wb, wc)


def _finish(y, x_ref, gp_ref, gn_ref, xo_ref, ho_ref):
    xn = x_ref[...] + _rms(y, gp_ref[...])
    xo_ref[...] = xn
    if ho_ref is not None:
        ho_ref[...] = _rms(xn, gn_ref[...]).astype(ho_ref.dtype)


def _gemm_res_kernel(nk, with_next, a_ref, w_ref, x_ref, gp_ref, gn_ref, xo_ref, *rest):
    ho_ref = rest[0] if with_next else None
    if nk == 1:
        y = jnp.dot(a_ref[...], w_ref[...], preferred_element_type=jnp.float32)
        _finish(y, x_ref, gp_ref, gn_ref, xo_ref, ho_ref)
        return
    acc = rest[-1]
    k = pl.program_id(1)

    @pl.when(k == 0)
    def _():
        acc[...] = jnp.zeros(acc.shape, jnp.float32)

    acc[...] += jnp.dot(a_ref[...], w_ref[...], preferred_element_type=jnp.float32)

    @pl.when(k == nk - 1)
    def _():
        _finish(acc[...], x_ref, gp_ref, gn_ref, xo_ref, ho_ref)


def _gemm_res(a, w, x, g_post, g_next, bk, bm=512, name="gemm_res"):
    t, kdim = a.shape
    n = w.shape[1]
    nk = kdim // bk
    with_next = g_next is not None
    if g_next is None:
        g_next = g_post
    out_shape = [jax.ShapeDtypeStruct((t, n), jnp.float32)]
    out_specs = [pl.BlockSpec((bm, n), lambda i, k: (i, 0))]
    if with_next:
        out_shape.append(jax.ShapeDtypeStruct((t, n), jnp.bfloat16))
        out_specs.append(pl.BlockSpec((bm, n), lambda i, k: (i, 0)))
    res = pl.pallas_call(
        functools.partial(_gemm_res_kernel, nk, with_next),
        grid=(t // bm, nk),
        in_specs=[
            pl.BlockSpec((bm, bk), lambda i, k: (i, k)),
            pl.BlockSpec((bk, n), lambda i, k: (k, 0)),
            pl.BlockSpec((bm, n), lambda i, k: (i, 0)),
            pl.BlockSpec((1, n), lambda i, k: (0, 0)),
            pl.BlockSpec((1, n), lambda i, k: (0, 0)),
        ],
        out_specs=out_specs,
        out_shape=out_shape,
        scratch_shapes=[pltpu.VMEM((bm, n), jnp.float32)] if nk > 1 else [],
        compiler_params=_cparams(("parallel", "arbitrary")),
        name=name,
    )(a, w, x, g_post, g_next)
    return (res[0], res[1]) if with_next else (res[0], None)


def _memkv_kernel(m_ref, g_ref, w_ref, o_ref):
    mn = _rms(m_ref[...], g_ref[...]).astype(jnp.bfloat16)
    o_ref[...] = jnp.dot(mn, w_ref[...], preferred_element_type=jnp.float32).astype(o_ref.dtype)


def _memkv(mem2, g, w):
    t, d = mem2.shape
    n = w.shape[1]
    bm = 512
    return pl.pallas_call(
        _memkv_kernel,
        grid=(t // bm,),
        in_specs=[pl.BlockSpec((bm, d), lambda i: (i, 0)), pl.BlockSpec((1, d), lambda i: (0, 0)),
                  pl.BlockSpec((d, n), lambda i: (0, 0))],
        out_specs=pl.BlockSpec((bm, n), lambda i: (i, 0)),
        out_shape=jax.ShapeDtypeStruct((t, n), jnp.bfloat16),
        compiler_params=_cparams(("parallel",)),
        name="mem_kv",
    )(mem2, g, w)


XA_BM = 512


def _xattn_kernel(h_ref, wq_ref, kv_ref, wo_ref, x_ref, gp_ref, gn_ref, xo_ref, ho_ref):
    f32 = jnp.float32
    q = jnp.dot(h_ref[...], wq_ref[...], preferred_element_type=f32).astype(jnp.bfloat16)
    scale = MEM_HEAD_DIM ** -0.5
    outs = []
    for h in range(MEM_HEADS):
        cols = slice(h * LANES, (h + 1) * LANES)
        kh = kv_ref[:, cols]
        vh = kv_ref[:, MEM_W + h * LANES:MEM_W + (h + 1) * LANES]
        s = lax.dot_general(q[:, cols], kh, _NT, preferred_element_type=f32) * scale
        m = jnp.max(s, axis=-1, keepdims=True)
        e = jnp.exp(s - m)
        l = jnp.sum(e, axis=-1, keepdims=True)
        o = jnp.dot(e.astype(jnp.bfloat16), vh, preferred_element_type=f32)
        outs.append((o / l).astype(jnp.bfloat16))
    o = jnp.concatenate(outs, axis=-1)
    y = jnp.dot(o, wo_ref[...], preferred_element_type=f32)
    _finish(y, x_ref, gp_ref, gn_ref, xo_ref, ho_ref)


def _xattn(h, wq, kv, wo, x, g_post, g_next):
    t, d = h.shape
    bm = XA_BM
    per_batch = SEQ // bm
    return pl.pallas_call(
        _xattn_kernel,
        grid=(t // bm,),
        in_specs=[
            pl.BlockSpec((bm, d), lambda i: (i, 0)),
            pl.BlockSpec((d, MEM_W), lambda i: (0, 0)),
            pl.BlockSpec((MEM_LEN, 2 * MEM_W), lambda i: (i // per_batch, 0)),
            pl.BlockSpec((MEM_W, d), lambda i: (0, 0)),
            pl.BlockSpec((bm, d), lambda i: (i, 0)),
            pl.BlockSpec((1, d), lambda i: (0, 0)),
            pl.BlockSpec((1, d), lambda i: (0, 0)),
        ],
        out_specs=[pl.BlockSpec((bm, d), lambda i: (i, 0)), pl.BlockSpec((bm, d), lambda i: (i, 0))],
        out_shape=[jax.ShapeDtypeStruct((t, d), jnp.float32), jax.ShapeDtypeStruct((t, d), jnp.bfloat16)],
        compiler_params=_cparams(("parallel",)),
        name="mem_xattn",
    )(h, wq, kv, wo, x, g_post, g_next)


def _ffn_up_kernel(h_ref, wg_ref, wv_ref, cwg_ref, cwv_ref, cbg_ref, cbv_ref, o_ref, ug, uv):
    i = pl.program_id(1)
    tiles_per_seq = SEQ // FF_BM
    hb = h_ref[...]
    for u, w_ref in ((ug, wg_ref), (uv, wv_ref)):
        @pl.when(i % tiles_per_seq == 0)
        def _():
            u[0:SUBLANES, :] = jnp.zeros((SUBLANES, FF_BN), jnp.float32)

        @pl.when(i % tiles_per_seq != 0)
        def _():
            u[0:SUBLANES, :] = u[FF_BM:FF_BM + SUBLANES, :]

        u[SUBLANES:SUBLANES + FF_BM, :] = jnp.dot(hb, w_ref[...], preferred_element_type=jnp.float32)

    def conv(u, cw_ref, cb_ref):
        acc = cb_ref[...] + u[SUBLANES - 2:SUBLANES - 2 + FF_BM, :] * cw_ref[0:1, :]
        acc = acc + u[SUBLANES - 1:SUBLANES - 1 + FF_BM, :] * cw_ref[1:2, :]
        return acc + u[SUBLANES:SUBLANES + FF_BM, :] * cw_ref[2:3, :]

    gate = conv(ug, cwg_ref, cbg_ref)
    val = conv(uv, cwv_ref, cbv_ref)
    o_ref[...] = (jax.nn.gelu(gate) * val).astype(o_ref.dtype)


def _ffn_up(h, w_up, conv_w, conv_b):
    t, d = h.shape
    nj = FF_P // FF_BN
    return pl.pallas_call(
        _ffn_up_kernel,
        grid=(nj, t // FF_BM),
        in_specs=[
            pl.BlockSpec((FF_BM, d), lambda j, i: (i, 0)),
            pl.BlockSpec((d, FF_BN), lambda j, i: (0, j)),
            pl.BlockSpec((d, FF_BN), lambda j, i: (0, nj + j)),
            pl.BlockSpec((CONV_W, FF_BN), lambda j, i: (0, j)),
            pl.BlockSpec((CONV_W, FF_BN), lambda j, i: (0, nj + j)),
            pl.BlockSpec((1, FF_BN), lambda j, i: (0, j)),
            pl.BlockSpec((1, FF_BN), lambda j, i: (0, nj + j)),
        ],
        out_specs=pl.BlockSpec((FF_BM, FF_BN), lambda j, i: (i, j)),
        out_shape=jax.ShapeDtypeStruct((t, FF_P), jnp.bfloat16),
        scratch_shapes=[pltpu.VMEM((FF_BM + SUBLANES, FF_BN), jnp.float32),
                        pltpu.VMEM((FF_BM + SUBLANES, FF_BN), jnp.float32)],
        compiler_params=_cparams(("arbitrary", "arbitrary")),
        name="ffn_up",
    )(h, w_up, w_up, conv_w, conv_w, conv_b, conv_b)


def _prep_w_in(w):
    d = w.shape[0]
    offs = np.cumsum([0, A_W, A_W, A_W, B_W, B_LATENT, IDX_HEADS * IDX_DIM, IDX_DIM, IDX_HEADS, C_W, 128, 128, 3 * D_MODEL])
    qa, ka, va, qb, ckv, qi, ki, wi, qc, kc, vc, gl = [w[:, offs[n]:offs[n + 1]] for n in range(12)]
    qc = qc.reshape(d, 2, C_GROUP, HEAD_DIM).transpose(0, 2, 1, 3).reshape(d, C_W)
    z = lambda n: jnp.zeros((d, n), w.dtype)
    cols = [qa, ka, va, qb, qi, ckv, ki, ki, wi, z(LANES - IDX_HEADS), kc, qc, vc, z(P_REST - P_VC - LANES), gl]
    out = jnp.concatenate(cols, axis=1).astype(jnp.bfloat16)
    assert out.shape[1] == IN_WP
    return out


def _pad_heads(w, axis):
    h = w.shape[0]
    zero = jnp.zeros_like(w)
    even = jnp.concatenate([w, zero], axis=axis)
    odd = jnp.concatenate([zero, w], axis=axis)
    sel = (jnp.arange(h) % 2 == 0).reshape((h, 1, 1))
    return jnp.where(sel, even, odd)


def _bias_a(rel_bias):
    r = np.arange(QB)[:, None]
    j = np.arange(A_WIN)[None, :]
    rel = A_PAD + r - j
    idx = np.clip(rel, -A_MAX_REL, A_MAX_REL) + A_MAX_REL
    diff = A_LEFT_CHUNKS + r // CHUNK - j // CHUNK
    band = (diff >= 0) & (diff <= A_LEFT_CHUNKS)
    bias = jnp.transpose(rel_bias[idx], (2, 0, 1)).astype(jnp.float32)
    return jnp.where(band[None], bias, NEG)


def _bias_c(t5_c):
    r = np.arange(QB)[:, None]
    j = np.arange(C_WIN)[None, :]
    rel = jnp.asarray(j - r - C_PAD, jnp.int32)
    diff = C_LEFT_CHUNKS + r // CHUNK - j // CHUNK
    band = (diff >= 0) & (diff <= C_LEFT_CHUNKS)
    bias = jnp.transpose(t5_c[_t5_bucket(rel)], (2, 0, 1)).astype(jnp.float32)
    return jnp.where(band[None], bias, NEG).reshape(C_Q_HEADS * QB, C_WIN)


def _bias_b(t5_b):
    r = np.arange(QB)[:, None]
    j = np.arange(KB)[None, :]
    tiles = []
    for n in range(B_NEAR + 1):
        rel = jnp.asarray(j - r - QB * n, jnp.int32) if n < B_NEAR else jnp.full((QB, KB), -SEQ, jnp.int32)
        tiles.append(jnp.transpose(t5_b[_t5_bucket(rel)], (2, 0, 1)).astype(jnp.float32).reshape(B_HEADS * QB, KB))
    return jnp.stack(tiles)


def _pad_ff(a, axis):
    g, v = jnp.split(a, 2, axis=axis)
    pad = [(0, 0)] * a.ndim
    pad[axis] = (0, FF_P - D_FF)
    return jnp.concatenate([jnp.pad(g, pad), jnp.pad(v, pad)], axis=axis)


def kernel(x, mem, t5_table, norm_gains, w_in, a_rel_bias, ckv_gain, w_uk, w_uv, sinks, w_branch, w_o,
           mem_gain, w_mq, w_mkv, w_mo, w_up, conv_w, conv_b, w_down):
    bf16 = jnp.bfloat16
    xs = x.reshape(TOKENS, D_MODEL)
    mem2 = mem.reshape(BATCH * MEM_LEN, D_MODEL)
    tri = jnp.asarray(np.triu(np.ones((KB, KB), np.float32), 1), bf16)
    bias_b = _bias_b(t5_table[:, :B_HEADS])
    bias_c = _bias_c(t5_table[:, B_HEADS:])
    gains = norm_gains.reshape(DEPTH, 6, 1, D_MODEL)

    h = _norm(xs, gains[0, 0])
    for l in range(DEPTH):
        g = gains[l]
        proj = _inproj(h, _prep_w_in(w_in[l]))
        proj3 = proj.reshape(BATCH, SEQ, IN_WP)
        pad_a = ((0, 0), (A_PAD, 0), (0, 0))
        pad_c = ((0, 0), (C_PAD, 0), (0, 0))
        oa = _attn_a(proj3, jnp.pad(proj3[:, :, P_KA:P_KA + A_W], pad_a), jnp.pad(proj3[:, :, P_VA:P_VA + A_W], pad_a),
                     _bias_a(a_rel_bias[l]))
        wuk = jnp.transpose(_pad_heads(w_uk[l], axis=2), (0, 2, 1)).astype(bf16)
        wuv = _pad_heads(w_uv[l], axis=2).astype(bf16)
        ob = _attn_b(proj3, ckv_gain[l].reshape(1, B_LATENT), wuk, wuv, bias_b, tri)
        oc = _attn_c(sinks[l], proj3, jnp.pad(proj3[:, :, P_KC:P_KC + LANES], pad_c),
                     jnp.pad(proj3[:, :, P_VC:P_VC + LANES], pad_c), bias_c)
        wa, wb, wc = jnp.split(w_branch[l], [A_W, A_W + B_W], axis=0)
        wc = wc.reshape(2, C_GROUP, HEAD_DIM, D_MODEL).transpose(1, 0, 2, 3).reshape(C_W, D_MODEL)
        merged = _merge(oa.reshape(TOKENS, A_W), ob.reshape(TOKENS, B_W), oc.reshape(TOKENS, C_W), proj,
                        wa.astype(bf16), wb.astype(bf16), wc.astype(bf16))
        xs, h = _gemm_res(merged, w_o[l].astype(bf16), xs, g[1], g[2], bk=D_MODEL, name="out_proj")
        kv = _memkv(mem2, mem_gain[l].reshape(1, D_MODEL), w_mkv[l].astype(bf16))
        xs, h = _xattn(h, w_mq[l].astype(bf16), kv, w_mo[l].astype(bf16), xs, g[3], g[4])
        hidden = _ffn_up(h, _pad_ff(w_up[l], 1).astype(bf16), _pad_ff(conv_w[l], 1), _pad_ff(conv_b[l].reshape(1, -1), 1))
        w_dn = jnp.pad(w_down[l], ((0, FF_P - D_FF), (0, 0))).astype(bf16)
        g_next = gains[l + 1, 0] if l + 1 < DEPTH else None
        xs, h = _gemm_res(hidden, w_dn, xs, g[5], g_next, bk=DOWN_BK, name="ffn_down")
    return xs.reshape(BATCH, SEQ, D_MODEL)
```

```python
import functools
import math

import numpy as np
import jax
import jax.numpy as jnp
from jax import lax
from jax.experimental import pallas as pl
from jax.experimental.pallas import tpu as pltpu

D_MODEL = 2048
BATCH = 4
SEQ = 2048
DEPTH = 2
TOKENS = BATCH * SEQ
CHUNK = 64
EPS = 1e-6
NEG = -1e30
A_HEADS = 8
A_LEFT_CHUNKS = 8
A_MAX_REL = 128
A_W = 512
B_HEADS = 8
B_W = 512
B_LATENT = 128
IDX_HEADS = 8
IDX_DIM = 64
TOPK = 256
C_Q_HEADS = 16
C_GROUP = 8
C_W = 1024
C_LEFT_CHUNKS = 2
T5_BUCKETS = 32
T5_MAX_DIST = 256
MEM_LEN = 256
MEM_HEADS = 4
MEM_HEAD_DIM = 128
MEM_W = 512
D_FF = 5504
CONV_W = 3

LANES = 128
SUBLANES = 8
HEAD_DIM = 64
QB = 128
KB = 256
VMEM_LIMIT = 56 * 1024 * 1024

P_QA, P_KA, P_VA, P_QB = 0, 512, 1024, 1536
P_QI = 2048
P_CKV = 2560
P_KI = 2688
P_WI = 2816
P_KC = 2944
P_QC = 3072
P_VC = 4096
P_REST = 4608
P_GATE = P_REST
IN_WP = P_GATE + 3 * D_MODEL
IN_BN = 1536
IN_BM = 1024
GATE_TILE0 = P_GATE // IN_BN

FF_P = 5632
FF_BN = 512
FF_BM = 1024
DOWN_BK = 1408

A_WIN = (A_LEFT_CHUNKS + 2) * CHUNK
C_WIN = (C_LEFT_CHUNKS + 2) * CHUNK
A_PAD = A_LEFT_CHUNKS * CHUNK
C_PAD = C_LEFT_CHUNKS * CHUNK
B_NEAR = 4
NKB = SEQ // KB

_NT = (((1,), (1,)), ((), ()))


def _cparams(sem):
    return pltpu.CompilerParams(dimension_semantics=sem, vmem_limit_bytes=VMEM_LIMIT)


def _t5_bucket(rel):
    half = T5_BUCKETS // 2
    max_exact = half // 2
    sign = jnp.where(rel > 0, half, 0)
    d = jnp.abs(rel)
    d_f = jnp.maximum(d, 1).astype(jnp.float32)
    large = max_exact + (jnp.log(d_f / max_exact) / math.log(T5_MAX_DIST / max_exact) * (half - max_exact)).astype(jnp.int32)
    large = jnp.minimum(large, half - 1)
    return sign + jnp.where(d < max_exact, d, large)


def _far_bucket_is_constant():
    d = np.arange(QB * B_NEAR - (KB - 1), SEQ, dtype=np.float32)
    assert d[0] > T5_MAX_DIST
    large = 8 + (np.log(d / 8) / math.log(T5_MAX_DIST / 8) * 8).astype(np.int32)
    return bool(np.all(np.minimum(large, 15) == 15))


assert _far_bucket_is_constant()


def _rms(v, g):
    return v * lax.rsqrt(jnp.mean(v * v, axis=-1, keepdims=True) + EPS) * g


def _norm_kernel(x_ref, g_ref, o_ref):
    o_ref[...] = _rms(x_ref[...], g_ref[...]).astype(o_ref.dtype)


def _norm(x, g, bm=1024):
    t, d = x.shape
    return pl.pallas_call(
        _norm_kernel,
        grid=(t // bm,),
        in_specs=[pl.BlockSpec((bm, d), lambda i: (i, 0)), pl.BlockSpec((1, d), lambda i: (0, 0))],
        out_specs=pl.BlockSpec((bm, d), lambda i: (i, 0)),
        out_shape=jax.ShapeDtypeStruct((t, d), jnp.bfloat16),
        compiler_params=_cparams(("parallel",)),
        name="rmsnorm",
    )(x, g)


def _inproj_kernel(h_ref, w_ref, o_ref):
    j = pl.program_id(0)
    acc = jnp.dot(h_ref[...], w_ref[...], preferred_element_type=jnp.float32)

    @pl.when(j < GATE_TILE0)
    def _():
        o_ref[...] = acc.astype(o_ref.dtype)

    @pl.when(j >= GATE_TILE0)
    def _():
        o_ref[...] = jax.nn.sigmoid(acc).astype(o_ref.dtype)


def _inproj(h, w):
    t, k = h.shape
    n = w.shape[1]
    return pl.pallas_call(
        _inproj_kernel,
        grid=(n // IN_BN, t // IN_BM),
        in_specs=[pl.BlockSpec((IN_BM, k), lambda j, i: (i, 0)), pl.BlockSpec((k, IN_BN), lambda j, i: (0, j))],
        out_specs=pl.BlockSpec((IN_BM, IN_BN), lambda j, i: (i, j)),
        out_shape=jax.ShapeDtypeStruct((t, n), jnp.bfloat16),
        compiler_params=_cparams(("parallel", "parallel")),
        name="in_proj",
    )(h, w)


def _lane_masks(dtype):
    lane = lax.broadcasted_iota(jnp.int32, (1, LANES), 1)
    lo = (lane < HEAD_DIM).astype(jnp.float32)
    return lo.astype(dtype), (1.0 - lo).astype(dtype)


def _fill_padded(src_ref, dst_ref, pad):
    dst_ref[0:pad, :] = jnp.zeros((pad, dst_ref.shape[1]), dst_ref.dtype)
    dst_ref[pad:pad + SEQ, :] = src_ref[...]


def _attn_a_kernel(q_ref, k_ref, v_ref, bias_ref, o_ref, kpad, vpad):
    i = pl.program_id(1)

    @pl.when(i == 0)
    def _():
        _fill_padded(k_ref, kpad, A_PAD)
        _fill_padded(v_ref, vpad, A_PAD)

    start = pl.multiple_of(i * QB, QB)
    kw = kpad[pl.ds(start, A_WIN), :]
    vw = vpad[pl.ds(start, A_WIN), :]
    mlo, mhi = _lane_masks(jnp.bfloat16)
    lane = lax.broadcasted_iota(jnp.int32, (QB, LANES), 1)
    kchunk = lax.broadcasted_iota(jnp.int32, (QB, A_WIN), 1) // CHUNK
    pad_mask = jnp.where(kchunk >= A_LEFT_CHUNKS - 2 * i, 0.0, NEG)
    scale = HEAD_DIM ** -0.5
    for p in range(A_HEADS // 2):
        cols = slice(p * LANES, (p + 1) * LANES)
        qp = q_ref[:, cols]
        kp = kw[:, cols]
        vp = vw[:, cols]
        outs = []
        for half, msk in enumerate((mlo, mhi)):
            s = lax.dot_general(qp * msk, kp, _NT, preferred_element_type=jnp.float32)
            s = s * scale + bias_ref[2 * p + half] + pad_mask
            m = jnp.max(s, axis=-1, keepdims=True)
            e = jnp.exp(s - m)
            l = jnp.sum(e, axis=-1, keepdims=True)
            o = jnp.dot(e.astype(jnp.bfloat16), vp, preferred_element_type=jnp.float32)
            outs.append(o / l)
        o_ref[:, cols] = jnp.where(lane < HEAD_DIM, outs[0], outs[1]).astype(o_ref.dtype)


def _attn_a(proj3, bias):
    nq = SEQ // QB
    return pl.pallas_call(
        _attn_a_kernel,
        grid=(BATCH, nq),
        in_specs=[
            pl.BlockSpec((None, QB, A_W), lambda b, i: (b, i, P_QA // A_W)),
            pl.BlockSpec((None, SEQ, A_W), lambda b, i: (b, 0, P_KA // A_W)),
            pl.BlockSpec((None, SEQ, A_W), lambda b, i: (b, 0, P_VA // A_W)),
            pl.BlockSpec((A_HEADS, QB, A_WIN), lambda b, i: (0, 0, 0)),
        ],
        out_specs=pl.BlockSpec((None, QB, A_W), lambda b, i: (b, i, 0)),
        out_shape=jax.ShapeDtypeStruct((BATCH, SEQ, A_W), jnp.bfloat16),
        scratch_shapes=[pltpu.VMEM((SEQ + A_PAD, A_W), jnp.bfloat16), pltpu.VMEM((SEQ + A_PAD, A_W), jnp.bfloat16)],
        compiler_params=_cparams(("arbitrary", "arbitrary")),
        name="attn_a",
    )(proj3, proj3, proj3, bias)


def _attn_c_kernel(sink_ref, q_ref, k_ref, v_ref, bias_ref, o_ref, kpad, vpad):
    i = pl.program_id(1)

    @pl.when(i == 0)
    def _():
        _fill_padded(k_ref, kpad, C_PAD)
        _fill_padded(v_ref, vpad, C_PAD)

    start = pl.multiple_of(i * QB, QB)
    kw = kpad[pl.ds(start, C_WIN), :]
    vw = vpad[pl.ds(start, C_WIN), :]
    mlo, mhi = _lane_masks(jnp.bfloat16)
    lane = lax.broadcasted_iota(jnp.int32, (QB, LANES), 1)
    kchunk = lax.broadcasted_iota(jnp.int32, (QB, C_WIN), 1) // CHUNK
    pad_mask = jnp.where(kchunk >= C_LEFT_CHUNKS - 2 * i, 0.0, NEG)
    scale = HEAD_DIM ** -0.5
    qs = [q_ref[:, p * LANES:(p + 1) * LANES] for p in range(C_GROUP)]
    halves = []
    for g, msk in enumerate((mlo, mhi)):
        qg = jnp.concatenate([q * msk for q in qs], axis=0)
        s_all = lax.dot_general(qg, kw, _NT, preferred_element_type=jnp.float32)
        ps, ls = [], []
        for p in range(C_GROUP):
            h = g * C_GROUP + p
            rows = slice(p * QB, (p + 1) * QB)
            s = s_all[rows] * scale + bias_ref[h * QB:(h + 1) * QB, :] + pad_mask
            sink = sink_ref[h]
            m = jnp.maximum(jnp.max(s, axis=-1, keepdims=True), sink)
            e = jnp.exp(s - m)
            ls.append(jnp.sum(e, axis=-1, keepdims=True) + jnp.exp(sink - m))
            ps.append(e.astype(jnp.bfloat16))
        o_all = jnp.dot(jnp.concatenate(ps, axis=0), vw, preferred_element_type=jnp.float32)
        halves.append([o_all[p * QB:(p + 1) * QB] / ls[p] for p in range(C_GROUP)])
    for p in range(C_GROUP):
        o_ref[:, p * LANES:(p + 1) * LANES] = jnp.where(lane < HEAD_DIM, halves[0][p], halves[1][p]).astype(o_ref.dtype)


def _attn_c(sinks, proj3, bias):
    nq = SEQ // QB
    return pl.pallas_call(
        _attn_c_kernel,
        grid=(BATCH, nq),
        in_specs=[
            pl.BlockSpec(memory_space=pltpu.SMEM),
            pl.BlockSpec((None, QB, C_W), lambda b, i: (b, i, P_QC // C_W)),
            pl.BlockSpec((None, SEQ, LANES), lambda b, i: (b, 0, P_KC // LANES)),
            pl.BlockSpec((None, SEQ, LANES), lambda b, i: (b, 0, P_VC // LANES)),
            pl.BlockSpec((C_Q_HEADS * QB, C_WIN), lambda b, i: (0, 0)),
        ],
        out_specs=pl.BlockSpec((None, QB, C_W), lambda b, i: (b, i, 0)),
        out_shape=jax.ShapeDtypeStruct((BATCH, SEQ, C_W), jnp.bfloat16),
        scratch_shapes=[pltpu.VMEM((SEQ + C_PAD, LANES), jnp.bfloat16), pltpu.VMEM((SEQ + C_PAD, LANES), jnp.bfloat16)],
        compiler_params=_cparams(("arbitrary", "arbitrary")),
        name="attn_c",
    )(sinks, proj3, proj3, proj3, bias)


def _sort_key(x):
    bits = lax.bitcast_convert_type(x + 0.0, jnp.int32)
    return bits ^ ((bits >> 31) & jnp.int32(0x7FFFFFFF))


def _attn_b_kernel(qb_ref, qi_ref, wi_ref, ki_ref, ckv_ref, gain_ref, wuk_ref, wuv_ref, bias_ref, tri_ref,
                   o_ref, ckvn, ql, qi_all, wib, sk, m_s, l_s, acc_s, p_s):
    i = pl.program_id(1)
    nkb = i // 2 + 1
    mlo, mhi = _lane_masks(jnp.bfloat16)

    @pl.when(i == 0)
    def _():
        c = ckv_ref[...].astype(jnp.float32)
        ckvn[...] = _rms(c, gain_ref[...]).astype(ckvn.dtype)

    wi = wi_ref[...].astype(jnp.float32) * (IDX_HEADS ** -0.5 * IDX_DIM ** -0.5)
    for h in range(B_HEADS):
        rows = slice(h * QB, (h + 1) * QB)
        cols = slice((h // 2) * LANES, (h // 2 + 1) * LANES)
        qlat = jnp.dot(qb_ref[:, cols], wuk_ref[h], preferred_element_type=jnp.float32)
        ql[rows, :] = (qlat * HEAD_DIM ** -0.5).astype(ql.dtype)
        qi_all[rows, :] = qi_ref[:, cols] * (mhi if h % 2 else mlo)
        wib[h] = jnp.broadcast_to(wi[:, h:h + 1], (QB, KB))

    qpos_chunk = (i * QB + lax.broadcasted_iota(jnp.int32, (QB, KB), 0)) // CHUNK
    kloc = lax.broadcasted_iota(jnp.int32, (QB, KB), 1)

    def admissible(kb):
        return (kb * KB + kloc) // CHUNK <= qpos_chunk

    def score_block(kb, carry):
        kblk = ki_ref[pl.ds(pl.multiple_of(kb * KB, KB), KB), :]
        dots = lax.dot_general(qi_all[...], kblk, _NT, preferred_element_type=jnp.float32)
        score = jnp.zeros((QB, KB), jnp.float32)
        for h in range(IDX_HEADS):
            score = score + jnp.maximum(dots[h * QB:(h + 1) * QB], 0.0) * wib[h]
        score = jnp.where(admissible(kb), score, NEG)
        sk[kb] = _sort_key(score)
        return carry

    lax.fori_loop(0, nkb, score_block, 0)

    neg_key = _sort_key(jnp.full((1, 1), NEG, jnp.float32))
    n_rest = ((NKB - nkb) * KB).astype(jnp.float32)

    def count(pred_fn, v):
        def body(kb, acc):
            hit = jnp.where(pred_fn(sk[kb], v), 1.0, 0.0)
            return acc + hit[:, :LANES] + hit[:, LANES:]

        acc = lax.fori_loop(0, nkb, body, jnp.zeros((QB, LANES), jnp.float32))
        return jnp.sum(acc, axis=-1, keepdims=True) + jnp.where(pred_fn(neg_key, v), n_rest, 0.0)

    ge = lambda a, v: a >= v
    int_min = jnp.int32(-2 ** 31)
    v0 = jnp.where(count(ge, jnp.zeros((QB, 1), jnp.int32)) >= TOPK, jnp.int32(0), int_min)

    def bit_step(t, v):
        cand = v | (jnp.int32(1) << (30 - t))
        return jnp.where(count(ge, cand) >= TOPK, cand, v)

    thr = lax.fori_loop(0, 31, bit_step, v0)
    need = TOPK - count(lambda a, v: a > v, thr)

    m_s[...] = jnp.full(m_s.shape, 4 * NEG, jnp.float32)
    l_s[...] = jnp.zeros(l_s.shape, jnp.float32)
    acc_s[...] = jnp.zeros(acc_s.shape, jnp.float32)

    def attend(kb, eq_seen):
        keys = sk[kb]
        eq = keys == thr
        eq_f = jnp.where(eq, 1.0, 0.0)
        before = jnp.dot(eq_f.astype(jnp.bfloat16), tri_ref[...], preferred_element_type=jnp.float32)
        take_eq = jnp.where(eq, jnp.where(eq_seen + before < need, 1.0, 0.0), 0.0)
        take = jnp.where(keys > thr, 1.0, take_eq)
        mask_add = jnp.where(admissible(kb), jnp.where(take > 0.0, 0.0, NEG), NEG)
        cblk = ckvn[pl.ds(pl.multiple_of(kb * KB, KB), KB), :]
        dots = lax.dot_general(ql[...], cblk, _NT, preferred_element_type=jnp.float32)
        tile = jnp.minimum(i - 2 * kb, B_NEAR)
        for h in range(B_HEADS):
            rows = slice(h * QB, (h + 1) * QB)
            s = dots[rows] + bias_ref[tile, rows, :] + mask_add
            m_old = m_s[rows, :]
            m_new = jnp.maximum(m_old, jnp.max(s, axis=-1, keepdims=True))
            alpha = jnp.exp(m_old - m_new)
            e = jnp.exp(s - jnp.tile(m_new, (1, KB // LANES)))
            l_s[rows, :] = alpha * l_s[rows, :] + jnp.sum(e, axis=-1, keepdims=True)
            m_s[rows, :] = m_new
            acc_s[rows, :] = acc_s[rows, :] * alpha
            p_s[rows, :] = e.astype(p_s.dtype)
        acc_s[...] += jnp.dot(p_s[...], cblk, preferred_element_type=jnp.float32)
        return eq_seen + jnp.sum(eq_f, axis=-1, keepdims=True)

    lax.fori_loop(0, nkb, attend, jnp.zeros((QB, 1), jnp.float32))

    for p in range(B_HEADS // 2):
        out = jnp.zeros((QB, LANES), jnp.float32)
        for h in (2 * p, 2 * p + 1):
            rows = slice(h * QB, (h + 1) * QB)
            o_lat = (acc_s[rows, :] / l_s[rows, :]).astype(jnp.bfloat16)
            out = out + jnp.dot(o_lat, wuv_ref[h], preferred_element_type=jnp.float32)
        o_ref[:, p * LANES:(p + 1) * LANES] = out.astype(o_ref.dtype)


def _attn_b(proj3, gain, wuk, wuv, bias, tri):
    nq = SEQ // QB
    hq = B_HEADS * QB
    return pl.pallas_call(
        _attn_b_kernel,
        grid=(BATCH, nq),
        in_specs=[
            pl.BlockSpec((None, QB, B_W), lambda b, i: (b, i, P_QB // B_W)),
            pl.BlockSpec((None, QB, IDX_HEADS * IDX_DIM), lambda b, i: (b, i, P_QI // (IDX_HEADS * IDX_DIM))),
            pl.BlockSpec((None, QB, LANES), lambda b, i: (b, i, P_WI // LANES)),
            pl.BlockSpec((None, SEQ, LANES), lambda b, i: (b, 0, P_KI // LANES)),
            pl.BlockSpec((None, SEQ, LANES), lambda b, i: (b, 0, P_CKV // LANES)),
            pl.BlockSpec((1, B_LATENT), lambda b, i: (0, 0)),
            pl.BlockSpec((B_HEADS, LANES, B_LATENT), lambda b, i: (0, 0, 0)),
            pl.BlockSpec((B_HEADS, B_LATENT, LANES), lambda b, i: (0, 0, 0)),
            pl.BlockSpec((B_NEAR + 1, hq, KB), lambda b, i: (0, 0, 0)),
            pl.BlockSpec((KB, KB), lambda b, i: (0, 0)),
        ],
        out_specs=pl.BlockSpec((None, QB, B_W), lambda b, i: (b, i, 0)),
        out_shape=jax.ShapeDtypeStruct((BATCH, SEQ, B_W), jnp.bfloat16),
        scratch_shapes=[
            pltpu.VMEM((SEQ, B_LATENT), jnp.bfloat16),
            pltpu.VMEM((hq, B_LATENT), jnp.bfloat16),
            pltpu.VMEM((hq, LANES), jnp.bfloat16),
            pltpu.VMEM((IDX_HEADS, QB, KB), jnp.float32),
            pltpu.VMEM((NKB, QB, KB), jnp.int32),
            pltpu.VMEM((hq, LANES), jnp.float32),
            pltpu.VMEM((hq, LANES), jnp.float32),
            pltpu.VMEM((hq, B_LATENT), jnp.float32),
            pltpu.VMEM((hq, KB), jnp.bfloat16),
        ],
        compiler_params=_cparams(("parallel", "arbitrary")),
        name="attn_b",
    )(proj3, proj3, proj3, proj3, proj3, gain, wuk, wuv, bias, tri)


MERGE_BN = 512
MERGE_BM = 1024


def _merge_kernel(oa_ref, ob_ref, oc_ref, ga_ref, gb_ref, gc_ref, wa_ref, wb_ref, wc_ref, o_ref):
    f32 = jnp.float32
    m = ga_ref[...].astype(f32) * jnp.dot(oa_ref[...], wa_ref[...], preferred_element_type=f32)
    m = m + gb_ref[...].astype(f32) * jnp.dot(ob_ref[...], wb_ref[...], preferred_element_type=f32)
    m = m + gc_ref[...].astype(f32) * jnp.dot(oc_ref[...], wc_ref[...], preferred_element_type=f32)
    o_ref[...] = m.astype(o_ref.dtype)


def _merge(oa, ob, oc, proj, wa, wb, wc):
    t = oa.shape[0]
    bn, bm = MERGE_BN, MERGE_BM
    g0 = P_GATE // bn
    gstep = D_MODEL // bn
    return pl.pallas_call(
        _merge_kernel,
        grid=(D_MODEL // bn, t // bm),
        in_specs=[
            pl.BlockSpec((bm, A_W), lambda j, i: (i, 0)),
            pl.BlockSpec((bm, B_W), lambda j, i: (i, 0)),
            pl.BlockSpec((bm, C_W), lambda j, i: (i, 0)),
            pl.BlockSpec((bm, bn), lambda j, i: (i, g0 + j)),
            pl.BlockSpec((bm, bn), lambda j, i: (i, g0 + gstep + j)),
            pl.BlockSpec((bm, bn), lambda j, i: (i, g0 + 2 * gstep + j)),
            pl.BlockSpec((A_W, bn), lambda j, i: (0, j)),
            pl.BlockSpec((B_W, bn), lambda j, i: (0, j)),
            pl.BlockSpec((C_W, bn), lambda j, i: (0, j)),
        ],
        out_specs=pl.BlockSpec((bm, bn), lambda j, i: (i, j)),
        out_shape=jax.ShapeDtypeStruct((t, D_MODEL), jnp.bfloat16),
        compiler_params=_cparams(("parallel", "parallel")),
        name="merge",
    )(oa, ob, oc, proj, proj, proj, wa, wb, wc)


def _finish(y, x_ref, gp_ref, gn_ref, xo_ref, ho_ref):
    xn = x_ref[...] + _rms(y, gp_ref[...])
    xo_ref[...] = xn
    if ho_ref is not None:
        ho_ref[...] = _rms(xn, gn_ref[...]).astype(ho_ref.dtype)


def _gemm_res_kernel(nk, with_next, a_ref, w_ref, x_ref, gp_ref, gn_ref, xo_ref, *rest):
    ho_ref = rest[0] if with_next else None
    if nk == 1:
        y = jnp.dot(a_ref[...], w_ref[...], preferred_element_type=jnp.float32)
        _finish(y, x_ref, gp_ref, gn_ref, xo_ref, ho_ref)
        return
    acc = rest[-1]
    k = pl.program_id(1)

    @pl.when(k == 0)
    def _():
        acc[...] = jnp.zeros(acc.shape, jnp.float32)

    acc[...] += jnp.dot(a_ref[...], w_ref[...], preferred_element_type=jnp.float32)

    @pl.when(k == nk - 1)
    def _():
        _finish(acc[...], x_ref, gp_ref, gn_ref, xo_ref, ho_ref)


def _gemm_res(a, w, x, g_post, g_next, bk, bm=512, name="gemm_res"):
    t, kdim = a.shape
    n = w.shape[1]
    nk = kdim // bk
    with_next = g_next is not None
    if g_next is None:
        g_next = g_post
    out_shape = [jax.ShapeDtypeStruct((t, n), jnp.float32)]
    out_specs = [pl.BlockSpec((bm, n), lambda i, k: (i, 0))]
    if with_next:
        out_shape.append(jax.ShapeDtypeStruct((t, n), jnp.bfloat16))
        out_specs.append(pl.BlockSpec((bm, n), lambda i, k: (i, 0)))
    res = pl.pallas_call(
        functools.partial(_gemm_res_kernel, nk, with_next),
        grid=(t // bm, nk),
        in_specs=[
            pl.BlockSpec((bm, bk), lambda i, k: (i, k)),
            pl.BlockSpec((bk, n), lambda i, k: (k, 0)),
            pl.BlockSpec((bm, n), lambda i, k: (i, 0)),
            pl.BlockSpec((1, n), lambda i, k: (0, 0)),
            pl.BlockSpec((1, n), lambda i, k: (0, 0)),
        ],
        out_specs=out_specs,
        out_shape=out_shape,
        scratch_shapes=[pltpu.VMEM((bm, n), jnp.float32)] if nk > 1 else [],
        compiler_params=_cparams(("parallel", "arbitrary")),
        name=name,
    )(a, w, x, g_post, g_next)
    return (res[0], res[1]) if with_next else (res[0], None)


def _memkv_kernel(m_ref, g_ref, w_ref, o_ref):
    mn = _rms(m_ref[...], g_ref[...]).astype(jnp.bfloat16)
    o_ref[...] = jnp.dot(mn, w_ref[...], preferred_element_type=jnp.float32).astype(o_ref.dtype)


def _memkv(mem2, g, w):
    t, d = mem2.shape
    n = w.shape[1]
    bm = 512
    return pl.pallas_call(
        _memkv_kernel,
        grid=(t // bm,),
        in_specs=[pl.BlockSpec((bm, d), lambda i: (i, 0)), pl.BlockSpec((1, d), lambda i: (0, 0)),
                  pl.BlockSpec((d, n), lambda i: (0, 0))],
        out_specs=pl.BlockSpec((bm, n), lambda i: (i, 0)),
        out_shape=jax.ShapeDtypeStruct((t, n), jnp.bfloat16),
        compiler_params=_cparams(("parallel",)),
        name="mem_kv",
    )(mem2, g, w)


XA_BM = 512


def _xattn_kernel(h_ref, wq_ref, kv_ref, wo_ref, x_ref, gp_ref, gn_ref, xo_ref, ho_ref):
    f32 = jnp.float32
    q = jnp.dot(h_ref[...], wq_ref[...], preferred_element_type=f32).astype(jnp.bfloat16)
    scale = MEM_HEAD_DIM ** -0.5
    outs = []
    for h in range(MEM_HEADS):
        cols = slice(h * LANES, (h + 1) * LANES)
        kh = kv_ref[:, cols]
        vh = kv_ref[:, MEM_W + h * LANES:MEM_W + (h + 1) * LANES]
        s = lax.dot_general(q[:, cols], kh, _NT, preferred_element_type=f32) * scale
        m = jnp.max(s, axis=-1, keepdims=True)
        e = jnp.exp(s - m)
        l = jnp.sum(e, axis=-1, keepdims=True)
        o = jnp.dot(e.astype(jnp.bfloat16), vh, preferred_element_type=f32)
        outs.append((o / l).astype(jnp.bfloat16))
    o = jnp.concatenate(outs, axis=-1)
    y = jnp.dot(o, wo_ref[...], preferred_element_type=f32)
    _finish(y, x_ref, gp_ref, gn_ref, xo_ref, ho_ref)


def _xattn(h, wq, kv, wo, x, g_post, g_next):
    t, d = h.shape
    bm = XA_BM
    per_batch = SEQ // bm
    return pl.pallas_call(
        _xattn_kernel,
        grid=(t // bm,),
        in_specs=[
            pl.BlockSpec((bm, d), lambda i: (i, 0)),
            pl.BlockSpec((d, MEM_W), lambda i: (0, 0)),
            pl.BlockSpec((MEM_LEN, 2 * MEM_W), lambda i: (i // per_batch, 0)),
            pl.BlockSpec((MEM_W, d), lambda i: (0, 0)),
            pl.BlockSpec((bm, d), lambda i: (i, 0)),
            pl.BlockSpec((1, d), lambda i: (0, 0)),
            pl.BlockSpec((1, d), lambda i: (0, 0)),
        ],
        out_specs=[pl.BlockSpec((bm, d), lambda i: (i, 0)), pl.BlockSpec((bm, d), lambda i: (i, 0))],
        out_shape=[jax.ShapeDtypeStruct((t, d), jnp.float32), jax.ShapeDtypeStruct((t, d), jnp.bfloat16)],
        compiler_params=_cparams(("parallel",)),
        name="mem_xattn",
    )(h, wq, kv, wo, x, g_post, g_next)


def _ffn_up_kernel(h_ref, wg_ref, wv_ref, cwg_ref, cwv_ref, cbg_ref, cbv_ref, o_ref, ug, uv):
    i = pl.program_id(1)
    tiles_per_seq = SEQ // FF_BM
    hb = h_ref[...]
    for u, w_ref in ((ug, wg_ref), (uv, wv_ref)):
        @pl.when(i % tiles_per_seq == 0)
        def _():
            u[0:SUBLANES, :] = jnp.zeros((SUBLANES, FF_BN), jnp.float32)

        @pl.when(i % tiles_per_seq != 0)
        def _():
            u[0:SUBLANES, :] = u[FF_BM:FF_BM + SUBLANES, :]

        u[SUBLANES:SUBLANES + FF_BM, :] = jnp.dot(hb, w_ref[...], preferred_element_type=jnp.float32)

    def conv(u, cw_ref, cb_ref):
        acc = cb_ref[...] + u[SUBLANES - 2:SUBLANES - 2 + FF_BM, :] * cw_ref[0:1, :]
        acc = acc + u[SUBLANES - 1:SUBLANES - 1 + FF_BM, :] * cw_ref[1:2, :]
        return acc + u[SUBLANES:SUBLANES + FF_BM, :] * cw_ref[2:3, :]

    gate = conv(ug, cwg_ref, cbg_ref)
    val = conv(uv, cwv_ref, cbv_ref)
    o_ref[...] = (jax.nn.gelu(gate) * val).astype(o_ref.dtype)


def _ffn_up(h, w_up, conv_w, conv_b):
    t, d = h.shape
    nj = FF_P // FF_BN
    return pl.pallas_call(
        _ffn_up_kernel,
        grid=(nj, t // FF_BM),
        in_specs=[
            pl.BlockSpec((FF_BM, d), lambda j, i: (i, 0)),
            pl.BlockSpec((d, FF_BN), lambda j, i: (0, j)),
            pl.BlockSpec((d, FF_BN), lambda j, i: (0, nj + j)),
            pl.BlockSpec((CONV_W, FF_BN), lambda j, i: (0, j)),
            pl.BlockSpec((CONV_W, FF_BN), lambda j, i: (0, nj + j)),
            pl.BlockSpec((1, FF_BN), lambda j, i: (0, j)),
            pl.BlockSpec((1, FF_BN), lambda j, i: (0, nj + j)),
        ],
        out_specs=pl.BlockSpec((FF_BM, FF_BN), lambda j, i: (i, j)),
        out_shape=jax.ShapeDtypeStruct((t, FF_P), jnp.bfloat16),
        scratch_shapes=[pltpu.VMEM((FF_BM + SUBLANES, FF_BN), jnp.float32),
                        pltpu.VMEM((FF_BM + SUBLANES, FF_BN), jnp.float32)],
        compiler_params=_cparams(("arbitrary", "arbitrary")),
        name="ffn_up",
    )(h, w_up, w_up, conv_w, conv_w, conv_b, conv_b)


def _prep_w_in(w):
    d = w.shape[0]
    offs = np.cumsum([0, A_W, A_W, A_W, B_W, B_LATENT, IDX_HEADS * IDX_DIM, IDX_DIM, IDX_HEADS, C_W, 128, 128, 3 * D_MODEL])
    qa, ka, va, qb, ckv, qi, ki, wi, qc, kc, vc, gl = [w[:, offs[n]:offs[n + 1]].astype(jnp.bfloat16) for n in range(12)]
    qc = qc.reshape(d, 2, C_GROUP, HEAD_DIM).transpose(0, 2, 1, 3).reshape(d, C_W)
    z = lambda n: jnp.zeros((d, n), jnp.bfloat16)
    cols = [qa, ka, va, qb, qi, ckv, ki, ki, wi, z(LANES - IDX_HEADS), kc, qc, vc, z(P_REST - P_VC - LANES), gl]
    out = jnp.concatenate(cols, axis=1)
    assert out.shape[1] == IN_WP
    return out


def _pad_heads(w, axis):
    h = w.shape[0]
    zero = jnp.zeros_like(w)
    even = jnp.concatenate([w, zero], axis=axis)
    odd = jnp.concatenate([zero, w], axis=axis)
    sel = (jnp.arange(h) % 2 == 0).reshape((h, 1, 1))
    return jnp.where(sel, even, odd)


def _toeplitz(fn, rows, cols):
    ks = np.concatenate([np.arange(0, cols), np.arange(-(rows - 1), 0)])
    w = fn(ks)
    h, period = w.shape
    x = jnp.tile(w, (1, rows))[:, :rows * (period - 1)].reshape(h, rows, period - 1)
    return x[:, :, :cols].astype(jnp.float32)


def _band(rows, cols, left):
    diff = left + np.arange(rows)[:, None] // CHUNK - np.arange(cols)[None, :] // CHUNK
    return (diff >= 0) & (diff <= left)


def _bias_a(rel_bias):
    fn = lambda ks: rel_bias[np.clip(A_PAD - ks, -A_MAX_REL, A_MAX_REL) + A_MAX_REL].T
    return jnp.where(_band(QB, A_WIN, A_LEFT_CHUNKS)[None], _toeplitz(fn, QB, A_WIN), NEG)


def _bias_c(t5_c):
    fn = lambda ks: t5_c[_t5_bucket(jnp.asarray(ks - C_PAD, jnp.int32))].T
    bias = jnp.where(_band(QB, C_WIN, C_LEFT_CHUNKS)[None], _toeplitz(fn, QB, C_WIN), NEG)
    return bias.reshape(C_Q_HEADS * QB, C_WIN)


def _bias_b(t5_b):
    tiles = []
    for n in range(B_NEAR + 1):
        off = QB * n if n < B_NEAR else SEQ
        fn = lambda ks, off=off: t5_b[_t5_bucket(jnp.asarray(ks - off, jnp.int32))].T
        tiles.append(_toeplitz(fn, QB, KB).reshape(B_HEADS * QB, KB))
    return jnp.stack(tiles)


def _pad_ff(a, dtype):
    z = jnp.zeros((a.shape[0], FF_P - D_FF), dtype)
    return jnp.concatenate([a[:, :D_FF].astype(dtype), z, a[:, D_FF:].astype(dtype), z], axis=1)


def kernel(x, mem, t5_table, norm_gains, w_in, a_rel_bias, ckv_gain, w_uk, w_uv, sinks, w_branch, w_o,
           mem_gain, w_mq, w_mkv, w_mo, w_up, conv_w, conv_b, w_down):
    bf16 = jnp.bfloat16
    xs = x.reshape(TOKENS, D_MODEL)
    mem2 = mem.reshape(BATCH * MEM_LEN, D_MODEL)
    tri = jnp.asarray(np.triu(np.ones((KB, KB), np.float32), 1), bf16)
    bias_b = _bias_b(t5_table[:, :B_HEADS])
    bias_c = _bias_c(t5_table[:, B_HEADS:])
    gains = norm_gains.reshape(DEPTH, 6, 1, D_MODEL)

    h = _norm(xs, gains[0, 0])
    for l in range(DEPTH):
        g = gains[l]
        proj = _inproj(h, _prep_w_in(w_in[l]))
        proj3 = proj.reshape(BATCH, SEQ, IN_WP)
        oa = _attn_a(proj3, _bias_a(a_rel_bias[l]))
        wuk = jnp.transpose(_pad_heads(w_uk[l], axis=2), (0, 2, 1)).astype(bf16)
        wuv = _pad_heads(w_uv[l], axis=2).astype(bf16)
        ob = _attn_b(proj3, ckv_gain[l].reshape(1, B_LATENT), wuk, wuv, bias_b, tri)
        oc = _attn_c(sinks[l], proj3, bias_c)
        wa, wb, wc = jnp.split(w_branch[l], [A_W, A_W + B_W], axis=0)
        wc = wc.reshape(2, C_GROUP, HEAD_DIM, D_MODEL).transpose(1, 0, 2, 3).reshape(C_W, D_MODEL)
        merged = _merge(oa.reshape(TOKENS, A_W), ob.reshape(TOKENS, B_W), oc.reshape(TOKENS, C_W), proj,
                        wa.astype(bf16), wb.astype(bf16), wc.astype(bf16))
        xs, h = _gemm_res(merged, w_o[l].astype(bf16), xs, g[1], g[2], bk=D_MODEL, name="out_proj")
        kv = _memkv(mem2, mem_gain[l].reshape(1, D_MODEL), w_mkv[l].astype(bf16))
        xs, h = _xattn(h, w_mq[l].astype(bf16), kv, w_mo[l].astype(bf16), xs, g[3], g[4])
        hidden = _ffn_up(h, _pad_ff(w_up[l], bf16), _pad_ff(conv_w[l], jnp.float32),
                         _pad_ff(conv_b[l].reshape(1, -1), jnp.float32))
        w_dn = jnp.pad(w_down[l].astype(bf16), ((0, FF_P - D_FF), (0, 0)))
        g_next = gains[l + 1, 0] if l + 1 < DEPTH else None
        xs, h = _gemm_res(hidden, w_dn, xs, g[5], g_next, bk=DOWN_BK, name="ffn_down")
    return xs.reshape(BATCH, SEQ, D_MODEL)
```

```python
import functools
import math

import numpy as np
import jax
import jax.numpy as jnp
from jax import lax
from jax.experimental import pallas as pl
from jax.experimental.pallas import tpu as pltpu

D_MODEL = 2048
BATCH = 4
SEQ = 2048
DEPTH = 2
TOKENS = BATCH * SEQ
CHUNK = 64
EPS = 1e-6
NEG = -1e30
A_HEADS = 8
A_LEFT_CHUNKS = 8
A_MAX_REL = 128
A_W = 512
B_HEADS = 8
B_W = 512
B_LATENT = 128
IDX_HEADS = 8
IDX_DIM = 64
TOPK = 256
C_Q_HEADS = 16
C_GROUP = 8
C_W = 1024
C_LEFT_CHUNKS = 2
T5_BUCKETS = 32
T5_MAX_DIST = 256
MEM_LEN = 256
MEM_HEADS = 4
MEM_HEAD_DIM = 128
MEM_W = 512
D_FF = 5504
CONV_W = 3

LANES = 128
SUBLANES = 8
HEAD_DIM = 64
QB = 128
KB = 256
VMEM_LIMIT = 56 * 1024 * 1024

O_QA, O_KA, O_VA, O_QB = 0, 512, 1024, 1536
O_CKV = 2048
O_QI = 2176
O_KI = 2688
O_WI = 2752
O_QC = 2760
O_GL = 4040
IN_W = O_GL + 3 * D_MODEL
IN_A = 3072
X_COL0 = 2048
C_COLS = O_GL - O_QC
PROJ_BM = 1024

FF_P = 5632
FF_BN = 512
FF_BM = 1024
FF_CH = 256
DOWN_BK = 1408

A_WIN = (A_LEFT_CHUNKS + 2) * CHUNK
C_WIN = (C_LEFT_CHUNKS + 2) * CHUNK
A_PAD = A_LEFT_CHUNKS * CHUNK
C_PAD = C_LEFT_CHUNKS * CHUNK
B_NEAR = 4
NKB = SEQ // KB
HQ = B_HEADS * QB

_NT = (((1,), (1,)), ((), ()))
_TN = (((0,), (0,)), ((), ()))


def _cparams(sem):
    return pltpu.CompilerParams(dimension_semantics=sem, vmem_limit_bytes=VMEM_LIMIT)


def _t5_bucket(rel):
    half = T5_BUCKETS // 2
    max_exact = half // 2
    sign = jnp.where(rel > 0, half, 0)
    d = jnp.abs(rel)
    d_f = jnp.maximum(d, 1).astype(jnp.float32)
    large = max_exact + (jnp.log(d_f / max_exact) / math.log(T5_MAX_DIST / max_exact) * (half - max_exact)).astype(jnp.int32)
    large = jnp.minimum(large, half - 1)
    return sign + jnp.where(d < max_exact, d, large)


def _far_bucket_is_constant():
    d = np.arange(QB * B_NEAR - (KB - 1), SEQ, dtype=np.float32)
    assert d[0] > T5_MAX_DIST
    large = 8 + (np.log(d / 8) / math.log(T5_MAX_DIST / 8) * 8).astype(np.int32)
    return bool(np.all(np.minimum(large, 15) == 15))


assert _far_bucket_is_constant()


def _rms(v, g):
    return v * lax.rsqrt(jnp.mean(v * v, axis=-1, keepdims=True) + EPS) * g


def _norm_kernel(x_ref, g_ref, o_ref):
    o_ref[...] = _rms(x_ref[...], g_ref[...]).astype(o_ref.dtype)


def _norm(x, g, bm=1024):
    t, d = x.shape
    return pl.pallas_call(
        _norm_kernel,
        grid=(t // bm,),
        in_specs=[pl.BlockSpec((bm, d), lambda i: (i, 0)), pl.BlockSpec((1, d), lambda i: (0, 0))],
        out_specs=pl.BlockSpec((bm, d), lambda i: (i, 0)),
        out_shape=jax.ShapeDtypeStruct((t, d), jnp.bfloat16),
        compiler_params=_cparams(("parallel",)),
        name="rmsnorm",
    )(x, g)


def _proj_kernel(gate, cast, h_ref, w_ref, o_ref, *scratch):
    if cast:
        wb = scratch[0]

        @pl.when(pl.program_id(1) == 0)
        def _():
            wb[...] = w_ref[...].astype(wb.dtype)

        w = wb[...]
    else:
        w = w_ref[...]
    acc = jnp.dot(h_ref[...], w, preferred_element_type=jnp.float32)
    o_ref[...] = (jax.nn.sigmoid(acc) if gate else acc).astype(o_ref.dtype)


def _proj(h, w, w_spec, n, bn, gate, cast, name):
    t, k = h.shape
    return pl.pallas_call(
        functools.partial(_proj_kernel, gate, cast),
        grid=(n // bn, t // PROJ_BM),
        in_specs=[pl.BlockSpec((PROJ_BM, k), lambda j, i: (i, 0)), w_spec],
        out_specs=pl.BlockSpec((PROJ_BM, bn), lambda j, i: (i, j)),
        out_shape=jax.ShapeDtypeStruct((t, n), jnp.bfloat16),
        scratch_shapes=[pltpu.VMEM((k, bn), jnp.bfloat16)] if cast else [],
        compiler_params=_cparams(("arbitrary", "arbitrary")),
        name=name,
    )(h, w)


def _lane_masks(dtype):
    lane = lax.broadcasted_iota(jnp.int32, (1, LANES), 1)
    lo = (lane < HEAD_DIM).astype(jnp.float32)
    return lo.astype(dtype), (1.0 - lo).astype(dtype)


def _swap_halves(x):
    return pltpu.roll(x.astype(jnp.float32), HEAD_DIM, 1).astype(x.dtype)


def _fill_padded(src, dst_ref, pad):
    dst_ref[0:pad, :] = jnp.zeros((pad, dst_ref.shape[1]), dst_ref.dtype)
    dst_ref[pad:pad + SEQ, :] = src


def _attn_a_kernel(q_ref, k_ref, v_ref, bias_ref, o_ref, kpad, vpad):
    i = pl.program_id(1)

    @pl.when(i == 0)
    def _():
        _fill_padded(k_ref[...], kpad, A_PAD)
        _fill_padded(v_ref[...], vpad, A_PAD)

    start = pl.multiple_of(i * QB, QB)
    kw = kpad[pl.ds(start, A_WIN), :]
    vw = vpad[pl.ds(start, A_WIN), :]
    mlo, mhi = _lane_masks(jnp.bfloat16)
    lane = lax.broadcasted_iota(jnp.int32, (QB, LANES), 1)
    kchunk = lax.broadcasted_iota(jnp.int32, (QB, A_WIN), 1) // CHUNK
    pad_mask = jnp.where(kchunk >= A_LEFT_CHUNKS - 2 * i, 0.0, NEG)
    scale = HEAD_DIM ** -0.5
    for p in range(A_HEADS // 2):
        cols = slice(p * LANES, (p + 1) * LANES)
        qp = q_ref[:, cols]
        kp = kw[:, cols]
        vp = vw[:, cols]
        outs = []
        for half, msk in enumerate((mlo, mhi)):
            s = lax.dot_general(qp * msk, kp, _NT, preferred_element_type=jnp.float32)
            s = s * scale + bias_ref[2 * p + half] + pad_mask
            m = jnp.max(s, axis=-1, keepdims=True)
            e = jnp.exp(s - m)
            l = jnp.sum(e, axis=-1, keepdims=True)
            o = jnp.dot(e.astype(jnp.bfloat16), vp, preferred_element_type=jnp.float32)
            outs.append(o / l)
        o_ref[:, cols] = jnp.where(lane < HEAD_DIM, outs[0], outs[1]).astype(o_ref.dtype)


def _attn_a(proj3, bias):
    nq = SEQ // QB
    return pl.pallas_call(
        _attn_a_kernel,
        grid=(BATCH, nq),
        in_specs=[
            pl.BlockSpec((None, QB, A_W), lambda b, i: (b, i, O_QA // A_W)),
            pl.BlockSpec((None, SEQ, A_W), lambda b, i: (b, 0, O_KA // A_W)),
            pl.BlockSpec((None, SEQ, A_W), lambda b, i: (b, 0, O_VA // A_W)),
            pl.BlockSpec((A_HEADS, QB, A_WIN), lambda b, i: (0, 0, 0)),
        ],
        out_specs=pl.BlockSpec((None, QB, A_W), lambda b, i: (b, i, 0)),
        out_shape=jax.ShapeDtypeStruct((BATCH, SEQ, A_W), jnp.bfloat16),
        scratch_shapes=[pltpu.VMEM((SEQ + A_PAD, A_W), jnp.bfloat16), pltpu.VMEM((SEQ + A_PAD, A_W), jnp.bfloat16)],
        compiler_params=_cparams(("arbitrary", "arbitrary")),
        name="attn_a",
    )(proj3, proj3, proj3, bias)


def _attn_c_kernel(sink_ref, q_ref, k_ref, v_ref, bias_ref, o_ref, kpad, kswp, vpad, vswp):
    i = pl.program_id(1)

    @pl.when(i == 0)
    def _():
        k = k_ref[...]
        v = v_ref[...]
        _fill_padded(k, kpad, C_PAD)
        _fill_padded(_swap_halves(k), kswp, C_PAD)
        _fill_padded(v, vpad, C_PAD)
        _fill_padded(_swap_halves(v), vswp, C_PAD)

    start = pl.multiple_of(i * QB, QB)
    mlo, mhi = _lane_masks(jnp.bfloat16)
    lane = lax.broadcasted_iota(jnp.int32, (QB, LANES), 1)
    kchunk = lax.broadcasted_iota(jnp.int32, (QB, C_WIN), 1) // CHUNK
    pad_mask = jnp.where(kchunk >= C_LEFT_CHUNKS - 2 * i, 0.0, NEG)
    scale = HEAD_DIM ** -0.5
    qs = [q_ref[:, p * LANES:(p + 1) * LANES] for p in range(C_GROUP)]
    npair = C_GROUP // 2
    stacks = []
    for straight in (True, False):
        kref, vref = (kpad, vpad) if straight else (kswp, vswp)
        kw = kref[pl.ds(start, C_WIN), :]
        vw = vref[pl.ds(start, C_WIN), :]
        halves = [int((p >= npair) == straight) for p in range(C_GROUP)]
        qg = jnp.concatenate([qs[p] * (mhi if halves[p] else mlo) for p in range(C_GROUP)], axis=0)
        s_all = lax.dot_general(qg, kw, _NT, preferred_element_type=jnp.float32)
        ps, ls = [], []
        for p in range(C_GROUP):
            h = 2 * p + halves[p]
            s = s_all[p * QB:(p + 1) * QB] * scale + bias_ref[h * QB:(h + 1) * QB, :] + pad_mask
            sink = sink_ref[h]
            m = jnp.maximum(jnp.max(s, axis=-1, keepdims=True), sink)
            e = jnp.exp(s - m)
            ls.append(jnp.sum(e, axis=-1, keepdims=True) + jnp.exp(sink - m))
            ps.append(e.astype(jnp.bfloat16))
        o_all = jnp.dot(jnp.concatenate(ps, axis=0), vw, preferred_element_type=jnp.float32)
        stacks.append(([o_all[p * QB:(p + 1) * QB] / ls[p] for p in range(C_GROUP)], halves))
    for p in range(C_GROUP):
        (o1, h1), (o2, _) = stacks
        lo, hi = (o2[p], o1[p]) if h1[p] else (o1[p], o2[p])
        o_ref[:, p * LANES:(p + 1) * LANES] = jnp.where(lane < HEAD_DIM, lo, hi).astype(o_ref.dtype)


def _attn_c(sinks, projc3, bias):
    nq = SEQ // QB
    pad_buf = pltpu.VMEM((SEQ + C_PAD, LANES), jnp.bfloat16)
    return pl.pallas_call(
        _attn_c_kernel,
        grid=(BATCH, nq),
        in_specs=[
            pl.BlockSpec(memory_space=pltpu.SMEM),
            pl.BlockSpec((None, QB, C_W), lambda b, i: (b, i, 0)),
            pl.BlockSpec((None, SEQ, LANES), lambda b, i: (b, 0, C_W // LANES)),
            pl.BlockSpec((None, SEQ, LANES), lambda b, i: (b, 0, C_W // LANES + 1)),
            pl.BlockSpec((C_Q_HEADS * QB, C_WIN), lambda b, i: (0, 0)),
        ],
        out_specs=pl.BlockSpec((None, QB, C_W), lambda b, i: (b, i, 0)),
        out_shape=jax.ShapeDtypeStruct((BATCH, SEQ, C_W), jnp.bfloat16),
        scratch_shapes=[pad_buf, pad_buf, pad_buf, pad_buf],
        compiler_params=_cparams(("arbitrary", "arbitrary")),
        name="attn_c",
    )(sinks, projc3, projc3, projc3, bias)


def _sort_key(x):
    bits = lax.bitcast_convert_type(x + 0.0, jnp.int32)
    return bits ^ ((bits >> 31) & jnp.int32(0x7FFFFFFF))


def _row_sum8(x):
    return jnp.sum(x.reshape(x.shape[0] // SUBLANES, SUBLANES, x.shape[1]), axis=0)


def _attn_b_kernel(qb_ref, x_ref, kiw_ref, ckv_ref, gain_ref, wuk_ref, wuv_ref, bias_ref, tri_ref,
                   o_ref, ckvn, kd, ql, qi_all, sk, acc_s, p_s):
    i = pl.program_id(1)
    nkb = i // 2 + 1
    mlo, mhi = _lane_masks(jnp.bfloat16)
    f32 = jnp.float32

    @pl.when(i == 0)
    def _():
        ckvn[...] = _rms(ckv_ref[...].astype(f32), gain_ref[...]).astype(ckvn.dtype)
        kiw = kiw_ref[...].astype(f32)
        lane = lax.broadcasted_iota(jnp.int32, kiw.shape, 1)
        kd[...] = jnp.where(lane < HEAD_DIM, kiw, pltpu.roll(kiw, HEAD_DIM, 1)).astype(kd.dtype)

    for h in range(B_HEADS):
        rows = slice(h * QB, (h + 1) * QB)
        qlat = jnp.dot(qb_ref[:, (h // 2) * LANES:(h // 2 + 1) * LANES], wuk_ref[h], preferred_element_type=f32)
        ql[rows, :] = (qlat * HEAD_DIM ** -0.5).astype(ql.dtype)
        qcol = O_QI - X_COL0 + (h // 2) * LANES
        qi_all[rows, :] = x_ref[:, qcol:qcol + LANES] * (mhi if h % 2 else mlo)
    wcol = O_KI - X_COL0
    wi_t = x_ref[:, wcol:wcol + LANES].astype(f32).T * (IDX_HEADS ** -0.5 * IDX_DIM ** -0.5)
    wi_rows = [wi_t[O_WI - O_KI + h:O_WI - O_KI + h + 1, :] for h in range(IDX_HEADS)]

    kpos = lax.broadcasted_iota(jnp.int32, (KB, QB), 0)
    qchunk = (i * QB + lax.broadcasted_iota(jnp.int32, (KB, QB), 1)) // CHUNK

    def admissible(kb):
        return (kb * KB + kpos) // CHUNK <= qchunk

    def score_block(kb, carry):
        kblk = kd[pl.ds(pl.multiple_of(kb * KB, KB), KB), :]
        dots = lax.dot_general(kblk, qi_all[...], _NT, preferred_element_type=f32)
        score = jnp.zeros((KB, QB), f32)
        for h in range(IDX_HEADS):
            score = score + jnp.maximum(dots[:, h * QB:(h + 1) * QB], 0.0) * wi_rows[h]
        sk[kb] = _sort_key(jnp.where(admissible(kb), score, NEG))
        return carry

    lax.fori_loop(0, nkb, score_block, 0)

    neg_key = _sort_key(jnp.full((1, 1), NEG, f32))
    n_rest = ((NKB - nkb) * KB).astype(f32)

    def count(pred_fn, v):
        def body(kb, acc):
            return acc + _row_sum8(jnp.where(pred_fn(sk[kb], v), 1.0, 0.0))

        acc = lax.fori_loop(0, nkb, body, jnp.zeros((SUBLANES, QB), f32))
        return jnp.sum(acc, axis=0, keepdims=True) + jnp.where(pred_fn(neg_key, v), n_rest, 0.0)

    ge = lambda a, v: a >= v
    int_min = jnp.int32(-2 ** 31)
    v0 = jnp.where(count(ge, jnp.zeros((1, QB), jnp.int32)) >= TOPK, jnp.int32(0), int_min)

    def bit_step(t, v):
        cand = v | (jnp.int32(1) << (30 - t))
        return jnp.where(count(ge, cand) >= TOPK, cand, v)

    thr = lax.fori_loop(0, 31, bit_step, v0)
    need = TOPK - count(lambda a, v: a > v, thr)

    acc_s[...] = jnp.zeros(acc_s.shape, f32)

    def attend(kb, carry):
        eq_seen, m_old, l_old = carry
        keys = sk[kb]
        eq = keys == thr
        eq_f = jnp.where(eq, 1.0, 0.0)
        before = jnp.dot(tri_ref[...], eq_f.astype(jnp.bfloat16), preferred_element_type=f32)
        take_eq = jnp.where(eq, jnp.where(eq_seen + before < need, 1.0, 0.0), 0.0)
        take = jnp.where(keys > thr, 1.0, take_eq)
        mask_add = jnp.where(admissible(kb), jnp.where(take > 0.0, 0.0, NEG), NEG)
        cblk = ckvn[pl.ds(pl.multiple_of(kb * KB, KB), KB), :]
        dots = lax.dot_general(cblk, ql[...], _NT, preferred_element_type=f32)
        tile = jnp.minimum(i - 2 * kb, B_NEAR)
        s = dots + bias_ref[tile] + jnp.tile(mask_add, (1, B_HEADS))
        m_new = jnp.maximum(m_old, jnp.max(s, axis=0, keepdims=True))
        alpha = jnp.exp(m_old - m_new)
        e = jnp.exp(s - m_new)
        l_new = alpha * l_old + jnp.sum(e, axis=0, keepdims=True)
        p_s[...] = e.astype(p_s.dtype)
        acc_s[...] = acc_s[...] * alpha + lax.dot_general(cblk, p_s[...], _TN, preferred_element_type=f32)
        return eq_seen + jnp.sum(eq_f, axis=0, keepdims=True), m_new, l_new

    init = (jnp.zeros((1, QB), f32), jnp.full((1, HQ), 4 * NEG, f32), jnp.zeros((1, HQ), f32))
    _, _, l_fin = lax.fori_loop(0, nkb, attend, init)

    o_lat_t = acc_s[...] / l_fin
    for p in range(B_HEADS // 2):
        out = jnp.zeros((QB, LANES), f32)
        for h in (2 * p, 2 * p + 1):
            o_lat = o_lat_t[:, h * QB:(h + 1) * QB].T.astype(jnp.bfloat16)
            out = out + jnp.dot(o_lat, wuv_ref[h], preferred_element_type=f32)
        o_ref[:, p * LANES:(p + 1) * LANES] = out.astype(o_ref.dtype)


def _attn_b(proj3, gain, wuk, wuv, bias, tri, nbatch=BATCH):
    nq = SEQ // QB
    xw = IN_A - X_COL0
    return pl.pallas_call(
        _attn_b_kernel,
        grid=(nbatch, nq),
        in_specs=[
            pl.BlockSpec((None, QB, B_W), lambda b, i: (b, i, O_QB // B_W)),
            pl.BlockSpec((None, QB, xw), lambda b, i: (b, i, X_COL0 // xw)),
            pl.BlockSpec((None, SEQ, LANES), lambda b, i: (b, 0, O_KI // LANES)),
            pl.BlockSpec((None, SEQ, LANES), lambda b, i: (b, 0, O_CKV // LANES)),
            pl.BlockSpec((1, B_LATENT), lambda b, i: (0, 0)),
            pl.BlockSpec((B_HEADS, LANES, B_LATENT), lambda b, i: (0, 0, 0)),
            pl.BlockSpec((B_HEADS, B_LATENT, LANES), lambda b, i: (0, 0, 0)),
            pl.BlockSpec((B_NEAR + 1, KB, HQ), lambda b, i: (0, 0, 0)),
            pl.BlockSpec((KB, KB), lambda b, i: (0, 0)),
        ],
        out_specs=pl.BlockSpec((None, QB, B_W), lambda b, i: (b, i, 0)),
        out_shape=jax.ShapeDtypeStruct((nbatch, SEQ, B_W), jnp.bfloat16),
        scratch_shapes=[
            pltpu.VMEM((SEQ, B_LATENT), jnp.bfloat16),
            pltpu.VMEM((SEQ, LANES), jnp.bfloat16),
            pltpu.VMEM((HQ, B_LATENT), jnp.bfloat16),
            pltpu.VMEM((HQ, LANES), jnp.bfloat16),
            pltpu.VMEM((NKB, KB, QB), jnp.int32),
            pltpu.VMEM((B_LATENT, HQ), jnp.float32),
            pltpu.VMEM((KB, HQ), jnp.bfloat16),
        ],
        compiler_params=_cparams(("arbitrary", "arbitrary")),
        name="attn_b",
    )(proj3, proj3, proj3, proj3, gain, wuk, wuv, bias, tri)


MERGE_BN = 512
MERGE_BM = 1024


def _merge_kernel(oa_ref, ob_ref, oc_ref, ga_ref, gb_ref, gc_ref, wa_ref, wb_ref, wc_ref, o_ref, wa, wb, wc):
    f32 = jnp.float32

    @pl.when(pl.program_id(1) == 0)
    def _():
        wa[...] = wa_ref[...].astype(wa.dtype)
        wb[...] = wb_ref[...].astype(wb.dtype)
        wc[...] = wc_ref[...].astype(wc.dtype)

    m = ga_ref[...].astype(f32) * jnp.dot(oa_ref[...], wa[...], preferred_element_type=f32)
    m = m + gb_ref[...].astype(f32) * jnp.dot(ob_ref[...], wb[...], preferred_element_type=f32)
    m = m + gc_ref[...].astype(f32) * jnp.dot(oc_ref[...], wc[...], preferred_element_type=f32)
    o_ref[...] = m.astype(o_ref.dtype)


def _merge(oa, ob, oc, gates, w_branch, layer):
    t = oa.shape[0]
    bn, bm = MERGE_BN, MERGE_BM
    gstep = D_MODEL // bn
    return pl.pallas_call(
        _merge_kernel,
        grid=(D_MODEL // bn, t // bm),
        in_specs=[
            pl.BlockSpec((bm, A_W), lambda j, i: (i, 0)),
            pl.BlockSpec((bm, B_W), lambda j, i: (i, 0)),
            pl.BlockSpec((bm, C_W), lambda j, i: (i, 0)),
            pl.BlockSpec((bm, bn), lambda j, i: (i, j)),
            pl.BlockSpec((bm, bn), lambda j, i: (i, gstep + j)),
            pl.BlockSpec((bm, bn), lambda j, i: (i, 2 * gstep + j)),
            pl.BlockSpec((None, A_W, bn), lambda j, i: (layer, 0, j)),
            pl.BlockSpec((None, B_W, bn), lambda j, i: (layer, A_W // B_W, j)),
            pl.BlockSpec((None, C_W, bn), lambda j, i: (layer, (A_W + B_W) // C_W, j)),
        ],
        out_specs=pl.BlockSpec((bm, bn), lambda j, i: (i, j)),
        out_shape=jax.ShapeDtypeStruct((t, D_MODEL), jnp.bfloat16),
        scratch_shapes=[pltpu.VMEM((A_W, bn), jnp.bfloat16), pltpu.VMEM((B_W, bn), jnp.bfloat16),
                        pltpu.VMEM((C_W, bn), jnp.bfloat16)],
        compiler_params=_cparams(("arbitrary", "arbitrary")),
        name="merge",
    )(oa, ob, oc, gates, gates, gates, w_branch, w_branch, w_branch)


def _finish(y, x_ref, gp_ref, gn_ref, xo_ref, ho_ref):
    xn = x_ref[...] + _rms(y, gp_ref[...])
    xo_ref[...] = xn
    if ho_ref is not None:
        ho_ref[...] = _rms(xn, gn_ref[...]).astype(ho_ref.dtype)


def _gemm_res_kernel(nk, with_next, a_ref, w_ref, x_ref, gp_ref, gn_ref, xo_ref, *rest):
    ho_ref = rest[0] if with_next else None
    if nk == 1:
        y = jnp.dot(a_ref[...], w_ref[...], preferred_element_type=jnp.float32)
        _finish(y, x_ref, gp_ref, gn_ref, xo_ref, ho_ref)
        return
    acc = rest[-1]
    k = pl.program_id(1)

    @pl.when(k == 0)
    def _():
        acc[...] = jnp.zeros(acc.shape, jnp.float32)

    acc[...] += jnp.dot(a_ref[...], w_ref[...], preferred_element_type=jnp.float32)

    @pl.when(k == nk - 1)
    def _():
        _finish(acc[...], x_ref, gp_ref, gn_ref, xo_ref, ho_ref)


def _gemm_res(a, w, layer, x, g_post, g_next, bk, bm=512, name="gemm_res"):
    t, kdim = a.shape
    n = w.shape[2]
    nk = kdim // bk
    with_next = g_next is not None
    if g_next is None:
        g_next = g_post
    out_shape = [jax.ShapeDtypeStruct((t, n), jnp.float32)]
    out_specs = [pl.BlockSpec((bm, n), lambda i, k: (i, 0))]
    if with_next:
        out_shape.append(jax.ShapeDtypeStruct((t, n), jnp.bfloat16))
        out_specs.append(pl.BlockSpec((bm, n), lambda i, k: (i, 0)))
    res = pl.pallas_call(
        functools.partial(_gemm_res_kernel, nk, with_next),
        grid=(t // bm, nk),
        in_specs=[
            pl.BlockSpec((bm, bk), lambda i, k: (i, k)),
            pl.BlockSpec((None, bk, n), lambda i, k: (layer, k, 0)),
            pl.BlockSpec((bm, n), lambda i, k: (i, 0)),
            pl.BlockSpec((1, n), lambda i, k: (0, 0)),
            pl.BlockSpec((1, n), lambda i, k: (0, 0)),
        ],
        out_specs=out_specs,
        out_shape=out_shape,
        scratch_shapes=[pltpu.VMEM((bm, n), jnp.float32)] if nk > 1 else [],
        compiler_params=_cparams(("parallel", "arbitrary")),
        name=name,
    )(a, w, x, g_post, g_next)
    return (res[0], res[1]) if with_next else (res[0], None)


def _memkv_kernel(m_ref, g_ref, w_ref, o_ref):
    mn = _rms(m_ref[...], g_ref[...]).astype(jnp.bfloat16)
    o_ref[...] = jnp.dot(mn, w_ref[...], preferred_element_type=jnp.float32).astype(o_ref.dtype)


def _memkv(mem2, g, w, layer):
    t, d = mem2.shape
    n = w.shape[2]
    bm = 512
    return pl.pallas_call(
        _memkv_kernel,
        grid=(t // bm,),
        in_specs=[pl.BlockSpec((bm, d), lambda i: (i, 0)), pl.BlockSpec((1, d), lambda i: (0, 0)),
                  pl.BlockSpec((None, d, n), lambda i: (layer, 0, 0))],
        out_specs=pl.BlockSpec((bm, n), lambda i: (i, 0)),
        out_shape=jax.ShapeDtypeStruct((t, n), jnp.bfloat16),
        compiler_params=_cparams(("parallel",)),
        name="mem_kv",
    )(mem2, g, w)


XA_BM = 512


def _xattn_kernel(h_ref, wq_ref, kv_ref, wo_ref, x_ref, gp_ref, gn_ref, xo_ref, ho_ref):
    f32 = jnp.float32
    q = jnp.dot(h_ref[...], wq_ref[...], preferred_element_type=f32).astype(jnp.bfloat16)
    scale = MEM_HEAD_DIM ** -0.5
    outs = []
    for h in range(MEM_HEADS):
        cols = slice(h * LANES, (h + 1) * LANES)
        kh = kv_ref[:, cols]
        vh = kv_ref[:, MEM_W + h * LANES:MEM_W + (h + 1) * LANES]
        s = lax.dot_general(q[:, cols], kh, _NT, preferred_element_type=f32) * scale
        m = jnp.max(s, axis=-1, keepdims=True)
        e = jnp.exp(s - m)
        l = jnp.sum(e, axis=-1, keepdims=True)
        o = jnp.dot(e.astype(jnp.bfloat16), vh, preferred_element_type=f32)
        outs.append((o / l).astype(jnp.bfloat16))
    o = jnp.concatenate(outs, axis=-1)
    y = jnp.dot(o, wo_ref[...], preferred_element_type=f32)
    _finish(y, x_ref, gp_ref, gn_ref, xo_ref, ho_ref)


def _xattn(h, wq, kv, wo, layer, x, g_post, g_next):
    t, d = h.shape
    bm = XA_BM
    per_batch = SEQ // bm
    return pl.pallas_call(
        _xattn_kernel,
        grid=(t // bm,),
        in_specs=[
            pl.BlockSpec((bm, d), lambda i: (i, 0)),
            pl.BlockSpec((None, d, MEM_W), lambda i: (layer, 0, 0)),
            pl.BlockSpec((MEM_LEN, 2 * MEM_W), lambda i: (i // per_batch, 0)),
            pl.BlockSpec((None, MEM_W, d), lambda i: (layer, 0, 0)),
            pl.BlockSpec((bm, d), lambda i: (i, 0)),
            pl.BlockSpec((1, d), lambda i: (0, 0)),
            pl.BlockSpec((1, d), lambda i: (0, 0)),
        ],
        out_specs=[pl.BlockSpec((bm, d), lambda i: (i, 0)), pl.BlockSpec((bm, d), lambda i: (i, 0))],
        out_shape=[jax.ShapeDtypeStruct((t, d), jnp.float32), jax.ShapeDtypeStruct((t, d), jnp.bfloat16)],
        compiler_params=_cparams(("parallel",)),
        name="mem_xattn",
    )(h, wq, kv, wo, x, g_post, g_next)


def _ffn_up_kernel(h_ref, wg_ref, wv_ref, cwg_ref, cwv_ref, cbg_ref, cbv_ref, o_ref, ug, uv):
    i = pl.program_id(1)
    tiles_per_seq = SEQ // FF_BM
    for u in (ug, uv):
        @pl.when(i % tiles_per_seq == 0)
        def _():
            u[0:SUBLANES, :] = jnp.zeros((SUBLANES, FF_BN), jnp.float32)

        @pl.when(i % tiles_per_seq != 0)
        def _():
            u[0:SUBLANES, :] = u[FF_BM:FF_BM + SUBLANES, :]

    def conv(u, cw_ref, cb_ref, r0):
        base = SUBLANES + r0
        acc = cb_ref[...] + u[base - 2:base - 2 + FF_CH, :] * cw_ref[0:1, :]
        acc = acc + u[base - 1:base - 1 + FF_CH, :] * cw_ref[1:2, :]
        return acc + u[base:base + FF_CH, :] * cw_ref[2:3, :]

    for c in range(FF_BM // FF_CH):
        r0 = c * FF_CH
        hb = h_ref[r0:r0 + FF_CH, :]
        ug[SUBLANES + r0:SUBLANES + r0 + FF_CH, :] = jnp.dot(hb, wg_ref[...], preferred_element_type=jnp.float32)
        uv[SUBLANES + r0:SUBLANES + r0 + FF_CH, :] = jnp.dot(hb, wv_ref[...], preferred_element_type=jnp.float32)
        gate = conv(ug, cwg_ref, cbg_ref, r0)
        val = conv(uv, cwv_ref, cbv_ref, r0)
        o_ref[r0:r0 + FF_CH, :] = (jax.nn.gelu(gate) * val).astype(o_ref.dtype)


def _ffn_up(h, w_up, conv_w, conv_b):
    t, d = h.shape
    nj = FF_P // FF_BN
    return pl.pallas_call(
        _ffn_up_kernel,
        grid=(nj, t // FF_BM),
        in_specs=[
            pl.BlockSpec((FF_BM, d), lambda j, i: (i, 0)),
            pl.BlockSpec((d, FF_BN), lambda j, i: (0, j)),
            pl.BlockSpec((d, FF_BN), lambda j, i: (0, nj + j)),
            pl.BlockSpec((CONV_W, FF_BN), lambda j, i: (0, j)),
            pl.BlockSpec((CONV_W, FF_BN), lambda j, i: (0, nj + j)),
            pl.BlockSpec((1, FF_BN), lambda j, i: (0, j)),
            pl.BlockSpec((1, FF_BN), lambda j, i: (0, nj + j)),
        ],
        out_specs=pl.BlockSpec((FF_BM, FF_BN), lambda j, i: (i, j)),
        out_shape=jax.ShapeDtypeStruct((t, FF_P), jnp.bfloat16),
        scratch_shapes=[pltpu.VMEM((FF_BM + SUBLANES, FF_BN), jnp.float32),
                        pltpu.VMEM((FF_BM + SUBLANES, FF_BN), jnp.float32)],
        compiler_params=_cparams(("arbitrary", "arbitrary")),
        name="ffn_up",
    )(h, w_up, w_up, conv_w, conv_w, conv_b, conv_b)


def _pad_heads(w, axis):
    h = w.shape[0]
    zero = jnp.zeros_like(w)
    even = jnp.concatenate([w, zero], axis=axis)
    odd = jnp.concatenate([zero, w], axis=axis)
    sel = (jnp.arange(h) % 2 == 0).reshape((h, 1, 1))
    return jnp.where(sel, even, odd)


def _toeplitz(fn, rows, cols):
    ks = np.concatenate([np.arange(0, cols), np.arange(-(rows - 1), 0)])
    w = fn(ks)
    h, period = w.shape
    x = jnp.tile(w, (1, rows))[:, :rows * (period - 1)].reshape(h, rows, period - 1)
    return x[:, :, :cols].astype(jnp.float32)


def _band(rows, cols, left):
    diff = left + np.arange(rows)[:, None] // CHUNK - np.arange(cols)[None, :] // CHUNK
    return (diff >= 0) & (diff <= left)


def _bias_a(rel_bias):
    fn = lambda ks: rel_bias[np.clip(A_PAD - ks, -A_MAX_REL, A_MAX_REL) + A_MAX_REL].T
    return jnp.where(_band(QB, A_WIN, A_LEFT_CHUNKS)[None], _toeplitz(fn, QB, A_WIN), NEG)


def _bias_c(t5_c):
    fn = lambda ks: t5_c[_t5_bucket(jnp.asarray(ks - C_PAD, jnp.int32))].T
    bias = jnp.where(_band(QB, C_WIN, C_LEFT_CHUNKS)[None], _toeplitz(fn, QB, C_WIN), NEG)
    return bias.reshape(C_Q_HEADS * QB, C_WIN)


def _bias_b(t5_b):
    tiles = []
    for n in range(B_NEAR + 1):
        off = QB * n if n < B_NEAR else SEQ
        fn = lambda ks, off=off: t5_b[_t5_bucket(jnp.asarray(-ks - off, jnp.int32))].T
        tile = _toeplitz(fn, KB, QB)
        tiles.append(jnp.transpose(tile, (1, 0, 2)).reshape(KB, HQ))
    return jnp.stack(tiles)


def _pad_ff(a, dtype):
    z = jnp.zeros((a.shape[0], FF_P - D_FF), dtype)
    return jnp.concatenate([a[:, :D_FF].astype(dtype), z, a[:, D_FF:].astype(dtype), z], axis=1)


def kernel(x, mem, t5_table, norm_gains, w_in, a_rel_bias, ckv_gain, w_uk, w_uv, sinks, w_branch, w_o,
           mem_gain, w_mq, w_mkv, w_mo, w_up, conv_w, conv_b, w_down):
    bf16 = jnp.bfloat16
    xs = x.reshape(TOKENS, D_MODEL)
    mem2 = mem.reshape(BATCH * MEM_LEN, D_MODEL)
    tri = jnp.asarray(np.tril(np.ones((KB, KB), np.float32), -1), bf16)
    bias_b = _bias_b(t5_table[:, :B_HEADS])
    bias_c = _bias_c(t5_table[:, B_HEADS:])
    gains = norm_gains.reshape(DEPTH, 6, 1, D_MODEL)
    w_o_b, w_mq_b, w_mkv_b, w_mo_b = (w.astype(bf16) for w in (w_o, w_mq, w_mkv, w_mo))
    w_dn_b = jnp.pad(w_down.astype(bf16), ((0, 0), (0, FF_P - D_FF), (0, 0)))

    h = _norm(xs, gains[0, 0])
    for l in range(DEPTH):
        g = gains[l]
        proja = _proj(h, w_in, pl.BlockSpec((None, D_MODEL, 1024), lambda j, i, l=l: (l, 0, j)), IN_A, 1024,
                      gate=False, cast=True, name="in_proj_a")
        w_c = w_in[l, :, O_QC:O_GL].astype(bf16)
        w_g = w_in[l, :, O_GL:].astype(bf16)
        projc = _proj(h, w_c, pl.BlockSpec((D_MODEL, C_COLS), lambda j, i: (0, j)), C_COLS, C_COLS,
                      gate=False, cast=False, name="in_proj_c")
        gates = _proj(h, w_g, pl.BlockSpec((D_MODEL, 1536), lambda j, i: (0, j)), 3 * D_MODEL, 1536,
                      gate=True, cast=False, name="in_proj_g")
        proja3 = proja.reshape(BATCH, SEQ, IN_A)
        oa = _attn_a(proja3, _bias_a(a_rel_bias[l]))
        wuk = jnp.transpose(_pad_heads(w_uk[l], axis=2), (0, 2, 1)).astype(bf16)
        wuv = _pad_heads(w_uv[l], axis=2).astype(bf16)
        ob = _attn_b(proja3, ckv_gain[l].reshape(1, B_LATENT), wuk, wuv, bias_b, tri)
        oc = _attn_c(sinks[l], projc.reshape(BATCH, SEQ, C_COLS), bias_c)
        merged = _merge(oa.reshape(TOKENS, A_W), ob.reshape(TOKENS, B_W), oc.reshape(TOKENS, C_W), gates, w_branch, l)
        xs, h = _gemm_res(merged, w_o_b, l, xs, g[1], g[2], bk=D_MODEL, name="out_proj")
        kv = _memkv(mem2, mem_gain[l].reshape(1, D_MODEL), w_mkv_b, l)
        xs, h = _xattn(h, w_mq_b, kv, w_mo_b, l, xs, g[3], g[4])
        hidden = _ffn_up(h, _pad_ff(w_up[l], bf16), _pad_ff(conv_w[l], jnp.float32),
                         _pad_ff(conv_b[l].reshape(1, -1), jnp.float32))
        g_next = gains[l + 1, 0] if l + 1 < DEPTH else None
        xs, h = _gemm_res(hidden, w_dn_b, l, xs, g[5], g_next, bk=DOWN_BK, name="ffn_down")
    return xs.reshape(BATCH, SEQ, D_MODEL)
```

```python
import functools
import math

import numpy as np
import jax
import jax.numpy as jnp
from jax import lax
from jax.experimental import pallas as pl
from jax.experimental.pallas import tpu as pltpu

D_MODEL = 2048
BATCH = 4
SEQ = 2048
DEPTH = 2
TOKENS = BATCH * SEQ
CHUNK = 64
EPS = 1e-6
NEG = -1e30
A_HEADS = 8
A_LEFT_CHUNKS = 8
A_MAX_REL = 128
A_W = 512
B_HEADS = 8
B_W = 512
B_LATENT = 128
IDX_HEADS = 8
IDX_DIM = 64
TOPK = 256
C_Q_HEADS = 16
C_GROUP = 8
C_W = 1024
C_LEFT_CHUNKS = 2
T5_BUCKETS = 32
T5_MAX_DIST = 256
MEM_LEN = 256
MEM_HEADS = 4
MEM_HEAD_DIM = 128
MEM_W = 512
D_FF = 5504
CONV_W = 3

LANES = 128
SUBLANES = 8
HEAD_DIM = 64
QB = 128
KB = 256
VMEM_LIMIT = 56 * 1024 * 1024

O_QA, O_KA, O_VA, O_QB = 0, 512, 1024, 1536
O_CKV = 2048
O_QI = 2176
O_KI = 2688
O_WI = 2752
O_QC = 2760
O_GL = 4040
IN_W = O_GL + 3 * D_MODEL
IN_A = 3072
X_COL0 = 2048
C_COLS = O_GL - O_QC
PROJ_BM = 1024
PROJ_TR = 256

FF_P = 5632
FF_BN = 512
FF_BM = 1024
FF_CH = 256
DOWN_BK = 1408

A_WIN = (A_LEFT_CHUNKS + 2) * CHUNK
C_WIN = (C_LEFT_CHUNKS + 2) * CHUNK
A_PAD = A_LEFT_CHUNKS * CHUNK
C_PAD = C_LEFT_CHUNKS * CHUNK
B_NEAR = 4
NKB = SEQ // KB
HQ = B_HEADS * QB

_NT = (((1,), (1,)), ((), ()))
_TN = (((0,), (0,)), ((), ()))


def _cparams(sem):
    return pltpu.CompilerParams(dimension_semantics=sem, vmem_limit_bytes=VMEM_LIMIT)


def _t5_bucket(rel):
    half = T5_BUCKETS // 2
    max_exact = half // 2
    sign = jnp.where(rel > 0, half, 0)
    d = jnp.abs(rel)
    d_f = jnp.maximum(d, 1).astype(jnp.float32)
    large = max_exact + (jnp.log(d_f / max_exact) / math.log(T5_MAX_DIST / max_exact) * (half - max_exact)).astype(jnp.int32)
    large = jnp.minimum(large, half - 1)
    return sign + jnp.where(d < max_exact, d, large)


def _far_bucket_is_constant():
    d = np.arange(QB * B_NEAR - (KB - 1), SEQ, dtype=np.float32)
    assert d[0] > T5_MAX_DIST
    large = 8 + (np.log(d / 8) / math.log(T5_MAX_DIST / 8) * 8).astype(np.int32)
    return bool(np.all(np.minimum(large, 15) == 15))


assert _far_bucket_is_constant()


def _rms(v, g):
    return v * lax.rsqrt(jnp.mean(v * v, axis=-1, keepdims=True) + EPS) * g


def _norm_kernel(x_ref, g_ref, o_ref):
    o_ref[...] = _rms(x_ref[...], g_ref[...]).astype(o_ref.dtype)


def _norm(x, g, bm=1024):
    t, d = x.shape
    return pl.pallas_call(
        _norm_kernel,
        grid=(t // bm,),
        in_specs=[pl.BlockSpec((bm, d), lambda i: (i, 0)), pl.BlockSpec((1, d), lambda i: (0, 0))],
        out_specs=pl.BlockSpec((bm, d), lambda i: (i, 0)),
        out_shape=jax.ShapeDtypeStruct((t, d), jnp.bfloat16),
        compiler_params=_cparams(("parallel",)),
        name="rmsnorm",
    )(x, g)


def _proj_kernel(gate, h_ref, wt_ref, o_ref, wb):
    @pl.when(pl.program_id(1) == 0)
    def _():
        for c in range(wt_ref.shape[0] // PROJ_TR):
            rows = slice(c * PROJ_TR, (c + 1) * PROJ_TR)
            wb[:, rows] = wt_ref[rows, :].T.astype(wb.dtype)

    acc = jnp.dot(h_ref[...], wb[...], preferred_element_type=jnp.float32)
    o_ref[...] = (jax.nn.sigmoid(acc) if gate else acc).astype(o_ref.dtype)


def _proj(h, w_t, layer, col0, n, bn, gate, name):
    t, k = h.shape
    assert col0 % SUBLANES == 0 and bn % SUBLANES == 0
    w_spec = pl.BlockSpec((None, pl.Element(bn), pl.Element(k)),
                          lambda j, i: (layer, pl.multiple_of(col0 + j * bn, SUBLANES), 0))
    return pl.pallas_call(
        functools.partial(_proj_kernel, gate),
        grid=(n // bn, t // PROJ_BM),
        in_specs=[pl.BlockSpec((PROJ_BM, k), lambda j, i: (i, 0)), w_spec],
        out_specs=pl.BlockSpec((PROJ_BM, bn), lambda j, i: (i, j)),
        out_shape=jax.ShapeDtypeStruct((t, n), jnp.bfloat16),
        scratch_shapes=[pltpu.VMEM((k, bn), jnp.bfloat16)],
        compiler_params=_cparams(("arbitrary", "arbitrary")),
        name=name,
    )(h, w_t)


def _lane_masks(dtype):
    lane = lax.broadcasted_iota(jnp.int32, (1, LANES), 1)
    lo = (lane < HEAD_DIM).astype(jnp.float32)
    return lo.astype(dtype), (1.0 - lo).astype(dtype)


def _swap_halves(x):
    return pltpu.roll(x.astype(jnp.float32), HEAD_DIM, 1).astype(x.dtype)


def _fill_padded(src, dst_ref, pad):
    dst_ref[0:pad, :] = jnp.zeros((pad, dst_ref.shape[1]), dst_ref.dtype)
    dst_ref[pad:pad + SEQ, :] = src


def _attn_a_kernel(q_ref, k_ref, v_ref, bias_ref, o_ref, kpad, vpad):
    i = pl.program_id(1)

    @pl.when(i == 0)
    def _():
        _fill_padded(k_ref[...], kpad, A_PAD)
        _fill_padded(v_ref[...], vpad, A_PAD)

    start = pl.multiple_of(i * QB, QB)
    kw = kpad[pl.ds(start, A_WIN), :]
    vw = vpad[pl.ds(start, A_WIN), :]
    mlo, mhi = _lane_masks(jnp.bfloat16)
    lane = lax.broadcasted_iota(jnp.int32, (QB, LANES), 1)
    kchunk = lax.broadcasted_iota(jnp.int32, (QB, A_WIN), 1) // CHUNK
    pad_mask = jnp.where(kchunk >= A_LEFT_CHUNKS - 2 * i, 0.0, NEG)
    scale = HEAD_DIM ** -0.5
    for p in range(A_HEADS // 2):
        cols = slice(p * LANES, (p + 1) * LANES)
        qp = q_ref[:, cols]
        kp = kw[:, cols]
        vp = vw[:, cols]
        outs = []
        for half, msk in enumerate((mlo, mhi)):
            s = lax.dot_general(qp * msk, kp, _NT, preferred_element_type=jnp.float32)
            s = s * scale + bias_ref[2 * p + half] + pad_mask
            m = jnp.max(s, axis=-1, keepdims=True)
            e = jnp.exp(s - m)
            l = jnp.sum(e, axis=-1, keepdims=True)
            o = jnp.dot(e.astype(jnp.bfloat16), vp, preferred_element_type=jnp.float32)
            outs.append(o / l)
        o_ref[:, cols] = jnp.where(lane < HEAD_DIM, outs[0], outs[1]).astype(o_ref.dtype)


def _attn_a(proj3, bias):
    nq = SEQ // QB
    return pl.pallas_call(
        _attn_a_kernel,
        grid=(BATCH, nq),
        in_specs=[
            pl.BlockSpec((None, QB, A_W), lambda b, i: (b, i, O_QA // A_W)),
            pl.BlockSpec((None, SEQ, A_W), lambda b, i: (b, 0, O_KA // A_W)),
            pl.BlockSpec((None, SEQ, A_W), lambda b, i: (b, 0, O_VA // A_W)),
            pl.BlockSpec((A_HEADS, QB, A_WIN), lambda b, i: (0, 0, 0)),
        ],
        out_specs=pl.BlockSpec((None, QB, A_W), lambda b, i: (b, i, 0)),
        out_shape=jax.ShapeDtypeStruct((BATCH, SEQ, A_W), jnp.bfloat16),
        scratch_shapes=[pltpu.VMEM((SEQ + A_PAD, A_W), jnp.bfloat16), pltpu.VMEM((SEQ + A_PAD, A_W), jnp.bfloat16)],
        compiler_params=_cparams(("arbitrary", "arbitrary")),
        name="attn_a",
    )(proj3, proj3, proj3, bias)


def _attn_c_kernel(sink_ref, q_ref, k_ref, v_ref, bias_ref, o_ref, kpad, kswp, vpad, vswp):
    i = pl.program_id(1)

    @pl.when(i == 0)
    def _():
        k = k_ref[...]
        v = v_ref[...]
        _fill_padded(k, kpad, C_PAD)
        _fill_padded(_swap_halves(k), kswp, C_PAD)
        _fill_padded(v, vpad, C_PAD)
        _fill_padded(_swap_halves(v), vswp, C_PAD)

    start = pl.multiple_of(i * QB, QB)
    mlo, mhi = _lane_masks(jnp.bfloat16)
    lane = lax.broadcasted_iota(jnp.int32, (QB, LANES), 1)
    kchunk = lax.broadcasted_iota(jnp.int32, (QB, C_WIN), 1) // CHUNK
    pad_mask = jnp.where(kchunk >= C_LEFT_CHUNKS - 2 * i, 0.0, NEG)
    scale = HEAD_DIM ** -0.5
    qs = [q_ref[:, p * LANES:(p + 1) * LANES] for p in range(C_GROUP)]
    npair = C_GROUP // 2
    stacks = []
    for straight in (True, False):
        kref, vref = (kpad, vpad) if straight else (kswp, vswp)
        kw = kref[pl.ds(start, C_WIN), :]
        vw = vref[pl.ds(start, C_WIN), :]
        halves = [int((p >= npair) == straight) for p in range(C_GROUP)]
        qg = jnp.concatenate([qs[p] * (mhi if halves[p] else mlo) for p in range(C_GROUP)], axis=0)
        s_all = lax.dot_general(qg, kw, _NT, preferred_element_type=jnp.float32)
        ps, ls = [], []
        for p in range(C_GROUP):
            h = 2 * p + halves[p]
            s = s_all[p * QB:(p + 1) * QB] * scale + bias_ref[h * QB:(h + 1) * QB, :] + pad_mask
            sink = sink_ref[h]
            m = jnp.maximum(jnp.max(s, axis=-1, keepdims=True), sink)
            e = jnp.exp(s - m)
            ls.append(jnp.sum(e, axis=-1, keepdims=True) + jnp.exp(sink - m))
            ps.append(e.astype(jnp.bfloat16))
        o_all = jnp.dot(jnp.concatenate(ps, axis=0), vw, preferred_element_type=jnp.float32)
        stacks.append(([o_all[p * QB:(p + 1) * QB] / ls[p] for p in range(C_GROUP)], halves))
    for p in range(C_GROUP):
        (o1, h1), (o2, _) = stacks
        lo, hi = (o2[p], o1[p]) if h1[p] else (o1[p], o2[p])
        o_ref[:, p * LANES:(p + 1) * LANES] = jnp.where(lane < HEAD_DIM, lo, hi).astype(o_ref.dtype)


def _attn_c(sinks, projc3, bias):
    nq = SEQ // QB
    pad_buf = pltpu.VMEM((SEQ + C_PAD, LANES), jnp.bfloat16)
    return pl.pallas_call(
        _attn_c_kernel,
        grid=(BATCH, nq),
        in_specs=[
            pl.BlockSpec(memory_space=pltpu.SMEM),
            pl.BlockSpec((None, QB, C_W), lambda b, i: (b, i, 0)),
            pl.BlockSpec((None, SEQ, LANES), lambda b, i: (b, 0, C_W // LANES)),
            pl.BlockSpec((None, SEQ, LANES), lambda b, i: (b, 0, C_W // LANES + 1)),
            pl.BlockSpec((C_Q_HEADS * QB, C_WIN), lambda b, i: (0, 0)),
        ],
        out_specs=pl.BlockSpec((None, QB, C_W), lambda b, i: (b, i, 0)),
        out_shape=jax.ShapeDtypeStruct((BATCH, SEQ, C_W), jnp.bfloat16),
        scratch_shapes=[pad_buf, pad_buf, pad_buf, pad_buf],
        compiler_params=_cparams(("arbitrary", "arbitrary")),
        name="attn_c",
    )(sinks, projc3, projc3, projc3, bias)


def _sort_key(x):
    bits = lax.bitcast_convert_type(x + 0.0, jnp.int32)
    return bits ^ ((bits >> 31) & jnp.int32(0x7FFFFFFF))


def _rows8(x, op):
    parts = [x[r:r + SUBLANES, :] for r in range(0, x.shape[0], SUBLANES)]
    while len(parts) > 1:
        nxt = [op(parts[k], parts[k + 1]) for k in range(0, len(parts) - 1, 2)]
        parts = nxt + ([parts[-1]] if len(parts) % 2 else [])
    return parts[0]


def _attn_b_kernel(qb_ref, x_ref, kiw_ref, ckv_ref, gain_ref, wuk_ref, wuv_ref, bias_ref, tri_ref,
                   o_ref, ckvn, kd, ql, qi_all, sk, acc_s, p_s):
    i = pl.program_id(1)
    nkb = i // 2 + 1
    mlo, mhi = _lane_masks(jnp.bfloat16)
    f32 = jnp.float32

    @pl.when(i == 0)
    def _():
        ckvn[...] = _rms(ckv_ref[...].astype(f32), gain_ref[...]).astype(ckvn.dtype)
        kiw = kiw_ref[...].astype(f32)
        lane = lax.broadcasted_iota(jnp.int32, kiw.shape, 1)
        kd[...] = jnp.where(lane < HEAD_DIM, kiw, pltpu.roll(kiw, HEAD_DIM, 1)).astype(kd.dtype)

    for h in range(B_HEADS):
        rows = slice(h * QB, (h + 1) * QB)
        qlat = jnp.dot(qb_ref[:, (h // 2) * LANES:(h // 2 + 1) * LANES], wuk_ref[h], preferred_element_type=f32)
        ql[rows, :] = (qlat * HEAD_DIM ** -0.5).astype(ql.dtype)
        qcol = O_QI - X_COL0 + (h // 2) * LANES
        qi_all[rows, :] = x_ref[:, qcol:qcol + LANES] * (mhi if h % 2 else mlo)
    wcol = O_KI - X_COL0
    wi_t = x_ref[:, wcol:wcol + LANES].astype(f32).T * (IDX_HEADS ** -0.5 * IDX_DIM ** -0.5)
    wi_rows = [wi_t[O_WI - O_KI + h:O_WI - O_KI + h + 1, :] for h in range(IDX_HEADS)]

    kpos = lax.broadcasted_iota(jnp.int32, (KB, QB), 0)
    key_limit = ((i * QB + lax.broadcasted_iota(jnp.int32, (1, QB), 1)) // CHUNK + 1) * CHUNK

    def admissible(kb):
        return kb * KB + kpos < key_limit

    def score_block(kb, carry):
        kblk = kd[pl.ds(pl.multiple_of(kb * KB, KB), KB), :]
        dots = lax.dot_general(kblk, qi_all[...], _NT, preferred_element_type=f32)
        score = jnp.zeros((KB, QB), f32)
        for h in range(IDX_HEADS):
            score = score + jnp.maximum(dots[:, h * QB:(h + 1) * QB], 0.0) * wi_rows[h]
        sk[kb] = _sort_key(jnp.where(admissible(kb), score, NEG))
        return carry

    lax.fori_loop(0, nkb, score_block, 0)

    neg_key = _sort_key(jnp.full((1, 1), NEG, f32))
    n_rest = ((NKB - nkb) * KB).astype(f32)

    def count(pred_fn, v):
        def body(kb, acc):
            return acc + _rows8(jnp.where(pred_fn(sk[kb], v), 1.0, 0.0), jnp.add)

        acc = lax.fori_loop(0, nkb, body, jnp.zeros((SUBLANES, QB), f32))
        return jnp.sum(acc, axis=0, keepdims=True) + jnp.where(pred_fn(neg_key, v), n_rest, 0.0)

    ge = lambda a, v: a >= v
    int_min = jnp.int32(-2 ** 31)
    v0 = jnp.where(count(ge, jnp.zeros((1, QB), jnp.int32)) >= TOPK, jnp.int32(0), int_min)

    def bit_step(t, v):
        cand = v | (jnp.int32(1) << (30 - t))
        return jnp.where(count(ge, cand) >= TOPK, cand, v)

    thr = lax.fori_loop(0, 31, bit_step, v0)
    need = TOPK - count(lambda a, v: a > v, thr)

    acc_s[...] = jnp.zeros(acc_s.shape, f32)

    def attend(kb, carry):
        eq_seen, m_old, l_old = carry
        keys = sk[kb]
        eq = keys == thr
        eq_f = jnp.where(eq, 1.0, 0.0)
        before = jnp.dot(tri_ref[...], eq_f.astype(jnp.bfloat16), preferred_element_type=f32)
        take_eq = jnp.where(eq, jnp.where(eq_seen + before < need, 1.0, 0.0), 0.0)
        take = jnp.where(keys > thr, 1.0, take_eq)
        mask_add = jnp.where(admissible(kb), jnp.where(take > 0.0, 0.0, NEG), NEG)
        cblk = ckvn[pl.ds(pl.multiple_of(kb * KB, KB), KB), :]
        dots = lax.dot_general(cblk, ql[...], _NT, preferred_element_type=f32)
        tile = jnp.minimum(i - 2 * kb, B_NEAR)
        m_parts, l_parts, a_parts = [], [], []
        for h in range(B_HEADS):
            cols = slice(h * QB, (h + 1) * QB)
            s = dots[:, cols] + bias_ref[tile, :, cols] + mask_add
            mo = m_old[:, cols]
            mn = jnp.maximum(mo, jnp.max(_rows8(s, jnp.maximum), axis=0, keepdims=True))
            a = jnp.exp(mo - mn)
            e = jnp.exp(s - mn)
            l_parts.append(a * l_old[:, cols] + jnp.sum(_rows8(e, jnp.add), axis=0, keepdims=True))
            m_parts.append(mn)
            a_parts.append(a)
            p_s[:, cols] = e.astype(p_s.dtype)
        alpha = jnp.concatenate(a_parts, axis=1)
        acc_s[...] = acc_s[...] * alpha + lax.dot_general(cblk, p_s[...], _TN, preferred_element_type=f32)
        eq_seen = eq_seen + jnp.sum(_rows8(eq_f, jnp.add), axis=0, keepdims=True)
        return eq_seen, jnp.concatenate(m_parts, axis=1), jnp.concatenate(l_parts, axis=1)

    init = (jnp.zeros((1, QB), f32), jnp.full((1, HQ), 4 * NEG, f32), jnp.zeros((1, HQ), f32))
    _, _, l_fin = lax.fori_loop(0, nkb, attend, init)

    o_lat_t = acc_s[...] / l_fin
    for p in range(B_HEADS // 2):
        out = jnp.zeros((QB, LANES), f32)
        for h in (2 * p, 2 * p + 1):
            o_lat = o_lat_t[:, h * QB:(h + 1) * QB].T.astype(jnp.bfloat16)
            out = out + jnp.dot(o_lat, wuv_ref[h], preferred_element_type=f32)
        o_ref[:, p * LANES:(p + 1) * LANES] = out.astype(o_ref.dtype)


def _attn_b(proj3, gain, wuk, wuv, bias, tri, nbatch=BATCH):
    nq = SEQ // QB
    xw = IN_A - X_COL0
    return pl.pallas_call(
        _attn_b_kernel,
        grid=(nbatch, nq),
        in_specs=[
            pl.BlockSpec((None, QB, B_W), lambda b, i: (b, i, O_QB // B_W)),
            pl.BlockSpec((None, QB, xw), lambda b, i: (b, i, X_COL0 // xw)),
            pl.BlockSpec((None, SEQ, LANES), lambda b, i: (b, 0, O_KI // LANES)),
            pl.BlockSpec((None, SEQ, LANES), lambda b, i: (b, 0, O_CKV // LANES)),
            pl.BlockSpec((1, B_LATENT), lambda b, i: (0, 0)),
            pl.BlockSpec((B_HEADS, LANES, B_LATENT), lambda b, i: (0, 0, 0)),
            pl.BlockSpec((B_HEADS, B_LATENT, LANES), lambda b, i: (0, 0, 0)),
            pl.BlockSpec((B_NEAR + 1, KB, HQ), lambda b, i: (0, 0, 0)),
            pl.BlockSpec((KB, KB), lambda b, i: (0, 0)),
        ],
        out_specs=pl.BlockSpec((None, QB, B_W), lambda b, i: (b, i, 0)),
        out_shape=jax.ShapeDtypeStruct((nbatch, SEQ, B_W), jnp.bfloat16),
        scratch_shapes=[
            pltpu.VMEM((SEQ, B_LATENT), jnp.bfloat16),
            pltpu.VMEM((SEQ, LANES), jnp.bfloat16),
            pltpu.VMEM((HQ, B_LATENT), jnp.bfloat16),
            pltpu.VMEM((HQ, LANES), jnp.bfloat16),
            pltpu.VMEM((NKB, KB, QB), jnp.int32),
            pltpu.VMEM((B_LATENT, HQ), jnp.float32),
            pltpu.VMEM((KB, HQ), jnp.bfloat16),
        ],
        compiler_params=_cparams(("arbitrary", "arbitrary")),
        name="attn_b",
    )(proj3, proj3, proj3, proj3, gain, wuk, wuv, bias, tri)


MERGE_BN = 512
MERGE_BM = 1024


def _merge_kernel(oa_ref, ob_ref, oc_ref, ga_ref, gb_ref, gc_ref, wa_ref, wb_ref, wc_ref, o_ref, wa, wb, wc):
    f32 = jnp.float32

    @pl.when(pl.program_id(1) == 0)
    def _():
        wa[...] = wa_ref[...].astype(wa.dtype)
        wb[...] = wb_ref[...].astype(wb.dtype)
        wc[...] = wc_ref[...].astype(wc.dtype)

    m = ga_ref[...].astype(f32) * jnp.dot(oa_ref[...], wa[...], preferred_element_type=f32)
    m = m + gb_ref[...].astype(f32) * jnp.dot(ob_ref[...], wb[...], preferred_element_type=f32)
    m = m + gc_ref[...].astype(f32) * jnp.dot(oc_ref[...], wc[...], preferred_element_type=f32)
    o_ref[...] = m.astype(o_ref.dtype)


def _merge(oa, ob, oc, gates, w_branch, layer):
    t = oa.shape[0]
    bn, bm = MERGE_BN, MERGE_BM
    gstep = D_MODEL // bn
    return pl.pallas_call(
        _merge_kernel,
        grid=(D_MODEL // bn, t // bm),
        in_specs=[
            pl.BlockSpec((bm, A_W), lambda j, i: (i, 0)),
            pl.BlockSpec((bm, B_W), lambda j, i: (i, 0)),
            pl.BlockSpec((bm, C_W), lambda j, i: (i, 0)),
            pl.BlockSpec((bm, bn), lambda j, i: (i, j)),
            pl.BlockSpec((bm, bn), lambda j, i: (i, gstep + j)),
            pl.BlockSpec((bm, bn), lambda j, i: (i, 2 * gstep + j)),
            pl.BlockSpec((None, A_W, bn), lambda j, i: (layer, 0, j)),
            pl.BlockSpec((None, B_W, bn), lambda j, i: (layer, A_W // B_W, j)),
            pl.BlockSpec((None, C_W, bn), lambda j, i: (layer, (A_W + B_W) // C_W, j)),
        ],
        out_specs=pl.BlockSpec((bm, bn), lambda j, i: (i, j)),
        out_shape=jax.ShapeDtypeStruct((t, D_MODEL), jnp.bfloat16),
        scratch_shapes=[pltpu.VMEM((A_W, bn), jnp.bfloat16), pltpu.VMEM((B_W, bn), jnp.bfloat16),
                        pltpu.VMEM((C_W, bn), jnp.bfloat16)],
        compiler_params=_cparams(("arbitrary", "arbitrary")),
        name="merge",
    )(oa, ob, oc, gates, gates, gates, w_branch, w_branch, w_branch)


def _finish(y, x_ref, gp_ref, gn_ref, xo_ref, ho_ref):
    xn = x_ref[...] + _rms(y, gp_ref[...])
    xo_ref[...] = xn
    if ho_ref is not None:
        ho_ref[...] = _rms(xn, gn_ref[...]).astype(ho_ref.dtype)


def _gemm_res_kernel(nk, with_next, a_ref, w_ref, x_ref, gp_ref, gn_ref, xo_ref, *rest):
    ho_ref = rest[0] if with_next else None
    if nk == 1:
        y = jnp.dot(a_ref[...], w_ref[...], preferred_element_type=jnp.float32)
        _finish(y, x_ref, gp_ref, gn_ref, xo_ref, ho_ref)
        return
    acc = rest[-1]
    k = pl.program_id(1)

    @pl.when(k == 0)
    def _():
        acc[...] = jnp.zeros(acc.shape, jnp.float32)

    acc[...] += jnp.dot(a_ref[...], w_ref[...], preferred_element_type=jnp.float32)

    @pl.when(k == nk - 1)
    def _():
        _finish(acc[...], x_ref, gp_ref, gn_ref, xo_ref, ho_ref)


def _gemm_res(a, w, layer, x, g_post, g_next, bk, bm=512, name="gemm_res"):
    t, kdim = a.shape
    n = w.shape[2]
    nk = kdim // bk
    with_next = g_next is not None
    if g_next is None:
        g_next = g_post
    out_shape = [jax.ShapeDtypeStruct((t, n), jnp.float32)]
    out_specs = [pl.BlockSpec((bm, n), lambda i, k: (i, 0))]
    if with_next:
        out_shape.append(jax.ShapeDtypeStruct((t, n), jnp.bfloat16))
        out_specs.append(pl.BlockSpec((bm, n), lambda i, k: (i, 0)))
    res = pl.pallas_call(
        functools.partial(_gemm_res_kernel, nk, with_next),
        grid=(t // bm, nk),
        in_specs=[
            pl.BlockSpec((bm, bk), lambda i, k: (i, k)),
            pl.BlockSpec((None, bk, n), lambda i, k: (layer, k, 0)),
            pl.BlockSpec((bm, n), lambda i, k: (i, 0)),
            pl.BlockSpec((1, n), lambda i, k: (0, 0)),
            pl.BlockSpec((1, n), lambda i, k: (0, 0)),
        ],
        out_specs=out_specs,
        out_shape=out_shape,
        scratch_shapes=[pltpu.VMEM((bm, n), jnp.float32)] if nk > 1 else [],
        compiler_params=_cparams(("parallel", "arbitrary")),
        name=name,
    )(a, w, x, g_post, g_next)
    return (res[0], res[1]) if with_next else (res[0], None)


def _memkv_kernel(m_ref, g_ref, w_ref, o_ref):
    mn = _rms(m_ref[...], g_ref[...]).astype(jnp.bfloat16)
    o_ref[...] = jnp.dot(mn, w_ref[...], preferred_element_type=jnp.float32).astype(o_ref.dtype)


def _memkv(mem2, g, w, layer):
    t, d = mem2.shape
    n = w.shape[2]
    bm = 512
    return pl.pallas_call(
        _memkv_kernel,
        grid=(t // bm,),
        in_specs=[pl.BlockSpec((bm, d), lambda i: (i, 0)), pl.BlockSpec((1, d), lambda i: (0, 0)),
                  pl.BlockSpec((None, d, n), lambda i: (layer, 0, 0))],
        out_specs=pl.BlockSpec((bm, n), lambda i: (i, 0)),
        out_shape=jax.ShapeDtypeStruct((t, n), jnp.bfloat16),
        compiler_params=_cparams(("parallel",)),
        name="mem_kv",
    )(mem2, g, w)


XA_BM = 512


def _xattn_kernel(h_ref, wq_ref, kv_ref, wo_ref, x_ref, gp_ref, gn_ref, xo_ref, ho_ref):
    f32 = jnp.float32
    q = jnp.dot(h_ref[...], wq_ref[...], preferred_element_type=f32).astype(jnp.bfloat16)
    scale = MEM_HEAD_DIM ** -0.5
    outs = []
    for h in range(MEM_HEADS):
        cols = slice(h * LANES, (h + 1) * LANES)
        kh = kv_ref[:, cols]
        vh = kv_ref[:, MEM_W + h * LANES:MEM_W + (h + 1) * LANES]
        s = lax.dot_general(q[:, cols], kh, _NT, preferred_element_type=f32) * scale
        m = jnp.max(s, axis=-1, keepdims=True)
        e = jnp.exp(s - m)
        l = jnp.sum(e, axis=-1, keepdims=True)
        o = jnp.dot(e.astype(jnp.bfloat16), vh, preferred_element_type=f32)
        outs.append((o / l).astype(jnp.bfloat16))
    o = jnp.concatenate(outs, axis=-1)
    y = jnp.dot(o, wo_ref[...], preferred_element_type=f32)
    _finish(y, x_ref, gp_ref, gn_ref, xo_ref, ho_ref)


def _xattn(h, wq, kv, wo, layer, x, g_post, g_next):
    t, d = h.shape
    bm = XA_BM
    per_batch = SEQ // bm
    return pl.pallas_call(
        _xattn_kernel,
        grid=(t // bm,),
        in_specs=[
            pl.BlockSpec((bm, d), lambda i: (i, 0)),
            pl.BlockSpec((None, d, MEM_W), lambda i: (layer, 0, 0)),
            pl.BlockSpec((MEM_LEN, 2 * MEM_W), lambda i: (i // per_batch, 0)),
            pl.BlockSpec((None, MEM_W, d), lambda i: (layer, 0, 0)),
            pl.BlockSpec((bm, d), lambda i: (i, 0)),
            pl.BlockSpec((1, d), lambda i: (0, 0)),
            pl.BlockSpec((1, d), lambda i: (0, 0)),
        ],
        out_specs=[pl.BlockSpec((bm, d), lambda i: (i, 0)), pl.BlockSpec((bm, d), lambda i: (i, 0))],
        out_shape=[jax.ShapeDtypeStruct((t, d), jnp.float32), jax.ShapeDtypeStruct((t, d), jnp.bfloat16)],
        compiler_params=_cparams(("parallel",)),
        name="mem_xattn",
    )(h, wq, kv, wo, x, g_post, g_next)


def _ffn_up_kernel(h_ref, wgf_ref, wvf_ref, cwg_ref, cwv_ref, cbg_ref, cbv_ref, o_ref, wg_ref, wv_ref, ug, uv):
    j = pl.program_id(0)
    i = pl.program_id(1)
    tiles_per_seq = SEQ // FF_BM
    last = FF_P // FF_BN - 1
    valid = D_FF - last * FF_BN
    shift = FF_BN - valid

    @pl.when((i == 0) & (j < last))
    def _():
        wg_ref[...] = wgf_ref[...].astype(wg_ref.dtype)
        wv_ref[...] = wvf_ref[...].astype(wv_ref.dtype)

    @pl.when((i == 0) & (j == last))
    def _():
        zeros = jnp.zeros((wg_ref.shape[0], FF_BN - valid), wg_ref.dtype)
        wg_ref[:, :valid] = wgf_ref[:, :valid].astype(wg_ref.dtype)
        wv_ref[:, :valid] = wvf_ref[:, shift:].astype(wv_ref.dtype)
        wg_ref[:, valid:] = zeros
        wv_ref[:, valid:] = zeros

    for u in (ug, uv):
        @pl.when(i % tiles_per_seq == 0)
        def _():
            u[0:SUBLANES, :] = jnp.zeros((SUBLANES, FF_BN), jnp.float32)

        @pl.when(i % tiles_per_seq != 0)
        def _():
            u[0:SUBLANES, :] = u[FF_BM:FF_BM + SUBLANES, :]

    def conv(u, cw_ref, cb_ref, r0):
        base = SUBLANES + r0
        acc = cb_ref[...] + u[base - 2:base - 2 + FF_CH, :] * cw_ref[0:1, :]
        acc = acc + u[base - 1:base - 1 + FF_CH, :] * cw_ref[1:2, :]
        return acc + u[base:base + FF_CH, :] * cw_ref[2:3, :]

    for c in range(FF_BM // FF_CH):
        r0 = c * FF_CH
        hb = h_ref[r0:r0 + FF_CH, :]
        ug[SUBLANES + r0:SUBLANES + r0 + FF_CH, :] = jnp.dot(hb, wg_ref[...], preferred_element_type=jnp.float32)
        uv[SUBLANES + r0:SUBLANES + r0 + FF_CH, :] = jnp.dot(hb, wv_ref[...], preferred_element_type=jnp.float32)
        gate = conv(ug, cwg_ref, cbg_ref, r0)
        val = conv(uv, cwv_ref, cbv_ref, r0)
        o_ref[r0:r0 + FF_CH, :] = (jax.nn.gelu(gate) * val).astype(o_ref.dtype)


def _ffn_up(h, w_up, layer, conv_w, conv_b):
    t, d = h.shape
    nj = FF_P // FF_BN
    w_block = (None, pl.Element(d), pl.Element(FF_BN))
    return pl.pallas_call(
        _ffn_up_kernel,
        grid=(nj, t // FF_BM),
        in_specs=[
            pl.BlockSpec((FF_BM, d), lambda j, i: (i, 0)),
            pl.BlockSpec(w_block, lambda j, i: (layer, 0, pl.multiple_of(j * FF_BN, LANES))),
            pl.BlockSpec(w_block, lambda j, i: (
                layer, 0, pl.multiple_of(jnp.minimum(D_FF + j * FF_BN, 2 * D_FF - FF_BN), LANES))),
            pl.BlockSpec((CONV_W, FF_BN), lambda j, i: (0, j)),
            pl.BlockSpec((CONV_W, FF_BN), lambda j, i: (0, nj + j)),
            pl.BlockSpec((1, FF_BN), lambda j, i: (0, j)),
            pl.BlockSpec((1, FF_BN), lambda j, i: (0, nj + j)),
        ],
        out_specs=pl.BlockSpec((FF_BM, FF_BN), lambda j, i: (i, j)),
        out_shape=jax.ShapeDtypeStruct((t, FF_P), jnp.bfloat16),
        scratch_shapes=[pltpu.VMEM((d, FF_BN), jnp.bfloat16), pltpu.VMEM((d, FF_BN), jnp.bfloat16),
                        pltpu.VMEM((FF_BM + SUBLANES, FF_BN), jnp.float32),
                        pltpu.VMEM((FF_BM + SUBLANES, FF_BN), jnp.float32)],
        compiler_params=_cparams(("arbitrary", "arbitrary")),
        name="ffn_up",
    )(h, w_up, w_up, conv_w, conv_w, conv_b, conv_b)


def _pad_heads(w, axis):
    h = w.shape[0]
    zero = jnp.zeros_like(w)
    even = jnp.concatenate([w, zero], axis=axis)
    odd = jnp.concatenate([zero, w], axis=axis)
    sel = (jnp.arange(h) % 2 == 0).reshape((h, 1, 1))
    return jnp.where(sel, even, odd)


def _toeplitz(fn, rows, cols):
    ks = np.concatenate([np.arange(0, cols), np.arange(-(rows - 1), 0)])
    w = fn(ks)
    h, period = w.shape
    x = jnp.tile(w, (1, rows))[:, :rows * (period - 1)].reshape(h, rows, period - 1)
    return x[:, :, :cols].astype(jnp.float32)


def _band(rows, cols, left):
    diff = left + np.arange(rows)[:, None] // CHUNK - np.arange(cols)[None, :] // CHUNK
    return (diff >= 0) & (diff <= left)


def _bias_a(rel_bias):
    fn = lambda ks: rel_bias[np.clip(A_PAD - ks, -A_MAX_REL, A_MAX_REL) + A_MAX_REL].T
    return jnp.where(_band(QB, A_WIN, A_LEFT_CHUNKS)[None], _toeplitz(fn, QB, A_WIN), NEG)


def _bias_c(t5_c):
    fn = lambda ks: t5_c[_t5_bucket(jnp.asarray(ks - C_PAD, jnp.int32))].T
    bias = jnp.where(_band(QB, C_WIN, C_LEFT_CHUNKS)[None], _toeplitz(fn, QB, C_WIN), NEG)
    return bias.reshape(C_Q_HEADS * QB, C_WIN)


def _bias_b(t5_b):
    tiles = []
    for n in range(B_NEAR + 1):
        off = QB * n if n < B_NEAR else SEQ
        fn = lambda ks, off=off: t5_b[_t5_bucket(jnp.asarray(-ks - off, jnp.int32))].T
        tile = _toeplitz(fn, KB, QB)
        tiles.append(jnp.transpose(tile, (1, 0, 2)).reshape(KB, HQ))
    return jnp.stack(tiles)


def _pad_ff(a, dtype):
    z = jnp.zeros((a.shape[0], FF_P - D_FF), dtype)
    return jnp.concatenate([a[:, :D_FF].astype(dtype), z, a[:, D_FF:].astype(dtype), z], axis=1)


def kernel(x, mem, t5_table, norm_gains, w_in, a_rel_bias, ckv_gain, w_uk, w_uv, sinks, w_branch, w_o,
           mem_gain, w_mq, w_mkv, w_mo, w_up, conv_w, conv_b, w_down):
    bf16 = jnp.bfloat16
    xs = x.reshape(TOKENS, D_MODEL)
    mem2 = mem.reshape(BATCH * MEM_LEN, D_MODEL)
    tri = jnp.asarray(np.tril(np.ones((KB, KB), np.float32), -1), bf16)
    bias_b = _bias_b(t5_table[:, :B_HEADS])
    bias_c = _bias_c(t5_table[:, B_HEADS:])
    gains = norm_gains.reshape(DEPTH, 6, 1, D_MODEL)
    w_o_b, w_mq_b, w_mkv_b, w_mo_b = (w.astype(bf16) for w in (w_o, w_mq, w_mkv, w_mo))
    w_dn_b = jnp.pad(w_down.astype(bf16), ((0, 0), (0, FF_P - D_FF), (0, 0)))
    w_in_t = jnp.swapaxes(w_in, 1, 2)

    h = _norm(xs, gains[0, 0])
    for l in range(DEPTH):
        g = gains[l]
        proja = _proj(h, w_in_t, l, 0, IN_A, 1024, gate=False, name="in_proj_a")
        projc = _proj(h, w_in_t, l, O_QC, C_COLS, C_COLS, gate=False, name="in_proj_c")
        gates = _proj(h, w_in_t, l, O_GL, 3 * D_MODEL, 1024, gate=True, name="in_proj_g")
        proja3 = proja.reshape(BATCH, SEQ, IN_A)
        oa = _attn_a(proja3, _bias_a(a_rel_bias[l]))
        wuk = jnp.transpose(_pad_heads(w_uk[l], axis=2), (0, 2, 1)).astype(bf16)
        wuv = _pad_heads(w_uv[l], axis=2).astype(bf16)
        ob = _attn_b(proja3, ckv_gain[l].reshape(1, B_LATENT), wuk, wuv, bias_b, tri)
        oc = _attn_c(sinks[l], projc.reshape(BATCH, SEQ, C_COLS), bias_c)
        merged = _merge(oa.reshape(TOKENS, A_W), ob.reshape(TOKENS, B_W), oc.reshape(TOKENS, C_W), gates, w_branch, l)
        xs, h = _gemm_res(merged, w_o_b, l, xs, g[1], g[2], bk=D_MODEL, name="out_proj")
        kv = _memkv(mem2, mem_gain[l].reshape(1, D_MODEL), w_mkv_b, l)
        xs, h = _xattn(h, w_mq_b, kv, w_mo_b, l, xs, g[3], g[4])
        hidden = _ffn_up(h, w_up, l, _pad_ff(conv_w[l], jnp.float32), _pad_ff(conv_b[l].reshape(1, -1), jnp.float32))
        g_next = gains[l + 1, 0] if l + 1 < DEPTH else None
        xs, h = _gemm_res(hidden, w_dn_b, l, xs, g[5], g_next, bk=DOWN_BK, name="ffn_down")
    return xs.reshape(BATCH, SEQ, D_MODEL)
```

```python
import functools
import math

import numpy as np
import jax
import jax.numpy as jnp
from jax import lax
from jax.experimental import pallas as pl
from jax.experimental.pallas import tpu as pltpu

D_MODEL = 2048
BATCH = 4
SEQ = 2048
DEPTH = 2
TOKENS = BATCH * SEQ
CHUNK = 64
EPS = 1e-6
NEG = -1e30
A_HEADS = 8
A_LEFT_CHUNKS = 8
A_MAX_REL = 128
A_W = 512
B_HEADS = 8
B_W = 512
B_LATENT = 128
IDX_HEADS = 8
IDX_DIM = 64
TOPK = 256
C_Q_HEADS = 16
C_GROUP = 8
C_W = 1024
C_LEFT_CHUNKS = 2
T5_BUCKETS = 32
T5_MAX_DIST = 256
MEM_LEN = 256
MEM_HEADS = 4
MEM_HEAD_DIM = 128
MEM_W = 512
D_FF = 5504
CONV_W = 3

LANES = 128
SUBLANES = 8
HEAD_DIM = 64
QB = 128
KB = 256
VMEM_LIMIT = 56 * 1024 * 1024

O_QA, O_KA, O_VA, O_QB = 0, 512, 1024, 1536
O_CKV = 2048
O_QI = 2176
O_KI = 2688
O_WI = 2752
O_QC = 2760
O_GL = 4040
IN_W = O_GL + 3 * D_MODEL
IN_A = 3072
X_COL0 = 2048
C_COLS = O_GL - O_QC
PROJ_BM = 1024
PROJ_TR = 256

FF_P = 5632
FF_BN = 512
FF_BM = 1024
FF_CH = 256
DOWN_BK = 1408

A_WIN = (A_LEFT_CHUNKS + 2) * CHUNK
C_WIN = (C_LEFT_CHUNKS + 2) * CHUNK
A_PAD = A_LEFT_CHUNKS * CHUNK
C_PAD = C_LEFT_CHUNKS * CHUNK
B_NEAR = 4
NKB = SEQ // KB
HQ = B_HEADS * QB

_NT = (((1,), (1,)), ((), ()))
_TN = (((0,), (0,)), ((), ()))


def _cparams(sem):
    return pltpu.CompilerParams(dimension_semantics=sem, vmem_limit_bytes=VMEM_LIMIT)


def _t5_bucket(rel):
    half = T5_BUCKETS // 2
    max_exact = half // 2
    sign = jnp.where(rel > 0, half, 0)
    d = jnp.abs(rel)
    d_f = jnp.maximum(d, 1).astype(jnp.float32)
    large = max_exact + (jnp.log(d_f / max_exact) / math.log(T5_MAX_DIST / max_exact) * (half - max_exact)).astype(jnp.int32)
    large = jnp.minimum(large, half - 1)
    return sign + jnp.where(d < max_exact, d, large)


def _far_bucket_is_constant():
    d = np.arange(QB * B_NEAR - (KB - 1), SEQ, dtype=np.float32)
    assert d[0] > T5_MAX_DIST
    large = 8 + (np.log(d / 8) / math.log(T5_MAX_DIST / 8) * 8).astype(np.int32)
    return bool(np.all(np.minimum(large, 15) == 15))


assert _far_bucket_is_constant()


def _rms(v, g):
    return v * lax.rsqrt(jnp.mean(v * v, axis=-1, keepdims=True) + EPS) * g


def _norm_kernel(x_ref, g_ref, o_ref):
    o_ref[...] = _rms(x_ref[...], g_ref[...]).astype(o_ref.dtype)


def _norm(x, g, bm=1024):
    t, d = x.shape
    return pl.pallas_call(
        _norm_kernel,
        grid=(t // bm,),
        in_specs=[pl.BlockSpec((bm, d), lambda i: (i, 0)), pl.BlockSpec((1, d), lambda i: (0, 0))],
        out_specs=pl.BlockSpec((bm, d), lambda i: (i, 0)),
        out_shape=jax.ShapeDtypeStruct((t, d), jnp.bfloat16),
        compiler_params=_cparams(("parallel",)),
        name="rmsnorm",
    )(x, g)


def _proj_kernel(gate, h_ref, wt_ref, o_ref, wb):
    @pl.when(pl.program_id(1) == 0)
    def _():
        for c in range(wt_ref.shape[0] // PROJ_TR):
            rows = slice(c * PROJ_TR, (c + 1) * PROJ_TR)
            wb[:, rows] = wt_ref[rows, :].T.astype(wb.dtype)

    acc = jnp.dot(h_ref[...], wb[...], preferred_element_type=jnp.float32)
    o_ref[...] = (jax.nn.sigmoid(acc) if gate else acc).astype(o_ref.dtype)


def _proj(h, w_t, layer, col0, n, bn, gate, name):
    t, k = h.shape
    assert col0 % SUBLANES == 0 and bn % SUBLANES == 0
    w_spec = pl.BlockSpec((None, pl.Element(bn), pl.Element(k)),
                          lambda j, i: (layer, pl.multiple_of(col0 + j * bn, SUBLANES), 0))
    return pl.pallas_call(
        functools.partial(_proj_kernel, gate),
        grid=(n // bn, t // PROJ_BM),
        in_specs=[pl.BlockSpec((PROJ_BM, k), lambda j, i: (i, 0)), w_spec],
        out_specs=pl.BlockSpec((PROJ_BM, bn), lambda j, i: (i, j)),
        out_shape=jax.ShapeDtypeStruct((t, n), jnp.bfloat16),
        scratch_shapes=[pltpu.VMEM((k, bn), jnp.bfloat16)],
        compiler_params=_cparams(("arbitrary", "arbitrary")),
        name=name,
    )(h, w_t)


def _lane_masks(dtype):
    lane = lax.broadcasted_iota(jnp.int32, (1, LANES), 1)
    lo = (lane < HEAD_DIM).astype(jnp.float32)
    return lo.astype(dtype), (1.0 - lo).astype(dtype)


def _swap_halves(x):
    return pltpu.roll(x.astype(jnp.float32), HEAD_DIM, 1).astype(x.dtype)


def _fill_padded(src, dst_ref, pad):
    dst_ref[0:pad, :] = jnp.zeros((pad, dst_ref.shape[1]), dst_ref.dtype)
    dst_ref[pad:pad + SEQ, :] = src


def _attn_a_kernel(q_ref, k_ref, v_ref, bias_ref, o_ref, kpad, vpad):
    i = pl.program_id(1)

    @pl.when(i == 0)
    def _():
        _fill_padded(k_ref[...], kpad, A_PAD)
        _fill_padded(v_ref[...], vpad, A_PAD)

    start = pl.multiple_of(i * QB, QB)
    kw = kpad[pl.ds(start, A_WIN), :]
    vw = vpad[pl.ds(start, A_WIN), :]
    mlo, mhi = _lane_masks(jnp.bfloat16)
    lane = lax.broadcasted_iota(jnp.int32, (QB, LANES), 1)
    kchunk = lax.broadcasted_iota(jnp.int32, (QB, A_WIN), 1) // CHUNK
    pad_mask = jnp.where(kchunk >= A_LEFT_CHUNKS - 2 * i, 0.0, NEG)
    scale = HEAD_DIM ** -0.5
    pad2 = jnp.concatenate([pad_mask, pad_mask], axis=0)
    for p in range(A_HEADS // 2):
        cols = slice(p * LANES, (p + 1) * LANES)
        qp = q_ref[:, cols]
        q2 = jnp.concatenate([qp * mlo, qp * mhi], axis=0)
        s = lax.dot_general(q2, kw[:, cols], _NT, preferred_element_type=jnp.float32)
        s = s * scale + bias_ref[2 * p * QB:(2 * p + 2) * QB, :] + pad2
        m = jnp.max(s, axis=-1, keepdims=True)
        e = jnp.exp(s - m)
        l = jnp.sum(e, axis=-1, keepdims=True)
        o = jnp.dot(e.astype(jnp.bfloat16), vw[:, cols], preferred_element_type=jnp.float32) / l
        o_ref[:, cols] = jnp.where(lane < HEAD_DIM, o[:QB], o[QB:]).astype(o_ref.dtype)


def _attn_a(proj3, bias):
    nq = SEQ // QB
    return pl.pallas_call(
        _attn_a_kernel,
        grid=(BATCH, nq),
        in_specs=[
            pl.BlockSpec((None, QB, A_W), lambda b, i: (b, i, O_QA // A_W)),
            pl.BlockSpec((None, SEQ, A_W), lambda b, i: (b, 0, O_KA // A_W)),
            pl.BlockSpec((None, SEQ, A_W), lambda b, i: (b, 0, O_VA // A_W)),
            pl.BlockSpec((A_HEADS * QB, A_WIN), lambda b, i: (0, 0)),
        ],
        out_specs=pl.BlockSpec((None, QB, A_W), lambda b, i: (b, i, 0)),
        out_shape=jax.ShapeDtypeStruct((BATCH, SEQ, A_W), jnp.bfloat16),
        scratch_shapes=[pltpu.VMEM((SEQ + A_PAD, A_W), jnp.bfloat16), pltpu.VMEM((SEQ + A_PAD, A_W), jnp.bfloat16)],
        compiler_params=_cparams(("arbitrary", "arbitrary")),
        name="attn_a",
    )(proj3, proj3, proj3, bias)


def _attn_c_kernel(sink_ref, q_ref, k_ref, v_ref, bias_ref, o_ref, kpad, kswp, vpad, vswp):
    i = pl.program_id(1)

    @pl.when(i == 0)
    def _():
        k = k_ref[...]
        v = v_ref[...]
        _fill_padded(k, kpad, C_PAD)
        _fill_padded(_swap_halves(k), kswp, C_PAD)
        _fill_padded(v, vpad, C_PAD)
        _fill_padded(_swap_halves(v), vswp, C_PAD)

    start = pl.multiple_of(i * QB, QB)
    mlo, mhi = _lane_masks(jnp.bfloat16)
    lane = lax.broadcasted_iota(jnp.int32, (QB, LANES), 1)
    kchunk = lax.broadcasted_iota(jnp.int32, (QB, C_WIN), 1) // CHUNK
    pad_mask = jnp.where(kchunk >= C_LEFT_CHUNKS - 2 * i, 0.0, NEG)
    scale = HEAD_DIM ** -0.5
    qs = [q_ref[:, p * LANES:(p + 1) * LANES] for p in range(C_GROUP)]
    npair = C_GROUP // 2
    stacks = []
    for straight in (True, False):
        kref, vref = (kpad, vpad) if straight else (kswp, vswp)
        kw = kref[pl.ds(start, C_WIN), :]
        vw = vref[pl.ds(start, C_WIN), :]
        halves = [int((p >= npair) == straight) for p in range(C_GROUP)]
        qg = jnp.concatenate([qs[p] * (mhi if halves[p] else mlo) for p in range(C_GROUP)], axis=0)
        s_all = lax.dot_general(qg, kw, _NT, preferred_element_type=jnp.float32)
        ps, ls = [], []
        for p in range(C_GROUP):
            h = 2 * p + halves[p]
            s = s_all[p * QB:(p + 1) * QB] * scale + bias_ref[h * QB:(h + 1) * QB, :] + pad_mask
            sink = sink_ref[h]
            m = jnp.maximum(jnp.max(s, axis=-1, keepdims=True), sink)
            e = jnp.exp(s - m)
            ls.append(jnp.sum(e, axis=-1, keepdims=True) + jnp.exp(sink - m))
            ps.append(e.astype(jnp.bfloat16))
        o_all = jnp.dot(jnp.concatenate(ps, axis=0), vw, preferred_element_type=jnp.float32)
        stacks.append(([o_all[p * QB:(p + 1) * QB] / ls[p] for p in range(C_GROUP)], halves))
    for p in range(C_GROUP):
        (o1, h1), (o2, _) = stacks
        lo, hi = (o2[p], o1[p]) if h1[p] else (o1[p], o2[p])
        o_ref[:, p * LANES:(p + 1) * LANES] = jnp.where(lane < HEAD_DIM, lo, hi).astype(o_ref.dtype)


def _attn_c(sinks, projc3, bias):
    nq = SEQ // QB
    pad_buf = pltpu.VMEM((SEQ + C_PAD, LANES), jnp.bfloat16)
    return pl.pallas_call(
        _attn_c_kernel,
        grid=(BATCH, nq),
        in_specs=[
            pl.BlockSpec(memory_space=pltpu.SMEM),
            pl.BlockSpec((None, QB, C_W), lambda b, i: (b, i, 0)),
            pl.BlockSpec((None, SEQ, LANES), lambda b, i: (b, 0, C_W // LANES)),
            pl.BlockSpec((None, SEQ, LANES), lambda b, i: (b, 0, C_W // LANES + 1)),
            pl.BlockSpec((C_Q_HEADS * QB, C_WIN), lambda b, i: (0, 0)),
        ],
        out_specs=pl.BlockSpec((None, QB, C_W), lambda b, i: (b, i, 0)),
        out_shape=jax.ShapeDtypeStruct((BATCH, SEQ, C_W), jnp.bfloat16),
        scratch_shapes=[pad_buf, pad_buf, pad_buf, pad_buf],
        compiler_params=_cparams(("arbitrary", "arbitrary")),
        name="attn_c",
    )(sinks, projc3, projc3, projc3, bias)


def _sort_key(x):
    bits = lax.bitcast_convert_type(x + 0.0, jnp.int32)
    return bits ^ ((bits >> 31) & jnp.int32(0x7FFFFFFF))


def _rows8(x, op):
    parts = [x[r:r + SUBLANES, :] for r in range(0, x.shape[0], SUBLANES)]
    while len(parts) > 1:
        nxt = [op(parts[k], parts[k + 1]) for k in range(0, len(parts) - 1, 2)]
        parts = nxt + ([parts[-1]] if len(parts) % 2 else [])
    return parts[0]


def _attn_b_kernel(qb_ref, x_ref, kiw_ref, ckv_ref, gain_ref, wuk_ref, wuv_ref, bias_ref, tri_ref,
                   o_ref, ckvn, kd, ql, qi_all, sk, acc_s, p_s):
    i = pl.program_id(1)
    nkb = i // 2 + 1
    mlo, mhi = _lane_masks(jnp.bfloat16)
    f32 = jnp.float32

    @pl.when(i == 0)
    def _():
        ckvn[...] = _rms(ckv_ref[...].astype(f32), gain_ref[...]).astype(ckvn.dtype)
        kiw = kiw_ref[...].astype(f32)
        lane = lax.broadcasted_iota(jnp.int32, kiw.shape, 1)
        kd[...] = jnp.where(lane < HEAD_DIM, kiw, pltpu.roll(kiw, HEAD_DIM, 1)).astype(kd.dtype)

    for h in range(B_HEADS):
        rows = slice(h * QB, (h + 1) * QB)
        qlat = jnp.dot(qb_ref[:, (h // 2) * LANES:(h // 2 + 1) * LANES], wuk_ref[h], preferred_element_type=f32)
        ql[rows, :] = (qlat * HEAD_DIM ** -0.5).astype(ql.dtype)
        qcol = O_QI - X_COL0 + (h // 2) * LANES
        qi_all[rows, :] = x_ref[:, qcol:qcol + LANES] * (mhi if h % 2 else mlo)
    wcol = O_KI - X_COL0
    wi_t = x_ref[:, wcol:wcol + LANES].astype(f32).T * (IDX_HEADS ** -0.5 * IDX_DIM ** -0.5)
    wi_rows = [wi_t[O_WI - O_KI + h:O_WI - O_KI + h + 1, :] for h in range(IDX_HEADS)]

    kpos = lax.broadcasted_iota(jnp.int32, (KB, QB), 0)
    key_limit = ((i * QB + lax.broadcasted_iota(jnp.int32, (1, QB), 1)) // CHUNK + 1) * CHUNK

    def admissible(kb):
        return kb * KB + kpos < key_limit

    def score_block(kb, carry):
        kblk = kd[pl.ds(pl.multiple_of(kb * KB, KB), KB), :]
        dots = lax.dot_general(kblk, qi_all[...], _NT, preferred_element_type=f32)
        score = jnp.zeros((KB, QB), f32)
        for h in range(IDX_HEADS):
            score = score + jnp.maximum(dots[:, h * QB:(h + 1) * QB], 0.0) * wi_rows[h]
        sk[kb] = _sort_key(jnp.where(admissible(kb), score, NEG))
        return carry

    lax.fori_loop(0, nkb, score_block, 0)

    neg_key = _sort_key(jnp.full((1, 1), NEG, f32))
    n_rest = ((NKB - nkb) * KB).astype(f32)

    def count(pred_fn, v):
        def body(kb, acc):
            return acc + _rows8(jnp.where(pred_fn(sk[kb], v), 1.0, 0.0), jnp.add)

        acc = lax.fori_loop(0, nkb, body, jnp.zeros((SUBLANES, QB), f32))
        return jnp.sum(acc, axis=0, keepdims=True) + jnp.where(pred_fn(neg_key, v), n_rest, 0.0)

    ge = lambda a, v: a >= v
    int_min = jnp.int32(-2 ** 31)
    c0 = count(ge, jnp.zeros((1, QB), jnp.int32))
    v0 = jnp.where(c0 >= TOPK, jnp.int32(0), int_min)

    def bit_step(t, carry):
        v, above = carry
        cand = v | (jnp.int32(1) << (30 - t))
        c = count(ge, cand)
        keep = c >= TOPK
        return jnp.where(keep, cand, v), jnp.where(keep, above, c)

    thr, above = lax.fori_loop(0, 31, bit_step, (v0, jnp.where(c0 >= TOPK, 0.0, c0)))
    need = TOPK - above

    acc_s[...] = jnp.zeros(acc_s.shape, f32)

    def attend(kb, carry):
        eq_seen, m_old, l_old = carry
        keys = sk[kb]
        eq = keys == thr
        eq_f = jnp.where(eq, 1.0, 0.0)
        before = jnp.dot(tri_ref[...], eq_f.astype(jnp.bfloat16), preferred_element_type=f32)
        take_eq = jnp.where(eq, jnp.where(eq_seen + before < need, 1.0, 0.0), 0.0)
        take = jnp.where(keys > thr, 1.0, take_eq)
        mask_add = jnp.where(admissible(kb), jnp.where(take > 0.0, 0.0, NEG), NEG)
        cblk = ckvn[pl.ds(pl.multiple_of(kb * KB, KB), KB), :]
        dots = lax.dot_general(cblk, ql[...], _NT, preferred_element_type=f32)
        tile = jnp.minimum(i - 2 * kb, B_NEAR)
        m_parts, l_parts, a_parts = [], [], []
        for h in range(B_HEADS):
            cols = slice(h * QB, (h + 1) * QB)
            s = dots[:, cols] + bias_ref[tile, :, cols] + mask_add
            mo = m_old[:, cols]
            mn = jnp.maximum(mo, jnp.max(_rows8(s, jnp.maximum), axis=0, keepdims=True))
            a = jnp.exp(mo - mn)
            e = jnp.exp(s - mn)
            l_parts.append(a * l_old[:, cols] + jnp.sum(_rows8(e, jnp.add), axis=0, keepdims=True))
            m_parts.append(mn)
            a_parts.append(a)
            p_s[:, cols] = e.astype(p_s.dtype)
        alpha = jnp.concatenate(a_parts, axis=1)
        acc_s[...] = acc_s[...] * alpha + lax.dot_general(cblk, p_s[...], _TN, preferred_element_type=f32)
        eq_seen = eq_seen + jnp.sum(_rows8(eq_f, jnp.add), axis=0, keepdims=True)
        return eq_seen, jnp.concatenate(m_parts, axis=1), jnp.concatenate(l_parts, axis=1)

    init = (jnp.zeros((1, QB), f32), jnp.full((1, HQ), 4 * NEG, f32), jnp.zeros((1, HQ), f32))
    _, _, l_fin = lax.fori_loop(0, nkb, attend, init)

    o_lat_t = acc_s[...] / l_fin
    for p in range(B_HEADS // 2):
        out = jnp.zeros((QB, LANES), f32)
        for h in (2 * p, 2 * p + 1):
            o_lat = o_lat_t[:, h * QB:(h + 1) * QB].T.astype(jnp.bfloat16)
            out = out + jnp.dot(o_lat, wuv_ref[h], preferred_element_type=f32)
        o_ref[:, p * LANES:(p + 1) * LANES] = out.astype(o_ref.dtype)


def _attn_b(proj3, gain, wuk, wuv, bias, tri, nbatch=BATCH):
    nq = SEQ // QB
    xw = IN_A - X_COL0
    return pl.pallas_call(
        _attn_b_kernel,
        grid=(nbatch, nq),
        in_specs=[
            pl.BlockSpec((None, QB, B_W), lambda b, i: (b, i, O_QB // B_W)),
            pl.BlockSpec((None, QB, xw), lambda b, i: (b, i, X_COL0 // xw)),
            pl.BlockSpec((None, SEQ, LANES), lambda b, i: (b, 0, O_KI // LANES)),
            pl.BlockSpec((None, SEQ, LANES), lambda b, i: (b, 0, O_CKV // LANES)),
            pl.BlockSpec((1, B_LATENT), lambda b, i: (0, 0)),
            pl.BlockSpec((B_HEADS, LANES, B_LATENT), lambda b, i: (0, 0, 0)),
            pl.BlockSpec((B_HEADS, B_LATENT, LANES), lambda b, i: (0, 0, 0)),
            pl.BlockSpec((B_NEAR + 1, KB, HQ), lambda b, i: (0, 0, 0)),
            pl.BlockSpec((KB, KB), lambda b, i: (0, 0)),
        ],
        out_specs=pl.BlockSpec((None, QB, B_W), lambda b, i: (b, i, 0)),
        out_shape=jax.ShapeDtypeStruct((nbatch, SEQ, B_W), jnp.bfloat16),
        scratch_shapes=[
            pltpu.VMEM((SEQ, B_LATENT), jnp.bfloat16),
            pltpu.VMEM((SEQ, LANES), jnp.bfloat16),
            pltpu.VMEM((HQ, B_LATENT), jnp.bfloat16),
            pltpu.VMEM((HQ, LANES), jnp.bfloat16),
            pltpu.VMEM((NKB, KB, QB), jnp.int32),
            pltpu.VMEM((B_LATENT, HQ), jnp.float32),
            pltpu.VMEM((KB, HQ), jnp.bfloat16),
        ],
        compiler_params=_cparams(("arbitrary", "arbitrary")),
        name="attn_b",
    )(proj3, proj3, proj3, proj3, gain, wuk, wuv, bias, tri)


MERGE_BN = 512
MERGE_BM = 1024


def _merge_kernel(oa_ref, ob_ref, oc_ref, ga_ref, gb_ref, gc_ref, wa_ref, wb_ref, wc_ref, o_ref, wa, wb, wc):
    f32 = jnp.float32

    @pl.when(pl.program_id(1) == 0)
    def _():
        wa[...] = wa_ref[...].astype(wa.dtype)
        wb[...] = wb_ref[...].astype(wb.dtype)
        wc[...] = wc_ref[...].astype(wc.dtype)

    m = ga_ref[...].astype(f32) * jnp.dot(oa_ref[...], wa[...], preferred_element_type=f32)
    m = m + gb_ref[...].astype(f32) * jnp.dot(ob_ref[...], wb[...], preferred_element_type=f32)
    m = m + gc_ref[...].astype(f32) * jnp.dot(oc_ref[...], wc[...], preferred_element_type=f32)
    o_ref[...] = m.astype(o_ref.dtype)


def _merge(oa, ob, oc, gates, w_branch, layer):
    t = oa.shape[0]
    bn, bm = MERGE_BN, MERGE_BM
    gstep = D_MODEL // bn
    return pl.pallas_call(
        _merge_kernel,
        grid=(D_MODEL // bn, t // bm),
        in_specs=[
            pl.BlockSpec((bm, A_W), lambda j, i: (i, 0)),
            pl.BlockSpec((bm, B_W), lambda j, i: (i, 0)),
            pl.BlockSpec((bm, C_W), lambda j, i: (i, 0)),
            pl.BlockSpec((bm, bn), lambda j, i: (i, j)),
            pl.BlockSpec((bm, bn), lambda j, i: (i, gstep + j)),
            pl.BlockSpec((bm, bn), lambda j, i: (i, 2 * gstep + j)),
            pl.BlockSpec((None, A_W, bn), lambda j, i: (layer, 0, j)),
            pl.BlockSpec((None, B_W, bn), lambda j, i: (layer, A_W // B_W, j)),
            pl.BlockSpec((None, C_W, bn), lambda j, i: (layer, (A_W + B_W) // C_W, j)),
        ],
        out_specs=pl.BlockSpec((bm, bn), lambda j, i: (i, j)),
        out_shape=jax.ShapeDtypeStruct((t, D_MODEL), jnp.bfloat16),
        scratch_shapes=[pltpu.VMEM((A_W, bn), jnp.bfloat16), pltpu.VMEM((B_W, bn), jnp.bfloat16),
                        pltpu.VMEM((C_W, bn), jnp.bfloat16)],
        compiler_params=_cparams(("arbitrary", "arbitrary")),
        name="merge",
    )(oa, ob, oc, gates, gates, gates, w_branch, w_branch, w_branch)


def _finish(y, x_ref, gp_ref, gn_ref, xo_ref, ho_ref):
    xn = x_ref[...] + _rms(y, gp_ref[...])
    xo_ref[...] = xn
    if ho_ref is not None:
        ho_ref[...] = _rms(xn, gn_ref[...]).astype(ho_ref.dtype)


def _gemm_res_kernel(nk, with_next, a_ref, w_ref, x_ref, gp_ref, gn_ref, xo_ref, *rest):
    ho_ref = rest[0] if with_next else None
    if nk == 1:
        y = jnp.dot(a_ref[...], w_ref[...], preferred_element_type=jnp.float32)
        _finish(y, x_ref, gp_ref, gn_ref, xo_ref, ho_ref)
        return
    acc = rest[-1]
    k = pl.program_id(1)

    @pl.when(k == 0)
    def _():
        acc[...] = jnp.zeros(acc.shape, jnp.float32)

    acc[...] += jnp.dot(a_ref[...], w_ref[...], preferred_element_type=jnp.float32)

    @pl.when(k == nk - 1)
    def _():
        _finish(acc[...], x_ref, gp_ref, gn_ref, xo_ref, ho_ref)


def _gemm_res(a, w, layer, x, g_post, g_next, bk, bm=512, name="gemm_res"):
    t, kdim = a.shape
    n = w.shape[2]
    nk = kdim // bk
    with_next = g_next is not None
    if g_next is None:
        g_next = g_post
    out_shape = [jax.ShapeDtypeStruct((t, n), jnp.float32)]
    out_specs = [pl.BlockSpec((bm, n), lambda i, k: (i, 0))]
    if with_next:
        out_shape.append(jax.ShapeDtypeStruct((t, n), jnp.bfloat16))
        out_specs.append(pl.BlockSpec((bm, n), lambda i, k: (i, 0)))
    res = pl.pallas_call(
        functools.partial(_gemm_res_kernel, nk, with_next),
        grid=(t // bm, nk),
        in_specs=[
            pl.BlockSpec((bm, bk), lambda i, k: (i, k)),
            pl.BlockSpec((None, bk, n), lambda i, k: (layer, k, 0)),
            pl.BlockSpec((bm, n), lambda i, k: (i, 0)),
            pl.BlockSpec((1, n), lambda i, k: (0, 0)),
            pl.BlockSpec((1, n), lambda i, k: (0, 0)),
        ],
        out_specs=out_specs,
        out_shape=out_shape,
        scratch_shapes=[pltpu.VMEM((bm, n), jnp.float32)] if nk > 1 else [],
        compiler_params=_cparams(("parallel", "arbitrary")),
        name=name,
    )(a, w, x, g_post, g_next)
    return (res[0], res[1]) if with_next else (res[0], None)


def _memkv_kernel(m_ref, g_ref, w_ref, o_ref):
    mn = _rms(m_ref[...], g_ref[...]).astype(jnp.bfloat16)
    o_ref[...] = jnp.dot(mn, w_ref[...], preferred_element_type=jnp.float32).astype(o_ref.dtype)


def _memkv(mem2, g, w, layer):
    t, d = mem2.shape
    n = w.shape[2]
    bm = 512
    return pl.pallas_call(
        _memkv_kernel,
        grid=(t // bm,),
        in_specs=[pl.BlockSpec((bm, d), lambda i: (i, 0)), pl.BlockSpec((1, d), lambda i: (0, 0)),
                  pl.BlockSpec((None, d, n), lambda i: (layer, 0, 0))],
        out_specs=pl.BlockSpec((bm, n), lambda i: (i, 0)),
        out_shape=jax.ShapeDtypeStruct((t, n), jnp.bfloat16),
        compiler_params=_cparams(("parallel",)),
        name="mem_kv",
    )(mem2, g, w)


XA_BM = 512


def _xattn_kernel(h_ref, wq_ref, kv_ref, wo_ref, x_ref, gp_ref, gn_ref, xo_ref, ho_ref):
    f32 = jnp.float32
    q = jnp.dot(h_ref[...], wq_ref[...], preferred_element_type=f32).astype(jnp.bfloat16)
    scale = MEM_HEAD_DIM ** -0.5
    outs = []
    for h in range(MEM_HEADS):
        cols = slice(h * LANES, (h + 1) * LANES)
        kh = kv_ref[:, cols]
        vh = kv_ref[:, MEM_W + h * LANES:MEM_W + (h + 1) * LANES]
        s = lax.dot_general(q[:, cols], kh, _NT, preferred_element_type=f32) * scale
        m = jnp.max(s, axis=-1, keepdims=True)
        e = jnp.exp(s - m)
        l = jnp.sum(e, axis=-1, keepdims=True)
        o = jnp.dot(e.astype(jnp.bfloat16), vh, preferred_element_type=f32)
        outs.append((o / l).astype(jnp.bfloat16))
    o = jnp.concatenate(outs, axis=-1)
    y = jnp.dot(o, wo_ref[...], preferred_element_type=f32)
    _finish(y, x_ref, gp_ref, gn_ref, xo_ref, ho_ref)


def _xattn(h, wq, kv, wo, layer, x, g_post, g_next):
    t, d = h.shape
    bm = XA_BM
    per_batch = SEQ // bm
    return pl.pallas_call(
        _xattn_kernel,
        grid=(t // bm,),
        in_specs=[
            pl.BlockSpec((bm, d), lambda i: (i, 0)),
            pl.BlockSpec((None, d, MEM_W), lambda i: (layer, 0, 0)),
            pl.BlockSpec((MEM_LEN, 2 * MEM_W), lambda i: (i // per_batch, 0)),
            pl.BlockSpec((None, MEM_W, d), lambda i: (layer, 0, 0)),
            pl.BlockSpec((bm, d), lambda i: (i, 0)),
            pl.BlockSpec((1, d), lambda i: (0, 0)),
            pl.BlockSpec((1, d), lambda i: (0, 0)),
        ],
        out_specs=[pl.BlockSpec((bm, d), lambda i: (i, 0)), pl.BlockSpec((bm, d), lambda i: (i, 0))],
        out_shape=[jax.ShapeDtypeStruct((t, d), jnp.float32), jax.ShapeDtypeStruct((t, d), jnp.bfloat16)],
        compiler_params=_cparams(("parallel",)),
        name="mem_xattn",
    )(h, wq, kv, wo, x, g_post, g_next)


def _ffn_up_kernel(h_ref, wgf_ref, wvf_ref, cwg_ref, cwv_ref, cbg_ref, cbv_ref, o_ref, wg_ref, wv_ref, ug, uv):
    j = pl.program_id(0)
    i = pl.program_id(1)
    tiles_per_seq = SEQ // FF_BM
    last = FF_P // FF_BN - 1
    valid = D_FF - last * FF_BN
    shift = FF_BN - valid

    @pl.when((i == 0) & (j < last))
    def _():
        wg_ref[...] = wgf_ref[...].astype(wg_ref.dtype)
        wv_ref[...] = wvf_ref[...].astype(wv_ref.dtype)

    @pl.when((i == 0) & (j == last))
    def _():
        zeros = jnp.zeros((wg_ref.shape[0], FF_BN - valid), wg_ref.dtype)
        wg_ref[:, :valid] = wgf_ref[:, :valid].astype(wg_ref.dtype)
        wv_ref[:, :valid] = wvf_ref[:, shift:].astype(wv_ref.dtype)
        wg_ref[:, valid:] = zeros
        wv_ref[:, valid:] = zeros

    for u in (ug, uv):
        @pl.when(i % tiles_per_seq == 0)
        def _():
            u[0:SUBLANES, :] = jnp.zeros((SUBLANES, FF_BN), jnp.float32)

        @pl.when(i % tiles_per_seq != 0)
        def _():
            u[0:SUBLANES, :] = u[FF_BM:FF_BM + SUBLANES, :]

    def conv(u, cw_ref, cb_ref, r0):
        base = SUBLANES + r0
        acc = cb_ref[...] + u[base - 2:base - 2 + FF_CH, :] * cw_ref[0:1, :]
        acc = acc + u[base - 1:base - 1 + FF_CH, :] * cw_ref[1:2, :]
        return acc + u[base:base + FF_CH, :] * cw_ref[2:3, :]

    for c in range(FF_BM // FF_CH):
        r0 = c * FF_CH
        hb = h_ref[r0:r0 + FF_CH, :]
        ug[SUBLANES + r0:SUBLANES + r0 + FF_CH, :] = jnp.dot(hb, wg_ref[...], preferred_element_type=jnp.float32)
        uv[SUBLANES + r0:SUBLANES + r0 + FF_CH, :] = jnp.dot(hb, wv_ref[...], preferred_element_type=jnp.float32)
        gate = conv(ug, cwg_ref, cbg_ref, r0)
        val = conv(uv, cwv_ref, cbv_ref, r0)
        o_ref[r0:r0 + FF_CH, :] = (jax.nn.gelu(gate) * val).astype(o_ref.dtype)


def _ffn_up(h, w_up, layer, conv_w, conv_b):
    t, d = h.shape
    nj = FF_P // FF_BN
    w_block = (None, pl.Element(d), pl.Element(FF_BN))
    return pl.pallas_call(
        _ffn_up_kernel,
        grid=(nj, t // FF_BM),
        in_specs=[
            pl.BlockSpec((FF_BM, d), lambda j, i: (i, 0)),
            pl.BlockSpec(w_block, lambda j, i: (layer, 0, pl.multiple_of(j * FF_BN, LANES))),
            pl.BlockSpec(w_block, lambda j, i: (
                layer, 0, pl.multiple_of(jnp.minimum(D_FF + j * FF_BN, 2 * D_FF - FF_BN), LANES))),
            pl.BlockSpec((CONV_W, FF_BN), lambda j, i: (0, j)),
            pl.BlockSpec((CONV_W, FF_BN), lambda j, i: (0, nj + j)),
            pl.BlockSpec((1, FF_BN), lambda j, i: (0, j)),
            pl.BlockSpec((1, FF_BN), lambda j, i: (0, nj + j)),
        ],
        out_specs=pl.BlockSpec((FF_BM, FF_BN), lambda j, i: (i, j)),
        out_shape=jax.ShapeDtypeStruct((t, FF_P), jnp.bfloat16),
        scratch_shapes=[pltpu.VMEM((d, FF_BN), jnp.bfloat16), pltpu.VMEM((d, FF_BN), jnp.bfloat16),
                        pltpu.VMEM((FF_BM + SUBLANES, FF_BN), jnp.float32),
                        pltpu.VMEM((FF_BM + SUBLANES, FF_BN), jnp.float32)],
        compiler_params=_cparams(("arbitrary", "arbitrary")),
        name="ffn_up",
    )(h, w_up, w_up, conv_w, conv_w, conv_b, conv_b)


def _pad_heads(w, axis):
    h = w.shape[0]
    zero = jnp.zeros_like(w)
    even = jnp.concatenate([w, zero], axis=axis)
    odd = jnp.concatenate([zero, w], axis=axis)
    sel = (jnp.arange(h) % 2 == 0).reshape((h, 1, 1))
    return jnp.where(sel, even, odd)


def _toeplitz(fn, rows, cols):
    ks = np.concatenate([np.arange(0, cols), np.arange(-(rows - 1), 0)])
    w = fn(ks)
    h, period = w.shape
    x = jnp.tile(w, (1, rows))[:, :rows * (period - 1)].reshape(h, rows, period - 1)
    return x[:, :, :cols].astype(jnp.float32)


def _band(rows, cols, left):
    diff = left + np.arange(rows)[:, None] // CHUNK - np.arange(cols)[None, :] // CHUNK
    return (diff >= 0) & (diff <= left)


def _bias_a(rel_bias):
    fn = lambda ks: rel_bias[np.clip(A_PAD - ks, -A_MAX_REL, A_MAX_REL) + A_MAX_REL].T
    bias = jnp.where(_band(QB, A_WIN, A_LEFT_CHUNKS)[None], _toeplitz(fn, QB, A_WIN), NEG)
    return bias.reshape(A_HEADS * QB, A_WIN)


def _bias_c(t5_c):
    fn = lambda ks: t5_c[_t5_bucket(jnp.asarray(ks - C_PAD, jnp.int32))].T
    bias = jnp.where(_band(QB, C_WIN, C_LEFT_CHUNKS)[None], _toeplitz(fn, QB, C_WIN), NEG)
    return bias.reshape(C_Q_HEADS * QB, C_WIN)


def _bias_b(t5_b):
    tiles = []
    for n in range(B_NEAR + 1):
        off = QB * n if n < B_NEAR else SEQ
        fn = lambda ks, off=off: t5_b[_t5_bucket(jnp.asarray(-ks - off, jnp.int32))].T
        tile = _toeplitz(fn, KB, QB)
        tiles.append(jnp.transpose(tile, (1, 0, 2)).reshape(KB, HQ))
    return jnp.stack(tiles)


def _pad_ff(a, dtype):
    z = jnp.zeros((a.shape[0], FF_P - D_FF), dtype)
    return jnp.concatenate([a[:, :D_FF].astype(dtype), z, a[:, D_FF:].astype(dtype), z], axis=1)


def kernel(x, mem, t5_table, norm_gains, w_in, a_rel_bias, ckv_gain, w_uk, w_uv, sinks, w_branch, w_o,
           mem_gain, w_mq, w_mkv, w_mo, w_up, conv_w, conv_b, w_down):
    bf16 = jnp.bfloat16
    xs = x.reshape(TOKENS, D_MODEL)
    mem2 = mem.reshape(BATCH * MEM_LEN, D_MODEL)
    tri = jnp.asarray(np.tril(np.ones((KB, KB), np.float32), -1), bf16)
    bias_b = _bias_b(t5_table[:, :B_HEADS])
    bias_c = _bias_c(t5_table[:, B_HEADS:])
    gains = norm_gains.reshape(DEPTH, 6, 1, D_MODEL)
    w_o_b, w_mq_b, w_mkv_b, w_mo_b = (w.astype(bf16) for w in (w_o, w_mq, w_mkv, w_mo))
    w_dn_b = jnp.pad(w_down.astype(bf16), ((0, 0), (0, FF_P - D_FF), (0, 0)))
    w_in_t = jnp.swapaxes(w_in, 1, 2)

    h = _norm(xs, gains[0, 0])
    for l in range(DEPTH):
        g = gains[l]
        proja = _proj(h, w_in_t, l, 0, IN_A, 1024, gate=False, name="in_proj_a")
        projc = _proj(h, w_in_t, l, O_QC, C_COLS, C_COLS, gate=False, name="in_proj_c")
        gates = _proj(h, w_in_t, l, O_GL, 3 * D_MODEL, 1024, gate=True, name="in_proj_g")
        proja3 = proja.reshape(BATCH, SEQ, IN_A)
        oa = _attn_a(proja3, _bias_a(a_rel_bias[l]))
        wuk = jnp.transpose(_pad_heads(w_uk[l], axis=2), (0, 2, 1)).astype(bf16)
        wuv = _pad_heads(w_uv[l], axis=2).astype(bf16)
        ob = _attn_b(proja3, ckv_gain[l].reshape(1, B_LATENT), wuk, wuv, bias_b, tri)
        oc = _attn_c(sinks[l], projc.reshape(BATCH, SEQ, C_COLS), bias_c)
        merged = _merge(oa.reshape(TOKENS, A_W), ob.reshape(TOKENS, B_W), oc.reshape(TOKENS, C_W), gates, w_branch, l)
        xs, h = _gemm_res(merged, w_o_b, l, xs, g[1], g[2], bk=D_MODEL, name="out_proj")
        kv = _memkv(mem2, mem_gain[l].reshape(1, D_MODEL), w_mkv_b, l)
        xs, h = _xattn(h, w_mq_b, kv, w_mo_b, l, xs, g[3], g[4])
        hidden = _ffn_up(h, w_up, l, _pad_ff(conv_w[l], jnp.float32), _pad_ff(conv_b[l].reshape(1, -1), jnp.float32))
        g_next = gains[l + 1, 0] if l + 1 < DEPTH else None
        xs, h = _gemm_res(hidden, w_dn_b, l, xs, g[5], g_next, bk=DOWN_BK, name="ffn_down")
    return xs.reshape(BATCH, SEQ, D_MODEL)
```

```python
import functools
import math

import numpy as np
import jax
import jax.numpy as jnp
from jax import lax
from jax.experimental import pallas as pl
from jax.experimental.pallas import tpu as pltpu

D_MODEL = 2048
BATCH = 4
SEQ = 2048
DEPTH = 2
TOKENS = BATCH * SEQ
CHUNK = 64
EPS = 1e-6
NEG = -1e30
A_HEADS = 8
A_LEFT_CHUNKS = 8
A_MAX_REL = 128
A_W = 512
B_HEADS = 8
B_W = 512
B_LATENT = 128
IDX_HEADS = 8
IDX_DIM = 64
TOPK = 256
C_Q_HEADS = 16
C_GROUP = 8
C_W = 1024
C_LEFT_CHUNKS = 2
T5_BUCKETS = 32
T5_MAX_DIST = 256
MEM_LEN = 256
MEM_HEADS = 4
MEM_HEAD_DIM = 128
MEM_W = 512
D_FF = 5504
CONV_W = 3

LANES = 128
SUBLANES = 8
HEAD_DIM = 64
QB = 128
KB = 256
VMEM_LIMIT = 56 * 1024 * 1024

O_QA, O_KA, O_VA, O_QB = 0, 512, 1024, 1536
O_CKV = 2048
O_QI = 2176
O_KI = 2688
O_WI = 2752
O_QC = 2760
O_GL = 4040
IN_W = O_GL + 3 * D_MODEL
IN_A = 3072
X_COL0 = 2048
C_COLS = O_GL - O_QC
PROJ_BM = 1024
PROJ_TR = 256

FF_P = 5632
FF_BN = 512
FF_BM = 1024
FF_CH = 256
DOWN_BK = 1408

A_WIN = (A_LEFT_CHUNKS + 2) * CHUNK
C_WIN = (C_LEFT_CHUNKS + 2) * CHUNK
A_PAD = A_LEFT_CHUNKS * CHUNK
C_PAD = C_LEFT_CHUNKS * CHUNK
BQ = 256
B_NEAR = 2
NKB = SEQ // KB
HQ = B_HEADS * BQ

_NT = (((1,), (1,)), ((), ()))
_TN = (((0,), (0,)), ((), ()))


def _cparams(sem):
    return pltpu.CompilerParams(dimension_semantics=sem, vmem_limit_bytes=VMEM_LIMIT)


def _t5_bucket(rel):
    half = T5_BUCKETS // 2
    max_exact = half // 2
    sign = jnp.where(rel > 0, half, 0)
    d = jnp.abs(rel)
    d_f = jnp.maximum(d, 1).astype(jnp.float32)
    large = max_exact + (jnp.log(d_f / max_exact) / math.log(T5_MAX_DIST / max_exact) * (half - max_exact)).astype(jnp.int32)
    large = jnp.minimum(large, half - 1)
    return sign + jnp.where(d < max_exact, d, large)


def _far_bucket_is_constant():
    d = np.arange(BQ * B_NEAR - (KB - 1), SEQ, dtype=np.float32)
    assert d[0] > T5_MAX_DIST
    large = 8 + (np.log(d / 8) / math.log(T5_MAX_DIST / 8) * 8).astype(np.int32)
    return bool(np.all(np.minimum(large, 15) == 15))


assert _far_bucket_is_constant()


def _rms(v, g):
    return v * lax.rsqrt(jnp.mean(v * v, axis=-1, keepdims=True) + EPS) * g


def _norm_kernel(x_ref, g_ref, o_ref):
    o_ref[...] = _rms(x_ref[...], g_ref[...]).astype(o_ref.dtype)


def _norm(x, g, bm=1024):
    t, d = x.shape
    return pl.pallas_call(
        _norm_kernel,
        grid=(t // bm,),
        in_specs=[pl.BlockSpec((bm, d), lambda i: (i, 0)), pl.BlockSpec((1, d), lambda i: (0, 0))],
        out_specs=pl.BlockSpec((bm, d), lambda i: (i, 0)),
        out_shape=jax.ShapeDtypeStruct((t, d), jnp.bfloat16),
        compiler_params=_cparams(("parallel",)),
        name="rmsnorm",
    )(x, g)


def _proj_kernel(gate, h_ref, wt_ref, o_ref, wb):
    @pl.when(pl.program_id(1) == 0)
    def _():
        for c in range(wt_ref.shape[0] // PROJ_TR):
            rows = slice(c * PROJ_TR, (c + 1) * PROJ_TR)
            wb[:, rows] = wt_ref[rows, :].T.astype(wb.dtype)

    acc = jnp.dot(h_ref[...], wb[...], preferred_element_type=jnp.float32)
    o_ref[...] = (jax.nn.sigmoid(acc) if gate else acc).astype(o_ref.dtype)


def _proj(h, w_t, layer, col0, n, bn, gate, name):
    t, k = h.shape
    assert col0 % SUBLANES == 0 and bn % SUBLANES == 0
    w_spec = pl.BlockSpec((None, pl.Element(bn), pl.Element(k)),
                          lambda j, i: (layer, pl.multiple_of(col0 + j * bn, SUBLANES), 0))
    return pl.pallas_call(
        functools.partial(_proj_kernel, gate),
        grid=(n // bn, t // PROJ_BM),
        in_specs=[pl.BlockSpec((PROJ_BM, k), lambda j, i: (i, 0)), w_spec],
        out_specs=pl.BlockSpec((PROJ_BM, bn), lambda j, i: (i, j)),
        out_shape=jax.ShapeDtypeStruct((t, n), jnp.bfloat16),
        scratch_shapes=[pltpu.VMEM((k, bn), jnp.bfloat16)],
        compiler_params=_cparams(("arbitrary", "arbitrary")),
        name=name,
    )(h, w_t)


def _lane_masks(dtype):
    lane = lax.broadcasted_iota(jnp.int32, (1, LANES), 1)
    lo = (lane < HEAD_DIM).astype(jnp.float32)
    return lo.astype(dtype), (1.0 - lo).astype(dtype)


def _swap_halves(x):
    return pltpu.roll(x.astype(jnp.float32), HEAD_DIM, 1).astype(x.dtype)


def _fill_padded(src, dst_ref, pad):
    dst_ref[0:pad, :] = jnp.zeros((pad, dst_ref.shape[1]), dst_ref.dtype)
    dst_ref[pad:pad + SEQ, :] = src


def _attn_a_kernel(q_ref, k_ref, v_ref, bias_ref, o_ref, kpad, vpad):
    i = pl.program_id(1)

    @pl.when(i == 0)
    def _():
        _fill_padded(k_ref[...], kpad, A_PAD)
        _fill_padded(v_ref[...], vpad, A_PAD)

    start = pl.multiple_of(i * QB, QB)
    kw = kpad[pl.ds(start, A_WIN), :]
    vw = vpad[pl.ds(start, A_WIN), :]
    mlo, mhi = _lane_masks(jnp.bfloat16)
    lane = lax.broadcasted_iota(jnp.int32, (QB, LANES), 1)
    kchunk = lax.broadcasted_iota(jnp.int32, (QB, A_WIN), 1) // CHUNK
    pad_mask = jnp.where(kchunk >= A_LEFT_CHUNKS - 2 * i, 0.0, NEG)
    scale = HEAD_DIM ** -0.5
    pad2 = jnp.concatenate([pad_mask, pad_mask], axis=0)
    for p in range(A_HEADS // 2):
        cols = slice(p * LANES, (p + 1) * LANES)
        qp = q_ref[:, cols]
        q2 = jnp.concatenate([qp * mlo, qp * mhi], axis=0)
        s = lax.dot_general(q2, kw[:, cols], _NT, preferred_element_type=jnp.float32)
        s = s * scale + bias_ref[2 * p * QB:(2 * p + 2) * QB, :] + pad2
        m = jnp.max(s, axis=-1, keepdims=True)
        e = jnp.exp(s - m)
        l = jnp.sum(e, axis=-1, keepdims=True)
        o = jnp.dot(e.astype(jnp.bfloat16), vw[:, cols], preferred_element_type=jnp.float32) / l
        o_ref[:, cols] = jnp.where(lane < HEAD_DIM, o[:QB], o[QB:]).astype(o_ref.dtype)


def _attn_a(proj3, bias):
    nq = SEQ // QB
    return pl.pallas_call(
        _attn_a_kernel,
        grid=(BATCH, nq),
        in_specs=[
            pl.BlockSpec((None, QB, A_W), lambda b, i: (b, i, O_QA // A_W)),
            pl.BlockSpec((None, SEQ, A_W), lambda b, i: (b, 0, O_KA // A_W)),
            pl.BlockSpec((None, SEQ, A_W), lambda b, i: (b, 0, O_VA // A_W)),
            pl.BlockSpec((A_HEADS * QB, A_WIN), lambda b, i: (0, 0)),
        ],
        out_specs=pl.BlockSpec((None, QB, A_W), lambda b, i: (b, i, 0)),
        out_shape=jax.ShapeDtypeStruct((BATCH, SEQ, A_W), jnp.bfloat16),
        scratch_shapes=[pltpu.VMEM((SEQ + A_PAD, A_W), jnp.bfloat16), pltpu.VMEM((SEQ + A_PAD, A_W), jnp.bfloat16)],
        compiler_params=_cparams(("arbitrary", "arbitrary")),
        name="attn_a",
    )(proj3, proj3, proj3, bias)


def _attn_c_kernel(sink_ref, q_ref, k_ref, v_ref, bias_ref, o_ref, kpad, kswp, vpad, vswp):
    i = pl.program_id(1)

    @pl.when(i == 0)
    def _():
        k = k_ref[...]
        v = v_ref[...]
        _fill_padded(k, kpad, C_PAD)
        _fill_padded(_swap_halves(k), kswp, C_PAD)
        _fill_padded(v, vpad, C_PAD)
        _fill_padded(_swap_halves(v), vswp, C_PAD)

    start = pl.multiple_of(i * QB, QB)
    mlo, mhi = _lane_masks(jnp.bfloat16)
    lane = lax.broadcasted_iota(jnp.int32, (QB, LANES), 1)
    kchunk = lax.broadcasted_iota(jnp.int32, (QB, C_WIN), 1) // CHUNK
    pad_mask = jnp.where(kchunk >= C_LEFT_CHUNKS - 2 * i, 0.0, NEG)
    scale = HEAD_DIM ** -0.5
    qs = [q_ref[:, p * LANES:(p + 1) * LANES] for p in range(C_GROUP)]
    npair = C_GROUP // 2
    stacks = []
    for straight in (True, False):
        kref, vref = (kpad, vpad) if straight else (kswp, vswp)
        kw = kref[pl.ds(start, C_WIN), :]
        vw = vref[pl.ds(start, C_WIN), :]
        halves = [int((p >= npair) == straight) for p in range(C_GROUP)]
        qg = jnp.concatenate([qs[p] * (mhi if halves[p] else mlo) for p in range(C_GROUP)], axis=0)
        s_all = lax.dot_general(qg, kw, _NT, preferred_element_type=jnp.float32)
        ps, ls = [], []
        for p in range(C_GROUP):
            h = 2 * p + halves[p]
            s = s_all[p * QB:(p + 1) * QB] * scale + bias_ref[h * QB:(h + 1) * QB, :] + pad_mask
            sink = sink_ref[h]
            m = jnp.maximum(jnp.max(s, axis=-1, keepdims=True), sink)
            e = jnp.exp(s - m)
            ls.append(jnp.sum(e, axis=-1, keepdims=True) + jnp.exp(sink - m))
            ps.append(e.astype(jnp.bfloat16))
        o_all = jnp.dot(jnp.concatenate(ps, axis=0), vw, preferred_element_type=jnp.float32)
        stacks.append(([o_all[p * QB:(p + 1) * QB] / ls[p] for p in range(C_GROUP)], halves))
    for p in range(C_GROUP):
        (o1, h1), (o2, _) = stacks
        lo, hi = (o2[p], o1[p]) if h1[p] else (o1[p], o2[p])
        o_ref[:, p * LANES:(p + 1) * LANES] = jnp.where(lane < HEAD_DIM, lo, hi).astype(o_ref.dtype)


def _attn_c(sinks, projc3, bias):
    nq = SEQ // QB
    pad_buf = pltpu.VMEM((SEQ + C_PAD, LANES), jnp.bfloat16)
    return pl.pallas_call(
        _attn_c_kernel,
        grid=(BATCH, nq),
        in_specs=[
            pl.BlockSpec(memory_space=pltpu.SMEM),
            pl.BlockSpec((None, QB, C_W), lambda b, i: (b, i, 0)),
            pl.BlockSpec((None, SEQ, LANES), lambda b, i: (b, 0, C_W // LANES)),
            pl.BlockSpec((None, SEQ, LANES), lambda b, i: (b, 0, C_W // LANES + 1)),
            pl.BlockSpec((C_Q_HEADS * QB, C_WIN), lambda b, i: (0, 0)),
        ],
        out_specs=pl.BlockSpec((None, QB, C_W), lambda b, i: (b, i, 0)),
        out_shape=jax.ShapeDtypeStruct((BATCH, SEQ, C_W), jnp.bfloat16),
        scratch_shapes=[pad_buf, pad_buf, pad_buf, pad_buf],
        compiler_params=_cparams(("arbitrary", "arbitrary")),
        name="attn_c",
    )(sinks, projc3, projc3, projc3, bias)


def _sort_key(x):
    bits = lax.bitcast_convert_type(x + 0.0, jnp.int32)
    return bits ^ ((bits >> 31) & jnp.int32(0x7FFFFFFF))


def _rows8(x, op):
    parts = [x[r:r + SUBLANES, :] for r in range(0, x.shape[0], SUBLANES)]
    while len(parts) > 1:
        nxt = [op(parts[k], parts[k + 1]) for k in range(0, len(parts) - 1, 2)]
        parts = nxt + ([parts[-1]] if len(parts) % 2 else [])
    return parts[0]


def _attn_b_kernel(qb_ref, x_ref, kiw_ref, ckv_ref, gain_ref, wuk_ref, wuv_ref, bias_ref, tri_ref,
                   o_ref, ckvn, kd, ql, qi_all, sk, acc_s, p_s):
    i = pl.program_id(1)
    nkb = (i * BQ) // KB + 1
    mlo, mhi = _lane_masks(jnp.bfloat16)
    f32 = jnp.float32

    @pl.when(i == 0)
    def _():
        ckvn[...] = _rms(ckv_ref[...].astype(f32), gain_ref[...]).astype(ckvn.dtype)
        kiw = kiw_ref[...].astype(f32)
        lane = lax.broadcasted_iota(jnp.int32, kiw.shape, 1)
        kd[...] = jnp.where(lane < HEAD_DIM, kiw, pltpu.roll(kiw, HEAD_DIM, 1)).astype(kd.dtype)

    for h in range(B_HEADS):
        rows = slice(h * BQ, (h + 1) * BQ)
        qlat = jnp.dot(qb_ref[:, (h // 2) * LANES:(h // 2 + 1) * LANES], wuk_ref[h], preferred_element_type=f32)
        ql[rows, :] = (qlat * HEAD_DIM ** -0.5).astype(ql.dtype)
        qcol = O_QI - X_COL0 + (h // 2) * LANES
        qi_all[rows, :] = x_ref[:, qcol:qcol + LANES] * (mhi if h % 2 else mlo)
    wcol = O_KI - X_COL0
    wi_t = x_ref[:, wcol:wcol + LANES].astype(f32).T * (IDX_HEADS ** -0.5 * IDX_DIM ** -0.5)
    wi_rows = [wi_t[O_WI - O_KI + h:O_WI - O_KI + h + 1, :] for h in range(IDX_HEADS)]

    kpos = lax.broadcasted_iota(jnp.int32, (KB, BQ), 0)
    key_limit = ((i * BQ + lax.broadcasted_iota(jnp.int32, (1, BQ), 1)) // CHUNK + 1) * CHUNK

    def admissible(kb):
        return kb * KB + kpos < key_limit

    def score_block(kb, carry):
        kblk = kd[pl.ds(pl.multiple_of(kb * KB, KB), KB), :]
        dots = lax.dot_general(kblk, qi_all[...], _NT, preferred_element_type=f32)
        score = jnp.zeros((KB, BQ), f32)
        for h in range(IDX_HEADS):
            score = score + jnp.maximum(dots[:, h * BQ:(h + 1) * BQ], 0.0) * wi_rows[h]
        sk[kb] = _sort_key(jnp.where(admissible(kb), score, NEG))
        return carry

    lax.fori_loop(0, nkb, score_block, 0)

    neg_key = _sort_key(jnp.full((1, 1), NEG, f32))
    n_rest = ((NKB - nkb) * KB).astype(f32)

    def count(pred_fn, v):
        def body(kb, acc):
            return acc + _rows8(jnp.where(pred_fn(sk[kb], v), 1.0, 0.0), jnp.add)

        acc = lax.fori_loop(0, nkb, body, jnp.zeros((SUBLANES, BQ), f32))
        return jnp.sum(acc, axis=0, keepdims=True) + jnp.where(pred_fn(neg_key, v), n_rest, 0.0)

    ge = lambda a, v: a >= v
    int_min = jnp.int32(-2 ** 31)
    c0 = count(ge, jnp.zeros((1, BQ), jnp.int32))
    v0 = jnp.where(c0 >= TOPK, jnp.int32(0), int_min)

    def bit_step(t, carry):
        v, above = carry
        cand = v | (jnp.int32(1) << (30 - t))
        c = count(ge, cand)
        keep = c >= TOPK
        return jnp.where(keep, cand, v), jnp.where(keep, above, c)

    thr, above = lax.fori_loop(0, 31, bit_step, (v0, jnp.where(c0 >= TOPK, 0.0, c0)))
    need = TOPK - above

    acc_s[...] = jnp.zeros(acc_s.shape, f32)

    def attend(kb, carry):
        eq_seen, m_old, l_old = carry
        keys = sk[kb]
        eq = keys == thr
        eq_f = jnp.where(eq, 1.0, 0.0)
        before = jnp.dot(tri_ref[...], eq_f.astype(jnp.bfloat16), preferred_element_type=f32)
        take_eq = jnp.where(eq, jnp.where(eq_seen + before < need, 1.0, 0.0), 0.0)
        take = jnp.where(keys > thr, 1.0, take_eq)
        mask_add = jnp.where(admissible(kb), jnp.where(take > 0.0, 0.0, NEG), NEG)
        cblk = ckvn[pl.ds(pl.multiple_of(kb * KB, KB), KB), :]
        dots = lax.dot_general(cblk, ql[...], _NT, preferred_element_type=f32)
        tile = jnp.minimum(i - (KB // BQ) * kb, B_NEAR)
        m_parts, l_parts, a_parts = [], [], []
        for h in range(B_HEADS):
            cols = slice(h * BQ, (h + 1) * BQ)
            s = dots[:, cols] + bias_ref[tile, :, cols] + mask_add
            mo = m_old[:, cols]
            mn = jnp.maximum(mo, jnp.max(_rows8(s, jnp.maximum), axis=0, keepdims=True))
            a = jnp.exp(mo - mn)
            e = jnp.exp(s - mn)
            l_parts.append(a * l_old[:, cols] + jnp.sum(_rows8(e, jnp.add), axis=0, keepdims=True))
            m_parts.append(mn)
            a_parts.append(a)
            p_s[:, cols] = e.astype(p_s.dtype)
        alpha = jnp.concatenate(a_parts, axis=1)
        acc_s[...] = acc_s[...] * alpha + lax.dot_general(cblk, p_s[...], _TN, preferred_element_type=f32)
        eq_seen = eq_seen + jnp.sum(_rows8(eq_f, jnp.add), axis=0, keepdims=True)
        return eq_seen, jnp.concatenate(m_parts, axis=1), jnp.concatenate(l_parts, axis=1)

    init = (jnp.zeros((1, BQ), f32), jnp.full((1, HQ), 4 * NEG, f32), jnp.zeros((1, HQ), f32))
    _, _, l_fin = lax.fori_loop(0, nkb, attend, init)

    o_lat_t = acc_s[...] / l_fin
    for p in range(B_HEADS // 2):
        out = jnp.zeros((BQ, LANES), f32)
        for h in (2 * p, 2 * p + 1):
            o_lat = o_lat_t[:, h * BQ:(h + 1) * BQ].T.astype(jnp.bfloat16)
            out = out + jnp.dot(o_lat, wuv_ref[h], preferred_element_type=f32)
        o_ref[:, p * LANES:(p + 1) * LANES] = out.astype(o_ref.dtype)


def _attn_b(proj3, gain, wuk, wuv, bias, tri, nbatch=BATCH):
    nq = SEQ // BQ
    xw = IN_A - X_COL0
    return pl.pallas_call(
        _attn_b_kernel,
        grid=(nbatch, nq),
        in_specs=[
            pl.BlockSpec((None, BQ, B_W), lambda b, i: (b, i, O_QB // B_W)),
            pl.BlockSpec((None, BQ, xw), lambda b, i: (b, i, X_COL0 // xw)),
            pl.BlockSpec((None, SEQ, LANES), lambda b, i: (b, 0, O_KI // LANES)),
            pl.BlockSpec((None, SEQ, LANES), lambda b, i: (b, 0, O_CKV // LANES)),
            pl.BlockSpec((1, B_LATENT), lambda b, i: (0, 0)),
            pl.BlockSpec((B_HEADS, LANES, B_LATENT), lambda b, i: (0, 0, 0)),
            pl.BlockSpec((B_HEADS, B_LATENT, LANES), lambda b, i: (0, 0, 0)),
            pl.BlockSpec((B_NEAR + 1, KB, HQ), lambda b, i: (0, 0, 0)),
            pl.BlockSpec((KB, KB), lambda b, i: (0, 0)),
        ],
        out_specs=pl.BlockSpec((None, BQ, B_W), lambda b, i: (b, i, 0)),
        out_shape=jax.ShapeDtypeStruct((nbatch, SEQ, B_W), jnp.bfloat16),
        scratch_shapes=[
            pltpu.VMEM((SEQ, B_LATENT), jnp.bfloat16),
            pltpu.VMEM((SEQ, LANES), jnp.bfloat16),
            pltpu.VMEM((HQ, B_LATENT), jnp.bfloat16),
            pltpu.VMEM((HQ, LANES), jnp.bfloat16),
            pltpu.VMEM((NKB, KB, BQ), jnp.int32),
            pltpu.VMEM((B_LATENT, HQ), jnp.float32),
            pltpu.VMEM((KB, HQ), jnp.bfloat16),
        ],
        compiler_params=_cparams(("arbitrary", "arbitrary")),
        name="attn_b",
    )(proj3, proj3, proj3, proj3, gain, wuk, wuv, bias, tri)


MERGE_BN = 512
MERGE_BM = 1024


def _merge_kernel(oa_ref, ob_ref, oc_ref, ga_ref, gb_ref, gc_ref, wa_ref, wb_ref, wc_ref, o_ref, wa, wb, wc):
    f32 = jnp.float32

    @pl.when(pl.program_id(1) == 0)
    def _():
        wa[...] = wa_ref[...].astype(wa.dtype)
        wb[...] = wb_ref[...].astype(wb.dtype)
        wc[...] = wc_ref[...].astype(wc.dtype)

    m = ga_ref[...].astype(f32) * jnp.dot(oa_ref[...], wa[...], preferred_element_type=f32)
    m = m + gb_ref[...].astype(f32) * jnp.dot(ob_ref[...], wb[...], preferred_element_type=f32)
    m = m + gc_ref[...].astype(f32) * jnp.dot(oc_ref[...], wc[...], preferred_element_type=f32)
    o_ref[...] = m.astype(o_ref.dtype)


def _merge(oa, ob, oc, gates, w_branch, layer):
    t = oa.shape[0]
    bn, bm = MERGE_BN, MERGE_BM
    gstep = D_MODEL // bn
    return pl.pallas_call(
        _merge_kernel,
        grid=(D_MODEL // bn, t // bm),
        in_specs=[
            pl.BlockSpec((bm, A_W), lambda j, i: (i, 0)),
            pl.BlockSpec((bm, B_W), lambda j, i: (i, 0)),
            pl.BlockSpec((bm, C_W), lambda j, i: (i, 0)),
            pl.BlockSpec((bm, bn), lambda j, i: (i, j)),
            pl.BlockSpec((bm, bn), lambda j, i: (i, gstep + j)),
            pl.BlockSpec((bm, bn), lambda j, i: (i, 2 * gstep + j)),
            pl.BlockSpec((None, A_W, bn), lambda j, i: (layer, 0, j)),
            pl.BlockSpec((None, B_W, bn), lambda j, i: (layer, A_W // B_W, j)),
            pl.BlockSpec((None, C_W, bn), lambda j, i: (layer, (A_W + B_W) // C_W, j)),
        ],
        out_specs=pl.BlockSpec((bm, bn), lambda j, i: (i, j)),
        out_shape=jax.ShapeDtypeStruct((t, D_MODEL), jnp.bfloat16),
        scratch_shapes=[pltpu.VMEM((A_W, bn), jnp.bfloat16), pltpu.VMEM((B_W, bn), jnp.bfloat16),
                        pltpu.VMEM((C_W, bn), jnp.bfloat16)],
        compiler_params=_cparams(("arbitrary", "arbitrary")),
        name="merge",
    )(oa, ob, oc, gates, gates, gates, w_branch, w_branch, w_branch)


def _finish(y, x_ref, gp_ref, gn_ref, xo_ref, ho_ref):
    xn = x_ref[...] + _rms(y, gp_ref[...])
    xo_ref[...] = xn
    if ho_ref is not None:
        ho_ref[...] = _rms(xn, gn_ref[...]).astype(ho_ref.dtype)


def _gemm_res_kernel(nk, with_next, a_ref, w_ref, x_ref, gp_ref, gn_ref, xo_ref, *rest):
    ho_ref = rest[0] if with_next else None
    if nk == 1:
        y = jnp.dot(a_ref[...], w_ref[...], preferred_element_type=jnp.float32)
        _finish(y, x_ref, gp_ref, gn_ref, xo_ref, ho_ref)
        return
    acc = rest[-1]
    k = pl.program_id(1)

    @pl.when(k == 0)
    def _():
        acc[...] = jnp.zeros(acc.shape, jnp.float32)

    acc[...] += jnp.dot(a_ref[...], w_ref[...], preferred_element_type=jnp.float32)

    @pl.when(k == nk - 1)
    def _():
        _finish(acc[...], x_ref, gp_ref, gn_ref, xo_ref, ho_ref)


def _gemm_res(a, w, layer, x, g_post, g_next, bk, bm=512, name="gemm_res"):
    t, kdim = a.shape
    n = w.shape[2]
    nk = kdim // bk
    with_next = g_next is not None
    if g_next is None:
        g_next = g_post
    out_shape = [jax.ShapeDtypeStruct((t, n), jnp.float32)]
    out_specs = [pl.BlockSpec((bm, n), lambda i, k: (i, 0))]
    if with_next:
        out_shape.append(jax.ShapeDtypeStruct((t, n), jnp.bfloat16))
        out_specs.append(pl.BlockSpec((bm, n), lambda i, k: (i, 0)))
    res = pl.pallas_call(
        functools.partial(_gemm_res_kernel, nk, with_next),
        grid=(t // bm, nk),
        in_specs=[
            pl.BlockSpec((bm, bk), lambda i, k: (i, k)),
            pl.BlockSpec((None, bk, n), lambda i, k: (layer, k, 0)),
            pl.BlockSpec((bm, n), lambda i, k: (i, 0)),
            pl.BlockSpec((1, n), lambda i, k: (0, 0)),
            pl.BlockSpec((1, n), lambda i, k: (0, 0)),
        ],
        out_specs=out_specs,
        out_shape=out_shape,
        scratch_shapes=[pltpu.VMEM((bm, n), jnp.float32)] if nk > 1 else [],
        compiler_params=_cparams(("parallel", "arbitrary")),
        name=name,
    )(a, w, x, g_post, g_next)
    return (res[0], res[1]) if with_next else (res[0], None)


def _memkv_kernel(m_ref, g_ref, w_ref, o_ref):
    mn = _rms(m_ref[...], g_ref[...]).astype(jnp.bfloat16)
    o_ref[...] = jnp.dot(mn, w_ref[...], preferred_element_type=jnp.float32).astype(o_ref.dtype)


def _memkv(mem2, g, w, layer):
    t, d = mem2.shape
    n = w.shape[2]
    bm = 512
    return pl.pallas_call(
        _memkv_kernel,
        grid=(t // bm,),
        in_specs=[pl.BlockSpec((bm, d), lambda i: (i, 0)), pl.BlockSpec((1, d), lambda i: (0, 0)),
                  pl.BlockSpec((None, d, n), lambda i: (layer, 0, 0))],
        out_specs=pl.BlockSpec((bm, n), lambda i: (i, 0)),
        out_shape=jax.ShapeDtypeStruct((t, n), jnp.bfloat16),
        compiler_params=_cparams(("parallel",)),
        name="mem_kv",
    )(mem2, g, w)


XA_BM = 512


def _xattn_kernel(h_ref, wq_ref, kv_ref, wo_ref, x_ref, gp_ref, gn_ref, xo_ref, ho_ref):
    f32 = jnp.float32
    q = jnp.dot(h_ref[...], wq_ref[...], preferred_element_type=f32).astype(jnp.bfloat16)
    scale = MEM_HEAD_DIM ** -0.5
    outs = []
    for h in range(MEM_HEADS):
        cols = slice(h * LANES, (h + 1) * LANES)
        kh = kv_ref[:, cols]
        vh = kv_ref[:, MEM_W + h * LANES:MEM_W + (h + 1) * LANES]
        s = lax.dot_general(q[:, cols], kh, _NT, preferred_element_type=f32) * scale
        m = jnp.max(s, axis=-1, keepdims=True)
        e = jnp.exp(s - m)
        l = jnp.sum(e, axis=-1, keepdims=True)
        o = jnp.dot(e.astype(jnp.bfloat16), vh, preferred_element_type=f32)
        outs.append((o / l).astype(jnp.bfloat16))
    o = jnp.concatenate(outs, axis=-1)
    y = jnp.dot(o, wo_ref[...], preferred_element_type=f32)
    _finish(y, x_ref, gp_ref, gn_ref, xo_ref, ho_ref)


def _xattn(h, wq, kv, wo, layer, x, g_post, g_next):
    t, d = h.shape
    bm = XA_BM
    per_batch = SEQ // bm
    return pl.pallas_call(
        _xattn_kernel,
        grid=(t // bm,),
        in_specs=[
            pl.BlockSpec((bm, d), lambda i: (i, 0)),
            pl.BlockSpec((None, d, MEM_W), lambda i: (layer, 0, 0)),
            pl.BlockSpec((MEM_LEN, 2 * MEM_W), lambda i: (i // per_batch, 0)),
            pl.BlockSpec((None, MEM_W, d), lambda i: (layer, 0, 0)),
            pl.BlockSpec((bm, d), lambda i: (i, 0)),
            pl.BlockSpec((1, d), lambda i: (0, 0)),
            pl.BlockSpec((1, d), lambda i: (0, 0)),
        ],
        out_specs=[pl.BlockSpec((bm, d), lambda i: (i, 0)), pl.BlockSpec((bm, d), lambda i: (i, 0))],
        out_shape=[jax.ShapeDtypeStruct((t, d), jnp.float32), jax.ShapeDtypeStruct((t, d), jnp.bfloat16)],
        compiler_params=_cparams(("parallel",)),
        name="mem_xattn",
    )(h, wq, kv, wo, x, g_post, g_next)


def _ffn_up_kernel(h_ref, wgf_ref, wvf_ref, cwg_ref, cwv_ref, cbg_ref, cbv_ref, o_ref, wg_ref, wv_ref, ug, uv):
    j = pl.program_id(0)
    i = pl.program_id(1)
    tiles_per_seq = SEQ // FF_BM
    last = FF_P // FF_BN - 1
    valid = D_FF - last * FF_BN
    shift = FF_BN - valid

    @pl.when((i == 0) & (j < last))
    def _():
        wg_ref[...] = wgf_ref[...].astype(wg_ref.dtype)
        wv_ref[...] = wvf_ref[...].astype(wv_ref.dtype)

    @pl.when((i == 0) & (j == last))
    def _():
        zeros = jnp.zeros((wg_ref.shape[0], FF_BN - valid), wg_ref.dtype)
        wg_ref[:, :valid] = wgf_ref[:, :valid].astype(wg_ref.dtype)
        wv_ref[:, :valid] = wvf_ref[:, shift:].astype(wv_ref.dtype)
        wg_ref[:, valid:] = zeros
        wv_ref[:, valid:] = zeros

    for u in (ug, uv):
        @pl.when(i % tiles_per_seq == 0)
        def _():
            u[0:SUBLANES, :] = jnp.zeros((SUBLANES, FF_BN), jnp.float32)

        @pl.when(i % tiles_per_seq != 0)
        def _():
            u[0:SUBLANES, :] = u[FF_BM:FF_BM + SUBLANES, :]

    def conv(u, cw_ref, cb_ref, r0):
        base = SUBLANES + r0
        acc = cb_ref[...] + u[base - 2:base - 2 + FF_CH, :] * cw_ref[0:1, :]
        acc = acc + u[base - 1:base - 1 + FF_CH, :] * cw_ref[1:2, :]
        return acc + u[base:base + FF_CH, :] * cw_ref[2:3, :]

    for c in range(FF_BM // FF_CH):
        r0 = c * FF_CH
        hb = h_ref[r0:r0 + FF_CH, :]
        ug[SUBLANES + r0:SUBLANES + r0 + FF_CH, :] = jnp.dot(hb, wg_ref[...], preferred_element_type=jnp.float32)
        uv[SUBLANES + r0:SUBLANES + r0 + FF_CH, :] = jnp.dot(hb, wv_ref[...], preferred_element_type=jnp.float32)
        gate = conv(ug, cwg_ref, cbg_ref, r0)
        val = conv(uv, cwv_ref, cbv_ref, r0)
        o_ref[r0:r0 + FF_CH, :] = (jax.nn.gelu(gate) * val).astype(o_ref.dtype)


def _ffn_up(h, w_up, layer, conv_w, conv_b):
    t, d = h.shape
    nj = FF_P // FF_BN
    w_block = (None, pl.Element(d), pl.Element(FF_BN))
    return pl.pallas_call(
        _ffn_up_kernel,
        grid=(nj, t // FF_BM),
        in_specs=[
            pl.BlockSpec((FF_BM, d), lambda j, i: (i, 0)),
            pl.BlockSpec(w_block, lambda j, i: (layer, 0, pl.multiple_of(j * FF_BN, LANES))),
            pl.BlockSpec(w_block, lambda j, i: (
                layer, 0, pl.multiple_of(jnp.minimum(D_FF + j * FF_BN, 2 * D_FF - FF_BN), LANES))),
            pl.BlockSpec((CONV_W, FF_BN), lambda j, i: (0, j)),
            pl.BlockSpec((CONV_W, FF_BN), lambda j, i: (0, nj + j)),
            pl.BlockSpec((1, FF_BN), lambda j, i: (0, j)),
            pl.BlockSpec((1, FF_BN), lambda j, i: (0, nj + j)),
        ],
        out_specs=pl.BlockSpec((FF_BM, FF_BN), lambda j, i: (i, j)),
        out_shape=jax.ShapeDtypeStruct((t, FF_P), jnp.bfloat16),
        scratch_shapes=[pltpu.VMEM((d, FF_BN), jnp.bfloat16), pltpu.VMEM((d, FF_BN), jnp.bfloat16),
                        pltpu.VMEM((FF_BM + SUBLANES, FF_BN), jnp.float32),
                        pltpu.VMEM((FF_BM + SUBLANES, FF_BN), jnp.float32)],
        compiler_params=_cparams(("arbitrary", "arbitrary")),
        name="ffn_up",
    )(h, w_up, w_up, conv_w, conv_w, conv_b, conv_b)


def _pad_heads(w, axis):
    h = w.shape[0]
    zero = jnp.zeros_like(w)
    even = jnp.concatenate([w, zero], axis=axis)
    odd = jnp.concatenate([zero, w], axis=axis)
    sel = (jnp.arange(h) % 2 == 0).reshape((h, 1, 1))
    return jnp.where(sel, even, odd)


def _toeplitz(fn, rows, cols):
    ks = np.concatenate([np.arange(0, cols), np.arange(-(rows - 1), 0)])
    w = fn(ks)
    h, period = w.shape
    x = jnp.tile(w, (1, rows))[:, :rows * (period - 1)].reshape(h, rows, period - 1)
    return x[:, :, :cols].astype(jnp.float32)


def _band(rows, cols, left):
    diff = left + np.arange(rows)[:, None] // CHUNK - np.arange(cols)[None, :] // CHUNK
    return (diff >= 0) & (diff <= left)


def _bias_a(rel_bias):
    fn = lambda ks: rel_bias[np.clip(A_PAD - ks, -A_MAX_REL, A_MAX_REL) + A_MAX_REL].T
    bias = jnp.where(_band(QB, A_WIN, A_LEFT_CHUNKS)[None], _toeplitz(fn, QB, A_WIN), NEG)
    return bias.reshape(A_HEADS * QB, A_WIN)


def _bias_c(t5_c):
    fn = lambda ks: t5_c[_t5_bucket(jnp.asarray(ks - C_PAD, jnp.int32))].T
    bias = jnp.where(_band(QB, C_WIN, C_LEFT_CHUNKS)[None], _toeplitz(fn, QB, C_WIN), NEG)
    return bias.reshape(C_Q_HEADS * QB, C_WIN)


def _bias_b(t5_b):
    tiles = []
    for n in range(B_NEAR + 1):
        off = BQ * n if n < B_NEAR else SEQ
        fn = lambda ks, off=off: t5_b[_t5_bucket(jnp.asarray(-ks - off, jnp.int32))].T
        tile = _toeplitz(fn, KB, BQ)
        tiles.append(jnp.transpose(tile, (1, 0, 2)).reshape(KB, HQ))
    return jnp.stack(tiles)


def _pad_ff(a, dtype):
    z = jnp.zeros((a.shape[0], FF_P - D_FF), dtype)
    return jnp.concatenate([a[:, :D_FF].astype(dtype), z, a[:, D_FF:].astype(dtype), z], axis=1)


def kernel(x, mem, t5_table, norm_gains, w_in, a_rel_bias, ckv_gain, w_uk, w_uv, sinks, w_branch, w_o,
           mem_gain, w_mq, w_mkv, w_mo, w_up, conv_w, conv_b, w_down):
    bf16 = jnp.bfloat16
    xs = x.reshape(TOKENS, D_MODEL)
    mem2 = mem.reshape(BATCH * MEM_LEN, D_MODEL)
    tri = jnp.asarray(np.tril(np.ones((KB, KB), np.float32), -1), bf16)
    bias_b = _bias_b(t5_table[:, :B_HEADS])
    bias_c = _bias_c(t5_table[:, B_HEADS:])
    gains = norm_gains.reshape(DEPTH, 6, 1, D_MODEL)
    w_o_b, w_mq_b, w_mkv_b, w_mo_b = (w.astype(bf16) for w in (w_o, w_mq, w_mkv, w_mo))
    w_dn_b = jnp.pad(w_down.astype(bf16), ((0, 0), (0, FF_P - D_FF), (0, 0)))
    w_in_t = jnp.swapaxes(w_in, 1, 2)

    h = _norm(xs, gains[0, 0])
    for l in range(DEPTH):
        g = gains[l]
        proja = _proj(h, w_in_t, l, 0, IN_A, 1024, gate=False, name="in_proj_a")
        projc = _proj(h, w_in_t, l, O_QC, C_COLS, C_COLS, gate=False, name="in_proj_c")
        gates = _proj(h, w_in_t, l, O_GL, 3 * D_MODEL, 1024, gate=True, name="in_proj_g")
        proja3 = proja.reshape(BATCH, SEQ, IN_A)
        oa = _attn_a(proja3, _bias_a(a_rel_bias[l]))
        wuk = jnp.transpose(_pad_heads(w_uk[l], axis=2), (0, 2, 1)).astype(bf16)
        wuv = _pad_heads(w_uv[l], axis=2).astype(bf16)
        ob = _attn_b(proja3, ckv_gain[l].reshape(1, B_LATENT), wuk, wuv, bias_b, tri)
        oc = _attn_c(sinks[l], projc.reshape(BATCH, SEQ, C_COLS), bias_c)
        merged = _merge(oa.reshape(TOKENS, A_W), ob.reshape(TOKENS, B_W), oc.reshape(TOKENS, C_W), gates, w_branch, l)
        xs, h = _gemm_res(merged, w_o_b, l, xs, g[1], g[2], bk=D_MODEL, name="out_proj")
        kv = _memkv(mem2, mem_gain[l].reshape(1, D_MODEL), w_mkv_b, l)
        xs, h = _xattn(h, w_mq_b, kv, w_mo_b, l, xs, g[3], g[4])
        hidden = _ffn_up(h, w_up, l, _pad_ff(conv_w[l], jnp.float32), _pad_ff(conv_b[l].reshape(1, -1), jnp.float32))
        g_next = gains[l + 1, 0] if l + 1 < DEPTH else None
        xs, h = _gemm_res(hidden, w_dn_b, l, xs, g[5], g_next, bk=DOWN_BK, name="ffn_down")
    return xs.reshape(BATCH, SEQ, D_MODEL)
```

```python
import functools
import math

import numpy as np
import jax
import jax.numpy as jnp
from jax import lax
from jax.experimental import pallas as pl
from jax.experimental.pallas import tpu as pltpu

D_MODEL = 2048
BATCH = 4
SEQ = 2048
DEPTH = 2
TOKENS = BATCH * SEQ
CHUNK = 64
EPS = 1e-6
NEG = -1e30
A_HEADS = 8
A_LEFT_CHUNKS = 8
A_MAX_REL = 128
A_W = 512
B_HEADS = 8
B_W = 512
B_LATENT = 128
IDX_HEADS = 8
IDX_DIM = 64
TOPK = 256
C_Q_HEADS = 16
C_GROUP = 8
C_W = 1024
C_LEFT_CHUNKS = 2
T5_BUCKETS = 32
T5_MAX_DIST = 256
MEM_LEN = 256
MEM_HEADS = 4
MEM_HEAD_DIM = 128
MEM_W = 512
D_FF = 5504
CONV_W = 3

LANES = 128
SUBLANES = 8
HEAD_DIM = 64
QB = 128
KB = 256
VMEM_LIMIT = 56 * 1024 * 1024

O_QA, O_KA, O_VA, O_QB = 0, 512, 1024, 1536
O_CKV = 2048
O_QI = 2176
O_KI = 2688
O_WI = 2752
O_QC = 2760
O_GL = 4040
IN_W = O_GL + 3 * D_MODEL
IN_A = 3072
X_COL0 = 2048
C_COLS = O_GL - O_QC
PROJ_BM = 1024
PROJ_TR = 256

FF_P = 5632
FF_BN = 512
FF_BM = 1024
FF_CH = 256
DOWN_BK = 1408

A_WIN = (A_LEFT_CHUNKS + 2) * CHUNK
C_WIN = (C_LEFT_CHUNKS + 2) * CHUNK
A_PAD = A_LEFT_CHUNKS * CHUNK
C_PAD = C_LEFT_CHUNKS * CHUNK
BQ = 256
B_NEAR = 2
NKB = SEQ // KB
HQ = B_HEADS * BQ

_NT = (((1,), (1,)), ((), ()))
_TN = (((0,), (0,)), ((), ()))


def _cparams(sem):
    return pltpu.CompilerParams(dimension_semantics=sem, vmem_limit_bytes=VMEM_LIMIT)


def _t5_bucket(rel):
    half = T5_BUCKETS // 2
    max_exact = half // 2
    sign = jnp.where(rel > 0, half, 0)
    d = jnp.abs(rel)
    d_f = jnp.maximum(d, 1).astype(jnp.float32)
    large = max_exact + (jnp.log(d_f / max_exact) / math.log(T5_MAX_DIST / max_exact) * (half - max_exact)).astype(jnp.int32)
    large = jnp.minimum(large, half - 1)
    return sign + jnp.where(d < max_exact, d, large)


def _far_bucket_is_constant():
    d = np.arange(BQ * B_NEAR - (KB - 1), SEQ, dtype=np.float32)
    assert d[0] > T5_MAX_DIST
    large = 8 + (np.log(d / 8) / math.log(T5_MAX_DIST / 8) * 8).astype(np.int32)
    return bool(np.all(np.minimum(large, 15) == 15))


assert _far_bucket_is_constant()


def _rms(v, g):
    return v * lax.rsqrt(jnp.mean(v * v, axis=-1, keepdims=True) + EPS) * g


def _norm_kernel(x_ref, g_ref, o_ref):
    o_ref[...] = _rms(x_ref[...], g_ref[...]).astype(o_ref.dtype)


def _norm(x, g, bm=1024):
    t, d = x.shape
    return pl.pallas_call(
        _norm_kernel,
        grid=(t // bm,),
        in_specs=[pl.BlockSpec((bm, d), lambda i: (i, 0)), pl.BlockSpec((1, d), lambda i: (0, 0))],
        out_specs=pl.BlockSpec((bm, d), lambda i: (i, 0)),
        out_shape=jax.ShapeDtypeStruct((t, d), jnp.bfloat16),
        compiler_params=_cparams(("parallel",)),
        name="rmsnorm",
    )(x, g)


def _proj_kernel(gate, h_ref, wt_ref, o_ref, wb):
    @pl.when(pl.program_id(1) == 0)
    def _():
        for c in range(wt_ref.shape[0] // PROJ_TR):
            rows = slice(c * PROJ_TR, (c + 1) * PROJ_TR)
            wb[:, rows] = wt_ref[rows, :].T.astype(wb.dtype)

    acc = jnp.dot(h_ref[...], wb[...], preferred_element_type=jnp.float32)
    o_ref[...] = (jax.nn.sigmoid(acc) if gate else acc).astype(o_ref.dtype)


def _proj(h, w_t, layer, col0, n, bn, gate, name):
    t, k = h.shape
    assert col0 % SUBLANES == 0 and bn % SUBLANES == 0
    w_spec = pl.BlockSpec((None, pl.Element(bn), pl.Element(k)),
                          lambda j, i: (layer, pl.multiple_of(col0 + j * bn, SUBLANES), 0))
    return pl.pallas_call(
        functools.partial(_proj_kernel, gate),
        grid=(n // bn, t // PROJ_BM),
        in_specs=[pl.BlockSpec((PROJ_BM, k), lambda j, i: (i, 0)), w_spec],
        out_specs=pl.BlockSpec((PROJ_BM, bn), lambda j, i: (i, j)),
        out_shape=jax.ShapeDtypeStruct((t, n), jnp.bfloat16),
        scratch_shapes=[pltpu.VMEM((k, bn), jnp.bfloat16)],
        compiler_params=_cparams(("arbitrary", "arbitrary")),
        name=name,
    )(h, w_t)


def _lane_masks(dtype):
    lane = lax.broadcasted_iota(jnp.int32, (1, LANES), 1)
    lo = (lane < HEAD_DIM).astype(jnp.float32)
    return lo.astype(dtype), (1.0 - lo).astype(dtype)


def _swap_halves(x):
    return pltpu.roll(x.astype(jnp.float32), HEAD_DIM, 1).astype(x.dtype)


def _fill_padded(src, dst_ref, pad):
    dst_ref[0:pad, :] = jnp.zeros((pad, dst_ref.shape[1]), dst_ref.dtype)
    dst_ref[pad:pad + SEQ, :] = src


def _attn_a_kernel(q_ref, k_ref, v_ref, bias_ref, o_ref, kpad, vpad):
    i = pl.program_id(1)

    @pl.when(i == 0)
    def _():
        _fill_padded(k_ref[...], kpad, A_PAD)
        _fill_padded(v_ref[...], vpad, A_PAD)

    start = pl.multiple_of(i * QB, QB)
    kw = kpad[pl.ds(start, A_WIN), :]
    vw = vpad[pl.ds(start, A_WIN), :]
    mlo, mhi = _lane_masks(jnp.bfloat16)
    lane = lax.broadcasted_iota(jnp.int32, (QB, LANES), 1)
    kchunk = lax.broadcasted_iota(jnp.int32, (QB, A_WIN), 1) // CHUNK
    pad_mask = jnp.where(kchunk >= A_LEFT_CHUNKS - 2 * i, 0.0, NEG)
    scale = HEAD_DIM ** -0.5
    pad2 = jnp.concatenate([pad_mask, pad_mask], axis=0)
    for p in range(A_HEADS // 2):
        cols = slice(p * LANES, (p + 1) * LANES)
        qp = q_ref[:, cols]
        q2 = jnp.concatenate([qp * mlo, qp * mhi], axis=0)
        s = lax.dot_general(q2, kw[:, cols], _NT, preferred_element_type=jnp.float32)
        s = s * scale + bias_ref[2 * p * QB:(2 * p + 2) * QB, :] + pad2
        m = jnp.max(s, axis=-1, keepdims=True)
        e = jnp.exp(s - m)
        l = jnp.sum(e, axis=-1, keepdims=True)
        o = jnp.dot(e.astype(jnp.bfloat16), vw[:, cols], preferred_element_type=jnp.float32) / l
        o_ref[:, cols] = jnp.where(lane < HEAD_DIM, o[:QB], o[QB:]).astype(o_ref.dtype)


def _attn_a(proj3, bias):
    nq = SEQ // QB
    return pl.pallas_call(
        _attn_a_kernel,
        grid=(BATCH, nq),
        in_specs=[
            pl.BlockSpec((None, QB, A_W), lambda b, i: (b, i, O_QA // A_W)),
            pl.BlockSpec((None, SEQ, A_W), lambda b, i: (b, 0, O_KA // A_W)),
            pl.BlockSpec((None, SEQ, A_W), lambda b, i: (b, 0, O_VA // A_W)),
            pl.BlockSpec((A_HEADS * QB, A_WIN), lambda b, i: (0, 0)),
        ],
        out_specs=pl.BlockSpec((None, QB, A_W), lambda b, i: (b, i, 0)),
        out_shape=jax.ShapeDtypeStruct((BATCH, SEQ, A_W), jnp.bfloat16),
        scratch_shapes=[pltpu.VMEM((SEQ + A_PAD, A_W), jnp.bfloat16), pltpu.VMEM((SEQ + A_PAD, A_W), jnp.bfloat16)],
        compiler_params=_cparams(("arbitrary", "arbitrary")),
        name="attn_a",
    )(proj3, proj3, proj3, bias)


def _attn_c_kernel(sink_ref, q_ref, k_ref, v_ref, bias_ref, o_ref, kpad, kswp, vpad, vswp):
    i = pl.program_id(1)

    @pl.when(i == 0)
    def _():
        k = k_ref[...]
        v = v_ref[...]
        _fill_padded(k, kpad, C_PAD)
        _fill_padded(_swap_halves(k), kswp, C_PAD)
        _fill_padded(v, vpad, C_PAD)
        _fill_padded(_swap_halves(v), vswp, C_PAD)

    start = pl.multiple_of(i * QB, QB)
    mlo, mhi = _lane_masks(jnp.bfloat16)
    lane = lax.broadcasted_iota(jnp.int32, (QB, LANES), 1)
    kchunk = lax.broadcasted_iota(jnp.int32, (QB, C_WIN), 1) // CHUNK
    pad_mask = jnp.where(kchunk >= C_LEFT_CHUNKS - 2 * i, 0.0, NEG)
    scale = HEAD_DIM ** -0.5
    qs = [q_ref[:, p * LANES:(p + 1) * LANES] for p in range(C_GROUP)]
    npair = C_GROUP // 2
    stacks = []
    for straight in (True, False):
        kref, vref = (kpad, vpad) if straight else (kswp, vswp)
        kw = kref[pl.ds(start, C_WIN), :]
        vw = vref[pl.ds(start, C_WIN), :]
        halves = [int((p >= npair) == straight) for p in range(C_GROUP)]
        qg = jnp.concatenate([qs[p] * (mhi if halves[p] else mlo) for p in range(C_GROUP)], axis=0)
        s_all = lax.dot_general(qg, kw, _NT, preferred_element_type=jnp.float32)
        ps, ls = [], []
        for p in range(C_GROUP):
            h = 2 * p + halves[p]
            s = s_all[p * QB:(p + 1) * QB] * scale + bias_ref[h * QB:(h + 1) * QB, :] + pad_mask
            sink = sink_ref[h]
            m = jnp.maximum(jnp.max(s, axis=-1, keepdims=True), sink)
            e = jnp.exp(s - m)
            ls.append(jnp.sum(e, axis=-1, keepdims=True) + jnp.exp(sink - m))
            ps.append(e.astype(jnp.bfloat16))
        o_all = jnp.dot(jnp.concatenate(ps, axis=0), vw, preferred_element_type=jnp.float32)
        stacks.append(([o_all[p * QB:(p + 1) * QB] / ls[p] for p in range(C_GROUP)], halves))
    for p in range(C_GROUP):
        (o1, h1), (o2, _) = stacks
        lo, hi = (o2[p], o1[p]) if h1[p] else (o1[p], o2[p])
        o_ref[:, p * LANES:(p + 1) * LANES] = jnp.where(lane < HEAD_DIM, lo, hi).astype(o_ref.dtype)


def _attn_c(sinks, projc3, bias):
    nq = SEQ // QB
    pad_buf = pltpu.VMEM((SEQ + C_PAD, LANES), jnp.bfloat16)
    return pl.pallas_call(
        _attn_c_kernel,
        grid=(BATCH, nq),
        in_specs=[
            pl.BlockSpec(memory_space=pltpu.SMEM),
            pl.BlockSpec((None, QB, C_W), lambda b, i: (b, i, 0)),
            pl.BlockSpec((None, SEQ, LANES), lambda b, i: (b, 0, C_W // LANES)),
            pl.BlockSpec((None, SEQ, LANES), lambda b, i: (b, 0, C_W // LANES + 1)),
            pl.BlockSpec((C_Q_HEADS * QB, C_WIN), lambda b, i: (0, 0)),
        ],
        out_specs=pl.BlockSpec((None, QB, C_W), lambda b, i: (b, i, 0)),
        out_shape=jax.ShapeDtypeStruct((BATCH, SEQ, C_W), jnp.bfloat16),
        scratch_shapes=[pad_buf, pad_buf, pad_buf, pad_buf],
        compiler_params=_cparams(("arbitrary", "arbitrary")),
        name="attn_c",
    )(sinks, projc3, projc3, projc3, bias)


def _sort_key(x):
    bits = lax.bitcast_convert_type(x + 0.0, jnp.int32)
    return bits ^ ((bits >> 31) & jnp.int32(0x7FFFFFFF))


def _rows_tree(x, op, slab):
    parts = [x[r:r + slab, :] for r in range(0, x.shape[0], slab)]
    while len(parts) > 1:
        nxt = [op(parts[k], parts[k + 1]) for k in range(0, len(parts) - 1, 2)]
        parts = nxt + ([parts[-1]] if len(parts) % 2 else [])
    return parts[0]


def _rows8(x, op):
    return _rows_tree(x, op, SUBLANES)


BF16_ROWS = 16


def _bit_search(count_ge, target, v, above, top_bit):
    def step(t, carry):
        v, above = carry
        cand = v | (jnp.int32(1) << (top_bit - t))
        c = count_ge(cand)
        keep = c >= target
        return jnp.where(keep, cand, v), jnp.where(keep, above, c)

    return lax.fori_loop(0, top_bit + 1, step, (v, above))


def _attn_b_kernel(qb_ref, x_ref, kiw_ref, ckv_ref, gain_ref, wuk_ref, wuv_ref, bias_ref, tri_ref,
                   o_ref, ckvn, kd, ql, qi_all, sk, dg, acc_s, p_s):
    i = pl.program_id(1)
    nkb = (i * BQ) // KB + 1
    mlo, mhi = _lane_masks(jnp.bfloat16)
    f32 = jnp.float32

    @pl.when(i == 0)
    def _():
        ckvn[...] = _rms(ckv_ref[...].astype(f32), gain_ref[...]).astype(ckvn.dtype)
        kiw = kiw_ref[...].astype(f32)
        lane = lax.broadcasted_iota(jnp.int32, kiw.shape, 1)
        kd[...] = jnp.where(lane < HEAD_DIM, kiw, pltpu.roll(kiw, HEAD_DIM, 1)).astype(kd.dtype)

    for h in range(B_HEADS):
        rows = slice(h * BQ, (h + 1) * BQ)
        qlat = jnp.dot(qb_ref[:, (h // 2) * LANES:(h // 2 + 1) * LANES], wuk_ref[h], preferred_element_type=f32)
        ql[rows, :] = (qlat * HEAD_DIM ** -0.5).astype(ql.dtype)
        qcol = O_QI - X_COL0 + (h // 2) * LANES
        qi_all[rows, :] = x_ref[:, qcol:qcol + LANES] * (mhi if h % 2 else mlo)
    wcol = O_KI - X_COL0
    wi_t = x_ref[:, wcol:wcol + LANES].astype(f32).T * (IDX_HEADS ** -0.5 * IDX_DIM ** -0.5)
    wi_rows = [wi_t[O_WI - O_KI + h:O_WI - O_KI + h + 1, :] for h in range(IDX_HEADS)]

    kpos = lax.broadcasted_iota(jnp.int32, (KB, BQ), 0)
    key_limit = ((i * BQ + lax.broadcasted_iota(jnp.int32, (1, BQ), 1)) // CHUNK + 1) * CHUNK

    def admissible(kb):
        return kb * KB + kpos < key_limit

    def score_block(kb, carry):
        kblk = kd[pl.ds(pl.multiple_of(kb * KB, KB), KB), :]
        dots = lax.dot_general(kblk, qi_all[...], _NT, preferred_element_type=f32)
        score = jnp.zeros((KB, BQ), f32)
        for h in range(IDX_HEADS):
            score = score + jnp.maximum(dots[:, h * BQ:(h + 1) * BQ], 0.0) * wi_rows[h]
        sk[kb] = _sort_key(jnp.where(admissible(kb), score, NEG))
        return carry

    lax.fori_loop(0, nkb, score_block, 0)

    neg_key = _sort_key(jnp.full((1, 1), NEG, f32))
    n_rest = ((NKB - nkb) * KB).astype(f32)
    one, zero = jnp.ones((), jnp.bfloat16), jnp.zeros((), jnp.bfloat16)

    def byte_of(key, byte):
        return ((key >> 24) + 128) if byte == 3 else ((key >> (8 * byte)) & 255)

    target = jnp.full((1, BQ), TOPK, f32)
    prefix = jnp.zeros((1, BQ), jnp.int32)
    above = jnp.zeros((1, BQ), f32)
    for byte in (3, 2, 1, 0):
        def in_class(key, byte=byte, prefix=prefix):
            return (key >> (8 * byte + 8)) == prefix

        def prepare(kb, carry, byte=byte, in_class=in_class):
            key = sk[kb]
            digit = byte_of(key, byte).astype(f32)
            if byte < 3:
                digit = jnp.where(in_class(key), digit, -1.0)
            dg[kb] = digit.astype(dg.dtype)
            return carry

        lax.fori_loop(0, nkb, prepare, 0)
        rest_digit = byte_of(neg_key, byte)
        rest_on = in_class(neg_key) if byte < 3 else (neg_key == neg_key)

        def count_ge(cand, rest_digit=rest_digit, rest_on=rest_on):
            cand_b = cand.astype(f32).astype(jnp.bfloat16)

            def body(kb, acc):
                hit = jnp.where(dg[kb] >= cand_b, one, zero)
                return acc + _rows_tree(hit, jnp.add, BF16_ROWS).astype(f32)

            acc = lax.fori_loop(0, nkb, body, jnp.zeros((BF16_ROWS, BQ), f32))
            rest = jnp.where(rest_on & (rest_digit >= cand), n_rest, 0.0)
            return jnp.sum(acc, axis=0, keepdims=True) + rest

        digit_thr, above_here = _bit_search(count_ge, target, jnp.zeros((1, BQ), jnp.int32),
                                            jnp.zeros((1, BQ), f32), 7)
        prefix = (digit_thr - 128) if byte == 3 else (prefix * 256 + digit_thr)
        above = above + above_here
        target = target - above_here
    thr = prefix
    need = TOPK - above

    acc_s[...] = jnp.zeros(acc_s.shape, f32)

    def attend(kb, carry):
        eq_seen, m_old, l_old = carry
        keys = sk[kb]
        eq = keys == thr
        eq_f = jnp.where(eq, 1.0, 0.0)
        before = jnp.dot(tri_ref[...], eq_f.astype(jnp.bfloat16), preferred_element_type=f32)
        take_eq = jnp.where(eq, jnp.where(eq_seen + before < need, 1.0, 0.0), 0.0)
        take = jnp.where(keys > thr, 1.0, take_eq)
        mask_add = jnp.where(admissible(kb), jnp.where(take > 0.0, 0.0, NEG), NEG)
        cblk = ckvn[pl.ds(pl.multiple_of(kb * KB, KB), KB), :]
        dots = lax.dot_general(cblk, ql[...], _NT, preferred_element_type=f32)
        tile = jnp.minimum(i - (KB // BQ) * kb, B_NEAR)
        m_parts, l_parts, a_parts = [], [], []
        for h in range(B_HEADS):
            cols = slice(h * BQ, (h + 1) * BQ)
            s = dots[:, cols] + bias_ref[tile, :, cols] + mask_add
            mo = m_old[:, cols]
            mn = jnp.maximum(mo, jnp.max(_rows8(s, jnp.maximum), axis=0, keepdims=True))
            a = jnp.exp(mo - mn)
            e = jnp.exp(s - mn)
            l_parts.append(a * l_old[:, cols] + jnp.sum(_rows8(e, jnp.add), axis=0, keepdims=True))
            m_parts.append(mn)
            a_parts.append(a)
            p_s[:, cols] = e.astype(p_s.dtype)
        alpha = jnp.concatenate(a_parts, axis=1)
        acc_s[...] = acc_s[...] * alpha + lax.dot_general(cblk, p_s[...], _TN, preferred_element_type=f32)
        eq_seen = eq_seen + jnp.sum(_rows8(eq_f, jnp.add), axis=0, keepdims=True)
        return eq_seen, jnp.concatenate(m_parts, axis=1), jnp.concatenate(l_parts, axis=1)

    init = (jnp.zeros((1, BQ), f32), jnp.full((1, HQ), 4 * NEG, f32), jnp.zeros((1, HQ), f32))
    _, _, l_fin = lax.fori_loop(0, nkb, attend, init)

    o_lat_t = acc_s[...] / l_fin
    for p in range(B_HEADS // 2):
        out = jnp.zeros((BQ, LANES), f32)
        for h in (2 * p, 2 * p + 1):
            o_lat = o_lat_t[:, h * BQ:(h + 1) * BQ].T.astype(jnp.bfloat16)
            out = out + jnp.dot(o_lat, wuv_ref[h], preferred_element_type=f32)
        o_ref[:, p * LANES:(p + 1) * LANES] = out.astype(o_ref.dtype)


def _attn_b(proj3, gain, wuk, wuv, bias, tri, nbatch=BATCH):
    nq = SEQ // BQ
    xw = IN_A - X_COL0
    return pl.pallas_call(
        _attn_b_kernel,
        grid=(nbatch, nq),
        in_specs=[
            pl.BlockSpec((None, BQ, B_W), lambda b, i: (b, i, O_QB // B_W)),
            pl.BlockSpec((None, BQ, xw), lambda b, i: (b, i, X_COL0 // xw)),
            pl.BlockSpec((None, SEQ, LANES), lambda b, i: (b, 0, O_KI // LANES)),
            pl.BlockSpec((None, SEQ, LANES), lambda b, i: (b, 0, O_CKV // LANES)),
            pl.BlockSpec((1, B_LATENT), lambda b, i: (0, 0)),
            pl.BlockSpec((B_HEADS, LANES, B_LATENT), lambda b, i: (0, 0, 0)),
            pl.BlockSpec((B_HEADS, B_LATENT, LANES), lambda b, i: (0, 0, 0)),
            pl.BlockSpec((B_NEAR + 1, KB, HQ), lambda b, i: (0, 0, 0)),
            pl.BlockSpec((KB, KB), lambda b, i: (0, 0)),
        ],
        out_specs=pl.BlockSpec((None, BQ, B_W), lambda b, i: (b, i, 0)),
        out_shape=jax.ShapeDtypeStruct((nbatch, SEQ, B_W), jnp.bfloat16),
        scratch_shapes=[
            pltpu.VMEM((SEQ, B_LATENT), jnp.bfloat16),
            pltpu.VMEM((SEQ, LANES), jnp.bfloat16),
            pltpu.VMEM((HQ, B_LATENT), jnp.bfloat16),
            pltpu.VMEM((HQ, LANES), jnp.bfloat16),
            pltpu.VMEM((NKB, KB, BQ), jnp.int32),
            pltpu.VMEM((NKB, KB, BQ), jnp.bfloat16),
            pltpu.VMEM((B_LATENT, HQ), jnp.float32),
            pltpu.VMEM((KB, HQ), jnp.bfloat16),
        ],
        compiler_params=_cparams(("arbitrary", "arbitrary")),
        name="attn_b",
    )(proj3, proj3, proj3, proj3, gain, wuk, wuv, bias, tri)


MERGE_BN = 512
MERGE_BM = 1024


def _merge_kernel(oa_ref, ob_ref, oc_ref, ga_ref, gb_ref, gc_ref, wa_ref, wb_ref, wc_ref, o_ref, wa, wb, wc):
    f32 = jnp.float32

    @pl.when(pl.program_id(1) == 0)
    def _():
        wa[...] = wa_ref[...].astype(wa.dtype)
        wb[...] = wb_ref[...].astype(wb.dtype)
        wc[...] = wc_ref[...].astype(wc.dtype)

    m = ga_ref[...].astype(f32) * jnp.dot(oa_ref[...], wa[...], preferred_element_type=f32)
    m = m + gb_ref[...].astype(f32) * jnp.dot(ob_ref[...], wb[...], preferred_element_type=f32)
    m = m + gc_ref[...].astype(f32) * jnp.dot(oc_ref[...], wc[...], preferred_element_type=f32)
    o_ref[...] = m.astype(o_ref.dtype)


def _merge(oa, ob, oc, gates, w_branch, layer):
    t = oa.shape[0]
    bn, bm = MERGE_BN, MERGE_BM
    gstep = D_MODEL // bn
    return pl.pallas_call(
        _merge_kernel,
        grid=(D_MODEL // bn, t // bm),
        in_specs=[
            pl.BlockSpec((bm, A_W), lambda j, i: (i, 0)),
            pl.BlockSpec((bm, B_W), lambda j, i: (i, 0)),
            pl.BlockSpec((bm, C_W), lambda j, i: (i, 0)),
            pl.BlockSpec((bm, bn), lambda j, i: (i, j)),
            pl.BlockSpec((bm, bn), lambda j, i: (i, gstep + j)),
            pl.BlockSpec((bm, bn), lambda j, i: (i, 2 * gstep + j)),
            pl.BlockSpec((None, A_W, bn), lambda j, i: (layer, 0, j)),
            pl.BlockSpec((None, B_W, bn), lambda j, i: (layer, A_W // B_W, j)),
            pl.BlockSpec((None, C_W, bn), lambda j, i: (layer, (A_W + B_W) // C_W, j)),
        ],
        out_specs=pl.BlockSpec((bm, bn), lambda j, i: (i, j)),
        out_shape=jax.ShapeDtypeStruct((t, D_MODEL), jnp.bfloat16),
        scratch_shapes=[pltpu.VMEM((A_W, bn), jnp.bfloat16), pltpu.VMEM((B_W, bn), jnp.bfloat16),
                        pltpu.VMEM((C_W, bn), jnp.bfloat16)],
        compiler_params=_cparams(("arbitrary", "arbitrary")),
        name="merge",
    )(oa, ob, oc, gates, gates, gates, w_branch, w_branch, w_branch)


def _finish(y, x_ref, gp_ref, gn_ref, xo_ref, ho_ref):
    xn = x_ref[...] + _rms(y, gp_ref[...])
    xo_ref[...] = xn
    if ho_ref is not None:
        ho_ref[...] = _rms(xn, gn_ref[...]).astype(ho_ref.dtype)


def _gemm_res_kernel(nk, with_next, a_ref, w_ref, x_ref, gp_ref, gn_ref, xo_ref, *rest):
    ho_ref = rest[0] if with_next else None
    if nk == 1:
        y = jnp.dot(a_ref[...], w_ref[...], preferred_element_type=jnp.float32)
        _finish(y, x_ref, gp_ref, gn_ref, xo_ref, ho_ref)
        return
    acc = rest[-1]
    k = pl.program_id(1)

    @pl.when(k == 0)
    def _():
        acc[...] = jnp.zeros(acc.shape, jnp.float32)

    acc[...] += jnp.dot(a_ref[...], w_ref[...], preferred_element_type=jnp.float32)

    @pl.when(k == nk - 1)
    def _():
        _finish(acc[...], x_ref, gp_ref, gn_ref, xo_ref, ho_ref)


def _gemm_res(a, w, layer, x, g_post, g_next, bk, bm=512, name="gemm_res"):
    t, kdim = a.shape
    n = w.shape[2]
    nk = kdim // bk
    with_next = g_next is not None
    if g_next is None:
        g_next = g_post
    out_shape = [jax.ShapeDtypeStruct((t, n), jnp.float32)]
    out_specs = [pl.BlockSpec((bm, n), lambda i, k: (i, 0))]
    if with_next:
        out_shape.append(jax.ShapeDtypeStruct((t, n), jnp.bfloat16))
        out_specs.append(pl.BlockSpec((bm, n), lambda i, k: (i, 0)))
    res = pl.pallas_call(
        functools.partial(_gemm_res_kernel, nk, with_next),
        grid=(t // bm, nk),
        in_specs=[
            pl.BlockSpec((bm, bk), lambda i, k: (i, k)),
            pl.BlockSpec((None, bk, n), lambda i, k: (layer, k, 0)),
            pl.BlockSpec((bm, n), lambda i, k: (i, 0)),
            pl.BlockSpec((1, n), lambda i, k: (0, 0)),
            pl.BlockSpec((1, n), lambda i, k: (0, 0)),
        ],
        out_specs=out_specs,
        out_shape=out_shape,
        scratch_shapes=[pltpu.VMEM((bm, n), jnp.float32)] if nk > 1 else [],
        compiler_params=_cparams(("parallel", "arbitrary")),
        name=name,
    )(a, w, x, g_post, g_next)
    return (res[0], res[1]) if with_next else (res[0], None)


def _memkv_kernel(m_ref, g_ref, w_ref, o_ref):
    mn = _rms(m_ref[...], g_ref[...]).astype(jnp.bfloat16)
    o_ref[...] = jnp.dot(mn, w_ref[...], preferred_element_type=jnp.float32).astype(o_ref.dtype)


def _memkv(mem2, g, w, layer):
    t, d = mem2.shape
    n = w.shape[2]
    bm = 512
    return pl.pallas_call(
        _memkv_kernel,
        grid=(t // bm,),
        in_specs=[pl.BlockSpec((bm, d), lambda i: (i, 0)), pl.BlockSpec((1, d), lambda i: (0, 0)),
                  pl.BlockSpec((None, d, n), lambda i: (layer, 0, 0))],
        out_specs=pl.BlockSpec((bm, n), lambda i: (i, 0)),
        out_shape=jax.ShapeDtypeStruct((t, n), jnp.bfloat16),
        compiler_params=_cparams(("parallel",)),
        name="mem_kv",
    )(mem2, g, w)


XA_BM = 512


def _xattn_kernel(h_ref, wq_ref, kv_ref, wo_ref, x_ref, gp_ref, gn_ref, xo_ref, ho_ref):
    f32 = jnp.float32
    q = jnp.dot(h_ref[...], wq_ref[...], preferred_element_type=f32).astype(jnp.bfloat16)
    scale = MEM_HEAD_DIM ** -0.5
    outs = []
    for h in range(MEM_HEADS):
        cols = slice(h * LANES, (h + 1) * LANES)
        kh = kv_ref[:, cols]
        vh = kv_ref[:, MEM_W + h * LANES:MEM_W + (h + 1) * LANES]
        s = lax.dot_general(q[:, cols], kh, _NT, preferred_element_type=f32) * scale
        m = jnp.max(s, axis=-1, keepdims=True)
        e = jnp.exp(s - m)
        l = jnp.sum(e, axis=-1, keepdims=True)
        o = jnp.dot(e.astype(jnp.bfloat16), vh, preferred_element_type=f32)
        outs.append((o / l).astype(jnp.bfloat16))
    o = jnp.concatenate(outs, axis=-1)
    y = jnp.dot(o, wo_ref[...], preferred_element_type=f32)
    _finish(y, x_ref, gp_ref, gn_ref, xo_ref, ho_ref)


def _xattn(h, wq, kv, wo, layer, x, g_post, g_next):
    t, d = h.shape
    bm = XA_BM
    per_batch = SEQ // bm
    return pl.pallas_call(
        _xattn_kernel,
        grid=(t // bm,),
        in_specs=[
            pl.BlockSpec((bm, d), lambda i: (i, 0)),
            pl.BlockSpec((None, d, MEM_W), lambda i: (layer, 0, 0)),
            pl.BlockSpec((MEM_LEN, 2 * MEM_W), lambda i: (i // per_batch, 0)),
            pl.BlockSpec((None, MEM_W, d), lambda i: (layer, 0, 0)),
            pl.BlockSpec((bm, d), lambda i: (i, 0)),
            pl.BlockSpec((1, d), lambda i: (0, 0)),
            pl.BlockSpec((1, d), lambda i: (0, 0)),
        ],
        out_specs=[pl.BlockSpec((bm, d), lambda i: (i, 0)), pl.BlockSpec((bm, d), lambda i: (i, 0))],
        out_shape=[jax.ShapeDtypeStruct((t, d), jnp.float32), jax.ShapeDtypeStruct((t, d), jnp.bfloat16)],
        compiler_params=_cparams(("parallel",)),
        name="mem_xattn",
    )(h, wq, kv, wo, x, g_post, g_next)


def _ffn_up_kernel(h_ref, wgf_ref, wvf_ref, cwg_ref, cwv_ref, cbg_ref, cbv_ref, o_ref, wg_ref, wv_ref, ug, uv):
    j = pl.program_id(0)
    i = pl.program_id(1)
    tiles_per_seq = SEQ // FF_BM
    last = FF_P // FF_BN - 1
    valid = D_FF - last * FF_BN
    shift = FF_BN - valid

    @pl.when((i == 0) & (j < last))
    def _():
        wg_ref[...] = wgf_ref[...].astype(wg_ref.dtype)
        wv_ref[...] = wvf_ref[...].astype(wv_ref.dtype)

    @pl.when((i == 0) & (j == last))
    def _():
        zeros = jnp.zeros((wg_ref.shape[0], FF_BN - valid), wg_ref.dtype)
        wg_ref[:, :valid] = wgf_ref[:, :valid].astype(wg_ref.dtype)
        wv_ref[:, :valid] = wvf_ref[:, shift:].astype(wv_ref.dtype)
        wg_ref[:, valid:] = zeros
        wv_ref[:, valid:] = zeros

    for u in (ug, uv):
        @pl.when(i % tiles_per_seq == 0)
        def _():
            u[0:SUBLANES, :] = jnp.zeros((SUBLANES, FF_BN), jnp.float32)

        @pl.when(i % tiles_per_seq != 0)
        def _():
            u[0:SUBLANES, :] = u[FF_BM:FF_BM + SUBLANES, :]

    def conv(u, cw_ref, cb_ref, r0):
        base = SUBLANES + r0
        acc = cb_ref[...] + u[base - 2:base - 2 + FF_CH, :] * cw_ref[0:1, :]
        acc = acc + u[base - 1:base - 1 + FF_CH, :] * cw_ref[1:2, :]
        return acc + u[base:base + FF_CH, :] * cw_ref[2:3, :]

    for c in range(FF_BM // FF_CH):
        r0 = c * FF_CH
        hb = h_ref[r0:r0 + FF_CH, :]
        ug[SUBLANES + r0:SUBLANES + r0 + FF_CH, :] = jnp.dot(hb, wg_ref[...], preferred_element_type=jnp.float32)
        uv[SUBLANES + r0:SUBLANES + r0 + FF_CH, :] = jnp.dot(hb, wv_ref[...], preferred_element_type=jnp.float32)
        gate = conv(ug, cwg_ref, cbg_ref, r0)
        val = conv(uv, cwv_ref, cbv_ref, r0)
        o_ref[r0:r0 + FF_CH, :] = (jax.nn.gelu(gate) * val).astype(o_ref.dtype)


def _ffn_up(h, w_up, layer, conv_w, conv_b):
    t, d = h.shape
    nj = FF_P // FF_BN
    w_block = (None, pl.Element(d), pl.Element(FF_BN))
    return pl.pallas_call(
        _ffn_up_kernel,
        grid=(nj, t // FF_BM),
        in_specs=[
            pl.BlockSpec((FF_BM, d), lambda j, i: (i, 0)),
            pl.BlockSpec(w_block, lambda j, i: (layer, 0, pl.multiple_of(j * FF_BN, LANES))),
            pl.BlockSpec(w_block, lambda j, i: (
                layer, 0, pl.multiple_of(jnp.minimum(D_FF + j * FF_BN, 2 * D_FF - FF_BN), LANES))),
            pl.BlockSpec((CONV_W, FF_BN), lambda j, i: (0, j)),
            pl.BlockSpec((CONV_W, FF_BN), lambda j, i: (0, nj + j)),
            pl.BlockSpec((1, FF_BN), lambda j, i: (0, j)),
            pl.BlockSpec((1, FF_BN), lambda j, i: (0, nj + j)),
        ],
        out_specs=pl.BlockSpec((FF_BM, FF_BN), lambda j, i: (i, j)),
        out_shape=jax.ShapeDtypeStruct((t, FF_P), jnp.bfloat16),
        scratch_shapes=[pltpu.VMEM((d, FF_BN), jnp.bfloat16), pltpu.VMEM((d, FF_BN), jnp.bfloat16),
                        pltpu.VMEM((FF_BM + SUBLANES, FF_BN), jnp.float32),
                        pltpu.VMEM((FF_BM + SUBLANES, FF_BN), jnp.float32)],
        compiler_params=_cparams(("arbitrary", "arbitrary")),
        name="ffn_up",
    )(h, w_up, w_up, conv_w, conv_w, conv_b, conv_b)


def _pad_heads(w, axis):
    h = w.shape[0]
    zero = jnp.zeros_like(w)
    even = jnp.concatenate([w, zero], axis=axis)
    odd = jnp.concatenate([zero, w], axis=axis)
    sel = (jnp.arange(h) % 2 == 0).reshape((h, 1, 1))
    return jnp.where(sel, even, odd)


def _toeplitz(fn, rows, cols):
    ks = np.concatenate([np.arange(0, cols), np.arange(-(rows - 1), 0)])
    w = fn(ks)
    h, period = w.shape
    x = jnp.tile(w, (1, rows))[:, :rows * (period - 1)].reshape(h, rows, period - 1)
    return x[:, :, :cols].astype(jnp.float32)


def _band(rows, cols, left):
    diff = left + np.arange(rows)[:, None] // CHUNK - np.arange(cols)[None, :] // CHUNK
    return (diff >= 0) & (diff <= left)


def _bias_a(rel_bias):
    fn = lambda ks: rel_bias[np.clip(A_PAD - ks, -A_MAX_REL, A_MAX_REL) + A_MAX_REL].T
    bias = jnp.where(_band(QB, A_WIN, A_LEFT_CHUNKS)[None], _toeplitz(fn, QB, A_WIN), NEG)
    return bias.reshape(A_HEADS * QB, A_WIN)


def _bias_c(t5_c):
    fn = lambda ks: t5_c[_t5_bucket(jnp.asarray(ks - C_PAD, jnp.int32))].T
    bias = jnp.where(_band(QB, C_WIN, C_LEFT_CHUNKS)[None], _toeplitz(fn, QB, C_WIN), NEG)
    return bias.reshape(C_Q_HEADS * QB, C_WIN)


def _bias_b(t5_b):
    tiles = []
    for n in range(B_NEAR + 1):
        off = BQ * n if n < B_NEAR else SEQ
        fn = lambda ks, off=off: t5_b[_t5_bucket(jnp.asarray(-ks - off, jnp.int32))].T
        tile = _toeplitz(fn, KB, BQ)
        tiles.append(jnp.transpose(tile, (1, 0, 2)).reshape(KB, HQ))
    return jnp.stack(tiles)


def _pad_ff(a, dtype):
    z = jnp.zeros((a.shape[0], FF_P - D_FF), dtype)
    return jnp.concatenate([a[:, :D_FF].astype(dtype), z, a[:, D_FF:].astype(dtype), z], axis=1)


def kernel(x, mem, t5_table, norm_gains, w_in, a_rel_bias, ckv_gain, w_uk, w_uv, sinks, w_branch, w_o,
           mem_gain, w_mq, w_mkv, w_mo, w_up, conv_w, conv_b, w_down):
    bf16 = jnp.bfloat16
    xs = x.reshape(TOKENS, D_MODEL)
    mem2 = mem.reshape(BATCH * MEM_LEN, D_MODEL)
    tri = jnp.asarray(np.tril(np.ones((KB, KB), np.float32), -1), bf16)
    bias_b = _bias_b(t5_table[:, :B_HEADS])
    bias_c = _bias_c(t5_table[:, B_HEADS:])
    gains = norm_gains.reshape(DEPTH, 6, 1, D_MODEL)
    w_o_b, w_mq_b, w_mkv_b, w_mo_b = (w.astype(bf16) for w in (w_o, w_mq, w_mkv, w_mo))
    w_dn_b = jnp.concatenate([w_down.astype(bf16), jnp.zeros((DEPTH, FF_P - D_FF, D_MODEL), bf16)], axis=1)
    w_in_t = jnp.swapaxes(w_in, 1, 2)

    h = _norm(xs, gains[0, 0])
    for l in range(DEPTH):
        g = gains[l]
        proja = _proj(h, w_in_t, l, 0, IN_A, 1024, gate=False, name="in_proj_a")
        projc = _proj(h, w_in_t, l, O_QC, C_COLS, C_COLS, gate=False, name="in_proj_c")
        gates = _proj(h, w_in_t, l, O_GL, 3 * D_MODEL, 1024, gate=True, name="in_proj_g")
        proja3 = proja.reshape(BATCH, SEQ, IN_A)
        oa = _attn_a(proja3, _bias_a(a_rel_bias[l]))
        wuk = jnp.transpose(_pad_heads(w_uk[l], axis=2), (0, 2, 1)).astype(bf16)
        wuv = _pad_heads(w_uv[l], axis=2).astype(bf16)
        ob = _attn_b(proja3, ckv_gain[l].reshape(1, B_LATENT), wuk, wuv, bias_b, tri)
        oc = _attn_c(sinks[l], projc.reshape(BATCH, SEQ, C_COLS), bias_c)
        merged = _merge(oa.reshape(TOKENS, A_W), ob.reshape(TOKENS, B_W), oc.reshape(TOKENS, C_W), gates, w_branch, l)
        xs, h = _gemm_res(merged, w_o_b, l, xs, g[1], g[2], bk=D_MODEL, name="out_proj")
        kv = _memkv(mem2, mem_gain[l].reshape(1, D_MODEL), w_mkv_b, l)
        xs, h = _xattn(h, w_mq_b, kv, w_mo_b, l, xs, g[3], g[4])
        hidden = _ffn_up(h, w_up, l, _pad_ff(conv_w[l], jnp.float32), _pad_ff(conv_b[l].reshape(1, -1), jnp.float32))
        g_next = gains[l + 1, 0] if l + 1 < DEPTH else None
        xs, h = _gemm_res(hidden, w_dn_b, l, xs, g[5], g_next, bk=DOWN_BK, name="ffn_down")
    return xs.reshape(BATCH, SEQ, D_MODEL)
```

```python
import functools
import math

import numpy as np
import jax
import jax.numpy as jnp
from jax import lax
from jax.experimental import pallas as pl
from jax.experimental.pallas import tpu as pltpu

D_MODEL = 2048
BATCH = 4
SEQ = 2048
DEPTH = 2
TOKENS = BATCH * SEQ
CHUNK = 64
EPS = 1e-6
NEG = -1e30
LOG2E = math.log2(math.e)
A_HEADS = 8
A_LEFT_CHUNKS = 8
A_MAX_REL = 128
A_W = 512
B_HEADS = 8
B_W = 512
B_LATENT = 128
IDX_HEADS = 8
IDX_DIM = 64
TOPK = 256
C_Q_HEADS = 16
C_GROUP = 8
C_W = 1024
C_LEFT_CHUNKS = 2
T5_BUCKETS = 32
T5_MAX_DIST = 256
MEM_LEN = 256
MEM_HEADS = 4
MEM_HEAD_DIM = 128
MEM_W = 512
D_FF = 5504
CONV_W = 3

LANES = 128
SUBLANES = 8
HEAD_DIM = 64
QB = 128
KB = 256
VMEM_LIMIT = 56 * 1024 * 1024

O_QA, O_KA, O_VA, O_QB = 0, 512, 1024, 1536
O_CKV = 2048
O_QI = 2176
O_KI = 2688
O_WI = 2752
O_QC = 2760
O_GL = 4040
IN_W = O_GL + 3 * D_MODEL
IN_A = 3072
X_COL0 = 2048
C_COLS = O_GL - O_QC
PROJ_BM = 1024
PROJ_TR = 256

FF_P = 5632
FF_BN = 512
FF_BM = 1024
FF_CH = 256
DOWN_BK = 1408

A_WIN = (A_LEFT_CHUNKS + 2) * CHUNK
C_WIN = (C_LEFT_CHUNKS + 2) * CHUNK
A_PAD = A_LEFT_CHUNKS * CHUNK
C_PAD = C_LEFT_CHUNKS * CHUNK
BQ = 256
B_NEAR = 2
NKB = SEQ // KB
HQ = B_HEADS * BQ

_NT = (((1,), (1,)), ((), ()))
_TN = (((0,), (0,)), ((), ()))


def _cparams(sem):
    return pltpu.CompilerParams(dimension_semantics=sem, vmem_limit_bytes=VMEM_LIMIT)


def _t5_bucket(rel):
    half = T5_BUCKETS // 2
    max_exact = half // 2
    sign = jnp.where(rel > 0, half, 0)
    d = jnp.abs(rel)
    d_f = jnp.maximum(d, 1).astype(jnp.float32)
    large = max_exact + (jnp.log(d_f / max_exact) / math.log(T5_MAX_DIST / max_exact) * (half - max_exact)).astype(jnp.int32)
    large = jnp.minimum(large, half - 1)
    return sign + jnp.where(d < max_exact, d, large)


def _far_bucket_is_constant():
    d = np.arange(BQ * B_NEAR - (KB - 1), SEQ, dtype=np.float32)
    assert d[0] > T5_MAX_DIST
    large = 8 + (np.log(d / 8) / math.log(T5_MAX_DIST / 8) * 8).astype(np.int32)
    return bool(np.all(np.minimum(large, 15) == 15))


assert _far_bucket_is_constant()


def _rms(v, g):
    return v * lax.rsqrt(jnp.mean(v * v, axis=-1, keepdims=True) + EPS) * g


def _norm_kernel(x_ref, g_ref, o_ref):
    o_ref[...] = _rms(x_ref[...], g_ref[...]).astype(o_ref.dtype)


def _norm(x, g, bm=1024):
    t, d = x.shape
    return pl.pallas_call(
        _norm_kernel,
        grid=(t // bm,),
        in_specs=[pl.BlockSpec((bm, d), lambda i: (i, 0)), pl.BlockSpec((1, d), lambda i: (0, 0))],
        out_specs=pl.BlockSpec((bm, d), lambda i: (i, 0)),
        out_shape=jax.ShapeDtypeStruct((t, d), jnp.bfloat16),
        compiler_params=_cparams(("parallel",)),
        name="rmsnorm",
    )(x, g)


def _proj_kernel(gate, h_ref, wt_ref, o_ref, wb):
    @pl.when(pl.program_id(1) == 0)
    def _():
        for c in range(wt_ref.shape[0] // PROJ_TR):
            rows = slice(c * PROJ_TR, (c + 1) * PROJ_TR)
            wb[:, rows] = wt_ref[rows, :].T.astype(wb.dtype)

    acc = jnp.dot(h_ref[...], wb[...], preferred_element_type=jnp.float32)
    o_ref[...] = (jax.nn.sigmoid(acc) if gate else acc).astype(o_ref.dtype)


def _proj(h, w_t, layer, col0, n, bn, gate, name):
    t, k = h.shape
    assert col0 % SUBLANES == 0 and bn % SUBLANES == 0
    w_spec = pl.BlockSpec((None, pl.Element(bn), pl.Element(k)),
                          lambda j, i: (layer, pl.multiple_of(col0 + j * bn, SUBLANES), 0))
    return pl.pallas_call(
        functools.partial(_proj_kernel, gate),
        grid=(n // bn, t // PROJ_BM),
        in_specs=[pl.BlockSpec((PROJ_BM, k), lambda j, i: (i, 0)), w_spec],
        out_specs=pl.BlockSpec((PROJ_BM, bn), lambda j, i: (i, j)),
        out_shape=jax.ShapeDtypeStruct((t, n), jnp.bfloat16),
        scratch_shapes=[pltpu.VMEM((k, bn), jnp.bfloat16)],
        compiler_params=_cparams(("arbitrary", "arbitrary")),
        name=name,
    )(h, w_t)


def _lane_masks(dtype):
    lane = lax.broadcasted_iota(jnp.int32, (1, LANES), 1)
    lo = (lane < HEAD_DIM).astype(jnp.float32)
    return lo.astype(dtype), (1.0 - lo).astype(dtype)


def _swap_halves(x):
    return pltpu.roll(x.astype(jnp.float32), HEAD_DIM, 1).astype(x.dtype)


def _fill_padded(src, dst_ref, pad):
    dst_ref[0:pad, :] = jnp.zeros((pad, dst_ref.shape[1]), dst_ref.dtype)
    dst_ref[pad:pad + SEQ, :] = src


def _attn_a_kernel(q_ref, k_ref, v_ref, bias_ref, o_ref, kpad, vpad):
    i = pl.program_id(1)

    @pl.when(i == 0)
    def _():
        _fill_padded(k_ref[...], kpad, A_PAD)
        _fill_padded(v_ref[...], vpad, A_PAD)

    start = pl.multiple_of(i * QB, QB)
    kw = kpad[pl.ds(start, A_WIN), :]
    vw = vpad[pl.ds(start, A_WIN), :]
    mlo, mhi = _lane_masks(jnp.bfloat16)
    lane = lax.broadcasted_iota(jnp.int32, (QB, LANES), 1)
    kchunk = lax.broadcasted_iota(jnp.int32, (QB, A_WIN), 1) // CHUNK
    pad_mask = jnp.where(kchunk >= A_LEFT_CHUNKS - 2 * i, 0.0, NEG)
    scale = HEAD_DIM ** -0.5 * LOG2E
    pad2 = jnp.concatenate([pad_mask, pad_mask], axis=0)
    for p in range(A_HEADS // 2):
        cols = slice(p * LANES, (p + 1) * LANES)
        qp = q_ref[:, cols]
        q2 = jnp.concatenate([qp * mlo, qp * mhi], axis=0)
        s = lax.dot_general(q2, kw[:, cols], _NT, preferred_element_type=jnp.float32)
        s = s * scale + bias_ref[2 * p * QB:(2 * p + 2) * QB, :] + pad2
        m = jnp.max(s, axis=-1, keepdims=True)
        e = jnp.exp2(s - m)
        l = jnp.sum(e, axis=-1, keepdims=True)
        o = jnp.dot(e.astype(jnp.bfloat16), vw[:, cols], preferred_element_type=jnp.float32) / l
        o_ref[:, cols] = jnp.where(lane < HEAD_DIM, o[:QB], o[QB:]).astype(o_ref.dtype)


def _attn_a(proj3, bias):
    nq = SEQ // QB
    return pl.pallas_call(
        _attn_a_kernel,
        grid=(BATCH, nq),
        in_specs=[
            pl.BlockSpec((None, QB, A_W), lambda b, i: (b, i, O_QA // A_W)),
            pl.BlockSpec((None, SEQ, A_W), lambda b, i: (b, 0, O_KA // A_W)),
            pl.BlockSpec((None, SEQ, A_W), lambda b, i: (b, 0, O_VA // A_W)),
            pl.BlockSpec((A_HEADS * QB, A_WIN), lambda b, i: (0, 0)),
        ],
        out_specs=pl.BlockSpec((None, QB, A_W), lambda b, i: (b, i, 0)),
        out_shape=jax.ShapeDtypeStruct((BATCH, SEQ, A_W), jnp.bfloat16),
        scratch_shapes=[pltpu.VMEM((SEQ + A_PAD, A_W), jnp.bfloat16), pltpu.VMEM((SEQ + A_PAD, A_W), jnp.bfloat16)],
        compiler_params=_cparams(("arbitrary", "arbitrary")),
        name="attn_a",
    )(proj3, proj3, proj3, bias)


def _attn_c_kernel(sink_ref, q_ref, k_ref, v_ref, bias_ref, o_ref, kpad, kswp, vpad, vswp):
    i = pl.program_id(1)

    @pl.when(i == 0)
    def _():
        k = k_ref[...]
        v = v_ref[...]
        _fill_padded(k, kpad, C_PAD)
        _fill_padded(_swap_halves(k), kswp, C_PAD)
        _fill_padded(v, vpad, C_PAD)
        _fill_padded(_swap_halves(v), vswp, C_PAD)

    start = pl.multiple_of(i * QB, QB)
    mlo, mhi = _lane_masks(jnp.bfloat16)
    lane = lax.broadcasted_iota(jnp.int32, (QB, LANES), 1)
    kchunk = lax.broadcasted_iota(jnp.int32, (QB, C_WIN), 1) // CHUNK
    pad_mask = jnp.where(kchunk >= C_LEFT_CHUNKS - 2 * i, 0.0, NEG)
    scale = HEAD_DIM ** -0.5 * LOG2E
    qs = [q_ref[:, p * LANES:(p + 1) * LANES] for p in range(C_GROUP)]
    npair = C_GROUP // 2
    stacks = []
    for straight in (True, False):
        kref, vref = (kpad, vpad) if straight else (kswp, vswp)
        kw = kref[pl.ds(start, C_WIN), :]
        vw = vref[pl.ds(start, C_WIN), :]
        halves = [int((p >= npair) == straight) for p in range(C_GROUP)]
        qg = jnp.concatenate([qs[p] * (mhi if halves[p] else mlo) for p in range(C_GROUP)], axis=0)
        s_all = lax.dot_general(qg, kw, _NT, preferred_element_type=jnp.float32)
        ps, ls = [], []
        for p in range(C_GROUP):
            h = 2 * p + halves[p]
            s = s_all[p * QB:(p + 1) * QB] * scale + bias_ref[h * QB:(h + 1) * QB, :] + pad_mask
            sink = sink_ref[h] * LOG2E
            m = jnp.maximum(jnp.max(s, axis=-1, keepdims=True), sink)
            e = jnp.exp2(s - m)
            ls.append(jnp.sum(e, axis=-1, keepdims=True) + jnp.exp2(sink - m))
            ps.append(e.astype(jnp.bfloat16))
        o_all = jnp.dot(jnp.concatenate(ps, axis=0), vw, preferred_element_type=jnp.float32)
        stacks.append(([o_all[p * QB:(p + 1) * QB] / ls[p] for p in range(C_GROUP)], halves))
    for p in range(C_GROUP):
        (o1, h1), (o2, _) = stacks
        lo, hi = (o2[p], o1[p]) if h1[p] else (o1[p], o2[p])
        o_ref[:, p * LANES:(p + 1) * LANES] = jnp.where(lane < HEAD_DIM, lo, hi).astype(o_ref.dtype)


def _attn_c(sinks, projc3, bias):
    nq = SEQ // QB
    pad_buf = pltpu.VMEM((SEQ + C_PAD, LANES), jnp.bfloat16)
    return pl.pallas_call(
        _attn_c_kernel,
        grid=(BATCH, nq),
        in_specs=[
            pl.BlockSpec(memory_space=pltpu.SMEM),
            pl.BlockSpec((None, QB, C_W), lambda b, i: (b, i, 0)),
            pl.BlockSpec((None, SEQ, LANES), lambda b, i: (b, 0, C_W // LANES)),
            pl.BlockSpec((None, SEQ, LANES), lambda b, i: (b, 0, C_W // LANES + 1)),
            pl.BlockSpec((C_Q_HEADS * QB, C_WIN), lambda b, i: (0, 0)),
        ],
        out_specs=pl.BlockSpec((None, QB, C_W), lambda b, i: (b, i, 0)),
        out_shape=jax.ShapeDtypeStruct((BATCH, SEQ, C_W), jnp.bfloat16),
        scratch_shapes=[pad_buf, pad_buf, pad_buf, pad_buf],
        compiler_params=_cparams(("arbitrary", "arbitrary")),
        name="attn_c",
    )(sinks, projc3, projc3, projc3, bias)


def _sort_key(x):
    bits = lax.bitcast_convert_type(x + 0.0, jnp.int32)
    return bits ^ ((bits >> 31) & jnp.int32(0x7FFFFFFF))


def _rows_tree(x, op, slab):
    parts = [x[r:r + slab, :] for r in range(0, x.shape[0], slab)]
    while len(parts) > 1:
        nxt = [op(parts[k], parts[k + 1]) for k in range(0, len(parts) - 1, 2)]
        parts = nxt + ([parts[-1]] if len(parts) % 2 else [])
    return parts[0]


def _rows8(x, op):
    return _rows_tree(x, op, SUBLANES)


BF16_ROWS = 16


def _bit_search(count_ge, target, v, above, top_bit):
    def step(t, carry):
        v, above = carry
        cand = v | (jnp.int32(1) << (top_bit - t))
        c = count_ge(cand)
        keep = c >= target
        return jnp.where(keep, cand, v), jnp.where(keep, above, c)

    return lax.fori_loop(0, top_bit + 1, step, (v, above))


def _attn_b_kernel(qb_ref, x_ref, kiw_ref, ckv_ref, gain_ref, wuk_ref, wuv_ref, bias_ref, tri_ref,
                   o_ref, ckvn, kd, ql, qi_all, sk, dg, acc_s, p_s):
    i = pl.program_id(1)
    nkb = (i * BQ) // KB + 1
    mlo, mhi = _lane_masks(jnp.bfloat16)
    f32 = jnp.float32

    @pl.when(i == 0)
    def _():
        ckvn[...] = _rms(ckv_ref[...].astype(f32), gain_ref[...]).astype(ckvn.dtype)
        kiw = kiw_ref[...].astype(f32)
        lane = lax.broadcasted_iota(jnp.int32, kiw.shape, 1)
        kd[...] = jnp.where(lane < HEAD_DIM, kiw, pltpu.roll(kiw, HEAD_DIM, 1)).astype(kd.dtype)

    for h in range(B_HEADS):
        rows = slice(h * BQ, (h + 1) * BQ)
        qlat = jnp.dot(qb_ref[:, (h // 2) * LANES:(h // 2 + 1) * LANES], wuk_ref[h], preferred_element_type=f32)
        ql[rows, :] = (qlat * (HEAD_DIM ** -0.5 * LOG2E)).astype(ql.dtype)
        qcol = O_QI - X_COL0 + (h // 2) * LANES
        qi_all[rows, :] = x_ref[:, qcol:qcol + LANES] * (mhi if h % 2 else mlo)
    wcol = O_KI - X_COL0
    wi_t = x_ref[:, wcol:wcol + LANES].astype(f32).T * (IDX_HEADS ** -0.5 * IDX_DIM ** -0.5)
    wi_rows = [wi_t[O_WI - O_KI + h:O_WI - O_KI + h + 1, :] for h in range(IDX_HEADS)]

    kpos = lax.broadcasted_iota(jnp.int32, (KB, BQ), 0)
    key_limit = ((i * BQ + lax.broadcasted_iota(jnp.int32, (1, BQ), 1)) // CHUNK + 1) * CHUNK

    def admissible(kb):
        return kb * KB + kpos < key_limit

    def score_block(kb, carry):
        kblk = kd[pl.ds(pl.multiple_of(kb * KB, KB), KB), :]
        dots = lax.dot_general(kblk, qi_all[...], _NT, preferred_element_type=f32)
        score = jnp.zeros((KB, BQ), f32)
        for h in range(IDX_HEADS):
            score = score + jnp.maximum(dots[:, h * BQ:(h + 1) * BQ], 0.0) * wi_rows[h]
        sk[kb] = _sort_key(jnp.where(admissible(kb), score, NEG))
        return carry

    lax.fori_loop(0, nkb, score_block, 0)

    neg_key = _sort_key(jnp.full((1, 1), NEG, f32))
    n_rest = ((NKB - nkb) * KB).astype(f32)
    one, zero = jnp.ones((), jnp.bfloat16), jnp.zeros((), jnp.bfloat16)

    def byte_of(key, byte):
        return ((key >> 24) + 128) if byte == 3 else ((key >> (8 * byte)) & 255)

    target = jnp.full((1, BQ), TOPK, f32)
    prefix = jnp.zeros((1, BQ), jnp.int32)
    above = jnp.zeros((1, BQ), f32)
    for byte in (3, 2, 1, 0):
        def in_class(key, byte=byte, prefix=prefix):
            return (key >> (8 * byte + 8)) == prefix

        def prepare(kb, carry, byte=byte, in_class=in_class):
            key = sk[kb]
            digit = byte_of(key, byte).astype(f32)
            if byte < 3:
                digit = jnp.where(in_class(key), digit, -1.0)
            dg[kb] = digit.astype(dg.dtype)
            return carry

        lax.fori_loop(0, nkb, prepare, 0)
        rest_digit = byte_of(neg_key, byte)
        rest_on = in_class(neg_key) if byte < 3 else (neg_key == neg_key)

        def count_ge(cand, rest_digit=rest_digit, rest_on=rest_on):
            cand_b = cand.astype(f32).astype(jnp.bfloat16)

            def body(kb, acc):
                hit = jnp.where(dg[kb] >= cand_b, one, zero)
                return acc + _rows_tree(hit, jnp.add, BF16_ROWS).astype(f32)

            acc = lax.fori_loop(0, nkb, body, jnp.zeros((BF16_ROWS, BQ), f32))
            rest = jnp.where(rest_on & (rest_digit >= cand), n_rest, 0.0)
            return jnp.sum(acc, axis=0, keepdims=True) + rest

        digit_thr, above_here = _bit_search(count_ge, target, jnp.zeros((1, BQ), jnp.int32),
                                            jnp.zeros((1, BQ), f32), 7)
        prefix = (digit_thr - 128) if byte == 3 else (prefix * 256 + digit_thr)
        above = above + above_here
        target = target - above_here
    thr = prefix
    need = TOPK - above

    acc_s[...] = jnp.zeros(acc_s.shape, f32)

    def attend(kb, carry):
        eq_seen, m_old, l_old = carry
        keys = sk[kb]
        eq = keys == thr
        eq_f = jnp.where(eq, 1.0, 0.0)
        before = jnp.dot(tri_ref[...], eq_f.astype(jnp.bfloat16), preferred_element_type=f32)
        take_eq = jnp.where(eq, jnp.where(eq_seen + before < need, 1.0, 0.0), 0.0)
        take = jnp.where(keys > thr, 1.0, take_eq)
        mask_add = jnp.where(admissible(kb), jnp.where(take > 0.0, 0.0, NEG), NEG)
        cblk = ckvn[pl.ds(pl.multiple_of(kb * KB, KB), KB), :]
        dots = lax.dot_general(cblk, ql[...], _NT, preferred_element_type=f32)
        tile = jnp.minimum(i - (KB // BQ) * kb, B_NEAR)
        m_parts, l_parts, a_parts = [], [], []
        for g in range(HQ // LANES):
            cols = slice(g * LANES, (g + 1) * LANES)
            qcols = slice((g % (BQ // LANES)) * LANES, (g % (BQ // LANES) + 1) * LANES)
            s = dots[:, cols] + bias_ref[tile, :, cols] + mask_add[:, qcols]
            mo = m_old[:, cols]
            mn = jnp.maximum(mo, jnp.max(_rows8(s, jnp.maximum), axis=0, keepdims=True))
            a = jnp.exp2(mo - mn)
            e = jnp.exp2(s - mn)
            l_parts.append(a * l_old[:, cols] + jnp.sum(_rows8(e, jnp.add), axis=0, keepdims=True))
            m_parts.append(mn)
            a_parts.append(a)
            p_s[:, cols] = e.astype(p_s.dtype)
        alpha = jnp.concatenate(a_parts, axis=1)
        acc_s[...] = acc_s[...] * alpha + lax.dot_general(cblk, p_s[...], _TN, preferred_element_type=f32)
        eq_seen = eq_seen + jnp.sum(_rows8(eq_f, jnp.add), axis=0, keepdims=True)
        return eq_seen, jnp.concatenate(m_parts, axis=1), jnp.concatenate(l_parts, axis=1)

    init = (jnp.zeros((1, BQ), f32), jnp.full((1, HQ), 4 * NEG, f32), jnp.zeros((1, HQ), f32))
    _, _, l_fin = lax.fori_loop(0, nkb, attend, init)

    o_lat_t = acc_s[...] / l_fin
    for p in range(B_HEADS // 2):
        out = jnp.zeros((BQ, LANES), f32)
        for h in (2 * p, 2 * p + 1):
            o_lat = o_lat_t[:, h * BQ:(h + 1) * BQ].T.astype(jnp.bfloat16)
            out = out + jnp.dot(o_lat, wuv_ref[h], preferred_element_type=f32)
        o_ref[:, p * LANES:(p + 1) * LANES] = out.astype(o_ref.dtype)


def _attn_b(proj3, gain, wuk, wuv, bias, tri, nbatch=BATCH):
    nq = SEQ // BQ
    xw = IN_A - X_COL0
    return pl.pallas_call(
        _attn_b_kernel,
        grid=(nbatch, nq),
        in_specs=[
            pl.BlockSpec((None, BQ, B_W), lambda b, i: (b, i, O_QB // B_W)),
            pl.BlockSpec((None, BQ, xw), lambda b, i: (b, i, X_COL0 // xw)),
            pl.BlockSpec((None, SEQ, LANES), lambda b, i: (b, 0, O_KI // LANES)),
            pl.BlockSpec((None, SEQ, LANES), lambda b, i: (b, 0, O_CKV // LANES)),
            pl.BlockSpec((1, B_LATENT), lambda b, i: (0, 0)),
            pl.BlockSpec((B_HEADS, LANES, B_LATENT), lambda b, i: (0, 0, 0)),
            pl.BlockSpec((B_HEADS, B_LATENT, LANES), lambda b, i: (0, 0, 0)),
            pl.BlockSpec((B_NEAR + 1, KB, HQ), lambda b, i: (0, 0, 0)),
            pl.BlockSpec((KB, KB), lambda b, i: (0, 0)),
        ],
        out_specs=pl.BlockSpec((None, BQ, B_W), lambda b, i: (b, i, 0)),
        out_shape=jax.ShapeDtypeStruct((nbatch, SEQ, B_W), jnp.bfloat16),
        scratch_shapes=[
            pltpu.VMEM((SEQ, B_LATENT), jnp.bfloat16),
            pltpu.VMEM((SEQ, LANES), jnp.bfloat16),
            pltpu.VMEM((HQ, B_LATENT), jnp.bfloat16),
            pltpu.VMEM((HQ, LANES), jnp.bfloat16),
            pltpu.VMEM((NKB, KB, BQ), jnp.int32),
            pltpu.VMEM((NKB, KB, BQ), jnp.bfloat16),
            pltpu.VMEM((B_LATENT, HQ), jnp.float32),
            pltpu.VMEM((KB, HQ), jnp.bfloat16),
        ],
        compiler_params=_cparams(("arbitrary", "arbitrary")),
        name="attn_b",
    )(proj3, proj3, proj3, proj3, gain, wuk, wuv, bias, tri)


MERGE_BN = 1024
MERGE_BM = 1024


def _merge_kernel(oa_ref, ob_ref, oc_ref, ga_ref, gb_ref, gc_ref, wa_ref, wb_ref, wc_ref, o_ref, wa, wb, wc):
    f32 = jnp.float32

    @pl.when(pl.program_id(1) == 0)
    def _():
        wa[...] = wa_ref[...].astype(wa.dtype)
        wb[...] = wb_ref[...].astype(wb.dtype)
        wc[...] = wc_ref[...].astype(wc.dtype)

    m = ga_ref[...].astype(f32) * jnp.dot(oa_ref[...], wa[...], preferred_element_type=f32)
    m = m + gb_ref[...].astype(f32) * jnp.dot(ob_ref[...], wb[...], preferred_element_type=f32)
    m = m + gc_ref[...].astype(f32) * jnp.dot(oc_ref[...], wc[...], preferred_element_type=f32)
    o_ref[...] = m.astype(o_ref.dtype)


def _merge(oa, ob, oc, gates, w_branch, layer):
    t = oa.shape[0]
    bn, bm = MERGE_BN, MERGE_BM
    gstep = D_MODEL // bn
    return pl.pallas_call(
        _merge_kernel,
        grid=(D_MODEL // bn, t // bm),
        in_specs=[
            pl.BlockSpec((bm, A_W), lambda j, i: (i, 0)),
            pl.BlockSpec((bm, B_W), lambda j, i: (i, 0)),
            pl.BlockSpec((bm, C_W), lambda j, i: (i, 0)),
            pl.BlockSpec((bm, bn), lambda j, i: (i, j)),
            pl.BlockSpec((bm, bn), lambda j, i: (i, gstep + j)),
            pl.BlockSpec((bm, bn), lambda j, i: (i, 2 * gstep + j)),
            pl.BlockSpec((None, A_W, bn), lambda j, i: (layer, 0, j)),
            pl.BlockSpec((None, B_W, bn), lambda j, i: (layer, A_W // B_W, j)),
            pl.BlockSpec((None, C_W, bn), lambda j, i: (layer, (A_W + B_W) // C_W, j)),
        ],
        out_specs=pl.BlockSpec((bm, bn), lambda j, i: (i, j)),
        out_shape=jax.ShapeDtypeStruct((t, D_MODEL), jnp.bfloat16),
        scratch_shapes=[pltpu.VMEM((A_W, bn), jnp.bfloat16), pltpu.VMEM((B_W, bn), jnp.bfloat16),
                        pltpu.VMEM((C_W, bn), jnp.bfloat16)],
        compiler_params=_cparams(("arbitrary", "arbitrary")),
        name="merge",
    )(oa, ob, oc, gates, gates, gates, w_branch, w_branch, w_branch)


def _finish(y, x_ref, gp_ref, gn_ref, xo_ref, ho_ref):
    xn = x_ref[...] + _rms(y, gp_ref[...])
    xo_ref[...] = xn
    if ho_ref is not None:
        ho_ref[...] = _rms(xn, gn_ref[...]).astype(ho_ref.dtype)


def _gemm_res_kernel(nk, with_next, a_ref, w_ref, x_ref, gp_ref, gn_ref, xo_ref, *rest):
    ho_ref = rest[0] if with_next else None
    if nk == 1:
        y = jnp.dot(a_ref[...], w_ref[...], preferred_element_type=jnp.float32)
        _finish(y, x_ref, gp_ref, gn_ref, xo_ref, ho_ref)
        return
    acc = rest[-1]
    k = pl.program_id(1)

    @pl.when(k == 0)
    def _():
        acc[...] = jnp.zeros(acc.shape, jnp.float32)

    acc[...] += jnp.dot(a_ref[...], w_ref[...], preferred_element_type=jnp.float32)

    @pl.when(k == nk - 1)
    def _():
        _finish(acc[...], x_ref, gp_ref, gn_ref, xo_ref, ho_ref)


def _gemm_res(a, w, layer, x, g_post, g_next, bk, bm=512, name="gemm_res"):
    t, kdim = a.shape
    n = w.shape[2]
    nk = kdim // bk
    with_next = g_next is not None
    if g_next is None:
        g_next = g_post
    out_shape = [jax.ShapeDtypeStruct((t, n), jnp.float32)]
    out_specs = [pl.BlockSpec((bm, n), lambda i, k: (i, 0))]
    if with_next:
        out_shape.append(jax.ShapeDtypeStruct((t, n), jnp.bfloat16))
        out_specs.append(pl.BlockSpec((bm, n), lambda i, k: (i, 0)))
    res = pl.pallas_call(
        functools.partial(_gemm_res_kernel, nk, with_next),
        grid=(t // bm, nk),
        in_specs=[
            pl.BlockSpec((bm, bk), lambda i, k: (i, k)),
            pl.BlockSpec((None, bk, n), lambda i, k: (layer, k, 0)),
            pl.BlockSpec((bm, n), lambda i, k: (i, 0)),
            pl.BlockSpec((1, n), lambda i, k: (0, 0)),
            pl.BlockSpec((1, n), lambda i, k: (0, 0)),
        ],
        out_specs=out_specs,
        out_shape=out_shape,
        scratch_shapes=[pltpu.VMEM((bm, n), jnp.float32)] if nk > 1 else [],
        compiler_params=_cparams(("parallel", "arbitrary")),
        name=name,
    )(a, w, x, g_post, g_next)
    return (res[0], res[1]) if with_next else (res[0], None)


def _memkv_kernel(m_ref, g_ref, w_ref, o_ref):
    mn = _rms(m_ref[...], g_ref[...]).astype(jnp.bfloat16)
    o_ref[...] = jnp.dot(mn, w_ref[...], preferred_element_type=jnp.float32).astype(o_ref.dtype)


def _memkv(mem2, g, w, layer):
    t, d = mem2.shape
    n = w.shape[2]
    bm = 512
    return pl.pallas_call(
        _memkv_kernel,
        grid=(t // bm,),
        in_specs=[pl.BlockSpec((bm, d), lambda i: (i, 0)), pl.BlockSpec((1, d), lambda i: (0, 0)),
                  pl.BlockSpec((None, d, n), lambda i: (layer, 0, 0))],
        out_specs=pl.BlockSpec((bm, n), lambda i: (i, 0)),
        out_shape=jax.ShapeDtypeStruct((t, n), jnp.bfloat16),
        compiler_params=_cparams(("parallel",)),
        name="mem_kv",
    )(mem2, g, w)


XA_BM = 512


def _xattn_kernel(h_ref, wq_ref, kv_ref, wo_ref, x_ref, gp_ref, gn_ref, xo_ref, ho_ref):
    f32 = jnp.float32
    q = jnp.dot(h_ref[...], wq_ref[...], preferred_element_type=f32).astype(jnp.bfloat16)
    scale = MEM_HEAD_DIM ** -0.5 * LOG2E
    outs = []
    for h in range(MEM_HEADS):
        cols = slice(h * LANES, (h + 1) * LANES)
        kh = kv_ref[:, cols]
        vh = kv_ref[:, MEM_W + h * LANES:MEM_W + (h + 1) * LANES]
        s = lax.dot_general(q[:, cols], kh, _NT, preferred_element_type=f32) * scale
        m = jnp.max(s, axis=-1, keepdims=True)
        e = jnp.exp2(s - m)
        l = jnp.sum(e, axis=-1, keepdims=True)
        o = jnp.dot(e.astype(jnp.bfloat16), vh, preferred_element_type=f32)
        outs.append((o / l).astype(jnp.bfloat16))
    o = jnp.concatenate(outs, axis=-1)
    y = jnp.dot(o, wo_ref[...], preferred_element_type=f32)
    _finish(y, x_ref, gp_ref, gn_ref, xo_ref, ho_ref)


def _xattn(h, wq, kv, wo, layer, x, g_post, g_next):
    t, d = h.shape
    bm = XA_BM
    per_batch = SEQ // bm
    return pl.pallas_call(
        _xattn_kernel,
        grid=(t // bm,),
        in_specs=[
            pl.BlockSpec((bm, d), lambda i: (i, 0)),
            pl.BlockSpec((None, d, MEM_W), lambda i: (layer, 0, 0)),
            pl.BlockSpec((MEM_LEN, 2 * MEM_W), lambda i: (i // per_batch, 0)),
            pl.BlockSpec((None, MEM_W, d), lambda i: (layer, 0, 0)),
            pl.BlockSpec((bm, d), lambda i: (i, 0)),
            pl.BlockSpec((1, d), lambda i: (0, 0)),
            pl.BlockSpec((1, d), lambda i: (0, 0)),
        ],
        out_specs=[pl.BlockSpec((bm, d), lambda i: (i, 0)), pl.BlockSpec((bm, d), lambda i: (i, 0))],
        out_shape=[jax.ShapeDtypeStruct((t, d), jnp.float32), jax.ShapeDtypeStruct((t, d), jnp.bfloat16)],
        compiler_params=_cparams(("parallel",)),
        name="mem_xattn",
    )(h, wq, kv, wo, x, g_post, g_next)


def _ffn_up_kernel(h_ref, wgf_ref, wvf_ref, cwg_ref, cwv_ref, cbg_ref, cbv_ref, o_ref, wg_ref, wv_ref, ug, uv):
    j = pl.program_id(0)
    i = pl.program_id(1)
    tiles_per_seq = SEQ // FF_BM
    last = FF_P // FF_BN - 1
    valid = D_FF - last * FF_BN
    shift = FF_BN - valid

    @pl.when((i == 0) & (j < last))
    def _():
        wg_ref[...] = wgf_ref[...].astype(wg_ref.dtype)
        wv_ref[...] = wvf_ref[...].astype(wv_ref.dtype)

    @pl.when((i == 0) & (j == last))
    def _():
        zeros = jnp.zeros((wg_ref.shape[0], FF_BN - valid), wg_ref.dtype)
        wg_ref[:, :valid] = wgf_ref[:, :valid].astype(wg_ref.dtype)
        wv_ref[:, :valid] = wvf_ref[:, shift:].astype(wv_ref.dtype)
        wg_ref[:, valid:] = zeros
        wv_ref[:, valid:] = zeros

    for u in (ug, uv):
        @pl.when(i % tiles_per_seq == 0)
        def _():
            u[0:SUBLANES, :] = jnp.zeros((SUBLANES, FF_BN), jnp.float32)

        @pl.when(i % tiles_per_seq != 0)
        def _():
            u[0:SUBLANES, :] = u[FF_BM:FF_BM + SUBLANES, :]

    def conv(u, cw_ref, cb_ref, r0):
        base = SUBLANES + r0
        acc = cb_ref[...] + u[base - 2:base - 2 + FF_CH, :] * cw_ref[0:1, :]
        acc = acc + u[base - 1:base - 1 + FF_CH, :] * cw_ref[1:2, :]
        return acc + u[base:base + FF_CH, :] * cw_ref[2:3, :]

    for c in range(FF_BM // FF_CH):
        r0 = c * FF_CH
        hb = h_ref[r0:r0 + FF_CH, :]
        ug[SUBLANES + r0:SUBLANES + r0 + FF_CH, :] = jnp.dot(hb, wg_ref[...], preferred_element_type=jnp.float32)
        uv[SUBLANES + r0:SUBLANES + r0 + FF_CH, :] = jnp.dot(hb, wv_ref[...], preferred_element_type=jnp.float32)
        gate = conv(ug, cwg_ref, cbg_ref, r0)
        val = conv(uv, cwv_ref, cbv_ref, r0)
        o_ref[r0:r0 + FF_CH, :] = (jax.nn.gelu(gate) * val).astype(o_ref.dtype)


def _ffn_up(h, w_up, layer, conv_w, conv_b):
    t, d = h.shape
    nj = FF_P // FF_BN
    w_block = (None, pl.Element(d), pl.Element(FF_BN))
    return pl.pallas_call(
        _ffn_up_kernel,
        grid=(nj, t // FF_BM),
        in_specs=[
            pl.BlockSpec((FF_BM, d), lambda j, i: (i, 0)),
            pl.BlockSpec(w_block, lambda j, i: (layer, 0, pl.multiple_of(j * FF_BN, LANES))),
            pl.BlockSpec(w_block, lambda j, i: (
                layer, 0, pl.multiple_of(jnp.minimum(D_FF + j * FF_BN, 2 * D_FF - FF_BN), LANES))),
            pl.BlockSpec((CONV_W, FF_BN), lambda j, i: (0, j)),
            pl.BlockSpec((CONV_W, FF_BN), lambda j, i: (0, nj + j)),
            pl.BlockSpec((1, FF_BN), lambda j, i: (0, j)),
            pl.BlockSpec((1, FF_BN), lambda j, i: (0, nj + j)),
        ],
        out_specs=pl.BlockSpec((FF_BM, FF_BN), lambda j, i: (i, j)),
        out_shape=jax.ShapeDtypeStruct((t, FF_P), jnp.bfloat16),
        scratch_shapes=[pltpu.VMEM((d, FF_BN), jnp.bfloat16), pltpu.VMEM((d, FF_BN), jnp.bfloat16),
                        pltpu.VMEM((FF_BM + SUBLANES, FF_BN), jnp.float32),
                        pltpu.VMEM((FF_BM + SUBLANES, FF_BN), jnp.float32)],
        compiler_params=_cparams(("arbitrary", "arbitrary")),
        name="ffn_up",
    )(h, w_up, w_up, conv_w, conv_w, conv_b, conv_b)


def _pad_heads(w, axis):
    h = w.shape[0]
    zero = jnp.zeros_like(w)
    even = jnp.concatenate([w, zero], axis=axis)
    odd = jnp.concatenate([zero, w], axis=axis)
    sel = (jnp.arange(h) % 2 == 0).reshape((h, 1, 1))
    return jnp.where(sel, even, odd)


def _toeplitz(fn, rows, cols):
    ks = np.concatenate([np.arange(0, cols), np.arange(-(rows - 1), 0)])
    w = fn(ks)
    h, period = w.shape
    x = jnp.tile(w, (1, rows))[:, :rows * (period - 1)].reshape(h, rows, period - 1)
    return x[:, :, :cols].astype(jnp.float32)


def _band(rows, cols, left):
    diff = left + np.arange(rows)[:, None] // CHUNK - np.arange(cols)[None, :] // CHUNK
    return (diff >= 0) & (diff <= left)


def _bias_a(rel_bias):
    fn = lambda ks: rel_bias[np.clip(A_PAD - ks, -A_MAX_REL, A_MAX_REL) + A_MAX_REL].T
    bias = jnp.where(_band(QB, A_WIN, A_LEFT_CHUNKS)[None], _toeplitz(fn, QB, A_WIN) * LOG2E, NEG)
    return bias.reshape(A_HEADS * QB, A_WIN)


def _bias_c(t5_c):
    fn = lambda ks: t5_c[_t5_bucket(jnp.asarray(ks - C_PAD, jnp.int32))].T
    bias = jnp.where(_band(QB, C_WIN, C_LEFT_CHUNKS)[None], _toeplitz(fn, QB, C_WIN) * LOG2E, NEG)
    return bias.reshape(C_Q_HEADS * QB, C_WIN)


def _bias_b(t5_b):
    tiles = []
    for n in range(B_NEAR + 1):
        off = BQ * n if n < B_NEAR else SEQ
        fn = lambda ks, off=off: t5_b[_t5_bucket(jnp.asarray(-ks - off, jnp.int32))].T
        tile = _toeplitz(fn, KB, BQ)
        tiles.append(jnp.transpose(tile * LOG2E, (1, 0, 2)).reshape(KB, HQ))
    return jnp.stack(tiles)


def _pad_ff(a, dtype):
    z = jnp.zeros((a.shape[0], FF_P - D_FF), dtype)
    return jnp.concatenate([a[:, :D_FF].astype(dtype), z, a[:, D_FF:].astype(dtype), z], axis=1)


def kernel(x, mem, t5_table, norm_gains, w_in, a_rel_bias, ckv_gain, w_uk, w_uv, sinks, w_branch, w_o,
           mem_gain, w_mq, w_mkv, w_mo, w_up, conv_w, conv_b, w_down):
    bf16 = jnp.bfloat16
    xs = x.reshape(TOKENS, D_MODEL)
    mem2 = mem.reshape(BATCH * MEM_LEN, D_MODEL)
    tri = jnp.asarray(np.tril(np.ones((KB, KB), np.float32), -1), bf16)
    bias_b = _bias_b(t5_table[:, :B_HEADS])
    bias_c = _bias_c(t5_table[:, B_HEADS:])
    gains = norm_gains.reshape(DEPTH, 6, 1, D_MODEL)
    w_o_b, w_mq_b, w_mkv_b, w_mo_b = (w.astype(bf16) for w in (w_o, w_mq, w_mkv, w_mo))
    w_dn_b = jnp.concatenate([w_down.astype(bf16), jnp.zeros((DEPTH, FF_P - D_FF, D_MODEL), bf16)], axis=1)
    w_in_t = jnp.swapaxes(w_in, 1, 2)

    h = _norm(xs, gains[0, 0])
    for l in range(DEPTH):
        g = gains[l]
        proja = _proj(h, w_in_t, l, 0, IN_A, 1024, gate=False, name="in_proj_a")
        projc = _proj(h, w_in_t, l, O_QC, C_COLS, C_COLS, gate=False, name="in_proj_c")
        gates = _proj(h, w_in_t, l, O_GL, 3 * D_MODEL, 1024, gate=True, name="in_proj_g")
        proja3 = proja.reshape(BATCH, SEQ, IN_A)
        oa = _attn_a(proja3, _bias_a(a_rel_bias[l]))
        wuk = jnp.transpose(_pad_heads(w_uk[l], axis=2), (0, 2, 1)).astype(bf16)
        wuv = _pad_heads(w_uv[l], axis=2).astype(bf16)
        ob = _attn_b(proja3, ckv_gain[l].reshape(1, B_LATENT), wuk, wuv, bias_b, tri)
        oc = _attn_c(sinks[l], projc.reshape(BATCH, SEQ, C_COLS), bias_c)
        merged = _merge(oa.reshape(TOKENS, A_W), ob.reshape(TOKENS, B_W), oc.reshape(TOKENS, C_W), gates, w_branch, l)
        xs, h = _gemm_res(merged, w_o_b, l, xs, g[1], g[2], bk=D_MODEL, name="out_proj")
        kv = _memkv(mem2, mem_gain[l].reshape(1, D_MODEL), w_mkv_b, l)
        xs, h = _xattn(h, w_mq_b, kv, w_mo_b, l, xs, g[3], g[4])
        hidden = _ffn_up(h, w_up, l, _pad_ff(conv_w[l], jnp.float32), _pad_ff(conv_b[l].reshape(1, -1), jnp.float32))
        g_next = gains[l + 1, 0] if l + 1 < DEPTH else None
        xs, h = _gemm_res(hidden, w_dn_b, l, xs, g[5], g_next, bk=DOWN_BK, name="ffn_down")
    return xs.reshape(BATCH, SEQ, D_MODEL)
```

```python
import functools
import math

import numpy as np
import jax
import jax.numpy as jnp
from jax import lax
from jax.experimental import pallas as pl
from jax.experimental.pallas import tpu as pltpu

D_MODEL = 2048
BATCH = 4
SEQ = 2048
DEPTH = 2
TOKENS = BATCH * SEQ
CHUNK = 64
EPS = 1e-6
NEG = -1e30
LOG2E = math.log2(math.e)
A_HEADS = 8
A_LEFT_CHUNKS = 8
A_MAX_REL = 128
A_W = 512
B_HEADS = 8
B_W = 512
B_LATENT = 128
IDX_HEADS = 8
IDX_DIM = 64
TOPK = 256
C_Q_HEADS = 16
C_GROUP = 8
C_W = 1024
C_LEFT_CHUNKS = 2
T5_BUCKETS = 32
T5_MAX_DIST = 256
MEM_LEN = 256
MEM_HEADS = 4
MEM_HEAD_DIM = 128
MEM_W = 512
D_FF = 5504
CONV_W = 3

LANES = 128
SUBLANES = 8
HEAD_DIM = 64
QB = 128
KB = 256
VMEM_LIMIT = 56 * 1024 * 1024

O_QA, O_KA, O_VA, O_QB = 0, 512, 1024, 1536
O_CKV = 2048
O_QI = 2176
O_KI = 2688
O_WI = 2752
O_QC = 2760
O_GL = 4040
IN_W = O_GL + 3 * D_MODEL
IN_A = 3072
X_COL0 = 2048
C_COLS = O_GL - O_QC
PROJ_BM = 1024
PROJ_TR = 256

FF_P = 5632
FF_BN = 512
FF_BM = 1024
FF_CH = 256
DOWN_BK = 1408

A_WIN = (A_LEFT_CHUNKS + 2) * CHUNK
C_WIN = (C_LEFT_CHUNKS + 2) * CHUNK
A_PAD = A_LEFT_CHUNKS * CHUNK
C_PAD = C_LEFT_CHUNKS * CHUNK
BQ = 256
B_NEAR = 2
NKB = SEQ // KB
HQ = B_HEADS * BQ

_NT = (((1,), (1,)), ((), ()))
_TN = (((0,), (0,)), ((), ()))


def _cparams(sem):
    return pltpu.CompilerParams(dimension_semantics=sem, vmem_limit_bytes=VMEM_LIMIT)


def _t5_bucket(rel):
    half = T5_BUCKETS // 2
    max_exact = half // 2
    sign = jnp.where(rel > 0, half, 0)
    d = jnp.abs(rel)
    d_f = jnp.maximum(d, 1).astype(jnp.float32)
    large = max_exact + (jnp.log(d_f / max_exact) / math.log(T5_MAX_DIST / max_exact) * (half - max_exact)).astype(jnp.int32)
    large = jnp.minimum(large, half - 1)
    return sign + jnp.where(d < max_exact, d, large)


def _far_bucket_is_constant():
    d = np.arange(BQ * B_NEAR - (KB - 1), SEQ, dtype=np.float32)
    assert d[0] > T5_MAX_DIST
    large = 8 + (np.log(d / 8) / math.log(T5_MAX_DIST / 8) * 8).astype(np.int32)
    return bool(np.all(np.minimum(large, 15) == 15))


assert _far_bucket_is_constant()


def _rms(v, g):
    return v * lax.rsqrt(jnp.mean(v * v, axis=-1, keepdims=True) + EPS) * g


def _norm_kernel(x_ref, g_ref, o_ref):
    o_ref[...] = _rms(x_ref[...], g_ref[...]).astype(o_ref.dtype)


def _norm(x, g, bm=1024):
    t, d = x.shape
    return pl.pallas_call(
        _norm_kernel,
        grid=(t // bm,),
        in_specs=[pl.BlockSpec((bm, d), lambda i: (i, 0)), pl.BlockSpec((1, d), lambda i: (0, 0))],
        out_specs=pl.BlockSpec((bm, d), lambda i: (i, 0)),
        out_shape=jax.ShapeDtypeStruct((t, d), jnp.bfloat16),
        compiler_params=_cparams(("parallel",)),
        name="rmsnorm",
    )(x, g)


def _proj_kernel(gate, h_ref, wt_ref, o_ref, wb):
    @pl.when(pl.program_id(1) == 0)
    def _():
        for c in range(wt_ref.shape[0] // PROJ_TR):
            rows = slice(c * PROJ_TR, (c + 1) * PROJ_TR)
            wb[:, rows] = wt_ref[rows, :].T.astype(wb.dtype)

    acc = jnp.dot(h_ref[...], wb[...], preferred_element_type=jnp.float32)
    o_ref[...] = (jax.nn.sigmoid(acc) if gate else acc).astype(o_ref.dtype)


def _proj(h, w_t, layer, col0, n, bn, gate, name):
    t, k = h.shape
    assert col0 % SUBLANES == 0 and bn % SUBLANES == 0
    w_spec = pl.BlockSpec((None, pl.Element(bn), pl.Element(k)),
                          lambda j, i: (layer, pl.multiple_of(col0 + j * bn, SUBLANES), 0))
    return pl.pallas_call(
        functools.partial(_proj_kernel, gate),
        grid=(n // bn, t // PROJ_BM),
        in_specs=[pl.BlockSpec((PROJ_BM, k), lambda j, i: (i, 0)), w_spec],
        out_specs=pl.BlockSpec((PROJ_BM, bn), lambda j, i: (i, j)),
        out_shape=jax.ShapeDtypeStruct((t, n), jnp.bfloat16),
        scratch_shapes=[pltpu.VMEM((k, bn), jnp.bfloat16)],
        compiler_params=_cparams(("arbitrary", "arbitrary")),
        name=name,
    )(h, w_t)


def _lane_masks(dtype):
    lane = lax.broadcasted_iota(jnp.int32, (1, LANES), 1)
    lo = (lane < HEAD_DIM).astype(jnp.float32)
    return lo.astype(dtype), (1.0 - lo).astype(dtype)


def _swap_halves(x):
    return pltpu.roll(x.astype(jnp.float32), HEAD_DIM, 1).astype(x.dtype)


def _fill_padded(src, dst_ref, pad):
    dst_ref[0:pad, :] = jnp.zeros((pad, dst_ref.shape[1]), dst_ref.dtype)
    dst_ref[pad:pad + SEQ, :] = src


def _attn_a_kernel(q_ref, k_ref, v_ref, bias_ref, o_ref, kpad, vpad):
    i = pl.program_id(1)

    @pl.when(i == 0)
    def _():
        _fill_padded(k_ref[...], kpad, A_PAD)
        _fill_padded(v_ref[...], vpad, A_PAD)

    start = pl.multiple_of(i * QB, QB)
    kw = kpad[pl.ds(start, A_WIN), :]
    vw = vpad[pl.ds(start, A_WIN), :]
    mlo, mhi = _lane_masks(jnp.bfloat16)
    lane = lax.broadcasted_iota(jnp.int32, (QB, LANES), 1)
    kchunk = lax.broadcasted_iota(jnp.int32, (QB, A_WIN), 1) // CHUNK
    pad_mask = jnp.where(kchunk >= A_LEFT_CHUNKS - 2 * i, 0.0, NEG)
    scale = HEAD_DIM ** -0.5 * LOG2E
    pad2 = jnp.concatenate([pad_mask, pad_mask], axis=0)
    npair = A_HEADS // 2
    pcols = [slice(p * LANES, (p + 1) * LANES) for p in range(npair)]
    ss = []
    for p in range(npair):
        qp = q_ref[:, pcols[p]]
        q2 = jnp.concatenate([qp * mlo, qp * mhi], axis=0)
        s = lax.dot_general(q2, kw[:, pcols[p]], _NT, preferred_element_type=jnp.float32)
        ss.append(s * scale + bias_ref[2 * p * QB:(2 * p + 2) * QB, :] + pad2)
    ms = [jnp.max(s, axis=-1, keepdims=True) for s in ss]
    es = [jnp.exp2(s - m) for s, m in zip(ss, ms)]
    ls = [jnp.sum(e, axis=-1, keepdims=True) for e in es]
    outs = [jnp.dot(e.astype(jnp.bfloat16), vw[:, pcols[p]], preferred_element_type=jnp.float32)
            for p, e in enumerate(es)]
    for p in range(npair):
        o = outs[p] / ls[p]
        o_ref[:, pcols[p]] = jnp.where(lane < HEAD_DIM, o[:QB], o[QB:]).astype(o_ref.dtype)


def _attn_a(proj3, bias):
    nq = SEQ // QB
    return pl.pallas_call(
        _attn_a_kernel,
        grid=(BATCH, nq),
        in_specs=[
            pl.BlockSpec((None, QB, A_W), lambda b, i: (b, i, O_QA // A_W)),
            pl.BlockSpec((None, SEQ, A_W), lambda b, i: (b, 0, O_KA // A_W)),
            pl.BlockSpec((None, SEQ, A_W), lambda b, i: (b, 0, O_VA // A_W)),
            pl.BlockSpec((A_HEADS * QB, A_WIN), lambda b, i: (0, 0)),
        ],
        out_specs=pl.BlockSpec((None, QB, A_W), lambda b, i: (b, i, 0)),
        out_shape=jax.ShapeDtypeStruct((BATCH, SEQ, A_W), jnp.bfloat16),
        scratch_shapes=[pltpu.VMEM((SEQ + A_PAD, A_W), jnp.bfloat16), pltpu.VMEM((SEQ + A_PAD, A_W), jnp.bfloat16)],
        compiler_params=_cparams(("arbitrary", "arbitrary")),
        name="attn_a",
    )(proj3, proj3, proj3, bias)


def _attn_c_kernel(sink_ref, q_ref, k_ref, v_ref, bias_ref, o_ref, kpad, kswp, vpad, vswp):
    i = pl.program_id(1)

    @pl.when(i == 0)
    def _():
        k = k_ref[...]
        v = v_ref[...]
        _fill_padded(k, kpad, C_PAD)
        _fill_padded(_swap_halves(k), kswp, C_PAD)
        _fill_padded(v, vpad, C_PAD)
        _fill_padded(_swap_halves(v), vswp, C_PAD)

    start = pl.multiple_of(i * QB, QB)
    mlo, mhi = _lane_masks(jnp.bfloat16)
    lane = lax.broadcasted_iota(jnp.int32, (QB, LANES), 1)
    kchunk = lax.broadcasted_iota(jnp.int32, (QB, C_WIN), 1) // CHUNK
    pad_mask = jnp.where(kchunk >= C_LEFT_CHUNKS - 2 * i, 0.0, NEG)
    scale = HEAD_DIM ** -0.5 * LOG2E
    qs = [q_ref[:, p * LANES:(p + 1) * LANES] for p in range(C_GROUP)]
    npair = C_GROUP // 2
    stacks = []
    for straight in (True, False):
        kref, vref = (kpad, vpad) if straight else (kswp, vswp)
        kw = kref[pl.ds(start, C_WIN), :]
        vw = vref[pl.ds(start, C_WIN), :]
        halves = [int((p >= npair) == straight) for p in range(C_GROUP)]
        qg = jnp.concatenate([qs[p] * (mhi if halves[p] else mlo) for p in range(C_GROUP)], axis=0)
        s_all = lax.dot_general(qg, kw, _NT, preferred_element_type=jnp.float32)
        ps, ls = [], []
        for p in range(C_GROUP):
            h = 2 * p + halves[p]
            s = s_all[p * QB:(p + 1) * QB] * scale + bias_ref[h * QB:(h + 1) * QB, :] + pad_mask
            sink = sink_ref[h] * LOG2E
            m = jnp.maximum(jnp.max(s, axis=-1, keepdims=True), sink)
            e = jnp.exp2(s - m)
            ls.append(jnp.sum(e, axis=-1, keepdims=True) + jnp.exp2(sink - m))
            ps.append(e.astype(jnp.bfloat16))
        o_all = jnp.dot(jnp.concatenate(ps, axis=0), vw, preferred_element_type=jnp.float32)
        stacks.append(([o_all[p * QB:(p + 1) * QB] / ls[p] for p in range(C_GROUP)], halves))
    for p in range(C_GROUP):
        (o1, h1), (o2, _) = stacks
        lo, hi = (o2[p], o1[p]) if h1[p] else (o1[p], o2[p])
        o_ref[:, p * LANES:(p + 1) * LANES] = jnp.where(lane < HEAD_DIM, lo, hi).astype(o_ref.dtype)


def _attn_c(sinks, projc3, bias):
    nq = SEQ // QB
    pad_buf = pltpu.VMEM((SEQ + C_PAD, LANES), jnp.bfloat16)
    return pl.pallas_call(
        _attn_c_kernel,
        grid=(BATCH, nq),
        in_specs=[
            pl.BlockSpec(memory_space=pltpu.SMEM),
            pl.BlockSpec((None, QB, C_W), lambda b, i: (b, i, 0)),
            pl.BlockSpec((None, SEQ, LANES), lambda b, i: (b, 0, C_W // LANES)),
            pl.BlockSpec((None, SEQ, LANES), lambda b, i: (b, 0, C_W // LANES + 1)),
            pl.BlockSpec((C_Q_HEADS * QB, C_WIN), lambda b, i: (0, 0)),
        ],
        out_specs=pl.BlockSpec((None, QB, C_W), lambda b, i: (b, i, 0)),
        out_shape=jax.ShapeDtypeStruct((BATCH, SEQ, C_W), jnp.bfloat16),
        scratch_shapes=[pad_buf, pad_buf, pad_buf, pad_buf],
        compiler_params=_cparams(("arbitrary", "arbitrary")),
        name="attn_c",
    )(sinks, projc3, projc3, projc3, bias)


def _sort_key(x):
    bits = lax.bitcast_convert_type(x + 0.0, jnp.int32)
    return bits ^ ((bits >> 31) & jnp.int32(0x7FFFFFFF))


def _rows_tree(x, op, slab):
    parts = [x[r:r + slab, :] for r in range(0, x.shape[0], slab)]
    while len(parts) > 1:
        nxt = [op(parts[k], parts[k + 1]) for k in range(0, len(parts) - 1, 2)]
        parts = nxt + ([parts[-1]] if len(parts) % 2 else [])
    return parts[0]


def _rows8(x, op):
    return _rows_tree(x, op, SUBLANES)


BF16_ROWS = 16


def _bit_search(count_ge, target, v, above, top_bit):
    def step(t, carry):
        v, above = carry
        cand = v | (jnp.int32(1) << (top_bit - t))
        c = count_ge(cand)
        keep = c >= target
        return jnp.where(keep, cand, v), jnp.where(keep, above, c)

    return lax.fori_loop(0, top_bit + 1, step, (v, above))


def _attn_b_kernel(qb_ref, x_ref, kiw_ref, ckv_ref, gain_ref, wuk_ref, wuv_ref, bias_ref, tri_ref,
                   o_ref, ckvn, kd, ql, qi_all, sk, dg, acc_s, p_s):
    i = pl.program_id(1)
    nkb = (i * BQ) // KB + 1
    mlo, mhi = _lane_masks(jnp.bfloat16)
    f32 = jnp.float32

    @pl.when(i == 0)
    def _():
        ckvn[...] = _rms(ckv_ref[...].astype(f32), gain_ref[...]).astype(ckvn.dtype)
        kiw = kiw_ref[...].astype(f32)
        lane = lax.broadcasted_iota(jnp.int32, kiw.shape, 1)
        kd[...] = jnp.where(lane < HEAD_DIM, kiw, pltpu.roll(kiw, HEAD_DIM, 1)).astype(kd.dtype)

    for h in range(B_HEADS):
        rows = slice(h * BQ, (h + 1) * BQ)
        qlat = jnp.dot(qb_ref[:, (h // 2) * LANES:(h // 2 + 1) * LANES], wuk_ref[h], preferred_element_type=f32)
        ql[rows, :] = (qlat * (HEAD_DIM ** -0.5 * LOG2E)).astype(ql.dtype)
        qcol = O_QI - X_COL0 + (h // 2) * LANES
        qi_all[rows, :] = x_ref[:, qcol:qcol + LANES] * (mhi if h % 2 else mlo)
    wcol = O_KI - X_COL0
    wi_t = x_ref[:, wcol:wcol + LANES].astype(f32).T * (IDX_HEADS ** -0.5 * IDX_DIM ** -0.5)
    wi_rows = [wi_t[O_WI - O_KI + h:O_WI - O_KI + h + 1, :] for h in range(IDX_HEADS)]

    kpos = lax.broadcasted_iota(jnp.int32, (KB, BQ), 0)
    key_limit = ((i * BQ + lax.broadcasted_iota(jnp.int32, (1, BQ), 1)) // CHUNK + 1) * CHUNK

    def admissible(kb):
        return kb * KB + kpos < key_limit

    def score_block(kb, carry):
        kblk = kd[pl.ds(pl.multiple_of(kb * KB, KB), KB), :]
        dots = lax.dot_general(kblk, qi_all[...], _NT, preferred_element_type=f32)
        score = jnp.zeros((KB, BQ), f32)
        for h in range(IDX_HEADS):
            score = score + jnp.maximum(dots[:, h * BQ:(h + 1) * BQ], 0.0) * wi_rows[h]
        sk[kb] = _sort_key(jnp.where(admissible(kb), score, NEG))
        return carry

    lax.fori_loop(0, nkb, score_block, 0)

    neg_key = _sort_key(jnp.full((1, 1), NEG, f32))
    n_rest = ((NKB - nkb) * KB).astype(f32)
    one, zero = jnp.ones((), jnp.bfloat16), jnp.zeros((), jnp.bfloat16)

    def byte_of(key, byte):
        return ((key >> 24) + 128) if byte == 3 else ((key >> (8 * byte)) & 255)

    target = jnp.full((1, BQ), TOPK, f32)
    prefix = jnp.zeros((1, BQ), jnp.int32)
    above = jnp.zeros((1, BQ), f32)
    for byte in (3, 2, 1, 0):
        def in_class(key, byte=byte, prefix=prefix):
            return (key >> (8 * byte + 8)) == prefix

        def prepare(kb, carry, byte=byte, in_class=in_class):
            key = sk[kb]
            digit = byte_of(key, byte).astype(f32)
            if byte < 3:
                digit = jnp.where(in_class(key), digit, -1.0)
            dg[kb] = digit.astype(dg.dtype)
            return carry

        lax.fori_loop(0, nkb, prepare, 0)
        rest_digit = byte_of(neg_key, byte)
        rest_on = in_class(neg_key) if byte < 3 else (neg_key == neg_key)

        def count_ge(cand, rest_digit=rest_digit, rest_on=rest_on):
            cand_b = cand.astype(f32).astype(jnp.bfloat16)

            def body(kb, acc):
                hit = jnp.where(dg[kb] >= cand_b, one, zero)
                return acc + _rows_tree(hit, jnp.add, BF16_ROWS).astype(f32)

            acc = lax.fori_loop(0, nkb, body, jnp.zeros((BF16_ROWS, BQ), f32))
            rest = jnp.where(rest_on & (rest_digit >= cand), n_rest, 0.0)
            return jnp.sum(acc, axis=0, keepdims=True) + rest

        digit_thr, above_here = _bit_search(count_ge, target, jnp.zeros((1, BQ), jnp.int32),
                                            jnp.zeros((1, BQ), f32), 7)
        prefix = (digit_thr - 128) if byte == 3 else (prefix * 256 + digit_thr)
        above = above + above_here
        target = target - above_here
    thr = prefix
    need = TOPK - above

    acc_s[...] = jnp.zeros(acc_s.shape, f32)

    def attend(kb, carry):
        eq_seen, m_old, l_old = carry
        keys = sk[kb]
        eq = keys == thr
        eq_f = jnp.where(eq, 1.0, 0.0)
        before = jnp.dot(tri_ref[...], eq_f.astype(jnp.bfloat16), preferred_element_type=f32)
        take_eq = jnp.where(eq, jnp.where(eq_seen + before < need, 1.0, 0.0), 0.0)
        take = jnp.where(keys > thr, 1.0, take_eq)
        mask_add = jnp.where(admissible(kb), jnp.where(take > 0.0, 0.0, NEG), NEG)
        cblk = ckvn[pl.ds(pl.multiple_of(kb * KB, KB), KB), :]
        dots = lax.dot_general(cblk, ql[...], _NT, preferred_element_type=f32)
        tile = jnp.minimum(i - (KB // BQ) * kb, B_NEAR)
        m_parts, l_parts, a_parts = [], [], []
        for g in range(HQ // LANES):
            cols = slice(g * LANES, (g + 1) * LANES)
            qcols = slice((g % (BQ // LANES)) * LANES, (g % (BQ // LANES) + 1) * LANES)
            s = dots[:, cols] + bias_ref[tile, :, cols] + mask_add[:, qcols]
            mo = m_old[:, cols]
            mn = jnp.maximum(mo, jnp.max(_rows8(s, jnp.maximum), axis=0, keepdims=True))
            a = jnp.exp2(mo - mn)
            e = jnp.exp2(s - mn)
            l_parts.append(a * l_old[:, cols] + jnp.sum(_rows8(e, jnp.add), axis=0, keepdims=True))
            m_parts.append(mn)
            a_parts.append(a)
            p_s[:, cols] = e.astype(p_s.dtype)
        alpha = jnp.concatenate(a_parts, axis=1)
        acc_s[...] = acc_s[...] * alpha + lax.dot_general(cblk, p_s[...], _TN, preferred_element_type=f32)
        eq_seen = eq_seen + jnp.sum(_rows8(eq_f, jnp.add), axis=0, keepdims=True)
        return eq_seen, jnp.concatenate(m_parts, axis=1), jnp.concatenate(l_parts, axis=1)

    init = (jnp.zeros((1, BQ), f32), jnp.full((1, HQ), 4 * NEG, f32), jnp.zeros((1, HQ), f32))
    _, _, l_fin = lax.fori_loop(0, nkb, attend, init)

    o_lat_t = acc_s[...] / l_fin
    for p in range(B_HEADS // 2):
        out = jnp.zeros((BQ, LANES), f32)
        for h in (2 * p, 2 * p + 1):
            o_lat = o_lat_t[:, h * BQ:(h + 1) * BQ].T.astype(jnp.bfloat16)
            out = out + jnp.dot(o_lat, wuv_ref[h], preferred_element_type=f32)
        o_ref[:, p * LANES:(p + 1) * LANES] = out.astype(o_ref.dtype)


def _attn_b(proj3, gain, wuk, wuv, bias, tri, nbatch=BATCH):
    nq = SEQ // BQ
    xw = IN_A - X_COL0
    return pl.pallas_call(
        _attn_b_kernel,
        grid=(nbatch, nq),
        in_specs=[
            pl.BlockSpec((None, BQ, B_W), lambda b, i: (b, i, O_QB // B_W)),
            pl.BlockSpec((None, BQ, xw), lambda b, i: (b, i, X_COL0 // xw)),
            pl.BlockSpec((None, SEQ, LANES), lambda b, i: (b, 0, O_KI // LANES)),
            pl.BlockSpec((None, SEQ, LANES), lambda b, i: (b, 0, O_CKV // LANES)),
            pl.BlockSpec((1, B_LATENT), lambda b, i: (0, 0)),
            pl.BlockSpec((B_HEADS, LANES, B_LATENT), lambda b, i: (0, 0, 0)),
            pl.BlockSpec((B_HEADS, B_LATENT, LANES), lambda b, i: (0, 0, 0)),
            pl.BlockSpec((B_NEAR + 1, KB, HQ), lambda b, i: (0, 0, 0)),
            pl.BlockSpec((KB, KB), lambda b, i: (0, 0)),
        ],
        out_specs=pl.BlockSpec((None, BQ, B_W), lambda b, i: (b, i, 0)),
        out_shape=jax.ShapeDtypeStruct((nbatch, SEQ, B_W), jnp.bfloat16),
        scratch_shapes=[
            pltpu.VMEM((SEQ, B_LATENT), jnp.bfloat16),
            pltpu.VMEM((SEQ, LANES), jnp.bfloat16),
            pltpu.VMEM((HQ, B_LATENT), jnp.bfloat16),
            pltpu.VMEM((HQ, LANES), jnp.bfloat16),
            pltpu.VMEM((NKB, KB, BQ), jnp.int32),
            pltpu.VMEM((NKB, KB, BQ), jnp.bfloat16),
            pltpu.VMEM((B_LATENT, HQ), jnp.float32),
            pltpu.VMEM((KB, HQ), jnp.bfloat16),
        ],
        compiler_params=_cparams(("arbitrary", "arbitrary")),
        name="attn_b",
    )(proj3, proj3, proj3, proj3, gain, wuk, wuv, bias, tri)


MERGE_BN = 1024
MERGE_BM = 1024


def _merge_kernel(oa_ref, ob_ref, oc_ref, ga_ref, gb_ref, gc_ref, wa_ref, wb_ref, wc_ref, o_ref, wa, wb, wc):
    f32 = jnp.float32

    @pl.when(pl.program_id(1) == 0)
    def _():
        wa[...] = wa_ref[...].astype(wa.dtype)
        wb[...] = wb_ref[...].astype(wb.dtype)
        wc[...] = wc_ref[...].astype(wc.dtype)

    m = ga_ref[...].astype(f32) * jnp.dot(oa_ref[...], wa[...], preferred_element_type=f32)
    m = m + gb_ref[...].astype(f32) * jnp.dot(ob_ref[...], wb[...], preferred_element_type=f32)
    m = m + gc_ref[...].astype(f32) * jnp.dot(oc_ref[...], wc[...], preferred_element_type=f32)
    o_ref[...] = m.astype(o_ref.dtype)


def _merge(oa, ob, oc, gates, w_branch, layer):
    t = oa.shape[0]
    bn, bm = MERGE_BN, MERGE_BM
    gstep = D_MODEL // bn
    return pl.pallas_call(
        _merge_kernel,
        grid=(D_MODEL // bn, t // bm),
        in_specs=[
            pl.BlockSpec((bm, A_W), lambda j, i: (i, 0)),
            pl.BlockSpec((bm, B_W), lambda j, i: (i, 0)),
            pl.BlockSpec((bm, C_W), lambda j, i: (i, 0)),
            pl.BlockSpec((bm, bn), lambda j, i: (i, j)),
            pl.BlockSpec((bm, bn), lambda j, i: (i, gstep + j)),
            pl.BlockSpec((bm, bn), lambda j, i: (i, 2 * gstep + j)),
            pl.BlockSpec((None, A_W, bn), lambda j, i: (layer, 0, j)),
            pl.BlockSpec((None, B_W, bn), lambda j, i: (layer, A_W // B_W, j)),
            pl.BlockSpec((None, C_W, bn), lambda j, i: (layer, (A_W + B_W) // C_W, j)),
        ],
        out_specs=pl.BlockSpec((bm, bn), lambda j, i: (i, j)),
        out_shape=jax.ShapeDtypeStruct((t, D_MODEL), jnp.bfloat16),
        scratch_shapes=[pltpu.VMEM((A_W, bn), jnp.bfloat16), pltpu.VMEM((B_W, bn), jnp.bfloat16),
                        pltpu.VMEM((C_W, bn), jnp.bfloat16)],
        compiler_params=_cparams(("arbitrary", "arbitrary")),
        name="merge",
    )(oa, ob, oc, gates, gates, gates, w_branch, w_branch, w_branch)


def _finish(y, x_ref, gp_ref, gn_ref, xo_ref, ho_ref):
    xn = x_ref[...] + _rms(y, gp_ref[...])
    xo_ref[...] = xn
    if ho_ref is not None:
        ho_ref[...] = _rms(xn, gn_ref[...]).astype(ho_ref.dtype)


def _gemm_res_kernel(nk, with_next, a_ref, w_ref, x_ref, gp_ref, gn_ref, xo_ref, *rest):
    ho_ref = rest[0] if with_next else None
    if nk == 1:
        y = jnp.dot(a_ref[...], w_ref[...], preferred_element_type=jnp.float32)
        _finish(y, x_ref, gp_ref, gn_ref, xo_ref, ho_ref)
        return
    acc = rest[-1]
    k = pl.program_id(1)

    @pl.when(k == 0)
    def _():
        acc[...] = jnp.zeros(acc.shape, jnp.float32)

    acc[...] += jnp.dot(a_ref[...], w_ref[...], preferred_element_type=jnp.float32)

    @pl.when(k == nk - 1)
    def _():
        _finish(acc[...], x_ref, gp_ref, gn_ref, xo_ref, ho_ref)


def _gemm_res(a, w, layer, x, g_post, g_next, bk, bm=512, name="gemm_res"):
    t, kdim = a.shape
    n = w.shape[2]
    nk = kdim // bk
    with_next = g_next is not None
    if g_next is None:
        g_next = g_post
    out_shape = [jax.ShapeDtypeStruct((t, n), jnp.float32)]
    out_specs = [pl.BlockSpec((bm, n), lambda i, k: (i, 0))]
    if with_next:
        out_shape.append(jax.ShapeDtypeStruct((t, n), jnp.bfloat16))
        out_specs.append(pl.BlockSpec((bm, n), lambda i, k: (i, 0)))
    res = pl.pallas_call(
        functools.partial(_gemm_res_kernel, nk, with_next),
        grid=(t // bm, nk),
        in_specs=[
            pl.BlockSpec((bm, bk), lambda i, k: (i, k)),
            pl.BlockSpec((None, bk, n), lambda i, k: (layer, k, 0)),
            pl.BlockSpec((bm, n), lambda i, k: (i, 0)),
            pl.BlockSpec((1, n), lambda i, k: (0, 0)),
            pl.BlockSpec((1, n), lambda i, k: (0, 0)),
        ],
        out_specs=out_specs,
        out_shape=out_shape,
        scratch_shapes=[pltpu.VMEM((bm, n), jnp.float32)] if nk > 1 else [],
        compiler_params=_cparams(("parallel", "arbitrary")),
        name=name,
    )(a, w, x, g_post, g_next)
    return (res[0], res[1]) if with_next else (res[0], None)


def _memkv_kernel(m_ref, g_ref, w_ref, o_ref):
    mn = _rms(m_ref[...], g_ref[...]).astype(jnp.bfloat16)
    o_ref[...] = jnp.dot(mn, w_ref[...], preferred_element_type=jnp.float32).astype(o_ref.dtype)


def _memkv(mem2, g, w, layer):
    t, d = mem2.shape
    n = w.shape[2]
    bm = 512
    return pl.pallas_call(
        _memkv_kernel,
        grid=(t // bm,),
        in_specs=[pl.BlockSpec((bm, d), lambda i: (i, 0)), pl.BlockSpec((1, d), lambda i: (0, 0)),
                  pl.BlockSpec((None, d, n), lambda i: (layer, 0, 0))],
        out_specs=pl.BlockSpec((bm, n), lambda i: (i, 0)),
        out_shape=jax.ShapeDtypeStruct((t, n), jnp.bfloat16),
        compiler_params=_cparams(("parallel",)),
        name="mem_kv",
    )(mem2, g, w)


XA_BM = 512


def _xattn_kernel(h_ref, wq_ref, kv_ref, wo_ref, x_ref, gp_ref, gn_ref, xo_ref, ho_ref):
    f32 = jnp.float32
    q = jnp.dot(h_ref[...], wq_ref[...], preferred_element_type=f32).astype(jnp.bfloat16)
    scale = MEM_HEAD_DIM ** -0.5 * LOG2E
    hcols = [slice(h * LANES, (h + 1) * LANES) for h in range(MEM_HEADS)]
    ss = [lax.dot_general(q[:, c], kv_ref[:, c], _NT, preferred_element_type=f32) * scale for c in hcols]
    ms = [jnp.max(s, axis=-1, keepdims=True) for s in ss]
    es = [jnp.exp2(s - m) for s, m in zip(ss, ms)]
    ls = [jnp.sum(e, axis=-1, keepdims=True) for e in es]
    pv = [jnp.dot(e.astype(jnp.bfloat16), kv_ref[:, MEM_W + h * LANES:MEM_W + (h + 1) * LANES],
                  preferred_element_type=f32) for h, e in enumerate(es)]
    o = jnp.concatenate([(o_h / l).astype(jnp.bfloat16) for o_h, l in zip(pv, ls)], axis=-1)
    y = jnp.dot(o, wo_ref[...], preferred_element_type=f32)
    _finish(y, x_ref, gp_ref, gn_ref, xo_ref, ho_ref)


def _xattn(h, wq, kv, wo, layer, x, g_post, g_next):
    t, d = h.shape
    bm = XA_BM
    per_batch = SEQ // bm
    return pl.pallas_call(
        _xattn_kernel,
        grid=(t // bm,),
        in_specs=[
            pl.BlockSpec((bm, d), lambda i: (i, 0)),
            pl.BlockSpec((None, d, MEM_W), lambda i: (layer, 0, 0)),
            pl.BlockSpec((MEM_LEN, 2 * MEM_W), lambda i: (i // per_batch, 0)),
            pl.BlockSpec((None, MEM_W, d), lambda i: (layer, 0, 0)),
            pl.BlockSpec((bm, d), lambda i: (i, 0)),
            pl.BlockSpec((1, d), lambda i: (0, 0)),
            pl.BlockSpec((1, d), lambda i: (0, 0)),
        ],
        out_specs=[pl.BlockSpec((bm, d), lambda i: (i, 0)), pl.BlockSpec((bm, d), lambda i: (i, 0))],
        out_shape=[jax.ShapeDtypeStruct((t, d), jnp.float32), jax.ShapeDtypeStruct((t, d), jnp.bfloat16)],
        compiler_params=_cparams(("parallel",)),
        name="mem_xattn",
    )(h, wq, kv, wo, x, g_post, g_next)


def _ffn_up_kernel(h_ref, wgf_ref, wvf_ref, cwg_ref, cwv_ref, cbg_ref, cbv_ref, o_ref, wg_ref, wv_ref, ug, uv):
    j = pl.program_id(0)
    i = pl.program_id(1)
    tiles_per_seq = SEQ // FF_BM
    last = FF_P // FF_BN - 1
    valid = D_FF - last * FF_BN
    shift = FF_BN - valid

    @pl.when((i == 0) & (j < last))
    def _():
        wg_ref[...] = wgf_ref[...].astype(wg_ref.dtype)
        wv_ref[...] = wvf_ref[...].astype(wv_ref.dtype)

    @pl.when((i == 0) & (j == last))
    def _():
        zeros = jnp.zeros((wg_ref.shape[0], FF_BN - valid), wg_ref.dtype)
        wg_ref[:, :valid] = wgf_ref[:, :valid].astype(wg_ref.dtype)
        wv_ref[:, :valid] = wvf_ref[:, shift:].astype(wv_ref.dtype)
        wg_ref[:, valid:] = zeros
        wv_ref[:, valid:] = zeros

    for u in (ug, uv):
        @pl.when(i % tiles_per_seq == 0)
        def _():
            u[0:SUBLANES, :] = jnp.zeros((SUBLANES, FF_BN), jnp.float32)

        @pl.when(i % tiles_per_seq != 0)
        def _():
            u[0:SUBLANES, :] = u[FF_BM:FF_BM + SUBLANES, :]

    def conv(u, cw_ref, cb_ref, r0):
        base = SUBLANES + r0
        acc = cb_ref[...] + u[base - 2:base - 2 + FF_CH, :] * cw_ref[0:1, :]
        acc = acc + u[base - 1:base - 1 + FF_CH, :] * cw_ref[1:2, :]
        return acc + u[base:base + FF_CH, :] * cw_ref[2:3, :]

    for c in range(FF_BM // FF_CH):
        r0 = c * FF_CH
        hb = h_ref[r0:r0 + FF_CH, :]
        ug[SUBLANES + r0:SUBLANES + r0 + FF_CH, :] = jnp.dot(hb, wg_ref[...], preferred_element_type=jnp.float32)
        uv[SUBLANES + r0:SUBLANES + r0 + FF_CH, :] = jnp.dot(hb, wv_ref[...], preferred_element_type=jnp.float32)
        gate = conv(ug, cwg_ref, cbg_ref, r0)
        val = conv(uv, cwv_ref, cbv_ref, r0)
        o_ref[r0:r0 + FF_CH, :] = (jax.nn.gelu(gate) * val).astype(o_ref.dtype)


def _ffn_up(h, w_up, layer, conv_w, conv_b):
    t, d = h.shape
    nj = FF_P // FF_BN
    w_block = (None, pl.Element(d), pl.Element(FF_BN))
    return pl.pallas_call(
        _ffn_up_kernel,
        grid=(nj, t // FF_BM),
        in_specs=[
            pl.BlockSpec((FF_BM, d), lambda j, i: (i, 0)),
            pl.BlockSpec(w_block, lambda j, i: (layer, 0, pl.multiple_of(j * FF_BN, LANES))),
            pl.BlockSpec(w_block, lambda j, i: (
                layer, 0, pl.multiple_of(jnp.minimum(D_FF + j * FF_BN, 2 * D_FF - FF_BN), LANES))),
            pl.BlockSpec((CONV_W, FF_BN), lambda j, i: (0, j)),
            pl.BlockSpec((CONV_W, FF_BN), lambda j, i: (0, nj + j)),
            pl.BlockSpec((1, FF_BN), lambda j, i: (0, j)),
            pl.BlockSpec((1, FF_BN), lambda j, i: (0, nj + j)),
        ],
        out_specs=pl.BlockSpec((FF_BM, FF_BN), lambda j, i: (i, j)),
        out_shape=jax.ShapeDtypeStruct((t, FF_P), jnp.bfloat16),
        scratch_shapes=[pltpu.VMEM((d, FF_BN), jnp.bfloat16), pltpu.VMEM((d, FF_BN), jnp.bfloat16),
                        pltpu.VMEM((FF_BM + SUBLANES, FF_BN), jnp.float32),
                        pltpu.VMEM((FF_BM + SUBLANES, FF_BN), jnp.float32)],
        compiler_params=_cparams(("arbitrary", "arbitrary")),
        name="ffn_up",
    )(h, w_up, w_up, conv_w, conv_w, conv_b, conv_b)


def _pad_heads(w, axis):
    h = w.shape[0]
    zero = jnp.zeros_like(w)
    even = jnp.concatenate([w, zero], axis=axis)
    odd = jnp.concatenate([zero, w], axis=axis)
    sel = (jnp.arange(h) % 2 == 0).reshape((h, 1, 1))
    return jnp.where(sel, even, odd)


def _toeplitz(fn, rows, cols):
    ks = np.concatenate([np.arange(0, cols), np.arange(-(rows - 1), 0)])
    w = fn(ks)
    h, period = w.shape
    x = jnp.tile(w, (1, rows))[:, :rows * (period - 1)].reshape(h, rows, period - 1)
    return x[:, :, :cols].astype(jnp.float32)


def _band(rows, cols, left):
    diff = left + np.arange(rows)[:, None] // CHUNK - np.arange(cols)[None, :] // CHUNK
    return (diff >= 0) & (diff <= left)


def _bias_a(rel_bias):
    fn = lambda ks: rel_bias[np.clip(A_PAD - ks, -A_MAX_REL, A_MAX_REL) + A_MAX_REL].T
    bias = jnp.where(_band(QB, A_WIN, A_LEFT_CHUNKS)[None], _toeplitz(fn, QB, A_WIN) * LOG2E, NEG)
    return bias.reshape(A_HEADS * QB, A_WIN)


def _bias_c(t5_c):
    fn = lambda ks: t5_c[_t5_bucket(jnp.asarray(ks - C_PAD, jnp.int32))].T
    bias = jnp.where(_band(QB, C_WIN, C_LEFT_CHUNKS)[None], _toeplitz(fn, QB, C_WIN) * LOG2E, NEG)
    return bias.reshape(C_Q_HEADS * QB, C_WIN)


def _bias_b(t5_b):
    tiles = []
    for n in range(B_NEAR + 1):
        off = BQ * n if n < B_NEAR else SEQ
        fn = lambda ks, off=off: t5_b[_t5_bucket(jnp.asarray(-ks - off, jnp.int32))].T
        tile = _toeplitz(fn, KB, BQ)
        tiles.append(jnp.transpose(tile * LOG2E, (1, 0, 2)).reshape(KB, HQ))
    return jnp.stack(tiles)


def _pad_ff(a, dtype):
    z = jnp.zeros((a.shape[0], FF_P - D_FF), dtype)
    return jnp.concatenate([a[:, :D_FF].astype(dtype), z, a[:, D_FF:].astype(dtype), z], axis=1)


def kernel(x, mem, t5_table, norm_gains, w_in, a_rel_bias, ckv_gain, w_uk, w_uv, sinks, w_branch, w_o,
           mem_gain, w_mq, w_mkv, w_mo, w_up, conv_w, conv_b, w_down):
    bf16 = jnp.bfloat16
    xs = x.reshape(TOKENS, D_MODEL)
    mem2 = mem.reshape(BATCH * MEM_LEN, D_MODEL)
    tri = jnp.asarray(np.tril(np.ones((KB, KB), np.float32), -1), bf16)
    bias_b = _bias_b(t5_table[:, :B_HEADS])
    bias_c = _bias_c(t5_table[:, B_HEADS:])
    gains = norm_gains.reshape(DEPTH, 6, 1, D_MODEL)
    w_o_b, w_mq_b, w_mkv_b, w_mo_b = (w.astype(bf16) for w in (w_o, w_mq, w_mkv, w_mo))
    w_dn_b = jnp.concatenate([w_down.astype(bf16), jnp.zeros((DEPTH, FF_P - D_FF, D_MODEL), bf16)], axis=1)
    w_in_t = jnp.swapaxes(w_in, 1, 2)

    h = _norm(xs, gains[0, 0])
    for l in range(DEPTH):
        g = gains[l]
        proja = _proj(h, w_in_t, l, 0, IN_A, 1024, gate=False, name="in_proj_a")
        projc = _proj(h, w_in_t, l, O_QC, C_COLS, C_COLS, gate=False, name="in_proj_c")
        gates = _proj(h, w_in_t, l, O_GL, 3 * D_MODEL, 1024, gate=True, name="in_proj_g")
        proja3 = proja.reshape(BATCH, SEQ, IN_A)
        oa = _attn_a(proja3, _bias_a(a_rel_bias[l]))
        wuk = jnp.transpose(_pad_heads(w_uk[l], axis=2), (0, 2, 1)).astype(bf16)
        wuv = _pad_heads(w_uv[l], axis=2).astype(bf16)
        ob = _attn_b(proja3, ckv_gain[l].reshape(1, B_LATENT), wuk, wuv, bias_b, tri)
        oc = _attn_c(sinks[l], projc.reshape(BATCH, SEQ, C_COLS), bias_c)
        merged = _merge(oa.reshape(TOKENS, A_W), ob.reshape(TOKENS, B_W), oc.reshape(TOKENS, C_W), gates, w_branch, l)
        xs, h = _gemm_res(merged, w_o_b, l, xs, g[1], g[2], bk=D_MODEL, name="out_proj")
        kv = _memkv(mem2, mem_gain[l].reshape(1, D_MODEL), w_mkv_b, l)
        xs, h = _xattn(h, w_mq_b, kv, w_mo_b, l, xs, g[3], g[4])
        hidden = _ffn_up(h, w_up, l, _pad_ff(conv_w[l], jnp.float32), _pad_ff(conv_b[l].reshape(1, -1), jnp.float32))
        g_next = gains[l + 1, 0] if l + 1 < DEPTH else None
        xs, h = _gemm_res(hidden, w_dn_b, l, xs, g[5], g_next, bk=DOWN_BK, name="ffn_down")
    return xs.reshape(BATCH, SEQ, D_MODEL)
```

```python
import functools
import math

import numpy as np
import jax
import jax.numpy as jnp
from jax import lax
from jax.experimental import pallas as pl
from jax.experimental.pallas import tpu as pltpu

D_MODEL = 2048
BATCH = 4
SEQ = 2048
DEPTH = 2
TOKENS = BATCH * SEQ
CHUNK = 64
EPS = 1e-6
NEG = -1e30
LOG2E = math.log2(math.e)
A_HEADS = 8
A_LEFT_CHUNKS = 8
A_MAX_REL = 128
A_W = 512
B_HEADS = 8
B_W = 512
B_LATENT = 128
IDX_HEADS = 8
IDX_DIM = 64
TOPK = 256
C_Q_HEADS = 16
C_GROUP = 8
C_W = 1024
C_LEFT_CHUNKS = 2
T5_BUCKETS = 32
T5_MAX_DIST = 256
MEM_LEN = 256
MEM_HEADS = 4
MEM_HEAD_DIM = 128
MEM_W = 512
D_FF = 5504
CONV_W = 3

LANES = 128
SUBLANES = 8
HEAD_DIM = 64
QB = 128
KB = 256
VMEM_LIMIT = 56 * 1024 * 1024

O_QA, O_KA, O_VA, O_QB = 0, 512, 1024, 1536
O_CKV = 2048
O_QI = 2176
O_KI = 2688
O_WI = 2752
O_QC = 2760
O_GL = 4040
IN_W = O_GL + 3 * D_MODEL
IN_A = 3072
X_COL0 = 2048
C_COLS = O_GL - O_QC
PROJ_BM = 1024
PROJ_TR = 256

FF_P = 5632
FF_BN = 512
FF_BM = 1024
FF_CH = 256
DOWN_BK = 1408

A_WIN = (A_LEFT_CHUNKS + 2) * CHUNK
C_WIN = (C_LEFT_CHUNKS + 2) * CHUNK
A_PAD = A_LEFT_CHUNKS * CHUNK
C_PAD = C_LEFT_CHUNKS * CHUNK
BQ = 256
B_NEAR = 2
NKB = SEQ // KB
HQ = B_HEADS * BQ

_NT = (((1,), (1,)), ((), ()))
_TN = (((0,), (0,)), ((), ()))


def _cparams(sem):
    return pltpu.CompilerParams(dimension_semantics=sem, vmem_limit_bytes=VMEM_LIMIT)


def _t5_bucket(rel):
    half = T5_BUCKETS // 2
    max_exact = half // 2
    sign = jnp.where(rel > 0, half, 0)
    d = jnp.abs(rel)
    d_f = jnp.maximum(d, 1).astype(jnp.float32)
    large = max_exact + (jnp.log(d_f / max_exact) / math.log(T5_MAX_DIST / max_exact) * (half - max_exact)).astype(jnp.int32)
    large = jnp.minimum(large, half - 1)
    return sign + jnp.where(d < max_exact, d, large)


def _far_bucket_is_constant():
    d = np.arange(BQ * B_NEAR - (KB - 1), SEQ, dtype=np.float32)
    assert d[0] > T5_MAX_DIST
    large = 8 + (np.log(d / 8) / math.log(T5_MAX_DIST / 8) * 8).astype(np.int32)
    return bool(np.all(np.minimum(large, 15) == 15))


assert _far_bucket_is_constant()


def _rms(v, g):
    return v * lax.rsqrt(jnp.mean(v * v, axis=-1, keepdims=True) + EPS) * g


def _norm_kernel(x_ref, g_ref, o_ref):
    o_ref[...] = _rms(x_ref[...], g_ref[...]).astype(o_ref.dtype)


def _norm(x, g, bm=1024):
    t, d = x.shape
    return pl.pallas_call(
        _norm_kernel,
        grid=(t // bm,),
        in_specs=[pl.BlockSpec((bm, d), lambda i: (i, 0)), pl.BlockSpec((1, d), lambda i: (0, 0))],
        out_specs=pl.BlockSpec((bm, d), lambda i: (i, 0)),
        out_shape=jax.ShapeDtypeStruct((t, d), jnp.bfloat16),
        compiler_params=_cparams(("parallel",)),
        name="rmsnorm",
    )(x, g)


def _proj_kernel(gate, h_ref, wt_ref, o_ref, wb):
    @pl.when(pl.program_id(1) == 0)
    def _():
        for c in range(wt_ref.shape[0] // PROJ_TR):
            rows = slice(c * PROJ_TR, (c + 1) * PROJ_TR)
            wb[:, rows] = wt_ref[rows, :].T.astype(wb.dtype)

    acc = jnp.dot(h_ref[...], wb[...], preferred_element_type=jnp.float32)
    o_ref[...] = (jax.nn.sigmoid(acc) if gate else acc).astype(o_ref.dtype)


def _proj(h, w_t, layer, col0, n, bn, gate, name):
    t, k = h.shape
    assert col0 % SUBLANES == 0 and bn % SUBLANES == 0
    w_spec = pl.BlockSpec((None, pl.Element(bn), pl.Element(k)),
                          lambda j, i: (layer, pl.multiple_of(col0 + j * bn, SUBLANES), 0))
    return pl.pallas_call(
        functools.partial(_proj_kernel, gate),
        grid=(n // bn, t // PROJ_BM),
        in_specs=[pl.BlockSpec((PROJ_BM, k), lambda j, i: (i, 0)), w_spec],
        out_specs=pl.BlockSpec((PROJ_BM, bn), lambda j, i: (i, j)),
        out_shape=jax.ShapeDtypeStruct((t, n), jnp.bfloat16),
        scratch_shapes=[pltpu.VMEM((k, bn), jnp.bfloat16)],
        compiler_params=_cparams(("arbitrary", "arbitrary")),
        name=name,
    )(h, w_t)


def _lane_masks(dtype):
    lane = lax.broadcasted_iota(jnp.int32, (1, LANES), 1)
    lo = (lane < HEAD_DIM).astype(jnp.float32)
    return lo.astype(dtype), (1.0 - lo).astype(dtype)


def _swap_halves(x):
    return pltpu.roll(x.astype(jnp.float32), HEAD_DIM, 1).astype(x.dtype)


def _fill_padded(src, dst_ref, pad):
    dst_ref[0:pad, :] = jnp.zeros((pad, dst_ref.shape[1]), dst_ref.dtype)
    dst_ref[pad:pad + SEQ, :] = src


def _attn_a_kernel(q_ref, k_ref, v_ref, bias_ref, o_ref, kpad, vpad):
    i = pl.program_id(1)

    @pl.when(i == 0)
    def _():
        _fill_padded(k_ref[...], kpad, A_PAD)
        _fill_padded(v_ref[...], vpad, A_PAD)

    start = pl.multiple_of(i * QB, QB)
    kw = kpad[pl.ds(start, A_WIN), :]
    vw = vpad[pl.ds(start, A_WIN), :]
    mlo, mhi = _lane_masks(jnp.bfloat16)
    lane = lax.broadcasted_iota(jnp.int32, (QB, LANES), 1)
    kchunk = lax.broadcasted_iota(jnp.int32, (QB, A_WIN), 1) // CHUNK
    pad_mask = jnp.where(kchunk >= A_LEFT_CHUNKS - 2 * i, 0.0, NEG)
    scale = HEAD_DIM ** -0.5 * LOG2E
    pad2 = jnp.concatenate([pad_mask, pad_mask], axis=0)
    npair = A_HEADS // 2
    pcols = [slice(p * LANES, (p + 1) * LANES) for p in range(npair)]
    ss = []
    for p in range(npair):
        qp = q_ref[:, pcols[p]]
        q2 = jnp.concatenate([qp * mlo, qp * mhi], axis=0)
        s = lax.dot_general(q2, kw[:, pcols[p]], _NT, preferred_element_type=jnp.float32)
        ss.append(s * scale + bias_ref[2 * p * QB:(2 * p + 2) * QB, :] + pad2)
    ms = [jnp.max(s, axis=-1, keepdims=True) for s in ss]
    es = [jnp.exp2(s - m) for s, m in zip(ss, ms)]
    ls = [jnp.sum(e, axis=-1, keepdims=True) for e in es]
    outs = [jnp.dot(e.astype(jnp.bfloat16), vw[:, pcols[p]], preferred_element_type=jnp.float32)
            for p, e in enumerate(es)]
    for p in range(npair):
        o = outs[p] / ls[p]
        o_ref[:, pcols[p]] = jnp.where(lane < HEAD_DIM, o[:QB], o[QB:]).astype(o_ref.dtype)


def _attn_a(proj3, bias):
    nq = SEQ // QB
    return pl.pallas_call(
        _attn_a_kernel,
        grid=(BATCH, nq),
        in_specs=[
            pl.BlockSpec((None, QB, A_W), lambda b, i: (b, i, O_QA // A_W)),
            pl.BlockSpec((None, SEQ, A_W), lambda b, i: (b, 0, O_KA // A_W)),
            pl.BlockSpec((None, SEQ, A_W), lambda b, i: (b, 0, O_VA // A_W)),
            pl.BlockSpec((A_HEADS * QB, A_WIN), lambda b, i: (0, 0)),
        ],
        out_specs=pl.BlockSpec((None, QB, A_W), lambda b, i: (b, i, 0)),
        out_shape=jax.ShapeDtypeStruct((BATCH, SEQ, A_W), jnp.bfloat16),
        scratch_shapes=[pltpu.VMEM((SEQ + A_PAD, A_W), jnp.bfloat16), pltpu.VMEM((SEQ + A_PAD, A_W), jnp.bfloat16)],
        compiler_params=_cparams(("arbitrary", "arbitrary")),
        name="attn_a",
    )(proj3, proj3, proj3, bias)


def _attn_c_kernel(sink_ref, q_ref, k_ref, v_ref, bias_ref, o_ref, kpad, kswp, vpad, vswp):
    i = pl.program_id(1)

    @pl.when(i == 0)
    def _():
        k = k_ref[...]
        v = v_ref[...]
        _fill_padded(k, kpad, C_PAD)
        _fill_padded(_swap_halves(k), kswp, C_PAD)
        _fill_padded(v, vpad, C_PAD)
        _fill_padded(_swap_halves(v), vswp, C_PAD)

    start = pl.multiple_of(i * QB, QB)
    mlo, mhi = _lane_masks(jnp.bfloat16)
    lane = lax.broadcasted_iota(jnp.int32, (QB, LANES), 1)
    kchunk = lax.broadcasted_iota(jnp.int32, (QB, C_WIN), 1) // CHUNK
    pad_mask = jnp.where(kchunk >= C_LEFT_CHUNKS - 2 * i, 0.0, NEG)
    scale = HEAD_DIM ** -0.5 * LOG2E
    qs = [q_ref[:, p * LANES:(p + 1) * LANES] for p in range(C_GROUP)]
    npair = C_GROUP // 2
    stacks = []
    for straight in (True, False):
        kref, vref = (kpad, vpad) if straight else (kswp, vswp)
        kw = kref[pl.ds(start, C_WIN), :]
        vw = vref[pl.ds(start, C_WIN), :]
        halves = [int((p >= npair) == straight) for p in range(C_GROUP)]
        qg = jnp.concatenate([qs[p] * (mhi if halves[p] else mlo) for p in range(C_GROUP)], axis=0)
        s_all = lax.dot_general(qg, kw, _NT, preferred_element_type=jnp.float32)
        ps, ls = [], []
        for p in range(C_GROUP):
            h = 2 * p + halves[p]
            s = s_all[p * QB:(p + 1) * QB] * scale + bias_ref[h * QB:(h + 1) * QB, :] + pad_mask
            sink = sink_ref[h] * LOG2E
            m = jnp.maximum(jnp.max(s, axis=-1, keepdims=True), sink)
            e = jnp.exp2(s - m)
            ls.append(jnp.sum(e, axis=-1, keepdims=True) + jnp.exp2(sink - m))
            ps.append(e.astype(jnp.bfloat16))
        o_all = jnp.dot(jnp.concatenate(ps, axis=0), vw, preferred_element_type=jnp.float32)
        stacks.append(([o_all[p * QB:(p + 1) * QB] / ls[p] for p in range(C_GROUP)], halves))
    for p in range(C_GROUP):
        (o1, h1), (o2, _) = stacks
        lo, hi = (o2[p], o1[p]) if h1[p] else (o1[p], o2[p])
        o_ref[:, p * LANES:(p + 1) * LANES] = jnp.where(lane < HEAD_DIM, lo, hi).astype(o_ref.dtype)


def _attn_c(sinks, projc3, bias):
    nq = SEQ // QB
    pad_buf = pltpu.VMEM((SEQ + C_PAD, LANES), jnp.bfloat16)
    return pl.pallas_call(
        _attn_c_kernel,
        grid=(BATCH, nq),
        in_specs=[
            pl.BlockSpec(memory_space=pltpu.SMEM),
            pl.BlockSpec((None, QB, C_W), lambda b, i: (b, i, 0)),
            pl.BlockSpec((None, SEQ, LANES), lambda b, i: (b, 0, C_W // LANES)),
            pl.BlockSpec((None, SEQ, LANES), lambda b, i: (b, 0, C_W // LANES + 1)),
            pl.BlockSpec((C_Q_HEADS * QB, C_WIN), lambda b, i: (0, 0)),
        ],
        out_specs=pl.BlockSpec((None, QB, C_W), lambda b, i: (b, i, 0)),
        out_shape=jax.ShapeDtypeStruct((BATCH, SEQ, C_W), jnp.bfloat16),
        scratch_shapes=[pad_buf, pad_buf, pad_buf, pad_buf],
        compiler_params=_cparams(("arbitrary", "arbitrary")),
        name="attn_c",
    )(sinks, projc3, projc3, projc3, bias)


def _sort_key(x):
    bits = lax.bitcast_convert_type(x + 0.0, jnp.int32)
    return bits ^ ((bits >> 31) & jnp.int32(0x7FFFFFFF))


def _rows_tree(x, op, slab):
    parts = [x[r:r + slab, :] for r in range(0, x.shape[0], slab)]
    while len(parts) > 1:
        nxt = [op(parts[k], parts[k + 1]) for k in range(0, len(parts) - 1, 2)]
        parts = nxt + ([parts[-1]] if len(parts) % 2 else [])
    return parts[0]


def _rows8(x, op):
    return _rows_tree(x, op, SUBLANES)


BF16_ROWS = 16


def _bit_search(count_ge, target, v, above, top_bit):
    def step(t, carry):
        v, above = carry
        cand = v | (jnp.int32(1) << (top_bit - t))
        c = count_ge(cand)
        keep = c >= target
        return jnp.where(keep, cand, v), jnp.where(keep, above, c)

    return lax.fori_loop(0, top_bit + 1, step, (v, above))


def _attn_b_kernel(qb_ref, x_ref, kiw_ref, ckv_ref, gain_ref, wuk_ref, wuv_ref, bias_ref, tri_ref,
                   o_ref, ckvn, kd, ql, qi_all, sk, dg, acc_s, p_s):
    i = pl.program_id(1)
    nkb = (i * BQ) // KB + 1
    mlo, mhi = _lane_masks(jnp.bfloat16)
    f32 = jnp.float32

    @pl.when(i == 0)
    def _():
        ckvn[...] = _rms(ckv_ref[...].astype(f32), gain_ref[...]).astype(ckvn.dtype)
        kiw = kiw_ref[...].astype(f32)
        lane = lax.broadcasted_iota(jnp.int32, kiw.shape, 1)
        kd[...] = jnp.where(lane < HEAD_DIM, kiw, pltpu.roll(kiw, HEAD_DIM, 1)).astype(kd.dtype)

    for h in range(B_HEADS):
        rows = slice(h * BQ, (h + 1) * BQ)
        qlat = jnp.dot(qb_ref[:, (h // 2) * LANES:(h // 2 + 1) * LANES], wuk_ref[h], preferred_element_type=f32)
        ql[rows, :] = (qlat * (HEAD_DIM ** -0.5 * LOG2E)).astype(ql.dtype)
        qcol = O_QI - X_COL0 + (h // 2) * LANES
        qi_all[rows, :] = x_ref[:, qcol:qcol + LANES] * (mhi if h % 2 else mlo)
    wcol = O_KI - X_COL0
    wi_t = x_ref[:, wcol:wcol + LANES].astype(f32).T * (IDX_HEADS ** -0.5 * IDX_DIM ** -0.5)
    wi_rows = [wi_t[O_WI - O_KI + h:O_WI - O_KI + h + 1, :] for h in range(IDX_HEADS)]

    kpos = lax.broadcasted_iota(jnp.int32, (KB, BQ), 0)
    key_limit = ((i * BQ + lax.broadcasted_iota(jnp.int32, (1, BQ), 1)) // CHUNK + 1) * CHUNK

    def admissible(kb):
        return kb * KB + kpos < key_limit

    def score_block(kb, carry):
        kblk = kd[pl.ds(pl.multiple_of(kb * KB, KB), KB), :]
        dots = lax.dot_general(kblk, qi_all[...], _NT, preferred_element_type=f32)
        score = jnp.zeros((KB, BQ), f32)
        for h in range(IDX_HEADS):
            score = score + jnp.maximum(dots[:, h * BQ:(h + 1) * BQ], 0.0) * wi_rows[h]
        sk[kb] = _sort_key(jnp.where(admissible(kb), score, NEG))
        return carry

    lax.fori_loop(0, nkb, score_block, 0)

    neg_key = _sort_key(jnp.full((1, 1), NEG, f32))
    n_rest = ((NKB - nkb) * KB).astype(f32)
    one, zero = jnp.ones((), jnp.bfloat16), jnp.zeros((), jnp.bfloat16)

    def byte_of(key, byte):
        return ((key >> 24) + 128) if byte == 3 else ((key >> (8 * byte)) & 255)

    target = jnp.full((1, BQ), TOPK, f32)
    prefix = jnp.zeros((1, BQ), jnp.int32)
    above = jnp.zeros((1, BQ), f32)
    for byte in (3, 2, 1, 0):
        def in_class(key, byte=byte, prefix=prefix):
            return (key >> (8 * byte + 8)) == prefix

        def prepare(kb, carry, byte=byte, in_class=in_class):
            key = sk[kb]
            digit = byte_of(key, byte).astype(f32)
            if byte < 3:
                digit = jnp.where(in_class(key), digit, -1.0)
            dg[kb] = digit.astype(dg.dtype)
            return carry

        lax.fori_loop(0, nkb, prepare, 0)
        rest_digit = byte_of(neg_key, byte)
        rest_on = in_class(neg_key) if byte < 3 else (neg_key == neg_key)

        def count_ge(cand, rest_digit=rest_digit, rest_on=rest_on):
            cand_b = cand.astype(f32).astype(jnp.bfloat16)

            def body(kb, acc):
                hit = jnp.where(dg[kb] >= cand_b, one, zero)
                return acc + _rows_tree(hit, jnp.add, BF16_ROWS).astype(f32)

            acc = lax.fori_loop(0, nkb, body, jnp.zeros((BF16_ROWS, BQ), f32))
            rest = jnp.where(rest_on & (rest_digit >= cand), n_rest, 0.0)
            return jnp.sum(acc, axis=0, keepdims=True) + rest

        digit_thr, above_here = _bit_search(count_ge, target, jnp.zeros((1, BQ), jnp.int32),
                                            jnp.zeros((1, BQ), f32), 7)
        prefix = (digit_thr - 128) if byte == 3 else (prefix * 256 + digit_thr)
        above = above + above_here
        target = target - above_here
    thr = prefix
    need = TOPK - above

    acc_s[...] = jnp.zeros(acc_s.shape, f32)

    far_bias = bias_ref[B_NEAR, 0:1, :]

    def attend(far, kb, carry):
        eq_seen, m_old, l_old = carry
        keys = sk[kb]
        eq = keys == thr
        eq_f = jnp.where(eq, 1.0, 0.0)
        before = jnp.dot(tri_ref[...], eq_f.astype(jnp.bfloat16), preferred_element_type=f32)
        take_eq = jnp.where(eq, jnp.where(eq_seen + before < need, 1.0, 0.0), 0.0)
        take = jnp.where(keys > thr, 1.0, take_eq)
        mask_add = jnp.where(admissible(kb), jnp.where(take > 0.0, 0.0, NEG), NEG)
        cblk = ckvn[pl.ds(pl.multiple_of(kb * KB, KB), KB), :]
        dots = lax.dot_general(cblk, ql[...], _NT, preferred_element_type=f32)
        tile = i - (KB // BQ) * kb
        m_parts, l_parts, a_parts = [], [], []
        for g in range(HQ // LANES):
            cols = slice(g * LANES, (g + 1) * LANES)
            qcols = slice((g % (BQ // LANES)) * LANES, (g % (BQ // LANES) + 1) * LANES)
            mo = m_old[:, cols]
            if far:
                s = dots[:, cols] + mask_add[:, qcols]
                shift = far_bias[:, cols]
                mn = jnp.maximum(mo, jnp.max(_rows8(s, jnp.maximum), axis=0, keepdims=True) + shift)
                e = jnp.exp2(s - (mn - shift))
            else:
                s = dots[:, cols] + bias_ref[tile, :, cols] + mask_add[:, qcols]
                mn = jnp.maximum(mo, jnp.max(_rows8(s, jnp.maximum), axis=0, keepdims=True))
                e = jnp.exp2(s - mn)
            a = jnp.exp2(mo - mn)
            l_parts.append(a * l_old[:, cols] + jnp.sum(_rows8(e, jnp.add), axis=0, keepdims=True))
            m_parts.append(mn)
            a_parts.append(a)
            p_s[:, cols] = e.astype(p_s.dtype)
        alpha = jnp.concatenate(a_parts, axis=1)
        acc_s[...] = acc_s[...] * alpha + lax.dot_general(cblk, p_s[...], _TN, preferred_element_type=f32)
        eq_seen = eq_seen + jnp.sum(_rows8(eq_f, jnp.add), axis=0, keepdims=True)
        return eq_seen, jnp.concatenate(m_parts, axis=1), jnp.concatenate(l_parts, axis=1)

    init = (jnp.zeros((1, BQ), f32), jnp.full((1, HQ), 4 * NEG, f32), jnp.zeros((1, HQ), f32))
    assert BQ == KB
    n_far = jnp.maximum(nkb - B_NEAR, 0)
    carry = lax.fori_loop(0, n_far, functools.partial(attend, True), init)
    _, _, l_fin = lax.fori_loop(n_far, nkb, functools.partial(attend, False), carry)

    o_lat_t = acc_s[...] / l_fin
    for p in range(B_HEADS // 2):
        out = jnp.zeros((BQ, LANES), f32)
        for h in (2 * p, 2 * p + 1):
            o_lat = o_lat_t[:, h * BQ:(h + 1) * BQ].T.astype(jnp.bfloat16)
            out = out + jnp.dot(o_lat, wuv_ref[h], preferred_element_type=f32)
        o_ref[:, p * LANES:(p + 1) * LANES] = out.astype(o_ref.dtype)


def _attn_b(proj3, gain, wuk, wuv, bias, tri, nbatch=BATCH):
    nq = SEQ // BQ
    xw = IN_A - X_COL0
    return pl.pallas_call(
        _attn_b_kernel,
        grid=(nbatch, nq),
        in_specs=[
            pl.BlockSpec((None, BQ, B_W), lambda b, i: (b, i, O_QB // B_W)),
            pl.BlockSpec((None, BQ, xw), lambda b, i: (b, i, X_COL0 // xw)),
            pl.BlockSpec((None, SEQ, LANES), lambda b, i: (b, 0, O_KI // LANES)),
            pl.BlockSpec((None, SEQ, LANES), lambda b, i: (b, 0, O_CKV // LANES)),
            pl.BlockSpec((1, B_LATENT), lambda b, i: (0, 0)),
            pl.BlockSpec((B_HEADS, LANES, B_LATENT), lambda b, i: (0, 0, 0)),
            pl.BlockSpec((B_HEADS, B_LATENT, LANES), lambda b, i: (0, 0, 0)),
            pl.BlockSpec((B_NEAR + 1, KB, HQ), lambda b, i: (0, 0, 0)),
            pl.BlockSpec((KB, KB), lambda b, i: (0, 0)),
        ],
        out_specs=pl.BlockSpec((None, BQ, B_W), lambda b, i: (b, i, 0)),
        out_shape=jax.ShapeDtypeStruct((nbatch, SEQ, B_W), jnp.bfloat16),
        scratch_shapes=[
            pltpu.VMEM((SEQ, B_LATENT), jnp.bfloat16),
            pltpu.VMEM((SEQ, LANES), jnp.bfloat16),
            pltpu.VMEM((HQ, B_LATENT), jnp.bfloat16),
            pltpu.VMEM((HQ, LANES), jnp.bfloat16),
            pltpu.VMEM((NKB, KB, BQ), jnp.int32),
            pltpu.VMEM((NKB, KB, BQ), jnp.bfloat16),
            pltpu.VMEM((B_LATENT, HQ), jnp.float32),
            pltpu.VMEM((KB, HQ), jnp.bfloat16),
        ],
        compiler_params=_cparams(("arbitrary", "arbitrary")),
        name="attn_b",
    )(proj3, proj3, proj3, proj3, gain, wuk, wuv, bias, tri)


MERGE_BN = 1024
MERGE_BM = 1024


def _merge_kernel(oa_ref, ob_ref, oc_ref, ga_ref, gb_ref, gc_ref, wa_ref, wb_ref, wc_ref, o_ref, wa, wb, wc):
    f32 = jnp.float32

    @pl.when(pl.program_id(1) == 0)
    def _():
        wa[...] = wa_ref[...].astype(wa.dtype)
        wb[...] = wb_ref[...].astype(wb.dtype)
        wc[...] = wc_ref[...].astype(wc.dtype)

    m = ga_ref[...].astype(f32) * jnp.dot(oa_ref[...], wa[...], preferred_element_type=f32)
    m = m + gb_ref[...].astype(f32) * jnp.dot(ob_ref[...], wb[...], preferred_element_type=f32)
    m = m + gc_ref[...].astype(f32) * jnp.dot(oc_ref[...], wc[...], preferred_element_type=f32)
    o_ref[...] = m.astype(o_ref.dtype)


def _merge(oa, ob, oc, gates, w_branch, layer):
    t = oa.shape[0]
    bn, bm = MERGE_BN, MERGE_BM
    gstep = D_MODEL // bn
    return pl.pallas_call(
        _merge_kernel,
        grid=(D_MODEL // bn, t // bm),
        in_specs=[
            pl.BlockSpec((bm, A_W), lambda j, i: (i, 0)),
            pl.BlockSpec((bm, B_W), lambda j, i: (i, 0)),
            pl.BlockSpec((bm, C_W), lambda j, i: (i, 0)),
            pl.BlockSpec((bm, bn), lambda j, i: (i, j)),
            pl.BlockSpec((bm, bn), lambda j, i: (i, gstep + j)),
            pl.BlockSpec((bm, bn), lambda j, i: (i, 2 * gstep + j)),
            pl.BlockSpec((None, A_W, bn), lambda j, i: (layer, 0, j)),
            pl.BlockSpec((None, B_W, bn), lambda j, i: (layer, A_W // B_W, j)),
            pl.BlockSpec((None, C_W, bn), lambda j, i: (layer, (A_W + B_W) // C_W, j)),
        ],
        out_specs=pl.BlockSpec((bm, bn), lambda j, i: (i, j)),
        out_shape=jax.ShapeDtypeStruct((t, D_MODEL), jnp.bfloat16),
        scratch_shapes=[pltpu.VMEM((A_W, bn), jnp.bfloat16), pltpu.VMEM((B_W, bn), jnp.bfloat16),
                        pltpu.VMEM((C_W, bn), jnp.bfloat16)],
        compiler_params=_cparams(("arbitrary", "arbitrary")),
        name="merge",
    )(oa, ob, oc, gates, gates, gates, w_branch, w_branch, w_branch)


def _finish(y, x_ref, gp_ref, gn_ref, xo_ref, ho_ref):
    xn = x_ref[...] + _rms(y, gp_ref[...])
    xo_ref[...] = xn
    if ho_ref is not None:
        ho_ref[...] = _rms(xn, gn_ref[...]).astype(ho_ref.dtype)


def _gemm_res_kernel(nk, with_next, a_ref, w_ref, x_ref, gp_ref, gn_ref, xo_ref, *rest):
    ho_ref = rest[0] if with_next else None
    if nk == 1:
        y = jnp.dot(a_ref[...], w_ref[...], preferred_element_type=jnp.float32)
        _finish(y, x_ref, gp_ref, gn_ref, xo_ref, ho_ref)
        return
    acc = rest[-1]
    k = pl.program_id(1)

    @pl.when(k == 0)
    def _():
        acc[...] = jnp.zeros(acc.shape, jnp.float32)

    acc[...] += jnp.dot(a_ref[...], w_ref[...], preferred_element_type=jnp.float32)

    @pl.when(k == nk - 1)
    def _():
        _finish(acc[...], x_ref, gp_ref, gn_ref, xo_ref, ho_ref)


def _gemm_res(a, w, layer, x, g_post, g_next, bk, bm=512, name="gemm_res"):
    t, kdim = a.shape
    n = w.shape[2]
    nk = kdim // bk
    with_next = g_next is not None
    if g_next is None:
        g_next = g_post
    out_shape = [jax.ShapeDtypeStruct((t, n), jnp.float32)]
    out_specs = [pl.BlockSpec((bm, n), lambda i, k: (i, 0))]
    if with_next:
        out_shape.append(jax.ShapeDtypeStruct((t, n), jnp.bfloat16))
        out_specs.append(pl.BlockSpec((bm, n), lambda i, k: (i, 0)))
    res = pl.pallas_call(
        functools.partial(_gemm_res_kernel, nk, with_next),
        grid=(t // bm, nk),
        in_specs=[
            pl.BlockSpec((bm, bk), lambda i, k: (i, k)),
            pl.BlockSpec((None, bk, n), lambda i, k: (layer, k, 0)),
            pl.BlockSpec((bm, n), lambda i, k: (i, 0)),
            pl.BlockSpec((1, n), lambda i, k: (0, 0)),
            pl.BlockSpec((1, n), lambda i, k: (0, 0)),
        ],
        out_specs=out_specs,
        out_shape=out_shape,
        scratch_shapes=[pltpu.VMEM((bm, n), jnp.float32)] if nk > 1 else [],
        compiler_params=_cparams(("parallel", "arbitrary")),
        name=name,
    )(a, w, x, g_post, g_next)
    return (res[0], res[1]) if with_next else (res[0], None)


def _cast_once(pairs):
    @pl.when(pl.program_id(0) == 0)
    def _():
        for src, dst in pairs:
            dst[...] = src[...].astype(dst.dtype)


def _memkv_kernel(m_ref, g_ref, w_ref, o_ref, wb):
    _cast_once([(w_ref, wb)])
    mn = _rms(m_ref[...], g_ref[...]).astype(jnp.bfloat16)
    o_ref[...] = jnp.dot(mn, wb[...], preferred_element_type=jnp.float32).astype(o_ref.dtype)


def _memkv(mem2, g, w, layer):
    t, d = mem2.shape
    n = w.shape[2]
    bm = 512
    return pl.pallas_call(
        _memkv_kernel,
        grid=(t // bm,),
        in_specs=[pl.BlockSpec((bm, d), lambda i: (i, 0)), pl.BlockSpec((1, d), lambda i: (0, 0)),
                  pl.BlockSpec((None, d, n), lambda i: (layer, 0, 0))],
        out_specs=pl.BlockSpec((bm, n), lambda i: (i, 0)),
        out_shape=jax.ShapeDtypeStruct((t, n), jnp.bfloat16),
        scratch_shapes=[pltpu.VMEM((d, n), jnp.bfloat16)],
        compiler_params=_cparams(("arbitrary",)),
        name="mem_kv",
    )(mem2, g, w)


XA_BM = 512


def _xattn_kernel(h_ref, wqf_ref, kv_ref, wof_ref, x_ref, gp_ref, gn_ref, xo_ref, ho_ref, wq_ref, wo_ref):
    f32 = jnp.float32
    _cast_once([(wqf_ref, wq_ref), (wof_ref, wo_ref)])
    q = jnp.dot(h_ref[...], wq_ref[...], preferred_element_type=f32).astype(jnp.bfloat16)
    scale = MEM_HEAD_DIM ** -0.5 * LOG2E
    hcols = [slice(h * LANES, (h + 1) * LANES) for h in range(MEM_HEADS)]
    ss = [lax.dot_general(q[:, c], kv_ref[:, c], _NT, preferred_element_type=f32) * scale for c in hcols]
    ms = [jnp.max(s, axis=-1, keepdims=True) for s in ss]
    es = [jnp.exp2(s - m) for s, m in zip(ss, ms)]
    ls = [jnp.sum(e, axis=-1, keepdims=True) for e in es]
    pv = [jnp.dot(e.astype(jnp.bfloat16), kv_ref[:, MEM_W + h * LANES:MEM_W + (h + 1) * LANES],
                  preferred_element_type=f32) for h, e in enumerate(es)]
    o = jnp.concatenate([(o_h / l).astype(jnp.bfloat16) for o_h, l in zip(pv, ls)], axis=-1)
    y = jnp.dot(o, wo_ref[...], preferred_element_type=f32)
    _finish(y, x_ref, gp_ref, gn_ref, xo_ref, ho_ref)


def _xattn(h, wq, kv, wo, layer, x, g_post, g_next):
    t, d = h.shape
    bm = XA_BM
    per_batch = SEQ // bm
    return pl.pallas_call(
        _xattn_kernel,
        grid=(t // bm,),
        in_specs=[
            pl.BlockSpec((bm, d), lambda i: (i, 0)),
            pl.BlockSpec((None, d, MEM_W), lambda i: (layer, 0, 0)),
            pl.BlockSpec((MEM_LEN, 2 * MEM_W), lambda i: (i // per_batch, 0)),
            pl.BlockSpec((None, MEM_W, d), lambda i: (layer, 0, 0)),
            pl.BlockSpec((bm, d), lambda i: (i, 0)),
            pl.BlockSpec((1, d), lambda i: (0, 0)),
            pl.BlockSpec((1, d), lambda i: (0, 0)),
        ],
        out_specs=[pl.BlockSpec((bm, d), lambda i: (i, 0)), pl.BlockSpec((bm, d), lambda i: (i, 0))],
        out_shape=[jax.ShapeDtypeStruct((t, d), jnp.float32), jax.ShapeDtypeStruct((t, d), jnp.bfloat16)],
        scratch_shapes=[pltpu.VMEM((d, MEM_W), jnp.bfloat16), pltpu.VMEM((MEM_W, d), jnp.bfloat16)],
        compiler_params=_cparams(("arbitrary",)),
        name="mem_xattn",
    )(h, wq, kv, wo, x, g_post, g_next)


def _ffn_up_kernel(h_ref, wgf_ref, wvf_ref, cwg_ref, cwv_ref, cbg_ref, cbv_ref, o_ref, wg_ref, wv_ref, ug, uv):
    j = pl.program_id(0)
    i = pl.program_id(1)
    tiles_per_seq = SEQ // FF_BM
    last = FF_P // FF_BN - 1
    valid = D_FF - last * FF_BN
    shift = FF_BN - valid

    @pl.when((i == 0) & (j < last))
    def _():
        wg_ref[...] = wgf_ref[...].astype(wg_ref.dtype)
        wv_ref[...] = wvf_ref[...].astype(wv_ref.dtype)

    @pl.when((i == 0) & (j == last))
    def _():
        zeros = jnp.zeros((wg_ref.shape[0], FF_BN - valid), wg_ref.dtype)
        wg_ref[:, :valid] = wgf_ref[:, :valid].astype(wg_ref.dtype)
        wv_ref[:, :valid] = wvf_ref[:, shift:].astype(wv_ref.dtype)
        wg_ref[:, valid:] = zeros
        wv_ref[:, valid:] = zeros

    for u in (ug, uv):
        @pl.when(i % tiles_per_seq == 0)
        def _():
            u[0:SUBLANES, :] = jnp.zeros((SUBLANES, FF_BN), jnp.float32)

        @pl.when(i % tiles_per_seq != 0)
        def _():
            u[0:SUBLANES, :] = u[FF_BM:FF_BM + SUBLANES, :]

    def conv(u, cw_ref, cb_ref, r0):
        base = SUBLANES + r0
        acc = cb_ref[...] + u[base - 2:base - 2 + FF_CH, :] * cw_ref[0:1, :]
        acc = acc + u[base - 1:base - 1 + FF_CH, :] * cw_ref[1:2, :]
        return acc + u[base:base + FF_CH, :] * cw_ref[2:3, :]

    for c in range(FF_BM // FF_CH):
        r0 = c * FF_CH
        hb = h_ref[r0:r0 + FF_CH, :]
        ug[SUBLANES + r0:SUBLANES + r0 + FF_CH, :] = jnp.dot(hb, wg_ref[...], preferred_element_type=jnp.float32)
        uv[SUBLANES + r0:SUBLANES + r0 + FF_CH, :] = jnp.dot(hb, wv_ref[...], preferred_element_type=jnp.float32)
        gate = conv(ug, cwg_ref, cbg_ref, r0)
        val = conv(uv, cwv_ref, cbv_ref, r0)
        o_ref[r0:r0 + FF_CH, :] = (jax.nn.gelu(gate) * val).astype(o_ref.dtype)


def _ffn_up(h, w_up, layer, conv_w, conv_b):
    t, d = h.shape
    nj = FF_P // FF_BN
    w_block = (None, pl.Element(d), pl.Element(FF_BN))
    return pl.pallas_call(
        _ffn_up_kernel,
        grid=(nj, t // FF_BM),
        in_specs=[
            pl.BlockSpec((FF_BM, d), lambda j, i: (i, 0)),
            pl.BlockSpec(w_block, lambda j, i: (layer, 0, pl.multiple_of(j * FF_BN, LANES))),
            pl.BlockSpec(w_block, lambda j, i: (
                layer, 0, pl.multiple_of(jnp.minimum(D_FF + j * FF_BN, 2 * D_FF - FF_BN), LANES))),
            pl.BlockSpec((CONV_W, FF_BN), lambda j, i: (0, j)),
            pl.BlockSpec((CONV_W, FF_BN), lambda j, i: (0, nj + j)),
            pl.BlockSpec((1, FF_BN), lambda j, i: (0, j)),
            pl.BlockSpec((1, FF_BN), lambda j, i: (0, nj + j)),
        ],
        out_specs=pl.BlockSpec((FF_BM, FF_BN), lambda j, i: (i, j)),
        out_shape=jax.ShapeDtypeStruct((t, FF_P), jnp.bfloat16),
        scratch_shapes=[pltpu.VMEM((d, FF_BN), jnp.bfloat16), pltpu.VMEM((d, FF_BN), jnp.bfloat16),
                        pltpu.VMEM((FF_BM + SUBLANES, FF_BN), jnp.float32),
                        pltpu.VMEM((FF_BM + SUBLANES, FF_BN), jnp.float32)],
        compiler_params=_cparams(("arbitrary", "arbitrary")),
        name="ffn_up",
    )(h, w_up, w_up, conv_w, conv_w, conv_b, conv_b)


def _pad_heads(w, axis):
    h = w.shape[0]
    zero = jnp.zeros_like(w)
    even = jnp.concatenate([w, zero], axis=axis)
    odd = jnp.concatenate([zero, w], axis=axis)
    sel = (jnp.arange(h) % 2 == 0).reshape((h, 1, 1))
    return jnp.where(sel, even, odd)


def _toeplitz(fn, rows, cols):
    ks = np.concatenate([np.arange(0, cols), np.arange(-(rows - 1), 0)])
    w = fn(ks)
    h, period = w.shape
    x = jnp.tile(w, (1, rows))[:, :rows * (period - 1)].reshape(h, rows, period - 1)
    return x[:, :, :cols].astype(jnp.float32)


def _band(rows, cols, left):
    diff = left + np.arange(rows)[:, None] // CHUNK - np.arange(cols)[None, :] // CHUNK
    return (diff >= 0) & (diff <= left)


def _bias_a(rel_bias):
    fn = lambda ks: rel_bias[np.clip(A_PAD - ks, -A_MAX_REL, A_MAX_REL) + A_MAX_REL].T
    bias = jnp.where(_band(QB, A_WIN, A_LEFT_CHUNKS)[None], _toeplitz(fn, QB, A_WIN) * LOG2E, NEG)
    return bias.reshape(A_HEADS * QB, A_WIN)


def _bias_c(t5_c):
    fn = lambda ks: t5_c[_t5_bucket(jnp.asarray(ks - C_PAD, jnp.int32))].T
    bias = jnp.where(_band(QB, C_WIN, C_LEFT_CHUNKS)[None], _toeplitz(fn, QB, C_WIN) * LOG2E, NEG)
    return bias.reshape(C_Q_HEADS * QB, C_WIN)


def _bias_b(t5_b):
    tiles = []
    for n in range(B_NEAR + 1):
        off = BQ * n if n < B_NEAR else SEQ
        fn = lambda ks, off=off: t5_b[_t5_bucket(jnp.asarray(-ks - off, jnp.int32))].T
        tile = _toeplitz(fn, KB, BQ)
        tiles.append(jnp.transpose(tile * LOG2E, (1, 0, 2)).reshape(KB, HQ))
    return jnp.stack(tiles)


def _pad_ff(a, dtype):
    z = jnp.zeros((a.shape[0], FF_P - D_FF), dtype)
    return jnp.concatenate([a[:, :D_FF].astype(dtype), z, a[:, D_FF:].astype(dtype), z], axis=1)


def kernel(x, mem, t5_table, norm_gains, w_in, a_rel_bias, ckv_gain, w_uk, w_uv, sinks, w_branch, w_o,
           mem_gain, w_mq, w_mkv, w_mo, w_up, conv_w, conv_b, w_down):
    bf16 = jnp.bfloat16
    xs = x.reshape(TOKENS, D_MODEL)
    mem2 = mem.reshape(BATCH * MEM_LEN, D_MODEL)
    tri = jnp.asarray(np.tril(np.ones((KB, KB), np.float32), -1), bf16)
    bias_b = _bias_b(t5_table[:, :B_HEADS])
    bias_c = _bias_c(t5_table[:, B_HEADS:])
    gains = norm_gains.reshape(DEPTH, 6, 1, D_MODEL)
    w_o_b = w_o.astype(bf16)
    w_dn_b = jnp.concatenate([w_down.astype(bf16), jnp.zeros((DEPTH, FF_P - D_FF, D_MODEL), bf16)], axis=1)
    w_in_t = jnp.swapaxes(w_in, 1, 2)

    h = _norm(xs, gains[0, 0])
    for l in range(DEPTH):
        g = gains[l]
        proja = _proj(h, w_in_t, l, 0, IN_A, 1024, gate=False, name="in_proj_a")
        projc = _proj(h, w_in_t, l, O_QC, C_COLS, C_COLS, gate=False, name="in_proj_c")
        gates = _proj(h, w_in_t, l, O_GL, 3 * D_MODEL, 1024, gate=True, name="in_proj_g")
        proja3 = proja.reshape(BATCH, SEQ, IN_A)
        oa = _attn_a(proja3, _bias_a(a_rel_bias[l]))
        wuk = jnp.transpose(_pad_heads(w_uk[l], axis=2), (0, 2, 1)).astype(bf16)
        wuv = _pad_heads(w_uv[l], axis=2).astype(bf16)
        ob = _attn_b(proja3, ckv_gain[l].reshape(1, B_LATENT), wuk, wuv, bias_b, tri)
        oc = _attn_c(sinks[l], projc.reshape(BATCH, SEQ, C_COLS), bias_c)
        merged = _merge(oa.reshape(TOKENS, A_W), ob.reshape(TOKENS, B_W), oc.reshape(TOKENS, C_W), gates, w_branch, l)
        xs, h = _gemm_res(merged, w_o_b, l, xs, g[1], g[2], bk=D_MODEL, name="out_proj")
        kv = _memkv(mem2, mem_gain[l].reshape(1, D_MODEL), w_mkv, l)
        xs, h = _xattn(h, w_mq, kv, w_mo, l, xs, g[3], g[4])
        hidden = _ffn_up(h, w_up, l, _pad_ff(conv_w[l], jnp.float32), _pad_ff(conv_b[l].reshape(1, -1), jnp.float32))
        g_next = gains[l + 1, 0] if l + 1 < DEPTH else None
        xs, h = _gemm_res(hidden, w_dn_b, l, xs, g[5], g_next, bk=DOWN_BK, name="ffn_down")
    return xs.reshape(BATCH, SEQ, D_MODEL)
```

```python
import functools
import math

import numpy as np
import jax
import jax.numpy as jnp
from jax import lax
from jax.experimental import pallas as pl
from jax.experimental.pallas import tpu as pltpu

D_MODEL = 2048
BATCH = 4
SEQ = 2048
DEPTH = 2
TOKENS = BATCH * SEQ
CHUNK = 64
EPS = 1e-6
NEG = -1e30
LOG2E = math.log2(math.e)
A_HEADS = 8
A_LEFT_CHUNKS = 8
A_MAX_REL = 128
A_W = 512
B_HEADS = 8
B_W = 512
B_LATENT = 128
IDX_HEADS = 8
IDX_DIM = 64
TOPK = 256
C_Q_HEADS = 16
C_GROUP = 8
C_W = 1024
C_LEFT_CHUNKS = 2
T5_BUCKETS = 32
T5_MAX_DIST = 256
MEM_LEN = 256
MEM_HEADS = 4
MEM_HEAD_DIM = 128
MEM_W = 512
D_FF = 5504
CONV_W = 3

LANES = 128
SUBLANES = 8
HEAD_DIM = 64
QB = 128
KB = 256
VMEM_LIMIT = 56 * 1024 * 1024

O_QA, O_KA, O_VA, O_QB = 0, 512, 1024, 1536
O_CKV = 2048
O_QI = 2176
O_KI = 2688
O_WI = 2752
O_QC = 2760
O_GL = 4040
IN_W = O_GL + 3 * D_MODEL
IN_A = 3072
X_COL0 = 2048
C_COLS = O_GL - O_QC
PROJ_BM = 1024
PROJ_TR = 256

FF_P = 5632
FF_BN = 512
FF_BM = 1024
FF_CH = 256
DOWN_BK = 1408

A_WIN = (A_LEFT_CHUNKS + 2) * CHUNK
C_WIN = (C_LEFT_CHUNKS + 2) * CHUNK
A_PAD = A_LEFT_CHUNKS * CHUNK
C_PAD = C_LEFT_CHUNKS * CHUNK
BQ = 256
B_NEAR = 2
NKB = SEQ // KB
HQ = B_HEADS * BQ

_NT = (((1,), (1,)), ((), ()))
_TN = (((0,), (0,)), ((), ()))


def _cparams(sem):
    return pltpu.CompilerParams(dimension_semantics=sem, vmem_limit_bytes=VMEM_LIMIT)


def _t5_bucket(rel):
    half = T5_BUCKETS // 2
    max_exact = half // 2
    sign = jnp.where(rel > 0, half, 0)
    d = jnp.abs(rel)
    d_f = jnp.maximum(d, 1).astype(jnp.float32)
    large = max_exact + (jnp.log(d_f / max_exact) / math.log(T5_MAX_DIST / max_exact) * (half - max_exact)).astype(jnp.int32)
    large = jnp.minimum(large, half - 1)
    return sign + jnp.where(d < max_exact, d, large)


def _far_bucket_is_constant():
    d = np.arange(BQ * B_NEAR - (KB - 1), SEQ, dtype=np.float32)
    assert d[0] > T5_MAX_DIST
    large = 8 + (np.log(d / 8) / math.log(T5_MAX_DIST / 8) * 8).astype(np.int32)
    return bool(np.all(np.minimum(large, 15) == 15))


assert _far_bucket_is_constant()


def _rms(v, g):
    return v * lax.rsqrt(jnp.mean(v * v, axis=-1, keepdims=True) + EPS) * g


def _norm_kernel(x_ref, g_ref, o_ref):
    o_ref[...] = _rms(x_ref[...], g_ref[...]).astype(o_ref.dtype)


def _norm(x, g, bm=1024):
    t, d = x.shape
    return pl.pallas_call(
        _norm_kernel,
        grid=(t // bm,),
        in_specs=[pl.BlockSpec((bm, d), lambda i: (i, 0)), pl.BlockSpec((1, d), lambda i: (0, 0))],
        out_specs=pl.BlockSpec((bm, d), lambda i: (i, 0)),
        out_shape=jax.ShapeDtypeStruct((t, d), jnp.bfloat16),
        compiler_params=_cparams(("parallel",)),
        name="rmsnorm",
    )(x, g)


def _proj_kernel(gate, h_ref, wt_ref, o_ref, wb):
    @pl.when(pl.program_id(1) == 0)
    def _():
        for c in range(wt_ref.shape[0] // PROJ_TR):
            rows = slice(c * PROJ_TR, (c + 1) * PROJ_TR)
            wb[:, rows] = wt_ref[rows, :].T.astype(wb.dtype)

    acc = jnp.dot(h_ref[...], wb[...], preferred_element_type=jnp.float32)
    o_ref[...] = (jax.nn.sigmoid(acc) if gate else acc).astype(o_ref.dtype)


def _proj(h, w_t, layer, col0, n, bn, gate, name):
    t, k = h.shape
    assert col0 % SUBLANES == 0 and bn % SUBLANES == 0
    w_spec = pl.BlockSpec((None, pl.Element(bn), pl.Element(k)),
                          lambda j, i: (layer, pl.multiple_of(col0 + j * bn, SUBLANES), 0))
    return pl.pallas_call(
        functools.partial(_proj_kernel, gate),
        grid=(n // bn, t // PROJ_BM),
        in_specs=[pl.BlockSpec((PROJ_BM, k), lambda j, i: (i, 0)), w_spec],
        out_specs=pl.BlockSpec((PROJ_BM, bn), lambda j, i: (i, j)),
        out_shape=jax.ShapeDtypeStruct((t, n), jnp.bfloat16),
        scratch_shapes=[pltpu.VMEM((k, bn), jnp.bfloat16)],
        compiler_params=_cparams(("arbitrary", "arbitrary")),
        name=name,
    )(h, w_t)


def _lane_masks(dtype):
    lane = lax.broadcasted_iota(jnp.int32, (1, LANES), 1)
    lo = (lane < HEAD_DIM).astype(jnp.float32)
    return lo.astype(dtype), (1.0 - lo).astype(dtype)


def _swap_halves(x):
    return pltpu.roll(x.astype(jnp.float32), HEAD_DIM, 1).astype(x.dtype)


def _fill_padded(src, dst_ref, pad):
    dst_ref[0:pad, :] = jnp.zeros((pad, dst_ref.shape[1]), dst_ref.dtype)
    dst_ref[pad:pad + SEQ, :] = src


def _attn_a_kernel(q_ref, k_ref, v_ref, bias_ref, o_ref, kpad, vpad):
    i = pl.program_id(1)

    @pl.when(i == 0)
    def _():
        _fill_padded(k_ref[...], kpad, A_PAD)
        _fill_padded(v_ref[...], vpad, A_PAD)

    start = pl.multiple_of(i * QB, QB)
    kw = kpad[pl.ds(start, A_WIN), :]
    vw = vpad[pl.ds(start, A_WIN), :]
    mlo, mhi = _lane_masks(jnp.bfloat16)
    lane = lax.broadcasted_iota(jnp.int32, (QB, LANES), 1)
    kchunk = lax.broadcasted_iota(jnp.int32, (QB, A_WIN), 1) // CHUNK
    pad_mask = jnp.where(kchunk >= A_LEFT_CHUNKS - 2 * i, 0.0, NEG)
    scale = HEAD_DIM ** -0.5 * LOG2E
    pad2 = jnp.concatenate([pad_mask, pad_mask], axis=0)
    npair = A_HEADS // 2
    pcols = [slice(p * LANES, (p + 1) * LANES) for p in range(npair)]
    ss = []
    for p in range(npair):
        qp = q_ref[:, pcols[p]]
        q2 = jnp.concatenate([qp * mlo, qp * mhi], axis=0)
        s = lax.dot_general(q2, kw[:, pcols[p]], _NT, preferred_element_type=jnp.float32)
        ss.append(s * scale + bias_ref[2 * p * QB:(2 * p + 2) * QB, :] + pad2)
    ms = [jnp.max(s, axis=-1, keepdims=True) for s in ss]
    es = [jnp.exp2(s - m) for s, m in zip(ss, ms)]
    ls = [jnp.sum(e, axis=-1, keepdims=True) for e in es]
    outs = [jnp.dot(e.astype(jnp.bfloat16), vw[:, pcols[p]], preferred_element_type=jnp.float32)
            for p, e in enumerate(es)]
    for p in range(npair):
        o = outs[p] / ls[p]
        o_ref[:, pcols[p]] = jnp.where(lane < HEAD_DIM, o[:QB], o[QB:]).astype(o_ref.dtype)


def _attn_a(proj3, bias):
    nq = SEQ // QB
    return pl.pallas_call(
        _attn_a_kernel,
        grid=(BATCH, nq),
        in_specs=[
            pl.BlockSpec((None, QB, A_W), lambda b, i: (b, i, O_QA // A_W)),
            pl.BlockSpec((None, SEQ, A_W), lambda b, i: (b, 0, O_KA // A_W)),
            pl.BlockSpec((None, SEQ, A_W), lambda b, i: (b, 0, O_VA // A_W)),
            pl.BlockSpec((A_HEADS * QB, A_WIN), lambda b, i: (0, 0)),
        ],
        out_specs=pl.BlockSpec((None, QB, A_W), lambda b, i: (b, i, 0)),
        out_shape=jax.ShapeDtypeStruct((BATCH, SEQ, A_W), jnp.bfloat16),
        scratch_shapes=[pltpu.VMEM((SEQ + A_PAD, A_W), jnp.bfloat16), pltpu.VMEM((SEQ + A_PAD, A_W), jnp.bfloat16)],
        compiler_params=_cparams(("arbitrary", "arbitrary")),
        name="attn_a",
    )(proj3, proj3, proj3, bias)


def _attn_c_kernel(sink_ref, q_ref, k_ref, v_ref, bias_ref, o_ref, kpad, kswp, vpad, vswp):
    i = pl.program_id(1)

    @pl.when(i == 0)
    def _():
        k = k_ref[...]
        v = v_ref[...]
        _fill_padded(k, kpad, C_PAD)
        _fill_padded(_swap_halves(k), kswp, C_PAD)
        _fill_padded(v, vpad, C_PAD)
        _fill_padded(_swap_halves(v), vswp, C_PAD)

    start = pl.multiple_of(i * QB, QB)
    mlo, mhi = _lane_masks(jnp.bfloat16)
    lane = lax.broadcasted_iota(jnp.int32, (QB, LANES), 1)
    kchunk = lax.broadcasted_iota(jnp.int32, (QB, C_WIN), 1) // CHUNK
    pad_mask = jnp.where(kchunk >= C_LEFT_CHUNKS - 2 * i, 0.0, NEG)
    scale = HEAD_DIM ** -0.5 * LOG2E
    qs = [q_ref[:, p * LANES:(p + 1) * LANES] for p in range(C_GROUP)]
    npair = C_GROUP // 2
    stacks = []
    for straight in (True, False):
        kref, vref = (kpad, vpad) if straight else (kswp, vswp)
        kw = kref[pl.ds(start, C_WIN), :]
        vw = vref[pl.ds(start, C_WIN), :]
        halves = [int((p >= npair) == straight) for p in range(C_GROUP)]
        qg = jnp.concatenate([qs[p] * (mhi if halves[p] else mlo) for p in range(C_GROUP)], axis=0)
        s_all = lax.dot_general(qg, kw, _NT, preferred_element_type=jnp.float32)
        ps, ls = [], []
        for p in range(C_GROUP):
            h = 2 * p + halves[p]
            s = s_all[p * QB:(p + 1) * QB] * scale + bias_ref[h * QB:(h + 1) * QB, :] + pad_mask
            sink = sink_ref[h] * LOG2E
            m = jnp.maximum(jnp.max(s, axis=-1, keepdims=True), sink)
            e = jnp.exp2(s - m)
            ls.append(jnp.sum(e, axis=-1, keepdims=True) + jnp.exp2(sink - m))
            ps.append(e.astype(jnp.bfloat16))
        o_all = jnp.dot(jnp.concatenate(ps, axis=0), vw, preferred_element_type=jnp.float32)
        stacks.append(([o_all[p * QB:(p + 1) * QB] / ls[p] for p in range(C_GROUP)], halves))
    for p in range(C_GROUP):
        (o1, h1), (o2, _) = stacks
        lo, hi = (o2[p], o1[p]) if h1[p] else (o1[p], o2[p])
        o_ref[:, p * LANES:(p + 1) * LANES] = jnp.where(lane < HEAD_DIM, lo, hi).astype(o_ref.dtype)


def _attn_c(sinks, projc3, bias):
    nq = SEQ // QB
    pad_buf = pltpu.VMEM((SEQ + C_PAD, LANES), jnp.bfloat16)
    return pl.pallas_call(
        _attn_c_kernel,
        grid=(BATCH, nq),
        in_specs=[
            pl.BlockSpec(memory_space=pltpu.SMEM),
            pl.BlockSpec((None, QB, C_W), lambda b, i: (b, i, 0)),
            pl.BlockSpec((None, SEQ, LANES), lambda b, i: (b, 0, C_W // LANES)),
            pl.BlockSpec((None, SEQ, LANES), lambda b, i: (b, 0, C_W // LANES + 1)),
            pl.BlockSpec((C_Q_HEADS * QB, C_WIN), lambda b, i: (0, 0)),
        ],
        out_specs=pl.BlockSpec((None, QB, C_W), lambda b, i: (b, i, 0)),
        out_shape=jax.ShapeDtypeStruct((BATCH, SEQ, C_W), jnp.bfloat16),
        scratch_shapes=[pad_buf, pad_buf, pad_buf, pad_buf],
        compiler_params=_cparams(("arbitrary", "arbitrary")),
        name="attn_c",
    )(sinks, projc3, projc3, projc3, bias)


def _sort_key(x):
    bits = lax.bitcast_convert_type(x + 0.0, jnp.int32)
    return bits ^ ((bits >> 31) & jnp.int32(0x7FFFFFFF))


def _rows_tree(x, op, slab):
    parts = [x[r:r + slab, :] for r in range(0, x.shape[0], slab)]
    while len(parts) > 1:
        nxt = [op(parts[k], parts[k + 1]) for k in range(0, len(parts) - 1, 2)]
        parts = nxt + ([parts[-1]] if len(parts) % 2 else [])
    return parts[0]


def _rows8(x, op):
    return _rows_tree(x, op, SUBLANES)


BF16_ROWS = 16


def _bit_search(count_ge, target, v, above, top_bit):
    def step(t, carry):
        v, above = carry
        cand = v | (jnp.int32(1) << (top_bit - t))
        c = count_ge(cand)
        keep = c >= target
        return jnp.where(keep, cand, v), jnp.where(keep, above, c)

    return lax.fori_loop(0, top_bit + 1, step, (v, above))


def _attn_b_kernel(qb_ref, x_ref, kiw_ref, ckv_ref, gain_ref, wuk_ref, wuv_ref, bias_ref, tri_ref,
                   o_ref, ckvn, kd, ql, qi_all, sk, dg, acc_s, p_s):
    i = pl.program_id(1)
    nkb = (i * BQ) // KB + 1
    mlo, mhi = _lane_masks(jnp.bfloat16)
    f32 = jnp.float32

    @pl.when(i == 0)
    def _():
        ckvn[...] = _rms(ckv_ref[...].astype(f32), gain_ref[...]).astype(ckvn.dtype)
        kiw = kiw_ref[...].astype(f32)
        lane = lax.broadcasted_iota(jnp.int32, kiw.shape, 1)
        kd[...] = jnp.where(lane < HEAD_DIM, kiw, pltpu.roll(kiw, HEAD_DIM, 1)).astype(kd.dtype)

    for h in range(B_HEADS):
        rows = slice(h * BQ, (h + 1) * BQ)
        qlat = jnp.dot(qb_ref[:, (h // 2) * LANES:(h // 2 + 1) * LANES], wuk_ref[h], preferred_element_type=f32)
        ql[rows, :] = (qlat * (HEAD_DIM ** -0.5 * LOG2E)).astype(ql.dtype)
        qcol = O_QI - X_COL0 + (h // 2) * LANES
        qi_all[rows, :] = x_ref[:, qcol:qcol + LANES] * (mhi if h % 2 else mlo)
    wcol = O_KI - X_COL0
    wi_t = x_ref[:, wcol:wcol + LANES].astype(f32).T * (IDX_HEADS ** -0.5 * IDX_DIM ** -0.5)
    wi_rows = [wi_t[O_WI - O_KI + h:O_WI - O_KI + h + 1, :] for h in range(IDX_HEADS)]

    kpos = lax.broadcasted_iota(jnp.int32, (KB, BQ), 0)
    key_limit = ((i * BQ + lax.broadcasted_iota(jnp.int32, (1, BQ), 1)) // CHUNK + 1) * CHUNK

    def admissible(kb):
        return kb * KB + kpos < key_limit

    def score_block(kb, carry):
        kblk = kd[pl.ds(pl.multiple_of(kb * KB, KB), KB), :]
        dots = lax.dot_general(kblk, qi_all[...], _NT, preferred_element_type=f32)
        score = jnp.zeros((KB, BQ), f32)
        for h in range(IDX_HEADS):
            score = score + jnp.maximum(dots[:, h * BQ:(h + 1) * BQ], 0.0) * wi_rows[h]
        sk[kb] = _sort_key(jnp.where(admissible(kb), score, NEG))
        return carry

    lax.fori_loop(0, nkb, score_block, 0)

    neg_key = _sort_key(jnp.full((1, 1), NEG, f32))
    n_rest = ((NKB - nkb) * KB).astype(f32)
    one, zero = jnp.ones((), jnp.bfloat16), jnp.zeros((), jnp.bfloat16)

    def byte_of(key, byte):
        return ((key >> 24) + 128) if byte == 3 else ((key >> (8 * byte)) & 255)

    target = jnp.full((1, BQ), TOPK, f32)
    prefix = jnp.zeros((1, BQ), jnp.int32)
    above = jnp.zeros((1, BQ), f32)
    for byte in (3, 2, 1, 0):
        def in_class(key, byte=byte, prefix=prefix):
            return (key >> (8 * byte + 8)) == prefix

        def prepare(kb, carry, byte=byte, in_class=in_class):
            key = sk[kb]
            digit = byte_of(key, byte).astype(f32)
            if byte < 3:
                digit = jnp.where(in_class(key), digit, -1.0)
            dg[kb] = digit.astype(dg.dtype)
            return carry

        lax.fori_loop(0, nkb, prepare, 0)
        rest_digit = byte_of(neg_key, byte)
        rest_on = in_class(neg_key) if byte < 3 else (neg_key == neg_key)

        def count_ge(cand, rest_digit=rest_digit, rest_on=rest_on):
            cand_b = cand.astype(f32).astype(jnp.bfloat16)

            def body(kb, acc):
                hit = jnp.where(dg[kb] >= cand_b, one, zero)
                return acc + _rows_tree(hit, jnp.add, BF16_ROWS).astype(f32)

            acc = lax.fori_loop(0, nkb, body, jnp.zeros((BF16_ROWS, BQ), f32))
            rest = jnp.where(rest_on & (rest_digit >= cand), n_rest, 0.0)
            return jnp.sum(acc, axis=0, keepdims=True) + rest

        digit_thr, above_here = _bit_search(count_ge, target, jnp.zeros((1, BQ), jnp.int32),
                                            jnp.zeros((1, BQ), f32), 7)
        prefix = (digit_thr - 128) if byte == 3 else (prefix * 256 + digit_thr)
        above = above + above_here
        target = target - above_here
    thr = prefix
    need = TOPK - above

    acc_s[...] = jnp.zeros(acc_s.shape, f32)

    far_bias = bias_ref[B_NEAR, 0:1, :]

    def attend(far, kb, carry):
        eq_seen, m_old, l_old = carry
        keys = sk[kb]
        eq = keys == thr
        eq_f = jnp.where(eq, 1.0, 0.0)
        before = jnp.dot(tri_ref[...], eq_f.astype(jnp.bfloat16), preferred_element_type=f32)
        take_eq = jnp.where(eq, jnp.where(eq_seen + before < need, 1.0, 0.0), 0.0)
        take = jnp.where(keys > thr, 1.0, take_eq)
        mask_add = jnp.where(admissible(kb), jnp.where(take > 0.0, 0.0, NEG), NEG)
        cblk = ckvn[pl.ds(pl.multiple_of(kb * KB, KB), KB), :]
        dots = lax.dot_general(cblk, ql[...], _NT, preferred_element_type=f32)
        tile = i - (KB // BQ) * kb
        m_parts, l_parts, a_parts = [], [], []
        for g in range(HQ // LANES):
            cols = slice(g * LANES, (g + 1) * LANES)
            qcols = slice((g % (BQ // LANES)) * LANES, (g % (BQ // LANES) + 1) * LANES)
            mo = m_old[:, cols]
            if far:
                s = dots[:, cols] + mask_add[:, qcols]
                shift = far_bias[:, cols]
                mn = jnp.maximum(mo, jnp.max(_rows8(s, jnp.maximum), axis=0, keepdims=True) + shift)
                e = jnp.exp2(s - (mn - shift))
            else:
                s = dots[:, cols] + bias_ref[tile, :, cols] + mask_add[:, qcols]
                mn = jnp.maximum(mo, jnp.max(_rows8(s, jnp.maximum), axis=0, keepdims=True))
                e = jnp.exp2(s - mn)
            a = jnp.exp2(mo - mn)
            l_parts.append(a * l_old[:, cols] + jnp.sum(_rows8(e, jnp.add), axis=0, keepdims=True))
            m_parts.append(mn)
            a_parts.append(a)
            p_s[:, cols] = e.astype(p_s.dtype)
        alpha = jnp.concatenate(a_parts, axis=1)
        acc_s[...] = acc_s[...] * alpha + lax.dot_general(cblk, p_s[...], _TN, preferred_element_type=f32)
        eq_seen = eq_seen + jnp.sum(_rows8(eq_f, jnp.add), axis=0, keepdims=True)
        return eq_seen, jnp.concatenate(m_parts, axis=1), jnp.concatenate(l_parts, axis=1)

    init = (jnp.zeros((1, BQ), f32), jnp.full((1, HQ), 4 * NEG, f32), jnp.zeros((1, HQ), f32))
    assert BQ == KB
    n_far = jnp.maximum(nkb - B_NEAR, 0)
    carry = lax.fori_loop(0, n_far, functools.partial(attend, True), init)
    _, _, l_fin = lax.fori_loop(n_far, nkb, functools.partial(attend, False), carry)

    o_lat_t = acc_s[...] / l_fin
    for p in range(B_HEADS // 2):
        out = jnp.zeros((BQ, LANES), f32)
        for h in (2 * p, 2 * p + 1):
            o_lat = o_lat_t[:, h * BQ:(h + 1) * BQ].T.astype(jnp.bfloat16)
            out = out + jnp.dot(o_lat, wuv_ref[h], preferred_element_type=f32)
        o_ref[:, p * LANES:(p + 1) * LANES] = out.astype(o_ref.dtype)


def _attn_b(proj3, gain, wuk, wuv, bias, tri, nbatch=BATCH):
    nq = SEQ // BQ
    xw = IN_A - X_COL0
    return pl.pallas_call(
        _attn_b_kernel,
        grid=(nbatch, nq),
        in_specs=[
            pl.BlockSpec((None, BQ, B_W), lambda b, i: (b, i, O_QB // B_W)),
            pl.BlockSpec((None, BQ, xw), lambda b, i: (b, i, X_COL0 // xw)),
            pl.BlockSpec((None, SEQ, LANES), lambda b, i: (b, 0, O_KI // LANES)),
            pl.BlockSpec((None, SEQ, LANES), lambda b, i: (b, 0, O_CKV // LANES)),
            pl.BlockSpec((1, B_LATENT), lambda b, i: (0, 0)),
            pl.BlockSpec((B_HEADS, LANES, B_LATENT), lambda b, i: (0, 0, 0)),
            pl.BlockSpec((B_HEADS, B_LATENT, LANES), lambda b, i: (0, 0, 0)),
            pl.BlockSpec((B_NEAR + 1, KB, HQ), lambda b, i: (0, 0, 0)),
            pl.BlockSpec((KB, KB), lambda b, i: (0, 0)),
        ],
        out_specs=pl.BlockSpec((None, BQ, B_W), lambda b, i: (b, i, 0)),
        out_shape=jax.ShapeDtypeStruct((nbatch, SEQ, B_W), jnp.bfloat16),
        scratch_shapes=[
            pltpu.VMEM((SEQ, B_LATENT), jnp.bfloat16),
            pltpu.VMEM((SEQ, LANES), jnp.bfloat16),
            pltpu.VMEM((HQ, B_LATENT), jnp.bfloat16),
            pltpu.VMEM((HQ, LANES), jnp.bfloat16),
            pltpu.VMEM((NKB, KB, BQ), jnp.int32),
            pltpu.VMEM((NKB, KB, BQ), jnp.bfloat16),
            pltpu.VMEM((B_LATENT, HQ), jnp.float32),
            pltpu.VMEM((KB, HQ), jnp.bfloat16),
        ],
        compiler_params=_cparams(("arbitrary", "arbitrary")),
        name="attn_b",
    )(proj3, proj3, proj3, proj3, gain, wuk, wuv, bias, tri)


MERGE_BN = 1024
MERGE_BM = 1024


def _merge_kernel(oa_ref, ob_ref, oc_ref, ga_ref, gb_ref, gc_ref, wa_ref, wb_ref, wc_ref, o_ref, wa, wb, wc):
    f32 = jnp.float32

    @pl.when(pl.program_id(1) == 0)
    def _():
        wa[...] = wa_ref[...].astype(wa.dtype)
        wb[...] = wb_ref[...].astype(wb.dtype)
        wc[...] = wc_ref[...].astype(wc.dtype)

    m = ga_ref[...].astype(f32) * jnp.dot(oa_ref[...], wa[...], preferred_element_type=f32)
    m = m + gb_ref[...].astype(f32) * jnp.dot(ob_ref[...], wb[...], preferred_element_type=f32)
    m = m + gc_ref[...].astype(f32) * jnp.dot(oc_ref[...], wc[...], preferred_element_type=f32)
    o_ref[...] = m.astype(o_ref.dtype)


def _merge(oa, ob, oc, gates, w_branch, layer):
    t = oa.shape[0]
    bn, bm = MERGE_BN, MERGE_BM
    gstep = D_MODEL // bn
    return pl.pallas_call(
        _merge_kernel,
        grid=(D_MODEL // bn, t // bm),
        in_specs=[
            pl.BlockSpec((bm, A_W), lambda j, i: (i, 0)),
            pl.BlockSpec((bm, B_W), lambda j, i: (i, 0)),
            pl.BlockSpec((bm, C_W), lambda j, i: (i, 0)),
            pl.BlockSpec((bm, bn), lambda j, i: (i, j)),
            pl.BlockSpec((bm, bn), lambda j, i: (i, gstep + j)),
            pl.BlockSpec((bm, bn), lambda j, i: (i, 2 * gstep + j)),
            pl.BlockSpec((None, A_W, bn), lambda j, i: (layer, 0, j)),
            pl.BlockSpec((None, B_W, bn), lambda j, i: (layer, A_W // B_W, j)),
            pl.BlockSpec((None, C_W, bn), lambda j, i: (layer, (A_W + B_W) // C_W, j)),
        ],
        out_specs=pl.BlockSpec((bm, bn), lambda j, i: (i, j)),
        out_shape=jax.ShapeDtypeStruct((t, D_MODEL), jnp.bfloat16),
        scratch_shapes=[pltpu.VMEM((A_W, bn), jnp.bfloat16), pltpu.VMEM((B_W, bn), jnp.bfloat16),
                        pltpu.VMEM((C_W, bn), jnp.bfloat16)],
        compiler_params=_cparams(("arbitrary", "arbitrary")),
        name="merge",
    )(oa, ob, oc, gates, gates, gates, w_branch, w_branch, w_branch)


FINISH_CHUNKS = 4


def _finish_chunked(y_of_rows, nrows, x_ref, gp_ref, gn_ref, xo_ref, ho_ref):
    step = nrows // FINISH_CHUNKS
    for r in range(0, nrows, step):
        rows = slice(r, r + step)
        xn = x_ref[rows, :] + _rms(y_of_rows(rows), gp_ref[...])
        xo_ref[rows, :] = xn
        if ho_ref is not None:
            ho_ref[rows, :] = _rms(xn, gn_ref[...]).astype(ho_ref.dtype)


def _gemm_res_kernel(nk, with_next, a_ref, w_ref, x_ref, gp_ref, gn_ref, xo_ref, *rest):
    ho_ref = rest[0] if with_next else None
    f32 = jnp.float32
    bm = a_ref.shape[0]
    if nk == 1:
        _finish_chunked(lambda rows: jnp.dot(a_ref[rows, :], w_ref[...], preferred_element_type=f32),
                        bm, x_ref, gp_ref, gn_ref, xo_ref, ho_ref)
        return
    acc = rest[-1]
    k = pl.program_id(1)

    @pl.when(k == 0)
    def _():
        acc[...] = jnp.dot(a_ref[...], w_ref[...], preferred_element_type=f32)

    @pl.when((k > 0) & (k < nk - 1))
    def _():
        acc[...] += jnp.dot(a_ref[...], w_ref[...], preferred_element_type=f32)

    @pl.when(k == nk - 1)
    def _():
        _finish_chunked(lambda rows: acc[rows, :] + jnp.dot(a_ref[rows, :], w_ref[...], preferred_element_type=f32),
                        bm, x_ref, gp_ref, gn_ref, xo_ref, ho_ref)


def _gemm_res(a, w, layer, x, g_post, g_next, bk, bm=512, name="gemm_res"):
    t, kdim = a.shape
    n = w.shape[2]
    nk = kdim // bk
    with_next = g_next is not None
    if g_next is None:
        g_next = g_post
    out_shape = [jax.ShapeDtypeStruct((t, n), jnp.float32)]
    out_specs = [pl.BlockSpec((bm, n), lambda i, k: (i, 0))]
    if with_next:
        out_shape.append(jax.ShapeDtypeStruct((t, n), jnp.bfloat16))
        out_specs.append(pl.BlockSpec((bm, n), lambda i, k: (i, 0)))
    res = pl.pallas_call(
        functools.partial(_gemm_res_kernel, nk, with_next),
        grid=(t // bm, nk),
        in_specs=[
            pl.BlockSpec((bm, bk), lambda i, k: (i, k)),
            pl.BlockSpec((None, bk, n), lambda i, k: (layer, k, 0)),
            pl.BlockSpec((bm, n), lambda i, k: (i, 0)),
            pl.BlockSpec((1, n), lambda i, k: (0, 0)),
            pl.BlockSpec((1, n), lambda i, k: (0, 0)),
        ],
        out_specs=out_specs,
        out_shape=out_shape,
        scratch_shapes=[pltpu.VMEM((bm, n), jnp.float32)] if nk > 1 else [],
        compiler_params=_cparams(("parallel", "arbitrary")),
        name=name,
    )(a, w, x, g_post, g_next)
    return (res[0], res[1]) if with_next else (res[0], None)


def _cast_once(pairs):
    @pl.when(pl.program_id(0) == 0)
    def _():
        for src, dst in pairs:
            dst[...] = src[...].astype(dst.dtype)


def _memkv_kernel(m_ref, g_ref, w_ref, o_ref, wb):
    _cast_once([(w_ref, wb)])
    mn = _rms(m_ref[...], g_ref[...]).astype(jnp.bfloat16)
    o_ref[...] = jnp.dot(mn, wb[...], preferred_element_type=jnp.float32).astype(o_ref.dtype)


def _memkv(mem2, g, w, layer):
    t, d = mem2.shape
    n = w.shape[2]
    bm = 512
    return pl.pallas_call(
        _memkv_kernel,
        grid=(t // bm,),
        in_specs=[pl.BlockSpec((bm, d), lambda i: (i, 0)), pl.BlockSpec((1, d), lambda i: (0, 0)),
                  pl.BlockSpec((None, d, n), lambda i: (layer, 0, 0))],
        out_specs=pl.BlockSpec((bm, n), lambda i: (i, 0)),
        out_shape=jax.ShapeDtypeStruct((t, n), jnp.bfloat16),
        scratch_shapes=[pltpu.VMEM((d, n), jnp.bfloat16)],
        compiler_params=_cparams(("arbitrary",)),
        name="mem_kv",
    )(mem2, g, w)


XA_BM = 512


def _xattn_kernel(h_ref, wqf_ref, kv_ref, wof_ref, x_ref, gp_ref, gn_ref, xo_ref, ho_ref, wq_ref, wo_ref):
    f32 = jnp.float32
    _cast_once([(wqf_ref, wq_ref), (wof_ref, wo_ref)])
    q = jnp.dot(h_ref[...], wq_ref[...], preferred_element_type=f32).astype(jnp.bfloat16)
    scale = MEM_HEAD_DIM ** -0.5 * LOG2E
    hcols = [slice(h * LANES, (h + 1) * LANES) for h in range(MEM_HEADS)]
    ss = [lax.dot_general(q[:, c], kv_ref[:, c], _NT, preferred_element_type=f32) * scale for c in hcols]
    ms = [jnp.max(s, axis=-1, keepdims=True) for s in ss]
    es = [jnp.exp2(s - m) for s, m in zip(ss, ms)]
    ls = [jnp.sum(e, axis=-1, keepdims=True) for e in es]
    pv = [jnp.dot(e.astype(jnp.bfloat16), kv_ref[:, MEM_W + h * LANES:MEM_W + (h + 1) * LANES],
                  preferred_element_type=f32) for h, e in enumerate(es)]
    o = jnp.concatenate([(o_h / l).astype(jnp.bfloat16) for o_h, l in zip(pv, ls)], axis=-1)
    _finish_chunked(lambda rows: jnp.dot(o[rows, :], wo_ref[...], preferred_element_type=f32),
                    o.shape[0], x_ref, gp_ref, gn_ref, xo_ref, ho_ref)


def _xattn(h, wq, kv, wo, layer, x, g_post, g_next):
    t, d = h.shape
    bm = XA_BM
    per_batch = SEQ // bm
    return pl.pallas_call(
        _xattn_kernel,
        grid=(t // bm,),
        in_specs=[
            pl.BlockSpec((bm, d), lambda i: (i, 0)),
            pl.BlockSpec((None, d, MEM_W), lambda i: (layer, 0, 0)),
            pl.BlockSpec((MEM_LEN, 2 * MEM_W), lambda i: (i // per_batch, 0)),
            pl.BlockSpec((None, MEM_W, d), lambda i: (layer, 0, 0)),
            pl.BlockSpec((bm, d), lambda i: (i, 0)),
            pl.BlockSpec((1, d), lambda i: (0, 0)),
            pl.BlockSpec((1, d), lambda i: (0, 0)),
        ],
        out_specs=[pl.BlockSpec((bm, d), lambda i: (i, 0)), pl.BlockSpec((bm, d), lambda i: (i, 0))],
        out_shape=[jax.ShapeDtypeStruct((t, d), jnp.float32), jax.ShapeDtypeStruct((t, d), jnp.bfloat16)],
        scratch_shapes=[pltpu.VMEM((d, MEM_W), jnp.bfloat16), pltpu.VMEM((MEM_W, d), jnp.bfloat16)],
        compiler_params=_cparams(("arbitrary",)),
        name="mem_xattn",
    )(h, wq, kv, wo, x, g_post, g_next)


def _ffn_up_kernel(h_ref, wgf_ref, wvf_ref, cwg_ref, cwv_ref, cbg_ref, cbv_ref, o_ref, wg_ref, wv_ref, ug, uv):
    j = pl.program_id(0)
    i = pl.program_id(1)
    tiles_per_seq = SEQ // FF_BM
    last = FF_P // FF_BN - 1
    valid = D_FF - last * FF_BN
    shift = FF_BN - valid

    @pl.when((i == 0) & (j < last))
    def _():
        wg_ref[...] = wgf_ref[...].astype(wg_ref.dtype)
        wv_ref[...] = wvf_ref[...].astype(wv_ref.dtype)

    @pl.when((i == 0) & (j == last))
    def _():
        zeros = jnp.zeros((wg_ref.shape[0], FF_BN - valid), wg_ref.dtype)
        wg_ref[:, :valid] = wgf_ref[:, :valid].astype(wg_ref.dtype)
        wv_ref[:, :valid] = wvf_ref[:, shift:].astype(wv_ref.dtype)
        wg_ref[:, valid:] = zeros
        wv_ref[:, valid:] = zeros

    for u in (ug, uv):
        @pl.when(i % tiles_per_seq == 0)
        def _():
            u[0:SUBLANES, :] = jnp.zeros((SUBLANES, FF_BN), jnp.float32)

        @pl.when(i % tiles_per_seq != 0)
        def _():
            u[0:SUBLANES, :] = u[FF_BM:FF_BM + SUBLANES, :]

    def conv(u, cw_ref, cb_ref, r0):
        base = SUBLANES + r0
        acc = cb_ref[...] + u[base - 2:base - 2 + FF_CH, :] * cw_ref[0:1, :]
        acc = acc + u[base - 1:base - 1 + FF_CH, :] * cw_ref[1:2, :]
        return acc + u[base:base + FF_CH, :] * cw_ref[2:3, :]

    for c in range(FF_BM // FF_CH):
        r0 = c * FF_CH
        hb = h_ref[r0:r0 + FF_CH, :]
        ug[SUBLANES + r0:SUBLANES + r0 + FF_CH, :] = jnp.dot(hb, wg_ref[...], preferred_element_type=jnp.float32)
        uv[SUBLANES + r0:SUBLANES + r0 + FF_CH, :] = jnp.dot(hb, wv_ref[...], preferred_element_type=jnp.float32)
        gate = conv(ug, cwg_ref, cbg_ref, r0)
        val = conv(uv, cwv_ref, cbv_ref, r0)
        o_ref[r0:r0 + FF_CH, :] = (jax.nn.gelu(gate) * val).astype(o_ref.dtype)


def _ffn_up(h, w_up, layer, conv_w, conv_b):
    t, d = h.shape
    nj = FF_P // FF_BN
    w_block = (None, pl.Element(d), pl.Element(FF_BN))
    return pl.pallas_call(
        _ffn_up_kernel,
        grid=(nj, t // FF_BM),
        in_specs=[
            pl.BlockSpec((FF_BM, d), lambda j, i: (i, 0)),
            pl.BlockSpec(w_block, lambda j, i: (layer, 0, pl.multiple_of(j * FF_BN, LANES))),
            pl.BlockSpec(w_block, lambda j, i: (
                layer, 0, pl.multiple_of(jnp.minimum(D_FF + j * FF_BN, 2 * D_FF - FF_BN), LANES))),
            pl.BlockSpec((CONV_W, FF_BN), lambda j, i: (0, j)),
            pl.BlockSpec((CONV_W, FF_BN), lambda j, i: (0, nj + j)),
            pl.BlockSpec((1, FF_BN), lambda j, i: (0, j)),
            pl.BlockSpec((1, FF_BN), lambda j, i: (0, nj + j)),
        ],
        out_specs=pl.BlockSpec((FF_BM, FF_BN), lambda j, i: (i, j)),
        out_shape=jax.ShapeDtypeStruct((t, FF_P), jnp.bfloat16),
        scratch_shapes=[pltpu.VMEM((d, FF_BN), jnp.bfloat16), pltpu.VMEM((d, FF_BN), jnp.bfloat16),
                        pltpu.VMEM((FF_BM + SUBLANES, FF_BN), jnp.float32),
                        pltpu.VMEM((FF_BM + SUBLANES, FF_BN), jnp.float32)],
        compiler_params=_cparams(("arbitrary", "arbitrary")),
        name="ffn_up",
    )(h, w_up, w_up, conv_w, conv_w, conv_b, conv_b)


def _pad_heads(w, axis):
    h = w.shape[0]
    zero = jnp.zeros_like(w)
    even = jnp.concatenate([w, zero], axis=axis)
    odd = jnp.concatenate([zero, w], axis=axis)
    sel = (jnp.arange(h) % 2 == 0).reshape((h, 1, 1))
    return jnp.where(sel, even, odd)


def _toeplitz(fn, rows, cols):
    ks = np.concatenate([np.arange(0, cols), np.arange(-(rows - 1), 0)])
    w = fn(ks)
    h, period = w.shape
    x = jnp.tile(w, (1, rows))[:, :rows * (period - 1)].reshape(h, rows, period - 1)
    return x[:, :, :cols].astype(jnp.float32)


def _band(rows, cols, left):
    diff = left + np.arange(rows)[:, None] // CHUNK - np.arange(cols)[None, :] // CHUNK
    return (diff >= 0) & (diff <= left)


def _bias_a(rel_bias):
    fn = lambda ks: rel_bias[np.clip(A_PAD - ks, -A_MAX_REL, A_MAX_REL) + A_MAX_REL].T
    bias = jnp.where(_band(QB, A_WIN, A_LEFT_CHUNKS)[None], _toeplitz(fn, QB, A_WIN) * LOG2E, NEG)
    return bias.reshape(A_HEADS * QB, A_WIN)


def _bias_c(t5_c):
    fn = lambda ks: t5_c[_t5_bucket(jnp.asarray(ks - C_PAD, jnp.int32))].T
    bias = jnp.where(_band(QB, C_WIN, C_LEFT_CHUNKS)[None], _toeplitz(fn, QB, C_WIN) * LOG2E, NEG)
    return bias.reshape(C_Q_HEADS * QB, C_WIN)


def _bias_b(t5_b):
    tiles = []
    for n in range(B_NEAR + 1):
        off = BQ * n if n < B_NEAR else SEQ
        fn = lambda ks, off=off: t5_b[_t5_bucket(jnp.asarray(-ks - off, jnp.int32))].T
        tile = _toeplitz(fn, KB, BQ)
        tiles.append(jnp.transpose(tile * LOG2E, (1, 0, 2)).reshape(KB, HQ))
    return jnp.stack(tiles)


def _pad_ff(a, dtype):
    z = jnp.zeros((a.shape[0], FF_P - D_FF), dtype)
    return jnp.concatenate([a[:, :D_FF].astype(dtype), z, a[:, D_FF:].astype(dtype), z], axis=1)


def kernel(x, mem, t5_table, norm_gains, w_in, a_rel_bias, ckv_gain, w_uk, w_uv, sinks, w_branch, w_o,
           mem_gain, w_mq, w_mkv, w_mo, w_up, conv_w, conv_b, w_down):
    bf16 = jnp.bfloat16
    xs = x.reshape(TOKENS, D_MODEL)
    mem2 = mem.reshape(BATCH * MEM_LEN, D_MODEL)
    tri = jnp.asarray(np.tril(np.ones((KB, KB), np.float32), -1), bf16)
    bias_b = _bias_b(t5_table[:, :B_HEADS])
    bias_c = _bias_c(t5_table[:, B_HEADS:])
    gains = norm_gains.reshape(DEPTH, 6, 1, D_MODEL)
    w_o_b = w_o.astype(bf16)
    w_dn_b = jnp.concatenate([w_down.astype(bf16), jnp.zeros((DEPTH, FF_P - D_FF, D_MODEL), bf16)], axis=1)
    w_in_t = jnp.swapaxes(w_in, 1, 2)

    h = _norm(xs, gains[0, 0])
    for l in range(DEPTH):
        g = gains[l]
        proja = _proj(h, w_in_t, l, 0, IN_A, 1024, gate=False, name="in_proj_a")
        projc = _proj(h, w_in_t, l, O_QC, C_COLS, C_COLS, gate=False, name="in_proj_c")
        gates = _proj(h, w_in_t, l, O_GL, 3 * D_MODEL, 1024, gate=True, name="in_proj_g")
        proja3 = proja.reshape(BATCH, SEQ, IN_A)
        oa = _attn_a(proja3, _bias_a(a_rel_bias[l]))
        wuk = jnp.transpose(_pad_heads(w_uk[l], axis=2), (0, 2, 1)).astype(bf16)
        wuv = _pad_heads(w_uv[l], axis=2).astype(bf16)
        ob = _attn_b(proja3, ckv_gain[l].reshape(1, B_LATENT), wuk, wuv, bias_b, tri)
        oc = _attn_c(sinks[l], projc.reshape(BATCH, SEQ, C_COLS), bias_c)
        merged = _merge(oa.reshape(TOKENS, A_W), ob.reshape(TOKENS, B_W), oc.reshape(TOKENS, C_W), gates, w_branch, l)
        xs, h = _gemm_res(merged, w_o_b, l, xs, g[1], g[2], bk=D_MODEL, name="out_proj")
        kv = _memkv(mem2, mem_gain[l].reshape(1, D_MODEL), w_mkv, l)
        xs, h = _xattn(h, w_mq, kv, w_mo, l, xs, g[3], g[4])
        hidden = _ffn_up(h, w_up, l, _pad_ff(conv_w[l], jnp.float32), _pad_ff(conv_b[l].reshape(1, -1), jnp.float32))
        g_next = gains[l + 1, 0] if l + 1 < DEPTH else None
        xs, h = _gemm_res(hidden, w_dn_b, l, xs, g[5], g_next, bk=DOWN_BK, name="ffn_down")
    return xs.reshape(BATCH, SEQ, D_MODEL)
```

```python
import functools
import math

import numpy as np
import jax
import jax.numpy as jnp
from jax import lax
from jax.experimental import pallas as pl
from jax.experimental.pallas import tpu as pltpu

D_MODEL = 2048
BATCH = 4
SEQ = 2048
DEPTH = 2
TOKENS = BATCH * SEQ
CHUNK = 64
EPS = 1e-6
NEG = -1e30
LOG2E = math.log2(math.e)
A_HEADS = 8
A_LEFT_CHUNKS = 8
A_MAX_REL = 128
A_W = 512
B_HEADS = 8
B_W = 512
B_LATENT = 128
IDX_HEADS = 8
IDX_DIM = 64
TOPK = 256
C_Q_HEADS = 16
C_GROUP = 8
C_W = 1024
C_LEFT_CHUNKS = 2
T5_BUCKETS = 32
T5_MAX_DIST = 256
MEM_LEN = 256
MEM_HEADS = 4
MEM_HEAD_DIM = 128
MEM_W = 512
D_FF = 5504
CONV_W = 3

LANES = 128
SUBLANES = 8
HEAD_DIM = 64
QB = 128
KB = 256
VMEM_LIMIT = 56 * 1024 * 1024

O_QA, O_KA, O_VA, O_QB = 0, 512, 1024, 1536
O_CKV = 2048
O_QI = 2176
O_KI = 2688
O_WI = 2752
O_QC = 2760
O_GL = 4040
IN_W = O_GL + 3 * D_MODEL
IN_A = 3072
X_COL0 = 2048
C_COLS = O_GL - O_QC
PROJ_BM = 1024
PROJ_TR = 256

FF_P = 5632
FF_BN = 512
FF_BM = 1024
FF_CH = 256
DOWN_BK = 1408

A_WIN = (A_LEFT_CHUNKS + 2) * CHUNK
C_WIN = (C_LEFT_CHUNKS + 2) * CHUNK
A_PAD = A_LEFT_CHUNKS * CHUNK
C_PAD = C_LEFT_CHUNKS * CHUNK
BQ = 256
B_NEAR = 2
NKB = SEQ // KB
HQ = B_HEADS * BQ

_NT = (((1,), (1,)), ((), ()))
_TN = (((0,), (0,)), ((), ()))


def _cparams(sem):
    return pltpu.CompilerParams(dimension_semantics=sem, vmem_limit_bytes=VMEM_LIMIT)


def _t5_bucket(rel):
    half = T5_BUCKETS // 2
    max_exact = half // 2
    sign = jnp.where(rel > 0, half, 0)
    d = jnp.abs(rel)
    d_f = jnp.maximum(d, 1).astype(jnp.float32)
    large = max_exact + (jnp.log(d_f / max_exact) / math.log(T5_MAX_DIST / max_exact) * (half - max_exact)).astype(jnp.int32)
    large = jnp.minimum(large, half - 1)
    return sign + jnp.where(d < max_exact, d, large)


def _far_bucket_is_constant():
    d = np.arange(BQ * B_NEAR - (KB - 1), SEQ, dtype=np.float32)
    assert d[0] > T5_MAX_DIST
    large = 8 + (np.log(d / 8) / math.log(T5_MAX_DIST / 8) * 8).astype(np.int32)
    return bool(np.all(np.minimum(large, 15) == 15))


assert _far_bucket_is_constant()


def _rms(v, g):
    return v * lax.rsqrt(jnp.mean(v * v, axis=-1, keepdims=True) + EPS) * g


def _norm_kernel(x_ref, g_ref, o_ref):
    o_ref[...] = _rms(x_ref[...], g_ref[...]).astype(o_ref.dtype)


def _norm(x, g, bm=1024):
    t, d = x.shape
    return pl.pallas_call(
        _norm_kernel,
        grid=(t // bm,),
        in_specs=[pl.BlockSpec((bm, d), lambda i: (i, 0)), pl.BlockSpec((1, d), lambda i: (0, 0))],
        out_specs=pl.BlockSpec((bm, d), lambda i: (i, 0)),
        out_shape=jax.ShapeDtypeStruct((t, d), jnp.bfloat16),
        compiler_params=_cparams(("parallel",)),
        name="rmsnorm",
    )(x, g)


def _proj_kernel(gate, h_ref, wt_ref, o_ref, wb):
    @pl.when(pl.program_id(1) == 0)
    def _():
        for c in range(wt_ref.shape[0] // PROJ_TR):
            rows = slice(c * PROJ_TR, (c + 1) * PROJ_TR)
            wb[:, rows] = wt_ref[rows, :].T.astype(wb.dtype)

    acc = jnp.dot(h_ref[...], wb[...], preferred_element_type=jnp.float32)
    o_ref[...] = (jax.nn.sigmoid(acc) if gate else acc).astype(o_ref.dtype)


def _proj(h, w_t, layer, col0, n, bn, gate, name):
    t, k = h.shape
    assert col0 % SUBLANES == 0 and bn % SUBLANES == 0
    w_spec = pl.BlockSpec((None, pl.Element(bn), pl.Element(k)),
                          lambda j, i: (layer, pl.multiple_of(col0 + j * bn, SUBLANES), 0))
    return pl.pallas_call(
        functools.partial(_proj_kernel, gate),
        grid=(n // bn, t // PROJ_BM),
        in_specs=[pl.BlockSpec((PROJ_BM, k), lambda j, i: (i, 0)), w_spec],
        out_specs=pl.BlockSpec((PROJ_BM, bn), lambda j, i: (i, j)),
        out_shape=jax.ShapeDtypeStruct((t, n), jnp.bfloat16),
        scratch_shapes=[pltpu.VMEM((k, bn), jnp.bfloat16)],
        compiler_params=_cparams(("arbitrary", "arbitrary")),
        name=name,
    )(h, w_t)


def _lane_masks(dtype):
    lane = lax.broadcasted_iota(jnp.int32, (1, LANES), 1)
    lo = (lane < HEAD_DIM).astype(jnp.float32)
    return lo.astype(dtype), (1.0 - lo).astype(dtype)


def _swap_halves(x):
    return pltpu.roll(x.astype(jnp.float32), HEAD_DIM, 1).astype(x.dtype)


def _fill_padded(src, dst_ref, pad):
    dst_ref[0:pad, :] = jnp.zeros((pad, dst_ref.shape[1]), dst_ref.dtype)
    dst_ref[pad:pad + SEQ, :] = src


def _attn_a_kernel(q_ref, k_ref, v_ref, bias_ref, o_ref, kpad, vpad):
    i = pl.program_id(1)

    @pl.when(i == 0)
    def _():
        _fill_padded(k_ref[...], kpad, A_PAD)
        _fill_padded(v_ref[...], vpad, A_PAD)

    start = pl.multiple_of(i * QB, QB)
    kw = kpad[pl.ds(start, A_WIN), :]
    vw = vpad[pl.ds(start, A_WIN), :]
    mlo, mhi = _lane_masks(jnp.bfloat16)
    lane = lax.broadcasted_iota(jnp.int32, (QB, LANES), 1)
    kchunk = lax.broadcasted_iota(jnp.int32, (QB, A_WIN), 1) // CHUNK
    pad_mask = jnp.where(kchunk >= A_LEFT_CHUNKS - 2 * i, 0.0, NEG)
    scale = HEAD_DIM ** -0.5 * LOG2E
    pad2 = jnp.concatenate([pad_mask, pad_mask], axis=0)
    npair = A_HEADS // 2
    pcols = [slice(p * LANES, (p + 1) * LANES) for p in range(npair)]
    ss = []
    for p in range(npair):
        qp = q_ref[:, pcols[p]]
        q2 = jnp.concatenate([qp * mlo, qp * mhi], axis=0)
        s = lax.dot_general(q2, kw[:, pcols[p]], _NT, preferred_element_type=jnp.float32)
        ss.append(s * scale + bias_ref[2 * p * QB:(2 * p + 2) * QB, :] + pad2)
    ms = [jnp.max(s, axis=-1, keepdims=True) for s in ss]
    es = [jnp.exp2(s - m) for s, m in zip(ss, ms)]
    ls = [jnp.sum(e, axis=-1, keepdims=True) for e in es]
    outs = [jnp.dot(e.astype(jnp.bfloat16), vw[:, pcols[p]], preferred_element_type=jnp.float32)
            for p, e in enumerate(es)]
    for p in range(npair):
        o = outs[p] / ls[p]
        o_ref[:, pcols[p]] = jnp.where(lane < HEAD_DIM, o[:QB], o[QB:]).astype(o_ref.dtype)


def _attn_a(proj3, bias):
    nq = SEQ // QB
    return pl.pallas_call(
        _attn_a_kernel,
        grid=(BATCH, nq),
        in_specs=[
            pl.BlockSpec((None, QB, A_W), lambda b, i: (b, i, O_QA // A_W)),
            pl.BlockSpec((None, SEQ, A_W), lambda b, i: (b, 0, O_KA // A_W)),
            pl.BlockSpec((None, SEQ, A_W), lambda b, i: (b, 0, O_VA // A_W)),
            pl.BlockSpec((A_HEADS * QB, A_WIN), lambda b, i: (0, 0)),
        ],
        out_specs=pl.BlockSpec((None, QB, A_W), lambda b, i: (b, i, 0)),
        out_shape=jax.ShapeDtypeStruct((BATCH, SEQ, A_W), jnp.bfloat16),
        scratch_shapes=[pltpu.VMEM((SEQ + A_PAD, A_W), jnp.bfloat16), pltpu.VMEM((SEQ + A_PAD, A_W), jnp.bfloat16)],
        compiler_params=_cparams(("arbitrary", "arbitrary")),
        name="attn_a",
    )(proj3, proj3, proj3, bias)


def _attn_c_kernel(sink_ref, q_ref, k_ref, v_ref, bias_ref, o_ref, kpad, kswp, vpad, vswp):
    i = pl.program_id(1)

    @pl.when(i == 0)
    def _():
        k = k_ref[...]
        v = v_ref[...]
        _fill_padded(k, kpad, C_PAD)
        _fill_padded(_swap_halves(k), kswp, C_PAD)
        _fill_padded(v, vpad, C_PAD)
        _fill_padded(_swap_halves(v), vswp, C_PAD)

    start = pl.multiple_of(i * QB, QB)
    mlo, mhi = _lane_masks(jnp.bfloat16)
    lane = lax.broadcasted_iota(jnp.int32, (QB, LANES), 1)
    kchunk = lax.broadcasted_iota(jnp.int32, (QB, C_WIN), 1) // CHUNK
    pad_mask = jnp.where(kchunk >= C_LEFT_CHUNKS - 2 * i, 0.0, NEG)
    scale = HEAD_DIM ** -0.5 * LOG2E
    qs = [q_ref[:, p * LANES:(p + 1) * LANES] for p in range(C_GROUP)]
    npair = C_GROUP // 2
    stacks = []
    for straight in (True, False):
        kref, vref = (kpad, vpad) if straight else (kswp, vswp)
        kw = kref[pl.ds(start, C_WIN), :]
        vw = vref[pl.ds(start, C_WIN), :]
        halves = [int((p >= npair) == straight) for p in range(C_GROUP)]
        qg = jnp.concatenate([qs[p] * (mhi if halves[p] else mlo) for p in range(C_GROUP)], axis=0)
        s_all = lax.dot_general(qg, kw, _NT, preferred_element_type=jnp.float32)
        ps, ls = [], []
        for p in range(C_GROUP):
            h = 2 * p + halves[p]
            s = s_all[p * QB:(p + 1) * QB] * scale + bias_ref[h * QB:(h + 1) * QB, :] + pad_mask
            sink = sink_ref[h] * LOG2E
            m = jnp.maximum(jnp.max(s, axis=-1, keepdims=True), sink)
            e = jnp.exp2(s - m)
            ls.append(jnp.sum(e, axis=-1, keepdims=True) + jnp.exp2(sink - m))
            ps.append(e.astype(jnp.bfloat16))
        o_all = jnp.dot(jnp.concatenate(ps, axis=0), vw, preferred_element_type=jnp.float32)
        stacks.append(([o_all[p * QB:(p + 1) * QB] / ls[p] for p in range(C_GROUP)], halves))
    for p in range(C_GROUP):
        (o1, h1), (o2, _) = stacks
        lo, hi = (o2[p], o1[p]) if h1[p] else (o1[p], o2[p])
        o_ref[:, p * LANES:(p + 1) * LANES] = jnp.where(lane < HEAD_DIM, lo, hi).astype(o_ref.dtype)


def _attn_c(sinks, projc3, bias):
    nq = SEQ // QB
    pad_buf = pltpu.VMEM((SEQ + C_PAD, LANES), jnp.bfloat16)
    return pl.pallas_call(
        _attn_c_kernel,
        grid=(BATCH, nq),
        in_specs=[
            pl.BlockSpec(memory_space=pltpu.SMEM),
            pl.BlockSpec((None, QB, C_W), lambda b, i: (b, i, 0)),
            pl.BlockSpec((None, SEQ, LANES), lambda b, i: (b, 0, C_W // LANES)),
            pl.BlockSpec((None, SEQ, LANES), lambda b, i: (b, 0, C_W // LANES + 1)),
            pl.BlockSpec((C_Q_HEADS * QB, C_WIN), lambda b, i: (0, 0)),
        ],
        out_specs=pl.BlockSpec((None, QB, C_W), lambda b, i: (b, i, 0)),
        out_shape=jax.ShapeDtypeStruct((BATCH, SEQ, C_W), jnp.bfloat16),
        scratch_shapes=[pad_buf, pad_buf, pad_buf, pad_buf],
        compiler_params=_cparams(("arbitrary", "arbitrary")),
        name="attn_c",
    )(sinks, projc3, projc3, projc3, bias)


def _sort_key(x):
    bits = lax.bitcast_convert_type(x + 0.0, jnp.int32)
    return bits ^ ((bits >> 31) & jnp.int32(0x7FFFFFFF))


def _rows_tree(x, op, slab):
    parts = [x[r:r + slab, :] for r in range(0, x.shape[0], slab)]
    while len(parts) > 1:
        nxt = [op(parts[k], parts[k + 1]) for k in range(0, len(parts) - 1, 2)]
        parts = nxt + ([parts[-1]] if len(parts) % 2 else [])
    return parts[0]


def _rows8(x, op):
    return _rows_tree(x, op, SUBLANES)


BF16_ROWS = 16


def _bit_search(count_ge, target, v, above, top_bit):
    def step(t, carry):
        v, above = carry
        cand = v | (jnp.int32(1) << (top_bit - t))
        c = count_ge(cand)
        keep = c >= target
        return jnp.where(keep, cand, v), jnp.where(keep, above, c)

    return lax.fori_loop(0, top_bit + 1, step, (v, above))


def _attn_b_kernel(qb_ref, x_ref, kiw_ref, ckv_ref, gain_ref, wuk_ref, wuv_ref, bias_ref, tri_ref,
                   o_ref, ckvn, kd, ql, qi_all, sk, dg, acc_s, p_s):
    i = pl.program_id(1)
    nkb = (i * BQ) // KB + 1
    mlo, mhi = _lane_masks(jnp.bfloat16)
    f32 = jnp.float32

    @pl.when(i == 0)
    def _():
        ckvn[...] = _rms(ckv_ref[...].astype(f32), gain_ref[...]).astype(ckvn.dtype)
        kiw = kiw_ref[...].astype(f32)
        lane = lax.broadcasted_iota(jnp.int32, kiw.shape, 1)
        kd[...] = jnp.where(lane < HEAD_DIM, kiw, pltpu.roll(kiw, HEAD_DIM, 1)).astype(kd.dtype)

    for h in range(B_HEADS):
        rows = slice(h * BQ, (h + 1) * BQ)
        qlat = jnp.dot(qb_ref[:, (h // 2) * LANES:(h // 2 + 1) * LANES], wuk_ref[h], preferred_element_type=f32)
        ql[rows, :] = (qlat * (HEAD_DIM ** -0.5 * LOG2E)).astype(ql.dtype)
        qcol = O_QI - X_COL0 + (h // 2) * LANES
        qi_all[rows, :] = x_ref[:, qcol:qcol + LANES] * (mhi if h % 2 else mlo)
    wcol = O_KI - X_COL0
    wi_t = x_ref[:, wcol:wcol + LANES].astype(f32).T * (IDX_HEADS ** -0.5 * IDX_DIM ** -0.5)
    wi_rows = [wi_t[O_WI - O_KI + h:O_WI - O_KI + h + 1, :] for h in range(IDX_HEADS)]

    kpos = lax.broadcasted_iota(jnp.int32, (KB, BQ), 0)
    key_limit = ((i * BQ + lax.broadcasted_iota(jnp.int32, (1, BQ), 1)) // CHUNK + 1) * CHUNK

    def admissible(kb):
        return kb * KB + kpos < key_limit

    def score_block(kb, carry):
        kblk = kd[pl.ds(pl.multiple_of(kb * KB, KB), KB), :]
        dots = lax.dot_general(kblk, qi_all[...], _NT, preferred_element_type=f32)
        score = jnp.zeros((KB, BQ), f32)
        for h in range(IDX_HEADS):
            score = score + jnp.maximum(dots[:, h * BQ:(h + 1) * BQ], 0.0) * wi_rows[h]
        sk[kb] = _sort_key(jnp.where(admissible(kb), score, NEG))
        return carry

    lax.fori_loop(0, nkb, score_block, 0)

    neg_key = _sort_key(jnp.full((1, 1), NEG, f32))
    n_rest = ((NKB - nkb) * KB).astype(f32)
    one, zero = jnp.ones((), jnp.bfloat16), jnp.zeros((), jnp.bfloat16)

    def byte_of(key, byte):
        return ((key >> 24) + 128) if byte == 3 else ((key >> (8 * byte)) & 255)

    target = jnp.full((1, BQ), TOPK, f32)
    prefix = jnp.zeros((1, BQ), jnp.int32)
    above = jnp.zeros((1, BQ), f32)
    for byte in (3, 2, 1, 0):
        def in_class(key, byte=byte, prefix=prefix):
            return (key >> (8 * byte + 8)) == prefix

        def prepare(kb, carry, byte=byte, in_class=in_class):
            key = sk[kb]
            digit = byte_of(key, byte).astype(f32)
            if byte < 3:
                digit = jnp.where(in_class(key), digit, -1.0)
            dg[kb] = digit.astype(dg.dtype)
            return carry

        lax.fori_loop(0, nkb, prepare, 0)
        rest_digit = byte_of(neg_key, byte)
        rest_on = in_class(neg_key) if byte < 3 else (neg_key == neg_key)

        def count_ge(cand, rest_digit=rest_digit, rest_on=rest_on):
            cand_b = cand.astype(f32).astype(jnp.bfloat16)

            def body(kb, acc):
                hit = jnp.where(dg[kb] >= cand_b, one, zero)
                return acc + _rows_tree(hit, jnp.add, BF16_ROWS).astype(f32)

            acc = lax.fori_loop(0, nkb, body, jnp.zeros((BF16_ROWS, BQ), f32))
            rest = jnp.where(rest_on & (rest_digit >= cand), n_rest, 0.0)
            return jnp.sum(acc, axis=0, keepdims=True) + rest

        digit_thr, above_here = _bit_search(count_ge, target, jnp.zeros((1, BQ), jnp.int32),
                                            jnp.zeros((1, BQ), f32), 7)
        prefix = (digit_thr - 128) if byte == 3 else (prefix * 256 + digit_thr)
        above = above + above_here
        target = target - above_here
    thr = prefix
    need = TOPK - above

    acc_s[...] = jnp.zeros(acc_s.shape, f32)

    far_bias = bias_ref[B_NEAR, 0:1, :]

    def attend(far, kb, carry):
        eq_seen, m_old, l_old = carry
        keys = sk[kb]
        eq = keys == thr
        eq_f = jnp.where(eq, 1.0, 0.0)
        before = jnp.dot(tri_ref[...], eq_f.astype(jnp.bfloat16), preferred_element_type=f32)
        take_eq = jnp.where(eq, jnp.where(eq_seen + before < need, 1.0, 0.0), 0.0)
        take = jnp.where(keys > thr, 1.0, take_eq)
        mask_add = jnp.where(admissible(kb), jnp.where(take > 0.0, 0.0, NEG), NEG)
        cblk = ckvn[pl.ds(pl.multiple_of(kb * KB, KB), KB), :]
        dots = lax.dot_general(cblk, ql[...], _NT, preferred_element_type=f32)
        tile = i - (KB // BQ) * kb
        m_parts, l_parts, a_parts = [], [], []
        for g in range(HQ // LANES):
            cols = slice(g * LANES, (g + 1) * LANES)
            qcols = slice((g % (BQ // LANES)) * LANES, (g % (BQ // LANES) + 1) * LANES)
            mo = m_old[:, cols]
            if far:
                s = dots[:, cols] + mask_add[:, qcols]
                shift = far_bias[:, cols]
                mn = jnp.maximum(mo, jnp.max(_rows8(s, jnp.maximum), axis=0, keepdims=True) + shift)
                e = jnp.exp2(s - (mn - shift))
            else:
                s = dots[:, cols] + bias_ref[tile, :, cols] + mask_add[:, qcols]
                mn = jnp.maximum(mo, jnp.max(_rows8(s, jnp.maximum), axis=0, keepdims=True))
                e = jnp.exp2(s - mn)
            a = jnp.exp2(mo - mn)
            l_parts.append(a * l_old[:, cols] + jnp.sum(_rows8(e, jnp.add), axis=0, keepdims=True))
            m_parts.append(mn)
            a_parts.append(a)
            p_s[:, cols] = e.astype(p_s.dtype)
        alpha = jnp.concatenate(a_parts, axis=1)
        acc_s[...] = acc_s[...] * alpha + lax.dot_general(cblk, p_s[...], _TN, preferred_element_type=f32)
        eq_seen = eq_seen + jnp.sum(_rows8(eq_f, jnp.add), axis=0, keepdims=True)
        return eq_seen, jnp.concatenate(m_parts, axis=1), jnp.concatenate(l_parts, axis=1)

    init = (jnp.zeros((1, BQ), f32), jnp.full((1, HQ), 4 * NEG, f32), jnp.zeros((1, HQ), f32))
    assert BQ == KB
    n_far = jnp.maximum(nkb - B_NEAR, 0)
    carry = lax.fori_loop(0, n_far, functools.partial(attend, True), init)
    _, _, l_fin = lax.fori_loop(n_far, nkb, functools.partial(attend, False), carry)

    o_lat_t = acc_s[...] / l_fin
    for p in range(B_HEADS // 2):
        out = jnp.zeros((BQ, LANES), f32)
        for h in (2 * p, 2 * p + 1):
            o_lat = o_lat_t[:, h * BQ:(h + 1) * BQ].T.astype(jnp.bfloat16)
            out = out + jnp.dot(o_lat, wuv_ref[h], preferred_element_type=f32)
        o_ref[:, p * LANES:(p + 1) * LANES] = out.astype(o_ref.dtype)


def _attn_b(proj3, gain, wuk, wuv, bias, tri, nbatch=BATCH):
    nq = SEQ // BQ
    xw = IN_A - X_COL0
    return pl.pallas_call(
        _attn_b_kernel,
        grid=(nbatch, nq),
        in_specs=[
            pl.BlockSpec((None, BQ, B_W), lambda b, i: (b, i, O_QB // B_W)),
            pl.BlockSpec((None, BQ, xw), lambda b, i: (b, i, X_COL0 // xw)),
            pl.BlockSpec((None, SEQ, LANES), lambda b, i: (b, 0, O_KI // LANES)),
            pl.BlockSpec((None, SEQ, LANES), lambda b, i: (b, 0, O_CKV // LANES)),
            pl.BlockSpec((1, B_LATENT), lambda b, i: (0, 0)),
            pl.BlockSpec((B_HEADS, LANES, B_LATENT), lambda b, i: (0, 0, 0)),
            pl.BlockSpec((B_HEADS, B_LATENT, LANES), lambda b, i: (0, 0, 0)),
            pl.BlockSpec((B_NEAR + 1, KB, HQ), lambda b, i: (0, 0, 0)),
            pl.BlockSpec((KB, KB), lambda b, i: (0, 0)),
        ],
        out_specs=pl.BlockSpec((None, BQ, B_W), lambda b, i: (b, i, 0)),
        out_shape=jax.ShapeDtypeStruct((nbatch, SEQ, B_W), jnp.bfloat16),
        scratch_shapes=[
            pltpu.VMEM((SEQ, B_LATENT), jnp.bfloat16),
            pltpu.VMEM((SEQ, LANES), jnp.bfloat16),
            pltpu.VMEM((HQ, B_LATENT), jnp.bfloat16),
            pltpu.VMEM((HQ, LANES), jnp.bfloat16),
            pltpu.VMEM((NKB, KB, BQ), jnp.int32),
            pltpu.VMEM((NKB, KB, BQ), jnp.bfloat16),
            pltpu.VMEM((B_LATENT, HQ), jnp.float32),
            pltpu.VMEM((KB, HQ), jnp.bfloat16),
        ],
        compiler_params=_cparams(("arbitrary", "arbitrary")),
        name="attn_b",
    )(proj3, proj3, proj3, proj3, gain, wuk, wuv, bias, tri)


MERGE_BN = 1024
MERGE_BM = 1024


def _merge_kernel(oa_ref, ob_ref, oc_ref, ga_ref, gb_ref, gc_ref, wa_ref, wb_ref, wc_ref, o_ref, wa, wb, wc):
    f32 = jnp.float32

    @pl.when(pl.program_id(1) == 0)
    def _():
        wa[...] = wa_ref[...].astype(wa.dtype)
        wb[...] = wb_ref[...].astype(wb.dtype)
        wc[...] = wc_ref[...].astype(wc.dtype)

    m = ga_ref[...].astype(f32) * jnp.dot(oa_ref[...], wa[...], preferred_element_type=f32)
    m = m + gb_ref[...].astype(f32) * jnp.dot(ob_ref[...], wb[...], preferred_element_type=f32)
    m = m + gc_ref[...].astype(f32) * jnp.dot(oc_ref[...], wc[...], preferred_element_type=f32)
    o_ref[...] = m.astype(o_ref.dtype)


def _merge(oa, ob, oc, gates, w_branch, layer):
    t = oa.shape[0]
    bn, bm = MERGE_BN, MERGE_BM
    gstep = D_MODEL // bn
    return pl.pallas_call(
        _merge_kernel,
        grid=(D_MODEL // bn, t // bm),
        in_specs=[
            pl.BlockSpec((bm, A_W), lambda j, i: (i, 0)),
            pl.BlockSpec((bm, B_W), lambda j, i: (i, 0)),
            pl.BlockSpec((bm, C_W), lambda j, i: (i, 0)),
            pl.BlockSpec((bm, bn), lambda j, i: (i, j)),
            pl.BlockSpec((bm, bn), lambda j, i: (i, gstep + j)),
            pl.BlockSpec((bm, bn), lambda j, i: (i, 2 * gstep + j)),
            pl.BlockSpec((None, A_W, bn), lambda j, i: (layer, 0, j)),
            pl.BlockSpec((None, B_W, bn), lambda j, i: (layer, A_W // B_W, j)),
            pl.BlockSpec((None, C_W, bn), lambda j, i: (layer, (A_W + B_W) // C_W, j)),
        ],
        out_specs=pl.BlockSpec((bm, bn), lambda j, i: (i, j)),
        out_shape=jax.ShapeDtypeStruct((t, D_MODEL), jnp.bfloat16),
        scratch_shapes=[pltpu.VMEM((A_W, bn), jnp.bfloat16), pltpu.VMEM((B_W, bn), jnp.bfloat16),
                        pltpu.VMEM((C_W, bn), jnp.bfloat16)],
        compiler_params=_cparams(("arbitrary", "arbitrary")),
        name="merge",
    )(oa, ob, oc, gates, gates, gates, w_branch, w_branch, w_branch)


FINISH_CHUNKS = 4


def _finish_chunked(y_of_rows, nrows, x_ref, gp_ref, gn_ref, xo_ref, ho_ref):
    step = nrows // FINISH_CHUNKS
    for r in range(0, nrows, step):
        rows = slice(r, r + step)
        xn = x_ref[rows, :] + _rms(y_of_rows(rows), gp_ref[...])
        xo_ref[rows, :] = xn
        if ho_ref is not None:
            ho_ref[rows, :] = _rms(xn, gn_ref[...]).astype(ho_ref.dtype)


def _gemm_res_kernel(nk, with_next, a_ref, w_ref, x_ref, gp_ref, gn_ref, xo_ref, *rest):
    ho_ref = rest[0] if with_next else None
    f32 = jnp.float32
    bm = a_ref.shape[0]
    if nk == 1:
        _finish_chunked(lambda rows: jnp.dot(a_ref[rows, :], w_ref[...], preferred_element_type=f32),
                        bm, x_ref, gp_ref, gn_ref, xo_ref, ho_ref)
        return
    acc = rest[-1]
    k = pl.program_id(1)

    @pl.when(k == 0)
    def _():
        acc[...] = jnp.dot(a_ref[...], w_ref[...], preferred_element_type=f32)

    @pl.when((k > 0) & (k < nk - 1))
    def _():
        acc[...] += jnp.dot(a_ref[...], w_ref[...], preferred_element_type=f32)

    @pl.when(k == nk - 1)
    def _():
        _finish_chunked(lambda rows: acc[rows, :] + jnp.dot(a_ref[rows, :], w_ref[...], preferred_element_type=f32),
                        bm, x_ref, gp_ref, gn_ref, xo_ref, ho_ref)


def _gemm_res(a, w, layer, x, g_post, g_next, bk, bm=512, name="gemm_res"):
    t, kdim = a.shape
    n = w.shape[2]
    nk = kdim // bk
    with_next = g_next is not None
    if g_next is None:
        g_next = g_post
    out_shape = [jax.ShapeDtypeStruct((t, n), jnp.float32)]
    out_specs = [pl.BlockSpec((bm, n), lambda i, k: (i, 0))]
    if with_next:
        out_shape.append(jax.ShapeDtypeStruct((t, n), jnp.bfloat16))
        out_specs.append(pl.BlockSpec((bm, n), lambda i, k: (i, 0)))
    res = pl.pallas_call(
        functools.partial(_gemm_res_kernel, nk, with_next),
        grid=(t // bm, nk),
        in_specs=[
            pl.BlockSpec((bm, bk), lambda i, k: (i, k)),
            pl.BlockSpec((None, bk, n), lambda i, k: (layer, k, 0)),
            pl.BlockSpec((bm, n), lambda i, k: (i, 0)),
            pl.BlockSpec((1, n), lambda i, k: (0, 0)),
            pl.BlockSpec((1, n), lambda i, k: (0, 0)),
        ],
        out_specs=out_specs,
        out_shape=out_shape,
        scratch_shapes=[pltpu.VMEM((bm, n), jnp.float32)] if nk > 1 else [],
        compiler_params=_cparams(("parallel", "arbitrary")),
        name=name,
    )(a, w, x, g_post, g_next)
    return (res[0], res[1]) if with_next else (res[0], None)


def _cast_once(pairs):
    @pl.when(pl.program_id(0) == 0)
    def _():
        for src, dst in pairs:
            dst[...] = src[...].astype(dst.dtype)


def _memkv_kernel(m_ref, g_ref, w_ref, o_ref, wb):
    _cast_once([(w_ref, wb)])
    mn = _rms(m_ref[...], g_ref[...]).astype(jnp.bfloat16)
    o_ref[...] = jnp.dot(mn, wb[...], preferred_element_type=jnp.float32).astype(o_ref.dtype)


def _memkv(mem2, g, w, layer):
    t, d = mem2.shape
    n = w.shape[2]
    bm = 512
    return pl.pallas_call(
        _memkv_kernel,
        grid=(t // bm,),
        in_specs=[pl.BlockSpec((bm, d), lambda i: (i, 0)), pl.BlockSpec((1, d), lambda i: (0, 0)),
                  pl.BlockSpec((None, d, n), lambda i: (layer, 0, 0))],
        out_specs=pl.BlockSpec((bm, n), lambda i: (i, 0)),
        out_shape=jax.ShapeDtypeStruct((t, n), jnp.bfloat16),
        scratch_shapes=[pltpu.VMEM((d, n), jnp.bfloat16)],
        compiler_params=_cparams(("arbitrary",)),
        name="mem_kv",
    )(mem2, g, w)


XA_BM = 512


def _xattn_kernel(h_ref, wqf_ref, kv_ref, wof_ref, x_ref, gp_ref, gn_ref, xo_ref, ho_ref, wq_ref, wo_ref):
    f32 = jnp.float32
    _cast_once([(wqf_ref, wq_ref), (wof_ref, wo_ref)])
    q = jnp.dot(h_ref[...], wq_ref[...], preferred_element_type=f32).astype(jnp.bfloat16)
    scale = MEM_HEAD_DIM ** -0.5 * LOG2E
    hcols = [slice(h * LANES, (h + 1) * LANES) for h in range(MEM_HEADS)]
    ss = [lax.dot_general(q[:, c], kv_ref[:, c], _NT, preferred_element_type=f32) * scale for c in hcols]
    ms = [jnp.max(s, axis=-1, keepdims=True) for s in ss]
    es = [jnp.exp2(s - m) for s, m in zip(ss, ms)]
    ls = [jnp.sum(e, axis=-1, keepdims=True) for e in es]
    pv = [jnp.dot(e.astype(jnp.bfloat16), kv_ref[:, MEM_W + h * LANES:MEM_W + (h + 1) * LANES],
                  preferred_element_type=f32) for h, e in enumerate(es)]
    o = jnp.concatenate([(o_h / l).astype(jnp.bfloat16) for o_h, l in zip(pv, ls)], axis=-1)
    _finish_chunked(lambda rows: jnp.dot(o[rows, :], wo_ref[...], preferred_element_type=f32),
                    o.shape[0], x_ref, gp_ref, gn_ref, xo_ref, ho_ref)


def _mix_out_xattn_kernel(m_ref, wo_ref, wq_ref, kv_ref, wmo_ref, x_ref, g1_ref, g2_ref, g3_ref, g4_ref,
                          xo_ref, ho_ref, x1_s, h1_s):
    f32 = jnp.float32
    bm = m_ref.shape[0]
    _finish_chunked(lambda rows: jnp.dot(m_ref[rows, :], wo_ref[...], preferred_element_type=f32),
                    bm, x_ref, g1_ref, g2_ref, x1_s, h1_s)
    q = jnp.dot(h1_s[...], wq_ref[...], preferred_element_type=f32).astype(jnp.bfloat16)
    scale = MEM_HEAD_DIM ** -0.5 * LOG2E
    hcols = [slice(h * LANES, (h + 1) * LANES) for h in range(MEM_HEADS)]
    ss = [lax.dot_general(q[:, c], kv_ref[:, c], _NT, preferred_element_type=f32) * scale for c in hcols]
    ms = [jnp.max(s, axis=-1, keepdims=True) for s in ss]
    es = [jnp.exp2(s - m) for s, m in zip(ss, ms)]
    ls = [jnp.sum(e, axis=-1, keepdims=True) for e in es]
    pv = [jnp.dot(e.astype(jnp.bfloat16), kv_ref[:, MEM_W + h * LANES:MEM_W + (h + 1) * LANES],
                  preferred_element_type=f32) for h, e in enumerate(es)]
    o = jnp.concatenate([(o_h / l).astype(jnp.bfloat16) for o_h, l in zip(pv, ls)], axis=-1)
    _finish_chunked(lambda rows: jnp.dot(o[rows, :], wmo_ref[...], preferred_element_type=f32),
                    bm, x1_s, g3_ref, g4_ref, xo_ref, ho_ref)


def _mix_out_xattn(merged, w_o, w_mq, kv, w_mo, layer, x, g1, g2, g3, g4):
    t, d = merged.shape
    bm = XA_BM
    per_batch = SEQ // bm
    once = pl.Buffered(1)
    row = pl.BlockSpec((bm, d), lambda i: (i, 0))
    gain = pl.BlockSpec((1, d), lambda i: (0, 0))
    return pl.pallas_call(
        _mix_out_xattn_kernel,
        grid=(t // bm,),
        in_specs=[
            row,
            pl.BlockSpec((None, d, d), lambda i: (layer, 0, 0), pipeline_mode=once),
            pl.BlockSpec((None, d, MEM_W), lambda i: (layer, 0, 0), pipeline_mode=once),
            pl.BlockSpec((MEM_LEN, 2 * MEM_W), lambda i: (i // per_batch, 0)),
            pl.BlockSpec((None, MEM_W, d), lambda i: (layer, 0, 0), pipeline_mode=once),
            row, gain, gain, gain, gain,
        ],
        out_specs=[row, row],
        out_shape=[jax.ShapeDtypeStruct((t, d), jnp.float32), jax.ShapeDtypeStruct((t, d), jnp.bfloat16)],
        scratch_shapes=[pltpu.VMEM((bm, d), jnp.float32), pltpu.VMEM((bm, d), jnp.bfloat16)],
        compiler_params=_cparams(("parallel",)),
        name="mix_out_xattn",
    )(merged, w_o, w_mq, kv, w_mo, x, g1, g2, g3, g4)


def _xattn(h, wq, kv, wo, layer, x, g_post, g_next):
    t, d = h.shape
    bm = XA_BM
    per_batch = SEQ // bm
    return pl.pallas_call(
        _xattn_kernel,
        grid=(t // bm,),
        in_specs=[
            pl.BlockSpec((bm, d), lambda i: (i, 0)),
            pl.BlockSpec((None, d, MEM_W), lambda i: (layer, 0, 0)),
            pl.BlockSpec((MEM_LEN, 2 * MEM_W), lambda i: (i // per_batch, 0)),
            pl.BlockSpec((None, MEM_W, d), lambda i: (layer, 0, 0)),
            pl.BlockSpec((bm, d), lambda i: (i, 0)),
            pl.BlockSpec((1, d), lambda i: (0, 0)),
            pl.BlockSpec((1, d), lambda i: (0, 0)),
        ],
        out_specs=[pl.BlockSpec((bm, d), lambda i: (i, 0)), pl.BlockSpec((bm, d), lambda i: (i, 0))],
        out_shape=[jax.ShapeDtypeStruct((t, d), jnp.float32), jax.ShapeDtypeStruct((t, d), jnp.bfloat16)],
        scratch_shapes=[pltpu.VMEM((d, MEM_W), jnp.bfloat16), pltpu.VMEM((MEM_W, d), jnp.bfloat16)],
        compiler_params=_cparams(("arbitrary",)),
        name="mem_xattn",
    )(h, wq, kv, wo, x, g_post, g_next)


def _ffn_up_kernel(h_ref, wgf_ref, wvf_ref, cwg_ref, cwv_ref, cbg_ref, cbv_ref, o_ref, wg_ref, wv_ref, ug, uv):
    j = pl.program_id(0)
    i = pl.program_id(1)
    tiles_per_seq = SEQ // FF_BM
    last = FF_P // FF_BN - 1
    valid = D_FF - last * FF_BN
    shift = FF_BN - valid

    @pl.when((i == 0) & (j < last))
    def _():
        wg_ref[...] = wgf_ref[...].astype(wg_ref.dtype)
        wv_ref[...] = wvf_ref[...].astype(wv_ref.dtype)

    @pl.when((i == 0) & (j == last))
    def _():
        zeros = jnp.zeros((wg_ref.shape[0], FF_BN - valid), wg_ref.dtype)
        wg_ref[:, :valid] = wgf_ref[:, :valid].astype(wg_ref.dtype)
        wv_ref[:, :valid] = wvf_ref[:, shift:].astype(wv_ref.dtype)
        wg_ref[:, valid:] = zeros
        wv_ref[:, valid:] = zeros

    for u in (ug, uv):
        @pl.when(i % tiles_per_seq == 0)
        def _():
            u[0:SUBLANES, :] = jnp.zeros((SUBLANES, FF_BN), jnp.float32)

        @pl.when(i % tiles_per_seq != 0)
        def _():
            u[0:SUBLANES, :] = u[FF_BM:FF_BM + SUBLANES, :]

    def conv(u, cw_ref, cb_ref, r0):
        base = SUBLANES + r0
        acc = cb_ref[...] + u[base - 2:base - 2 + FF_CH, :] * cw_ref[0:1, :]
        acc = acc + u[base - 1:base - 1 + FF_CH, :] * cw_ref[1:2, :]
        return acc + u[base:base + FF_CH, :] * cw_ref[2:3, :]

    for c in range(FF_BM // FF_CH):
        r0 = c * FF_CH
        hb = h_ref[r0:r0 + FF_CH, :]
        ug[SUBLANES + r0:SUBLANES + r0 + FF_CH, :] = jnp.dot(hb, wg_ref[...], preferred_element_type=jnp.float32)
        uv[SUBLANES + r0:SUBLANES + r0 + FF_CH, :] = jnp.dot(hb, wv_ref[...], preferred_element_type=jnp.float32)
        gate = conv(ug, cwg_ref, cbg_ref, r0)
        val = conv(uv, cwv_ref, cbv_ref, r0)
        o_ref[r0:r0 + FF_CH, :] = (jax.nn.gelu(gate) * val).astype(o_ref.dtype)


def _ffn_up(h, w_up, layer, conv_w, conv_b):
    t, d = h.shape
    nj = FF_P // FF_BN
    w_block = (None, pl.Element(d), pl.Element(FF_BN))
    return pl.pallas_call(
        _ffn_up_kernel,
        grid=(nj, t // FF_BM),
        in_specs=[
            pl.BlockSpec((FF_BM, d), lambda j, i: (i, 0)),
            pl.BlockSpec(w_block, lambda j, i: (layer, 0, pl.multiple_of(j * FF_BN, LANES))),
            pl.BlockSpec(w_block, lambda j, i: (
                layer, 0, pl.multiple_of(jnp.minimum(D_FF + j * FF_BN, 2 * D_FF - FF_BN), LANES))),
            pl.BlockSpec((CONV_W, FF_BN), lambda j, i: (0, j)),
            pl.BlockSpec((CONV_W, FF_BN), lambda j, i: (0, nj + j)),
            pl.BlockSpec((1, FF_BN), lambda j, i: (0, j)),
            pl.BlockSpec((1, FF_BN), lambda j, i: (0, nj + j)),
        ],
        out_specs=pl.BlockSpec((FF_BM, FF_BN), lambda j, i: (i, j)),
        out_shape=jax.ShapeDtypeStruct((t, FF_P), jnp.bfloat16),
        scratch_shapes=[pltpu.VMEM((d, FF_BN), jnp.bfloat16), pltpu.VMEM((d, FF_BN), jnp.bfloat16),
                        pltpu.VMEM((FF_BM + SUBLANES, FF_BN), jnp.float32),
                        pltpu.VMEM((FF_BM + SUBLANES, FF_BN), jnp.float32)],
        compiler_params=_cparams(("arbitrary", "arbitrary")),
        name="ffn_up",
    )(h, w_up, w_up, conv_w, conv_w, conv_b, conv_b)


def _pad_heads(w, axis):
    h = w.shape[0]
    zero = jnp.zeros_like(w)
    even = jnp.concatenate([w, zero], axis=axis)
    odd = jnp.concatenate([zero, w], axis=axis)
    sel = (jnp.arange(h) % 2 == 0).reshape((h, 1, 1))
    return jnp.where(sel, even, odd)


def _toeplitz(fn, rows, cols):
    ks = np.concatenate([np.arange(0, cols), np.arange(-(rows - 1), 0)])
    w = fn(ks)
    h, period = w.shape
    x = jnp.tile(w, (1, rows))[:, :rows * (period - 1)].reshape(h, rows, period - 1)
    return x[:, :, :cols].astype(jnp.float32)


def _band(rows, cols, left):
    diff = left + np.arange(rows)[:, None] // CHUNK - np.arange(cols)[None, :] // CHUNK
    return (diff >= 0) & (diff <= left)


def _bias_a(rel_bias):
    fn = lambda ks: rel_bias[np.clip(A_PAD - ks, -A_MAX_REL, A_MAX_REL) + A_MAX_REL].T
    bias = jnp.where(_band(QB, A_WIN, A_LEFT_CHUNKS)[None], _toeplitz(fn, QB, A_WIN) * LOG2E, NEG)
    return bias.reshape(A_HEADS * QB, A_WIN)


def _bias_c(t5_c):
    fn = lambda ks: t5_c[_t5_bucket(jnp.asarray(ks - C_PAD, jnp.int32))].T
    bias = jnp.where(_band(QB, C_WIN, C_LEFT_CHUNKS)[None], _toeplitz(fn, QB, C_WIN) * LOG2E, NEG)
    return bias.reshape(C_Q_HEADS * QB, C_WIN)


def _bias_b(t5_b):
    tiles = []
    for n in range(B_NEAR + 1):
        off = BQ * n if n < B_NEAR else SEQ
        fn = lambda ks, off=off: t5_b[_t5_bucket(jnp.asarray(-ks - off, jnp.int32))].T
        tile = _toeplitz(fn, KB, BQ)
        tiles.append(jnp.transpose(tile * LOG2E, (1, 0, 2)).reshape(KB, HQ))
    return jnp.stack(tiles)


def _pad_ff(a, dtype):
    z = jnp.zeros((a.shape[0], FF_P - D_FF), dtype)
    return jnp.concatenate([a[:, :D_FF].astype(dtype), z, a[:, D_FF:].astype(dtype), z], axis=1)


def kernel(x, mem, t5_table, norm_gains, w_in, a_rel_bias, ckv_gain, w_uk, w_uv, sinks, w_branch, w_o,
           mem_gain, w_mq, w_mkv, w_mo, w_up, conv_w, conv_b, w_down):
    bf16 = jnp.bfloat16
    xs = x.reshape(TOKENS, D_MODEL)
    mem2 = mem.reshape(BATCH * MEM_LEN, D_MODEL)
    tri = jnp.asarray(np.tril(np.ones((KB, KB), np.float32), -1), bf16)
    bias_b = _bias_b(t5_table[:, :B_HEADS])
    bias_c = _bias_c(t5_table[:, B_HEADS:])
    gains = norm_gains.reshape(DEPTH, 6, 1, D_MODEL)
    w_o_b, w_mq_b, w_mo_b = (w.astype(bf16) for w in (w_o, w_mq, w_mo))
    w_dn_b = jnp.concatenate([w_down.astype(bf16), jnp.zeros((DEPTH, FF_P - D_FF, D_MODEL), bf16)], axis=1)
    w_in_t = jnp.swapaxes(w_in, 1, 2)

    h = _norm(xs, gains[0, 0])
    for l in range(DEPTH):
        g = gains[l]
        proja = _proj(h, w_in_t, l, 0, IN_A, 1024, gate=False, name="in_proj_a")
        projc = _proj(h, w_in_t, l, O_QC, C_COLS, C_COLS, gate=False, name="in_proj_c")
        gates = _proj(h, w_in_t, l, O_GL, 3 * D_MODEL, 1024, gate=True, name="in_proj_g")
        proja3 = proja.reshape(BATCH, SEQ, IN_A)
        oa = _attn_a(proja3, _bias_a(a_rel_bias[l]))
        wuk = jnp.transpose(_pad_heads(w_uk[l], axis=2), (0, 2, 1)).astype(bf16)
        wuv = _pad_heads(w_uv[l], axis=2).astype(bf16)
        ob = _attn_b(proja3, ckv_gain[l].reshape(1, B_LATENT), wuk, wuv, bias_b, tri)
        oc = _attn_c(sinks[l], projc.reshape(BATCH, SEQ, C_COLS), bias_c)
        merged = _merge(oa.reshape(TOKENS, A_W), ob.reshape(TOKENS, B_W), oc.reshape(TOKENS, C_W), gates, w_branch, l)
        kv = _memkv(mem2, mem_gain[l].reshape(1, D_MODEL), w_mkv, l)
        xs, h = _mix_out_xattn(merged, w_o_b, w_mq_b, kv, w_mo_b, l, xs, g[1], g[2], g[3], g[4])
        hidden = _ffn_up(h, w_up, l, _pad_ff(conv_w[l], jnp.float32), _pad_ff(conv_b[l].reshape(1, -1), jnp.float32))
        g_next = gains[l + 1, 0] if l + 1 < DEPTH else None
        xs, h = _gemm_res(hidden, w_dn_b, l, xs, g[5], g_next, bk=DOWN_BK, name="ffn_down")
    return xs.reshape(BATCH, SEQ, D_MODEL)
```

```python
import functools
import math

import numpy as np
import jax
import jax.numpy as jnp
from jax import lax
from jax.experimental import pallas as pl
from jax.experimental.pallas import tpu as pltpu

D_MODEL = 2048
BATCH = 4
SEQ = 2048
DEPTH = 2
TOKENS = BATCH * SEQ
CHUNK = 64
EPS = 1e-6
NEG = -1e30
LOG2E = math.log2(math.e)
A_HEADS = 8
A_LEFT_CHUNKS = 8
A_MAX_REL = 128
A_W = 512
B_HEADS = 8
B_W = 512
B_LATENT = 128
IDX_HEADS = 8
IDX_DIM = 64
TOPK = 256
C_Q_HEADS = 16
C_GROUP = 8
C_W = 1024
C_LEFT_CHUNKS = 2
T5_BUCKETS = 32
T5_MAX_DIST = 256
MEM_LEN = 256
MEM_HEADS = 4
MEM_HEAD_DIM = 128
MEM_W = 512
D_FF = 5504
CONV_W = 3

LANES = 128
SUBLANES = 8
HEAD_DIM = 64
QB = 128
KB = 256
VMEM_LIMIT = 56 * 1024 * 1024

O_QA, O_KA, O_VA, O_QB = 0, 512, 1024, 1536
O_CKV = 2048
O_QI = 2176
O_KI = 2688
O_WI = 2752
O_QC = 2760
O_GL = 4040
IN_W = O_GL + 3 * D_MODEL
IN_A = 3072
X_COL0 = 2048
C_COLS = O_GL - O_QC
PROJ_BM = 1024
PROJ_TR = 256

FF_P = 5632
FF_BN = 512
FF_BM = 1024
FF_CH = 256
DOWN_BK = 1408

A_WIN = (A_LEFT_CHUNKS + 2) * CHUNK
C_WIN = (C_LEFT_CHUNKS + 2) * CHUNK
A_PAD = A_LEFT_CHUNKS * CHUNK
C_PAD = C_LEFT_CHUNKS * CHUNK
BQ = 512
B_QK = BQ // KB
B_NEAR = B_QK + 1
NKB = SEQ // KB
HQ = B_HEADS * BQ

_NT = (((1,), (1,)), ((), ()))
_TN = (((0,), (0,)), ((), ()))


def _cparams(sem):
    return pltpu.CompilerParams(dimension_semantics=sem, vmem_limit_bytes=VMEM_LIMIT)


def _t5_bucket(rel):
    half = T5_BUCKETS // 2
    max_exact = half // 2
    sign = jnp.where(rel > 0, half, 0)
    d = jnp.abs(rel)
    d_f = jnp.maximum(d, 1).astype(jnp.float32)
    large = max_exact + (jnp.log(d_f / max_exact) / math.log(T5_MAX_DIST / max_exact) * (half - max_exact)).astype(jnp.int32)
    large = jnp.minimum(large, half - 1)
    return sign + jnp.where(d < max_exact, d, large)


def _far_bucket_is_constant():
    assert BQ % KB == 0
    d = np.arange(KB + 1, SEQ, dtype=np.float32)
    assert d[0] > T5_MAX_DIST
    large = 8 + (np.log(d / 8) / math.log(T5_MAX_DIST / 8) * 8).astype(np.int32)
    return bool(np.all(np.minimum(large, 15) == 15))


assert _far_bucket_is_constant()


def _rms(v, g):
    return v * lax.rsqrt(jnp.mean(v * v, axis=-1, keepdims=True) + EPS) * g


def _norm_kernel(x_ref, g_ref, o_ref):
    o_ref[...] = _rms(x_ref[...], g_ref[...]).astype(o_ref.dtype)


def _norm(x, g, bm=1024):
    t, d = x.shape
    return pl.pallas_call(
        _norm_kernel,
        grid=(t // bm,),
        in_specs=[pl.BlockSpec((bm, d), lambda i: (i, 0)), pl.BlockSpec((1, d), lambda i: (0, 0))],
        out_specs=pl.BlockSpec((bm, d), lambda i: (i, 0)),
        out_shape=jax.ShapeDtypeStruct((t, d), jnp.bfloat16),
        compiler_params=_cparams(("parallel",)),
        name="rmsnorm",
    )(x, g)


def _proj_kernel(gate, h_ref, wt_ref, o_ref, wb):
    @pl.when(pl.program_id(1) == 0)
    def _():
        for c in range(wt_ref.shape[0] // PROJ_TR):
            rows = slice(c * PROJ_TR, (c + 1) * PROJ_TR)
            wb[:, rows] = wt_ref[rows, :].T.astype(wb.dtype)

    acc = jnp.dot(h_ref[...], wb[...], preferred_element_type=jnp.float32)
    o_ref[...] = (jax.nn.sigmoid(acc) if gate else acc).astype(o_ref.dtype)


def _proj(h, w_t, layer, col0, n, bn, gate, name):
    t, k = h.shape
    assert col0 % SUBLANES == 0 and bn % SUBLANES == 0
    w_spec = pl.BlockSpec((None, pl.Element(bn), pl.Element(k)),
                          lambda j, i: (layer, pl.multiple_of(col0 + j * bn, SUBLANES), 0))
    return pl.pallas_call(
        functools.partial(_proj_kernel, gate),
        grid=(n // bn, t // PROJ_BM),
        in_specs=[pl.BlockSpec((PROJ_BM, k), lambda j, i: (i, 0)), w_spec],
        out_specs=pl.BlockSpec((PROJ_BM, bn), lambda j, i: (i, j)),
        out_shape=jax.ShapeDtypeStruct((t, n), jnp.bfloat16),
        scratch_shapes=[pltpu.VMEM((k, bn), jnp.bfloat16)],
        compiler_params=_cparams(("arbitrary", "arbitrary")),
        name=name,
    )(h, w_t)


def _lane_masks(dtype):
    lane = lax.broadcasted_iota(jnp.int32, (1, LANES), 1)
    lo = (lane < HEAD_DIM).astype(jnp.float32)
    return lo.astype(dtype), (1.0 - lo).astype(dtype)


def _swap_halves(x):
    return pltpu.roll(x.astype(jnp.float32), HEAD_DIM, 1).astype(x.dtype)


def _fill_padded(src, dst_ref, pad):
    dst_ref[0:pad, :] = jnp.zeros((pad, dst_ref.shape[1]), dst_ref.dtype)
    dst_ref[pad:pad + SEQ, :] = src


def _attn_a_kernel(q_ref, k_ref, v_ref, bias_ref, o_ref, kpad, vpad):
    i = pl.program_id(1)

    @pl.when(i == 0)
    def _():
        _fill_padded(k_ref[...], kpad, A_PAD)
        _fill_padded(v_ref[...], vpad, A_PAD)

    start = pl.multiple_of(i * QB, QB)
    kw = kpad[pl.ds(start, A_WIN), :]
    vw = vpad[pl.ds(start, A_WIN), :]
    mlo, mhi = _lane_masks(jnp.bfloat16)
    lane = lax.broadcasted_iota(jnp.int32, (QB, LANES), 1)
    kchunk = lax.broadcasted_iota(jnp.int32, (QB, A_WIN), 1) // CHUNK
    pad_mask = jnp.where(kchunk >= A_LEFT_CHUNKS - 2 * i, 0.0, NEG)
    scale = HEAD_DIM ** -0.5 * LOG2E
    pad2 = jnp.concatenate([pad_mask, pad_mask], axis=0)
    npair = A_HEADS // 2
    pcols = [slice(p * LANES, (p + 1) * LANES) for p in range(npair)]
    ss = []
    for p in range(npair):
        qp = q_ref[:, pcols[p]]
        q2 = jnp.concatenate([qp * mlo, qp * mhi], axis=0)
        s = lax.dot_general(q2, kw[:, pcols[p]], _NT, preferred_element_type=jnp.float32)
        ss.append(s * scale + bias_ref[2 * p * QB:(2 * p + 2) * QB, :] + pad2)
    ms = [jnp.max(s, axis=-1, keepdims=True) for s in ss]
    es = [jnp.exp2(s - m) for s, m in zip(ss, ms)]
    ls = [jnp.sum(e, axis=-1, keepdims=True) for e in es]
    outs = [jnp.dot(e.astype(jnp.bfloat16), vw[:, pcols[p]], preferred_element_type=jnp.float32)
            for p, e in enumerate(es)]
    for p in range(npair):
        o = outs[p] / ls[p]
        o_ref[:, pcols[p]] = jnp.where(lane < HEAD_DIM, o[:QB], o[QB:]).astype(o_ref.dtype)


def _attn_a(proj3, bias):
    nq = SEQ // QB
    return pl.pallas_call(
        _attn_a_kernel,
        grid=(BATCH, nq),
        in_specs=[
            pl.BlockSpec((None, QB, A_W), lambda b, i: (b, i, O_QA // A_W)),
            pl.BlockSpec((None, SEQ, A_W), lambda b, i: (b, 0, O_KA // A_W)),
            pl.BlockSpec((None, SEQ, A_W), lambda b, i: (b, 0, O_VA // A_W)),
            pl.BlockSpec((A_HEADS * QB, A_WIN), lambda b, i: (0, 0)),
        ],
        out_specs=pl.BlockSpec((None, QB, A_W), lambda b, i: (b, i, 0)),
        out_shape=jax.ShapeDtypeStruct((BATCH, SEQ, A_W), jnp.bfloat16),
        scratch_shapes=[pltpu.VMEM((SEQ + A_PAD, A_W), jnp.bfloat16), pltpu.VMEM((SEQ + A_PAD, A_W), jnp.bfloat16)],
        compiler_params=_cparams(("arbitrary", "arbitrary")),
        name="attn_a",
    )(proj3, proj3, proj3, bias)


def _attn_c_kernel(sink_ref, q_ref, k_ref, v_ref, bias_ref, o_ref, kpad, kswp, vpad, vswp):
    i = pl.program_id(1)

    @pl.when(i == 0)
    def _():
        k = k_ref[...]
        v = v_ref[...]
        _fill_padded(k, kpad, C_PAD)
        _fill_padded(_swap_halves(k), kswp, C_PAD)
        _fill_padded(v, vpad, C_PAD)
        _fill_padded(_swap_halves(v), vswp, C_PAD)

    start = pl.multiple_of(i * QB, QB)
    mlo, mhi = _lane_masks(jnp.bfloat16)
    lane = lax.broadcasted_iota(jnp.int32, (QB, LANES), 1)
    kchunk = lax.broadcasted_iota(jnp.int32, (QB, C_WIN), 1) // CHUNK
    pad_mask = jnp.where(kchunk >= C_LEFT_CHUNKS - 2 * i, 0.0, NEG)
    scale = HEAD_DIM ** -0.5 * LOG2E
    qs = [q_ref[:, p * LANES:(p + 1) * LANES] for p in range(C_GROUP)]
    npair = C_GROUP // 2
    stacks = []
    for straight in (True, False):
        kref, vref = (kpad, vpad) if straight else (kswp, vswp)
        kw = kref[pl.ds(start, C_WIN), :]
        vw = vref[pl.ds(start, C_WIN), :]
        halves = [int((p >= npair) == straight) for p in range(C_GROUP)]
        qg = jnp.concatenate([qs[p] * (mhi if halves[p] else mlo) for p in range(C_GROUP)], axis=0)
        s_all = lax.dot_general(qg, kw, _NT, preferred_element_type=jnp.float32)
        ps, ls = [], []
        for p in range(C_GROUP):
            h = 2 * p + halves[p]
            s = s_all[p * QB:(p + 1) * QB] * scale + bias_ref[h * QB:(h + 1) * QB, :] + pad_mask
            sink = sink_ref[h] * LOG2E
            m = jnp.maximum(jnp.max(s, axis=-1, keepdims=True), sink)
            e = jnp.exp2(s - m)
            ls.append(jnp.sum(e, axis=-1, keepdims=True) + jnp.exp2(sink - m))
            ps.append(e.astype(jnp.bfloat16))
        o_all = jnp.dot(jnp.concatenate(ps, axis=0), vw, preferred_element_type=jnp.float32)
        stacks.append(([o_all[p * QB:(p + 1) * QB] / ls[p] for p in range(C_GROUP)], halves))
    for p in range(C_GROUP):
        (o1, h1), (o2, _) = stacks
        lo, hi = (o2[p], o1[p]) if h1[p] else (o1[p], o2[p])
        o_ref[:, p * LANES:(p + 1) * LANES] = jnp.where(lane < HEAD_DIM, lo, hi).astype(o_ref.dtype)


def _attn_c(sinks, projc3, bias):
    nq = SEQ // QB
    pad_buf = pltpu.VMEM((SEQ + C_PAD, LANES), jnp.bfloat16)
    return pl.pallas_call(
        _attn_c_kernel,
        grid=(BATCH, nq),
        in_specs=[
            pl.BlockSpec(memory_space=pltpu.SMEM),
            pl.BlockSpec((None, QB, C_W), lambda b, i: (b, i, 0)),
            pl.BlockSpec((None, SEQ, LANES), lambda b, i: (b, 0, C_W // LANES)),
            pl.BlockSpec((None, SEQ, LANES), lambda b, i: (b, 0, C_W // LANES + 1)),
            pl.BlockSpec((C_Q_HEADS * QB, C_WIN), lambda b, i: (0, 0)),
        ],
        out_specs=pl.BlockSpec((None, QB, C_W), lambda b, i: (b, i, 0)),
        out_shape=jax.ShapeDtypeStruct((BATCH, SEQ, C_W), jnp.bfloat16),
        scratch_shapes=[pad_buf, pad_buf, pad_buf, pad_buf],
        compiler_params=_cparams(("arbitrary", "arbitrary")),
        name="attn_c",
    )(sinks, projc3, projc3, projc3, bias)


def _sort_key(x):
    bits = lax.bitcast_convert_type(x + 0.0, jnp.int32)
    return bits ^ ((bits >> 31) & jnp.int32(0x7FFFFFFF))


def _rows_tree(x, op, slab):
    parts = [x[r:r + slab, :] for r in range(0, x.shape[0], slab)]
    while len(parts) > 1:
        nxt = [op(parts[k], parts[k + 1]) for k in range(0, len(parts) - 1, 2)]
        parts = nxt + ([parts[-1]] if len(parts) % 2 else [])
    return parts[0]


def _rows8(x, op):
    return _rows_tree(x, op, SUBLANES)


BF16_ROWS = 16


def _bit_search(count_ge, target, v, above, top_bit):
    def step(t, carry):
        v, above = carry
        cand = v | (jnp.int32(1) << (top_bit - t))
        c = count_ge(cand)
        keep = c >= target
        return jnp.where(keep, cand, v), jnp.where(keep, above, c)

    return lax.fori_loop(0, top_bit + 1, step, (v, above))


def _attn_b_kernel(qb_ref, x_ref, kiw_ref, ckv_ref, gain_ref, wuk_ref, wuv_ref, bias_ref, tri_ref,
                   o_ref, ckvn, kd, ql, qi_all, sk, dg, acc_s, p_s):
    i = pl.program_id(1)
    nkb = (i + 1) * B_QK
    mlo, mhi = _lane_masks(jnp.bfloat16)
    f32 = jnp.float32

    @pl.when(i == 0)
    def _():
        ckvn[...] = _rms(ckv_ref[...].astype(f32), gain_ref[...]).astype(ckvn.dtype)
        kiw = kiw_ref[...].astype(f32)
        lane = lax.broadcasted_iota(jnp.int32, kiw.shape, 1)
        kd[...] = jnp.where(lane < HEAD_DIM, kiw, pltpu.roll(kiw, HEAD_DIM, 1)).astype(kd.dtype)

    for h in range(B_HEADS):
        rows = slice(h * BQ, (h + 1) * BQ)
        qlat = jnp.dot(qb_ref[:, (h // 2) * LANES:(h // 2 + 1) * LANES], wuk_ref[h], preferred_element_type=f32)
        ql[rows, :] = (qlat * (HEAD_DIM ** -0.5 * LOG2E)).astype(ql.dtype)
        qcol = O_QI - X_COL0 + (h // 2) * LANES
        qi_all[rows, :] = x_ref[:, qcol:qcol + LANES] * (mhi if h % 2 else mlo)
    wcol = O_KI - X_COL0
    wi_t = x_ref[:, wcol:wcol + LANES].astype(f32).T * (IDX_HEADS ** -0.5 * IDX_DIM ** -0.5)
    wi_rows = [wi_t[O_WI - O_KI + h:O_WI - O_KI + h + 1, :] for h in range(IDX_HEADS)]

    kpos = lax.broadcasted_iota(jnp.int32, (KB, BQ), 0)
    key_limit = ((i * BQ + lax.broadcasted_iota(jnp.int32, (1, BQ), 1)) // CHUNK + 1) * CHUNK

    def admissible(kb):
        return kb * KB + kpos < key_limit

    def score_block(kb, carry):
        kblk = kd[pl.ds(pl.multiple_of(kb * KB, KB), KB), :]
        dots = lax.dot_general(kblk, qi_all[...], _NT, preferred_element_type=f32)
        score = jnp.zeros((KB, BQ), f32)
        for h in range(IDX_HEADS):
            score = score + jnp.maximum(dots[:, h * BQ:(h + 1) * BQ], 0.0) * wi_rows[h]
        sk[kb] = _sort_key(jnp.where(admissible(kb), score, NEG))
        return carry

    lax.fori_loop(0, nkb, score_block, 0)

    neg_key = _sort_key(jnp.full((1, 1), NEG, f32))
    n_rest = ((NKB - nkb) * KB).astype(f32)
    one, zero = jnp.ones((), jnp.bfloat16), jnp.zeros((), jnp.bfloat16)

    def byte_of(key, byte):
        return ((key >> 24) + 128) if byte == 3 else ((key >> (8 * byte)) & 255)

    target = jnp.full((1, BQ), TOPK, f32)
    prefix = jnp.zeros((1, BQ), jnp.int32)
    above = jnp.zeros((1, BQ), f32)
    for byte in (3, 2, 1, 0):
        def in_class(key, byte=byte, prefix=prefix):
            return (key >> (8 * byte + 8)) == prefix

        def prepare(kb, carry, byte=byte, in_class=in_class):
            key = sk[kb]
            digit = byte_of(key, byte).astype(f32)
            if byte < 3:
                digit = jnp.where(in_class(key), digit, -1.0)
            dg[kb] = digit.astype(dg.dtype)
            return carry

        lax.fori_loop(0, nkb, prepare, 0)
        rest_digit = byte_of(neg_key, byte)
        rest_on = in_class(neg_key) if byte < 3 else (neg_key == neg_key)

        def count_ge(cand, rest_digit=rest_digit, rest_on=rest_on):
            cand_b = cand.astype(f32).astype(jnp.bfloat16)

            def body(kb, acc):
                hit = jnp.where(dg[kb] >= cand_b, one, zero)
                return acc + _rows_tree(hit, jnp.add, BF16_ROWS).astype(f32)

            acc = lax.fori_loop(0, nkb, body, jnp.zeros((BF16_ROWS, BQ), f32))
            rest = jnp.where(rest_on & (rest_digit >= cand), n_rest, 0.0)
            return jnp.sum(acc, axis=0, keepdims=True) + rest

        digit_thr, above_here = _bit_search(count_ge, target, jnp.zeros((1, BQ), jnp.int32),
                                            jnp.zeros((1, BQ), f32), 7)
        prefix = (digit_thr - 128) if byte == 3 else (prefix * 256 + digit_thr)
        above = above + above_here
        target = target - above_here
    thr = prefix
    need = TOPK - above

    acc_s[...] = jnp.zeros(acc_s.shape, f32)

    far_bias = bias_ref[B_NEAR, 0:1, :]

    def attend(far, kb, carry):
        eq_seen, m_old, l_old = carry
        keys = sk[kb]
        eq = keys == thr
        eq_f = jnp.where(eq, 1.0, 0.0)
        before = jnp.dot(tri_ref[...], eq_f.astype(jnp.bfloat16), preferred_element_type=f32)
        take_eq = jnp.where(eq, jnp.where(eq_seen + before < need, 1.0, 0.0), 0.0)
        take = jnp.where(keys > thr, 1.0, take_eq)
        mask_add = jnp.where(admissible(kb), jnp.where(take > 0.0, 0.0, NEG), NEG)
        cblk = ckvn[pl.ds(pl.multiple_of(kb * KB, KB), KB), :]
        dots = lax.dot_general(cblk, ql[...], _NT, preferred_element_type=f32)
        tile = nkb - 1 - kb
        m_parts, l_parts, a_parts = [], [], []
        for g in range(HQ // LANES):
            cols = slice(g * LANES, (g + 1) * LANES)
            qcols = slice((g % (BQ // LANES)) * LANES, (g % (BQ // LANES) + 1) * LANES)
            mo = m_old[:, cols]
            if far:
                s = dots[:, cols] + mask_add[:, qcols]
                shift = far_bias[:, cols]
                mn = jnp.maximum(mo, jnp.max(_rows8(s, jnp.maximum), axis=0, keepdims=True) + shift)
                e = jnp.exp2(s - (mn - shift))
            else:
                s = dots[:, cols] + bias_ref[tile, :, cols] + mask_add[:, qcols]
                mn = jnp.maximum(mo, jnp.max(_rows8(s, jnp.maximum), axis=0, keepdims=True))
                e = jnp.exp2(s - mn)
            a = jnp.exp2(mo - mn)
            l_parts.append(a * l_old[:, cols] + jnp.sum(_rows8(e, jnp.add), axis=0, keepdims=True))
            m_parts.append(mn)
            a_parts.append(a)
            p_s[:, cols] = e.astype(p_s.dtype)
        alpha = jnp.concatenate(a_parts, axis=1)
        acc_s[...] = acc_s[...] * alpha + lax.dot_general(cblk, p_s[...], _TN, preferred_element_type=f32)
        eq_seen = eq_seen + jnp.sum(_rows8(eq_f, jnp.add), axis=0, keepdims=True)
        return eq_seen, jnp.concatenate(m_parts, axis=1), jnp.concatenate(l_parts, axis=1)

    init = (jnp.zeros((1, BQ), f32), jnp.full((1, HQ), 4 * NEG, f32), jnp.zeros((1, HQ), f32))
    n_far = jnp.maximum(nkb - B_NEAR, 0)
    carry = lax.fori_loop(0, n_far, functools.partial(attend, True), init)
    _, _, l_fin = lax.fori_loop(n_far, nkb, functools.partial(attend, False), carry)

    o_lat_t = acc_s[...] / l_fin
    for p in range(B_HEADS // 2):
        out = jnp.zeros((BQ, LANES), f32)
        for h in (2 * p, 2 * p + 1):
            o_lat = o_lat_t[:, h * BQ:(h + 1) * BQ].T.astype(jnp.bfloat16)
            out = out + jnp.dot(o_lat, wuv_ref[h], preferred_element_type=f32)
        o_ref[:, p * LANES:(p + 1) * LANES] = out.astype(o_ref.dtype)


def _attn_b(proj3, gain, wuk, wuv, bias, tri, nbatch=BATCH):
    nq = SEQ // BQ
    xw = IN_A - X_COL0
    return pl.pallas_call(
        _attn_b_kernel,
        grid=(nbatch, nq),
        in_specs=[
            pl.BlockSpec((None, BQ, B_W), lambda b, i: (b, i, O_QB // B_W)),
            pl.BlockSpec((None, BQ, xw), lambda b, i: (b, i, X_COL0 // xw)),
            pl.BlockSpec((None, SEQ, LANES), lambda b, i: (b, 0, O_KI // LANES)),
            pl.BlockSpec((None, SEQ, LANES), lambda b, i: (b, 0, O_CKV // LANES)),
            pl.BlockSpec((1, B_LATENT), lambda b, i: (0, 0)),
            pl.BlockSpec((B_HEADS, LANES, B_LATENT), lambda b, i: (0, 0, 0)),
            pl.BlockSpec((B_HEADS, B_LATENT, LANES), lambda b, i: (0, 0, 0)),
            pl.BlockSpec((B_NEAR + 1, KB, HQ), lambda b, i: (0, 0, 0), pipeline_mode=pl.Buffered(1)),
            pl.BlockSpec((KB, KB), lambda b, i: (0, 0)),
        ],
        out_specs=pl.BlockSpec((None, BQ, B_W), lambda b, i: (b, i, 0)),
        out_shape=jax.ShapeDtypeStruct((nbatch, SEQ, B_W), jnp.bfloat16),
        scratch_shapes=[
            pltpu.VMEM((SEQ, B_LATENT), jnp.bfloat16),
            pltpu.VMEM((SEQ, LANES), jnp.bfloat16),
            pltpu.VMEM((HQ, B_LATENT), jnp.bfloat16),
            pltpu.VMEM((HQ, LANES), jnp.bfloat16),
            pltpu.VMEM((NKB, KB, BQ), jnp.int32),
            pltpu.VMEM((NKB, KB, BQ), jnp.bfloat16),
            pltpu.VMEM((B_LATENT, HQ), jnp.float32),
            pltpu.VMEM((KB, HQ), jnp.bfloat16),
        ],
        compiler_params=_cparams(("arbitrary", "arbitrary")),
        name="attn_b",
    )(proj3, proj3, proj3, proj3, gain, wuk, wuv, bias, tri)


MERGE_BN = 1024
MERGE_BM = 1024


def _merge_kernel(oa_ref, ob_ref, oc_ref, ga_ref, gb_ref, gc_ref, wa_ref, wb_ref, wc_ref, o_ref, wa, wb, wc):
    f32 = jnp.float32

    @pl.when(pl.program_id(1) == 0)
    def _():
        wa[...] = wa_ref[...].astype(wa.dtype)
        wb[...] = wb_ref[...].astype(wb.dtype)
        wc[...] = wc_ref[...].astype(wc.dtype)

    m = ga_ref[...].astype(f32) * jnp.dot(oa_ref[...], wa[...], preferred_element_type=f32)
    m = m + gb_ref[...].astype(f32) * jnp.dot(ob_ref[...], wb[...], preferred_element_type=f32)
    m = m + gc_ref[...].astype(f32) * jnp.dot(oc_ref[...], wc[...], preferred_element_type=f32)
    o_ref[...] = m.astype(o_ref.dtype)


def _merge(oa, ob, oc, gates, w_branch, layer):
    t = oa.shape[0]
    bn, bm = MERGE_BN, MERGE_BM
    gstep = D_MODEL // bn
    return pl.pallas_call(
        _merge_kernel,
        grid=(D_MODEL // bn, t // bm),
        in_specs=[
            pl.BlockSpec((bm, A_W), lambda j, i: (i, 0)),
            pl.BlockSpec((bm, B_W), lambda j, i: (i, 0)),
            pl.BlockSpec((bm, C_W), lambda j, i: (i, 0)),
            pl.BlockSpec((bm, bn), lambda j, i: (i, j)),
            pl.BlockSpec((bm, bn), lambda j, i: (i, gstep + j)),
            pl.BlockSpec((bm, bn), lambda j, i: (i, 2 * gstep + j)),
            pl.BlockSpec((None, A_W, bn), lambda j, i: (layer, 0, j)),
            pl.BlockSpec((None, B_W, bn), lambda j, i: (layer, A_W // B_W, j)),
            pl.BlockSpec((None, C_W, bn), lambda j, i: (layer, (A_W + B_W) // C_W, j)),
        ],
        out_specs=pl.BlockSpec((bm, bn), lambda j, i: (i, j)),
        out_shape=jax.ShapeDtypeStruct((t, D_MODEL), jnp.bfloat16),
        scratch_shapes=[pltpu.VMEM((A_W, bn), jnp.bfloat16), pltpu.VMEM((B_W, bn), jnp.bfloat16),
                        pltpu.VMEM((C_W, bn), jnp.bfloat16)],
        compiler_params=_cparams(("arbitrary", "arbitrary")),
        name="merge",
    )(oa, ob, oc, gates, gates, gates, w_branch, w_branch, w_branch)


FINISH_CHUNKS = 4


def _finish_chunked(y_of_rows, nrows, x_ref, gp_ref, gn_ref, xo_ref, ho_ref):
    step = nrows // FINISH_CHUNKS
    for r in range(0, nrows, step):
        rows = slice(r, r + step)
        xn = x_ref[rows, :] + _rms(y_of_rows(rows), gp_ref[...])
        xo_ref[rows, :] = xn
        if ho_ref is not None:
            ho_ref[rows, :] = _rms(xn, gn_ref[...]).astype(ho_ref.dtype)


def _gemm_res_kernel(nk, with_next, a_ref, w_ref, x_ref, gp_ref, gn_ref, xo_ref, *rest):
    ho_ref = rest[0] if with_next else None
    f32 = jnp.float32
    bm = a_ref.shape[0]
    if nk == 1:
        _finish_chunked(lambda rows: jnp.dot(a_ref[rows, :], w_ref[...], preferred_element_type=f32),
                        bm, x_ref, gp_ref, gn_ref, xo_ref, ho_ref)
        return
    acc = rest[-1]
    k = pl.program_id(1)

    @pl.when(k == 0)
    def _():
        acc[...] = jnp.dot(a_ref[...], w_ref[...], preferred_element_type=f32)

    @pl.when((k > 0) & (k < nk - 1))
    def _():
        acc[...] += jnp.dot(a_ref[...], w_ref[...], preferred_element_type=f32)

    @pl.when(k == nk - 1)
    def _():
        _finish_chunked(lambda rows: acc[rows, :] + jnp.dot(a_ref[rows, :], w_ref[...], preferred_element_type=f32),
                        bm, x_ref, gp_ref, gn_ref, xo_ref, ho_ref)


def _gemm_res(a, w, layer, x, g_post, g_next, bk, bm=512, name="gemm_res"):
    t, kdim = a.shape
    n = w.shape[2]
    nk = kdim // bk
    with_next = g_next is not None
    if g_next is None:
        g_next = g_post
    out_shape = [jax.ShapeDtypeStruct((t, n), jnp.float32)]
    out_specs = [pl.BlockSpec((bm, n), lambda i, k: (i, 0))]
    if with_next:
        out_shape.append(jax.ShapeDtypeStruct((t, n), jnp.bfloat16))
        out_specs.append(pl.BlockSpec((bm, n), lambda i, k: (i, 0)))
    res = pl.pallas_call(
        functools.partial(_gemm_res_kernel, nk, with_next),
        grid=(t // bm, nk),
        in_specs=[
            pl.BlockSpec((bm, bk), lambda i, k: (i, k)),
            pl.BlockSpec((None, bk, n), lambda i, k: (layer, k, 0)),
            pl.BlockSpec((bm, n), lambda i, k: (i, 0)),
            pl.BlockSpec((1, n), lambda i, k: (0, 0)),
            pl.BlockSpec((1, n), lambda i, k: (0, 0)),
        ],
        out_specs=out_specs,
        out_shape=out_shape,
        scratch_shapes=[pltpu.VMEM((bm, n), jnp.float32)] if nk > 1 else [],
        compiler_params=_cparams(("parallel", "arbitrary")),
        name=name,
    )(a, w, x, g_post, g_next)
    return (res[0], res[1]) if with_next else (res[0], None)


def _cast_once(pairs):
    @pl.when(pl.program_id(0) == 0)
    def _():
        for src, dst in pairs:
            dst[...] = src[...].astype(dst.dtype)


def _memkv_kernel(m_ref, g_ref, w_ref, o_ref, wb):
    _cast_once([(w_ref, wb)])
    mn = _rms(m_ref[...], g_ref[...]).astype(jnp.bfloat16)
    o_ref[...] = jnp.dot(mn, wb[...], preferred_element_type=jnp.float32).astype(o_ref.dtype)


def _memkv(mem2, g, w, layer):
    t, d = mem2.shape
    n = w.shape[2]
    bm = 512
    return pl.pallas_call(
        _memkv_kernel,
        grid=(t // bm,),
        in_specs=[pl.BlockSpec((bm, d), lambda i: (i, 0)), pl.BlockSpec((1, d), lambda i: (0, 0)),
                  pl.BlockSpec((None, d, n), lambda i: (layer, 0, 0))],
        out_specs=pl.BlockSpec((bm, n), lambda i: (i, 0)),
        out_shape=jax.ShapeDtypeStruct((t, n), jnp.bfloat16),
        scratch_shapes=[pltpu.VMEM((d, n), jnp.bfloat16)],
        compiler_params=_cparams(("arbitrary",)),
        name="mem_kv",
    )(mem2, g, w)


XA_BM = 512


def _mix_out_xattn_kernel(m_ref, wo_ref, wq_ref, kv_ref, wmo_ref, x_ref, g1_ref, g2_ref, g3_ref, g4_ref,
                          xo_ref, ho_ref, x1_s, h1_s):
    f32 = jnp.float32
    bm = m_ref.shape[0]
    _finish_chunked(lambda rows: jnp.dot(m_ref[rows, :], wo_ref[...], preferred_element_type=f32),
                    bm, x_ref, g1_ref, g2_ref, x1_s, h1_s)
    q = jnp.dot(h1_s[...], wq_ref[...], preferred_element_type=f32).astype(jnp.bfloat16)
    scale = MEM_HEAD_DIM ** -0.5 * LOG2E
    hcols = [slice(h * LANES, (h + 1) * LANES) for h in range(MEM_HEADS)]
    ss = [lax.dot_general(q[:, c], kv_ref[:, c], _NT, preferred_element_type=f32) * scale for c in hcols]
    ms = [jnp.max(s, axis=-1, keepdims=True) for s in ss]
    es = [jnp.exp2(s - m) for s, m in zip(ss, ms)]
    ls = [jnp.sum(e, axis=-1, keepdims=True) for e in es]
    pv = [jnp.dot(e.astype(jnp.bfloat16), kv_ref[:, MEM_W + h * LANES:MEM_W + (h + 1) * LANES],
                  preferred_element_type=f32) for h, e in enumerate(es)]
    o = jnp.concatenate([(o_h / l).astype(jnp.bfloat16) for o_h, l in zip(pv, ls)], axis=-1)
    _finish_chunked(lambda rows: jnp.dot(o[rows, :], wmo_ref[...], preferred_element_type=f32),
                    bm, x1_s, g3_ref, g4_ref, xo_ref, ho_ref)


def _mix_out_xattn(merged, w_o, w_mq, kv, w_mo, layer, x, g1, g2, g3, g4):
    t, d = merged.shape
    bm = XA_BM
    per_batch = SEQ // bm
    once = pl.Buffered(1)
    row = pl.BlockSpec((bm, d), lambda i: (i, 0))
    gain = pl.BlockSpec((1, d), lambda i: (0, 0))
    return pl.pallas_call(
        _mix_out_xattn_kernel,
        grid=(t // bm,),
        in_specs=[
            row,
            pl.BlockSpec((None, d, d), lambda i: (layer, 0, 0), pipeline_mode=once),
            pl.BlockSpec((None, d, MEM_W), lambda i: (layer, 0, 0), pipeline_mode=once),
            pl.BlockSpec((MEM_LEN, 2 * MEM_W), lambda i: (i // per_batch, 0)),
            pl.BlockSpec((None, MEM_W, d), lambda i: (layer, 0, 0), pipeline_mode=once),
            row, gain, gain, gain, gain,
        ],
        out_specs=[row, row],
        out_shape=[jax.ShapeDtypeStruct((t, d), jnp.float32), jax.ShapeDtypeStruct((t, d), jnp.bfloat16)],
        scratch_shapes=[pltpu.VMEM((bm, d), jnp.float32), pltpu.VMEM((bm, d), jnp.bfloat16)],
        compiler_params=_cparams(("parallel",)),
        name="mix_out_xattn",
    )(merged, w_o, w_mq, kv, w_mo, x, g1, g2, g3, g4)


def _ffn_up_kernel(h_ref, wgf_ref, wvf_ref, cwg_ref, cwv_ref, cbg_ref, cbv_ref, o_ref, wg_ref, wv_ref, ug, uv):
    j = pl.program_id(0)
    i = pl.program_id(1)
    tiles_per_seq = SEQ // FF_BM
    last = FF_P // FF_BN - 1
    valid = D_FF - last * FF_BN
    shift = FF_BN - valid

    @pl.when((i == 0) & (j < last))
    def _():
        wg_ref[...] = wgf_ref[...].astype(wg_ref.dtype)
        wv_ref[...] = wvf_ref[...].astype(wv_ref.dtype)

    @pl.when((i == 0) & (j == last))
    def _():
        zeros = jnp.zeros((wg_ref.shape[0], FF_BN - valid), wg_ref.dtype)
        wg_ref[:, :valid] = wgf_ref[:, :valid].astype(wg_ref.dtype)
        wv_ref[:, :valid] = wvf_ref[:, shift:].astype(wv_ref.dtype)
        wg_ref[:, valid:] = zeros
        wv_ref[:, valid:] = zeros

    for u in (ug, uv):
        @pl.when(i % tiles_per_seq == 0)
        def _():
            u[0:SUBLANES, :] = jnp.zeros((SUBLANES, FF_BN), jnp.float32)

        @pl.when(i % tiles_per_seq != 0)
        def _():
            u[0:SUBLANES, :] = u[FF_BM:FF_BM + SUBLANES, :]

    def conv(u, cw_ref, cb_ref, r0):
        base = SUBLANES + r0
        acc = cb_ref[...] + u[base - 2:base - 2 + FF_CH, :] * cw_ref[0:1, :]
        acc = acc + u[base - 1:base - 1 + FF_CH, :] * cw_ref[1:2, :]
        return acc + u[base:base + FF_CH, :] * cw_ref[2:3, :]

    for c in range(FF_BM // FF_CH):
        r0 = c * FF_CH
        hb = h_ref[r0:r0 + FF_CH, :]
        ug[SUBLANES + r0:SUBLANES + r0 + FF_CH, :] = jnp.dot(hb, wg_ref[...], preferred_element_type=jnp.float32)
        uv[SUBLANES + r0:SUBLANES + r0 + FF_CH, :] = jnp.dot(hb, wv_ref[...], preferred_element_type=jnp.float32)
        gate = conv(ug, cwg_ref, cbg_ref, r0)
        val = conv(uv, cwv_ref, cbv_ref, r0)
        o_ref[r0:r0 + FF_CH, :] = (jax.nn.gelu(gate) * val).astype(o_ref.dtype)


def _ffn_up(h, w_up, layer, conv_w, conv_b):
    t, d = h.shape
    nj = FF_P // FF_BN
    w_block = (None, pl.Element(d), pl.Element(FF_BN))
    return pl.pallas_call(
        _ffn_up_kernel,
        grid=(nj, t // FF_BM),
        in_specs=[
            pl.BlockSpec((FF_BM, d), lambda j, i: (i, 0)),
            pl.BlockSpec(w_block, lambda j, i: (layer, 0, pl.multiple_of(j * FF_BN, LANES))),
            pl.BlockSpec(w_block, lambda j, i: (
                layer, 0, pl.multiple_of(jnp.minimum(D_FF + j * FF_BN, 2 * D_FF - FF_BN), LANES))),
            pl.BlockSpec((CONV_W, FF_BN), lambda j, i: (0, j)),
            pl.BlockSpec((CONV_W, FF_BN), lambda j, i: (0, nj + j)),
            pl.BlockSpec((1, FF_BN), lambda j, i: (0, j)),
            pl.BlockSpec((1, FF_BN), lambda j, i: (0, nj + j)),
        ],
        out_specs=pl.BlockSpec((FF_BM, FF_BN), lambda j, i: (i, j)),
        out_shape=jax.ShapeDtypeStruct((t, FF_P), jnp.bfloat16),
        scratch_shapes=[pltpu.VMEM((d, FF_BN), jnp.bfloat16), pltpu.VMEM((d, FF_BN), jnp.bfloat16),
                        pltpu.VMEM((FF_BM + SUBLANES, FF_BN), jnp.float32),
                        pltpu.VMEM((FF_BM + SUBLANES, FF_BN), jnp.float32)],
        compiler_params=_cparams(("arbitrary", "arbitrary")),
        name="ffn_up",
    )(h, w_up, w_up, conv_w, conv_w, conv_b, conv_b)


def _pad_heads(w, axis):
    h = w.shape[0]
    zero = jnp.zeros_like(w)
    even = jnp.concatenate([w, zero], axis=axis)
    odd = jnp.concatenate([zero, w], axis=axis)
    sel = (jnp.arange(h) % 2 == 0).reshape((h, 1, 1))
    return jnp.where(sel, even, odd)


def _toeplitz(fn, rows, cols):
    ks = np.concatenate([np.arange(0, cols), np.arange(-(rows - 1), 0)])
    w = fn(ks)
    h, period = w.shape
    x = jnp.tile(w, (1, rows))[:, :rows * (period - 1)].reshape(h, rows, period - 1)
    return x[:, :, :cols].astype(jnp.float32)


def _band(rows, cols, left):
    diff = left + np.arange(rows)[:, None] // CHUNK - np.arange(cols)[None, :] // CHUNK
    return (diff >= 0) & (diff <= left)


def _bias_a(rel_bias):
    fn = lambda ks: rel_bias[np.clip(A_PAD - ks, -A_MAX_REL, A_MAX_REL) + A_MAX_REL].T
    bias = jnp.where(_band(QB, A_WIN, A_LEFT_CHUNKS)[None], _toeplitz(fn, QB, A_WIN) * LOG2E, NEG)
    return bias.reshape(A_HEADS * QB, A_WIN)


def _bias_c(t5_c):
    fn = lambda ks: t5_c[_t5_bucket(jnp.asarray(ks - C_PAD, jnp.int32))].T
    bias = jnp.where(_band(QB, C_WIN, C_LEFT_CHUNKS)[None], _toeplitz(fn, QB, C_WIN) * LOG2E, NEG)
    return bias.reshape(C_Q_HEADS * QB, C_WIN)


def _bias_b(t5_b):
    tiles = []
    for n in range(B_NEAR + 1):
        off = KB * (n - (B_QK - 1)) if n < B_NEAR else SEQ
        fn = lambda ks, off=off: t5_b[_t5_bucket(jnp.asarray(-ks - off, jnp.int32))].T
        tile = _toeplitz(fn, KB, BQ)
        tiles.append(jnp.transpose(tile * LOG2E, (1, 0, 2)).reshape(KB, HQ))
    return jnp.stack(tiles)


def _pad_ff(a, dtype):
    z = jnp.zeros((a.shape[0], FF_P - D_FF), dtype)
    return jnp.concatenate([a[:, :D_FF].astype(dtype), z, a[:, D_FF:].astype(dtype), z], axis=1)


def kernel(x, mem, t5_table, norm_gains, w_in, a_rel_bias, ckv_gain, w_uk, w_uv, sinks, w_branch, w_o,
           mem_gain, w_mq, w_mkv, w_mo, w_up, conv_w, conv_b, w_down):
    bf16 = jnp.bfloat16
    xs = x.reshape(TOKENS, D_MODEL)
    mem2 = mem.reshape(BATCH * MEM_LEN, D_MODEL)
    tri = jnp.asarray(np.tril(np.ones((KB, KB), np.float32), -1), bf16)
    bias_b = _bias_b(t5_table[:, :B_HEADS])
    bias_c = _bias_c(t5_table[:, B_HEADS:])
    gains = norm_gains.reshape(DEPTH, 6, 1, D_MODEL)
    w_o_b, w_mq_b, w_mo_b = (w.astype(bf16) for w in (w_o, w_mq, w_mo))
    w_dn_b = jnp.concatenate([w_down.astype(bf16), jnp.zeros((DEPTH, FF_P - D_FF, D_MODEL), bf16)], axis=1)
    w_in_t = jnp.swapaxes(w_in, 1, 2)

    h = _norm(xs, gains[0, 0])
    for l in range(DEPTH):
        g = gains[l]
        proja = _proj(h, w_in_t, l, 0, IN_A, 1024, gate=False, name="in_proj_a")
        projc = _proj(h, w_in_t, l, O_QC, C_COLS, C_COLS, gate=False, name="in_proj_c")
        gates = _proj(h, w_in_t, l, O_GL, 3 * D_MODEL, 1024, gate=True, name="in_proj_g")
        proja3 = proja.reshape(BATCH, SEQ, IN_A)
        oa = _attn_a(proja3, _bias_a(a_rel_bias[l]))
        wuk = jnp.transpose(_pad_heads(w_uk[l], axis=2), (0, 2, 1)).astype(bf16)
        wuv = _pad_heads(w_uv[l], axis=2).astype(bf16)
        ob = _attn_b(proja3, ckv_gain[l].reshape(1, B_LATENT), wuk, wuv, bias_b, tri)
        oc = _attn_c(sinks[l], projc.reshape(BATCH, SEQ, C_COLS), bias_c)
        merged = _merge(oa.reshape(TOKENS, A_W), ob.reshape(TOKENS, B_W), oc.reshape(TOKENS, C_W), gates, w_branch, l)
        kv = _memkv(mem2, mem_gain[l].reshape(1, D_MODEL), w_mkv, l)
        xs, h = _mix_out_xattn(merged, w_o_b, w_mq_b, kv, w_mo_b, l, xs, g[1], g[2], g[3], g[4])
        hidden = _ffn_up(h, w_up, l, _pad_ff(conv_w[l], jnp.float32), _pad_ff(conv_b[l].reshape(1, -1), jnp.float32))
        g_next = gains[l + 1, 0] if l + 1 < DEPTH else None
        xs, h = _gemm_res(hidden, w_dn_b, l, xs, g[5], g_next, bk=DOWN_BK, name="ffn_down")
    return xs.reshape(BATCH, SEQ, D_MODEL)
```

```python
import functools
import math

import numpy as np
import jax
import jax.numpy as jnp
from jax import lax
from jax.experimental import pallas as pl
from jax.experimental.pallas import tpu as pltpu

D_MODEL = 2048
BATCH = 4
SEQ = 2048
DEPTH = 2
TOKENS = BATCH * SEQ
CHUNK = 64
EPS = 1e-6
NEG = -1e30
LOG2E = math.log2(math.e)
A_HEADS = 8
A_LEFT_CHUNKS = 8
A_MAX_REL = 128
A_W = 512
B_HEADS = 8
B_W = 512
B_LATENT = 128
IDX_HEADS = 8
IDX_DIM = 64
TOPK = 256
C_Q_HEADS = 16
C_GROUP = 8
C_W = 1024
C_LEFT_CHUNKS = 2
T5_BUCKETS = 32
T5_MAX_DIST = 256
MEM_LEN = 256
MEM_HEADS = 4
MEM_HEAD_DIM = 128
MEM_W = 512
D_FF = 5504
CONV_W = 3

LANES = 128
SUBLANES = 8
HEAD_DIM = 64
QB = 128
KB = 256
VMEM_LIMIT = 56 * 1024 * 1024

O_QA, O_KA, O_VA, O_QB = 0, 512, 1024, 1536
O_CKV = 2048
O_QI = 2176
O_KI = 2688
O_WI = 2752
O_QC = 2760
O_GL = 4040
IN_W = O_GL + 3 * D_MODEL
IN_A = 3072
X_COL0 = 2048
C_COLS = O_GL - O_QC
PROJ_BM = 1024
PROJ_TR = 256

FF_P = 5632
FF_BN = 512
FF_BM = 1024
FF_CH = 256
DOWN_BK = 1408

A_WIN = (A_LEFT_CHUNKS + 2) * CHUNK
C_WIN = (C_LEFT_CHUNKS + 2) * CHUNK
A_PAD = A_LEFT_CHUNKS * CHUNK
C_PAD = C_LEFT_CHUNKS * CHUNK
BQ = 256
B_QK = BQ // KB
B_NEAR = B_QK + 1
NKB = SEQ // KB
HQ = B_HEADS * BQ

_NT = (((1,), (1,)), ((), ()))
_TN = (((0,), (0,)), ((), ()))


def _cparams(sem):
    return pltpu.CompilerParams(dimension_semantics=sem, vmem_limit_bytes=VMEM_LIMIT)


def _t5_bucket(rel):
    half = T5_BUCKETS // 2
    max_exact = half // 2
    sign = jnp.where(rel > 0, half, 0)
    d = jnp.abs(rel)
    d_f = jnp.maximum(d, 1).astype(jnp.float32)
    large = max_exact + (jnp.log(d_f / max_exact) / math.log(T5_MAX_DIST / max_exact) * (half - max_exact)).astype(jnp.int32)
    large = jnp.minimum(large, half - 1)
    return sign + jnp.where(d < max_exact, d, large)


def _far_bucket_is_constant():
    assert BQ % KB == 0
    d = np.arange(KB + 1, SEQ, dtype=np.float32)
    assert d[0] > T5_MAX_DIST
    large = 8 + (np.log(d / 8) / math.log(T5_MAX_DIST / 8) * 8).astype(np.int32)
    return bool(np.all(np.minimum(large, 15) == 15))


assert _far_bucket_is_constant()


def _rms(v, g):
    return v * lax.rsqrt(jnp.mean(v * v, axis=-1, keepdims=True) + EPS) * g


def _norm_kernel(x_ref, g_ref, o_ref):
    o_ref[...] = _rms(x_ref[...], g_ref[...]).astype(o_ref.dtype)


def _norm(x, g, bm=1024):
    t, d = x.shape
    return pl.pallas_call(
        _norm_kernel,
        grid=(t // bm,),
        in_specs=[pl.BlockSpec((bm, d), lambda i: (i, 0)), pl.BlockSpec((1, d), lambda i: (0, 0))],
        out_specs=pl.BlockSpec((bm, d), lambda i: (i, 0)),
        out_shape=jax.ShapeDtypeStruct((t, d), jnp.bfloat16),
        compiler_params=_cparams(("parallel",)),
        name="rmsnorm",
    )(x, g)


def _proj_kernel(gate, h_ref, wt_ref, o_ref, wb):
    @pl.when(pl.program_id(1) == 0)
    def _():
        for c in range(wt_ref.shape[0] // PROJ_TR):
            rows = slice(c * PROJ_TR, (c + 1) * PROJ_TR)
            wb[:, rows] = wt_ref[rows, :].T.astype(wb.dtype)

    acc = jnp.dot(h_ref[...], wb[...], preferred_element_type=jnp.float32)
    o_ref[...] = (jax.nn.sigmoid(acc) if gate else acc).astype(o_ref.dtype)


def _proj(h, w_t, layer, col0, n, bn, gate, name):
    t, k = h.shape
    assert col0 % SUBLANES == 0 and bn % SUBLANES == 0
    w_spec = pl.BlockSpec((None, pl.Element(bn), pl.Element(k)),
                          lambda j, i: (layer, pl.multiple_of(col0 + j * bn, SUBLANES), 0))
    return pl.pallas_call(
        functools.partial(_proj_kernel, gate),
        grid=(n // bn, t // PROJ_BM),
        in_specs=[pl.BlockSpec((PROJ_BM, k), lambda j, i: (i, 0)), w_spec],
        out_specs=pl.BlockSpec((PROJ_BM, bn), lambda j, i: (i, j)),
        out_shape=jax.ShapeDtypeStruct((t, n), jnp.bfloat16),
        scratch_shapes=[pltpu.VMEM((k, bn), jnp.bfloat16)],
        compiler_params=_cparams(("arbitrary", "arbitrary")),
        name=name,
    )(h, w_t)


def _lane_masks(dtype):
    lane = lax.broadcasted_iota(jnp.int32, (1, LANES), 1)
    lo = (lane < HEAD_DIM).astype(jnp.float32)
    return lo.astype(dtype), (1.0 - lo).astype(dtype)


def _swap_halves(x):
    return pltpu.roll(x.astype(jnp.float32), HEAD_DIM, 1).astype(x.dtype)


def _fill_padded(src, dst_ref, pad):
    dst_ref[0:pad, :] = jnp.zeros((pad, dst_ref.shape[1]), dst_ref.dtype)
    dst_ref[pad:pad + SEQ, :] = src


def _attn_a_kernel(q_ref, k_ref, v_ref, bias_ref, o_ref, kpad, vpad):
    i = pl.program_id(1)

    @pl.when(i == 0)
    def _():
        _fill_padded(k_ref[...], kpad, A_PAD)
        _fill_padded(v_ref[...], vpad, A_PAD)

    start = pl.multiple_of(i * QB, QB)
    kw = kpad[pl.ds(start, A_WIN), :]
    vw = vpad[pl.ds(start, A_WIN), :]
    mlo, mhi = _lane_masks(jnp.bfloat16)
    lane = lax.broadcasted_iota(jnp.int32, (QB, LANES), 1)
    kchunk = lax.broadcasted_iota(jnp.int32, (QB, A_WIN), 1) // CHUNK
    pad_mask = jnp.where(kchunk >= A_LEFT_CHUNKS - 2 * i, 0.0, NEG)
    scale = HEAD_DIM ** -0.5 * LOG2E
    pad2 = jnp.concatenate([pad_mask, pad_mask], axis=0)
    npair = A_HEADS // 2
    pcols = [slice(p * LANES, (p + 1) * LANES) for p in range(npair)]
    ss = []
    for p in range(npair):
        qp = q_ref[:, pcols[p]]
        q2 = jnp.concatenate([qp * mlo, qp * mhi], axis=0)
        s = lax.dot_general(q2, kw[:, pcols[p]], _NT, preferred_element_type=jnp.float32)
        ss.append(s * scale + bias_ref[2 * p * QB:(2 * p + 2) * QB, :] + pad2)
    ms = [jnp.max(s, axis=-1, keepdims=True) for s in ss]
    es = [jnp.exp2(s - m) for s, m in zip(ss, ms)]
    ls = [jnp.sum(e, axis=-1, keepdims=True) for e in es]
    outs = [jnp.dot(e.astype(jnp.bfloat16), vw[:, pcols[p]], preferred_element_type=jnp.float32)
            for p, e in enumerate(es)]
    for p in range(npair):
        o = outs[p] / ls[p]
        o_ref[:, pcols[p]] = jnp.where(lane < HEAD_DIM, o[:QB], o[QB:]).astype(o_ref.dtype)


def _attn_a(proj3, bias):
    nq = SEQ // QB
    return pl.pallas_call(
        _attn_a_kernel,
        grid=(BATCH, nq),
        in_specs=[
            pl.BlockSpec((None, QB, A_W), lambda b, i: (b, i, O_QA // A_W)),
            pl.BlockSpec((None, SEQ, A_W), lambda b, i: (b, 0, O_KA // A_W)),
            pl.BlockSpec((None, SEQ, A_W), lambda b, i: (b, 0, O_VA // A_W)),
            pl.BlockSpec((A_HEADS * QB, A_WIN), lambda b, i: (0, 0)),
        ],
        out_specs=pl.BlockSpec((None, QB, A_W), lambda b, i: (b, i, 0)),
        out_shape=jax.ShapeDtypeStruct((BATCH, SEQ, A_W), jnp.bfloat16),
        scratch_shapes=[pltpu.VMEM((SEQ + A_PAD, A_W), jnp.bfloat16), pltpu.VMEM((SEQ + A_PAD, A_W), jnp.bfloat16)],
        compiler_params=_cparams(("arbitrary", "arbitrary")),
        name="attn_a",
    )(proj3, proj3, proj3, bias)


def _attn_c_kernel(sink_ref, q_ref, k_ref, v_ref, bias_ref, o_ref, kpad, kswp, vpad, vswp):
    i = pl.program_id(1)

    @pl.when(i == 0)
    def _():
        k = k_ref[...]
        v = v_ref[...]
        _fill_padded(k, kpad, C_PAD)
        _fill_padded(_swap_halves(k), kswp, C_PAD)
        _fill_padded(v, vpad, C_PAD)
        _fill_padded(_swap_halves(v), vswp, C_PAD)

    start = pl.multiple_of(i * QB, QB)
    mlo, mhi = _lane_masks(jnp.bfloat16)
    lane = lax.broadcasted_iota(jnp.int32, (QB, LANES), 1)
    kchunk = lax.broadcasted_iota(jnp.int32, (QB, C_WIN), 1) // CHUNK
    pad_mask = jnp.where(kchunk >= C_LEFT_CHUNKS - 2 * i, 0.0, NEG)
    scale = HEAD_DIM ** -0.5 * LOG2E
    qs = [q_ref[:, p * LANES:(p + 1) * LANES] for p in range(C_GROUP)]
    npair = C_GROUP // 2
    stacks = []
    for straight in (True, False):
        kref, vref = (kpad, vpad) if straight else (kswp, vswp)
        kw = kref[pl.ds(start, C_WIN), :]
        vw = vref[pl.ds(start, C_WIN), :]
        halves = [int((p >= npair) == straight) for p in range(C_GROUP)]
        qg = jnp.concatenate([qs[p] * (mhi if halves[p] else mlo) for p in range(C_GROUP)], axis=0)
        s_all = lax.dot_general(qg, kw, _NT, preferred_element_type=jnp.float32)
        ps, ls = [], []
        for p in range(C_GROUP):
            h = 2 * p + halves[p]
            s = s_all[p * QB:(p + 1) * QB] * scale + bias_ref[h * QB:(h + 1) * QB, :] + pad_mask
            sink = sink_ref[h] * LOG2E
            m = jnp.maximum(jnp.max(s, axis=-1, keepdims=True), sink)
            e = jnp.exp2(s - m)
            ls.append(jnp.sum(e, axis=-1, keepdims=True) + jnp.exp2(sink - m))
            ps.append(e.astype(jnp.bfloat16))
        o_all = jnp.dot(jnp.concatenate(ps, axis=0), vw, preferred_element_type=jnp.float32)
        stacks.append(([o_all[p * QB:(p + 1) * QB] / ls[p] for p in range(C_GROUP)], halves))
    for p in range(C_GROUP):
        (o1, h1), (o2, _) = stacks
        lo, hi = (o2[p], o1[p]) if h1[p] else (o1[p], o2[p])
        o_ref[:, p * LANES:(p + 1) * LANES] = jnp.where(lane < HEAD_DIM, lo, hi).astype(o_ref.dtype)


def _attn_c(sinks, projc3, bias):
    nq = SEQ // QB
    pad_buf = pltpu.VMEM((SEQ + C_PAD, LANES), jnp.bfloat16)
    return pl.pallas_call(
        _attn_c_kernel,
        grid=(BATCH, nq),
        in_specs=[
            pl.BlockSpec(memory_space=pltpu.SMEM),
            pl.BlockSpec((None, QB, C_W), lambda b, i: (b, i, 0)),
            pl.BlockSpec((None, SEQ, LANES), lambda b, i: (b, 0, C_W // LANES)),
            pl.BlockSpec((None, SEQ, LANES), lambda b, i: (b, 0, C_W // LANES + 1)),
            pl.BlockSpec((C_Q_HEADS * QB, C_WIN), lambda b, i: (0, 0)),
        ],
        out_specs=pl.BlockSpec((None, QB, C_W), lambda b, i: (b, i, 0)),
        out_shape=jax.ShapeDtypeStruct((BATCH, SEQ, C_W), jnp.bfloat16),
        scratch_shapes=[pad_buf, pad_buf, pad_buf, pad_buf],
        compiler_params=_cparams(("arbitrary", "arbitrary")),
        name="attn_c",
    )(sinks, projc3, projc3, projc3, bias)


def _sort_key(x):
    bits = lax.bitcast_convert_type(x + 0.0, jnp.int32)
    return bits ^ ((bits >> 31) & jnp.int32(0x7FFFFFFF))


def _rows_tree(x, op, slab):
    parts = [x[r:r + slab, :] for r in range(0, x.shape[0], slab)]
    while len(parts) > 1:
        nxt = [op(parts[k], parts[k + 1]) for k in range(0, len(parts) - 1, 2)]
        parts = nxt + ([parts[-1]] if len(parts) % 2 else [])
    return parts[0]


def _rows8(x, op):
    return _rows_tree(x, op, SUBLANES)


BF16_ROWS = 16


def _bit_search(count_ge, target, v, above, top_bit):
    def step(t, carry):
        v, above = carry
        cand = v | (jnp.int32(1) << (top_bit - t))
        c = count_ge(cand)
        keep = c >= target
        return jnp.where(keep, cand, v), jnp.where(keep, above, c)

    return lax.fori_loop(0, top_bit + 1, step, (v, above))


def _attn_b_kernel(qb_ref, x_ref, kiw_ref, ckv_ref, gain_ref, wuk_ref, wuv_ref, bias_ref, tri_ref,
                   o_ref, ckvn, kd, ql, qi_all, sk, dg, acc_s, p_s):
    i = pl.program_id(1)
    nkb = (i + 1) * B_QK
    mlo, mhi = _lane_masks(jnp.bfloat16)
    f32 = jnp.float32

    @pl.when(i == 0)
    def _():
        ckvn[...] = _rms(ckv_ref[...].astype(f32), gain_ref[...]).astype(ckvn.dtype)
        kiw = kiw_ref[...].astype(f32)
        lane = lax.broadcasted_iota(jnp.int32, kiw.shape, 1)
        kd[...] = jnp.where(lane < HEAD_DIM, kiw, pltpu.roll(kiw, HEAD_DIM, 1)).astype(kd.dtype)

    for h in range(B_HEADS):
        rows = slice(h * BQ, (h + 1) * BQ)
        qlat = jnp.dot(qb_ref[:, (h // 2) * LANES:(h // 2 + 1) * LANES], wuk_ref[h], preferred_element_type=f32)
        ql[rows, :] = (qlat * (HEAD_DIM ** -0.5 * LOG2E)).astype(ql.dtype)
        qcol = O_QI - X_COL0 + (h // 2) * LANES
        qi_all[rows, :] = x_ref[:, qcol:qcol + LANES] * (mhi if h % 2 else mlo)
    wcol = O_KI - X_COL0
    wi_t = x_ref[:, wcol:wcol + LANES].astype(f32).T * (IDX_HEADS ** -0.5 * IDX_DIM ** -0.5)
    wi_rows = [wi_t[O_WI - O_KI + h:O_WI - O_KI + h + 1, :] for h in range(IDX_HEADS)]

    kpos = lax.broadcasted_iota(jnp.int32, (KB, BQ), 0)
    key_limit = ((i * BQ + lax.broadcasted_iota(jnp.int32, (1, BQ), 1)) // CHUNK + 1) * CHUNK

    def admissible(kb):
        return kb * KB + kpos < key_limit

    def score_block(kb, carry):
        kblk = kd[pl.ds(pl.multiple_of(kb * KB, KB), KB), :]
        dots = lax.dot_general(kblk, qi_all[...], _NT, preferred_element_type=f32)
        score = jnp.zeros((KB, BQ), f32)
        for h in range(IDX_HEADS):
            score = score + jnp.maximum(dots[:, h * BQ:(h + 1) * BQ], 0.0) * wi_rows[h]
        sk[kb] = _sort_key(jnp.where(admissible(kb), score, NEG))
        return carry

    lax.fori_loop(0, nkb, score_block, 0)

    neg_key = _sort_key(jnp.full((1, 1), NEG, f32))
    n_rest = ((NKB - nkb) * KB).astype(f32)
    one, zero = jnp.ones((), jnp.bfloat16), jnp.zeros((), jnp.bfloat16)

    def byte_of(key, byte):
        return ((key >> 24) + 128) if byte == 3 else ((key >> (8 * byte)) & 255)

    target = jnp.full((1, BQ), TOPK, f32)
    prefix = jnp.zeros((1, BQ), jnp.int32)
    above = jnp.zeros((1, BQ), f32)
    for byte in (3, 2, 1, 0):
        def in_class(key, byte=byte, prefix=prefix):
            return (key >> (8 * byte + 8)) == prefix

        def prepare(kb, carry, byte=byte, in_class=in_class):
            key = sk[kb]
            digit = byte_of(key, byte).astype(f32)
            if byte < 3:
                digit = jnp.where(in_class(key), digit, -1.0)
            dg[kb] = digit.astype(dg.dtype)
            return carry

        lax.fori_loop(0, nkb, prepare, 0)
        rest_digit = byte_of(neg_key, byte)
        rest_on = in_class(neg_key) if byte < 3 else (neg_key == neg_key)

        def count_ge(cand, rest_digit=rest_digit, rest_on=rest_on):
            cand_b = cand.astype(f32).astype(jnp.bfloat16)

            def body(kb, acc):
                hit = jnp.where(dg[kb] >= cand_b, one, zero)
                return acc + _rows_tree(hit, jnp.add, BF16_ROWS).astype(f32)

            acc = lax.fori_loop(0, nkb, body, jnp.zeros((BF16_ROWS, BQ), f32))
            rest = jnp.where(rest_on & (rest_digit >= cand), n_rest, 0.0)
            return jnp.sum(acc, axis=0, keepdims=True) + rest

        digit_thr, above_here = _bit_search(count_ge, target, jnp.zeros((1, BQ), jnp.int32),
                                            jnp.zeros((1, BQ), f32), 7)
        prefix = (digit_thr - 128) if byte == 3 else (prefix * 256 + digit_thr)
        above = above + above_here
        target = target - above_here
    thr = prefix
    need = TOPK - above

    acc_s[...] = jnp.zeros(acc_s.shape, f32)

    far_bias = bias_ref[B_NEAR, 0:1, :]

    def attend(far, kb, carry):
        eq_seen, m_old, l_old = carry
        keys = sk[kb]
        eq = keys == thr
        eq_f = jnp.where(eq, 1.0, 0.0)
        before = jnp.dot(tri_ref[...], eq_f.astype(jnp.bfloat16), preferred_element_type=f32)
        take_eq = jnp.where(eq, jnp.where(eq_seen + before < need, 1.0, 0.0), 0.0)
        take = jnp.where(keys > thr, 1.0, take_eq)
        mask_add = jnp.where(admissible(kb), jnp.where(take > 0.0, 0.0, NEG), NEG)
        cblk = ckvn[pl.ds(pl.multiple_of(kb * KB, KB), KB), :]
        dots = lax.dot_general(cblk, ql[...], _NT, preferred_element_type=f32)
        tile = nkb - 1 - kb
        m_parts, l_parts, a_parts = [], [], []
        for g in range(HQ // LANES):
            cols = slice(g * LANES, (g + 1) * LANES)
            qcols = slice((g % (BQ // LANES)) * LANES, (g % (BQ // LANES) + 1) * LANES)
            mo = m_old[:, cols]
            if far:
                s = dots[:, cols] + mask_add[:, qcols]
                shift = far_bias[:, cols]
                mn = jnp.maximum(mo, jnp.max(_rows8(s, jnp.maximum), axis=0, keepdims=True) + shift)
                e = jnp.exp2(s - (mn - shift))
            else:
                s = dots[:, cols] + bias_ref[tile, :, cols] + mask_add[:, qcols]
                mn = jnp.maximum(mo, jnp.max(_rows8(s, jnp.maximum), axis=0, keepdims=True))
                e = jnp.exp2(s - mn)
            a = jnp.exp2(mo - mn)
            l_parts.append(a * l_old[:, cols] + jnp.sum(_rows8(e, jnp.add), axis=0, keepdims=True))
            m_parts.append(mn)
            a_parts.append(a)
            p_s[:, cols] = e.astype(p_s.dtype)
        alpha = jnp.concatenate(a_parts, axis=1)
        acc_s[...] = acc_s[...] * alpha + lax.dot_general(cblk, p_s[...], _TN, preferred_element_type=f32)
        eq_seen = eq_seen + jnp.sum(_rows8(eq_f, jnp.add), axis=0, keepdims=True)
        return eq_seen, jnp.concatenate(m_parts, axis=1), jnp.concatenate(l_parts, axis=1)

    init = (jnp.zeros((1, BQ), f32), jnp.full((1, HQ), 4 * NEG, f32), jnp.zeros((1, HQ), f32))
    n_far = jnp.maximum(nkb - B_NEAR, 0)
    carry = lax.fori_loop(0, n_far, functools.partial(attend, True), init)
    _, _, l_fin = lax.fori_loop(n_far, nkb, functools.partial(attend, False), carry)

    o_lat_t = acc_s[...] / l_fin
    for p in range(B_HEADS // 2):
        out = jnp.zeros((BQ, LANES), f32)
        for h in (2 * p, 2 * p + 1):
            o_lat = o_lat_t[:, h * BQ:(h + 1) * BQ].T.astype(jnp.bfloat16)
            out = out + jnp.dot(o_lat, wuv_ref[h], preferred_element_type=f32)
        o_ref[:, p * LANES:(p + 1) * LANES] = out.astype(o_ref.dtype)


def _attn_b(proj3, gain, wuk, wuv, bias, tri, nbatch=BATCH):
    nq = SEQ // BQ
    xw = IN_A - X_COL0
    return pl.pallas_call(
        _attn_b_kernel,
        grid=(nbatch, nq),
        in_specs=[
            pl.BlockSpec((None, BQ, B_W), lambda b, i: (b, i, O_QB // B_W)),
            pl.BlockSpec((None, BQ, xw), lambda b, i: (b, i, X_COL0 // xw)),
            pl.BlockSpec((None, SEQ, LANES), lambda b, i: (b, 0, O_KI // LANES)),
            pl.BlockSpec((None, SEQ, LANES), lambda b, i: (b, 0, O_CKV // LANES)),
            pl.BlockSpec((1, B_LATENT), lambda b, i: (0, 0)),
            pl.BlockSpec((B_HEADS, LANES, B_LATENT), lambda b, i: (0, 0, 0)),
            pl.BlockSpec((B_HEADS, B_LATENT, LANES), lambda b, i: (0, 0, 0)),
            pl.BlockSpec((B_NEAR + 1, KB, HQ), lambda b, i: (0, 0, 0), pipeline_mode=pl.Buffered(1)),
            pl.BlockSpec((KB, KB), lambda b, i: (0, 0)),
        ],
        out_specs=pl.BlockSpec((None, BQ, B_W), lambda b, i: (b, i, 0)),
        out_shape=jax.ShapeDtypeStruct((nbatch, SEQ, B_W), jnp.bfloat16),
        scratch_shapes=[
            pltpu.VMEM((SEQ, B_LATENT), jnp.bfloat16),
            pltpu.VMEM((SEQ, LANES), jnp.bfloat16),
            pltpu.VMEM((HQ, B_LATENT), jnp.bfloat16),
            pltpu.VMEM((HQ, LANES), jnp.bfloat16),
            pltpu.VMEM((NKB, KB, BQ), jnp.int32),
            pltpu.VMEM((NKB, KB, BQ), jnp.bfloat16),
            pltpu.VMEM((B_LATENT, HQ), jnp.float32),
            pltpu.VMEM((KB, HQ), jnp.bfloat16),
        ],
        compiler_params=_cparams(("arbitrary", "arbitrary")),
        name="attn_b",
    )(proj3, proj3, proj3, proj3, gain, wuk, wuv, bias, tri)


MERGE_BN = 1024
MERGE_BM = 1024


def _merge_kernel(oa_ref, ob_ref, oc_ref, ga_ref, gb_ref, gc_ref, wa_ref, wb_ref, wc_ref, o_ref, wa, wb, wc):
    f32 = jnp.float32

    @pl.when(pl.program_id(1) == 0)
    def _():
        wa[...] = wa_ref[...].astype(wa.dtype)
        wb[...] = wb_ref[...].astype(wb.dtype)
        wc[...] = wc_ref[...].astype(wc.dtype)

    m = ga_ref[...].astype(f32) * jnp.dot(oa_ref[...], wa[...], preferred_element_type=f32)
    m = m + gb_ref[...].astype(f32) * jnp.dot(ob_ref[...], wb[...], preferred_element_type=f32)
    m = m + gc_ref[...].astype(f32) * jnp.dot(oc_ref[...], wc[...], preferred_element_type=f32)
    o_ref[...] = m.astype(o_ref.dtype)


def _merge(oa, ob, oc, gates, w_branch, layer):
    t = oa.shape[0]
    bn, bm = MERGE_BN, MERGE_BM
    gstep = D_MODEL // bn
    return pl.pallas_call(
        _merge_kernel,
        grid=(D_MODEL // bn, t // bm),
        in_specs=[
            pl.BlockSpec((bm, A_W), lambda j, i: (i, 0)),
            pl.BlockSpec((bm, B_W), lambda j, i: (i, 0)),
            pl.BlockSpec((bm, C_W), lambda j, i: (i, 0)),
            pl.BlockSpec((bm, bn), lambda j, i: (i, j)),
            pl.BlockSpec((bm, bn), lambda j, i: (i, gstep + j)),
            pl.BlockSpec((bm, bn), lambda j, i: (i, 2 * gstep + j)),
            pl.BlockSpec((None, A_W, bn), lambda j, i: (layer, 0, j)),
            pl.BlockSpec((None, B_W, bn), lambda j, i: (layer, A_W // B_W, j)),
            pl.BlockSpec((None, C_W, bn), lambda j, i: (layer, (A_W + B_W) // C_W, j)),
        ],
        out_specs=pl.BlockSpec((bm, bn), lambda j, i: (i, j)),
        out_shape=jax.ShapeDtypeStruct((t, D_MODEL), jnp.bfloat16),
        scratch_shapes=[pltpu.VMEM((A_W, bn), jnp.bfloat16), pltpu.VMEM((B_W, bn), jnp.bfloat16),
                        pltpu.VMEM((C_W, bn), jnp.bfloat16)],
        compiler_params=_cparams(("arbitrary", "arbitrary")),
        name="merge",
    )(oa, ob, oc, gates, gates, gates, w_branch, w_branch, w_branch)


FINISH_CHUNKS = 4


def _finish_chunked(y_of_rows, nrows, x_ref, gp_ref, gn_ref, xo_ref, ho_ref):
    step = nrows // FINISH_CHUNKS
    for r in range(0, nrows, step):
        rows = slice(r, r + step)
        xn = x_ref[rows, :] + _rms(y_of_rows(rows), gp_ref[...])
        xo_ref[rows, :] = xn
        if ho_ref is not None:
            ho_ref[rows, :] = _rms(xn, gn_ref[...]).astype(ho_ref.dtype)


def _gemm_res_kernel(nk, with_next, a_ref, w_ref, x_ref, gp_ref, gn_ref, xo_ref, *rest):
    ho_ref = rest[0] if with_next else None
    f32 = jnp.float32
    bm = a_ref.shape[0]
    if nk == 1:
        _finish_chunked(lambda rows: jnp.dot(a_ref[rows, :], w_ref[...], preferred_element_type=f32),
                        bm, x_ref, gp_ref, gn_ref, xo_ref, ho_ref)
        return
    acc = rest[-1]
    k = pl.program_id(1)

    @pl.when(k == 0)
    def _():
        acc[...] = jnp.dot(a_ref[...], w_ref[...], preferred_element_type=f32)

    @pl.when((k > 0) & (k < nk - 1))
    def _():
        acc[...] += jnp.dot(a_ref[...], w_ref[...], preferred_element_type=f32)

    @pl.when(k == nk - 1)
    def _():
        _finish_chunked(lambda rows: acc[rows, :] + jnp.dot(a_ref[rows, :], w_ref[...], preferred_element_type=f32),
                        bm, x_ref, gp_ref, gn_ref, xo_ref, ho_ref)


def _gemm_res(a, w, layer, x, g_post, g_next, bk, bm=512, name="gemm_res"):
    t, kdim = a.shape
    n = w.shape[2]
    nk = kdim // bk
    with_next = g_next is not None
    if g_next is None:
        g_next = g_post
    out_shape = [jax.ShapeDtypeStruct((t, n), jnp.float32)]
    out_specs = [pl.BlockSpec((bm, n), lambda i, k: (i, 0))]
    if with_next:
        out_shape.append(jax.ShapeDtypeStruct((t, n), jnp.bfloat16))
        out_specs.append(pl.BlockSpec((bm, n), lambda i, k: (i, 0)))
    res = pl.pallas_call(
        functools.partial(_gemm_res_kernel, nk, with_next),
        grid=(t // bm, nk),
        in_specs=[
            pl.BlockSpec((bm, bk), lambda i, k: (i, k)),
            pl.BlockSpec((None, bk, n), lambda i, k: (layer, k, 0)),
            pl.BlockSpec((bm, n), lambda i, k: (i, 0)),
            pl.BlockSpec((1, n), lambda i, k: (0, 0)),
            pl.BlockSpec((1, n), lambda i, k: (0, 0)),
        ],
        out_specs=out_specs,
        out_shape=out_shape,
        scratch_shapes=[pltpu.VMEM((bm, n), jnp.float32)] if nk > 1 else [],
        compiler_params=_cparams(("parallel", "arbitrary")),
        name=name,
    )(a, w, x, g_post, g_next)
    return (res[0], res[1]) if with_next else (res[0], None)


def _cast_once(pairs):
    @pl.when(pl.program_id(0) == 0)
    def _():
        for src, dst in pairs:
            dst[...] = src[...].astype(dst.dtype)


def _memkv_kernel(m_ref, g_ref, w_ref, o_ref, wb):
    _cast_once([(w_ref, wb)])
    mn = _rms(m_ref[...], g_ref[...]).astype(jnp.bfloat16)
    o_ref[...] = jnp.dot(mn, wb[...], preferred_element_type=jnp.float32).astype(o_ref.dtype)


def _memkv(mem2, g, w, layer):
    t, d = mem2.shape
    n = w.shape[2]
    bm = 512
    return pl.pallas_call(
        _memkv_kernel,
        grid=(t // bm,),
        in_specs=[pl.BlockSpec((bm, d), lambda i: (i, 0)), pl.BlockSpec((1, d), lambda i: (0, 0)),
                  pl.BlockSpec((None, d, n), lambda i: (layer, 0, 0))],
        out_specs=pl.BlockSpec((bm, n), lambda i: (i, 0)),
        out_shape=jax.ShapeDtypeStruct((t, n), jnp.bfloat16),
        scratch_shapes=[pltpu.VMEM((d, n), jnp.bfloat16)],
        compiler_params=_cparams(("arbitrary",)),
        name="mem_kv",
    )(mem2, g, w)


XA_BM = 512


def _mix_out_xattn_kernel(m_ref, wo_ref, wq_ref, kv_ref, wmo_ref, x_ref, g1_ref, g2_ref, g3_ref, g4_ref,
                          xo_ref, ho_ref, x1_s, h1_s):
    f32 = jnp.float32
    bm = m_ref.shape[0]
    _finish_chunked(lambda rows: jnp.dot(m_ref[rows, :], wo_ref[...], preferred_element_type=f32),
                    bm, x_ref, g1_ref, g2_ref, x1_s, h1_s)
    q = jnp.dot(h1_s[...], wq_ref[...], preferred_element_type=f32).astype(jnp.bfloat16)
    scale = MEM_HEAD_DIM ** -0.5 * LOG2E
    hcols = [slice(h * LANES, (h + 1) * LANES) for h in range(MEM_HEADS)]
    ss = [lax.dot_general(q[:, c], kv_ref[:, c], _NT, preferred_element_type=f32) * scale for c in hcols]
    ms = [jnp.max(s, axis=-1, keepdims=True) for s in ss]
    es = [jnp.exp2(s - m) for s, m in zip(ss, ms)]
    ls = [jnp.sum(e, axis=-1, keepdims=True) for e in es]
    pv = [jnp.dot(e.astype(jnp.bfloat16), kv_ref[:, MEM_W + h * LANES:MEM_W + (h + 1) * LANES],
                  preferred_element_type=f32) for h, e in enumerate(es)]
    o = jnp.concatenate([(o_h / l).astype(jnp.bfloat16) for o_h, l in zip(pv, ls)], axis=-1)
    _finish_chunked(lambda rows: jnp.dot(o[rows, :], wmo_ref[...], preferred_element_type=f32),
                    bm, x1_s, g3_ref, g4_ref, xo_ref, ho_ref)


def _mix_out_xattn(merged, w_o, w_mq, kv, w_mo, layer, x, g1, g2, g3, g4):
    t, d = merged.shape
    bm = XA_BM
    per_batch = SEQ // bm
    once = pl.Buffered(1)
    row = pl.BlockSpec((bm, d), lambda i: (i, 0))
    gain = pl.BlockSpec((1, d), lambda i: (0, 0))
    return pl.pallas_call(
        _mix_out_xattn_kernel,
        grid=(t // bm,),
        in_specs=[
            row,
            pl.BlockSpec((None, d, d), lambda i: (layer, 0, 0), pipeline_mode=once),
            pl.BlockSpec((None, d, MEM_W), lambda i: (layer, 0, 0), pipeline_mode=once),
            pl.BlockSpec((MEM_LEN, 2 * MEM_W), lambda i: (i // per_batch, 0)),
            pl.BlockSpec((None, MEM_W, d), lambda i: (layer, 0, 0), pipeline_mode=once),
            row, gain, gain, gain, gain,
        ],
        out_specs=[row, row],
        out_shape=[jax.ShapeDtypeStruct((t, d), jnp.float32), jax.ShapeDtypeStruct((t, d), jnp.bfloat16)],
        scratch_shapes=[pltpu.VMEM((bm, d), jnp.float32), pltpu.VMEM((bm, d), jnp.bfloat16)],
        compiler_params=_cparams(("parallel",)),
        name="mix_out_xattn",
    )(merged, w_o, w_mq, kv, w_mo, x, g1, g2, g3, g4)


def _ffn_up_kernel(h_ref, wgf_ref, wvf_ref, cwg_ref, cwv_ref, cbg_ref, cbv_ref, o_ref, wg_ref, wv_ref, ug, uv):
    j = pl.program_id(0)
    i = pl.program_id(1)
    tiles_per_seq = SEQ // FF_BM
    last = FF_P // FF_BN - 1
    valid = D_FF - last * FF_BN
    shift = FF_BN - valid

    @pl.when((i == 0) & (j < last))
    def _():
        wg_ref[...] = wgf_ref[...].astype(wg_ref.dtype)
        wv_ref[...] = wvf_ref[...].astype(wv_ref.dtype)

    @pl.when((i == 0) & (j == last))
    def _():
        zeros = jnp.zeros((wg_ref.shape[0], FF_BN - valid), wg_ref.dtype)
        wg_ref[:, :valid] = wgf_ref[:, :valid].astype(wg_ref.dtype)
        wv_ref[:, :valid] = wvf_ref[:, shift:].astype(wv_ref.dtype)
        wg_ref[:, valid:] = zeros
        wv_ref[:, valid:] = zeros

    for u in (ug, uv):
        @pl.when(i % tiles_per_seq == 0)
        def _():
            u[0:SUBLANES, :] = jnp.zeros((SUBLANES, FF_BN), jnp.float32)

        @pl.when(i % tiles_per_seq != 0)
        def _():
            u[0:SUBLANES, :] = u[FF_BM:FF_BM + SUBLANES, :]

    def conv(u, cw_ref, cb_ref, r0):
        base = SUBLANES + r0
        acc = cb_ref[...] + u[base - 2:base - 2 + FF_CH, :] * cw_ref[0:1, :]
        acc = acc + u[base - 1:base - 1 + FF_CH, :] * cw_ref[1:2, :]
        return acc + u[base:base + FF_CH, :] * cw_ref[2:3, :]

    for c in range(FF_BM // FF_CH):
        r0 = c * FF_CH
        hb = h_ref[r0:r0 + FF_CH, :]
        ug[SUBLANES + r0:SUBLANES + r0 + FF_CH, :] = jnp.dot(hb, wg_ref[...], preferred_element_type=jnp.float32)
        uv[SUBLANES + r0:SUBLANES + r0 + FF_CH, :] = jnp.dot(hb, wv_ref[...], preferred_element_type=jnp.float32)
        gate = conv(ug, cwg_ref, cbg_ref, r0)
        val = conv(uv, cwv_ref, cbv_ref, r0)
        o_ref[r0:r0 + FF_CH, :] = (jax.nn.gelu(gate) * val).astype(o_ref.dtype)


def _ffn_up(h, w_up, layer, conv_w, conv_b):
    t, d = h.shape
    nj = FF_P // FF_BN
    w_block = (None, pl.Element(d), pl.Element(FF_BN))
    return pl.pallas_call(
        _ffn_up_kernel,
        grid=(nj, t // FF_BM),
        in_specs=[
            pl.BlockSpec((FF_BM, d), lambda j, i: (i, 0)),
            pl.BlockSpec(w_block, lambda j, i: (layer, 0, pl.multiple_of(j * FF_BN, LANES))),
            pl.BlockSpec(w_block, lambda j, i: (
                layer, 0, pl.multiple_of(jnp.minimum(D_FF + j * FF_BN, 2 * D_FF - FF_BN), LANES))),
            pl.BlockSpec((CONV_W, FF_BN), lambda j, i: (0, j)),
            pl.BlockSpec((CONV_W, FF_BN), lambda j, i: (0, nj + j)),
            pl.BlockSpec((1, FF_BN), lambda j, i: (0, j)),
            pl.BlockSpec((1, FF_BN), lambda j, i: (0, nj + j)),
        ],
        out_specs=pl.BlockSpec((FF_BM, FF_BN), lambda j, i: (i, j)),
        out_shape=jax.ShapeDtypeStruct((t, FF_P), jnp.bfloat16),
        scratch_shapes=[pltpu.VMEM((d, FF_BN), jnp.bfloat16), pltpu.VMEM((d, FF_BN), jnp.bfloat16),
                        pltpu.VMEM((FF_BM + SUBLANES, FF_BN), jnp.float32),
                        pltpu.VMEM((FF_BM + SUBLANES, FF_BN), jnp.float32)],
        compiler_params=_cparams(("arbitrary", "arbitrary")),
        name="ffn_up",
    )(h, w_up, w_up, conv_w, conv_w, conv_b, conv_b)


def _pad_heads(w, axis):
    h = w.shape[0]
    zero = jnp.zeros_like(w)
    even = jnp.concatenate([w, zero], axis=axis)
    odd = jnp.concatenate([zero, w], axis=axis)
    sel = (jnp.arange(h) % 2 == 0).reshape((h, 1, 1))
    return jnp.where(sel, even, odd)


def _toeplitz(fn, rows, cols):
    ks = np.concatenate([np.arange(0, cols), np.arange(-(rows - 1), 0)])
    w = fn(ks)
    h, period = w.shape
    x = jnp.tile(w, (1, rows))[:, :rows * (period - 1)].reshape(h, rows, period - 1)
    return x[:, :, :cols].astype(jnp.float32)


def _band(rows, cols, left):
    diff = left + np.arange(rows)[:, None] // CHUNK - np.arange(cols)[None, :] // CHUNK
    return (diff >= 0) & (diff <= left)


def _bias_a(rel_bias):
    fn = lambda ks: rel_bias[np.clip(A_PAD - ks, -A_MAX_REL, A_MAX_REL) + A_MAX_REL].T
    bias = jnp.where(_band(QB, A_WIN, A_LEFT_CHUNKS)[None], _toeplitz(fn, QB, A_WIN) * LOG2E, NEG)
    return bias.reshape(A_HEADS * QB, A_WIN)


def _bias_c(t5_c):
    fn = lambda ks: t5_c[_t5_bucket(jnp.asarray(ks - C_PAD, jnp.int32))].T
    bias = jnp.where(_band(QB, C_WIN, C_LEFT_CHUNKS)[None], _toeplitz(fn, QB, C_WIN) * LOG2E, NEG)
    return bias.reshape(C_Q_HEADS * QB, C_WIN)


def _bias_b(t5_b):
    tiles = []
    for n in range(B_NEAR + 1):
        off = KB * (n - (B_QK - 1)) if n < B_NEAR else SEQ
        fn = lambda ks, off=off: t5_b[_t5_bucket(jnp.asarray(-ks - off, jnp.int32))].T
        tile = _toeplitz(fn, KB, BQ)
        tiles.append(jnp.transpose(tile * LOG2E, (1, 0, 2)).reshape(KB, HQ))
    return jnp.stack(tiles)


def _pad_ff(a, dtype):
    z = jnp.zeros((a.shape[0], FF_P - D_FF), dtype)
    return jnp.concatenate([a[:, :D_FF].astype(dtype), z, a[:, D_FF:].astype(dtype), z], axis=1)


def kernel(x, mem, t5_table, norm_gains, w_in, a_rel_bias, ckv_gain, w_uk, w_uv, sinks, w_branch, w_o,
           mem_gain, w_mq, w_mkv, w_mo, w_up, conv_w, conv_b, w_down):
    bf16 = jnp.bfloat16
    xs = x.reshape(TOKENS, D_MODEL)
    mem2 = mem.reshape(BATCH * MEM_LEN, D_MODEL)
    tri = jnp.asarray(np.tril(np.ones((KB, KB), np.float32), -1), bf16)
    bias_b = _bias_b(t5_table[:, :B_HEADS])
    bias_c = _bias_c(t5_table[:, B_HEADS:])
    gains = norm_gains.reshape(DEPTH, 6, 1, D_MODEL)
    w_o_b, w_mq_b, w_mo_b = (w.astype(bf16) for w in (w_o, w_mq, w_mo))
    w_dn_b = jnp.concatenate([w_down.astype(bf16), jnp.zeros((DEPTH, FF_P - D_FF, D_MODEL), bf16)], axis=1)
    w_in_t = jnp.swapaxes(w_in, 1, 2)

    h = _norm(xs, gains[0, 0])
    for l in range(DEPTH):
        g = gains[l]
        proja = _proj(h, w_in_t, l, 0, IN_A, 1024, gate=False, name="in_proj_a")
        projc = _proj(h, w_in_t, l, O_QC, C_COLS, C_COLS, gate=False, name="in_proj_c")
        gates = _proj(h, w_in_t, l, O_GL, 3 * D_MODEL, 1024, gate=True, name="in_proj_g")
        proja3 = proja.reshape(BATCH, SEQ, IN_A)
        oa = _attn_a(proja3, _bias_a(a_rel_bias[l]))
        wuk = jnp.transpose(_pad_heads(w_uk[l], axis=2), (0, 2, 1)).astype(bf16)
        wuv = _pad_heads(w_uv[l], axis=2).astype(bf16)
        ob = _attn_b(proja3, ckv_gain[l].reshape(1, B_LATENT), wuk, wuv, bias_b, tri)
        oc = _attn_c(sinks[l], projc.reshape(BATCH, SEQ, C_COLS), bias_c)
        merged = _merge(oa.reshape(TOKENS, A_W), ob.reshape(TOKENS, B_W), oc.reshape(TOKENS, C_W), gates, w_branch, l)
        kv = _memkv(mem2, mem_gain[l].reshape(1, D_MODEL), w_mkv, l)
        xs, h = _mix_out_xattn(merged, w_o_b, w_mq_b, kv, w_mo_b, l, xs, g[1], g[2], g[3], g[4])
        hidden = _ffn_up(h, w_up, l, _pad_ff(conv_w[l], jnp.float32), _pad_ff(conv_b[l].reshape(1, -1), jnp.float32))
        g_next = gains[l + 1, 0] if l + 1 < DEPTH else None
        xs, h = _gemm_res(hidden, w_dn_b, l, xs, g[5], g_next, bk=DOWN_BK, name="ffn_down")
    return xs.reshape(BATCH, SEQ, D_MODEL)
```

```python
import functools
import math

import numpy as np
import jax
import jax.numpy as jnp
from jax import lax
from jax.experimental import pallas as pl
from jax.experimental.pallas import tpu as pltpu

D_MODEL = 2048
BATCH = 4
SEQ = 2048
DEPTH = 2
TOKENS = BATCH * SEQ
CHUNK = 64
EPS = 1e-6
NEG = -1e30
LOG2E = math.log2(math.e)
A_HEADS = 8
A_LEFT_CHUNKS = 8
A_MAX_REL = 128
A_W = 512
B_HEADS = 8
B_W = 512
B_LATENT = 128
IDX_HEADS = 8
IDX_DIM = 64
TOPK = 256
C_Q_HEADS = 16
C_GROUP = 8
C_W = 1024
C_LEFT_CHUNKS = 2
T5_BUCKETS = 32
T5_MAX_DIST = 256
MEM_LEN = 256
MEM_HEADS = 4
MEM_HEAD_DIM = 128
MEM_W = 512
D_FF = 5504
CONV_W = 3

LANES = 128
SUBLANES = 8
HEAD_DIM = 64
QB = 128
KB = 256
VMEM_LIMIT = 56 * 1024 * 1024

O_QA, O_KA, O_VA, O_QB = 0, 512, 1024, 1536
O_CKV = 2048
O_QI = 2176
O_KI = 2688
O_WI = 2752
O_QC = 2760
O_GL = 4040
IN_W = O_GL + 3 * D_MODEL
IN_A = 3072
X_COL0 = 2048
C_COLS = O_GL - O_QC
PROJ_BM = 1024
PROJ_TR = 256

FF_P = 5632
FF_BN = 512
FF_BM = 1024
FF_CH = 256
DOWN_BK = 1408

A_WIN = (A_LEFT_CHUNKS + 2) * CHUNK
C_WIN = (C_LEFT_CHUNKS + 2) * CHUNK
A_PAD = A_LEFT_CHUNKS * CHUNK
C_PAD = C_LEFT_CHUNKS * CHUNK
BQ = 256
B_QK = BQ // KB
B_NEAR = B_QK + 1
NKB = SEQ // KB
HQ = B_HEADS * BQ

_NT = (((1,), (1,)), ((), ()))
_TN = (((0,), (0,)), ((), ()))


def _cparams(sem):
    return pltpu.CompilerParams(dimension_semantics=sem, vmem_limit_bytes=VMEM_LIMIT)


def _t5_bucket(rel):
    half = T5_BUCKETS // 2
    max_exact = half // 2
    sign = jnp.where(rel > 0, half, 0)
    d = jnp.abs(rel)
    d_f = jnp.maximum(d, 1).astype(jnp.float32)
    large = max_exact + (jnp.log(d_f / max_exact) / math.log(T5_MAX_DIST / max_exact) * (half - max_exact)).astype(jnp.int32)
    large = jnp.minimum(large, half - 1)
    return sign + jnp.where(d < max_exact, d, large)


def _far_bucket_is_constant():
    assert BQ % KB == 0
    d = np.arange(KB + 1, SEQ, dtype=np.float32)
    assert d[0] > T5_MAX_DIST
    large = 8 + (np.log(d / 8) / math.log(T5_MAX_DIST / 8) * 8).astype(np.int32)
    return bool(np.all(np.minimum(large, 15) == 15))


assert _far_bucket_is_constant()


def _rms(v, g):
    return v * lax.rsqrt(jnp.mean(v * v, axis=-1, keepdims=True) + EPS) * g


def _norm_kernel(x_ref, g_ref, o_ref):
    o_ref[...] = _rms(x_ref[...], g_ref[...]).astype(o_ref.dtype)


def _norm(x, g, bm=1024):
    t, d = x.shape
    return pl.pallas_call(
        _norm_kernel,
        grid=(t // bm,),
        in_specs=[pl.BlockSpec((bm, d), lambda i: (i, 0)), pl.BlockSpec((1, d), lambda i: (0, 0))],
        out_specs=pl.BlockSpec((bm, d), lambda i: (i, 0)),
        out_shape=jax.ShapeDtypeStruct((t, d), jnp.bfloat16),
        compiler_params=_cparams(("parallel",)),
        name="rmsnorm",
    )(x, g)


def _proj_kernel(gate, h_ref, wt_ref, o_ref, wb):
    @pl.when(pl.program_id(1) == 0)
    def _():
        for c in range(wt_ref.shape[0] // PROJ_TR):
            rows = slice(c * PROJ_TR, (c + 1) * PROJ_TR)
            wb[:, rows] = wt_ref[rows, :].T.astype(wb.dtype)

    acc = jnp.dot(h_ref[...], wb[...], preferred_element_type=jnp.float32)
    o_ref[...] = (jax.nn.sigmoid(acc) if gate else acc).astype(o_ref.dtype)


def _proj(h, w_t, layer, col0, n, bn, gate, name):
    t, k = h.shape
    assert col0 % SUBLANES == 0 and bn % SUBLANES == 0
    w_spec = pl.BlockSpec((None, pl.Element(bn), pl.Element(k)),
                          lambda j, i: (layer, pl.multiple_of(col0 + j * bn, SUBLANES), 0))
    return pl.pallas_call(
        functools.partial(_proj_kernel, gate),
        grid=(n // bn, t // PROJ_BM),
        in_specs=[pl.BlockSpec((PROJ_BM, k), lambda j, i: (i, 0)), w_spec],
        out_specs=pl.BlockSpec((PROJ_BM, bn), lambda j, i: (i, j)),
        out_shape=jax.ShapeDtypeStruct((t, n), jnp.bfloat16),
        scratch_shapes=[pltpu.VMEM((k, bn), jnp.bfloat16)],
        compiler_params=_cparams(("arbitrary", "arbitrary")),
        name=name,
    )(h, w_t)


def _lane_masks(dtype):
    lane = lax.broadcasted_iota(jnp.int32, (1, LANES), 1)
    lo = (lane < HEAD_DIM).astype(jnp.float32)
    return lo.astype(dtype), (1.0 - lo).astype(dtype)


def _swap_halves(x):
    return pltpu.roll(x.astype(jnp.float32), HEAD_DIM, 1).astype(x.dtype)


def _fill_padded(src, dst_ref, pad):
    dst_ref[0:pad, :] = jnp.zeros((pad, dst_ref.shape[1]), dst_ref.dtype)
    dst_ref[pad:pad + SEQ, :] = src


def _attn_a_kernel(q_ref, k_ref, v_ref, bias_ref, o_ref, kpad, vpad):
    i = pl.program_id(1)

    @pl.when(i == 0)
    def _():
        _fill_padded(k_ref[...], kpad, A_PAD)
        _fill_padded(v_ref[...], vpad, A_PAD)

    start = pl.multiple_of(i * QB, QB)
    kw = kpad[pl.ds(start, A_WIN), :]
    vw = vpad[pl.ds(start, A_WIN), :]
    mlo, mhi = _lane_masks(jnp.bfloat16)
    lane = lax.broadcasted_iota(jnp.int32, (QB, LANES), 1)
    kchunk = lax.broadcasted_iota(jnp.int32, (QB, A_WIN), 1) // CHUNK
    pad_mask = jnp.where(kchunk >= A_LEFT_CHUNKS - 2 * i, 0.0, NEG)
    scale = HEAD_DIM ** -0.5 * LOG2E
    pad2 = jnp.concatenate([pad_mask, pad_mask], axis=0)
    npair = A_HEADS // 2
    pcols = [slice(p * LANES, (p + 1) * LANES) for p in range(npair)]
    ss = []
    for p in range(npair):
        qp = q_ref[:, pcols[p]]
        q2 = jnp.concatenate([qp * mlo, qp * mhi], axis=0)
        s = lax.dot_general(q2, kw[:, pcols[p]], _NT, preferred_element_type=jnp.float32)
        ss.append(s * scale + bias_ref[2 * p * QB:(2 * p + 2) * QB, :] + pad2)
    ms = [jnp.max(s, axis=-1, keepdims=True) for s in ss]
    es = [jnp.exp2(s - m) for s, m in zip(ss, ms)]
    ls = [jnp.sum(e, axis=-1, keepdims=True) for e in es]
    outs = [jnp.dot(e.astype(jnp.bfloat16), vw[:, pcols[p]], preferred_element_type=jnp.float32)
            for p, e in enumerate(es)]
    for p in range(npair):
        o = outs[p] / ls[p]
        o_ref[:, pcols[p]] = jnp.where(lane < HEAD_DIM, o[:QB], o[QB:]).astype(o_ref.dtype)


def _attn_a(proj3, bias):
    nq = SEQ // QB
    return pl.pallas_call(
        _attn_a_kernel,
        grid=(BATCH, nq),
        in_specs=[
            pl.BlockSpec((None, QB, A_W), lambda b, i: (b, i, O_QA // A_W)),
            pl.BlockSpec((None, SEQ, A_W), lambda b, i: (b, 0, O_KA // A_W)),
            pl.BlockSpec((None, SEQ, A_W), lambda b, i: (b, 0, O_VA // A_W)),
            pl.BlockSpec((A_HEADS * QB, A_WIN), lambda b, i: (0, 0)),
        ],
        out_specs=pl.BlockSpec((None, QB, A_W), lambda b, i: (b, i, 0)),
        out_shape=jax.ShapeDtypeStruct((BATCH, SEQ, A_W), jnp.bfloat16),
        scratch_shapes=[pltpu.VMEM((SEQ + A_PAD, A_W), jnp.bfloat16), pltpu.VMEM((SEQ + A_PAD, A_W), jnp.bfloat16)],
        compiler_params=_cparams(("arbitrary", "arbitrary")),
        name="attn_a",
    )(proj3, proj3, proj3, bias)


def _attn_c_kernel(sink_ref, q_ref, k_ref, v_ref, bias_ref, o_ref, kpad, kswp, vpad, vswp):
    i = pl.program_id(1)

    @pl.when(i == 0)
    def _():
        k = k_ref[...]
        v = v_ref[...]
        _fill_padded(k, kpad, C_PAD)
        _fill_padded(_swap_halves(k), kswp, C_PAD)
        _fill_padded(v, vpad, C_PAD)
        _fill_padded(_swap_halves(v), vswp, C_PAD)

    start = pl.multiple_of(i * QB, QB)
    mlo, mhi = _lane_masks(jnp.bfloat16)
    lane = lax.broadcasted_iota(jnp.int32, (QB, LANES), 1)
    kchunk = lax.broadcasted_iota(jnp.int32, (QB, C_WIN), 1) // CHUNK
    pad_mask = jnp.where(kchunk >= C_LEFT_CHUNKS - 2 * i, 0.0, NEG)
    scale = HEAD_DIM ** -0.5 * LOG2E
    qs = [q_ref[:, p * LANES:(p + 1) * LANES] for p in range(C_GROUP)]
    npair = C_GROUP // 2
    stacks = []
    for straight in (True, False):
        kref, vref = (kpad, vpad) if straight else (kswp, vswp)
        kw = kref[pl.ds(start, C_WIN), :]
        vw = vref[pl.ds(start, C_WIN), :]
        halves = [int((p >= npair) == straight) for p in range(C_GROUP)]
        qg = jnp.concatenate([qs[p] * (mhi if halves[p] else mlo) for p in range(C_GROUP)], axis=0)
        s_all = lax.dot_general(qg, kw, _NT, preferred_element_type=jnp.float32)
        ps, ls = [], []
        for p in range(C_GROUP):
            h = 2 * p + halves[p]
            s = s_all[p * QB:(p + 1) * QB] * scale + bias_ref[h * QB:(h + 1) * QB, :] + pad_mask
            sink = sink_ref[h] * LOG2E
            m = jnp.maximum(jnp.max(s, axis=-1, keepdims=True), sink)
            e = jnp.exp2(s - m)
            ls.append(jnp.sum(e, axis=-1, keepdims=True) + jnp.exp2(sink - m))
            ps.append(e.astype(jnp.bfloat16))
        o_all = jnp.dot(jnp.concatenate(ps, axis=0), vw, preferred_element_type=jnp.float32)
        stacks.append(([o_all[p * QB:(p + 1) * QB] / ls[p] for p in range(C_GROUP)], halves))
    for p in range(C_GROUP):
        (o1, h1), (o2, _) = stacks
        lo, hi = (o2[p], o1[p]) if h1[p] else (o1[p], o2[p])
        o_ref[:, p * LANES:(p + 1) * LANES] = jnp.where(lane < HEAD_DIM, lo, hi).astype(o_ref.dtype)


def _attn_c(sinks, projc3, bias):
    nq = SEQ // QB
    pad_buf = pltpu.VMEM((SEQ + C_PAD, LANES), jnp.bfloat16)
    return pl.pallas_call(
        _attn_c_kernel,
        grid=(BATCH, nq),
        in_specs=[
            pl.BlockSpec(memory_space=pltpu.SMEM),
            pl.BlockSpec((None, QB, C_W), lambda b, i: (b, i, 0)),
            pl.BlockSpec((None, SEQ, LANES), lambda b, i: (b, 0, C_W // LANES)),
            pl.BlockSpec((None, SEQ, LANES), lambda b, i: (b, 0, C_W // LANES + 1)),
            pl.BlockSpec((C_Q_HEADS * QB, C_WIN), lambda b, i: (0, 0)),
        ],
        out_specs=pl.BlockSpec((None, QB, C_W), lambda b, i: (b, i, 0)),
        out_shape=jax.ShapeDtypeStruct((BATCH, SEQ, C_W), jnp.bfloat16),
        scratch_shapes=[pad_buf, pad_buf, pad_buf, pad_buf],
        compiler_params=_cparams(("arbitrary", "arbitrary")),
        name="attn_c",
    )(sinks, projc3, projc3, projc3, bias)


def _sort_key(x):
    bits = lax.bitcast_convert_type(x + 0.0, jnp.int32)
    return bits ^ ((bits >> 31) & jnp.int32(0x7FFFFFFF))


def _rows_tree(x, op, slab):
    parts = [x[r:r + slab, :] for r in range(0, x.shape[0], slab)]
    while len(parts) > 1:
        nxt = [op(parts[k], parts[k + 1]) for k in range(0, len(parts) - 1, 2)]
        parts = nxt + ([parts[-1]] if len(parts) % 2 else [])
    return parts[0]


def _rows8(x, op):
    return _rows_tree(x, op, SUBLANES)


BF16_ROWS = 16


def _bit_search(count_ge, target, v, above, top_bit):
    def step(t, carry):
        v, above = carry
        cand = v | (jnp.int32(1) << (top_bit - t))
        c = count_ge(cand)
        keep = c >= target
        return jnp.where(keep, cand, v), jnp.where(keep, above, c)

    return lax.fori_loop(0, top_bit + 1, step, (v, above))


def _attn_b_kernel(qb_ref, x_ref, kiw_ref, ckv_ref, gain_ref, wuk_ref, wuv_ref, bias_ref, tri_ref,
                   o_ref, ckvn, kd, ql, qi_all, sk, dg, acc_s, p_s):
    i = pl.program_id(1)
    nkb = (i + 1) * B_QK
    mlo, mhi = _lane_masks(jnp.bfloat16)
    f32 = jnp.float32

    @pl.when(i == 0)
    def _():
        ckvn[...] = _rms(ckv_ref[...].astype(f32), gain_ref[...]).astype(ckvn.dtype)
        kiw = kiw_ref[...].astype(f32)
        lane = lax.broadcasted_iota(jnp.int32, kiw.shape, 1)
        kd[...] = jnp.where(lane < HEAD_DIM, kiw, pltpu.roll(kiw, HEAD_DIM, 1)).astype(kd.dtype)

    for h in range(B_HEADS):
        rows = slice(h * BQ, (h + 1) * BQ)
        qlat = jnp.dot(qb_ref[:, (h // 2) * LANES:(h // 2 + 1) * LANES], wuk_ref[h], preferred_element_type=f32)
        ql[rows, :] = (qlat * (HEAD_DIM ** -0.5 * LOG2E)).astype(ql.dtype)
        qcol = O_QI - X_COL0 + (h // 2) * LANES
        qi_all[rows, :] = x_ref[:, qcol:qcol + LANES] * (mhi if h % 2 else mlo)
    wcol = O_KI - X_COL0
    wi_t = x_ref[:, wcol:wcol + LANES].astype(f32).T * (IDX_HEADS ** -0.5 * IDX_DIM ** -0.5)
    wi_rows = [wi_t[O_WI - O_KI + h:O_WI - O_KI + h + 1, :] for h in range(IDX_HEADS)]

    kpos = lax.broadcasted_iota(jnp.int32, (KB, BQ), 0)
    key_limit = ((i * BQ + lax.broadcasted_iota(jnp.int32, (1, BQ), 1)) // CHUNK + 1) * CHUNK

    def admissible(kb):
        return kb * KB + kpos < key_limit

    def score_block(kb, carry):
        kblk = kd[pl.ds(pl.multiple_of(kb * KB, KB), KB), :]
        dots = lax.dot_general(kblk, qi_all[...], _NT, preferred_element_type=f32)
        score = jnp.zeros((KB, BQ), f32)
        for h in range(IDX_HEADS):
            score = score + jnp.maximum(dots[:, h * BQ:(h + 1) * BQ], 0.0) * wi_rows[h]
        sk[kb] = _sort_key(jnp.where(admissible(kb), score, NEG))
        return carry

    lax.fori_loop(0, nkb, score_block, 0)

    neg_key = _sort_key(jnp.full((1, 1), NEG, f32))
    n_rest = ((NKB - nkb) * KB).astype(f32)
    one, zero = jnp.ones((), jnp.bfloat16), jnp.zeros((), jnp.bfloat16)

    def byte_of(key, byte):
        return ((key >> 24) + 128) if byte == 3 else ((key >> (8 * byte)) & 255)

    target = jnp.full((1, BQ), TOPK, f32)
    prefix = jnp.zeros((1, BQ), jnp.int32)
    above = jnp.zeros((1, BQ), f32)
    for byte in (3, 2, 1, 0):
        def in_class(key, byte=byte, prefix=prefix):
            return (key >> (8 * byte + 8)) == prefix

        def prepare(kb, carry, byte=byte, in_class=in_class):
            key = sk[kb]
            digit = byte_of(key, byte).astype(f32)
            if byte < 3:
                digit = jnp.where(in_class(key), digit, -1.0)
            dg[kb] = digit.astype(dg.dtype)
            return carry

        lax.fori_loop(0, nkb, prepare, 0)
        rest_digit = byte_of(neg_key, byte)
        rest_on = in_class(neg_key) if byte < 3 else (neg_key == neg_key)

        def count_ge(cand, rest_digit=rest_digit, rest_on=rest_on):
            cand_b = cand.astype(f32).astype(jnp.bfloat16)

            def body(kb, acc):
                hit = jnp.where(dg[kb] >= cand_b, one, zero)
                return acc + _rows_tree(hit, jnp.add, BF16_ROWS).astype(f32)

            acc = lax.fori_loop(0, nkb, body, jnp.zeros((BF16_ROWS, BQ), f32))
            rest = jnp.where(rest_on & (rest_digit >= cand), n_rest, 0.0)
            return jnp.sum(acc, axis=0, keepdims=True) + rest

        digit_thr, above_here = _bit_search(count_ge, target, jnp.zeros((1, BQ), jnp.int32),
                                            jnp.zeros((1, BQ), f32), 7)
        prefix = (digit_thr - 128) if byte == 3 else (prefix * 256 + digit_thr)
        above = above + above_here
        target = target - above_here
    thr = prefix
    need = TOPK - above

    acc_s[...] = jnp.zeros(acc_s.shape, f32)

    far_bias = bias_ref[B_NEAR, 0:1, :]

    def attend(far, kb, carry):
        eq_seen, m_old, l_old = carry
        keys = sk[kb]
        eq = keys == thr
        eq_f = jnp.where(eq, 1.0, 0.0)
        before = jnp.dot(tri_ref[...], eq_f.astype(jnp.bfloat16), preferred_element_type=f32)
        take_eq = jnp.where(eq, jnp.where(eq_seen + before < need, 1.0, 0.0), 0.0)
        take = jnp.where(keys > thr, 1.0, take_eq)
        mask_add = jnp.where(admissible(kb), jnp.where(take > 0.0, 0.0, NEG), NEG)
        cblk = ckvn[pl.ds(pl.multiple_of(kb * KB, KB), KB), :]
        dots = lax.dot_general(cblk, ql[...], _NT, preferred_element_type=f32)
        tile = nkb - 1 - kb
        m_parts, l_parts, a_parts = [], [], []
        for g in range(HQ // LANES):
            cols = slice(g * LANES, (g + 1) * LANES)
            qcols = slice((g % (BQ // LANES)) * LANES, (g % (BQ // LANES) + 1) * LANES)
            mo = m_old[:, cols]
            if far:
                s = dots[:, cols] + mask_add[:, qcols]
                shift = far_bias[:, cols]
                mn = jnp.maximum(mo, jnp.max(_rows8(s, jnp.maximum), axis=0, keepdims=True) + shift)
                e = jnp.exp2(s - (mn - shift))
            else:
                s = dots[:, cols] + bias_ref[tile, :, cols] + mask_add[:, qcols]
                mn = jnp.maximum(mo, jnp.max(_rows8(s, jnp.maximum), axis=0, keepdims=True))
                e = jnp.exp2(s - mn)
            a = jnp.exp2(mo - mn)
            l_parts.append(a * l_old[:, cols] + jnp.sum(_rows8(e, jnp.add), axis=0, keepdims=True))
            m_parts.append(mn)
            a_parts.append(a)
            p_s[:, cols] = e.astype(p_s.dtype)
        alpha = jnp.concatenate(a_parts, axis=1)
        acc_s[...] = acc_s[...] * alpha + lax.dot_general(cblk, p_s[...], _TN, preferred_element_type=f32)
        eq_seen = eq_seen + jnp.sum(_rows8(eq_f, jnp.add), axis=0, keepdims=True)
        return eq_seen, jnp.concatenate(m_parts, axis=1), jnp.concatenate(l_parts, axis=1)

    init = (jnp.zeros((1, BQ), f32), jnp.full((1, HQ), 4 * NEG, f32), jnp.zeros((1, HQ), f32))
    n_far = jnp.maximum(nkb - B_NEAR, 0)
    carry = lax.fori_loop(0, n_far, functools.partial(attend, True), init)
    _, _, l_fin = lax.fori_loop(n_far, nkb, functools.partial(attend, False), carry)

    o_lat_t = acc_s[...] / l_fin
    for p in range(B_HEADS // 2):
        out = jnp.zeros((BQ, LANES), f32)
        for h in (2 * p, 2 * p + 1):
            o_lat = o_lat_t[:, h * BQ:(h + 1) * BQ].T.astype(jnp.bfloat16)
            out = out + jnp.dot(o_lat, wuv_ref[h], preferred_element_type=f32)
        o_ref[:, p * LANES:(p + 1) * LANES] = out.astype(o_ref.dtype)


def _attn_b(proj3, gain, wuk, wuv, bias, tri, nbatch=BATCH):
    nq = SEQ // BQ
    xw = IN_A - X_COL0
    return pl.pallas_call(
        _attn_b_kernel,
        grid=(nbatch, nq),
        in_specs=[
            pl.BlockSpec((None, BQ, B_W), lambda b, i: (b, i, O_QB // B_W)),
            pl.BlockSpec((None, BQ, xw), lambda b, i: (b, i, X_COL0 // xw)),
            pl.BlockSpec((None, SEQ, LANES), lambda b, i: (b, 0, O_KI // LANES)),
            pl.BlockSpec((None, SEQ, LANES), lambda b, i: (b, 0, O_CKV // LANES)),
            pl.BlockSpec((1, B_LATENT), lambda b, i: (0, 0)),
            pl.BlockSpec((B_HEADS, LANES, B_LATENT), lambda b, i: (0, 0, 0)),
            pl.BlockSpec((B_HEADS, B_LATENT, LANES), lambda b, i: (0, 0, 0)),
            pl.BlockSpec((B_NEAR + 1, KB, HQ), lambda b, i: (0, 0, 0), pipeline_mode=pl.Buffered(1)),
            pl.BlockSpec((KB, KB), lambda b, i: (0, 0)),
        ],
        out_specs=pl.BlockSpec((None, BQ, B_W), lambda b, i: (b, i, 0)),
        out_shape=jax.ShapeDtypeStruct((nbatch, SEQ, B_W), jnp.bfloat16),
        scratch_shapes=[
            pltpu.VMEM((SEQ, B_LATENT), jnp.bfloat16),
            pltpu.VMEM((SEQ, LANES), jnp.bfloat16),
            pltpu.VMEM((HQ, B_LATENT), jnp.bfloat16),
            pltpu.VMEM((HQ, LANES), jnp.bfloat16),
            pltpu.VMEM((NKB, KB, BQ), jnp.int32),
            pltpu.VMEM((NKB, KB, BQ), jnp.bfloat16),
            pltpu.VMEM((B_LATENT, HQ), jnp.float32),
            pltpu.VMEM((KB, HQ), jnp.bfloat16),
        ],
        compiler_params=_cparams(("arbitrary", "arbitrary")),
        name="attn_b",
    )(proj3, proj3, proj3, proj3, gain, wuk, wuv, bias, tri)


MERGE_BN = 1024
MERGE_BM = 1024


def _merge_kernel(oa_ref, ob_ref, oc_ref, ga_ref, gb_ref, gc_ref, wa_ref, wb_ref, wc_ref,
                  wo_in, wq_in, wmo_in, o_ref, wo_out, wq_out, wmo_out, wa, wb, wc):
    f32 = jnp.float32

    @pl.when(pl.program_id(1) == 0)
    def _():
        wa[...] = wa_ref[...].astype(wa.dtype)
        wb[...] = wb_ref[...].astype(wb.dtype)
        wc[...] = wc_ref[...].astype(wc.dtype)

    for src, dst in ((wo_in, wo_out), (wq_in, wq_out), (wmo_in, wmo_out)):
        dst[...] = src[...].astype(dst.dtype)

    m = ga_ref[...].astype(f32) * jnp.dot(oa_ref[...], wa[...], preferred_element_type=f32)
    m = m + gb_ref[...].astype(f32) * jnp.dot(ob_ref[...], wb[...], preferred_element_type=f32)
    m = m + gc_ref[...].astype(f32) * jnp.dot(oc_ref[...], wc[...], preferred_element_type=f32)
    o_ref[...] = m.astype(o_ref.dtype)


def _merge(oa, ob, oc, gates, w_branch, layer, next_weights):
    t = oa.shape[0]
    bn, bm = MERGE_BN, MERGE_BM
    gstep = D_MODEL // bn
    ni = t // bm
    nsteps = (D_MODEL // bn) * ni
    slabs = [w.shape[1] // nsteps for w in next_weights]
    assert all(s * nsteps == w.shape[1] and s % BF16_ROWS == 0 for s, w in zip(slabs, next_weights))
    side_in = [pl.BlockSpec((None, s, w.shape[2]), lambda j, i: (layer, j * ni + i, 0))
               for s, w in zip(slabs, next_weights)]
    side_out = [pl.BlockSpec((s, w.shape[2]), lambda j, i: (j * ni + i, 0)) for s, w in zip(slabs, next_weights)]
    res = pl.pallas_call(
        _merge_kernel,
        grid=(D_MODEL // bn, ni),
        in_specs=[
            pl.BlockSpec((bm, A_W), lambda j, i: (i, 0)),
            pl.BlockSpec((bm, B_W), lambda j, i: (i, 0)),
            pl.BlockSpec((bm, C_W), lambda j, i: (i, 0)),
            pl.BlockSpec((bm, bn), lambda j, i: (i, j)),
            pl.BlockSpec((bm, bn), lambda j, i: (i, gstep + j)),
            pl.BlockSpec((bm, bn), lambda j, i: (i, 2 * gstep + j)),
            pl.BlockSpec((None, A_W, bn), lambda j, i: (layer, 0, j)),
            pl.BlockSpec((None, B_W, bn), lambda j, i: (layer, A_W // B_W, j)),
            pl.BlockSpec((None, C_W, bn), lambda j, i: (layer, (A_W + B_W) // C_W, j)),
        ] + side_in,
        out_specs=[pl.BlockSpec((bm, bn), lambda j, i: (i, j))] + side_out,
        out_shape=[jax.ShapeDtypeStruct((t, D_MODEL), jnp.bfloat16)]
        + [jax.ShapeDtypeStruct(w.shape[1:], jnp.bfloat16) for w in next_weights],
        scratch_shapes=[pltpu.VMEM((A_W, bn), jnp.bfloat16), pltpu.VMEM((B_W, bn), jnp.bfloat16),
                        pltpu.VMEM((C_W, bn), jnp.bfloat16)],
        compiler_params=_cparams(("arbitrary", "arbitrary")),
        name="merge",
    )(oa, ob, oc, gates, gates, gates, w_branch, w_branch, w_branch, *next_weights)
    return res[0], res[1:]


FINISH_CHUNKS = 4


def _finish_chunked(y_of_rows, nrows, x_ref, gp_ref, gn_ref, xo_ref, ho_ref):
    step = nrows // FINISH_CHUNKS
    for r in range(0, nrows, step):
        rows = slice(r, r + step)
        xn = x_ref[rows, :] + _rms(y_of_rows(rows), gp_ref[...])
        xo_ref[rows, :] = xn
        if ho_ref is not None:
            ho_ref[rows, :] = _rms(xn, gn_ref[...]).astype(ho_ref.dtype)


def _gemm_res_kernel(nk, with_next, a_ref, w_ref, x_ref, gp_ref, gn_ref, xo_ref, *rest):
    ho_ref = rest[0] if with_next else None
    f32 = jnp.float32
    bm = a_ref.shape[0]
    if nk == 1:
        _finish_chunked(lambda rows: jnp.dot(a_ref[rows, :], w_ref[...], preferred_element_type=f32),
                        bm, x_ref, gp_ref, gn_ref, xo_ref, ho_ref)
        return
    acc = rest[-1]
    k = pl.program_id(1)

    @pl.when(k == 0)
    def _():
        acc[...] = jnp.dot(a_ref[...], w_ref[...], preferred_element_type=f32)

    @pl.when((k > 0) & (k < nk - 1))
    def _():
        acc[...] += jnp.dot(a_ref[...], w_ref[...], preferred_element_type=f32)

    @pl.when(k == nk - 1)
    def _():
        _finish_chunked(lambda rows: acc[rows, :] + jnp.dot(a_ref[rows, :], w_ref[...], preferred_element_type=f32),
                        bm, x_ref, gp_ref, gn_ref, xo_ref, ho_ref)


def _gemm_res(a, w, layer, x, g_post, g_next, bk, bm=512, name="gemm_res"):
    t, kdim = a.shape
    n = w.shape[2]
    nk = kdim // bk
    with_next = g_next is not None
    if g_next is None:
        g_next = g_post
    out_shape = [jax.ShapeDtypeStruct((t, n), jnp.float32)]
    out_specs = [pl.BlockSpec((bm, n), lambda i, k: (i, 0))]
    if with_next:
        out_shape.append(jax.ShapeDtypeStruct((t, n), jnp.bfloat16))
        out_specs.append(pl.BlockSpec((bm, n), lambda i, k: (i, 0)))
    res = pl.pallas_call(
        functools.partial(_gemm_res_kernel, nk, with_next),
        grid=(t // bm, nk),
        in_specs=[
            pl.BlockSpec((bm, bk), lambda i, k: (i, k)),
            pl.BlockSpec((None, bk, n), lambda i, k: (layer, k, 0)),
            pl.BlockSpec((bm, n), lambda i, k: (i, 0)),
            pl.BlockSpec((1, n), lambda i, k: (0, 0)),
            pl.BlockSpec((1, n), lambda i, k: (0, 0)),
        ],
        out_specs=out_specs,
        out_shape=out_shape,
        scratch_shapes=[pltpu.VMEM((bm, n), jnp.float32)] if nk > 1 else [],
        compiler_params=_cparams(("parallel", "arbitrary")),
        name=name,
    )(a, w, x, g_post, g_next)
    return (res[0], res[1]) if with_next else (res[0], None)


def _cast_once(pairs):
    @pl.when(pl.program_id(0) == 0)
    def _():
        for src, dst in pairs:
            dst[...] = src[...].astype(dst.dtype)


def _memkv_kernel(m_ref, g_ref, w_ref, o_ref, wb):
    _cast_once([(w_ref, wb)])
    mn = _rms(m_ref[...], g_ref[...]).astype(jnp.bfloat16)
    o_ref[...] = jnp.dot(mn, wb[...], preferred_element_type=jnp.float32).astype(o_ref.dtype)


def _memkv(mem2, g, w, layer):
    t, d = mem2.shape
    n = w.shape[2]
    bm = 512
    return pl.pallas_call(
        _memkv_kernel,
        grid=(t // bm,),
        in_specs=[pl.BlockSpec((bm, d), lambda i: (i, 0)), pl.BlockSpec((1, d), lambda i: (0, 0)),
                  pl.BlockSpec((None, d, n), lambda i: (layer, 0, 0))],
        out_specs=pl.BlockSpec((bm, n), lambda i: (i, 0)),
        out_shape=jax.ShapeDtypeStruct((t, n), jnp.bfloat16),
        scratch_shapes=[pltpu.VMEM((d, n), jnp.bfloat16)],
        compiler_params=_cparams(("arbitrary",)),
        name="mem_kv",
    )(mem2, g, w)


XA_BM = 512


def _mix_out_xattn_kernel(m_ref, wo_ref, wq_ref, kv_ref, wmo_ref, x_ref, g1_ref, g2_ref, g3_ref, g4_ref,
                          xo_ref, ho_ref, x1_s, h1_s):
    f32 = jnp.float32
    bm = m_ref.shape[0]
    _finish_chunked(lambda rows: jnp.dot(m_ref[rows, :], wo_ref[...], preferred_element_type=f32),
                    bm, x_ref, g1_ref, g2_ref, x1_s, h1_s)
    q = jnp.dot(h1_s[...], wq_ref[...], preferred_element_type=f32).astype(jnp.bfloat16)
    scale = MEM_HEAD_DIM ** -0.5 * LOG2E
    hcols = [slice(h * LANES, (h + 1) * LANES) for h in range(MEM_HEADS)]
    ss = [lax.dot_general(q[:, c], kv_ref[:, c], _NT, preferred_element_type=f32) * scale for c in hcols]
    ms = [jnp.max(s, axis=-1, keepdims=True) for s in ss]
    es = [jnp.exp2(s - m) for s, m in zip(ss, ms)]
    ls = [jnp.sum(e, axis=-1, keepdims=True) for e in es]
    pv = [jnp.dot(e.astype(jnp.bfloat16), kv_ref[:, MEM_W + h * LANES:MEM_W + (h + 1) * LANES],
                  preferred_element_type=f32) for h, e in enumerate(es)]
    o = jnp.concatenate([(o_h / l).astype(jnp.bfloat16) for o_h, l in zip(pv, ls)], axis=-1)
    _finish_chunked(lambda rows: jnp.dot(o[rows, :], wmo_ref[...], preferred_element_type=f32),
                    bm, x1_s, g3_ref, g4_ref, xo_ref, ho_ref)


def _mix_out_xattn(merged, w_o, w_mq, kv, w_mo, layer, x, g1, g2, g3, g4):
    t, d = merged.shape
    bm = XA_BM
    per_batch = SEQ // bm
    once = pl.Buffered(1)
    row = pl.BlockSpec((bm, d), lambda i: (i, 0))
    gain = pl.BlockSpec((1, d), lambda i: (0, 0))
    return pl.pallas_call(
        _mix_out_xattn_kernel,
        grid=(t // bm,),
        in_specs=[
            row,
            pl.BlockSpec((None, d, d), lambda i: (layer, 0, 0), pipeline_mode=once),
            pl.BlockSpec((None, d, MEM_W), lambda i: (layer, 0, 0), pipeline_mode=once),
            pl.BlockSpec((MEM_LEN, 2 * MEM_W), lambda i: (i // per_batch, 0)),
            pl.BlockSpec((None, MEM_W, d), lambda i: (layer, 0, 0), pipeline_mode=once),
            row, gain, gain, gain, gain,
        ],
        out_specs=[row, row],
        out_shape=[jax.ShapeDtypeStruct((t, d), jnp.float32), jax.ShapeDtypeStruct((t, d), jnp.bfloat16)],
        scratch_shapes=[pltpu.VMEM((bm, d), jnp.float32), pltpu.VMEM((bm, d), jnp.bfloat16)],
        compiler_params=_cparams(("parallel",)),
        name="mix_out_xattn",
    )(merged, w_o, w_mq, kv, w_mo, x, g1, g2, g3, g4)


def _ffn_up_kernel(h_ref, wgf_ref, wvf_ref, cwg_ref, cwv_ref, cbg_ref, cbv_ref, wd_ref, o_ref, wdo_ref,
                   wg_ref, wv_ref, ug, uv):
    j = pl.program_id(0)
    i = pl.program_id(1)

    slab = j * pl.num_programs(1) + i

    @pl.when(slab < D_FF // wd_ref.shape[0])
    def _():
        wdo_ref[...] = wd_ref[...].astype(wdo_ref.dtype)

    @pl.when(slab >= D_FF // wd_ref.shape[0])
    def _():
        wdo_ref[...] = jnp.zeros(wdo_ref.shape, wdo_ref.dtype)

    tiles_per_seq = SEQ // FF_BM
    last = FF_P // FF_BN - 1
    valid = D_FF - last * FF_BN
    shift = FF_BN - valid

    @pl.when((i == 0) & (j < last))
    def _():
        wg_ref[...] = wgf_ref[...].astype(wg_ref.dtype)
        wv_ref[...] = wvf_ref[...].astype(wv_ref.dtype)

    @pl.when((i == 0) & (j == last))
    def _():
        zeros = jnp.zeros((wg_ref.shape[0], FF_BN - valid), wg_ref.dtype)
        wg_ref[:, :valid] = wgf_ref[:, :valid].astype(wg_ref.dtype)
        wv_ref[:, :valid] = wvf_ref[:, shift:].astype(wv_ref.dtype)
        wg_ref[:, valid:] = zeros
        wv_ref[:, valid:] = zeros

    for u in (ug, uv):
        @pl.when(i % tiles_per_seq == 0)
        def _():
            u[0:SUBLANES, :] = jnp.zeros((SUBLANES, FF_BN), jnp.float32)

        @pl.when(i % tiles_per_seq != 0)
        def _():
            u[0:SUBLANES, :] = u[FF_BM:FF_BM + SUBLANES, :]

    def conv(u, cw_ref, cb_ref, r0):
        base = SUBLANES + r0
        acc = cb_ref[...] + u[base - 2:base - 2 + FF_CH, :] * cw_ref[0:1, :]
        acc = acc + u[base - 1:base - 1 + FF_CH, :] * cw_ref[1:2, :]
        return acc + u[base:base + FF_CH, :] * cw_ref[2:3, :]

    for c in range(FF_BM // FF_CH):
        r0 = c * FF_CH
        hb = h_ref[r0:r0 + FF_CH, :]
        ug[SUBLANES + r0:SUBLANES + r0 + FF_CH, :] = jnp.dot(hb, wg_ref[...], preferred_element_type=jnp.float32)
        uv[SUBLANES + r0:SUBLANES + r0 + FF_CH, :] = jnp.dot(hb, wv_ref[...], preferred_element_type=jnp.float32)
        gate = conv(ug, cwg_ref, cbg_ref, r0)
        val = conv(uv, cwv_ref, cbv_ref, r0)
        o_ref[r0:r0 + FF_CH, :] = (jax.nn.gelu(gate) * val).astype(o_ref.dtype)


def _ffn_up(h, w_up, w_down, layer, conv_w, conv_b):
    t, d = h.shape
    nj = FF_P // FF_BN
    ni = t // FF_BM
    slab = FF_P // (nj * ni)
    assert slab * nj * ni == FF_P and D_FF % slab == 0 and slab % BF16_ROWS == 0
    n_real = D_FF // slab
    w_block = (None, pl.Element(d), pl.Element(FF_BN))
    return pl.pallas_call(
        _ffn_up_kernel,
        grid=(nj, ni),
        in_specs=[
            pl.BlockSpec((FF_BM, d), lambda j, i: (i, 0)),
            pl.BlockSpec(w_block, lambda j, i: (layer, 0, pl.multiple_of(j * FF_BN, LANES))),
            pl.BlockSpec(w_block, lambda j, i: (
                layer, 0, pl.multiple_of(jnp.minimum(D_FF + j * FF_BN, 2 * D_FF - FF_BN), LANES))),
            pl.BlockSpec((CONV_W, FF_BN), lambda j, i: (0, j)),
            pl.BlockSpec((CONV_W, FF_BN), lambda j, i: (0, nj + j)),
            pl.BlockSpec((1, FF_BN), lambda j, i: (0, j)),
            pl.BlockSpec((1, FF_BN), lambda j, i: (0, nj + j)),
            pl.BlockSpec((None, slab, d), lambda j, i: (layer, jnp.minimum(j * ni + i, n_real - 1), 0)),
        ],
        out_specs=[pl.BlockSpec((FF_BM, FF_BN), lambda j, i: (i, j)),
                   pl.BlockSpec((slab, d), lambda j, i: (j * ni + i, 0))],
        out_shape=[jax.ShapeDtypeStruct((t, FF_P), jnp.bfloat16),
                   jax.ShapeDtypeStruct((FF_P, w_down.shape[2]), jnp.bfloat16)],
        scratch_shapes=[pltpu.VMEM((d, FF_BN), jnp.bfloat16), pltpu.VMEM((d, FF_BN), jnp.bfloat16),
                        pltpu.VMEM((FF_BM + SUBLANES, FF_BN), jnp.float32),
                        pltpu.VMEM((FF_BM + SUBLANES, FF_BN), jnp.float32)],
        compiler_params=_cparams(("arbitrary", "arbitrary")),
        name="ffn_up",
    )(h, w_up, w_up, conv_w, conv_w, conv_b, conv_b, w_down)


def _pad_heads(w, axis):
    h = w.shape[0]
    zero = jnp.zeros_like(w)
    even = jnp.concatenate([w, zero], axis=axis)
    odd = jnp.concatenate([zero, w], axis=axis)
    sel = (jnp.arange(h) % 2 == 0).reshape((h, 1, 1))
    return jnp.where(sel, even, odd)


def _toeplitz(fn, rows, cols):
    ks = np.concatenate([np.arange(0, cols), np.arange(-(rows - 1), 0)])
    w = fn(ks)
    h, period = w.shape
    x = jnp.tile(w, (1, rows))[:, :rows * (period - 1)].reshape(h, rows, period - 1)
    return x[:, :, :cols].astype(jnp.float32)


def _band(rows, cols, left):
    diff = left + np.arange(rows)[:, None] // CHUNK - np.arange(cols)[None, :] // CHUNK
    return (diff >= 0) & (diff <= left)


def _bias_a(rel_bias):
    fn = lambda ks: rel_bias[np.clip(A_PAD - ks, -A_MAX_REL, A_MAX_REL) + A_MAX_REL].T
    bias = jnp.where(_band(QB, A_WIN, A_LEFT_CHUNKS)[None], _toeplitz(fn, QB, A_WIN) * LOG2E, NEG)
    return bias.reshape(A_HEADS * QB, A_WIN)


def _bias_c(t5_c):
    fn = lambda ks: t5_c[_t5_bucket(jnp.asarray(ks - C_PAD, jnp.int32))].T
    bias = jnp.where(_band(QB, C_WIN, C_LEFT_CHUNKS)[None], _toeplitz(fn, QB, C_WIN) * LOG2E, NEG)
    return bias.reshape(C_Q_HEADS * QB, C_WIN)


def _bias_b(t5_b):
    tiles = []
    for n in range(B_NEAR + 1):
        off = KB * (n - (B_QK - 1)) if n < B_NEAR else SEQ
        fn = lambda ks, off=off: t5_b[_t5_bucket(jnp.asarray(-ks - off, jnp.int32))].T
        tile = _toeplitz(fn, KB, BQ)
        tiles.append(jnp.transpose(tile * LOG2E, (1, 0, 2)).reshape(KB, HQ))
    return jnp.stack(tiles)


def _pad_ff(a, dtype):
    z = jnp.zeros((a.shape[0], FF_P - D_FF), dtype)
    return jnp.concatenate([a[:, :D_FF].astype(dtype), z, a[:, D_FF:].astype(dtype), z], axis=1)


def kernel(x, mem, t5_table, norm_gains, w_in, a_rel_bias, ckv_gain, w_uk, w_uv, sinks, w_branch, w_o,
           mem_gain, w_mq, w_mkv, w_mo, w_up, conv_w, conv_b, w_down):
    bf16 = jnp.bfloat16
    xs = x.reshape(TOKENS, D_MODEL)
    mem2 = mem.reshape(BATCH * MEM_LEN, D_MODEL)
    tri = jnp.asarray(np.tril(np.ones((KB, KB), np.float32), -1), bf16)
    bias_b = _bias_b(t5_table[:, :B_HEADS])
    bias_c = _bias_c(t5_table[:, B_HEADS:])
    gains = norm_gains.reshape(DEPTH, 6, 1, D_MODEL)
    w_in_t = jnp.swapaxes(w_in, 1, 2)

    h = _norm(xs, gains[0, 0])
    for l in range(DEPTH):
        g = gains[l]
        proja = _proj(h, w_in_t, l, 0, IN_A, 1024, gate=False, name="in_proj_a")
        projc = _proj(h, w_in_t, l, O_QC, C_COLS, C_COLS, gate=False, name="in_proj_c")
        gates = _proj(h, w_in_t, l, O_GL, 3 * D_MODEL, 1024, gate=True, name="in_proj_g")
        proja3 = proja.reshape(BATCH, SEQ, IN_A)
        oa = _attn_a(proja3, _bias_a(a_rel_bias[l]))
        wuk = jnp.transpose(_pad_heads(w_uk[l], axis=2), (0, 2, 1)).astype(bf16)
        wuv = _pad_heads(w_uv[l], axis=2).astype(bf16)
        ob = _attn_b(proja3, ckv_gain[l].reshape(1, B_LATENT), wuk, wuv, bias_b, tri)
        oc = _attn_c(sinks[l], projc.reshape(BATCH, SEQ, C_COLS), bias_c)
        merged, (w_o_b, w_mq_b, w_mo_b) = _merge(oa.reshape(TOKENS, A_W), ob.reshape(TOKENS, B_W),
                                                 oc.reshape(TOKENS, C_W), gates, w_branch, l, (w_o, w_mq, w_mo))
        kv = _memkv(mem2, mem_gain[l].reshape(1, D_MODEL), w_mkv, l)
        xs, h = _mix_out_xattn(merged, w_o_b[None], w_mq_b[None], kv, w_mo_b[None], 0, xs, g[1], g[2], g[3], g[4])
        hidden, w_dn_b = _ffn_up(h, w_up, w_down, l, _pad_ff(conv_w[l], jnp.float32),
                                 _pad_ff(conv_b[l].reshape(1, -1), jnp.float32))
        g_next = gains[l + 1, 0] if l + 1 < DEPTH else None
        xs, h = _gemm_res(hidden, w_dn_b[None], 0, xs, g[5], g_next, bk=DOWN_BK, name="ffn_down")
    return xs.reshape(BATCH, SEQ, D_MODEL)
```

```python
import functools
import math

import numpy as np
import jax
import jax.numpy as jnp
from jax import lax
from jax.experimental import pallas as pl
from jax.experimental.pallas import tpu as pltpu

D_MODEL = 2048
BATCH = 4
SEQ = 2048
DEPTH = 2
TOKENS = BATCH * SEQ
CHUNK = 64
EPS = 1e-6
NEG = -1e30
LOG2E = math.log2(math.e)
A_HEADS = 8
A_LEFT_CHUNKS = 8
A_MAX_REL = 128
A_W = 512
B_HEADS = 8
B_W = 512
B_LATENT = 128
IDX_HEADS = 8
IDX_DIM = 64
TOPK = 256
C_Q_HEADS = 16
C_GROUP = 8
C_W = 1024
C_LEFT_CHUNKS = 2
T5_BUCKETS = 32
T5_MAX_DIST = 256
MEM_LEN = 256
MEM_HEADS = 4
MEM_HEAD_DIM = 128
MEM_W = 512
D_FF = 5504
CONV_W = 3

LANES = 128
SUBLANES = 8
HEAD_DIM = 64
QB = 128
KB = 256
VMEM_LIMIT = 56 * 1024 * 1024

O_QA, O_KA, O_VA, O_QB = 0, 512, 1024, 1536
O_CKV = 2048
O_QI = 2176
O_KI = 2688
O_WI = 2752
O_QC = 2760
O_GL = 4040
IN_W = O_GL + 3 * D_MODEL
IN_A = 3072
X_COL0 = 2048
C_COLS = O_GL - O_QC
PROJ_BM = 1024
PROJ_TR = 256

FF_P = 5632
FF_BN = 512
FF_BM = 2048
FF_CH = 256
DOWN_BK = 1408

A_WIN = (A_LEFT_CHUNKS + 2) * CHUNK
C_WIN = (C_LEFT_CHUNKS + 2) * CHUNK
A_PAD = A_LEFT_CHUNKS * CHUNK
C_PAD = C_LEFT_CHUNKS * CHUNK
BQ = 256
B_QK = BQ // KB
B_NEAR = B_QK + 1
NKB = SEQ // KB
HQ = B_HEADS * BQ

_NT = (((1,), (1,)), ((), ()))
_TN = (((0,), (0,)), ((), ()))


def _cparams(sem):
    return pltpu.CompilerParams(dimension_semantics=sem, vmem_limit_bytes=VMEM_LIMIT)


def _t5_bucket(rel):
    half = T5_BUCKETS // 2
    max_exact = half // 2
    sign = jnp.where(rel > 0, half, 0)
    d = jnp.abs(rel)
    d_f = jnp.maximum(d, 1).astype(jnp.float32)
    large = max_exact + (jnp.log(d_f / max_exact) / math.log(T5_MAX_DIST / max_exact) * (half - max_exact)).astype(jnp.int32)
    large = jnp.minimum(large, half - 1)
    return sign + jnp.where(d < max_exact, d, large)


def _far_bucket_is_constant():
    assert BQ % KB == 0
    d = np.arange(KB + 1, SEQ, dtype=np.float32)
    assert d[0] > T5_MAX_DIST
    large = 8 + (np.log(d / 8) / math.log(T5_MAX_DIST / 8) * 8).astype(np.int32)
    return bool(np.all(np.minimum(large, 15) == 15))


assert _far_bucket_is_constant()


def _rms(v, g):
    return v * lax.rsqrt(jnp.mean(v * v, axis=-1, keepdims=True) + EPS) * g


def _norm_kernel(x_ref, g_ref, o_ref):
    o_ref[...] = _rms(x_ref[...], g_ref[...]).astype(o_ref.dtype)


def _norm(x, g, bm=1024):
    t, d = x.shape
    return pl.pallas_call(
        _norm_kernel,
        grid=(t // bm,),
        in_specs=[pl.BlockSpec((bm, d), lambda i: (i, 0)), pl.BlockSpec((1, d), lambda i: (0, 0))],
        out_specs=pl.BlockSpec((bm, d), lambda i: (i, 0)),
        out_shape=jax.ShapeDtypeStruct((t, d), jnp.bfloat16),
        compiler_params=_cparams(("parallel",)),
        name="rmsnorm",
    )(x, g)


def _proj_kernel(gate, h_ref, wt_ref, o_ref, wb):
    @pl.when(pl.program_id(1) == 0)
    def _():
        for c in range(wt_ref.shape[0] // PROJ_TR):
            rows = slice(c * PROJ_TR, (c + 1) * PROJ_TR)
            wb[:, rows] = wt_ref[rows, :].T.astype(wb.dtype)

    acc = jnp.dot(h_ref[...], wb[...], preferred_element_type=jnp.float32)
    o_ref[...] = (jax.nn.sigmoid(acc) if gate else acc).astype(o_ref.dtype)


def _proj(h, w_t, layer, col0, n, bn, gate, name):
    t, k = h.shape
    assert col0 % SUBLANES == 0 and bn % SUBLANES == 0
    w_spec = pl.BlockSpec((None, pl.Element(bn), pl.Element(k)),
                          lambda j, i: (layer, pl.multiple_of(col0 + j * bn, SUBLANES), 0))
    return pl.pallas_call(
        functools.partial(_proj_kernel, gate),
        grid=(n // bn, t // PROJ_BM),
        in_specs=[pl.BlockSpec((PROJ_BM, k), lambda j, i: (i, 0)), w_spec],
        out_specs=pl.BlockSpec((PROJ_BM, bn), lambda j, i: (i, j)),
        out_shape=jax.ShapeDtypeStruct((t, n), jnp.bfloat16),
        scratch_shapes=[pltpu.VMEM((k, bn), jnp.bfloat16)],
        compiler_params=_cparams(("arbitrary", "arbitrary")),
        name=name,
    )(h, w_t)


def _lane_masks(dtype):
    lane = lax.broadcasted_iota(jnp.int32, (1, LANES), 1)
    lo = (lane < HEAD_DIM).astype(jnp.float32)
    return lo.astype(dtype), (1.0 - lo).astype(dtype)


def _swap_halves(x):
    return pltpu.roll(x.astype(jnp.float32), HEAD_DIM, 1).astype(x.dtype)


def _fill_padded(src, dst_ref, pad):
    dst_ref[0:pad, :] = jnp.zeros((pad, dst_ref.shape[1]), dst_ref.dtype)
    dst_ref[pad:pad + SEQ, :] = src


def _attn_a_kernel(q_ref, k_ref, v_ref, bias_ref, o_ref, kpad, vpad):
    i = pl.program_id(1)

    @pl.when(i == 0)
    def _():
        _fill_padded(k_ref[...], kpad, A_PAD)
        _fill_padded(v_ref[...], vpad, A_PAD)

    start = pl.multiple_of(i * QB, QB)
    kw = kpad[pl.ds(start, A_WIN), :]
    vw = vpad[pl.ds(start, A_WIN), :]
    mlo, mhi = _lane_masks(jnp.bfloat16)
    lane = lax.broadcasted_iota(jnp.int32, (QB, LANES), 1)
    kchunk = lax.broadcasted_iota(jnp.int32, (QB, A_WIN), 1) // CHUNK
    pad_mask = jnp.where(kchunk >= A_LEFT_CHUNKS - 2 * i, 0.0, NEG)
    scale = HEAD_DIM ** -0.5 * LOG2E
    pad2 = jnp.concatenate([pad_mask, pad_mask], axis=0)
    npair = A_HEADS // 2
    pcols = [slice(p * LANES, (p + 1) * LANES) for p in range(npair)]
    ss = []
    for p in range(npair):
        qp = q_ref[:, pcols[p]]
        q2 = jnp.concatenate([qp * mlo, qp * mhi], axis=0)
        s = lax.dot_general(q2, kw[:, pcols[p]], _NT, preferred_element_type=jnp.float32)
        ss.append(s * scale + bias_ref[2 * p * QB:(2 * p + 2) * QB, :] + pad2)
    ms = [jnp.max(s, axis=-1, keepdims=True) for s in ss]
    es = [jnp.exp2(s - m) for s, m in zip(ss, ms)]
    ls = [jnp.sum(e, axis=-1, keepdims=True) for e in es]
    outs = [jnp.dot(e.astype(jnp.bfloat16), vw[:, pcols[p]], preferred_element_type=jnp.float32)
            for p, e in enumerate(es)]
    for p in range(npair):
        o = outs[p] / ls[p]
        o_ref[:, pcols[p]] = jnp.where(lane < HEAD_DIM, o[:QB], o[QB:]).astype(o_ref.dtype)


def _attn_a(proj3, bias):
    nq = SEQ // QB
    return pl.pallas_call(
        _attn_a_kernel,
        grid=(BATCH, nq),
        in_specs=[
            pl.BlockSpec((None, QB, A_W), lambda b, i: (b, i, O_QA // A_W)),
            pl.BlockSpec((None, SEQ, A_W), lambda b, i: (b, 0, O_KA // A_W)),
            pl.BlockSpec((None, SEQ, A_W), lambda b, i: (b, 0, O_VA // A_W)),
            pl.BlockSpec((A_HEADS * QB, A_WIN), lambda b, i: (0, 0)),
        ],
        out_specs=pl.BlockSpec((None, QB, A_W), lambda b, i: (b, i, 0)),
        out_shape=jax.ShapeDtypeStruct((BATCH, SEQ, A_W), jnp.bfloat16),
        scratch_shapes=[pltpu.VMEM((SEQ + A_PAD, A_W), jnp.bfloat16), pltpu.VMEM((SEQ + A_PAD, A_W), jnp.bfloat16)],
        compiler_params=_cparams(("arbitrary", "arbitrary")),
        name="attn_a",
    )(proj3, proj3, proj3, bias)


def _attn_c_kernel(sink_ref, q_ref, k_ref, v_ref, bias_ref, o_ref, kpad, kswp, vpad, vswp):
    i = pl.program_id(1)

    @pl.when(i == 0)
    def _():
        k = k_ref[...]
        v = v_ref[...]
        _fill_padded(k, kpad, C_PAD)
        _fill_padded(_swap_halves(k), kswp, C_PAD)
        _fill_padded(v, vpad, C_PAD)
        _fill_padded(_swap_halves(v), vswp, C_PAD)

    start = pl.multiple_of(i * QB, QB)
    mlo, mhi = _lane_masks(jnp.bfloat16)
    lane = lax.broadcasted_iota(jnp.int32, (QB, LANES), 1)
    kchunk = lax.broadcasted_iota(jnp.int32, (QB, C_WIN), 1) // CHUNK
    pad_mask = jnp.where(kchunk >= C_LEFT_CHUNKS - 2 * i, 0.0, NEG)
    scale = HEAD_DIM ** -0.5 * LOG2E
    qs = [q_ref[:, p * LANES:(p + 1) * LANES] for p in range(C_GROUP)]
    npair = C_GROUP // 2
    stacks = []
    for straight in (True, False):
        kref, vref = (kpad, vpad) if straight else (kswp, vswp)
        kw = kref[pl.ds(start, C_WIN), :]
        vw = vref[pl.ds(start, C_WIN), :]
        halves = [int((p >= npair) == straight) for p in range(C_GROUP)]
        qg = jnp.concatenate([qs[p] * (mhi if halves[p] else mlo) for p in range(C_GROUP)], axis=0)
        s_all = lax.dot_general(qg, kw, _NT, preferred_element_type=jnp.float32)
        ps, ls = [], []
        for p in range(C_GROUP):
            h = 2 * p + halves[p]
            s = s_all[p * QB:(p + 1) * QB] * scale + bias_ref[h * QB:(h + 1) * QB, :] + pad_mask
            sink = sink_ref[h] * LOG2E
            m = jnp.maximum(jnp.max(s, axis=-1, keepdims=True), sink)
            e = jnp.exp2(s - m)
            ls.append(jnp.sum(e, axis=-1, keepdims=True) + jnp.exp2(sink - m))
            ps.append(e.astype(jnp.bfloat16))
        o_all = jnp.dot(jnp.concatenate(ps, axis=0), vw, preferred_element_type=jnp.float32)
        stacks.append(([o_all[p * QB:(p + 1) * QB] / ls[p] for p in range(C_GROUP)], halves))
    for p in range(C_GROUP):
        (o1, h1), (o2, _) = stacks
        lo, hi = (o2[p], o1[p]) if h1[p] else (o1[p], o2[p])
        o_ref[:, p * LANES:(p + 1) * LANES] = jnp.where(lane < HEAD_DIM, lo, hi).astype(o_ref.dtype)


def _attn_c(sinks, projc3, bias):
    nq = SEQ // QB
    pad_buf = pltpu.VMEM((SEQ + C_PAD, LANES), jnp.bfloat16)
    return pl.pallas_call(
        _attn_c_kernel,
        grid=(BATCH, nq),
        in_specs=[
            pl.BlockSpec(memory_space=pltpu.SMEM),
            pl.BlockSpec((None, QB, C_W), lambda b, i: (b, i, 0)),
            pl.BlockSpec((None, SEQ, LANES), lambda b, i: (b, 0, C_W // LANES)),
            pl.BlockSpec((None, SEQ, LANES), lambda b, i: (b, 0, C_W // LANES + 1)),
            pl.BlockSpec((C_Q_HEADS * QB, C_WIN), lambda b, i: (0, 0)),
        ],
        out_specs=pl.BlockSpec((None, QB, C_W), lambda b, i: (b, i, 0)),
        out_shape=jax.ShapeDtypeStruct((BATCH, SEQ, C_W), jnp.bfloat16),
        scratch_shapes=[pad_buf, pad_buf, pad_buf, pad_buf],
        compiler_params=_cparams(("arbitrary", "arbitrary")),
        name="attn_c",
    )(sinks, projc3, projc3, projc3, bias)


def _sort_key(x):
    bits = lax.bitcast_convert_type(x + 0.0, jnp.int32)
    return bits ^ ((bits >> 31) & jnp.int32(0x7FFFFFFF))


def _rows_tree(x, op, slab):
    parts = [x[r:r + slab, :] for r in range(0, x.shape[0], slab)]
    while len(parts) > 1:
        nxt = [op(parts[k], parts[k + 1]) for k in range(0, len(parts) - 1, 2)]
        parts = nxt + ([parts[-1]] if len(parts) % 2 else [])
    return parts[0]


def _rows8(x, op):
    return _rows_tree(x, op, SUBLANES)


BF16_ROWS = 16


def _bit_search(count_ge, target, v, above, top_bit):
    def step(t, carry):
        v, above = carry
        cand = v | (jnp.int32(1) << (top_bit - t))
        c = count_ge(cand)
        keep = c >= target
        return jnp.where(keep, cand, v), jnp.where(keep, above, c)

    return lax.fori_loop(0, top_bit + 1, step, (v, above))


def _attn_b_kernel(qb_ref, x_ref, kiw_ref, ckv_ref, gain_ref, wuk_ref, wuv_ref, bias_ref, tri_ref,
                   o_ref, ckvn, kd, ql, qi_all, sk, dg, acc_s, p_s):
    i = pl.program_id(1)
    nkb = (i + 1) * B_QK
    mlo, mhi = _lane_masks(jnp.bfloat16)
    f32 = jnp.float32

    @pl.when(i == 0)
    def _():
        ckvn[...] = _rms(ckv_ref[...].astype(f32), gain_ref[...]).astype(ckvn.dtype)
        kiw = kiw_ref[...].astype(f32)
        lane = lax.broadcasted_iota(jnp.int32, kiw.shape, 1)
        kd[...] = jnp.where(lane < HEAD_DIM, kiw, pltpu.roll(kiw, HEAD_DIM, 1)).astype(kd.dtype)

    for h in range(B_HEADS):
        rows = slice(h * BQ, (h + 1) * BQ)
        qlat = jnp.dot(qb_ref[:, (h // 2) * LANES:(h // 2 + 1) * LANES], wuk_ref[h], preferred_element_type=f32)
        ql[rows, :] = (qlat * (HEAD_DIM ** -0.5 * LOG2E)).astype(ql.dtype)
        qcol = O_QI - X_COL0 + (h // 2) * LANES
        qi_all[rows, :] = x_ref[:, qcol:qcol + LANES] * (mhi if h % 2 else mlo)
    wcol = O_KI - X_COL0
    wi_t = x_ref[:, wcol:wcol + LANES].astype(f32).T * (IDX_HEADS ** -0.5 * IDX_DIM ** -0.5)
    wi_rows = [wi_t[O_WI - O_KI + h:O_WI - O_KI + h + 1, :] for h in range(IDX_HEADS)]

    kpos = lax.broadcasted_iota(jnp.int32, (KB, BQ), 0)
    key_limit = ((i * BQ + lax.broadcasted_iota(jnp.int32, (1, BQ), 1)) // CHUNK + 1) * CHUNK

    def admissible(kb):
        return kb * KB + kpos < key_limit

    def score_block(kb, carry):
        kblk = kd[pl.ds(pl.multiple_of(kb * KB, KB), KB), :]
        dots = lax.dot_general(kblk, qi_all[...], _NT, preferred_element_type=f32)
        score = jnp.zeros((KB, BQ), f32)
        for h in range(IDX_HEADS):
            score = score + jnp.maximum(dots[:, h * BQ:(h + 1) * BQ], 0.0) * wi_rows[h]
        sk[kb] = _sort_key(jnp.where(admissible(kb), score, NEG))
        return carry

    lax.fori_loop(0, nkb, score_block, 0)

    neg_key = _sort_key(jnp.full((1, 1), NEG, f32))
    n_rest = ((NKB - nkb) * KB).astype(f32)
    one, zero = jnp.ones((), jnp.bfloat16), jnp.zeros((), jnp.bfloat16)

    def byte_of(key, byte):
        return ((key >> 24) + 128) if byte == 3 else ((key >> (8 * byte)) & 255)

    target = jnp.full((1, BQ), TOPK, f32)
    prefix = jnp.zeros((1, BQ), jnp.int32)
    above = jnp.zeros((1, BQ), f32)
    for byte in (3, 2, 1, 0):
        def in_class(key, byte=byte, prefix=prefix):
            return (key >> (8 * byte + 8)) == prefix

        def prepare(kb, carry, byte=byte, in_class=in_class):
            key = sk[kb]
            digit = byte_of(key, byte).astype(f32)
            if byte < 3:
                digit = jnp.where(in_class(key), digit, -1.0)
            dg[kb] = digit.astype(dg.dtype)
            return carry

        lax.fori_loop(0, nkb, prepare, 0)
        rest_digit = byte_of(neg_key, byte)
        rest_on = in_class(neg_key) if byte < 3 else (neg_key == neg_key)

        def count_ge(cand, rest_digit=rest_digit, rest_on=rest_on):
            cand_b = cand.astype(f32).astype(jnp.bfloat16)

            def body(kb, acc):
                hit = jnp.where(dg[kb] >= cand_b, one, zero)
                return acc + _rows_tree(hit, jnp.add, BF16_ROWS).astype(f32)

            acc = lax.fori_loop(0, nkb, body, jnp.zeros((BF16_ROWS, BQ), f32))
            rest = jnp.where(rest_on & (rest_digit >= cand), n_rest, 0.0)
            return jnp.sum(acc, axis=0, keepdims=True) + rest

        digit_thr, above_here = _bit_search(count_ge, target, jnp.zeros((1, BQ), jnp.int32),
                                            jnp.zeros((1, BQ), f32), 7)
        prefix = (digit_thr - 128) if byte == 3 else (prefix * 256 + digit_thr)
        above = above + above_here
        target = target - above_here
    thr = prefix
    need = TOPK - above

    acc_s[...] = jnp.zeros(acc_s.shape, f32)

    far_bias = bias_ref[B_NEAR, 0:1, :]

    def attend(far, kb, carry):
        eq_seen, m_old, l_old = carry
        keys = sk[kb]
        eq = keys == thr
        eq_f = jnp.where(eq, 1.0, 0.0)
        before = jnp.dot(tri_ref[...], eq_f.astype(jnp.bfloat16), preferred_element_type=f32)
        take_eq = jnp.where(eq, jnp.where(eq_seen + before < need, 1.0, 0.0), 0.0)
        take = jnp.where(keys > thr, 1.0, take_eq)
        mask_add = jnp.where(admissible(kb), jnp.where(take > 0.0, 0.0, NEG), NEG)
        cblk = ckvn[pl.ds(pl.multiple_of(kb * KB, KB), KB), :]
        dots = lax.dot_general(cblk, ql[...], _NT, preferred_element_type=f32)
        tile = nkb - 1 - kb
        m_parts, l_parts, a_parts = [], [], []
        for g in range(HQ // LANES):
            cols = slice(g * LANES, (g + 1) * LANES)
            qcols = slice((g % (BQ // LANES)) * LANES, (g % (BQ // LANES) + 1) * LANES)
            mo = m_old[:, cols]
            if far:
                s = dots[:, cols] + mask_add[:, qcols]
                shift = far_bias[:, cols]
                mn = jnp.maximum(mo, jnp.max(_rows8(s, jnp.maximum), axis=0, keepdims=True) + shift)
                e = jnp.exp2(s - (mn - shift))
            else:
                s = dots[:, cols] + bias_ref[tile, :, cols] + mask_add[:, qcols]
                mn = jnp.maximum(mo, jnp.max(_rows8(s, jnp.maximum), axis=0, keepdims=True))
                e = jnp.exp2(s - mn)
            a = jnp.exp2(mo - mn)
            l_parts.append(a * l_old[:, cols] + jnp.sum(_rows8(e, jnp.add), axis=0, keepdims=True))
            m_parts.append(mn)
            a_parts.append(a)
            p_s[:, cols] = e.astype(p_s.dtype)
        alpha = jnp.concatenate(a_parts, axis=1)
        acc_s[...] = acc_s[...] * alpha + lax.dot_general(cblk, p_s[...], _TN, preferred_element_type=f32)
        eq_seen = eq_seen + jnp.sum(_rows8(eq_f, jnp.add), axis=0, keepdims=True)
        return eq_seen, jnp.concatenate(m_parts, axis=1), jnp.concatenate(l_parts, axis=1)

    init = (jnp.zeros((1, BQ), f32), jnp.full((1, HQ), 4 * NEG, f32), jnp.zeros((1, HQ), f32))
    n_far = jnp.maximum(nkb - B_NEAR, 0)
    carry = lax.fori_loop(0, n_far, functools.partial(attend, True), init)
    _, _, l_fin = lax.fori_loop(n_far, nkb, functools.partial(attend, False), carry)

    o_lat_t = acc_s[...] / l_fin
    for p in range(B_HEADS // 2):
        out = jnp.zeros((BQ, LANES), f32)
        for h in (2 * p, 2 * p + 1):
            o_lat = o_lat_t[:, h * BQ:(h + 1) * BQ].T.astype(jnp.bfloat16)
            out = out + jnp.dot(o_lat, wuv_ref[h], preferred_element_type=f32)
        o_ref[:, p * LANES:(p + 1) * LANES] = out.astype(o_ref.dtype)


def _attn_b(proj3, gain, wuk, wuv, bias, tri, nbatch=BATCH):
    nq = SEQ // BQ
    xw = IN_A - X_COL0
    return pl.pallas_call(
        _attn_b_kernel,
        grid=(nbatch, nq),
        in_specs=[
            pl.BlockSpec((None, BQ, B_W), lambda b, i: (b, i, O_QB // B_W)),
            pl.BlockSpec((None, BQ, xw), lambda b, i: (b, i, X_COL0 // xw)),
            pl.BlockSpec((None, SEQ, LANES), lambda b, i: (b, 0, O_KI // LANES)),
            pl.BlockSpec((None, SEQ, LANES), lambda b, i: (b, 0, O_CKV // LANES)),
            pl.BlockSpec((1, B_LATENT), lambda b, i: (0, 0)),
            pl.BlockSpec((B_HEADS, LANES, B_LATENT), lambda b, i: (0, 0, 0)),
            pl.BlockSpec((B_HEADS, B_LATENT, LANES), lambda b, i: (0, 0, 0)),
            pl.BlockSpec((B_NEAR + 1, KB, HQ), lambda b, i: (0, 0, 0), pipeline_mode=pl.Buffered(1)),
            pl.BlockSpec((KB, KB), lambda b, i: (0, 0)),
        ],
        out_specs=pl.BlockSpec((None, BQ, B_W), lambda b, i: (b, i, 0)),
        out_shape=jax.ShapeDtypeStruct((nbatch, SEQ, B_W), jnp.bfloat16),
        scratch_shapes=[
            pltpu.VMEM((SEQ, B_LATENT), jnp.bfloat16),
            pltpu.VMEM((SEQ, LANES), jnp.bfloat16),
            pltpu.VMEM((HQ, B_LATENT), jnp.bfloat16),
            pltpu.VMEM((HQ, LANES), jnp.bfloat16),
            pltpu.VMEM((NKB, KB, BQ), jnp.int32),
            pltpu.VMEM((NKB, KB, BQ), jnp.bfloat16),
            pltpu.VMEM((B_LATENT, HQ), jnp.float32),
            pltpu.VMEM((KB, HQ), jnp.bfloat16),
        ],
        compiler_params=_cparams(("arbitrary", "arbitrary")),
        name="attn_b",
    )(proj3, proj3, proj3, proj3, gain, wuk, wuv, bias, tri)


MERGE_BN = 1024
MERGE_BM = 1024


def _merge_kernel(oa_ref, ob_ref, oc_ref, ga_ref, gb_ref, gc_ref, wa_ref, wb_ref, wc_ref,
                  wo_in, wq_in, wmo_in, o_ref, wo_out, wq_out, wmo_out, wa, wb, wc):
    f32 = jnp.float32

    @pl.when(pl.program_id(1) == 0)
    def _():
        wa[...] = wa_ref[...].astype(wa.dtype)
        wb[...] = wb_ref[...].astype(wb.dtype)
        wc[...] = wc_ref[...].astype(wc.dtype)

    for src, dst in ((wo_in, wo_out), (wq_in, wq_out), (wmo_in, wmo_out)):
        dst[...] = src[...].astype(dst.dtype)

    m = ga_ref[...].astype(f32) * jnp.dot(oa_ref[...], wa[...], preferred_element_type=f32)
    m = m + gb_ref[...].astype(f32) * jnp.dot(ob_ref[...], wb[...], preferred_element_type=f32)
    m = m + gc_ref[...].astype(f32) * jnp.dot(oc_ref[...], wc[...], preferred_element_type=f32)
    o_ref[...] = m.astype(o_ref.dtype)


def _merge(oa, ob, oc, gates, w_branch, layer, next_weights):
    t = oa.shape[0]
    bn, bm = MERGE_BN, MERGE_BM
    gstep = D_MODEL // bn
    ni = t // bm
    nsteps = (D_MODEL // bn) * ni
    slabs = [w.shape[1] // nsteps for w in next_weights]
    assert all(s * nsteps == w.shape[1] and s % BF16_ROWS == 0 for s, w in zip(slabs, next_weights))
    side_in = [pl.BlockSpec((None, s, w.shape[2]), lambda j, i: (layer, j * ni + i, 0))
               for s, w in zip(slabs, next_weights)]
    side_out = [pl.BlockSpec((s, w.shape[2]), lambda j, i: (j * ni + i, 0)) for s, w in zip(slabs, next_weights)]
    res = pl.pallas_call(
        _merge_kernel,
        grid=(D_MODEL // bn, ni),
        in_specs=[
            pl.BlockSpec((bm, A_W), lambda j, i: (i, 0)),
            pl.BlockSpec((bm, B_W), lambda j, i: (i, 0)),
            pl.BlockSpec((bm, C_W), lambda j, i: (i, 0)),
            pl.BlockSpec((bm, bn), lambda j, i: (i, j)),
            pl.BlockSpec((bm, bn), lambda j, i: (i, gstep + j)),
            pl.BlockSpec((bm, bn), lambda j, i: (i, 2 * gstep + j)),
            pl.BlockSpec((None, A_W, bn), lambda j, i: (layer, 0, j)),
            pl.BlockSpec((None, B_W, bn), lambda j, i: (layer, A_W // B_W, j)),
            pl.BlockSpec((None, C_W, bn), lambda j, i: (layer, (A_W + B_W) // C_W, j)),
        ] + side_in,
        out_specs=[pl.BlockSpec((bm, bn), lambda j, i: (i, j))] + side_out,
        out_shape=[jax.ShapeDtypeStruct((t, D_MODEL), jnp.bfloat16)]
        + [jax.ShapeDtypeStruct(w.shape[1:], jnp.bfloat16) for w in next_weights],
        scratch_shapes=[pltpu.VMEM((A_W, bn), jnp.bfloat16), pltpu.VMEM((B_W, bn), jnp.bfloat16),
                        pltpu.VMEM((C_W, bn), jnp.bfloat16)],
        compiler_params=_cparams(("arbitrary", "arbitrary")),
        name="merge",
    )(oa, ob, oc, gates, gates, gates, w_branch, w_branch, w_branch, *next_weights)
    return res[0], res[1:]


FINISH_CHUNKS = 4


def _finish_chunked(y_of_rows, nrows, x_ref, gp_ref, gn_ref, xo_ref, ho_ref):
    step = nrows // FINISH_CHUNKS
    for r in range(0, nrows, step):
        rows = slice(r, r + step)
        xn = x_ref[rows, :] + _rms(y_of_rows(rows), gp_ref[...])
        xo_ref[rows, :] = xn
        if ho_ref is not None:
            ho_ref[rows, :] = _rms(xn, gn_ref[...]).astype(ho_ref.dtype)


def _gemm_res_kernel(nk, with_next, a_ref, w_ref, x_ref, gp_ref, gn_ref, xo_ref, *rest):
    ho_ref = rest[0] if with_next else None
    f32 = jnp.float32
    bm = a_ref.shape[0]
    if nk == 1:
        _finish_chunked(lambda rows: jnp.dot(a_ref[rows, :], w_ref[...], preferred_element_type=f32),
                        bm, x_ref, gp_ref, gn_ref, xo_ref, ho_ref)
        return
    acc = rest[-1]
    k = pl.program_id(1)

    @pl.when(k == 0)
    def _():
        acc[...] = jnp.dot(a_ref[...], w_ref[...], preferred_element_type=f32)

    @pl.when((k > 0) & (k < nk - 1))
    def _():
        acc[...] += jnp.dot(a_ref[...], w_ref[...], preferred_element_type=f32)

    @pl.when(k == nk - 1)
    def _():
        _finish_chunked(lambda rows: acc[rows, :] + jnp.dot(a_ref[rows, :], w_ref[...], preferred_element_type=f32),
                        bm, x_ref, gp_ref, gn_ref, xo_ref, ho_ref)


def _gemm_res(a, w, layer, x, g_post, g_next, bk, bm=512, name="gemm_res"):
    t, kdim = a.shape
    n = w.shape[2]
    nk = kdim // bk
    with_next = g_next is not None
    if g_next is None:
        g_next = g_post
    out_shape = [jax.ShapeDtypeStruct((t, n), jnp.float32)]
    out_specs = [pl.BlockSpec((bm, n), lambda i, k: (i, 0))]
    if with_next:
        out_shape.append(jax.ShapeDtypeStruct((t, n), jnp.bfloat16))
        out_specs.append(pl.BlockSpec((bm, n), lambda i, k: (i, 0)))
    res = pl.pallas_call(
        functools.partial(_gemm_res_kernel, nk, with_next),
        grid=(t // bm, nk),
        in_specs=[
            pl.BlockSpec((bm, bk), lambda i, k: (i, k)),
            pl.BlockSpec((None, bk, n), lambda i, k: (layer, k, 0)),
            pl.BlockSpec((bm, n), lambda i, k: (i, 0)),
            pl.BlockSpec((1, n), lambda i, k: (0, 0)),
            pl.BlockSpec((1, n), lambda i, k: (0, 0)),
        ],
        out_specs=out_specs,
        out_shape=out_shape,
        scratch_shapes=[pltpu.VMEM((bm, n), jnp.float32)] if nk > 1 else [],
        compiler_params=_cparams(("parallel", "arbitrary")),
        name=name,
    )(a, w, x, g_post, g_next)
    return (res[0], res[1]) if with_next else (res[0], None)


def _cast_once(pairs):
    @pl.when(pl.program_id(0) == 0)
    def _():
        for src, dst in pairs:
            dst[...] = src[...].astype(dst.dtype)


def _memkv_kernel(m_ref, g_ref, w_ref, o_ref, wb):
    _cast_once([(w_ref, wb)])
    mn = _rms(m_ref[...], g_ref[...]).astype(jnp.bfloat16)
    o_ref[...] = jnp.dot(mn, wb[...], preferred_element_type=jnp.float32).astype(o_ref.dtype)


def _memkv(mem2, g, w, layer):
    t, d = mem2.shape
    n = w.shape[2]
    bm = 512
    return pl.pallas_call(
        _memkv_kernel,
        grid=(t // bm,),
        in_specs=[pl.BlockSpec((bm, d), lambda i: (i, 0)), pl.BlockSpec((1, d), lambda i: (0, 0)),
                  pl.BlockSpec((None, d, n), lambda i: (layer, 0, 0))],
        out_specs=pl.BlockSpec((bm, n), lambda i: (i, 0)),
        out_shape=jax.ShapeDtypeStruct((t, n), jnp.bfloat16),
        scratch_shapes=[pltpu.VMEM((d, n), jnp.bfloat16)],
        compiler_params=_cparams(("arbitrary",)),
        name="mem_kv",
    )(mem2, g, w)


XA_BM = 512


def _mix_out_xattn_kernel(m_ref, wo_ref, wq_ref, kv_ref, wmo_ref, x_ref, g1_ref, g2_ref, g3_ref, g4_ref,
                          xo_ref, ho_ref, x1_s, h1_s):
    f32 = jnp.float32
    bm = m_ref.shape[0]
    _finish_chunked(lambda rows: jnp.dot(m_ref[rows, :], wo_ref[...], preferred_element_type=f32),
                    bm, x_ref, g1_ref, g2_ref, x1_s, h1_s)
    q = jnp.dot(h1_s[...], wq_ref[...], preferred_element_type=f32).astype(jnp.bfloat16)
    scale = MEM_HEAD_DIM ** -0.5 * LOG2E
    hcols = [slice(h * LANES, (h + 1) * LANES) for h in range(MEM_HEADS)]
    ss = [lax.dot_general(q[:, c], kv_ref[:, c], _NT, preferred_element_type=f32) * scale for c in hcols]
    ms = [jnp.max(s, axis=-1, keepdims=True) for s in ss]
    es = [jnp.exp2(s - m) for s, m in zip(ss, ms)]
    ls = [jnp.sum(e, axis=-1, keepdims=True) for e in es]
    pv = [jnp.dot(e.astype(jnp.bfloat16), kv_ref[:, MEM_W + h * LANES:MEM_W + (h + 1) * LANES],
                  preferred_element_type=f32) for h, e in enumerate(es)]
    o = jnp.concatenate([(o_h / l).astype(jnp.bfloat16) for o_h, l in zip(pv, ls)], axis=-1)
    _finish_chunked(lambda rows: jnp.dot(o[rows, :], wmo_ref[...], preferred_element_type=f32),
                    bm, x1_s, g3_ref, g4_ref, xo_ref, ho_ref)


def _mix_out_xattn(merged, w_o, w_mq, kv, w_mo, layer, x, g1, g2, g3, g4):
    t, d = merged.shape
    bm = XA_BM
    per_batch = SEQ // bm
    once = pl.Buffered(1)
    row = pl.BlockSpec((bm, d), lambda i: (i, 0))
    gain = pl.BlockSpec((1, d), lambda i: (0, 0))
    return pl.pallas_call(
        _mix_out_xattn_kernel,
        grid=(t // bm,),
        in_specs=[
            row,
            pl.BlockSpec((None, d, d), lambda i: (layer, 0, 0), pipeline_mode=once),
            pl.BlockSpec((None, d, MEM_W), lambda i: (layer, 0, 0), pipeline_mode=once),
            pl.BlockSpec((MEM_LEN, 2 * MEM_W), lambda i: (i // per_batch, 0)),
            pl.BlockSpec((None, MEM_W, d), lambda i: (layer, 0, 0), pipeline_mode=once),
            row, gain, gain, gain, gain,
        ],
        out_specs=[row, row],
        out_shape=[jax.ShapeDtypeStruct((t, d), jnp.float32), jax.ShapeDtypeStruct((t, d), jnp.bfloat16)],
        scratch_shapes=[pltpu.VMEM((bm, d), jnp.float32), pltpu.VMEM((bm, d), jnp.bfloat16)],
        compiler_params=_cparams(("parallel",)),
        name="mix_out_xattn",
    )(merged, w_o, w_mq, kv, w_mo, x, g1, g2, g3, g4)


def _ffn_up_kernel(h_ref, wgf_ref, wvf_ref, cwg_ref, cwv_ref, cbg_ref, cbv_ref, wd_ref, o_ref, wdo_ref,
                   wg_ref, wv_ref, ug, uv):
    j = pl.program_id(0)
    i = pl.program_id(1)

    slab = j * pl.num_programs(1) + i

    @pl.when(slab < D_FF // wd_ref.shape[0])
    def _():
        wdo_ref[...] = wd_ref[...].astype(wdo_ref.dtype)

    @pl.when(slab >= D_FF // wd_ref.shape[0])
    def _():
        wdo_ref[...] = jnp.zeros(wdo_ref.shape, wdo_ref.dtype)

    tiles_per_seq = SEQ // FF_BM
    last = FF_P // FF_BN - 1
    valid = D_FF - last * FF_BN
    shift = FF_BN - valid

    @pl.when((i == 0) & (j < last))
    def _():
        wg_ref[...] = wgf_ref[...].astype(wg_ref.dtype)
        wv_ref[...] = wvf_ref[...].astype(wv_ref.dtype)

    @pl.when((i == 0) & (j == last))
    def _():
        zeros = jnp.zeros((wg_ref.shape[0], FF_BN - valid), wg_ref.dtype)
        wg_ref[:, :valid] = wgf_ref[:, :valid].astype(wg_ref.dtype)
        wv_ref[:, :valid] = wvf_ref[:, shift:].astype(wv_ref.dtype)
        wg_ref[:, valid:] = zeros
        wv_ref[:, valid:] = zeros

    for u in (ug, uv):
        @pl.when(i % tiles_per_seq == 0)
        def _():
            u[0:SUBLANES, :] = jnp.zeros((SUBLANES, FF_BN), jnp.float32)

        @pl.when(i % tiles_per_seq != 0)
        def _():
            u[0:SUBLANES, :] = u[FF_BM:FF_BM + SUBLANES, :]

    def conv(u, cw_ref, cb_ref, r0):
        base = SUBLANES + r0
        acc = cb_ref[...] + u[base - 2:base - 2 + FF_CH, :] * cw_ref[0:1, :]
        acc = acc + u[base - 1:base - 1 + FF_CH, :] * cw_ref[1:2, :]
        return acc + u[base:base + FF_CH, :] * cw_ref[2:3, :]

    for c in range(FF_BM // FF_CH):
        r0 = c * FF_CH
        hb = h_ref[r0:r0 + FF_CH, :]
        ug[SUBLANES + r0:SUBLANES + r0 + FF_CH, :] = jnp.dot(hb, wg_ref[...], preferred_element_type=jnp.float32)
        uv[SUBLANES + r0:SUBLANES + r0 + FF_CH, :] = jnp.dot(hb, wv_ref[...], preferred_element_type=jnp.float32)
        gate = conv(ug, cwg_ref, cbg_ref, r0)
        val = conv(uv, cwv_ref, cbv_ref, r0)
        o_ref[r0:r0 + FF_CH, :] = (jax.nn.gelu(gate) * val).astype(o_ref.dtype)


def _ffn_up(h, w_up, w_down, layer, conv_w, conv_b):
    t, d = h.shape
    nj = FF_P // FF_BN
    ni = t // FF_BM
    slab = FF_P // (nj * ni)
    assert slab * nj * ni == FF_P and D_FF % slab == 0 and slab % BF16_ROWS == 0
    n_real = D_FF // slab
    w_block = (None, pl.Element(d), pl.Element(FF_BN))
    return pl.pallas_call(
        _ffn_up_kernel,
        grid=(nj, ni),
        in_specs=[
            pl.BlockSpec((FF_BM, d), lambda j, i: (i, 0)),
            pl.BlockSpec(w_block, lambda j, i: (layer, 0, pl.multiple_of(j * FF_BN, LANES))),
            pl.BlockSpec(w_block, lambda j, i: (
                layer, 0, pl.multiple_of(jnp.minimum(D_FF + j * FF_BN, 2 * D_FF - FF_BN), LANES))),
            pl.BlockSpec((CONV_W, FF_BN), lambda j, i: (0, j)),
            pl.BlockSpec((CONV_W, FF_BN), lambda j, i: (0, nj + j)),
            pl.BlockSpec((1, FF_BN), lambda j, i: (0, j)),
            pl.BlockSpec((1, FF_BN), lambda j, i: (0, nj + j)),
            pl.BlockSpec((None, slab, d), lambda j, i: (layer, jnp.minimum(j * ni + i, n_real - 1), 0)),
        ],
        out_specs=[pl.BlockSpec((FF_BM, FF_BN), lambda j, i: (i, j)),
                   pl.BlockSpec((slab, d), lambda j, i: (j * ni + i, 0))],
        out_shape=[jax.ShapeDtypeStruct((t, FF_P), jnp.bfloat16),
                   jax.ShapeDtypeStruct((FF_P, w_down.shape[2]), jnp.bfloat16)],
        scratch_shapes=[pltpu.VMEM((d, FF_BN), jnp.bfloat16), pltpu.VMEM((d, FF_BN), jnp.bfloat16),
                        pltpu.VMEM((FF_BM + SUBLANES, FF_BN), jnp.float32),
                        pltpu.VMEM((FF_BM + SUBLANES, FF_BN), jnp.float32)],
        compiler_params=_cparams(("arbitrary", "arbitrary")),
        name="ffn_up",
    )(h, w_up, w_up, conv_w, conv_w, conv_b, conv_b, w_down)


def _pad_heads(w, axis):
    h = w.shape[0]
    zero = jnp.zeros_like(w)
    even = jnp.concatenate([w, zero], axis=axis)
    odd = jnp.concatenate([zero, w], axis=axis)
    sel = (jnp.arange(h) % 2 == 0).reshape((h, 1, 1))
    return jnp.where(sel, even, odd)


def _toeplitz(fn, rows, cols):
    ks = np.concatenate([np.arange(0, cols), np.arange(-(rows - 1), 0)])
    w = fn(ks)
    h, period = w.shape
    x = jnp.tile(w, (1, rows))[:, :rows * (period - 1)].reshape(h, rows, period - 1)
    return x[:, :, :cols].astype(jnp.float32)


def _band(rows, cols, left):
    diff = left + np.arange(rows)[:, None] // CHUNK - np.arange(cols)[None, :] // CHUNK
    return (diff >= 0) & (diff <= left)


def _bias_a(rel_bias):
    fn = lambda ks: rel_bias[np.clip(A_PAD - ks, -A_MAX_REL, A_MAX_REL) + A_MAX_REL].T
    bias = jnp.where(_band(QB, A_WIN, A_LEFT_CHUNKS)[None], _toeplitz(fn, QB, A_WIN) * LOG2E, NEG)
    return bias.reshape(A_HEADS * QB, A_WIN)


def _bias_c(t5_c):
    fn = lambda ks: t5_c[_t5_bucket(jnp.asarray(ks - C_PAD, jnp.int32))].T
    bias = jnp.where(_band(QB, C_WIN, C_LEFT_CHUNKS)[None], _toeplitz(fn, QB, C_WIN) * LOG2E, NEG)
    return bias.reshape(C_Q_HEADS * QB, C_WIN)


def _bias_b(t5_b):
    tiles = []
    for n in range(B_NEAR + 1):
        off = KB * (n - (B_QK - 1)) if n < B_NEAR else SEQ
        fn = lambda ks, off=off: t5_b[_t5_bucket(jnp.asarray(-ks - off, jnp.int32))].T
        tile = _toeplitz(fn, KB, BQ)
        tiles.append(jnp.transpose(tile * LOG2E, (1, 0, 2)).reshape(KB, HQ))
    return jnp.stack(tiles)


def _pad_ff(a, dtype):
    z = jnp.zeros((a.shape[0], FF_P - D_FF), dtype)
    return jnp.concatenate([a[:, :D_FF].astype(dtype), z, a[:, D_FF:].astype(dtype), z], axis=1)


def kernel(x, mem, t5_table, norm_gains, w_in, a_rel_bias, ckv_gain, w_uk, w_uv, sinks, w_branch, w_o,
           mem_gain, w_mq, w_mkv, w_mo, w_up, conv_w, conv_b, w_down):
    bf16 = jnp.bfloat16
    xs = x.reshape(TOKENS, D_MODEL)
    mem2 = mem.reshape(BATCH * MEM_LEN, D_MODEL)
    tri = jnp.asarray(np.tril(np.ones((KB, KB), np.float32), -1), bf16)
    bias_b = _bias_b(t5_table[:, :B_HEADS])
    bias_c = _bias_c(t5_table[:, B_HEADS:])
    gains = norm_gains.reshape(DEPTH, 6, 1, D_MODEL)
    w_in_t = jnp.swapaxes(w_in, 1, 2)

    h = _norm(xs, gains[0, 0])
    for l in range(DEPTH):
        g = gains[l]
        proja = _proj(h, w_in_t, l, 0, IN_A, 1024, gate=False, name="in_proj_a")
        projc = _proj(h, w_in_t, l, O_QC, C_COLS, C_COLS, gate=False, name="in_proj_c")
        gates = _proj(h, w_in_t, l, O_GL, 3 * D_MODEL, 1024, gate=True, name="in_proj_g")
        proja3 = proja.reshape(BATCH, SEQ, IN_A)
        oa = _attn_a(proja3, _bias_a(a_rel_bias[l]))
        wuk = jnp.transpose(_pad_heads(w_uk[l], axis=2), (0, 2, 1)).astype(bf16)
        wuv = _pad_heads(w_uv[l], axis=2).astype(bf16)
        ob = _attn_b(proja3, ckv_gain[l].reshape(1, B_LATENT), wuk, wuv, bias_b, tri)
        oc = _attn_c(sinks[l], projc.reshape(BATCH, SEQ, C_COLS), bias_c)
        merged, (w_o_b, w_mq_b, w_mo_b) = _merge(oa.reshape(TOKENS, A_W), ob.reshape(TOKENS, B_W),
                                                 oc.reshape(TOKENS, C_W), gates, w_branch, l, (w_o, w_mq, w_mo))
        kv = _memkv(mem2, mem_gain[l].reshape(1, D_MODEL), w_mkv, l)
        xs, h = _mix_out_xattn(merged, w_o_b[None], w_mq_b[None], kv, w_mo_b[None], 0, xs, g[1], g[2], g[3], g[4])
        hidden, w_dn_b = _ffn_up(h, w_up, w_down, l, _pad_ff(conv_w[l], jnp.float32),
                                 _pad_ff(conv_b[l].reshape(1, -1), jnp.float32))
        g_next = gains[l + 1, 0] if l + 1 < DEPTH else None
        xs, h = _gemm_res(hidden, w_dn_b[None], 0, xs, g[5], g_next, bk=DOWN_BK, name="ffn_down")
    return xs.reshape(BATCH, SEQ, D_MODEL)
```

```python
import functools
import math

import numpy as np
import jax
import jax.numpy as jnp
from jax import lax
from jax.experimental import pallas as pl
from jax.experimental.pallas import tpu as pltpu

D_MODEL = 2048
BATCH = 4
SEQ = 2048
DEPTH = 2
TOKENS = BATCH * SEQ
CHUNK = 64
EPS = 1e-6
NEG = -1e30
LOG2E = math.log2(math.e)
A_HEADS = 8
A_LEFT_CHUNKS = 8
A_MAX_REL = 128
A_W = 512
B_HEADS = 8
B_W = 512
B_LATENT = 128
IDX_HEADS = 8
IDX_DIM = 64
TOPK = 256
C_Q_HEADS = 16
C_GROUP = 8
C_W = 1024
C_LEFT_CHUNKS = 2
T5_BUCKETS = 32
T5_MAX_DIST = 256
MEM_LEN = 256
MEM_HEADS = 4
MEM_HEAD_DIM = 128
MEM_W = 512
D_FF = 5504
CONV_W = 3

LANES = 128
SUBLANES = 8
HEAD_DIM = 64
QB = 128
KB = 256
VMEM_LIMIT = 56 * 1024 * 1024

O_QA, O_KA, O_VA, O_QB = 0, 512, 1024, 1536
O_CKV = 2048
O_QI = 2176
O_KI = 2688
O_WI = 2752
O_QC = 2760
O_GL = 4040
IN_W = O_GL + 3 * D_MODEL
IN_A = 3072
X_COL0 = 2048
C_COLS = O_GL - O_QC
PROJ_BM = 1024
PROJ_TR = 256

FF_P = 5632
FF_BN = 512
FF_BM = 2048
FF_CH = 256
DOWN_BK = 1408

A_WIN = (A_LEFT_CHUNKS + 2) * CHUNK
C_WIN = (C_LEFT_CHUNKS + 2) * CHUNK
A_PAD = A_LEFT_CHUNKS * CHUNK
C_PAD = C_LEFT_CHUNKS * CHUNK
BQ = 256
B_QK = BQ // KB
B_NEAR = B_QK + 1
NKB = SEQ // KB
HQ = B_HEADS * BQ

_NT = (((1,), (1,)), ((), ()))
_TN = (((0,), (0,)), ((), ()))


def _cparams(sem):
    return pltpu.CompilerParams(dimension_semantics=sem, vmem_limit_bytes=VMEM_LIMIT)


def _t5_bucket(rel):
    half = T5_BUCKETS // 2
    max_exact = half // 2
    sign = jnp.where(rel > 0, half, 0)
    d = jnp.abs(rel)
    d_f = jnp.maximum(d, 1).astype(jnp.float32)
    large = max_exact + (jnp.log(d_f / max_exact) / math.log(T5_MAX_DIST / max_exact) * (half - max_exact)).astype(jnp.int32)
    large = jnp.minimum(large, half - 1)
    return sign + jnp.where(d < max_exact, d, large)


def _far_bucket_is_constant():
    assert BQ % KB == 0
    d = np.arange(KB + 1, SEQ, dtype=np.float32)
    assert d[0] > T5_MAX_DIST
    large = 8 + (np.log(d / 8) / math.log(T5_MAX_DIST / 8) * 8).astype(np.int32)
    return bool(np.all(np.minimum(large, 15) == 15))


assert _far_bucket_is_constant()


def _rms(v, g):
    return v * lax.rsqrt(jnp.mean(v * v, axis=-1, keepdims=True) + EPS) * g


def _norm_kernel(x_ref, g_ref, o_ref):
    o_ref[...] = _rms(x_ref[...], g_ref[...]).astype(o_ref.dtype)


def _norm(x, g, bm=1024):
    t, d = x.shape
    return pl.pallas_call(
        _norm_kernel,
        grid=(t // bm,),
        in_specs=[pl.BlockSpec((bm, d), lambda i: (i, 0)), pl.BlockSpec((1, d), lambda i: (0, 0))],
        out_specs=pl.BlockSpec((bm, d), lambda i: (i, 0)),
        out_shape=jax.ShapeDtypeStruct((t, d), jnp.bfloat16),
        compiler_params=_cparams(("parallel",)),
        name="rmsnorm",
    )(x, g)


def _proj_kernel(gate, h_ref, wt_ref, o_ref, wb):
    @pl.when(pl.program_id(1) == 0)
    def _():
        for c in range(wt_ref.shape[0] // PROJ_TR):
            rows = slice(c * PROJ_TR, (c + 1) * PROJ_TR)
            wb[:, rows] = wt_ref[rows, :].T.astype(wb.dtype)

    acc = jnp.dot(h_ref[...], wb[...], preferred_element_type=jnp.float32)
    o_ref[...] = (jax.nn.sigmoid(acc) if gate else acc).astype(o_ref.dtype)


def _proj(h, w_t, layer, col0, n, bn, gate, name):
    t, k = h.shape
    assert col0 % SUBLANES == 0 and bn % SUBLANES == 0
    w_spec = pl.BlockSpec((None, pl.Element(bn), pl.Element(k)),
                          lambda j, i: (layer, pl.multiple_of(col0 + j * bn, SUBLANES), 0))
    return pl.pallas_call(
        functools.partial(_proj_kernel, gate),
        grid=(n // bn, t // PROJ_BM),
        in_specs=[pl.BlockSpec((PROJ_BM, k), lambda j, i: (i, 0)), w_spec],
        out_specs=pl.BlockSpec((PROJ_BM, bn), lambda j, i: (i, j)),
        out_shape=jax.ShapeDtypeStruct((t, n), jnp.bfloat16),
        scratch_shapes=[pltpu.VMEM((k, bn), jnp.bfloat16)],
        compiler_params=_cparams(("arbitrary", "arbitrary")),
        name=name,
    )(h, w_t)


def _lane_masks(dtype):
    lane = lax.broadcasted_iota(jnp.int32, (1, LANES), 1)
    lo = (lane < HEAD_DIM).astype(jnp.float32)
    return lo.astype(dtype), (1.0 - lo).astype(dtype)


def _swap_halves(x):
    return pltpu.roll(x.astype(jnp.float32), HEAD_DIM, 1).astype(x.dtype)


def _fill_padded(src, dst_ref, pad):
    dst_ref[0:pad, :] = jnp.zeros((pad, dst_ref.shape[1]), dst_ref.dtype)
    dst_ref[pad:pad + SEQ, :] = src


def _attn_a_kernel(q_ref, k_ref, v_ref, bias_ref, o_ref, kpad, vpad):
    i = pl.program_id(1)

    @pl.when(i == 0)
    def _():
        _fill_padded(k_ref[...], kpad, A_PAD)
        _fill_padded(v_ref[...], vpad, A_PAD)

    start = pl.multiple_of(i * QB, QB)
    kw = kpad[pl.ds(start, A_WIN), :]
    vw = vpad[pl.ds(start, A_WIN), :]
    mlo, mhi = _lane_masks(jnp.bfloat16)
    lane = lax.broadcasted_iota(jnp.int32, (QB, LANES), 1)
    kchunk = lax.broadcasted_iota(jnp.int32, (QB, A_WIN), 1) // CHUNK
    pad_mask = jnp.where(kchunk >= A_LEFT_CHUNKS - 2 * i, 0.0, NEG)
    scale = HEAD_DIM ** -0.5 * LOG2E
    pad2 = jnp.concatenate([pad_mask, pad_mask], axis=0)
    npair = A_HEADS // 2
    pcols = [slice(p * LANES, (p + 1) * LANES) for p in range(npair)]
    ss = []
    for p in range(npair):
        qp = q_ref[:, pcols[p]]
        q2 = jnp.concatenate([qp * mlo, qp * mhi], axis=0)
        s = lax.dot_general(q2, kw[:, pcols[p]], _NT, preferred_element_type=jnp.float32)
        ss.append(s * scale + bias_ref[2 * p * QB:(2 * p + 2) * QB, :] + pad2)
    ms = [jnp.max(s, axis=-1, keepdims=True) for s in ss]
    es = [jnp.exp2(s - m) for s, m in zip(ss, ms)]
    ls = [jnp.sum(e, axis=-1, keepdims=True) for e in es]
    outs = [jnp.dot(e.astype(jnp.bfloat16), vw[:, pcols[p]], preferred_element_type=jnp.float32)
            for p, e in enumerate(es)]
    for p in range(npair):
        o = outs[p] / ls[p]
        o_ref[:, pcols[p]] = jnp.where(lane < HEAD_DIM, o[:QB], o[QB:]).astype(o_ref.dtype)


def _attn_a(proj3, bias):
    nq = SEQ // QB
    return pl.pallas_call(
        _attn_a_kernel,
        grid=(BATCH, nq),
        in_specs=[
            pl.BlockSpec((None, QB, A_W), lambda b, i: (b, i, O_QA // A_W)),
            pl.BlockSpec((None, SEQ, A_W), lambda b, i: (b, 0, O_KA // A_W)),
            pl.BlockSpec((None, SEQ, A_W), lambda b, i: (b, 0, O_VA // A_W)),
            pl.BlockSpec((A_HEADS * QB, A_WIN), lambda b, i: (0, 0)),
        ],
        out_specs=pl.BlockSpec((None, QB, A_W), lambda b, i: (b, i, 0)),
        out_shape=jax.ShapeDtypeStruct((BATCH, SEQ, A_W), jnp.bfloat16),
        scratch_shapes=[pltpu.VMEM((SEQ + A_PAD, A_W), jnp.bfloat16), pltpu.VMEM((SEQ + A_PAD, A_W), jnp.bfloat16)],
        compiler_params=_cparams(("arbitrary", "arbitrary")),
        name="attn_a",
    )(proj3, proj3, proj3, bias)


def _attn_c_kernel(sink_ref, q_ref, k_ref, v_ref, bias_ref, o_ref, kpad, kswp, vpad, vswp):
    i = pl.program_id(1)

    @pl.when(i == 0)
    def _():
        k = k_ref[...]
        v = v_ref[...]
        _fill_padded(k, kpad, C_PAD)
        _fill_padded(_swap_halves(k), kswp, C_PAD)
        _fill_padded(v, vpad, C_PAD)
        _fill_padded(_swap_halves(v), vswp, C_PAD)

    start = pl.multiple_of(i * QB, QB)
    mlo, mhi = _lane_masks(jnp.bfloat16)
    lane = lax.broadcasted_iota(jnp.int32, (QB, LANES), 1)
    kchunk = lax.broadcasted_iota(jnp.int32, (QB, C_WIN), 1) // CHUNK
    pad_mask = jnp.where(kchunk >= C_LEFT_CHUNKS - 2 * i, 0.0, NEG)
    scale = HEAD_DIM ** -0.5 * LOG2E
    qs = [q_ref[:, p * LANES:(p + 1) * LANES] for p in range(C_GROUP)]
    npair = C_GROUP // 2
    stacks = []
    for straight in (True, False):
        kref, vref = (kpad, vpad) if straight else (kswp, vswp)
        kw = kref[pl.ds(start, C_WIN), :]
        vw = vref[pl.ds(start, C_WIN), :]
        halves = [int((p >= npair) == straight) for p in range(C_GROUP)]
        qg = jnp.concatenate([qs[p] * (mhi if halves[p] else mlo) for p in range(C_GROUP)], axis=0)
        s_all = lax.dot_general(qg, kw, _NT, preferred_element_type=jnp.float32)
        ps, ls = [], []
        for p in range(C_GROUP):
            h = 2 * p + halves[p]
            s = s_all[p * QB:(p + 1) * QB] * scale + bias_ref[h * QB:(h + 1) * QB, :] + pad_mask
            sink = sink_ref[h] * LOG2E
            m = jnp.maximum(jnp.max(s, axis=-1, keepdims=True), sink)
            e = jnp.exp2(s - m)
            ls.append(jnp.sum(e, axis=-1, keepdims=True) + jnp.exp2(sink - m))
            ps.append(e.astype(jnp.bfloat16))
        o_all = jnp.dot(jnp.concatenate(ps, axis=0), vw, preferred_element_type=jnp.float32)
        stacks.append(([o_all[p * QB:(p + 1) * QB] / ls[p] for p in range(C_GROUP)], halves))
    for p in range(C_GROUP):
        (o1, h1), (o2, _) = stacks
        lo, hi = (o2[p], o1[p]) if h1[p] else (o1[p], o2[p])
        o_ref[:, p * LANES:(p + 1) * LANES] = jnp.where(lane < HEAD_DIM, lo, hi).astype(o_ref.dtype)


def _attn_c(sinks, projc3, bias):
    nq = SEQ // QB
    pad_buf = pltpu.VMEM((SEQ + C_PAD, LANES), jnp.bfloat16)
    return pl.pallas_call(
        _attn_c_kernel,
        grid=(BATCH, nq),
        in_specs=[
            pl.BlockSpec(memory_space=pltpu.SMEM),
            pl.BlockSpec((None, QB, C_W), lambda b, i: (b, i, 0)),
            pl.BlockSpec((None, SEQ, LANES), lambda b, i: (b, 0, C_W // LANES)),
            pl.BlockSpec((None, SEQ, LANES), lambda b, i: (b, 0, C_W // LANES + 1)),
            pl.BlockSpec((C_Q_HEADS * QB, C_WIN), lambda b, i: (0, 0)),
        ],
        out_specs=pl.BlockSpec((None, QB, C_W), lambda b, i: (b, i, 0)),
        out_shape=jax.ShapeDtypeStruct((BATCH, SEQ, C_W), jnp.bfloat16),
        scratch_shapes=[pad_buf, pad_buf, pad_buf, pad_buf],
        compiler_params=_cparams(("arbitrary", "arbitrary")),
        name="attn_c",
    )(sinks, projc3, projc3, projc3, bias)


def _sort_key(x):
    bits = lax.bitcast_convert_type(x + 0.0, jnp.int32)
    return bits ^ ((bits >> 31) & jnp.int32(0x7FFFFFFF))


def _rows_tree(x, op, slab):
    parts = [x[r:r + slab, :] for r in range(0, x.shape[0], slab)]
    while len(parts) > 1:
        nxt = [op(parts[k], parts[k + 1]) for k in range(0, len(parts) - 1, 2)]
        parts = nxt + ([parts[-1]] if len(parts) % 2 else [])
    return parts[0]


def _rows8(x, op):
    return _rows_tree(x, op, SUBLANES)


BF16_ROWS = 16


def _bit_search(count_ge, target, v, above, top_bit):
    def step(t, carry):
        v, above = carry
        cand = v | (jnp.int32(1) << (top_bit - t))
        c = count_ge(cand)
        keep = c >= target
        return jnp.where(keep, cand, v), jnp.where(keep, above, c)

    return lax.fori_loop(0, top_bit + 1, step, (v, above))


def _attn_b_kernel(qb_ref, x_ref, kiw_ref, ckv_ref, gain_ref, wuk_ref, wuv_ref, bias_ref, tri_ref,
                   o_ref, ckvn, kd, ql, qi_all, sk, dg, acc_s, p_s):
    i = pl.program_id(1)
    nkb = (i + 1) * B_QK
    mlo, mhi = _lane_masks(jnp.bfloat16)
    f32 = jnp.float32

    @pl.when(i == 0)
    def _():
        ckvn[...] = _rms(ckv_ref[...].astype(f32), gain_ref[...]).astype(ckvn.dtype)
        kiw = kiw_ref[...].astype(f32)
        lane = lax.broadcasted_iota(jnp.int32, kiw.shape, 1)
        kd[...] = jnp.where(lane < HEAD_DIM, kiw, pltpu.roll(kiw, HEAD_DIM, 1)).astype(kd.dtype)

    for h in range(B_HEADS):
        rows = slice(h * BQ, (h + 1) * BQ)
        qlat = jnp.dot(qb_ref[:, (h // 2) * LANES:(h // 2 + 1) * LANES], wuk_ref[h], preferred_element_type=f32)
        ql[rows, :] = (qlat * (HEAD_DIM ** -0.5 * LOG2E)).astype(ql.dtype)
        qcol = O_QI - X_COL0 + (h // 2) * LANES
        qi_all[rows, :] = x_ref[:, qcol:qcol + LANES] * (mhi if h % 2 else mlo)
    wcol = O_KI - X_COL0
    wi_t = x_ref[:, wcol:wcol + LANES].astype(f32).T * (IDX_HEADS ** -0.5 * IDX_DIM ** -0.5)
    wi_rows = [wi_t[O_WI - O_KI + h:O_WI - O_KI + h + 1, :] for h in range(IDX_HEADS)]

    kpos = lax.broadcasted_iota(jnp.int32, (KB, BQ), 0)
    key_limit = ((i * BQ + lax.broadcasted_iota(jnp.int32, (1, BQ), 1)) // CHUNK + 1) * CHUNK

    def admissible(kb):
        return kb * KB + kpos < key_limit

    def score_block(kb, carry):
        kblk = kd[pl.ds(pl.multiple_of(kb * KB, KB), KB), :]
        dots = lax.dot_general(kblk, qi_all[...], _NT, preferred_element_type=f32)
        score = jnp.zeros((KB, BQ), f32)
        for h in range(IDX_HEADS):
            score = score + jnp.maximum(dots[:, h * BQ:(h + 1) * BQ], 0.0) * wi_rows[h]
        sk[kb] = _sort_key(jnp.where(admissible(kb), score, NEG))
        return carry

    lax.fori_loop(0, nkb, score_block, 0)

    neg_key = _sort_key(jnp.full((1, 1), NEG, f32))
    n_rest = ((NKB - nkb) * KB).astype(f32)
    one, zero = jnp.ones((), jnp.bfloat16), jnp.zeros((), jnp.bfloat16)

    def byte_of(key, byte):
        return ((key >> 24) + 128) if byte == 3 else ((key >> (8 * byte)) & 255)

    target = jnp.full((1, BQ), TOPK, f32)
    prefix = jnp.zeros((1, BQ), jnp.int32)
    above = jnp.zeros((1, BQ), f32)
    for byte in (3, 2, 1, 0):
        def in_class(key, byte=byte, prefix=prefix):
            return (key >> (8 * byte + 8)) == prefix

        def prepare(kb, carry, byte=byte, in_class=in_class):
            key = sk[kb]
            digit = byte_of(key, byte).astype(f32)
            if byte < 3:
                digit = jnp.where(in_class(key), digit, -1.0)
            dg[kb] = digit.astype(dg.dtype)
            return carry

        lax.fori_loop(0, nkb, prepare, 0)
        rest_digit = byte_of(neg_key, byte)
        rest_on = in_class(neg_key) if byte < 3 else (neg_key == neg_key)

        def count_ge(cand, rest_digit=rest_digit, rest_on=rest_on):
            cand_b = cand.astype(f32).astype(jnp.bfloat16)

            def body(kb, acc):
                hit = jnp.where(dg[kb] >= cand_b, one, zero)
                return acc + _rows_tree(hit, jnp.add, BF16_ROWS).astype(f32)

            acc = lax.fori_loop(0, nkb, body, jnp.zeros((BF16_ROWS, BQ), f32))
            rest = jnp.where(rest_on & (rest_digit >= cand), n_rest, 0.0)
            return jnp.sum(acc, axis=0, keepdims=True) + rest

        digit_thr, above_here = _bit_search(count_ge, target, jnp.zeros((1, BQ), jnp.int32),
                                            jnp.zeros((1, BQ), f32), 7)
        prefix = (digit_thr - 128) if byte == 3 else (prefix * 256 + digit_thr)
        above = above + above_here
        target = target - above_here
    thr = prefix
    need = TOPK - above

    acc_s[...] = jnp.zeros(acc_s.shape, f32)

    far_bias = bias_ref[B_NEAR, 0:1, :]

    def attend(far, kb, carry):
        eq_seen, m_old, l_old = carry
        keys = sk[kb]
        eq = keys == thr
        eq_f = jnp.where(eq, 1.0, 0.0)
        before = jnp.dot(tri_ref[...], eq_f.astype(jnp.bfloat16), preferred_element_type=f32)
        take_eq = jnp.where(eq, jnp.where(eq_seen + before < need, 1.0, 0.0), 0.0)
        take = jnp.where(keys > thr, 1.0, take_eq)
        mask_add = jnp.where(admissible(kb), jnp.where(take > 0.0, 0.0, NEG), NEG)
        cblk = ckvn[pl.ds(pl.multiple_of(kb * KB, KB), KB), :]
        dots = lax.dot_general(cblk, ql[...], _NT, preferred_element_type=f32)
        tile = nkb - 1 - kb
        m_parts, l_parts, a_parts = [], [], []
        for g in range(HQ // LANES):
            cols = slice(g * LANES, (g + 1) * LANES)
            qcols = slice((g % (BQ // LANES)) * LANES, (g % (BQ // LANES) + 1) * LANES)
            mo = m_old[:, cols]
            if far:
                s = dots[:, cols] + mask_add[:, qcols]
                shift = far_bias[:, cols]
                mn = jnp.maximum(mo, jnp.max(_rows8(s, jnp.maximum), axis=0, keepdims=True) + shift)
                e = jnp.exp2(s - (mn - shift))
            else:
                s = dots[:, cols] + bias_ref[tile, :, cols] + mask_add[:, qcols]
                mn = jnp.maximum(mo, jnp.max(_rows8(s, jnp.maximum), axis=0, keepdims=True))
                e = jnp.exp2(s - mn)
            a = jnp.exp2(mo - mn)
            l_parts.append(a * l_old[:, cols] + jnp.sum(_rows8(e, jnp.add), axis=0, keepdims=True))
            m_parts.append(mn)
            a_parts.append(a)
            p_s[:, cols] = e.astype(p_s.dtype)
        alpha = jnp.concatenate(a_parts, axis=1)
        acc_s[...] = acc_s[...] * alpha + lax.dot_general(cblk, p_s[...], _TN, preferred_element_type=f32)
        eq_seen = eq_seen + jnp.sum(_rows8(eq_f, jnp.add), axis=0, keepdims=True)
        return eq_seen, jnp.concatenate(m_parts, axis=1), jnp.concatenate(l_parts, axis=1)

    init = (jnp.zeros((1, BQ), f32), jnp.full((1, HQ), 4 * NEG, f32), jnp.zeros((1, HQ), f32))
    n_far = jnp.maximum(nkb - B_NEAR, 0)
    carry = lax.fori_loop(0, n_far, functools.partial(attend, True), init)
    _, _, l_fin = lax.fori_loop(n_far, nkb, functools.partial(attend, False), carry)

    o_lat_t = acc_s[...] / l_fin
    for p in range(B_HEADS // 2):
        out = jnp.zeros((BQ, LANES), f32)
        for h in (2 * p, 2 * p + 1):
            o_lat = o_lat_t[:, h * BQ:(h + 1) * BQ].T.astype(jnp.bfloat16)
            out = out + jnp.dot(o_lat, wuv_ref[h], preferred_element_type=f32)
        o_ref[:, p * LANES:(p + 1) * LANES] = out.astype(o_ref.dtype)


def _attn_b(proj3, gain, wuk, wuv, bias, tri, nbatch=BATCH):
    nq = SEQ // BQ
    xw = IN_A - X_COL0
    return pl.pallas_call(
        _attn_b_kernel,
        grid=(nbatch, nq),
        in_specs=[
            pl.BlockSpec((None, BQ, B_W), lambda b, i: (b, i, O_QB // B_W)),
            pl.BlockSpec((None, BQ, xw), lambda b, i: (b, i, X_COL0 // xw)),
            pl.BlockSpec((None, SEQ, LANES), lambda b, i: (b, 0, O_KI // LANES)),
            pl.BlockSpec((None, SEQ, LANES), lambda b, i: (b, 0, O_CKV // LANES)),
            pl.BlockSpec((1, B_LATENT), lambda b, i: (0, 0)),
            pl.BlockSpec((B_HEADS, LANES, B_LATENT), lambda b, i: (0, 0, 0)),
            pl.BlockSpec((B_HEADS, B_LATENT, LANES), lambda b, i: (0, 0, 0)),
            pl.BlockSpec((B_NEAR + 1, KB, HQ), lambda b, i: (0, 0, 0), pipeline_mode=pl.Buffered(1)),
            pl.BlockSpec((KB, KB), lambda b, i: (0, 0)),
        ],
        out_specs=pl.BlockSpec((None, BQ, B_W), lambda b, i: (b, i, 0)),
        out_shape=jax.ShapeDtypeStruct((nbatch, SEQ, B_W), jnp.bfloat16),
        scratch_shapes=[
            pltpu.VMEM((SEQ, B_LATENT), jnp.bfloat16),
            pltpu.VMEM((SEQ, LANES), jnp.bfloat16),
            pltpu.VMEM((HQ, B_LATENT), jnp.bfloat16),
            pltpu.VMEM((HQ, LANES), jnp.bfloat16),
            pltpu.VMEM((NKB, KB, BQ), jnp.int32),
            pltpu.VMEM((NKB, KB, BQ), jnp.bfloat16),
            pltpu.VMEM((B_LATENT, HQ), jnp.float32),
            pltpu.VMEM((KB, HQ), jnp.bfloat16),
        ],
        compiler_params=_cparams(("arbitrary", "arbitrary")),
        name="attn_b",
    )(proj3, proj3, proj3, proj3, gain, wuk, wuv, bias, tri)


MERGE_BN = 1024
MERGE_BM = 1024


def _merge_kernel(oa_ref, ob_ref, oc_ref, ga_ref, gb_ref, gc_ref, wa_ref, wb_ref, wc_ref,
                  wo_in, wq_in, wmo_in, o_ref, wo_out, wq_out, wmo_out, wa, wb, wc):
    f32 = jnp.float32

    @pl.when(pl.program_id(1) == 0)
    def _():
        wa[...] = wa_ref[...].astype(wa.dtype)
        wb[...] = wb_ref[...].astype(wb.dtype)
        wc[...] = wc_ref[...].astype(wc.dtype)

    for src, dst in ((wo_in, wo_out), (wq_in, wq_out), (wmo_in, wmo_out)):
        dst[...] = src[...].astype(dst.dtype)

    m = ga_ref[...].astype(f32) * jnp.dot(oa_ref[...], wa[...], preferred_element_type=f32)
    m = m + gb_ref[...].astype(f32) * jnp.dot(ob_ref[...], wb[...], preferred_element_type=f32)
    m = m + gc_ref[...].astype(f32) * jnp.dot(oc_ref[...], wc[...], preferred_element_type=f32)
    o_ref[...] = m.astype(o_ref.dtype)


def _merge(oa, ob, oc, gates, w_branch, layer, next_weights):
    t = oa.shape[0]
    bn, bm = MERGE_BN, MERGE_BM
    gstep = D_MODEL // bn
    ni = t // bm
    nsteps = (D_MODEL // bn) * ni
    slabs = [w.shape[1] // nsteps for w in next_weights]
    assert all(s * nsteps == w.shape[1] and s % BF16_ROWS == 0 for s, w in zip(slabs, next_weights))
    side_in = [pl.BlockSpec((None, s, w.shape[2]), lambda j, i: (layer, j * ni + i, 0))
               for s, w in zip(slabs, next_weights)]
    side_out = [pl.BlockSpec((s, w.shape[2]), lambda j, i: (j * ni + i, 0)) for s, w in zip(slabs, next_weights)]
    res = pl.pallas_call(
        _merge_kernel,
        grid=(D_MODEL // bn, ni),
        in_specs=[
            pl.BlockSpec((bm, A_W), lambda j, i: (i, 0)),
            pl.BlockSpec((bm, B_W), lambda j, i: (i, 0)),
            pl.BlockSpec((bm, C_W), lambda j, i: (i, 0)),
            pl.BlockSpec((bm, bn), lambda j, i: (i, j)),
            pl.BlockSpec((bm, bn), lambda j, i: (i, gstep + j)),
            pl.BlockSpec((bm, bn), lambda j, i: (i, 2 * gstep + j)),
            pl.BlockSpec((None, A_W, bn), lambda j, i: (layer, 0, j)),
            pl.BlockSpec((None, B_W, bn), lambda j, i: (layer, A_W // B_W, j)),
            pl.BlockSpec((None, C_W, bn), lambda j, i: (layer, (A_W + B_W) // C_W, j)),
        ] + side_in,
        out_specs=[pl.BlockSpec((bm, bn), lambda j, i: (i, j))] + side_out,
        out_shape=[jax.ShapeDtypeStruct((t, D_MODEL), jnp.bfloat16)]
        + [jax.ShapeDtypeStruct(w.shape[1:], jnp.bfloat16) for w in next_weights],
        scratch_shapes=[pltpu.VMEM((A_W, bn), jnp.bfloat16), pltpu.VMEM((B_W, bn), jnp.bfloat16),
                        pltpu.VMEM((C_W, bn), jnp.bfloat16)],
        compiler_params=_cparams(("arbitrary", "arbitrary")),
        name="merge",
    )(oa, ob, oc, gates, gates, gates, w_branch, w_branch, w_branch, *next_weights)
    return res[0], res[1:]


FINISH_CHUNKS = 4


def _finish_chunked(y_of_rows, nrows, x_ref, gp_ref, gn_ref, xo_ref, ho_ref):
    step = nrows // FINISH_CHUNKS
    for r in range(0, nrows, step):
        rows = slice(r, r + step)
        xn = x_ref[rows, :] + _rms(y_of_rows(rows), gp_ref[...])
        xo_ref[rows, :] = xn
        if ho_ref is not None:
            ho_ref[rows, :] = _rms(xn, gn_ref[...]).astype(ho_ref.dtype)


DOWN_BM = 1024


def _ffn_down_kernel(nk, with_next, a_ref, w_ref, x_ref, gp_ref, gn_ref, xo_ref, *rest):
    ho_ref = rest[0] if with_next else None
    acc = rest[-1]
    f32 = jnp.float32
    k = pl.program_id(1)
    half = acc.shape[0] // 2

    @pl.when(k == 0)
    def _():
        acc[...] = jnp.dot(a_ref[...], w_ref[...], preferred_element_type=f32)

    @pl.when((k > 0) & (k < nk - 1))
    def _():
        acc[...] += jnp.dot(a_ref[...], w_ref[...], preferred_element_type=f32)

    @pl.when(k == nk - 1)
    def _():
        acc[half:, :] += jnp.dot(a_ref[half:, :], w_ref[...], preferred_element_type=f32)
        _finish_chunked(lambda rows: acc[rows, :] + jnp.dot(a_ref[rows, :], w_ref[...], preferred_element_type=f32),
                        half, x_ref, gp_ref, gn_ref, xo_ref, ho_ref)

    @pl.when(k == nk)
    def _():
        _finish_chunked(lambda rows: acc[half + rows.start:half + rows.stop, :],
                        half, x_ref, gp_ref, gn_ref, xo_ref, ho_ref)


def _ffn_down(a, w, x, g_post, g_next, bk):
    t, kdim = a.shape
    n = w.shape[1]
    nk = kdim // bk
    bm = DOWN_BM
    half = bm // 2
    with_next = g_next is not None
    if g_next is None:
        g_next = g_post
    last = nk - 1
    res_block = pl.BlockSpec((half, n), lambda i, k: (2 * i + jnp.where(k == nk, 1, 0), 0))
    out_shape = [jax.ShapeDtypeStruct((t, n), jnp.float32)]
    out_specs = [res_block]
    if with_next:
        out_shape.append(jax.ShapeDtypeStruct((t, n), jnp.bfloat16))
        out_specs.append(res_block)
    res = pl.pallas_call(
        functools.partial(_ffn_down_kernel, nk, with_next),
        grid=(t // bm, nk + 1),
        in_specs=[
            pl.BlockSpec((bm, bk), lambda i, k: (i, jnp.minimum(k, last))),
            pl.BlockSpec((bk, n), lambda i, k: (jnp.minimum(k, last), 0)),
            res_block,
            pl.BlockSpec((1, n), lambda i, k: (0, 0)),
            pl.BlockSpec((1, n), lambda i, k: (0, 0)),
        ],
        out_specs=out_specs,
        out_shape=out_shape,
        scratch_shapes=[pltpu.VMEM((bm, n), jnp.float32)],
        compiler_params=_cparams(("arbitrary", "arbitrary")),
        name="ffn_down",
    )(a, w, x, g_post, g_next)
    return (res[0], res[1]) if with_next else (res[0], None)


def _cast_once(pairs):
    @pl.when(pl.program_id(0) == 0)
    def _():
        for src, dst in pairs:
            dst[...] = src[...].astype(dst.dtype)


def _memkv_kernel(m_ref, g_ref, w_ref, o_ref, wb):
    _cast_once([(w_ref, wb)])
    mn = _rms(m_ref[...], g_ref[...]).astype(jnp.bfloat16)
    o_ref[...] = jnp.dot(mn, wb[...], preferred_element_type=jnp.float32).astype(o_ref.dtype)


def _memkv(mem2, g, w, layer):
    t, d = mem2.shape
    n = w.shape[2]
    bm = 512
    return pl.pallas_call(
        _memkv_kernel,
        grid=(t // bm,),
        in_specs=[pl.BlockSpec((bm, d), lambda i: (i, 0)), pl.BlockSpec((1, d), lambda i: (0, 0)),
                  pl.BlockSpec((None, d, n), lambda i: (layer, 0, 0))],
        out_specs=pl.BlockSpec((bm, n), lambda i: (i, 0)),
        out_shape=jax.ShapeDtypeStruct((t, n), jnp.bfloat16),
        scratch_shapes=[pltpu.VMEM((d, n), jnp.bfloat16)],
        compiler_params=_cparams(("arbitrary",)),
        name="mem_kv",
    )(mem2, g, w)


XA_BM = 512


def _mix_out_xattn_kernel(m_ref, wo_ref, wq_ref, kv_ref, wmo_ref, x_ref, g1_ref, g2_ref, g3_ref, g4_ref,
                          xo_ref, ho_ref, x1_s, h1_s):
    f32 = jnp.float32
    bm = m_ref.shape[0]
    _finish_chunked(lambda rows: jnp.dot(m_ref[rows, :], wo_ref[...], preferred_element_type=f32),
                    bm, x_ref, g1_ref, g2_ref, x1_s, h1_s)
    q = jnp.dot(h1_s[...], wq_ref[...], preferred_element_type=f32).astype(jnp.bfloat16)
    scale = MEM_HEAD_DIM ** -0.5 * LOG2E
    hcols = [slice(h * LANES, (h + 1) * LANES) for h in range(MEM_HEADS)]
    ss = [lax.dot_general(q[:, c], kv_ref[:, c], _NT, preferred_element_type=f32) * scale for c in hcols]
    ms = [jnp.max(s, axis=-1, keepdims=True) for s in ss]
    es = [jnp.exp2(s - m) for s, m in zip(ss, ms)]
    ls = [jnp.sum(e, axis=-1, keepdims=True) for e in es]
    pv = [jnp.dot(e.astype(jnp.bfloat16), kv_ref[:, MEM_W + h * LANES:MEM_W + (h + 1) * LANES],
                  preferred_element_type=f32) for h, e in enumerate(es)]
    o = jnp.concatenate([(o_h / l).astype(jnp.bfloat16) for o_h, l in zip(pv, ls)], axis=-1)
    _finish_chunked(lambda rows: jnp.dot(o[rows, :], wmo_ref[...], preferred_element_type=f32),
                    bm, x1_s, g3_ref, g4_ref, xo_ref, ho_ref)


def _mix_out_xattn(merged, w_o, w_mq, kv, w_mo, layer, x, g1, g2, g3, g4):
    t, d = merged.shape
    bm = XA_BM
    per_batch = SEQ // bm
    once = pl.Buffered(1)
    row = pl.BlockSpec((bm, d), lambda i: (i, 0))
    gain = pl.BlockSpec((1, d), lambda i: (0, 0))
    return pl.pallas_call(
        _mix_out_xattn_kernel,
        grid=(t // bm,),
        in_specs=[
            row,
            pl.BlockSpec((None, d, d), lambda i: (layer, 0, 0), pipeline_mode=once),
            pl.BlockSpec((None, d, MEM_W), lambda i: (layer, 0, 0), pipeline_mode=once),
            pl.BlockSpec((MEM_LEN, 2 * MEM_W), lambda i: (i // per_batch, 0)),
            pl.BlockSpec((None, MEM_W, d), lambda i: (layer, 0, 0), pipeline_mode=once),
            row, gain, gain, gain, gain,
        ],
        out_specs=[row, row],
        out_shape=[jax.ShapeDtypeStruct((t, d), jnp.float32), jax.ShapeDtypeStruct((t, d), jnp.bfloat16)],
        scratch_shapes=[pltpu.VMEM((bm, d), jnp.float32), pltpu.VMEM((bm, d), jnp.bfloat16)],
        compiler_params=_cparams(("parallel",)),
        name="mix_out_xattn",
    )(merged, w_o, w_mq, kv, w_mo, x, g1, g2, g3, g4)


def _ffn_up_kernel(h_ref, wgf_ref, wvf_ref, cwg_ref, cwv_ref, cbg_ref, cbv_ref, wd_ref, o_ref, wdo_ref,
                   wg_ref, wv_ref, ug, uv):
    j = pl.program_id(0)
    i = pl.program_id(1)

    slab = j * pl.num_programs(1) + i

    @pl.when(slab < D_FF // wd_ref.shape[0])
    def _():
        wdo_ref[...] = wd_ref[...].astype(wdo_ref.dtype)

    @pl.when(slab >= D_FF // wd_ref.shape[0])
    def _():
        wdo_ref[...] = jnp.zeros(wdo_ref.shape, wdo_ref.dtype)

    tiles_per_seq = SEQ // FF_BM
    last = FF_P // FF_BN - 1
    valid = D_FF - last * FF_BN
    shift = FF_BN - valid

    @pl.when((i == 0) & (j < last))
    def _():
        wg_ref[...] = wgf_ref[...].astype(wg_ref.dtype)
        wv_ref[...] = wvf_ref[...].astype(wv_ref.dtype)

    @pl.when((i == 0) & (j == last))
    def _():
        zeros = jnp.zeros((wg_ref.shape[0], FF_BN - valid), wg_ref.dtype)
        wg_ref[:, :valid] = wgf_ref[:, :valid].astype(wg_ref.dtype)
        wv_ref[:, :valid] = wvf_ref[:, shift:].astype(wv_ref.dtype)
        wg_ref[:, valid:] = zeros
        wv_ref[:, valid:] = zeros

    for u in (ug, uv):
        @pl.when(i % tiles_per_seq == 0)
        def _():
            u[0:SUBLANES, :] = jnp.zeros((SUBLANES, FF_BN), jnp.float32)

        @pl.when(i % tiles_per_seq != 0)
        def _():
            u[0:SUBLANES, :] = u[FF_BM:FF_BM + SUBLANES, :]

    def conv(u, cw_ref, cb_ref, r0):
        base = SUBLANES + r0
        acc = cb_ref[...] + u[base - 2:base - 2 + FF_CH, :] * cw_ref[0:1, :]
        acc = acc + u[base - 1:base - 1 + FF_CH, :] * cw_ref[1:2, :]
        return acc + u[base:base + FF_CH, :] * cw_ref[2:3, :]

    for c in range(FF_BM // FF_CH):
        r0 = c * FF_CH
        hb = h_ref[r0:r0 + FF_CH, :]
        ug[SUBLANES + r0:SUBLANES + r0 + FF_CH, :] = jnp.dot(hb, wg_ref[...], preferred_element_type=jnp.float32)
        uv[SUBLANES + r0:SUBLANES + r0 + FF_CH, :] = jnp.dot(hb, wv_ref[...], preferred_element_type=jnp.float32)
        gate = conv(ug, cwg_ref, cbg_ref, r0)
        val = conv(uv, cwv_ref, cbv_ref, r0)
        o_ref[r0:r0 + FF_CH, :] = (jax.nn.gelu(gate) * val).astype(o_ref.dtype)


def _ffn_up(h, w_up, w_down, layer, conv_w, conv_b):
    t, d = h.shape
    nj = FF_P // FF_BN
    ni = t // FF_BM
    slab = FF_P // (nj * ni)
    assert slab * nj * ni == FF_P and D_FF % slab == 0 and slab % BF16_ROWS == 0
    n_real = D_FF // slab
    w_block = (None, pl.Element(d), pl.Element(FF_BN))
    return pl.pallas_call(
        _ffn_up_kernel,
        grid=(nj, ni),
        in_specs=[
            pl.BlockSpec((FF_BM, d), lambda j, i: (i, 0)),
            pl.BlockSpec(w_block, lambda j, i: (layer, 0, pl.multiple_of(j * FF_BN, LANES))),
            pl.BlockSpec(w_block, lambda j, i: (
                layer, 0, pl.multiple_of(jnp.minimum(D_FF + j * FF_BN, 2 * D_FF - FF_BN), LANES))),
            pl.BlockSpec((CONV_W, FF_BN), lambda j, i: (0, j)),
            pl.BlockSpec((CONV_W, FF_BN), lambda j, i: (0, nj + j)),
            pl.BlockSpec((1, FF_BN), lambda j, i: (0, j)),
            pl.BlockSpec((1, FF_BN), lambda j, i: (0, nj + j)),
            pl.BlockSpec((None, slab, d), lambda j, i: (layer, jnp.minimum(j * ni + i, n_real - 1), 0)),
        ],
        out_specs=[pl.BlockSpec((FF_BM, FF_BN), lambda j, i: (i, j)),
                   pl.BlockSpec((slab, d), lambda j, i: (j * ni + i, 0))],
        out_shape=[jax.ShapeDtypeStruct((t, FF_P), jnp.bfloat16),
                   jax.ShapeDtypeStruct((FF_P, w_down.shape[2]), jnp.bfloat16)],
        scratch_shapes=[pltpu.VMEM((d, FF_BN), jnp.bfloat16), pltpu.VMEM((d, FF_BN), jnp.bfloat16),
                        pltpu.VMEM((FF_BM + SUBLANES, FF_BN), jnp.float32),
                        pltpu.VMEM((FF_BM + SUBLANES, FF_BN), jnp.float32)],
        compiler_params=_cparams(("arbitrary", "arbitrary")),
        name="ffn_up",
    )(h, w_up, w_up, conv_w, conv_w, conv_b, conv_b, w_down)


def _pad_heads(w, axis):
    h = w.shape[0]
    zero = jnp.zeros_like(w)
    even = jnp.concatenate([w, zero], axis=axis)
    odd = jnp.concatenate([zero, w], axis=axis)
    sel = (jnp.arange(h) % 2 == 0).reshape((h, 1, 1))
    return jnp.where(sel, even, odd)


def _toeplitz(fn, rows, cols):
    ks = np.concatenate([np.arange(0, cols), np.arange(-(rows - 1), 0)])
    w = fn(ks)
    h, period = w.shape
    x = jnp.tile(w, (1, rows))[:, :rows * (period - 1)].reshape(h, rows, period - 1)
    return x[:, :, :cols].astype(jnp.float32)


def _band(rows, cols, left):
    diff = left + np.arange(rows)[:, None] // CHUNK - np.arange(cols)[None, :] // CHUNK
    return (diff >= 0) & (diff <= left)


def _bias_a(rel_bias):
    fn = lambda ks: rel_bias[np.clip(A_PAD - ks, -A_MAX_REL, A_MAX_REL) + A_MAX_REL].T
    bias = jnp.where(_band(QB, A_WIN, A_LEFT_CHUNKS)[None], _toeplitz(fn, QB, A_WIN) * LOG2E, NEG)
    return bias.reshape(A_HEADS * QB, A_WIN)


def _bias_c(t5_c):
    fn = lambda ks: t5_c[_t5_bucket(jnp.asarray(ks - C_PAD, jnp.int32))].T
    bias = jnp.where(_band(QB, C_WIN, C_LEFT_CHUNKS)[None], _toeplitz(fn, QB, C_WIN) * LOG2E, NEG)
    return bias.reshape(C_Q_HEADS * QB, C_WIN)


def _bias_b(t5_b):
    tiles = []
    for n in range(B_NEAR + 1):
        off = KB * (n - (B_QK - 1)) if n < B_NEAR else SEQ
        fn = lambda ks, off=off: t5_b[_t5_bucket(jnp.asarray(-ks - off, jnp.int32))].T
        tile = _toeplitz(fn, KB, BQ)
        tiles.append(jnp.transpose(tile * LOG2E, (1, 0, 2)).reshape(KB, HQ))
    return jnp.stack(tiles)


def _pad_ff(a, dtype):
    z = jnp.zeros((a.shape[0], FF_P - D_FF), dtype)
    return jnp.concatenate([a[:, :D_FF].astype(dtype), z, a[:, D_FF:].astype(dtype), z], axis=1)


def kernel(x, mem, t5_table, norm_gains, w_in, a_rel_bias, ckv_gain, w_uk, w_uv, sinks, w_branch, w_o,
           mem_gain, w_mq, w_mkv, w_mo, w_up, conv_w, conv_b, w_down):
    bf16 = jnp.bfloat16
    xs = x.reshape(TOKENS, D_MODEL)
    mem2 = mem.reshape(BATCH * MEM_LEN, D_MODEL)
    tri = jnp.asarray(np.tril(np.ones((KB, KB), np.float32), -1), bf16)
    bias_b = _bias_b(t5_table[:, :B_HEADS])
    bias_c = _bias_c(t5_table[:, B_HEADS:])
    gains = norm_gains.reshape(DEPTH, 6, 1, D_MODEL)
    w_in_t = jnp.swapaxes(w_in, 1, 2)

    h = _norm(xs, gains[0, 0])
    for l in range(DEPTH):
        g = gains[l]
        proja = _proj(h, w_in_t, l, 0, IN_A, 1024, gate=False, name="in_proj_a")
        projc = _proj(h, w_in_t, l, O_QC, C_COLS, C_COLS, gate=False, name="in_proj_c")
        gates = _proj(h, w_in_t, l, O_GL, 3 * D_MODEL, 1024, gate=True, name="in_proj_g")
        proja3 = proja.reshape(BATCH, SEQ, IN_A)
        oa = _attn_a(proja3, _bias_a(a_rel_bias[l]))
        wuk = jnp.transpose(_pad_heads(w_uk[l], axis=2), (0, 2, 1)).astype(bf16)
        wuv = _pad_heads(w_uv[l], axis=2).astype(bf16)
        ob = _attn_b(proja3, ckv_gain[l].reshape(1, B_LATENT), wuk, wuv, bias_b, tri)
        oc = _attn_c(sinks[l], projc.reshape(BATCH, SEQ, C_COLS), bias_c)
        merged, (w_o_b, w_mq_b, w_mo_b) = _merge(oa.reshape(TOKENS, A_W), ob.reshape(TOKENS, B_W),
                                                 oc.reshape(TOKENS, C_W), gates, w_branch, l, (w_o, w_mq, w_mo))
        kv = _memkv(mem2, mem_gain[l].reshape(1, D_MODEL), w_mkv, l)
        xs, h = _mix_out_xattn(merged, w_o_b[None], w_mq_b[None], kv, w_mo_b[None], 0, xs, g[1], g[2], g[3], g[4])
        hidden, w_dn_b = _ffn_up(h, w_up, w_down, l, _pad_ff(conv_w[l], jnp.float32),
                                 _pad_ff(conv_b[l].reshape(1, -1), jnp.float32))
        g_next = gains[l + 1, 0] if l + 1 < DEPTH else None
        xs, h = _ffn_down(hidden, w_dn_b, xs, g[5], g_next, DOWN_BK)
    return xs.reshape(BATCH, SEQ, D_MODEL)
```

```python
import functools
import math

import numpy as np
import jax
import jax.numpy as jnp
from jax import lax
from jax.experimental import pallas as pl
from jax.experimental.pallas import tpu as pltpu

D_MODEL = 2048
BATCH = 4
SEQ = 2048
DEPTH = 2
TOKENS = BATCH * SEQ
CHUNK = 64
EPS = 1e-6
NEG = -1e30
LOG2E = math.log2(math.e)
A_HEADS = 8
A_LEFT_CHUNKS = 8
A_MAX_REL = 128
A_W = 512
B_HEADS = 8
B_W = 512
B_LATENT = 128
IDX_HEADS = 8
IDX_DIM = 64
TOPK = 256
C_Q_HEADS = 16
C_GROUP = 8
C_W = 1024
C_LEFT_CHUNKS = 2
T5_BUCKETS = 32
T5_MAX_DIST = 256
MEM_LEN = 256
MEM_HEADS = 4
MEM_HEAD_DIM = 128
MEM_W = 512
D_FF = 5504
CONV_W = 3

LANES = 128
SUBLANES = 8
HEAD_DIM = 64
QB = 128
KB = 256
VMEM_LIMIT = 56 * 1024 * 1024

O_QA, O_KA, O_VA, O_QB = 0, 512, 1024, 1536
O_CKV = 2048
O_QI = 2176
O_KI = 2688
O_WI = 2752
O_QC = 2760
O_GL = 4040
IN_W = O_GL + 3 * D_MODEL
IN_A = 3072
X_COL0 = 2048
C_COLS = O_GL - O_QC
PROJ_BM = 1024
PROJ_TR = 256

FF_P = 5632
FF_BN = 512
FF_BM = 2048
FF_CH = 256
DOWN_BK = 1408

A_WIN = (A_LEFT_CHUNKS + 2) * CHUNK
C_WIN = (C_LEFT_CHUNKS + 2) * CHUNK
A_PAD = A_LEFT_CHUNKS * CHUNK
C_PAD = C_LEFT_CHUNKS * CHUNK
BQ = 256
B_QK = BQ // KB
B_NEAR = B_QK + 1
NKB = SEQ // KB
HQ = B_HEADS * BQ

_NT = (((1,), (1,)), ((), ()))
_TN = (((0,), (0,)), ((), ()))


def _cparams(sem):
    return pltpu.CompilerParams(dimension_semantics=sem, vmem_limit_bytes=VMEM_LIMIT)


def _t5_bucket(rel):
    half = T5_BUCKETS // 2
    max_exact = half // 2
    sign = jnp.where(rel > 0, half, 0)
    d = jnp.abs(rel)
    d_f = jnp.maximum(d, 1).astype(jnp.float32)
    large = max_exact + (jnp.log(d_f / max_exact) / math.log(T5_MAX_DIST / max_exact) * (half - max_exact)).astype(jnp.int32)
    large = jnp.minimum(large, half - 1)
    return sign + jnp.where(d < max_exact, d, large)


def _far_bucket_is_constant():
    assert BQ % KB == 0
    d = np.arange(KB + 1, SEQ, dtype=np.float32)
    assert d[0] > T5_MAX_DIST
    large = 8 + (np.log(d / 8) / math.log(T5_MAX_DIST / 8) * 8).astype(np.int32)
    return bool(np.all(np.minimum(large, 15) == 15))


assert _far_bucket_is_constant()


def _rms(v, g):
    return v * lax.rsqrt(jnp.mean(v * v, axis=-1, keepdims=True) + EPS) * g


def _norm_kernel(x_ref, g_ref, o_ref):
    o_ref[...] = _rms(x_ref[...], g_ref[...]).astype(o_ref.dtype)


def _norm(x, g, bm=1024):
    t, d = x.shape
    return pl.pallas_call(
        _norm_kernel,
        grid=(t // bm,),
        in_specs=[pl.BlockSpec((bm, d), lambda i: (i, 0)), pl.BlockSpec((1, d), lambda i: (0, 0))],
        out_specs=pl.BlockSpec((bm, d), lambda i: (i, 0)),
        out_shape=jax.ShapeDtypeStruct((t, d), jnp.bfloat16),
        compiler_params=_cparams(("parallel",)),
        name="rmsnorm",
    )(x, g)


def _proj_kernel(gate, h_ref, wt_ref, o_ref, wb):
    @pl.when(pl.program_id(1) == 0)
    def _():
        for c in range(wt_ref.shape[0] // PROJ_TR):
            rows = slice(c * PROJ_TR, (c + 1) * PROJ_TR)
            wb[:, rows] = wt_ref[rows, :].T.astype(wb.dtype)

    acc = jnp.dot(h_ref[...], wb[...], preferred_element_type=jnp.float32)
    o_ref[...] = (jax.nn.sigmoid(acc) if gate else acc).astype(o_ref.dtype)


def _proj(h, w_t, layer, col0, n, bn, gate, name):
    t, k = h.shape
    assert col0 % SUBLANES == 0 and bn % SUBLANES == 0
    w_spec = pl.BlockSpec((None, pl.Element(bn), pl.Element(k)),
                          lambda j, i: (layer, pl.multiple_of(col0 + j * bn, SUBLANES), 0))
    return pl.pallas_call(
        functools.partial(_proj_kernel, gate),
        grid=(n // bn, t // PROJ_BM),
        in_specs=[pl.BlockSpec((PROJ_BM, k), lambda j, i: (i, 0)), w_spec],
        out_specs=pl.BlockSpec((PROJ_BM, bn), lambda j, i: (i, j)),
        out_shape=jax.ShapeDtypeStruct((t, n), jnp.bfloat16),
        scratch_shapes=[pltpu.VMEM((k, bn), jnp.bfloat16)],
        compiler_params=_cparams(("arbitrary", "arbitrary")),
        name=name,
    )(h, w_t)


def _lane_masks(dtype):
    lane = lax.broadcasted_iota(jnp.int32, (1, LANES), 1)
    lo = (lane < HEAD_DIM).astype(jnp.float32)
    return lo.astype(dtype), (1.0 - lo).astype(dtype)


def _swap_halves(x):
    return pltpu.roll(x.astype(jnp.float32), HEAD_DIM, 1).astype(x.dtype)


def _fill_padded(src, dst_ref, pad):
    dst_ref[0:pad, :] = jnp.zeros((pad, dst_ref.shape[1]), dst_ref.dtype)
    dst_ref[pad:pad + SEQ, :] = src


def _attn_a_kernel(q_ref, k_ref, v_ref, bias_ref, o_ref, kpad, vpad):
    i = pl.program_id(1)

    @pl.when(i == 0)
    def _():
        _fill_padded(k_ref[...], kpad, A_PAD)
        _fill_padded(v_ref[...], vpad, A_PAD)

    start = pl.multiple_of(i * QB, QB)
    kw = kpad[pl.ds(start, A_WIN), :]
    vw = vpad[pl.ds(start, A_WIN), :]
    mlo, mhi = _lane_masks(jnp.bfloat16)
    lane = lax.broadcasted_iota(jnp.int32, (QB, LANES), 1)
    kchunk = lax.broadcasted_iota(jnp.int32, (QB, A_WIN), 1) // CHUNK
    pad_mask = jnp.where(kchunk >= A_LEFT_CHUNKS - 2 * i, 0.0, NEG)
    scale = HEAD_DIM ** -0.5 * LOG2E
    pad2 = jnp.concatenate([pad_mask, pad_mask], axis=0)
    npair = A_HEADS // 2
    pcols = [slice(p * LANES, (p + 1) * LANES) for p in range(npair)]
    ss = []
    for p in range(npair):
        qp = q_ref[:, pcols[p]]
        q2 = jnp.concatenate([qp * mlo, qp * mhi], axis=0)
        s = lax.dot_general(q2, kw[:, pcols[p]], _NT, preferred_element_type=jnp.float32)
        ss.append(s * scale + bias_ref[2 * p * QB:(2 * p + 2) * QB, :] + pad2)
    ms = [jnp.max(s, axis=-1, keepdims=True) for s in ss]
    es = [jnp.exp2(s - m) for s, m in zip(ss, ms)]
    ls = [jnp.sum(e, axis=-1, keepdims=True) for e in es]
    outs = [jnp.dot(e.astype(jnp.bfloat16), vw[:, pcols[p]], preferred_element_type=jnp.float32)
            for p, e in enumerate(es)]
    for p in range(npair):
        o = outs[p] / ls[p]
        o_ref[:, pcols[p]] = jnp.where(lane < HEAD_DIM, o[:QB], o[QB:]).astype(o_ref.dtype)


def _attn_a(proj3, bias):
    nq = SEQ // QB
    return pl.pallas_call(
        _attn_a_kernel,
        grid=(BATCH, nq),
        in_specs=[
            pl.BlockSpec((None, QB, A_W), lambda b, i: (b, i, O_QA // A_W)),
            pl.BlockSpec((None, SEQ, A_W), lambda b, i: (b, 0, O_KA // A_W)),
            pl.BlockSpec((None, SEQ, A_W), lambda b, i: (b, 0, O_VA // A_W)),
            pl.BlockSpec((A_HEADS * QB, A_WIN), lambda b, i: (0, 0)),
        ],
        out_specs=pl.BlockSpec((None, QB, A_W), lambda b, i: (b, i, 0)),
        out_shape=jax.ShapeDtypeStruct((BATCH, SEQ, A_W), jnp.bfloat16),
        scratch_shapes=[pltpu.VMEM((SEQ + A_PAD, A_W), jnp.bfloat16), pltpu.VMEM((SEQ + A_PAD, A_W), jnp.bfloat16)],
        compiler_params=_cparams(("arbitrary", "arbitrary")),
        name="attn_a",
    )(proj3, proj3, proj3, bias)


def _attn_c_kernel(sink_ref, q_ref, k_ref, v_ref, bias_ref, o_ref, kpad, kswp, vpad, vswp):
    i = pl.program_id(1)

    @pl.when(i == 0)
    def _():
        k = k_ref[...]
        v = v_ref[...]
        _fill_padded(k, kpad, C_PAD)
        _fill_padded(_swap_halves(k), kswp, C_PAD)
        _fill_padded(v, vpad, C_PAD)
        _fill_padded(_swap_halves(v), vswp, C_PAD)

    start = pl.multiple_of(i * QB, QB)
    mlo, mhi = _lane_masks(jnp.bfloat16)
    lane = lax.broadcasted_iota(jnp.int32, (QB, LANES), 1)
    kchunk = lax.broadcasted_iota(jnp.int32, (QB, C_WIN), 1) // CHUNK
    pad_mask = jnp.where(kchunk >= C_LEFT_CHUNKS - 2 * i, 0.0, NEG)
    scale = HEAD_DIM ** -0.5 * LOG2E
    qs = [q_ref[:, p * LANES:(p + 1) * LANES] for p in range(C_GROUP)]
    npair = C_GROUP // 2
    stacks = []
    for straight in (True, False):
        kref, vref = (kpad, vpad) if straight else (kswp, vswp)
        kw = kref[pl.ds(start, C_WIN), :]
        vw = vref[pl.ds(start, C_WIN), :]
        halves = [int((p >= npair) == straight) for p in range(C_GROUP)]
        qg = jnp.concatenate([qs[p] * (mhi if halves[p] else mlo) for p in range(C_GROUP)], axis=0)
        s_all = lax.dot_general(qg, kw, _NT, preferred_element_type=jnp.float32)
        ps, ls = [], []
        for p in range(C_GROUP):
            h = 2 * p + halves[p]
            s = s_all[p * QB:(p + 1) * QB] * scale + bias_ref[h * QB:(h + 1) * QB, :] + pad_mask
            sink = sink_ref[h] * LOG2E
            m = jnp.maximum(jnp.max(s, axis=-1, keepdims=True), sink)
            e = jnp.exp2(s - m)
            ls.append(jnp.sum(e, axis=-1, keepdims=True) + jnp.exp2(sink - m))
            ps.append(e.astype(jnp.bfloat16))
        o_all = jnp.dot(jnp.concatenate(ps, axis=0), vw, preferred_element_type=jnp.float32)
        stacks.append(([o_all[p * QB:(p + 1) * QB] / ls[p] for p in range(C_GROUP)], halves))
    for p in range(C_GROUP):
        (o1, h1), (o2, _) = stacks
        lo, hi = (o2[p], o1[p]) if h1[p] else (o1[p], o2[p])
        o_ref[:, p * LANES:(p + 1) * LANES] = jnp.where(lane < HEAD_DIM, lo, hi).astype(o_ref.dtype)


def _attn_c(sinks, projc3, bias):
    nq = SEQ // QB
    pad_buf = pltpu.VMEM((SEQ + C_PAD, LANES), jnp.bfloat16)
    return pl.pallas_call(
        _attn_c_kernel,
        grid=(BATCH, nq),
        in_specs=[
            pl.BlockSpec(memory_space=pltpu.SMEM),
            pl.BlockSpec((None, QB, C_W), lambda b, i: (b, i, 0)),
            pl.BlockSpec((None, SEQ, LANES), lambda b, i: (b, 0, C_W // LANES)),
            pl.BlockSpec((None, SEQ, LANES), lambda b, i: (b, 0, C_W // LANES + 1)),
            pl.BlockSpec((C_Q_HEADS * QB, C_WIN), lambda b, i: (0, 0)),
        ],
        out_specs=pl.BlockSpec((None, QB, C_W), lambda b, i: (b, i, 0)),
        out_shape=jax.ShapeDtypeStruct((BATCH, SEQ, C_W), jnp.bfloat16),
        scratch_shapes=[pad_buf, pad_buf, pad_buf, pad_buf],
        compiler_params=_cparams(("arbitrary", "arbitrary")),
        name="attn_c",
    )(sinks, projc3, projc3, projc3, bias)


def _sort_key(x):
    bits = lax.bitcast_convert_type(x + 0.0, jnp.int32)
    return bits ^ ((bits >> 31) & jnp.int32(0x7FFFFFFF))


def _rows_tree(x, op, slab):
    parts = [x[r:r + slab, :] for r in range(0, x.shape[0], slab)]
    while len(parts) > 1:
        nxt = [op(parts[k], parts[k + 1]) for k in range(0, len(parts) - 1, 2)]
        parts = nxt + ([parts[-1]] if len(parts) % 2 else [])
    return parts[0]


def _rows8(x, op):
    return _rows_tree(x, op, SUBLANES)


BF16_ROWS = 16


def _bit_search(count_ge, target, v, above, top_bit):
    def step(t, carry):
        v, above = carry
        cand = v | (jnp.int32(1) << (top_bit - t))
        c = count_ge(cand)
        keep = c >= target
        return jnp.where(keep, cand, v), jnp.where(keep, above, c)

    return lax.fori_loop(0, top_bit + 1, step, (v, above))


def _attn_b_kernel(qb_ref, x_ref, kiw_ref, ckv_ref, gain_ref, wuk_ref, wuv_ref, bias_ref, tri_ref,
                   o_ref, ckvn, kd, ql, qi_all, sk, dg, acc_s, p_s):
    i = pl.program_id(1)
    nkb = (i + 1) * B_QK
    mlo, mhi = _lane_masks(jnp.bfloat16)
    f32 = jnp.float32

    @pl.when(i == 0)
    def _():
        ckvn[...] = _rms(ckv_ref[...].astype(f32), gain_ref[...]).astype(ckvn.dtype)
        kiw = kiw_ref[...].astype(f32)
        lane = lax.broadcasted_iota(jnp.int32, kiw.shape, 1)
        kd[...] = jnp.where(lane < HEAD_DIM, kiw, pltpu.roll(kiw, HEAD_DIM, 1)).astype(kd.dtype)

    for h in range(B_HEADS):
        rows = slice(h * BQ, (h + 1) * BQ)
        qlat = jnp.dot(qb_ref[:, (h // 2) * LANES:(h // 2 + 1) * LANES], wuk_ref[h], preferred_element_type=f32)
        ql[rows, :] = (qlat * (HEAD_DIM ** -0.5 * LOG2E)).astype(ql.dtype)
        qcol = O_QI - X_COL0 + (h // 2) * LANES
        qi_all[rows, :] = x_ref[:, qcol:qcol + LANES] * (mhi if h % 2 else mlo)
    wcol = O_KI - X_COL0
    wi_t = x_ref[:, wcol:wcol + LANES].astype(f32).T * (IDX_HEADS ** -0.5 * IDX_DIM ** -0.5)
    wi_rows = [wi_t[O_WI - O_KI + h:O_WI - O_KI + h + 1, :] for h in range(IDX_HEADS)]

    kpos = lax.broadcasted_iota(jnp.int32, (KB, BQ), 0)
    key_limit = ((i * BQ + lax.broadcasted_iota(jnp.int32, (1, BQ), 1)) // CHUNK + 1) * CHUNK

    def admissible(kb):
        return kb * KB + kpos < key_limit

    def score_block(kb, carry):
        kblk = kd[pl.ds(pl.multiple_of(kb * KB, KB), KB), :]
        dots = lax.dot_general(kblk, qi_all[...], _NT, preferred_element_type=f32)
        score = jnp.zeros((KB, BQ), f32)
        for h in range(IDX_HEADS):
            score = score + jnp.maximum(dots[:, h * BQ:(h + 1) * BQ], 0.0) * wi_rows[h]
        sk[kb] = _sort_key(jnp.where(admissible(kb), score, NEG))
        return carry

    lax.fori_loop(0, nkb, score_block, 0)

    neg_key = _sort_key(jnp.full((1, 1), NEG, f32))
    n_rest = ((NKB - nkb) * KB).astype(f32)
    one, zero = jnp.ones((), jnp.bfloat16), jnp.zeros((), jnp.bfloat16)

    def byte_of(key, byte):
        return ((key >> 24) + 128) if byte == 3 else ((key >> (8 * byte)) & 255)

    target = jnp.full((1, BQ), TOPK, f32)
    prefix = jnp.zeros((1, BQ), jnp.int32)
    above = jnp.zeros((1, BQ), f32)
    for byte in (3, 2, 1, 0):
        def in_class(key, byte=byte, prefix=prefix):
            return (key >> (8 * byte + 8)) == prefix

        def prepare(kb, carry, byte=byte, in_class=in_class):
            key = sk[kb]
            digit = byte_of(key, byte).astype(f32)
            if byte < 3:
                digit = jnp.where(in_class(key), digit, -1.0)
            dg[kb] = digit.astype(dg.dtype)
            return carry

        lax.fori_loop(0, nkb, prepare, 0)
        rest_digit = byte_of(neg_key, byte)
        rest_on = in_class(neg_key) if byte < 3 else (neg_key == neg_key)

        def count_ge(cand, rest_digit=rest_digit, rest_on=rest_on):
            cand_b = cand.astype(f32).astype(jnp.bfloat16)

            def body(kb, acc):
                hit = jnp.where(dg[kb] >= cand_b, one, zero)
                return acc + _rows_tree(hit, jnp.add, BF16_ROWS).astype(f32)

            acc = lax.fori_loop(0, nkb, body, jnp.zeros((BF16_ROWS, BQ), f32))
            rest = jnp.where(rest_on & (rest_digit >= cand), n_rest, 0.0)
            return jnp.sum(acc, axis=0, keepdims=True) + rest

        digit_thr, above_here = _bit_search(count_ge, target, jnp.zeros((1, BQ), jnp.int32),
                                            jnp.zeros((1, BQ), f32), 7)
        prefix = (digit_thr - 128) if byte == 3 else (prefix * 256 + digit_thr)
        above = above + above_here
        target = target - above_here
    thr = prefix
    need = TOPK - above

    acc_s[...] = jnp.zeros(acc_s.shape, f32)

    far_bias = bias_ref[B_NEAR, 0:1, :]

    def attend(far, edge, kb, carry):
        eq_seen, m_old, l_old = carry
        keys = sk[kb]
        eq = keys == thr
        eq_f = jnp.where(eq, 1.0, 0.0)
        before = jnp.dot(tri_ref[...], eq_f.astype(jnp.bfloat16), preferred_element_type=f32)
        tie_add = jnp.where(eq_seen + before < need, 0.0, NEG)
        mask_add = jnp.where(keys > thr, 0.0, jnp.where(eq, tie_add, NEG))
        if edge:
            mask_add = jnp.where(admissible(kb), mask_add, NEG)
        cblk = ckvn[pl.ds(pl.multiple_of(kb * KB, KB), KB), :]
        dots = lax.dot_general(cblk, ql[...], _NT, preferred_element_type=f32)
        tile = nkb - 1 - kb
        m_parts, l_parts, a_parts = [], [], []
        for g in range(HQ // LANES):
            cols = slice(g * LANES, (g + 1) * LANES)
            qcols = slice((g % (BQ // LANES)) * LANES, (g % (BQ // LANES) + 1) * LANES)
            mo = m_old[:, cols]
            if far:
                s = dots[:, cols] + mask_add[:, qcols]
                shift = far_bias[:, cols]
                mn = jnp.maximum(mo, jnp.max(_rows8(s, jnp.maximum), axis=0, keepdims=True) + shift)
                e = jnp.exp2(s - (mn - shift))
            else:
                s = dots[:, cols] + bias_ref[tile, :, cols] + mask_add[:, qcols]
                mn = jnp.maximum(mo, jnp.max(_rows8(s, jnp.maximum), axis=0, keepdims=True))
                e = jnp.exp2(s - mn)
            a = jnp.exp2(mo - mn)
            l_parts.append(a * l_old[:, cols] + jnp.sum(_rows8(e, jnp.add), axis=0, keepdims=True))
            m_parts.append(mn)
            a_parts.append(a)
            p_s[:, cols] = e.astype(p_s.dtype)
        alpha = jnp.concatenate(a_parts, axis=1)
        acc_s[...] = acc_s[...] * alpha + lax.dot_general(cblk, p_s[...], _TN, preferred_element_type=f32)
        eq_seen = eq_seen + jnp.sum(_rows8(eq_f, jnp.add), axis=0, keepdims=True)
        return eq_seen, jnp.concatenate(m_parts, axis=1), jnp.concatenate(l_parts, axis=1)

    init = (jnp.zeros((1, BQ), f32), jnp.full((1, HQ), 4 * NEG, f32), jnp.zeros((1, HQ), f32))
    n_far = jnp.maximum(nkb - B_NEAR, 0)
    n_full = nkb - B_QK
    carry = lax.fori_loop(0, n_far, functools.partial(attend, True, False), init)
    carry = lax.fori_loop(n_far, n_full, functools.partial(attend, False, False), carry)
    _, _, l_fin = lax.fori_loop(n_full, nkb, functools.partial(attend, False, True), carry)

    o_lat_t = acc_s[...] / l_fin
    for p in range(B_HEADS // 2):
        out = jnp.zeros((BQ, LANES), f32)
        for h in (2 * p, 2 * p + 1):
            o_lat = o_lat_t[:, h * BQ:(h + 1) * BQ].T.astype(jnp.bfloat16)
            out = out + jnp.dot(o_lat, wuv_ref[h], preferred_element_type=f32)
        o_ref[:, p * LANES:(p + 1) * LANES] = out.astype(o_ref.dtype)


def _attn_b(proj3, gain, wuk, wuv, bias, tri, nbatch=BATCH):
    nq = SEQ // BQ
    xw = IN_A - X_COL0
    return pl.pallas_call(
        _attn_b_kernel,
        grid=(nbatch, nq),
        in_specs=[
            pl.BlockSpec((None, BQ, B_W), lambda b, i: (b, i, O_QB // B_W)),
            pl.BlockSpec((None, BQ, xw), lambda b, i: (b, i, X_COL0 // xw)),
            pl.BlockSpec((None, SEQ, LANES), lambda b, i: (b, 0, O_KI // LANES)),
            pl.BlockSpec((None, SEQ, LANES), lambda b, i: (b, 0, O_CKV // LANES)),
            pl.BlockSpec((1, B_LATENT), lambda b, i: (0, 0)),
            pl.BlockSpec((B_HEADS, LANES, B_LATENT), lambda b, i: (0, 0, 0)),
            pl.BlockSpec((B_HEADS, B_LATENT, LANES), lambda b, i: (0, 0, 0)),
            pl.BlockSpec((B_NEAR + 1, KB, HQ), lambda b, i: (0, 0, 0), pipeline_mode=pl.Buffered(1)),
            pl.BlockSpec((KB, KB), lambda b, i: (0, 0)),
        ],
        out_specs=pl.BlockSpec((None, BQ, B_W), lambda b, i: (b, i, 0)),
        out_shape=jax.ShapeDtypeStruct((nbatch, SEQ, B_W), jnp.bfloat16),
        scratch_shapes=[
            pltpu.VMEM((SEQ, B_LATENT), jnp.bfloat16),
            pltpu.VMEM((SEQ, LANES), jnp.bfloat16),
            pltpu.VMEM((HQ, B_LATENT), jnp.bfloat16),
            pltpu.VMEM((HQ, LANES), jnp.bfloat16),
            pltpu.VMEM((NKB, KB, BQ), jnp.int32),
            pltpu.VMEM((NKB, KB, BQ), jnp.bfloat16),
            pltpu.VMEM((B_LATENT, HQ), jnp.float32),
            pltpu.VMEM((KB, HQ), jnp.bfloat16),
        ],
        compiler_params=_cparams(("arbitrary", "arbitrary")),
        name="attn_b",
    )(proj3, proj3, proj3, proj3, gain, wuk, wuv, bias, tri)


MERGE_BN = 1024
MERGE_BM = 1024


def _merge_kernel(oa_ref, ob_ref, oc_ref, ga_ref, gb_ref, gc_ref, wa_ref, wb_ref, wc_ref,
                  wo_in, wq_in, wmo_in, o_ref, wo_out, wq_out, wmo_out, wa, wb, wc):
    f32 = jnp.float32

    @pl.when(pl.program_id(1) == 0)
    def _():
        wa[...] = wa_ref[...].astype(wa.dtype)
        wb[...] = wb_ref[...].astype(wb.dtype)
        wc[...] = wc_ref[...].astype(wc.dtype)

    for src, dst in ((wo_in, wo_out), (wq_in, wq_out), (wmo_in, wmo_out)):
        dst[...] = src[...].astype(dst.dtype)

    m = ga_ref[...].astype(f32) * jnp.dot(oa_ref[...], wa[...], preferred_element_type=f32)
    m = m + gb_ref[...].astype(f32) * jnp.dot(ob_ref[...], wb[...], preferred_element_type=f32)
    m = m + gc_ref[...].astype(f32) * jnp.dot(oc_ref[...], wc[...], preferred_element_type=f32)
    o_ref[...] = m.astype(o_ref.dtype)


def _merge(oa, ob, oc, gates, w_branch, layer, next_weights):
    t = oa.shape[0]
    bn, bm = MERGE_BN, MERGE_BM
    gstep = D_MODEL // bn
    ni = t // bm
    nsteps = (D_MODEL // bn) * ni
    slabs = [w.shape[1] // nsteps for w in next_weights]
    assert all(s * nsteps == w.shape[1] and s % BF16_ROWS == 0 for s, w in zip(slabs, next_weights))
    side_in = [pl.BlockSpec((None, s, w.shape[2]), lambda j, i: (layer, j * ni + i, 0))
               for s, w in zip(slabs, next_weights)]
    side_out = [pl.BlockSpec((s, w.shape[2]), lambda j, i: (j * ni + i, 0)) for s, w in zip(slabs, next_weights)]
    res = pl.pallas_call(
        _merge_kernel,
        grid=(D_MODEL // bn, ni),
        in_specs=[
            pl.BlockSpec((bm, A_W), lambda j, i: (i, 0)),
            pl.BlockSpec((bm, B_W), lambda j, i: (i, 0)),
            pl.BlockSpec((bm, C_W), lambda j, i: (i, 0)),
            pl.BlockSpec((bm, bn), lambda j, i: (i, j)),
            pl.BlockSpec((bm, bn), lambda j, i: (i, gstep + j)),
            pl.BlockSpec((bm, bn), lambda j, i: (i, 2 * gstep + j)),
            pl.BlockSpec((None, A_W, bn), lambda j, i: (layer, 0, j)),
            pl.BlockSpec((None, B_W, bn), lambda j, i: (layer, A_W // B_W, j)),
            pl.BlockSpec((None, C_W, bn), lambda j, i: (layer, (A_W + B_W) // C_W, j)),
        ] + side_in,
        out_specs=[pl.BlockSpec((bm, bn), lambda j, i: (i, j))] + side_out,
        out_shape=[jax.ShapeDtypeStruct((t, D_MODEL), jnp.bfloat16)]
        + [jax.ShapeDtypeStruct(w.shape[1:], jnp.bfloat16) for w in next_weights],
        scratch_shapes=[pltpu.VMEM((A_W, bn), jnp.bfloat16), pltpu.VMEM((B_W, bn), jnp.bfloat16),
                        pltpu.VMEM((C_W, bn), jnp.bfloat16)],
        compiler_params=_cparams(("arbitrary", "arbitrary")),
        name="merge",
    )(oa, ob, oc, gates, gates, gates, w_branch, w_branch, w_branch, *next_weights)
    return res[0], res[1:]


FINISH_CHUNKS = 4


def _finish_chunked(y_of_rows, nrows, x_ref, gp_ref, gn_ref, xo_ref, ho_ref):
    step = nrows // FINISH_CHUNKS
    for r in range(0, nrows, step):
        rows = slice(r, r + step)
        xn = x_ref[rows, :] + _rms(y_of_rows(rows), gp_ref[...])
        xo_ref[rows, :] = xn
        if ho_ref is not None:
            ho_ref[rows, :] = _rms(xn, gn_ref[...]).astype(ho_ref.dtype)


def _gemm_res_kernel(nk, with_next, a_ref, w_ref, x_ref, gp_ref, gn_ref, xo_ref, *rest):
    ho_ref = rest[0] if with_next else None
    f32 = jnp.float32
    bm = a_ref.shape[0]
    acc = rest[-1]
    k = pl.program_id(1)

    @pl.when(k == 0)
    def _():
        acc[...] = jnp.dot(a_ref[...], w_ref[...], preferred_element_type=f32)

    @pl.when((k > 0) & (k < nk - 1))
    def _():
        acc[...] += jnp.dot(a_ref[...], w_ref[...], preferred_element_type=f32)

    @pl.when(k == nk - 1)
    def _():
        _finish_chunked(lambda rows: acc[rows, :] + jnp.dot(a_ref[rows, :], w_ref[...], preferred_element_type=f32),
                        bm, x_ref, gp_ref, gn_ref, xo_ref, ho_ref)


def _gemm_res(a, w, layer, x, g_post, g_next, bk, bm=512, name="gemm_res"):
    t, kdim = a.shape
    n = w.shape[2]
    nk = kdim // bk
    assert nk > 1
    with_next = g_next is not None
    if g_next is None:
        g_next = g_post
    out_shape = [jax.ShapeDtypeStruct((t, n), jnp.float32)]
    out_specs = [pl.BlockSpec((bm, n), lambda i, k: (i, 0))]
    if with_next:
        out_shape.append(jax.ShapeDtypeStruct((t, n), jnp.bfloat16))
        out_specs.append(pl.BlockSpec((bm, n), lambda i, k: (i, 0)))
    res = pl.pallas_call(
        functools.partial(_gemm_res_kernel, nk, with_next),
        grid=(t // bm, nk),
        in_specs=[
            pl.BlockSpec((bm, bk), lambda i, k: (i, k)),
            pl.BlockSpec((None, bk, n), lambda i, k: (layer, k, 0)),
            pl.BlockSpec((bm, n), lambda i, k: (i, 0)),
            pl.BlockSpec((1, n), lambda i, k: (0, 0)),
            pl.BlockSpec((1, n), lambda i, k: (0, 0)),
        ],
        out_specs=out_specs,
        out_shape=out_shape,
        scratch_shapes=[pltpu.VMEM((bm, n), jnp.float32)],
        compiler_params=_cparams(("parallel", "arbitrary")),
        name=name,
    )(a, w, x, g_post, g_next)
    return (res[0], res[1]) if with_next else (res[0], None)


def _cast_once(pairs):
    @pl.when(pl.program_id(0) == 0)
    def _():
        for src, dst in pairs:
            dst[...] = src[...].astype(dst.dtype)


def _memkv_kernel(m_ref, g_ref, w_ref, o_ref, wb):
    _cast_once([(w_ref, wb)])
    mn = _rms(m_ref[...], g_ref[...]).astype(jnp.bfloat16)
    o_ref[...] = jnp.dot(mn, wb[...], preferred_element_type=jnp.float32).astype(o_ref.dtype)


def _memkv(mem2, g, w, layer):
    t, d = mem2.shape
    n = w.shape[2]
    bm = 512
    return pl.pallas_call(
        _memkv_kernel,
        grid=(t // bm,),
        in_specs=[pl.BlockSpec((bm, d), lambda i: (i, 0)), pl.BlockSpec((1, d), lambda i: (0, 0)),
                  pl.BlockSpec((None, d, n), lambda i: (layer, 0, 0))],
        out_specs=pl.BlockSpec((bm, n), lambda i: (i, 0)),
        out_shape=jax.ShapeDtypeStruct((t, n), jnp.bfloat16),
        scratch_shapes=[pltpu.VMEM((d, n), jnp.bfloat16)],
        compiler_params=_cparams(("arbitrary",)),
        name="mem_kv",
    )(mem2, g, w)


XA_BM = 512


def _mix_out_xattn_kernel(m_ref, wo_ref, wq_ref, kv_ref, wmo_ref, x_ref, g1_ref, g2_ref, g3_ref, g4_ref,
                          xo_ref, ho_ref, x1_s, h1_s):
    f32 = jnp.float32
    bm = m_ref.shape[0]
    _finish_chunked(lambda rows: jnp.dot(m_ref[rows, :], wo_ref[...], preferred_element_type=f32),
                    bm, x_ref, g1_ref, g2_ref, x1_s, h1_s)
    q = jnp.dot(h1_s[...], wq_ref[...], preferred_element_type=f32).astype(jnp.bfloat16)
    scale = MEM_HEAD_DIM ** -0.5 * LOG2E
    hcols = [slice(h * LANES, (h + 1) * LANES) for h in range(MEM_HEADS)]
    ss = [lax.dot_general(q[:, c], kv_ref[:, c], _NT, preferred_element_type=f32) * scale for c in hcols]
    ms = [jnp.max(s, axis=-1, keepdims=True) for s in ss]
    es = [jnp.exp2(s - m) for s, m in zip(ss, ms)]
    ls = [jnp.sum(e, axis=-1, keepdims=True) for e in es]
    pv = [jnp.dot(e.astype(jnp.bfloat16), kv_ref[:, MEM_W + h * LANES:MEM_W + (h + 1) * LANES],
                  preferred_element_type=f32) for h, e in enumerate(es)]
    o = jnp.concatenate([(o_h / l).astype(jnp.bfloat16) for o_h, l in zip(pv, ls)], axis=-1)
    _finish_chunked(lambda rows: jnp.dot(o[rows, :], wmo_ref[...], preferred_element_type=f32),
                    bm, x1_s, g3_ref, g4_ref, xo_ref, ho_ref)


def _mix_out_xattn(merged, w_o, w_mq, kv, w_mo, layer, x, g1, g2, g3, g4):
    t, d = merged.shape
    bm = XA_BM
    per_batch = SEQ // bm
    once = pl.Buffered(1)
    row = pl.BlockSpec((bm, d), lambda i: (i, 0))
    gain = pl.BlockSpec((1, d), lambda i: (0, 0))
    return pl.pallas_call(
        _mix_out_xattn_kernel,
        grid=(t // bm,),
        in_specs=[
            row,
            pl.BlockSpec((None, d, d), lambda i: (layer, 0, 0), pipeline_mode=once),
            pl.BlockSpec((None, d, MEM_W), lambda i: (layer, 0, 0), pipeline_mode=once),
            pl.BlockSpec((MEM_LEN, 2 * MEM_W), lambda i: (i // per_batch, 0)),
            pl.BlockSpec((None, MEM_W, d), lambda i: (layer, 0, 0), pipeline_mode=once),
            row, gain, gain, gain, gain,
        ],
        out_specs=[row, row],
        out_shape=[jax.ShapeDtypeStruct((t, d), jnp.float32), jax.ShapeDtypeStruct((t, d), jnp.bfloat16)],
        scratch_shapes=[pltpu.VMEM((bm, d), jnp.float32), pltpu.VMEM((bm, d), jnp.bfloat16)],
        compiler_params=_cparams(("parallel",)),
        name="mix_out_xattn",
    )(merged, w_o, w_mq, kv, w_mo, x, g1, g2, g3, g4)


def _ffn_up_kernel(h_ref, wgf_ref, wvf_ref, cwg_ref, cwv_ref, cbg_ref, cbv_ref, wd_ref, o_ref, wdo_ref,
                   wg_ref, wv_ref, ug, uv):
    j = pl.program_id(0)
    i = pl.program_id(1)

    slab = j * pl.num_programs(1) + i

    @pl.when(slab < D_FF // wd_ref.shape[0])
    def _():
        wdo_ref[...] = wd_ref[...].astype(wdo_ref.dtype)

    @pl.when(slab >= D_FF // wd_ref.shape[0])
    def _():
        wdo_ref[...] = jnp.zeros(wdo_ref.shape, wdo_ref.dtype)

    tiles_per_seq = SEQ // FF_BM
    last = FF_P // FF_BN - 1
    valid = D_FF - last * FF_BN
    shift = FF_BN - valid

    @pl.when((i == 0) & (j < last))
    def _():
        wg_ref[...] = wgf_ref[...].astype(wg_ref.dtype)
        wv_ref[...] = wvf_ref[...].astype(wv_ref.dtype)

    @pl.when((i == 0) & (j == last))
    def _():
        zeros = jnp.zeros((wg_ref.shape[0], FF_BN - valid), wg_ref.dtype)
        wg_ref[:, :valid] = wgf_ref[:, :valid].astype(wg_ref.dtype)
        wv_ref[:, :valid] = wvf_ref[:, shift:].astype(wv_ref.dtype)
        wg_ref[:, valid:] = zeros
        wv_ref[:, valid:] = zeros

    for u in (ug, uv):
        @pl.when(i % tiles_per_seq == 0)
        def _():
            u[0:SUBLANES, :] = jnp.zeros((SUBLANES, FF_BN), jnp.float32)

        @pl.when(i % tiles_per_seq != 0)
        def _():
            u[0:SUBLANES, :] = u[FF_BM:FF_BM + SUBLANES, :]

    def conv(u, cw_ref, cb_ref, r0):
        base = SUBLANES + r0
        acc = cb_ref[...] + u[base - 2:base - 2 + FF_CH, :] * cw_ref[0:1, :]
        acc = acc + u[base - 1:base - 1 + FF_CH, :] * cw_ref[1:2, :]
        return acc + u[base:base + FF_CH, :] * cw_ref[2:3, :]

    for c in range(FF_BM // FF_CH):
        r0 = c * FF_CH
        hb = h_ref[r0:r0 + FF_CH, :]
        ug[SUBLANES + r0:SUBLANES + r0 + FF_CH, :] = jnp.dot(hb, wg_ref[...], preferred_element_type=jnp.float32)
        uv[SUBLANES + r0:SUBLANES + r0 + FF_CH, :] = jnp.dot(hb, wv_ref[...], preferred_element_type=jnp.float32)
        gate = conv(ug, cwg_ref, cbg_ref, r0)
        val = conv(uv, cwv_ref, cbv_ref, r0)
        o_ref[r0:r0 + FF_CH, :] = (jax.nn.gelu(gate) * val).astype(o_ref.dtype)


def _ffn_up(h, w_up, w_down, layer, conv_w, conv_b):
    t, d = h.shape
    nj = FF_P // FF_BN
    ni = t // FF_BM
    slab = FF_P // (nj * ni)
    assert slab * nj * ni == FF_P and D_FF % slab == 0 and slab % BF16_ROWS == 0
    n_real = D_FF // slab
    w_block = (None, pl.Element(d), pl.Element(FF_BN))
    return pl.pallas_call(
        _ffn_up_kernel,
        grid=(nj, ni),
        in_specs=[
            pl.BlockSpec((FF_BM, d), lambda j, i: (i, 0)),
            pl.BlockSpec(w_block, lambda j, i: (layer, 0, pl.multiple_of(j * FF_BN, LANES))),
            pl.BlockSpec(w_block, lambda j, i: (
                layer, 0, pl.multiple_of(jnp.minimum(D_FF + j * FF_BN, 2 * D_FF - FF_BN), LANES))),
            pl.BlockSpec((CONV_W, FF_BN), lambda j, i: (0, j)),
            pl.BlockSpec((CONV_W, FF_BN), lambda j, i: (0, nj + j)),
            pl.BlockSpec((1, FF_BN), lambda j, i: (0, j)),
            pl.BlockSpec((1, FF_BN), lambda j, i: (0, nj + j)),
            pl.BlockSpec((None, slab, d), lambda j, i: (layer, jnp.minimum(j * ni + i, n_real - 1), 0)),
        ],
        out_specs=[pl.BlockSpec((FF_BM, FF_BN), lambda j, i: (i, j)),
                   pl.BlockSpec((slab, d), lambda j, i: (j * ni + i, 0))],
        out_shape=[jax.ShapeDtypeStruct((t, FF_P), jnp.bfloat16),
                   jax.ShapeDtypeStruct((FF_P, w_down.shape[2]), jnp.bfloat16)],
        scratch_shapes=[pltpu.VMEM((d, FF_BN), jnp.bfloat16), pltpu.VMEM((d, FF_BN), jnp.bfloat16),
                        pltpu.VMEM((FF_BM + SUBLANES, FF_BN), jnp.float32),
                        pltpu.VMEM((FF_BM + SUBLANES, FF_BN), jnp.float32)],
        compiler_params=_cparams(("arbitrary", "arbitrary")),
        name="ffn_up",
    )(h, w_up, w_up, conv_w, conv_w, conv_b, conv_b, w_down)


def _pad_heads(w, axis):
    h = w.shape[0]
    zero = jnp.zeros_like(w)
    even = jnp.concatenate([w, zero], axis=axis)
    odd = jnp.concatenate([zero, w], axis=axis)
    sel = (jnp.arange(h) % 2 == 0).reshape((h, 1, 1))
    return jnp.where(sel, even, odd)


def _toeplitz(fn, rows, cols):
    ks = np.concatenate([np.arange(0, cols), np.arange(-(rows - 1), 0)])
    w = fn(ks)
    h, period = w.shape
    x = jnp.tile(w, (1, rows))[:, :rows * (period - 1)].reshape(h, rows, period - 1)
    return x[:, :, :cols].astype(jnp.float32)


def _band(rows, cols, left):
    diff = left + np.arange(rows)[:, None] // CHUNK - np.arange(cols)[None, :] // CHUNK
    return (diff >= 0) & (diff <= left)


def _bias_a(rel_bias):
    fn = lambda ks: rel_bias[np.clip(A_PAD - ks, -A_MAX_REL, A_MAX_REL) + A_MAX_REL].T
    bias = jnp.where(_band(QB, A_WIN, A_LEFT_CHUNKS)[None], _toeplitz(fn, QB, A_WIN) * LOG2E, NEG)
    return bias.reshape(A_HEADS * QB, A_WIN)


def _bias_c(t5_c):
    fn = lambda ks: t5_c[_t5_bucket(jnp.asarray(ks - C_PAD, jnp.int32))].T
    bias = jnp.where(_band(QB, C_WIN, C_LEFT_CHUNKS)[None], _toeplitz(fn, QB, C_WIN) * LOG2E, NEG)
    return bias.reshape(C_Q_HEADS * QB, C_WIN)


def _bias_b(t5_b):
    tiles = []
    for n in range(B_NEAR + 1):
        off = KB * (n - (B_QK - 1)) if n < B_NEAR else SEQ
        fn = lambda ks, off=off: t5_b[_t5_bucket(jnp.asarray(-ks - off, jnp.int32))].T
        tile = _toeplitz(fn, KB, BQ)
        tiles.append(jnp.transpose(tile * LOG2E, (1, 0, 2)).reshape(KB, HQ))
    return jnp.stack(tiles)


def _pad_ff(a, dtype):
    z = jnp.zeros((a.shape[0], FF_P - D_FF), dtype)
    return jnp.concatenate([a[:, :D_FF].astype(dtype), z, a[:, D_FF:].astype(dtype), z], axis=1)


def kernel(x, mem, t5_table, norm_gains, w_in, a_rel_bias, ckv_gain, w_uk, w_uv, sinks, w_branch, w_o,
           mem_gain, w_mq, w_mkv, w_mo, w_up, conv_w, conv_b, w_down):
    bf16 = jnp.bfloat16
    xs = x.reshape(TOKENS, D_MODEL)
    mem2 = mem.reshape(BATCH * MEM_LEN, D_MODEL)
    tri = jnp.asarray(np.tril(np.ones((KB, KB), np.float32), -1), bf16)
    bias_b = _bias_b(t5_table[:, :B_HEADS])
    bias_c = _bias_c(t5_table[:, B_HEADS:])
    gains = norm_gains.reshape(DEPTH, 6, 1, D_MODEL)
    w_in_t = jnp.swapaxes(w_in, 1, 2)

    h = _norm(xs, gains[0, 0])
    for l in range(DEPTH):
        g = gains[l]
        proja = _proj(h, w_in_t, l, 0, IN_A, 1024, gate=False, name="in_proj_a")
        projc = _proj(h, w_in_t, l, O_QC, C_COLS, C_COLS, gate=False, name="in_proj_c")
        gates = _proj(h, w_in_t, l, O_GL, 3 * D_MODEL, 1024, gate=True, name="in_proj_g")
        proja3 = proja.reshape(BATCH, SEQ, IN_A)
        oa = _attn_a(proja3, _bias_a(a_rel_bias[l]))
        wuk = jnp.transpose(_pad_heads(w_uk[l], axis=2), (0, 2, 1)).astype(bf16)
        wuv = _pad_heads(w_uv[l], axis=2).astype(bf16)
        ob = _attn_b(proja3, ckv_gain[l].reshape(1, B_LATENT), wuk, wuv, bias_b, tri)
        oc = _attn_c(sinks[l], projc.reshape(BATCH, SEQ, C_COLS), bias_c)
        merged, (w_o_b, w_mq_b, w_mo_b) = _merge(oa.reshape(TOKENS, A_W), ob.reshape(TOKENS, B_W),
                                                 oc.reshape(TOKENS, C_W), gates, w_branch, l, (w_o, w_mq, w_mo))
        kv = _memkv(mem2, mem_gain[l].reshape(1, D_MODEL), w_mkv, l)
        xs, h = _mix_out_xattn(merged, w_o_b[None], w_mq_b[None], kv, w_mo_b[None], 0, xs, g[1], g[2], g[3], g[4])
        hidden, w_dn_b = _ffn_up(h, w_up, w_down, l, _pad_ff(conv_w[l], jnp.float32),
                                 _pad_ff(conv_b[l].reshape(1, -1), jnp.float32))
        g_next = gains[l + 1, 0] if l + 1 < DEPTH else None
        xs, h = _gemm_res(hidden, w_dn_b[None], 0, xs, g[5], g_next, bk=DOWN_BK, name="ffn_down")
    return xs.reshape(BATCH, SEQ, D_MODEL)
```

```python
import functools
import math

import numpy as np
import jax
import jax.numpy as jnp
from jax import lax
from jax.experimental import pallas as pl
from jax.experimental.pallas import tpu as pltpu

D_MODEL = 2048
BATCH = 4
SEQ = 2048
DEPTH = 2
TOKENS = BATCH * SEQ
CHUNK = 64
EPS = 1e-6
NEG = -1e30
LOG2E = math.log2(math.e)
A_HEADS = 8
A_LEFT_CHUNKS = 8
A_MAX_REL = 128
A_W = 512
B_HEADS = 8
B_W = 512
B_LATENT = 128
IDX_HEADS = 8
IDX_DIM = 64
TOPK = 256
C_Q_HEADS = 16
C_GROUP = 8
C_W = 1024
C_LEFT_CHUNKS = 2
T5_BUCKETS = 32
T5_MAX_DIST = 256
MEM_LEN = 256
MEM_HEADS = 4
MEM_HEAD_DIM = 128
MEM_W = 512
D_FF = 5504
CONV_W = 3

LANES = 128
SUBLANES = 8
HEAD_DIM = 64
QB = 128
KB = 256
VMEM_LIMIT = 56 * 1024 * 1024

O_QA, O_KA, O_VA, O_QB = 0, 512, 1024, 1536
O_CKV = 2048
O_QI = 2176
O_KI = 2688
O_WI = 2752
O_QC = 2760
O_GL = 4040
IN_W = O_GL + 3 * D_MODEL
IN_A = 3072
X_COL0 = 2048
C_COLS = O_GL - O_QC
PROJ_BM = 1024
PROJ_TR = 256

FF_P = 5632
FF_BN = 512
FF_BM = 2048
FF_CH = 256
DOWN_BK = 1408

A_WIN = (A_LEFT_CHUNKS + 2) * CHUNK
C_WIN = (C_LEFT_CHUNKS + 2) * CHUNK
A_PAD = A_LEFT_CHUNKS * CHUNK
C_PAD = C_LEFT_CHUNKS * CHUNK
BQ = 256
B_QK = BQ // KB
B_NEAR = B_QK + 1
NKB = SEQ // KB
HQ = B_HEADS * BQ

_NT = (((1,), (1,)), ((), ()))
_TN = (((0,), (0,)), ((), ()))


def _cparams(sem):
    return pltpu.CompilerParams(dimension_semantics=sem, vmem_limit_bytes=VMEM_LIMIT)


def _t5_bucket(rel):
    half = T5_BUCKETS // 2
    max_exact = half // 2
    sign = jnp.where(rel > 0, half, 0)
    d = jnp.abs(rel)
    d_f = jnp.maximum(d, 1).astype(jnp.float32)
    large = max_exact + (jnp.log(d_f / max_exact) / math.log(T5_MAX_DIST / max_exact) * (half - max_exact)).astype(jnp.int32)
    large = jnp.minimum(large, half - 1)
    return sign + jnp.where(d < max_exact, d, large)


def _far_bucket_is_constant():
    assert BQ % KB == 0
    d = np.arange(KB + 1, SEQ, dtype=np.float32)
    assert d[0] > T5_MAX_DIST
    large = 8 + (np.log(d / 8) / math.log(T5_MAX_DIST / 8) * 8).astype(np.int32)
    return bool(np.all(np.minimum(large, 15) == 15))


assert _far_bucket_is_constant()


def _rms(v, g):
    return v * lax.rsqrt(jnp.mean(v * v, axis=-1, keepdims=True) + EPS) * g


def _norm_kernel(x_ref, g_ref, o_ref):
    o_ref[...] = _rms(x_ref[...], g_ref[...]).astype(o_ref.dtype)


def _norm(x, g, bm=1024):
    t, d = x.shape
    return pl.pallas_call(
        _norm_kernel,
        grid=(t // bm,),
        in_specs=[pl.BlockSpec((bm, d), lambda i: (i, 0)), pl.BlockSpec((1, d), lambda i: (0, 0))],
        out_specs=pl.BlockSpec((bm, d), lambda i: (i, 0)),
        out_shape=jax.ShapeDtypeStruct((t, d), jnp.bfloat16),
        compiler_params=_cparams(("parallel",)),
        name="rmsnorm",
    )(x, g)


def _proj_kernel(gate, h_ref, wt_ref, o_ref, wb):
    @pl.when(pl.program_id(1) == 0)
    def _():
        for c in range(wt_ref.shape[0] // PROJ_TR):
            rows = slice(c * PROJ_TR, (c + 1) * PROJ_TR)
            wb[:, rows] = wt_ref[rows, :].T.astype(wb.dtype)

    acc = jnp.dot(h_ref[...], wb[...], preferred_element_type=jnp.float32)
    o_ref[...] = (jax.nn.sigmoid(acc) if gate else acc).astype(o_ref.dtype)


def _proj(h, w_t, layer, col0, n, bn, gate, name):
    t, k = h.shape
    assert col0 % SUBLANES == 0 and bn % SUBLANES == 0
    w_spec = pl.BlockSpec((None, pl.Element(bn), pl.Element(k)),
                          lambda j, i: (layer, pl.multiple_of(col0 + j * bn, SUBLANES), 0))
    return pl.pallas_call(
        functools.partial(_proj_kernel, gate),
        grid=(n // bn, t // PROJ_BM),
        in_specs=[pl.BlockSpec((PROJ_BM, k), lambda j, i: (i, 0)), w_spec],
        out_specs=pl.BlockSpec((PROJ_BM, bn), lambda j, i: (i, j)),
        out_shape=jax.ShapeDtypeStruct((t, n), jnp.bfloat16),
        scratch_shapes=[pltpu.VMEM((k, bn), jnp.bfloat16)],
        compiler_params=_cparams(("arbitrary", "arbitrary")),
        name=name,
    )(h, w_t)


def _lane_masks(dtype):
    lane = lax.broadcasted_iota(jnp.int32, (1, LANES), 1)
    lo = (lane < HEAD_DIM).astype(jnp.float32)
    return lo.astype(dtype), (1.0 - lo).astype(dtype)


def _swap_halves(x):
    return pltpu.roll(x.astype(jnp.float32), HEAD_DIM, 1).astype(x.dtype)


def _fill_padded(src, dst_ref, pad):
    dst_ref[0:pad, :] = jnp.zeros((pad, dst_ref.shape[1]), dst_ref.dtype)
    dst_ref[pad:pad + SEQ, :] = src


def _attn_a_kernel(q_ref, k_ref, v_ref, bias_ref, o_ref, kpad, vpad):
    i = pl.program_id(1)

    @pl.when(i == 0)
    def _():
        _fill_padded(k_ref[...], kpad, A_PAD)
        _fill_padded(v_ref[...], vpad, A_PAD)

    start = pl.multiple_of(i * QB, QB)
    kw = kpad[pl.ds(start, A_WIN), :]
    vw = vpad[pl.ds(start, A_WIN), :]
    mlo, mhi = _lane_masks(jnp.bfloat16)
    lane = lax.broadcasted_iota(jnp.int32, (QB, LANES), 1)
    kchunk = lax.broadcasted_iota(jnp.int32, (QB, A_WIN), 1) // CHUNK
    pad_mask = jnp.where(kchunk >= A_LEFT_CHUNKS - 2 * i, 0.0, NEG)
    scale = HEAD_DIM ** -0.5 * LOG2E
    pad2 = jnp.concatenate([pad_mask, pad_mask], axis=0)
    npair = A_HEADS // 2
    pcols = [slice(p * LANES, (p + 1) * LANES) for p in range(npair)]
    ss = []
    for p in range(npair):
        qp = q_ref[:, pcols[p]]
        q2 = jnp.concatenate([qp * mlo, qp * mhi], axis=0)
        s = lax.dot_general(q2, kw[:, pcols[p]], _NT, preferred_element_type=jnp.float32)
        ss.append(s * scale + bias_ref[2 * p * QB:(2 * p + 2) * QB, :] + pad2)
    ms = [jnp.max(s, axis=-1, keepdims=True) for s in ss]
    es = [jnp.exp2(s - m) for s, m in zip(ss, ms)]
    ls = [jnp.sum(e, axis=-1, keepdims=True) for e in es]
    outs = [jnp.dot(e.astype(jnp.bfloat16), vw[:, pcols[p]], preferred_element_type=jnp.float32)
            for p, e in enumerate(es)]
    for p in range(npair):
        o = outs[p] / ls[p]
        o_ref[:, pcols[p]] = jnp.where(lane < HEAD_DIM, o[:QB], o[QB:]).astype(o_ref.dtype)


def _attn_a(proj3, bias):
    nq = SEQ // QB
    return pl.pallas_call(
        _attn_a_kernel,
        grid=(BATCH, nq),
        in_specs=[
            pl.BlockSpec((None, QB, A_W), lambda b, i: (b, i, O_QA // A_W)),
            pl.BlockSpec((None, SEQ, A_W), lambda b, i: (b, 0, O_KA // A_W)),
            pl.BlockSpec((None, SEQ, A_W), lambda b, i: (b, 0, O_VA // A_W)),
            pl.BlockSpec((A_HEADS * QB, A_WIN), lambda b, i: (0, 0)),
        ],
        out_specs=pl.BlockSpec((None, QB, A_W), lambda b, i: (b, i, 0)),
        out_shape=jax.ShapeDtypeStruct((BATCH, SEQ, A_W), jnp.bfloat16),
        scratch_shapes=[pltpu.VMEM((SEQ + A_PAD, A_W), jnp.bfloat16), pltpu.VMEM((SEQ + A_PAD, A_W), jnp.bfloat16)],
        compiler_params=_cparams(("arbitrary", "arbitrary")),
        name="attn_a",
    )(proj3, proj3, proj3, bias)


def _attn_c_kernel(sink_ref, q_ref, k_ref, v_ref, bias_ref, o_ref, kpad, kswp, vpad, vswp):
    i = pl.program_id(1)

    @pl.when(i == 0)
    def _():
        k = k_ref[...]
        v = v_ref[...]
        _fill_padded(k, kpad, C_PAD)
        _fill_padded(_swap_halves(k), kswp, C_PAD)
        _fill_padded(v, vpad, C_PAD)
        _fill_padded(_swap_halves(v), vswp, C_PAD)

    start = pl.multiple_of(i * QB, QB)
    mlo, mhi = _lane_masks(jnp.bfloat16)
    lane = lax.broadcasted_iota(jnp.int32, (QB, LANES), 1)
    kchunk = lax.broadcasted_iota(jnp.int32, (QB, C_WIN), 1) // CHUNK
    pad_mask = jnp.where(kchunk >= C_LEFT_CHUNKS - 2 * i, 0.0, NEG)
    scale = HEAD_DIM ** -0.5 * LOG2E
    qs = [q_ref[:, p * LANES:(p + 1) * LANES] for p in range(C_GROUP)]
    npair = C_GROUP // 2
    stacks = []
    for straight in (True, False):
        kref, vref = (kpad, vpad) if straight else (kswp, vswp)
        kw = kref[pl.ds(start, C_WIN), :]
        vw = vref[pl.ds(start, C_WIN), :]
        halves = [int((p >= npair) == straight) for p in range(C_GROUP)]
        qg = jnp.concatenate([qs[p] * (mhi if halves[p] else mlo) for p in range(C_GROUP)], axis=0)
        s_all = lax.dot_general(qg, kw, _NT, preferred_element_type=jnp.float32)
        ps, ls = [], []
        for p in range(C_GROUP):
            h = 2 * p + halves[p]
            s = s_all[p * QB:(p + 1) * QB] * scale + bias_ref[h * QB:(h + 1) * QB, :] + pad_mask
            sink = sink_ref[h] * LOG2E
            m = jnp.maximum(jnp.max(s, axis=-1, keepdims=True), sink)
            e = jnp.exp2(s - m)
            ls.append(jnp.sum(e, axis=-1, keepdims=True) + jnp.exp2(sink - m))
            ps.append(e.astype(jnp.bfloat16))
        o_all = jnp.dot(jnp.concatenate(ps, axis=0), vw, preferred_element_type=jnp.float32)
        stacks.append(([o_all[p * QB:(p + 1) * QB] / ls[p] for p in range(C_GROUP)], halves))
    for p in range(C_GROUP):
        (o1, h1), (o2, _) = stacks
        lo, hi = (o2[p], o1[p]) if h1[p] else (o1[p], o2[p])
        o_ref[:, p * LANES:(p + 1) * LANES] = jnp.where(lane < HEAD_DIM, lo, hi).astype(o_ref.dtype)


def _attn_c(sinks, projc3, bias):
    nq = SEQ // QB
    pad_buf = pltpu.VMEM((SEQ + C_PAD, LANES), jnp.bfloat16)
    return pl.pallas_call(
        _attn_c_kernel,
        grid=(BATCH, nq),
        in_specs=[
            pl.BlockSpec(memory_space=pltpu.SMEM),
            pl.BlockSpec((None, QB, C_W), lambda b, i: (b, i, 0)),
            pl.BlockSpec((None, SEQ, LANES), lambda b, i: (b, 0, C_W // LANES)),
            pl.BlockSpec((None, SEQ, LANES), lambda b, i: (b, 0, C_W // LANES + 1)),
            pl.BlockSpec((C_Q_HEADS * QB, C_WIN), lambda b, i: (0, 0)),
        ],
        out_specs=pl.BlockSpec((None, QB, C_W), lambda b, i: (b, i, 0)),
        out_shape=jax.ShapeDtypeStruct((BATCH, SEQ, C_W), jnp.bfloat16),
        scratch_shapes=[pad_buf, pad_buf, pad_buf, pad_buf],
        compiler_params=_cparams(("arbitrary", "arbitrary")),
        name="attn_c",
    )(sinks, projc3, projc3, projc3, bias)


def _sort_key(x):
    bits = lax.bitcast_convert_type(x + 0.0, jnp.int32)
    return bits ^ ((bits >> 31) & jnp.int32(0x7FFFFFFF))


def _rows_tree(x, op, slab):
    parts = [x[r:r + slab, :] for r in range(0, x.shape[0], slab)]
    while len(parts) > 1:
        nxt = [op(parts[k], parts[k + 1]) for k in range(0, len(parts) - 1, 2)]
        parts = nxt + ([parts[-1]] if len(parts) % 2 else [])
    return parts[0]


def _rows8(x, op):
    return _rows_tree(x, op, SUBLANES)


BF16_ROWS = 16


def _bit_search(count_ge, target, v, above, top_bit):
    def step(t, carry):
        v, above = carry
        cand = v | (jnp.int32(1) << (top_bit - t))
        c = count_ge(cand)
        keep = c >= target
        return jnp.where(keep, cand, v), jnp.where(keep, above, c)

    return lax.fori_loop(0, top_bit + 1, step, (v, above))


def _attn_b_kernel(qb_ref, x_ref, kiw_ref, ckv_ref, gain_ref, wuk_ref, wuv_ref, bias_ref, tri_ref,
                   o_ref, ckvn, kd, ql, qi_all, sk, dg, acc_s, p_s):
    i = pl.program_id(1)
    nkb = (i + 1) * B_QK
    mlo, mhi = _lane_masks(jnp.bfloat16)
    f32 = jnp.float32

    @pl.when(i == 0)
    def _():
        ckvn[...] = _rms(ckv_ref[...].astype(f32), gain_ref[...]).astype(ckvn.dtype)
        kiw = kiw_ref[...].astype(f32)
        lane = lax.broadcasted_iota(jnp.int32, kiw.shape, 1)
        kd[...] = jnp.where(lane < HEAD_DIM, kiw, pltpu.roll(kiw, HEAD_DIM, 1)).astype(kd.dtype)

    for h in range(B_HEADS):
        rows = slice(h * BQ, (h + 1) * BQ)
        qlat = jnp.dot(qb_ref[:, (h // 2) * LANES:(h // 2 + 1) * LANES], wuk_ref[h], preferred_element_type=f32)
        ql[rows, :] = (qlat * (HEAD_DIM ** -0.5 * LOG2E)).astype(ql.dtype)
        qcol = O_QI - X_COL0 + (h // 2) * LANES
        qi_all[rows, :] = x_ref[:, qcol:qcol + LANES] * (mhi if h % 2 else mlo)
    wcol = O_KI - X_COL0
    wi_t = x_ref[:, wcol:wcol + LANES].astype(f32).T * (IDX_HEADS ** -0.5 * IDX_DIM ** -0.5)
    wi_rows = [wi_t[O_WI - O_KI + h:O_WI - O_KI + h + 1, :] for h in range(IDX_HEADS)]

    kpos = lax.broadcasted_iota(jnp.int32, (KB, BQ), 0)
    key_limit = ((i * BQ + lax.broadcasted_iota(jnp.int32, (1, BQ), 1)) // CHUNK + 1) * CHUNK

    def admissible(kb):
        return kb * KB + kpos < key_limit

    def score_block(edge, kb, carry):
        kblk = kd[pl.ds(pl.multiple_of(kb * KB, KB), KB), :]
        dots = lax.dot_general(kblk, qi_all[...], _NT, preferred_element_type=f32)
        score = jnp.maximum(dots[:, 0:BQ], 0.0) * wi_rows[0]
        for h in range(1, IDX_HEADS):
            score = score + jnp.maximum(dots[:, h * BQ:(h + 1) * BQ], 0.0) * wi_rows[h]
        sk[kb] = _sort_key(jnp.where(admissible(kb), score, NEG) if edge else score)
        return carry

    n_full = nkb - B_QK
    lax.fori_loop(0, n_full, functools.partial(score_block, False), 0)
    lax.fori_loop(n_full, nkb, functools.partial(score_block, True), 0)

    neg_key = _sort_key(jnp.full((1, 1), NEG, f32))
    n_rest = ((NKB - nkb) * KB).astype(f32)
    one, zero = jnp.ones((), jnp.bfloat16), jnp.zeros((), jnp.bfloat16)

    def byte_of(key, byte):
        return ((key >> 24) + 128) if byte == 3 else ((key >> (8 * byte)) & 255)

    target = jnp.full((1, BQ), TOPK, f32)
    prefix = jnp.zeros((1, BQ), jnp.int32)
    above = jnp.zeros((1, BQ), f32)
    for byte in (3, 2, 1, 0):
        def in_class(key, byte=byte, prefix=prefix):
            return (key >> (8 * byte + 8)) == prefix

        def prepare(kb, carry, byte=byte, in_class=in_class):
            key = sk[kb]
            digit = byte_of(key, byte).astype(f32)
            if byte < 3:
                digit = jnp.where(in_class(key), digit, -1.0)
            dg[kb] = digit.astype(dg.dtype)
            return carry

        lax.fori_loop(0, nkb, prepare, 0)
        rest_digit = byte_of(neg_key, byte)
        rest_on = in_class(neg_key) if byte < 3 else (neg_key == neg_key)

        def count_ge(cand, rest_digit=rest_digit, rest_on=rest_on):
            cand_b = cand.astype(f32).astype(jnp.bfloat16)

            def body(kb, acc):
                hit = jnp.where(dg[kb] >= cand_b, one, zero)
                return acc + _rows_tree(hit, jnp.add, BF16_ROWS).astype(f32)

            acc = lax.fori_loop(0, nkb, body, jnp.zeros((BF16_ROWS, BQ), f32))
            rest = jnp.where(rest_on & (rest_digit >= cand), n_rest, 0.0)
            return jnp.sum(acc, axis=0, keepdims=True) + rest

        digit_thr, above_here = _bit_search(count_ge, target, jnp.zeros((1, BQ), jnp.int32),
                                            jnp.zeros((1, BQ), f32), 7)
        prefix = (digit_thr - 128) if byte == 3 else (prefix * 256 + digit_thr)
        above = above + above_here
        target = target - above_here
    thr = prefix
    need = TOPK - above

    acc_s[...] = jnp.zeros(acc_s.shape, f32)

    far_bias = bias_ref[B_NEAR, 0:1, :]

    def attend(far, edge, kb, carry):
        eq_seen, m_old, l_old = carry
        keys = sk[kb]
        eq = keys == thr
        eq_f = jnp.where(eq, 1.0, 0.0)
        before = jnp.dot(tri_ref[...], eq_f.astype(jnp.bfloat16), preferred_element_type=f32)
        tie_add = jnp.where(eq_seen + before < need, 0.0, NEG)
        mask_add = jnp.where(keys > thr, 0.0, jnp.where(eq, tie_add, NEG))
        if edge:
            mask_add = jnp.where(admissible(kb), mask_add, NEG)
        cblk = ckvn[pl.ds(pl.multiple_of(kb * KB, KB), KB), :]
        dots = lax.dot_general(cblk, ql[...], _NT, preferred_element_type=f32)
        tile = nkb - 1 - kb
        m_parts, l_parts, a_parts = [], [], []
        for g in range(HQ // LANES):
            cols = slice(g * LANES, (g + 1) * LANES)
            qcols = slice((g % (BQ // LANES)) * LANES, (g % (BQ // LANES) + 1) * LANES)
            mo = m_old[:, cols]
            if far:
                s = dots[:, cols] + mask_add[:, qcols]
                shift = far_bias[:, cols]
                mn = jnp.maximum(mo, jnp.max(_rows8(s, jnp.maximum), axis=0, keepdims=True) + shift)
                e = jnp.exp2(s - (mn - shift))
            else:
                s = dots[:, cols] + bias_ref[tile, :, cols] + mask_add[:, qcols]
                mn = jnp.maximum(mo, jnp.max(_rows8(s, jnp.maximum), axis=0, keepdims=True))
                e = jnp.exp2(s - mn)
            a = jnp.exp2(mo - mn)
            l_parts.append(a * l_old[:, cols] + jnp.sum(_rows8(e, jnp.add), axis=0, keepdims=True))
            m_parts.append(mn)
            a_parts.append(a)
            p_s[:, cols] = e.astype(p_s.dtype)
        alpha = jnp.concatenate(a_parts, axis=1)
        acc_s[...] = acc_s[...] * alpha + lax.dot_general(cblk, p_s[...], _TN, preferred_element_type=f32)
        eq_seen = eq_seen + jnp.sum(_rows8(eq_f, jnp.add), axis=0, keepdims=True)
        return eq_seen, jnp.concatenate(m_parts, axis=1), jnp.concatenate(l_parts, axis=1)

    init = (jnp.zeros((1, BQ), f32), jnp.full((1, HQ), 4 * NEG, f32), jnp.zeros((1, HQ), f32))
    n_far = jnp.maximum(nkb - B_NEAR, 0)
    carry = lax.fori_loop(0, n_far, functools.partial(attend, True, False), init)
    carry = lax.fori_loop(n_far, n_full, functools.partial(attend, False, False), carry)
    _, _, l_fin = lax.fori_loop(n_full, nkb, functools.partial(attend, False, True), carry)

    o_lat_t = acc_s[...] / l_fin
    for p in range(B_HEADS // 2):
        out = jnp.zeros((BQ, LANES), f32)
        for h in (2 * p, 2 * p + 1):
            o_lat = o_lat_t[:, h * BQ:(h + 1) * BQ].T.astype(jnp.bfloat16)
            out = out + jnp.dot(o_lat, wuv_ref[h], preferred_element_type=f32)
        o_ref[:, p * LANES:(p + 1) * LANES] = out.astype(o_ref.dtype)


def _attn_b(proj3, gain, wuk, wuv, bias, tri, nbatch=BATCH):
    nq = SEQ // BQ
    xw = IN_A - X_COL0
    return pl.pallas_call(
        _attn_b_kernel,
        grid=(nbatch, nq),
        in_specs=[
            pl.BlockSpec((None, BQ, B_W), lambda b, i: (b, i, O_QB // B_W)),
            pl.BlockSpec((None, BQ, xw), lambda b, i: (b, i, X_COL0 // xw)),
            pl.BlockSpec((None, SEQ, LANES), lambda b, i: (b, 0, O_KI // LANES)),
            pl.BlockSpec((None, SEQ, LANES), lambda b, i: (b, 0, O_CKV // LANES)),
            pl.BlockSpec((1, B_LATENT), lambda b, i: (0, 0)),
            pl.BlockSpec((B_HEADS, LANES, B_LATENT), lambda b, i: (0, 0, 0)),
            pl.BlockSpec((B_HEADS, B_LATENT, LANES), lambda b, i: (0, 0, 0)),
            pl.BlockSpec((B_NEAR + 1, KB, HQ), lambda b, i: (0, 0, 0), pipeline_mode=pl.Buffered(1)),
            pl.BlockSpec((KB, KB), lambda b, i: (0, 0)),
        ],
        out_specs=pl.BlockSpec((None, BQ, B_W), lambda b, i: (b, i, 0)),
        out_shape=jax.ShapeDtypeStruct((nbatch, SEQ, B_W), jnp.bfloat16),
        scratch_shapes=[
            pltpu.VMEM((SEQ, B_LATENT), jnp.bfloat16),
            pltpu.VMEM((SEQ, LANES), jnp.bfloat16),
            pltpu.VMEM((HQ, B_LATENT), jnp.bfloat16),
            pltpu.VMEM((HQ, LANES), jnp.bfloat16),
            pltpu.VMEM((NKB, KB, BQ), jnp.int32),
            pltpu.VMEM((NKB, KB, BQ), jnp.bfloat16),
            pltpu.VMEM((B_LATENT, HQ), jnp.float32),
            pltpu.VMEM((KB, HQ), jnp.bfloat16),
        ],
        compiler_params=_cparams(("arbitrary", "arbitrary")),
        name="attn_b",
    )(proj3, proj3, proj3, proj3, gain, wuk, wuv, bias, tri)


MERGE_BN = 1024
MERGE_BM = 1024


def _merge_kernel(oa_ref, ob_ref, oc_ref, ga_ref, gb_ref, gc_ref, wa_ref, wb_ref, wc_ref,
                  wo_in, wq_in, wmo_in, o_ref, wo_out, wq_out, wmo_out, wa, wb, wc):
    f32 = jnp.float32

    @pl.when(pl.program_id(1) == 0)
    def _():
        wa[...] = wa_ref[...].astype(wa.dtype)
        wb[...] = wb_ref[...].astype(wb.dtype)
        wc[...] = wc_ref[...].astype(wc.dtype)

    for src, dst in ((wo_in, wo_out), (wq_in, wq_out), (wmo_in, wmo_out)):
        dst[...] = src[...].astype(dst.dtype)

    m = ga_ref[...].astype(f32) * jnp.dot(oa_ref[...], wa[...], preferred_element_type=f32)
    m = m + gb_ref[...].astype(f32) * jnp.dot(ob_ref[...], wb[...], preferred_element_type=f32)
    m = m + gc_ref[...].astype(f32) * jnp.dot(oc_ref[...], wc[...], preferred_element_type=f32)
    o_ref[...] = m.astype(o_ref.dtype)


def _merge(oa, ob, oc, gates, w_branch, layer, next_weights):
    t = oa.shape[0]
    bn, bm = MERGE_BN, MERGE_BM
    gstep = D_MODEL // bn
    ni = t // bm
    nsteps = (D_MODEL // bn) * ni
    slabs = [w.shape[1] // nsteps for w in next_weights]
    assert all(s * nsteps == w.shape[1] and s % BF16_ROWS == 0 for s, w in zip(slabs, next_weights))
    side_in = [pl.BlockSpec((None, s, w.shape[2]), lambda j, i: (layer, j * ni + i, 0))
               for s, w in zip(slabs, next_weights)]
    side_out = [pl.BlockSpec((s, w.shape[2]), lambda j, i: (j * ni + i, 0)) for s, w in zip(slabs, next_weights)]
    res = pl.pallas_call(
        _merge_kernel,
        grid=(D_MODEL // bn, ni),
        in_specs=[
            pl.BlockSpec((bm, A_W), lambda j, i: (i, 0)),
            pl.BlockSpec((bm, B_W), lambda j, i: (i, 0)),
            pl.BlockSpec((bm, C_W), lambda j, i: (i, 0)),
            pl.BlockSpec((bm, bn), lambda j, i: (i, j)),
            pl.BlockSpec((bm, bn), lambda j, i: (i, gstep + j)),
            pl.BlockSpec((bm, bn), lambda j, i: (i, 2 * gstep + j)),
            pl.BlockSpec((None, A_W, bn), lambda j, i: (layer, 0, j)),
            pl.BlockSpec((None, B_W, bn), lambda j, i: (layer, A_W // B_W, j)),
            pl.BlockSpec((None, C_W, bn), lambda j, i: (layer, (A_W + B_W) // C_W, j)),
        ] + side_in,
        out_specs=[pl.BlockSpec((bm, bn), lambda j, i: (i, j))] + side_out,
        out_shape=[jax.ShapeDtypeStruct((t, D_MODEL), jnp.bfloat16)]
        + [jax.ShapeDtypeStruct(w.shape[1:], jnp.bfloat16) for w in next_weights],
        scratch_shapes=[pltpu.VMEM((A_W, bn), jnp.bfloat16), pltpu.VMEM((B_W, bn), jnp.bfloat16),
                        pltpu.VMEM((C_W, bn), jnp.bfloat16)],
        compiler_params=_cparams(("arbitrary", "arbitrary")),
        name="merge",
    )(oa, ob, oc, gates, gates, gates, w_branch, w_branch, w_branch, *next_weights)
    return res[0], res[1:]


FINISH_CHUNKS = 4


def _finish_chunked(y_of_rows, nrows, x_ref, gp_ref, gn_ref, xo_ref, ho_ref):
    step = nrows // FINISH_CHUNKS
    for r in range(0, nrows, step):
        rows = slice(r, r + step)
        xn = x_ref[rows, :] + _rms(y_of_rows(rows), gp_ref[...])
        xo_ref[rows, :] = xn
        if ho_ref is not None:
            ho_ref[rows, :] = _rms(xn, gn_ref[...]).astype(ho_ref.dtype)


def _gemm_res_kernel(nk, with_next, a_ref, w_ref, x_ref, gp_ref, gn_ref, xo_ref, *rest):
    ho_ref = rest[0] if with_next else None
    f32 = jnp.float32
    bm = a_ref.shape[0]
    acc = rest[-1]
    k = pl.program_id(1)

    @pl.when(k == 0)
    def _():
        acc[...] = jnp.dot(a_ref[...], w_ref[...], preferred_element_type=f32)

    @pl.when((k > 0) & (k < nk - 1))
    def _():
        acc[...] += jnp.dot(a_ref[...], w_ref[...], preferred_element_type=f32)

    @pl.when(k == nk - 1)
    def _():
        _finish_chunked(lambda rows: acc[rows, :] + jnp.dot(a_ref[rows, :], w_ref[...], preferred_element_type=f32),
                        bm, x_ref, gp_ref, gn_ref, xo_ref, ho_ref)


def _gemm_res(a, w, layer, x, g_post, g_next, bk, bm=512, name="gemm_res"):
    t, kdim = a.shape
    n = w.shape[2]
    nk = kdim // bk
    assert nk > 1
    with_next = g_next is not None
    if g_next is None:
        g_next = g_post
    out_shape = [jax.ShapeDtypeStruct((t, n), jnp.float32)]
    out_specs = [pl.BlockSpec((bm, n), lambda i, k: (i, 0))]
    if with_next:
        out_shape.append(jax.ShapeDtypeStruct((t, n), jnp.bfloat16))
        out_specs.append(pl.BlockSpec((bm, n), lambda i, k: (i, 0)))
    res = pl.pallas_call(
        functools.partial(_gemm_res_kernel, nk, with_next),
        grid=(t // bm, nk),
        in_specs=[
            pl.BlockSpec((bm, bk), lambda i, k: (i, k)),
            pl.BlockSpec((None, bk, n), lambda i, k: (layer, k, 0)),
            pl.BlockSpec((bm, n), lambda i, k: (i, 0)),
            pl.BlockSpec((1, n), lambda i, k: (0, 0)),
            pl.BlockSpec((1, n), lambda i, k: (0, 0)),
        ],
        out_specs=out_specs,
        out_shape=out_shape,
        scratch_shapes=[pltpu.VMEM((bm, n), jnp.float32)],
        compiler_params=_cparams(("parallel", "arbitrary")),
        name=name,
    )(a, w, x, g_post, g_next)
    return (res[0], res[1]) if with_next else (res[0], None)


def _cast_once(pairs):
    @pl.when(pl.program_id(0) == 0)
    def _():
        for src, dst in pairs:
            dst[...] = src[...].astype(dst.dtype)


def _memkv_kernel(m_ref, g_ref, w_ref, o_ref, wb):
    _cast_once([(w_ref, wb)])
    mn = _rms(m_ref[...], g_ref[...]).astype(jnp.bfloat16)
    o_ref[...] = jnp.dot(mn, wb[...], preferred_element_type=jnp.float32).astype(o_ref.dtype)


def _memkv(mem2, g, w, layer):
    t, d = mem2.shape
    n = w.shape[2]
    bm = 512
    return pl.pallas_call(
        _memkv_kernel,
        grid=(t // bm,),
        in_specs=[pl.BlockSpec((bm, d), lambda i: (i, 0)), pl.BlockSpec((1, d), lambda i: (0, 0)),
                  pl.BlockSpec((None, d, n), lambda i: (layer, 0, 0))],
        out_specs=pl.BlockSpec((bm, n), lambda i: (i, 0)),
        out_shape=jax.ShapeDtypeStruct((t, n), jnp.bfloat16),
        scratch_shapes=[pltpu.VMEM((d, n), jnp.bfloat16)],
        compiler_params=_cparams(("arbitrary",)),
        name="mem_kv",
    )(mem2, g, w)


XA_BM = 512


def _mix_out_xattn_kernel(m_ref, wo_ref, wq_ref, kv_ref, wmo_ref, x_ref, g1_ref, g2_ref, g3_ref, g4_ref,
                          xo_ref, ho_ref, x1_s, h1_s):
    f32 = jnp.float32
    bm = m_ref.shape[0]
    _finish_chunked(lambda rows: jnp.dot(m_ref[rows, :], wo_ref[...], preferred_element_type=f32),
                    bm, x_ref, g1_ref, g2_ref, x1_s, h1_s)
    q = jnp.dot(h1_s[...], wq_ref[...], preferred_element_type=f32).astype(jnp.bfloat16)
    scale = MEM_HEAD_DIM ** -0.5 * LOG2E
    hcols = [slice(h * LANES, (h + 1) * LANES) for h in range(MEM_HEADS)]
    ss = [lax.dot_general(q[:, c], kv_ref[:, c], _NT, preferred_element_type=f32) * scale for c in hcols]
    ms = [jnp.max(s, axis=-1, keepdims=True) for s in ss]
    es = [jnp.exp2(s - m) for s, m in zip(ss, ms)]
    ls = [jnp.sum(e, axis=-1, keepdims=True) for e in es]
    pv = [jnp.dot(e.astype(jnp.bfloat16), kv_ref[:, MEM_W + h * LANES:MEM_W + (h + 1) * LANES],
                  preferred_element_type=f32) for h, e in enumerate(es)]
    o = jnp.concatenate([(o_h / l).astype(jnp.bfloat16) for o_h, l in zip(pv, ls)], axis=-1)
    _finish_chunked(lambda rows: jnp.dot(o[rows, :], wmo_ref[...], preferred_element_type=f32),
                    bm, x1_s, g3_ref, g4_ref, xo_ref, ho_ref)


def _mix_out_xattn(merged, w_o, w_mq, kv, w_mo, layer, x, g1, g2, g3, g4):
    t, d = merged.shape
    bm = XA_BM
    per_batch = SEQ // bm
    once = pl.Buffered(1)
    row = pl.BlockSpec((bm, d), lambda i: (i, 0))
    gain = pl.BlockSpec((1, d), lambda i: (0, 0))
    return pl.pallas_call(
        _mix_out_xattn_kernel,
        grid=(t // bm,),
        in_specs=[
            row,
            pl.BlockSpec((None, d, d), lambda i: (layer, 0, 0), pipeline_mode=once),
            pl.BlockSpec((None, d, MEM_W), lambda i: (layer, 0, 0), pipeline_mode=once),
            pl.BlockSpec((MEM_LEN, 2 * MEM_W), lambda i: (i // per_batch, 0)),
            pl.BlockSpec((None, MEM_W, d), lambda i: (layer, 0, 0), pipeline_mode=once),
            row, gain, gain, gain, gain,
        ],
        out_specs=[row, row],
        out_shape=[jax.ShapeDtypeStruct((t, d), jnp.float32), jax.ShapeDtypeStruct((t, d), jnp.bfloat16)],
        scratch_shapes=[pltpu.VMEM((bm, d), jnp.float32), pltpu.VMEM((bm, d), jnp.bfloat16)],
        compiler_params=_cparams(("parallel",)),
        name="mix_out_xattn",
    )(merged, w_o, w_mq, kv, w_mo, x, g1, g2, g3, g4)


def _ffn_up_kernel(h_ref, wgf_ref, wvf_ref, cwg_ref, cwv_ref, cbg_ref, cbv_ref, wd_ref, o_ref, wdo_ref,
                   wg_ref, wv_ref, ug, uv):
    j = pl.program_id(0)
    i = pl.program_id(1)

    slab = j * pl.num_programs(1) + i

    @pl.when(slab < D_FF // wd_ref.shape[0])
    def _():
        wdo_ref[...] = wd_ref[...].astype(wdo_ref.dtype)

    @pl.when(slab >= D_FF // wd_ref.shape[0])
    def _():
        wdo_ref[...] = jnp.zeros(wdo_ref.shape, wdo_ref.dtype)

    tiles_per_seq = SEQ // FF_BM
    last = FF_P // FF_BN - 1
    valid = D_FF - last * FF_BN
    shift = FF_BN - valid

    @pl.when((i == 0) & (j < last))
    def _():
        wg_ref[...] = wgf_ref[...].astype(wg_ref.dtype)
        wv_ref[...] = wvf_ref[...].astype(wv_ref.dtype)

    @pl.when((i == 0) & (j == last))
    def _():
        zeros = jnp.zeros((wg_ref.shape[0], FF_BN - valid), wg_ref.dtype)
        wg_ref[:, :valid] = wgf_ref[:, :valid].astype(wg_ref.dtype)
        wv_ref[:, :valid] = wvf_ref[:, shift:].astype(wv_ref.dtype)
        wg_ref[:, valid:] = zeros
        wv_ref[:, valid:] = zeros

    for u in (ug, uv):
        @pl.when(i % tiles_per_seq == 0)
        def _():
            u[0:SUBLANES, :] = jnp.zeros((SUBLANES, FF_BN), jnp.float32)

        @pl.when(i % tiles_per_seq != 0)
        def _():
            u[0:SUBLANES, :] = u[FF_BM:FF_BM + SUBLANES, :]

    def conv(u, cw_ref, cb_ref, r0):
        base = SUBLANES + r0
        acc = cb_ref[...] + u[base - 2:base - 2 + FF_CH, :] * cw_ref[0:1, :]
        acc = acc + u[base - 1:base - 1 + FF_CH, :] * cw_ref[1:2, :]
        return acc + u[base:base + FF_CH, :] * cw_ref[2:3, :]

    for c in range(FF_BM // FF_CH):
        r0 = c * FF_CH
        hb = h_ref[r0:r0 + FF_CH, :]
        ug[SUBLANES + r0:SUBLANES + r0 + FF_CH, :] = jnp.dot(hb, wg_ref[...], preferred_element_type=jnp.float32)
        uv[SUBLANES + r0:SUBLANES + r0 + FF_CH, :] = jnp.dot(hb, wv_ref[...], preferred_element_type=jnp.float32)
        gate = conv(ug, cwg_ref, cbg_ref, r0)
        val = conv(uv, cwv_ref, cbv_ref, r0)
        o_ref[r0:r0 + FF_CH, :] = (jax.nn.gelu(gate) * val).astype(o_ref.dtype)


def _ffn_up(h, w_up, w_down, layer, conv_w, conv_b):
    t, d = h.shape
    nj = FF_P // FF_BN
    ni = t // FF_BM
    slab = FF_P // (nj * ni)
    assert slab * nj * ni == FF_P and D_FF % slab == 0 and slab % BF16_ROWS == 0
    n_real = D_FF // slab
    w_block = (None, pl.Element(d), pl.Element(FF_BN))
    return pl.pallas_call(
        _ffn_up_kernel,
        grid=(nj, ni),
        in_specs=[
            pl.BlockSpec((FF_BM, d), lambda j, i: (i, 0)),
            pl.BlockSpec(w_block, lambda j, i: (layer, 0, pl.multiple_of(j * FF_BN, LANES))),
            pl.BlockSpec(w_block, lambda j, i: (
                layer, 0, pl.multiple_of(jnp.minimum(D_FF + j * FF_BN, 2 * D_FF - FF_BN), LANES))),
            pl.BlockSpec((CONV_W, FF_BN), lambda j, i: (0, j)),
            pl.BlockSpec((CONV_W, FF_BN), lambda j, i: (0, nj + j)),
            pl.BlockSpec((1, FF_BN), lambda j, i: (0, j)),
            pl.BlockSpec((1, FF_BN), lambda j, i: (0, nj + j)),
            pl.BlockSpec((None, slab, d), lambda j, i: (layer, jnp.minimum(j * ni + i, n_real - 1), 0)),
        ],
        out_specs=[pl.BlockSpec((FF_BM, FF_BN), lambda j, i: (i, j)),
                   pl.BlockSpec((slab, d), lambda j, i: (j * ni + i, 0))],
        out_shape=[jax.ShapeDtypeStruct((t, FF_P), jnp.bfloat16),
                   jax.ShapeDtypeStruct((FF_P, w_down.shape[2]), jnp.bfloat16)],
        scratch_shapes=[pltpu.VMEM((d, FF_BN), jnp.bfloat16), pltpu.VMEM((d, FF_BN), jnp.bfloat16),
                        pltpu.VMEM((FF_BM + SUBLANES, FF_BN), jnp.float32),
                        pltpu.VMEM((FF_BM + SUBLANES, FF_BN), jnp.float32)],
        compiler_params=_cparams(("arbitrary", "arbitrary")),
        name="ffn_up",
    )(h, w_up, w_up, conv_w, conv_w, conv_b, conv_b, w_down)


def _pad_heads(w, axis):
    h = w.shape[0]
    zero = jnp.zeros_like(w)
    even = jnp.concatenate([w, zero], axis=axis)
    odd = jnp.concatenate([zero, w], axis=axis)
    sel = (jnp.arange(h) % 2 == 0).reshape((h, 1, 1))
    return jnp.where(sel, even, odd)


def _toeplitz(fn, rows, cols):
    ks = np.concatenate([np.arange(0, cols), np.arange(-(rows - 1), 0)])
    w = fn(ks)
    h, period = w.shape
    x = jnp.tile(w, (1, rows))[:, :rows * (period - 1)].reshape(h, rows, period - 1)
    return x[:, :, :cols].astype(jnp.float32)


def _band(rows, cols, left):
    diff = left + np.arange(rows)[:, None] // CHUNK - np.arange(cols)[None, :] // CHUNK
    return (diff >= 0) & (diff <= left)


def _bias_a(rel_bias):
    fn = lambda ks: rel_bias[np.clip(A_PAD - ks, -A_MAX_REL, A_MAX_REL) + A_MAX_REL].T
    bias = jnp.where(_band(QB, A_WIN, A_LEFT_CHUNKS)[None], _toeplitz(fn, QB, A_WIN) * LOG2E, NEG)
    return bias.reshape(A_HEADS * QB, A_WIN)


def _bias_c(t5_c):
    fn = lambda ks: t5_c[_t5_bucket(jnp.asarray(ks - C_PAD, jnp.int32))].T
    bias = jnp.where(_band(QB, C_WIN, C_LEFT_CHUNKS)[None], _toeplitz(fn, QB, C_WIN) * LOG2E, NEG)
    return bias.reshape(C_Q_HEADS * QB, C_WIN)


def _bias_b(t5_b):
    tiles = []
    for n in range(B_NEAR + 1):
        off = KB * (n - (B_QK - 1)) if n < B_NEAR else SEQ
        fn = lambda ks, off=off: t5_b[_t5_bucket(jnp.asarray(-ks - off, jnp.int32))].T
        tile = _toeplitz(fn, KB, BQ)
        tiles.append(jnp.transpose(tile * LOG2E, (1, 0, 2)).reshape(KB, HQ))
    return jnp.stack(tiles)


def _pad_ff(a, dtype):
    z = jnp.zeros((a.shape[0], FF_P - D_FF), dtype)
    return jnp.concatenate([a[:, :D_FF].astype(dtype), z, a[:, D_FF:].astype(dtype), z], axis=1)


def kernel(x, mem, t5_table, norm_gains, w_in, a_rel_bias, ckv_gain, w_uk, w_uv, sinks, w_branch, w_o,
           mem_gain, w_mq, w_mkv, w_mo, w_up, conv_w, conv_b, w_down):
    bf16 = jnp.bfloat16
    xs = x.reshape(TOKENS, D_MODEL)
    mem2 = mem.reshape(BATCH * MEM_LEN, D_MODEL)
    tri = jnp.asarray(np.tril(np.ones((KB, KB), np.float32), -1), bf16)
    bias_b = _bias_b(t5_table[:, :B_HEADS])
    bias_c = _bias_c(t5_table[:, B_HEADS:])
    gains = norm_gains.reshape(DEPTH, 6, 1, D_MODEL)
    w_in_t = jnp.swapaxes(w_in, 1, 2)

    h = _norm(xs, gains[0, 0])
    for l in range(DEPTH):
        g = gains[l]
        proja = _proj(h, w_in_t, l, 0, IN_A, 1024, gate=False, name="in_proj_a")
        projc = _proj(h, w_in_t, l, O_QC, C_COLS, C_COLS, gate=False, name="in_proj_c")
        gates = _proj(h, w_in_t, l, O_GL, 3 * D_MODEL, 1024, gate=True, name="in_proj_g")
        proja3 = proja.reshape(BATCH, SEQ, IN_A)
        oa = _attn_a(proja3, _bias_a(a_rel_bias[l]))
        wuk = jnp.transpose(_pad_heads(w_uk[l], axis=2), (0, 2, 1)).astype(bf16)
        wuv = _pad_heads(w_uv[l], axis=2).astype(bf16)
        ob = _attn_b(proja3, ckv_gain[l].reshape(1, B_LATENT), wuk, wuv, bias_b, tri)
        oc = _attn_c(sinks[l], projc.reshape(BATCH, SEQ, C_COLS), bias_c)
        merged, (w_o_b, w_mq_b, w_mo_b) = _merge(oa.reshape(TOKENS, A_W), ob.reshape(TOKENS, B_W),
                                                 oc.reshape(TOKENS, C_W), gates, w_branch, l, (w_o, w_mq, w_mo))
        kv = _memkv(mem2, mem_gain[l].reshape(1, D_MODEL), w_mkv, l)
        xs, h = _mix_out_xattn(merged, w_o_b[None], w_mq_b[None], kv, w_mo_b[None], 0, xs, g[1], g[2], g[3], g[4])
        hidden, w_dn_b = _ffn_up(h, w_up, w_down, l, _pad_ff(conv_w[l], jnp.float32),
                                 _pad_ff(conv_b[l].reshape(1, -1), jnp.float32))
        g_next = gains[l + 1, 0] if l + 1 < DEPTH else None
        xs, h = _gemm_res(hidden, w_dn_b[None], 0, xs, g[5], g_next, bk=DOWN_BK, name="ffn_down")
    return xs.reshape(BATCH, SEQ, D_MODEL)
```

```python
import functools
import math

import numpy as np
import jax
import jax.numpy as jnp
from jax import lax
from jax.experimental import pallas as pl
from jax.experimental.pallas import tpu as pltpu

D_MODEL = 2048
BATCH = 4
SEQ = 2048
DEPTH = 2
TOKENS = BATCH * SEQ
CHUNK = 64
EPS = 1e-6
NEG = -1e30
LOG2E = math.log2(math.e)
A_HEADS = 8
A_LEFT_CHUNKS = 8
A_MAX_REL = 128
A_W = 512
B_HEADS = 8
B_W = 512
B_LATENT = 128
IDX_HEADS = 8
IDX_DIM = 64
TOPK = 256
C_Q_HEADS = 16
C_GROUP = 8
C_W = 1024
C_LEFT_CHUNKS = 2
T5_BUCKETS = 32
T5_MAX_DIST = 256
MEM_LEN = 256
MEM_HEADS = 4
MEM_HEAD_DIM = 128
MEM_W = 512
D_FF = 5504
CONV_W = 3

LANES = 128
SUBLANES = 8
HEAD_DIM = 64
QB = 128
KB = 256
VMEM_LIMIT = 56 * 1024 * 1024

O_QA, O_KA, O_VA, O_QB = 0, 512, 1024, 1536
O_CKV = 2048
O_QI = 2176
O_KI = 2688
O_WI = 2752
O_QC = 2760
O_GL = 4040
IN_W = O_GL + 3 * D_MODEL
IN_A = 3072
X_COL0 = 2048
C_COLS = O_GL - O_QC
PROJ_BM = 1024
PROJ_TR = 256

FF_P = 5632
FF_BN = 512
FF_BM = 2048
FF_CH = 256
DOWN_BK = 2816

A_WIN = (A_LEFT_CHUNKS + 2) * CHUNK
C_WIN = (C_LEFT_CHUNKS + 2) * CHUNK
A_PAD = A_LEFT_CHUNKS * CHUNK
C_PAD = C_LEFT_CHUNKS * CHUNK
BQ = 256
B_QK = BQ // KB
B_NEAR = B_QK + 1
NKB = SEQ // KB
HQ = B_HEADS * BQ

_NT = (((1,), (1,)), ((), ()))
_TN = (((0,), (0,)), ((), ()))


def _cparams(sem):
    return pltpu.CompilerParams(dimension_semantics=sem, vmem_limit_bytes=VMEM_LIMIT)


def _t5_bucket(rel):
    half = T5_BUCKETS // 2
    max_exact = half // 2
    sign = jnp.where(rel > 0, half, 0)
    d = jnp.abs(rel)
    d_f = jnp.maximum(d, 1).astype(jnp.float32)
    large = max_exact + (jnp.log(d_f / max_exact) / math.log(T5_MAX_DIST / max_exact) * (half - max_exact)).astype(jnp.int32)
    large = jnp.minimum(large, half - 1)
    return sign + jnp.where(d < max_exact, d, large)


def _far_bucket_is_constant():
    assert BQ % KB == 0
    d = np.arange(KB + 1, SEQ, dtype=np.float32)
    assert d[0] > T5_MAX_DIST
    large = 8 + (np.log(d / 8) / math.log(T5_MAX_DIST / 8) * 8).astype(np.int32)
    return bool(np.all(np.minimum(large, 15) == 15))


assert _far_bucket_is_constant()


def _rms(v, g):
    return v * lax.rsqrt(jnp.mean(v * v, axis=-1, keepdims=True) + EPS) * g


def _norm_kernel(x_ref, g_ref, o_ref):
    o_ref[...] = _rms(x_ref[...], g_ref[...]).astype(o_ref.dtype)


def _norm(x, g, bm=1024):
    t, d = x.shape
    return pl.pallas_call(
        _norm_kernel,
        grid=(t // bm,),
        in_specs=[pl.BlockSpec((bm, d), lambda i: (i, 0)), pl.BlockSpec((1, d), lambda i: (0, 0))],
        out_specs=pl.BlockSpec((bm, d), lambda i: (i, 0)),
        out_shape=jax.ShapeDtypeStruct((t, d), jnp.bfloat16),
        compiler_params=_cparams(("parallel",)),
        name="rmsnorm",
    )(x, g)


def _proj_kernel(gate, h_ref, wt_ref, o_ref, wb):
    @pl.when(pl.program_id(1) == 0)
    def _():
        for c in range(wt_ref.shape[0] // PROJ_TR):
            rows = slice(c * PROJ_TR, (c + 1) * PROJ_TR)
            wb[:, rows] = wt_ref[rows, :].T.astype(wb.dtype)

    acc = jnp.dot(h_ref[...], wb[...], preferred_element_type=jnp.float32)
    o_ref[...] = (jax.nn.sigmoid(acc) if gate else acc).astype(o_ref.dtype)


def _proj(h, w_t, layer, col0, n, bn, gate, name):
    t, k = h.shape
    assert col0 % SUBLANES == 0 and bn % SUBLANES == 0
    w_spec = pl.BlockSpec((None, pl.Element(bn), pl.Element(k)),
                          lambda j, i: (layer, pl.multiple_of(col0 + j * bn, SUBLANES), 0))
    return pl.pallas_call(
        functools.partial(_proj_kernel, gate),
        grid=(n // bn, t // PROJ_BM),
        in_specs=[pl.BlockSpec((PROJ_BM, k), lambda j, i: (i, 0)), w_spec],
        out_specs=pl.BlockSpec((PROJ_BM, bn), lambda j, i: (i, j)),
        out_shape=jax.ShapeDtypeStruct((t, n), jnp.bfloat16),
        scratch_shapes=[pltpu.VMEM((k, bn), jnp.bfloat16)],
        compiler_params=_cparams(("arbitrary", "arbitrary")),
        name=name,
    )(h, w_t)


def _lane_masks(dtype):
    lane = lax.broadcasted_iota(jnp.int32, (1, LANES), 1)
    lo = (lane < HEAD_DIM).astype(jnp.float32)
    return lo.astype(dtype), (1.0 - lo).astype(dtype)


def _swap_halves(x):
    return pltpu.roll(x.astype(jnp.float32), HEAD_DIM, 1).astype(x.dtype)


def _fill_padded(src, dst_ref, pad):
    dst_ref[0:pad, :] = jnp.zeros((pad, dst_ref.shape[1]), dst_ref.dtype)
    dst_ref[pad:pad + SEQ, :] = src


def _attn_a_kernel(q_ref, k_ref, v_ref, bias_ref, o_ref, kpad, vpad):
    i = pl.program_id(1)

    @pl.when(i == 0)
    def _():
        _fill_padded(k_ref[...], kpad, A_PAD)
        _fill_padded(v_ref[...], vpad, A_PAD)

    start = pl.multiple_of(i * QB, QB)
    kw = kpad[pl.ds(start, A_WIN), :]
    vw = vpad[pl.ds(start, A_WIN), :]
    mlo, mhi = _lane_masks(jnp.bfloat16)
    lane = lax.broadcasted_iota(jnp.int32, (QB, LANES), 1)
    kchunk = lax.broadcasted_iota(jnp.int32, (QB, A_WIN), 1) // CHUNK
    pad_mask = jnp.where(kchunk >= A_LEFT_CHUNKS - 2 * i, 0.0, NEG)
    scale = HEAD_DIM ** -0.5 * LOG2E
    pad2 = jnp.concatenate([pad_mask, pad_mask], axis=0)
    npair = A_HEADS // 2
    pcols = [slice(p * LANES, (p + 1) * LANES) for p in range(npair)]
    ss = []
    for p in range(npair):
        qp = q_ref[:, pcols[p]]
        q2 = jnp.concatenate([qp * mlo, qp * mhi], axis=0)
        s = lax.dot_general(q2, kw[:, pcols[p]], _NT, preferred_element_type=jnp.float32)
        ss.append(s * scale + bias_ref[2 * p * QB:(2 * p + 2) * QB, :] + pad2)
    ms = [jnp.max(s, axis=-1, keepdims=True) for s in ss]
    es = [jnp.exp2(s - m) for s, m in zip(ss, ms)]
    ls = [jnp.sum(e, axis=-1, keepdims=True) for e in es]
    outs = [jnp.dot(e.astype(jnp.bfloat16), vw[:, pcols[p]], preferred_element_type=jnp.float32)
            for p, e in enumerate(es)]
    for p in range(npair):
        o = outs[p] / ls[p]
        o_ref[:, pcols[p]] = jnp.where(lane < HEAD_DIM, o[:QB], o[QB:]).astype(o_ref.dtype)


def _attn_a(proj3, bias):
    nq = SEQ // QB
    return pl.pallas_call(
        _attn_a_kernel,
        grid=(BATCH, nq),
        in_specs=[
            pl.BlockSpec((None, QB, A_W), lambda b, i: (b, i, O_QA // A_W)),
            pl.BlockSpec((None, SEQ, A_W), lambda b, i: (b, 0, O_KA // A_W)),
            pl.BlockSpec((None, SEQ, A_W), lambda b, i: (b, 0, O_VA // A_W)),
            pl.BlockSpec((A_HEADS * QB, A_WIN), lambda b, i: (0, 0)),
        ],
        out_specs=pl.BlockSpec((None, QB, A_W), lambda b, i: (b, i, 0)),
        out_shape=jax.ShapeDtypeStruct((BATCH, SEQ, A_W), jnp.bfloat16),
        scratch_shapes=[pltpu.VMEM((SEQ + A_PAD, A_W), jnp.bfloat16), pltpu.VMEM((SEQ + A_PAD, A_W), jnp.bfloat16)],
        compiler_params=_cparams(("arbitrary", "arbitrary")),
        name="attn_a",
    )(proj3, proj3, proj3, bias)


def _attn_c_kernel(sink_ref, q_ref, k_ref, v_ref, bias_ref, o_ref, kpad, kswp, vpad, vswp):
    i = pl.program_id(1)

    @pl.when(i == 0)
    def _():
        k = k_ref[...]
        v = v_ref[...]
        _fill_padded(k, kpad, C_PAD)
        _fill_padded(_swap_halves(k), kswp, C_PAD)
        _fill_padded(v, vpad, C_PAD)
        _fill_padded(_swap_halves(v), vswp, C_PAD)

    start = pl.multiple_of(i * QB, QB)
    mlo, mhi = _lane_masks(jnp.bfloat16)
    lane = lax.broadcasted_iota(jnp.int32, (QB, LANES), 1)
    kchunk = lax.broadcasted_iota(jnp.int32, (QB, C_WIN), 1) // CHUNK
    pad_mask = jnp.where(kchunk >= C_LEFT_CHUNKS - 2 * i, 0.0, NEG)
    scale = HEAD_DIM ** -0.5 * LOG2E
    qs = [q_ref[:, p * LANES:(p + 1) * LANES] for p in range(C_GROUP)]
    npair = C_GROUP // 2
    stacks = []
    for straight in (True, False):
        kref, vref = (kpad, vpad) if straight else (kswp, vswp)
        kw = kref[pl.ds(start, C_WIN), :]
        vw = vref[pl.ds(start, C_WIN), :]
        halves = [int((p >= npair) == straight) for p in range(C_GROUP)]
        qg = jnp.concatenate([qs[p] * (mhi if halves[p] else mlo) for p in range(C_GROUP)], axis=0)
        s_all = lax.dot_general(qg, kw, _NT, preferred_element_type=jnp.float32)
        ps, ls = [], []
        for p in range(C_GROUP):
            h = 2 * p + halves[p]
            s = s_all[p * QB:(p + 1) * QB] * scale + bias_ref[h * QB:(h + 1) * QB, :] + pad_mask
            sink = sink_ref[h] * LOG2E
            m = jnp.maximum(jnp.max(s, axis=-1, keepdims=True), sink)
            e = jnp.exp2(s - m)
            ls.append(jnp.sum(e, axis=-1, keepdims=True) + jnp.exp2(sink - m))
            ps.append(e.astype(jnp.bfloat16))
        o_all = jnp.dot(jnp.concatenate(ps, axis=0), vw, preferred_element_type=jnp.float32)
        stacks.append(([o_all[p * QB:(p + 1) * QB] / ls[p] for p in range(C_GROUP)], halves))
    for p in range(C_GROUP):
        (o1, h1), (o2, _) = stacks
        lo, hi = (o2[p], o1[p]) if h1[p] else (o1[p], o2[p])
        o_ref[:, p * LANES:(p + 1) * LANES] = jnp.where(lane < HEAD_DIM, lo, hi).astype(o_ref.dtype)


def _attn_c(sinks, projc3, bias):
    nq = SEQ // QB
    pad_buf = pltpu.VMEM((SEQ + C_PAD, LANES), jnp.bfloat16)
    return pl.pallas_call(
        _attn_c_kernel,
        grid=(BATCH, nq),
        in_specs=[
            pl.BlockSpec(memory_space=pltpu.SMEM),
            pl.BlockSpec((None, QB, C_W), lambda b, i: (b, i, 0)),
            pl.BlockSpec((None, SEQ, LANES), lambda b, i: (b, 0, C_W // LANES)),
            pl.BlockSpec((None, SEQ, LANES), lambda b, i: (b, 0, C_W // LANES + 1)),
            pl.BlockSpec((C_Q_HEADS * QB, C_WIN), lambda b, i: (0, 0)),
        ],
        out_specs=pl.BlockSpec((None, QB, C_W), lambda b, i: (b, i, 0)),
        out_shape=jax.ShapeDtypeStruct((BATCH, SEQ, C_W), jnp.bfloat16),
        scratch_shapes=[pad_buf, pad_buf, pad_buf, pad_buf],
        compiler_params=_cparams(("arbitrary", "arbitrary")),
        name="attn_c",
    )(sinks, projc3, projc3, projc3, bias)


def _sort_key(x):
    bits = lax.bitcast_convert_type(x + 0.0, jnp.int32)
    return bits ^ ((bits >> 31) & jnp.int32(0x7FFFFFFF))


def _rows_tree(x, op, slab):
    parts = [x[r:r + slab, :] for r in range(0, x.shape[0], slab)]
    while len(parts) > 1:
        nxt = [op(parts[k], parts[k + 1]) for k in range(0, len(parts) - 1, 2)]
        parts = nxt + ([parts[-1]] if len(parts) % 2 else [])
    return parts[0]


def _rows8(x, op):
    return _rows_tree(x, op, SUBLANES)


BF16_ROWS = 16


def _bit_search(count_ge, target, v, above, top_bit):
    def step(t, carry):
        v, above = carry
        cand = v | (jnp.int32(1) << (top_bit - t))
        c = count_ge(cand)
        keep = c >= target
        return jnp.where(keep, cand, v), jnp.where(keep, above, c)

    return lax.fori_loop(0, top_bit + 1, step, (v, above))


def _attn_b_kernel(qb_ref, x_ref, kiw_ref, ckv_ref, gain_ref, wuk_ref, wuv_ref, bias_ref, tri_ref,
                   o_ref, ckvn, kd, ql, qi_all, sk, dg, acc_s, p_s):
    i = pl.program_id(1)
    nkb = (i + 1) * B_QK
    mlo, mhi = _lane_masks(jnp.bfloat16)
    f32 = jnp.float32

    @pl.when(i == 0)
    def _():
        ckvn[...] = _rms(ckv_ref[...].astype(f32), gain_ref[...]).astype(ckvn.dtype)
        kiw = kiw_ref[...].astype(f32)
        lane = lax.broadcasted_iota(jnp.int32, kiw.shape, 1)
        kd[...] = jnp.where(lane < HEAD_DIM, kiw, pltpu.roll(kiw, HEAD_DIM, 1)).astype(kd.dtype)

    for h in range(B_HEADS):
        rows = slice(h * BQ, (h + 1) * BQ)
        qlat = jnp.dot(qb_ref[:, (h // 2) * LANES:(h // 2 + 1) * LANES], wuk_ref[h], preferred_element_type=f32)
        ql[rows, :] = (qlat * (HEAD_DIM ** -0.5 * LOG2E)).astype(ql.dtype)
        qcol = O_QI - X_COL0 + (h // 2) * LANES
        qi_all[rows, :] = x_ref[:, qcol:qcol + LANES] * (mhi if h % 2 else mlo)
    wcol = O_KI - X_COL0
    wi_t = x_ref[:, wcol:wcol + LANES].astype(f32).T * (IDX_HEADS ** -0.5 * IDX_DIM ** -0.5)
    wi_rows = [wi_t[O_WI - O_KI + h:O_WI - O_KI + h + 1, :] for h in range(IDX_HEADS)]

    kpos = lax.broadcasted_iota(jnp.int32, (KB, BQ), 0)
    key_limit = ((i * BQ + lax.broadcasted_iota(jnp.int32, (1, BQ), 1)) // CHUNK + 1) * CHUNK

    def admissible(kb):
        return kb * KB + kpos < key_limit

    def score_block(kb, carry):
        kblk = kd[pl.ds(pl.multiple_of(kb * KB, KB), KB), :]
        dots = lax.dot_general(kblk, qi_all[...], _NT, preferred_element_type=f32)
        score = jnp.zeros((KB, BQ), f32)
        for h in range(IDX_HEADS):
            score = score + jnp.maximum(dots[:, h * BQ:(h + 1) * BQ], 0.0) * wi_rows[h]
        sk[kb] = _sort_key(jnp.where(admissible(kb), score, NEG))
        return carry

    lax.fori_loop(0, nkb, score_block, 0)

    neg_key = _sort_key(jnp.full((1, 1), NEG, f32))
    n_rest = ((NKB - nkb) * KB).astype(f32)
    one, zero = jnp.ones((), jnp.bfloat16), jnp.zeros((), jnp.bfloat16)

    def byte_of(key, byte):
        return ((key >> 24) + 128) if byte == 3 else ((key >> (8 * byte)) & 255)

    target = jnp.full((1, BQ), TOPK, f32)
    prefix = jnp.zeros((1, BQ), jnp.int32)
    above = jnp.zeros((1, BQ), f32)
    for byte in (3, 2, 1, 0):
        def in_class(key, byte=byte, prefix=prefix):
            return (key >> (8 * byte + 8)) == prefix

        def prepare(kb, carry, byte=byte, in_class=in_class):
            key = sk[kb]
            digit = byte_of(key, byte).astype(f32)
            if byte < 3:
                digit = jnp.where(in_class(key), digit, -1.0)
            dg[kb] = digit.astype(dg.dtype)
            return carry

        lax.fori_loop(0, nkb, prepare, 0)
        rest_digit = byte_of(neg_key, byte)
        rest_on = in_class(neg_key) if byte < 3 else (neg_key == neg_key)

        def count_ge(cand, rest_digit=rest_digit, rest_on=rest_on):
            cand_b = cand.astype(f32).astype(jnp.bfloat16)

            def body(kb, acc):
                hit = jnp.where(dg[kb] >= cand_b, one, zero)
                return acc + _rows_tree(hit, jnp.add, BF16_ROWS).astype(f32)

            acc = lax.fori_loop(0, nkb, body, jnp.zeros((BF16_ROWS, BQ), f32))
            rest = jnp.where(rest_on & (rest_digit >= cand), n_rest, 0.0)
            return jnp.sum(acc, axis=0, keepdims=True) + rest

        digit_thr, above_here = _bit_search(count_ge, target, jnp.zeros((1, BQ), jnp.int32),
                                            jnp.zeros((1, BQ), f32), 7)
        prefix = (digit_thr - 128) if byte == 3 else (prefix * 256 + digit_thr)
        above = above + above_here
        target = target - above_here
    thr = prefix
    need = TOPK - above

    acc_s[...] = jnp.zeros(acc_s.shape, f32)

    far_bias = bias_ref[B_NEAR, 0:1, :]

    def attend(far, edge, kb, carry):
        eq_seen, m_old, l_old = carry
        keys = sk[kb]
        eq = keys == thr
        eq_f = jnp.where(eq, 1.0, 0.0)
        before = jnp.dot(tri_ref[...], eq_f.astype(jnp.bfloat16), preferred_element_type=f32)
        tie_add = jnp.where(eq_seen + before < need, 0.0, NEG)
        mask_add = jnp.where(keys > thr, 0.0, jnp.where(eq, tie_add, NEG))
        if edge:
            mask_add = jnp.where(admissible(kb), mask_add, NEG)
        cblk = ckvn[pl.ds(pl.multiple_of(kb * KB, KB), KB), :]
        dots = lax.dot_general(cblk, ql[...], _NT, preferred_element_type=f32)
        tile = nkb - 1 - kb
        m_parts, l_parts, a_parts = [], [], []
        for g in range(HQ // LANES):
            cols = slice(g * LANES, (g + 1) * LANES)
            qcols = slice((g % (BQ // LANES)) * LANES, (g % (BQ // LANES) + 1) * LANES)
            mo = m_old[:, cols]
            if far:
                s = dots[:, cols] + mask_add[:, qcols]
                shift = far_bias[:, cols]
                mn = jnp.maximum(mo, jnp.max(_rows8(s, jnp.maximum), axis=0, keepdims=True) + shift)
                e = jnp.exp2(s - (mn - shift))
            else:
                s = dots[:, cols] + bias_ref[tile, :, cols] + mask_add[:, qcols]
                mn = jnp.maximum(mo, jnp.max(_rows8(s, jnp.maximum), axis=0, keepdims=True))
                e = jnp.exp2(s - mn)
            a = jnp.exp2(mo - mn)
            l_parts.append(a * l_old[:, cols] + jnp.sum(_rows8(e, jnp.add), axis=0, keepdims=True))
            m_parts.append(mn)
            a_parts.append(a)
            p_s[:, cols] = e.astype(p_s.dtype)
        alpha = jnp.concatenate(a_parts, axis=1)
        acc_s[...] = acc_s[...] * alpha + lax.dot_general(cblk, p_s[...], _TN, preferred_element_type=f32)
        eq_seen = eq_seen + jnp.sum(_rows8(eq_f, jnp.add), axis=0, keepdims=True)
        return eq_seen, jnp.concatenate(m_parts, axis=1), jnp.concatenate(l_parts, axis=1)

    init = (jnp.zeros((1, BQ), f32), jnp.full((1, HQ), 4 * NEG, f32), jnp.zeros((1, HQ), f32))
    n_far = jnp.maximum(nkb - B_NEAR, 0)
    n_full = nkb - B_QK
    carry = lax.fori_loop(0, n_far, functools.partial(attend, True, False), init)
    carry = lax.fori_loop(n_far, n_full, functools.partial(attend, False, False), carry)
    _, _, l_fin = lax.fori_loop(n_full, nkb, functools.partial(attend, False, True), carry)

    o_lat_t = acc_s[...] / l_fin
    for p in range(B_HEADS // 2):
        out = jnp.zeros((BQ, LANES), f32)
        for h in (2 * p, 2 * p + 1):
            o_lat = o_lat_t[:, h * BQ:(h + 1) * BQ].T.astype(jnp.bfloat16)
            out = out + jnp.dot(o_lat, wuv_ref[h], preferred_element_type=f32)
        o_ref[:, p * LANES:(p + 1) * LANES] = out.astype(o_ref.dtype)


def _attn_b(proj3, gain, wuk, wuv, bias, tri, nbatch=BATCH):
    nq = SEQ // BQ
    xw = IN_A - X_COL0
    return pl.pallas_call(
        _attn_b_kernel,
        grid=(nbatch, nq),
        in_specs=[
            pl.BlockSpec((None, BQ, B_W), lambda b, i: (b, i, O_QB // B_W)),
            pl.BlockSpec((None, BQ, xw), lambda b, i: (b, i, X_COL0 // xw)),
            pl.BlockSpec((None, SEQ, LANES), lambda b, i: (b, 0, O_KI // LANES)),
            pl.BlockSpec((None, SEQ, LANES), lambda b, i: (b, 0, O_CKV // LANES)),
            pl.BlockSpec((1, B_LATENT), lambda b, i: (0, 0)),
            pl.BlockSpec((B_HEADS, LANES, B_LATENT), lambda b, i: (0, 0, 0)),
            pl.BlockSpec((B_HEADS, B_LATENT, LANES), lambda b, i: (0, 0, 0)),
            pl.BlockSpec((B_NEAR + 1, KB, HQ), lambda b, i: (0, 0, 0), pipeline_mode=pl.Buffered(1)),
            pl.BlockSpec((KB, KB), lambda b, i: (0, 0)),
        ],
        out_specs=pl.BlockSpec((None, BQ, B_W), lambda b, i: (b, i, 0)),
        out_shape=jax.ShapeDtypeStruct((nbatch, SEQ, B_W), jnp.bfloat16),
        scratch_shapes=[
            pltpu.VMEM((SEQ, B_LATENT), jnp.bfloat16),
            pltpu.VMEM((SEQ, LANES), jnp.bfloat16),
            pltpu.VMEM((HQ, B_LATENT), jnp.bfloat16),
            pltpu.VMEM((HQ, LANES), jnp.bfloat16),
            pltpu.VMEM((NKB, KB, BQ), jnp.int32),
            pltpu.VMEM((NKB, KB, BQ), jnp.bfloat16),
            pltpu.VMEM((B_LATENT, HQ), jnp.float32),
            pltpu.VMEM((KB, HQ), jnp.bfloat16),
        ],
        compiler_params=_cparams(("arbitrary", "arbitrary")),
        name="attn_b",
    )(proj3, proj3, proj3, proj3, gain, wuk, wuv, bias, tri)


MERGE_BN = 1024
MERGE_BM = 1024


def _merge_kernel(oa_ref, ob_ref, oc_ref, ga_ref, gb_ref, gc_ref, wa_ref, wb_ref, wc_ref,
                  wo_in, wq_in, wmo_in, o_ref, wo_out, wq_out, wmo_out, wa, wb, wc):
    f32 = jnp.float32

    @pl.when(pl.program_id(1) == 0)
    def _():
        wa[...] = wa_ref[...].astype(wa.dtype)
        wb[...] = wb_ref[...].astype(wb.dtype)
        wc[...] = wc_ref[...].astype(wc.dtype)

    for src, dst in ((wo_in, wo_out), (wq_in, wq_out), (wmo_in, wmo_out)):
        dst[...] = src[...].astype(dst.dtype)

    m = ga_ref[...].astype(f32) * jnp.dot(oa_ref[...], wa[...], preferred_element_type=f32)
    m = m + gb_ref[...].astype(f32) * jnp.dot(ob_ref[...], wb[...], preferred_element_type=f32)
    m = m + gc_ref[...].astype(f32) * jnp.dot(oc_ref[...], wc[...], preferred_element_type=f32)
    o_ref[...] = m.astype(o_ref.dtype)


def _merge(oa, ob, oc, gates, w_branch, layer, next_weights):
    t = oa.shape[0]
    bn, bm = MERGE_BN, MERGE_BM
    gstep = D_MODEL // bn
    ni = t // bm
    nsteps = (D_MODEL // bn) * ni
    slabs = [w.shape[1] // nsteps for w in next_weights]
    assert all(s * nsteps == w.shape[1] and s % BF16_ROWS == 0 for s, w in zip(slabs, next_weights))
    side_in = [pl.BlockSpec((None, s, w.shape[2]), lambda j, i: (layer, j * ni + i, 0))
               for s, w in zip(slabs, next_weights)]
    side_out = [pl.BlockSpec((s, w.shape[2]), lambda j, i: (j * ni + i, 0)) for s, w in zip(slabs, next_weights)]
    res = pl.pallas_call(
        _merge_kernel,
        grid=(D_MODEL // bn, ni),
        in_specs=[
            pl.BlockSpec((bm, A_W), lambda j, i: (i, 0)),
            pl.BlockSpec((bm, B_W), lambda j, i: (i, 0)),
            pl.BlockSpec((bm, C_W), lambda j, i: (i, 0)),
            pl.BlockSpec((bm, bn), lambda j, i: (i, j)),
            pl.BlockSpec((bm, bn), lambda j, i: (i, gstep + j)),
            pl.BlockSpec((bm, bn), lambda j, i: (i, 2 * gstep + j)),
            pl.BlockSpec((None, A_W, bn), lambda j, i: (layer, 0, j)),
            pl.BlockSpec((None, B_W, bn), lambda j, i: (layer, A_W // B_W, j)),
            pl.BlockSpec((None, C_W, bn), lambda j, i: (layer, (A_W + B_W) // C_W, j)),
        ] + side_in,
        out_specs=[pl.BlockSpec((bm, bn), lambda j, i: (i, j))] + side_out,
        out_shape=[jax.ShapeDtypeStruct((t, D_MODEL), jnp.bfloat16)]
        + [jax.ShapeDtypeStruct(w.shape[1:], jnp.bfloat16) for w in next_weights],
        scratch_shapes=[pltpu.VMEM((A_W, bn), jnp.bfloat16), pltpu.VMEM((B_W, bn), jnp.bfloat16),
                        pltpu.VMEM((C_W, bn), jnp.bfloat16)],
        compiler_params=_cparams(("arbitrary", "arbitrary")),
        name="merge",
    )(oa, ob, oc, gates, gates, gates, w_branch, w_branch, w_branch, *next_weights)
    return res[0], res[1:]


FINISH_CHUNKS = 4


def _finish_chunked(y_of_rows, nrows, x_ref, gp_ref, gn_ref, xo_ref, ho_ref):
    step = nrows // FINISH_CHUNKS
    for r in range(0, nrows, step):
        rows = slice(r, r + step)
        xn = x_ref[rows, :] + _rms(y_of_rows(rows), gp_ref[...])
        xo_ref[rows, :] = xn
        if ho_ref is not None:
            ho_ref[rows, :] = _rms(xn, gn_ref[...]).astype(ho_ref.dtype)


def _gemm_res_kernel(nk, with_next, a_ref, w_ref, x_ref, gp_ref, gn_ref, xo_ref, *rest):
    ho_ref = rest[0] if with_next else None
    f32 = jnp.float32
    bm = a_ref.shape[0]
    acc = rest[-1]
    k = pl.program_id(1)

    step = bm // FINISH_CHUNKS

    @pl.when(k == 0)
    def _():
        for r in range(0, bm, step):
            acc[r:r + step, :] = jnp.dot(a_ref[r:r + step, :], w_ref[...], preferred_element_type=f32)

    @pl.when((k > 0) & (k < nk - 1))
    def _():
        for r in range(0, bm, step):
            acc[r:r + step, :] += jnp.dot(a_ref[r:r + step, :], w_ref[...], preferred_element_type=f32)

    @pl.when(k == nk - 1)
    def _():
        _finish_chunked(lambda rows: acc[rows, :] + jnp.dot(a_ref[rows, :], w_ref[...], preferred_element_type=f32),
                        bm, x_ref, gp_ref, gn_ref, xo_ref, ho_ref)


def _gemm_res(a, w, layer, x, g_post, g_next, bk, bm=512, name="gemm_res"):
    t, kdim = a.shape
    n = w.shape[2]
    nk = kdim // bk
    assert nk > 1
    with_next = g_next is not None
    if g_next is None:
        g_next = g_post
    out_shape = [jax.ShapeDtypeStruct((t, n), jnp.float32)]
    out_specs = [pl.BlockSpec((bm, n), lambda i, k: (i, 0))]
    if with_next:
        out_shape.append(jax.ShapeDtypeStruct((t, n), jnp.bfloat16))
        out_specs.append(pl.BlockSpec((bm, n), lambda i, k: (i, 0)))
    res = pl.pallas_call(
        functools.partial(_gemm_res_kernel, nk, with_next),
        grid=(t // bm, nk),
        in_specs=[
            pl.BlockSpec((bm, bk), lambda i, k: (i, k)),
            pl.BlockSpec((None, bk, n), lambda i, k: (layer, k, 0)),
            pl.BlockSpec((bm, n), lambda i, k: (i, 0)),
            pl.BlockSpec((1, n), lambda i, k: (0, 0)),
            pl.BlockSpec((1, n), lambda i, k: (0, 0)),
        ],
        out_specs=out_specs,
        out_shape=out_shape,
        scratch_shapes=[pltpu.VMEM((bm, n), jnp.float32)],
        compiler_params=_cparams(("parallel", "arbitrary")),
        name=name,
    )(a, w, x, g_post, g_next)
    return (res[0], res[1]) if with_next else (res[0], None)


def _cast_once(pairs):
    @pl.when(pl.program_id(0) == 0)
    def _():
        for src, dst in pairs:
            dst[...] = src[...].astype(dst.dtype)


def _memkv_kernel(m_ref, g_ref, w_ref, o_ref, wb):
    _cast_once([(w_ref, wb)])
    mn = _rms(m_ref[...], g_ref[...]).astype(jnp.bfloat16)
    o_ref[...] = jnp.dot(mn, wb[...], preferred_element_type=jnp.float32).astype(o_ref.dtype)


def _memkv(mem2, g, w, layer):
    t, d = mem2.shape
    n = w.shape[2]
    bm = 512
    return pl.pallas_call(
        _memkv_kernel,
        grid=(t // bm,),
        in_specs=[pl.BlockSpec((bm, d), lambda i: (i, 0)), pl.BlockSpec((1, d), lambda i: (0, 0)),
                  pl.BlockSpec((None, d, n), lambda i: (layer, 0, 0))],
        out_specs=pl.BlockSpec((bm, n), lambda i: (i, 0)),
        out_shape=jax.ShapeDtypeStruct((t, n), jnp.bfloat16),
        scratch_shapes=[pltpu.VMEM((d, n), jnp.bfloat16)],
        compiler_params=_cparams(("arbitrary",)),
        name="mem_kv",
    )(mem2, g, w)


XA_BM = 512


def _mix_out_xattn_kernel(m_ref, wo_ref, wq_ref, kv_ref, wmo_ref, x_ref, g1_ref, g2_ref, g3_ref, g4_ref,
                          xo_ref, ho_ref, x1_s, h1_s):
    f32 = jnp.float32
    bm = m_ref.shape[0]
    _finish_chunked(lambda rows: jnp.dot(m_ref[rows, :], wo_ref[...], preferred_element_type=f32),
                    bm, x_ref, g1_ref, g2_ref, x1_s, h1_s)
    q = jnp.dot(h1_s[...], wq_ref[...], preferred_element_type=f32).astype(jnp.bfloat16)
    scale = MEM_HEAD_DIM ** -0.5 * LOG2E
    hcols = [slice(h * LANES, (h + 1) * LANES) for h in range(MEM_HEADS)]
    ss = [lax.dot_general(q[:, c], kv_ref[:, c], _NT, preferred_element_type=f32) * scale for c in hcols]
    ms = [jnp.max(s, axis=-1, keepdims=True) for s in ss]
    es = [jnp.exp2(s - m) for s, m in zip(ss, ms)]
    ls = [jnp.sum(e, axis=-1, keepdims=True) for e in es]
    pv = [jnp.dot(e.astype(jnp.bfloat16), kv_ref[:, MEM_W + h * LANES:MEM_W + (h + 1) * LANES],
                  preferred_element_type=f32) for h, e in enumerate(es)]
    o = jnp.concatenate([(o_h / l).astype(jnp.bfloat16) for o_h, l in zip(pv, ls)], axis=-1)
    _finish_chunked(lambda rows: jnp.dot(o[rows, :], wmo_ref[...], preferred_element_type=f32),
                    bm, x1_s, g3_ref, g4_ref, xo_ref, ho_ref)


def _mix_out_xattn(merged, w_o, w_mq, kv, w_mo, layer, x, g1, g2, g3, g4):
    t, d = merged.shape
    bm = XA_BM
    per_batch = SEQ // bm
    once = pl.Buffered(1)
    row = pl.BlockSpec((bm, d), lambda i: (i, 0))
    gain = pl.BlockSpec((1, d), lambda i: (0, 0))
    return pl.pallas_call(
        _mix_out_xattn_kernel,
        grid=(t // bm,),
        in_specs=[
            row,
            pl.BlockSpec((None, d, d), lambda i: (layer, 0, 0), pipeline_mode=once),
            pl.BlockSpec((None, d, MEM_W), lambda i: (layer, 0, 0), pipeline_mode=once),
            pl.BlockSpec((MEM_LEN, 2 * MEM_W), lambda i: (i // per_batch, 0)),
            pl.BlockSpec((None, MEM_W, d), lambda i: (layer, 0, 0), pipeline_mode=once),
            row, gain, gain, gain, gain,
        ],
        out_specs=[row, row],
        out_shape=[jax.ShapeDtypeStruct((t, d), jnp.float32), jax.ShapeDtypeStruct((t, d), jnp.bfloat16)],
        scratch_shapes=[pltpu.VMEM((bm, d), jnp.float32), pltpu.VMEM((bm, d), jnp.bfloat16)],
        compiler_params=_cparams(("parallel",)),
        name="mix_out_xattn",
    )(merged, w_o, w_mq, kv, w_mo, x, g1, g2, g3, g4)


def _ffn_up_kernel(h_ref, wgf_ref, wvf_ref, cwg_ref, cwv_ref, cbg_ref, cbv_ref, wd_ref, o_ref, wdo_ref,
                   wg_ref, wv_ref, ug, uv):
    j = pl.program_id(0)
    i = pl.program_id(1)

    slab = j * pl.num_programs(1) + i

    @pl.when(slab < D_FF // wd_ref.shape[0])
    def _():
        wdo_ref[...] = wd_ref[...].astype(wdo_ref.dtype)

    @pl.when(slab >= D_FF // wd_ref.shape[0])
    def _():
        wdo_ref[...] = jnp.zeros(wdo_ref.shape, wdo_ref.dtype)

    tiles_per_seq = SEQ // FF_BM
    last = FF_P // FF_BN - 1
    valid = D_FF - last * FF_BN
    shift = FF_BN - valid

    @pl.when((i == 0) & (j < last))
    def _():
        wg_ref[...] = wgf_ref[...].astype(wg_ref.dtype)
        wv_ref[...] = wvf_ref[...].astype(wv_ref.dtype)

    @pl.when((i == 0) & (j == last))
    def _():
        zeros = jnp.zeros((wg_ref.shape[0], FF_BN - valid), wg_ref.dtype)
        wg_ref[:, :valid] = wgf_ref[:, :valid].astype(wg_ref.dtype)
        wv_ref[:, :valid] = wvf_ref[:, shift:].astype(wv_ref.dtype)
        wg_ref[:, valid:] = zeros
        wv_ref[:, valid:] = zeros

    for u in (ug, uv):
        @pl.when(i % tiles_per_seq == 0)
        def _():
            u[0:SUBLANES, :] = jnp.zeros((SUBLANES, FF_BN), jnp.float32)

        @pl.when(i % tiles_per_seq != 0)
        def _():
            u[0:SUBLANES, :] = u[FF_BM:FF_BM + SUBLANES, :]

    def conv(u, cw_ref, cb_ref, r0):
        base = SUBLANES + r0
        acc = cb_ref[...] + u[base - 2:base - 2 + FF_CH, :] * cw_ref[0:1, :]
        acc = acc + u[base - 1:base - 1 + FF_CH, :] * cw_ref[1:2, :]
        return acc + u[base:base + FF_CH, :] * cw_ref[2:3, :]

    for c in range(FF_BM // FF_CH):
        r0 = c * FF_CH
        hb = h_ref[r0:r0 + FF_CH, :]
        ug[SUBLANES + r0:SUBLANES + r0 + FF_CH, :] = jnp.dot(hb, wg_ref[...], preferred_element_type=jnp.float32)
        uv[SUBLANES + r0:SUBLANES + r0 + FF_CH, :] = jnp.dot(hb, wv_ref[...], preferred_element_type=jnp.float32)
        gate = conv(ug, cwg_ref, cbg_ref, r0)
        val = conv(uv, cwv_ref, cbv_ref, r0)
        o_ref[r0:r0 + FF_CH, :] = (jax.nn.gelu(gate) * val).astype(o_ref.dtype)


def _ffn_up(h, w_up, w_down, layer, conv_w, conv_b):
    t, d = h.shape
    nj = FF_P // FF_BN
    ni = t // FF_BM
    slab = FF_P // (nj * ni)
    assert slab * nj * ni == FF_P and D_FF % slab == 0 and slab % BF16_ROWS == 0
    n_real = D_FF // slab
    w_block = (None, pl.Element(d), pl.Element(FF_BN))
    return pl.pallas_call(
        _ffn_up_kernel,
        grid=(nj, ni),
        in_specs=[
            pl.BlockSpec((FF_BM, d), lambda j, i: (i, 0)),
            pl.BlockSpec(w_block, lambda j, i: (layer, 0, pl.multiple_of(j * FF_BN, LANES))),
            pl.BlockSpec(w_block, lambda j, i: (
                layer, 0, pl.multiple_of(jnp.minimum(D_FF + j * FF_BN, 2 * D_FF - FF_BN), LANES))),
            pl.BlockSpec((CONV_W, FF_BN), lambda j, i: (0, j)),
            pl.BlockSpec((CONV_W, FF_BN), lambda j, i: (0, nj + j)),
            pl.BlockSpec((1, FF_BN), lambda j, i: (0, j)),
            pl.BlockSpec((1, FF_BN), lambda j, i: (0, nj + j)),
            pl.BlockSpec((None, slab, d), lambda j, i: (layer, jnp.minimum(j * ni + i, n_real - 1), 0)),
        ],
        out_specs=[pl.BlockSpec((FF_BM, FF_BN), lambda j, i: (i, j)),
                   pl.BlockSpec((slab, d), lambda j, i: (j * ni + i, 0))],
        out_shape=[jax.ShapeDtypeStruct((t, FF_P), jnp.bfloat16),
                   jax.ShapeDtypeStruct((FF_P, w_down.shape[2]), jnp.bfloat16)],
        scratch_shapes=[pltpu.VMEM((d, FF_BN), jnp.bfloat16), pltpu.VMEM((d, FF_BN), jnp.bfloat16),
                        pltpu.VMEM((FF_BM + SUBLANES, FF_BN), jnp.float32),
                        pltpu.VMEM((FF_BM + SUBLANES, FF_BN), jnp.float32)],
        compiler_params=_cparams(("arbitrary", "arbitrary")),
        name="ffn_up",
    )(h, w_up, w_up, conv_w, conv_w, conv_b, conv_b, w_down)


def _pad_heads(w, axis):
    h = w.shape[0]
    zero = jnp.zeros_like(w)
    even = jnp.concatenate([w, zero], axis=axis)
    odd = jnp.concatenate([zero, w], axis=axis)
    sel = (jnp.arange(h) % 2 == 0).reshape((h, 1, 1))
    return jnp.where(sel, even, odd)


def _toeplitz(fn, rows, cols):
    ks = np.concatenate([np.arange(0, cols), np.arange(-(rows - 1), 0)])
    w = fn(ks)
    h, period = w.shape
    x = jnp.tile(w, (1, rows))[:, :rows * (period - 1)].reshape(h, rows, period - 1)
    return x[:, :, :cols].astype(jnp.float32)


def _band(rows, cols, left):
    diff = left + np.arange(rows)[:, None] // CHUNK - np.arange(cols)[None, :] // CHUNK
    return (diff >= 0) & (diff <= left)


def _bias_a(rel_bias):
    fn = lambda ks: rel_bias[np.clip(A_PAD - ks, -A_MAX_REL, A_MAX_REL) + A_MAX_REL].T
    bias = jnp.where(_band(QB, A_WIN, A_LEFT_CHUNKS)[None], _toeplitz(fn, QB, A_WIN) * LOG2E, NEG)
    return bias.reshape(A_HEADS * QB, A_WIN)


def _bias_c(t5_c):
    fn = lambda ks: t5_c[_t5_bucket(jnp.asarray(ks - C_PAD, jnp.int32))].T
    bias = jnp.where(_band(QB, C_WIN, C_LEFT_CHUNKS)[None], _toeplitz(fn, QB, C_WIN) * LOG2E, NEG)
    return bias.reshape(C_Q_HEADS * QB, C_WIN)


def _bias_b(t5_b):
    tiles = []
    for n in range(B_NEAR + 1):
        off = KB * (n - (B_QK - 1)) if n < B_NEAR else SEQ
        fn = lambda ks, off=off: t5_b[_t5_bucket(jnp.asarray(-ks - off, jnp.int32))].T
        tile = _toeplitz(fn, KB, BQ)
        tiles.append(jnp.transpose(tile * LOG2E, (1, 0, 2)).reshape(KB, HQ))
    return jnp.stack(tiles)


def _pad_ff(a, dtype):
    z = jnp.zeros((a.shape[0], FF_P - D_FF), dtype)
    return jnp.concatenate([a[:, :D_FF].astype(dtype), z, a[:, D_FF:].astype(dtype), z], axis=1)


def kernel(x, mem, t5_table, norm_gains, w_in, a_rel_bias, ckv_gain, w_uk, w_uv, sinks, w_branch, w_o,
           mem_gain, w_mq, w_mkv, w_mo, w_up, conv_w, conv_b, w_down):
    bf16 = jnp.bfloat16
    xs = x.reshape(TOKENS, D_MODEL)
    mem2 = mem.reshape(BATCH * MEM_LEN, D_MODEL)
    tri = jnp.asarray(np.tril(np.ones((KB, KB), np.float32), -1), bf16)
    bias_b = _bias_b(t5_table[:, :B_HEADS])
    bias_c = _bias_c(t5_table[:, B_HEADS:])
    gains = norm_gains.reshape(DEPTH, 6, 1, D_MODEL)
    w_in_t = jnp.swapaxes(w_in, 1, 2)

    h = _norm(xs, gains[0, 0])
    for l in range(DEPTH):
        g = gains[l]
        proja = _proj(h, w_in_t, l, 0, IN_A, 1024, gate=False, name="in_proj_a")
        projc = _proj(h, w_in_t, l, O_QC, C_COLS, C_COLS, gate=False, name="in_proj_c")
        gates = _proj(h, w_in_t, l, O_GL, 3 * D_MODEL, 1024, gate=True, name="in_proj_g")
        proja3 = proja.reshape(BATCH, SEQ, IN_A)
        oa = _attn_a(proja3, _bias_a(a_rel_bias[l]))
        wuk = jnp.transpose(_pad_heads(w_uk[l], axis=2), (0, 2, 1)).astype(bf16)
        wuv = _pad_heads(w_uv[l], axis=2).astype(bf16)
        ob = _attn_b(proja3, ckv_gain[l].reshape(1, B_LATENT), wuk, wuv, bias_b, tri)
        oc = _attn_c(sinks[l], projc.reshape(BATCH, SEQ, C_COLS), bias_c)
        merged, (w_o_b, w_mq_b, w_mo_b) = _merge(oa.reshape(TOKENS, A_W), ob.reshape(TOKENS, B_W),
                                                 oc.reshape(TOKENS, C_W), gates, w_branch, l, (w_o, w_mq, w_mo))
        kv = _memkv(mem2, mem_gain[l].reshape(1, D_MODEL), w_mkv, l)
        xs, h = _mix_out_xattn(merged, w_o_b[None], w_mq_b[None], kv, w_mo_b[None], 0, xs, g[1], g[2], g[3], g[4])
        hidden, w_dn_b = _ffn_up(h, w_up, w_down, l, _pad_ff(conv_w[l], jnp.float32),
                                 _pad_ff(conv_b[l].reshape(1, -1), jnp.float32))
        g_next = gains[l + 1, 0] if l + 1 < DEPTH else None
        xs, h = _gemm_res(hidden, w_dn_b[None], 0, xs, g[5], g_next, bk=DOWN_BK, name="ffn_down")
    return xs.reshape(BATCH, SEQ, D_MODEL)
```

```python
import functools
import math

import numpy as np
import jax
import jax.numpy as jnp
from jax import lax
from jax.experimental import pallas as pl
from jax.experimental.pallas import tpu as pltpu

D_MODEL = 2048
BATCH = 4
SEQ = 2048
DEPTH = 2
TOKENS = BATCH * SEQ
CHUNK = 64
EPS = 1e-6
NEG = -1e30
LOG2E = math.log2(math.e)
A_HEADS = 8
A_LEFT_CHUNKS = 8
A_MAX_REL = 128
A_W = 512
B_HEADS = 8
B_W = 512
B_LATENT = 128
IDX_HEADS = 8
IDX_DIM = 64
TOPK = 256
C_Q_HEADS = 16
C_GROUP = 8
C_W = 1024
C_LEFT_CHUNKS = 2
T5_BUCKETS = 32
T5_MAX_DIST = 256
MEM_LEN = 256
MEM_HEADS = 4
MEM_HEAD_DIM = 128
MEM_W = 512
D_FF = 5504
CONV_W = 3

LANES = 128
SUBLANES = 8
HEAD_DIM = 64
QB = 128
KB = 256
VMEM_LIMIT = 56 * 1024 * 1024

O_QA, O_KA, O_VA, O_QB = 0, 512, 1024, 1536
O_CKV = 2048
O_QI = 2176
O_KI = 2688
O_WI = 2752
O_QC = 2760
O_GL = 4040
IN_W = O_GL + 3 * D_MODEL
IN_A = 3072
X_COL0 = 2048
C_COLS = O_GL - O_QC
PROJ_BM = 1024
PROJ_BN = 1536
PROJ_TR = 256

FF_P = 5632
FF_BN = 512
FF_BM = 2048
FF_CH = 256
DOWN_BK = 2816

A_WIN = (A_LEFT_CHUNKS + 2) * CHUNK
C_WIN = (C_LEFT_CHUNKS + 2) * CHUNK
A_PAD = A_LEFT_CHUNKS * CHUNK
C_PAD = C_LEFT_CHUNKS * CHUNK
BQ = 256
B_QK = BQ // KB
B_NEAR = B_QK + 1
NKB = SEQ // KB
HQ = B_HEADS * BQ

_NT = (((1,), (1,)), ((), ()))
_TN = (((0,), (0,)), ((), ()))


def _cparams(sem):
    return pltpu.CompilerParams(dimension_semantics=sem, vmem_limit_bytes=VMEM_LIMIT)


def _t5_bucket(rel):
    half = T5_BUCKETS // 2
    max_exact = half // 2
    sign = jnp.where(rel > 0, half, 0)
    d = jnp.abs(rel)
    d_f = jnp.maximum(d, 1).astype(jnp.float32)
    large = max_exact + (jnp.log(d_f / max_exact) / math.log(T5_MAX_DIST / max_exact) * (half - max_exact)).astype(jnp.int32)
    large = jnp.minimum(large, half - 1)
    return sign + jnp.where(d < max_exact, d, large)


def _far_bucket_is_constant():
    assert BQ % KB == 0
    d = np.arange(KB + 1, SEQ, dtype=np.float32)
    assert d[0] > T5_MAX_DIST
    large = 8 + (np.log(d / 8) / math.log(T5_MAX_DIST / 8) * 8).astype(np.int32)
    return bool(np.all(np.minimum(large, 15) == 15))


assert _far_bucket_is_constant()


def _rms(v, g):
    return v * lax.rsqrt(jnp.mean(v * v, axis=-1, keepdims=True) + EPS) * g


def _norm_kernel(x_ref, g_ref, o_ref):
    o_ref[...] = _rms(x_ref[...], g_ref[...]).astype(o_ref.dtype)


def _norm(x, g, bm=1024):
    t, d = x.shape
    return pl.pallas_call(
        _norm_kernel,
        grid=(t // bm,),
        in_specs=[pl.BlockSpec((bm, d), lambda i: (i, 0)), pl.BlockSpec((1, d), lambda i: (0, 0))],
        out_specs=pl.BlockSpec((bm, d), lambda i: (i, 0)),
        out_shape=jax.ShapeDtypeStruct((t, d), jnp.bfloat16),
        compiler_params=_cparams(("parallel",)),
        name="rmsnorm",
    )(x, g)


def _proj_kernel(gate, h_ref, wt_ref, o_ref, wb):
    @pl.when(pl.program_id(1) == 0)
    def _():
        for c in range(wt_ref.shape[0] // PROJ_TR):
            rows = slice(c * PROJ_TR, (c + 1) * PROJ_TR)
            wb[:, rows] = wt_ref[rows, :].T.astype(wb.dtype)

    step = h_ref.shape[0] // FINISH_CHUNKS
    for r in range(0, h_ref.shape[0], step):
        acc = jnp.dot(h_ref[r:r + step, :], wb[...], preferred_element_type=jnp.float32)
        o_ref[r:r + step, :] = (jax.nn.sigmoid(acc) if gate else acc).astype(o_ref.dtype)


def _proj(h, w_t, layer, col0, n, bn, gate, name):
    t, k = h.shape
    assert col0 % SUBLANES == 0 and bn % SUBLANES == 0
    w_spec = pl.BlockSpec((None, pl.Element(bn), pl.Element(k)),
                          lambda j, i: (layer, pl.multiple_of(col0 + j * bn, SUBLANES), 0))
    return pl.pallas_call(
        functools.partial(_proj_kernel, gate),
        grid=(n // bn, t // PROJ_BM),
        in_specs=[pl.BlockSpec((PROJ_BM, k), lambda j, i: (i, 0)), w_spec],
        out_specs=pl.BlockSpec((PROJ_BM, bn), lambda j, i: (i, j)),
        out_shape=jax.ShapeDtypeStruct((t, n), jnp.bfloat16),
        scratch_shapes=[pltpu.VMEM((k, bn), jnp.bfloat16)],
        compiler_params=_cparams(("arbitrary", "arbitrary")),
        name=name,
    )(h, w_t)


def _lane_masks(dtype):
    lane = lax.broadcasted_iota(jnp.int32, (1, LANES), 1)
    lo = (lane < HEAD_DIM).astype(jnp.float32)
    return lo.astype(dtype), (1.0 - lo).astype(dtype)


def _swap_halves(x):
    return pltpu.roll(x.astype(jnp.float32), HEAD_DIM, 1).astype(x.dtype)


def _fill_padded(src, dst_ref, pad):
    dst_ref[0:pad, :] = jnp.zeros((pad, dst_ref.shape[1]), dst_ref.dtype)
    dst_ref[pad:pad + SEQ, :] = src


def _attn_a_kernel(q_ref, k_ref, v_ref, bias_ref, o_ref, kpad, vpad):
    i = pl.program_id(1)

    @pl.when(i == 0)
    def _():
        _fill_padded(k_ref[...], kpad, A_PAD)
        _fill_padded(v_ref[...], vpad, A_PAD)

    start = pl.multiple_of(i * QB, QB)
    kw = kpad[pl.ds(start, A_WIN), :]
    vw = vpad[pl.ds(start, A_WIN), :]
    mlo, mhi = _lane_masks(jnp.bfloat16)
    lane = lax.broadcasted_iota(jnp.int32, (QB, LANES), 1)
    kchunk = lax.broadcasted_iota(jnp.int32, (QB, A_WIN), 1) // CHUNK
    pad_mask = jnp.where(kchunk >= A_LEFT_CHUNKS - 2 * i, 0.0, NEG)
    scale = HEAD_DIM ** -0.5 * LOG2E
    pad2 = jnp.concatenate([pad_mask, pad_mask], axis=0)
    npair = A_HEADS // 2
    pcols = [slice(p * LANES, (p + 1) * LANES) for p in range(npair)]
    ss = []
    for p in range(npair):
        qp = q_ref[:, pcols[p]]
        q2 = jnp.concatenate([qp * mlo, qp * mhi], axis=0)
        s = lax.dot_general(q2, kw[:, pcols[p]], _NT, preferred_element_type=jnp.float32)
        ss.append(s * scale + bias_ref[2 * p * QB:(2 * p + 2) * QB, :] + pad2)
    ms = [jnp.max(s, axis=-1, keepdims=True) for s in ss]
    es = [jnp.exp2(s - m) for s, m in zip(ss, ms)]
    ls = [jnp.sum(e, axis=-1, keepdims=True) for e in es]
    outs = [jnp.dot(e.astype(jnp.bfloat16), vw[:, pcols[p]], preferred_element_type=jnp.float32)
            for p, e in enumerate(es)]
    for p in range(npair):
        o = outs[p] / ls[p]
        o_ref[:, pcols[p]] = jnp.where(lane < HEAD_DIM, o[:QB], o[QB:]).astype(o_ref.dtype)


def _attn_a(proj3, bias):
    nq = SEQ // QB
    return pl.pallas_call(
        _attn_a_kernel,
        grid=(BATCH, nq),
        in_specs=[
            pl.BlockSpec((None, QB, A_W), lambda b, i: (b, i, O_QA // A_W)),
            pl.BlockSpec((None, SEQ, A_W), lambda b, i: (b, 0, O_KA // A_W)),
            pl.BlockSpec((None, SEQ, A_W), lambda b, i: (b, 0, O_VA // A_W)),
            pl.BlockSpec((A_HEADS * QB, A_WIN), lambda b, i: (0, 0)),
        ],
        out_specs=pl.BlockSpec((None, QB, A_W), lambda b, i: (b, i, 0)),
        out_shape=jax.ShapeDtypeStruct((BATCH, SEQ, A_W), jnp.bfloat16),
        scratch_shapes=[pltpu.VMEM((SEQ + A_PAD, A_W), jnp.bfloat16), pltpu.VMEM((SEQ + A_PAD, A_W), jnp.bfloat16)],
        compiler_params=_cparams(("arbitrary", "arbitrary")),
        name="attn_a",
    )(proj3, proj3, proj3, bias)


def _attn_c_kernel(sink_ref, q_ref, k_ref, v_ref, bias_ref, o_ref, kpad, kswp, vpad, vswp):
    i = pl.program_id(1)

    @pl.when(i == 0)
    def _():
        k = k_ref[...]
        v = v_ref[...]
        _fill_padded(k, kpad, C_PAD)
        _fill_padded(_swap_halves(k), kswp, C_PAD)
        _fill_padded(v, vpad, C_PAD)
        _fill_padded(_swap_halves(v), vswp, C_PAD)

    start = pl.multiple_of(i * QB, QB)
    mlo, mhi = _lane_masks(jnp.bfloat16)
    lane = lax.broadcasted_iota(jnp.int32, (QB, LANES), 1)
    kchunk = lax.broadcasted_iota(jnp.int32, (QB, C_WIN), 1) // CHUNK
    pad_mask = jnp.where(kchunk >= C_LEFT_CHUNKS - 2 * i, 0.0, NEG)
    scale = HEAD_DIM ** -0.5 * LOG2E
    qs = [q_ref[:, p * LANES:(p + 1) * LANES] for p in range(C_GROUP)]
    npair = C_GROUP // 2
    stacks = []
    for straight in (True, False):
        kref, vref = (kpad, vpad) if straight else (kswp, vswp)
        kw = kref[pl.ds(start, C_WIN), :]
        vw = vref[pl.ds(start, C_WIN), :]
        halves = [int((p >= npair) == straight) for p in range(C_GROUP)]
        qg = jnp.concatenate([qs[p] * (mhi if halves[p] else mlo) for p in range(C_GROUP)], axis=0)
        s_all = lax.dot_general(qg, kw, _NT, preferred_element_type=jnp.float32)
        ps, ls = [], []
        for p in range(C_GROUP):
            h = 2 * p + halves[p]
            s = s_all[p * QB:(p + 1) * QB] * scale + bias_ref[h * QB:(h + 1) * QB, :] + pad_mask
            sink = sink_ref[h] * LOG2E
            m = jnp.maximum(jnp.max(s, axis=-1, keepdims=True), sink)
            e = jnp.exp2(s - m)
            ls.append(jnp.sum(e, axis=-1, keepdims=True) + jnp.exp2(sink - m))
            ps.append(e.astype(jnp.bfloat16))
        o_all = jnp.dot(jnp.concatenate(ps, axis=0), vw, preferred_element_type=jnp.float32)
        stacks.append(([o_all[p * QB:(p + 1) * QB] / ls[p] for p in range(C_GROUP)], halves))
    for p in range(C_GROUP):
        (o1, h1), (o2, _) = stacks
        lo, hi = (o2[p], o1[p]) if h1[p] else (o1[p], o2[p])
        o_ref[:, p * LANES:(p + 1) * LANES] = jnp.where(lane < HEAD_DIM, lo, hi).astype(o_ref.dtype)


def _attn_c(sinks, projc3, bias):
    nq = SEQ // QB
    pad_buf = pltpu.VMEM((SEQ + C_PAD, LANES), jnp.bfloat16)
    return pl.pallas_call(
        _attn_c_kernel,
        grid=(BATCH, nq),
        in_specs=[
            pl.BlockSpec(memory_space=pltpu.SMEM),
            pl.BlockSpec((None, QB, C_W), lambda b, i: (b, i, 0)),
            pl.BlockSpec((None, SEQ, LANES), lambda b, i: (b, 0, C_W // LANES)),
            pl.BlockSpec((None, SEQ, LANES), lambda b, i: (b, 0, C_W // LANES + 1)),
            pl.BlockSpec((C_Q_HEADS * QB, C_WIN), lambda b, i: (0, 0)),
        ],
        out_specs=pl.BlockSpec((None, QB, C_W), lambda b, i: (b, i, 0)),
        out_shape=jax.ShapeDtypeStruct((BATCH, SEQ, C_W), jnp.bfloat16),
        scratch_shapes=[pad_buf, pad_buf, pad_buf, pad_buf],
        compiler_params=_cparams(("arbitrary", "arbitrary")),
        name="attn_c",
    )(sinks, projc3, projc3, projc3, bias)


def _sort_key(x):
    bits = lax.bitcast_convert_type(x + 0.0, jnp.int32)
    return bits ^ ((bits >> 31) & jnp.int32(0x7FFFFFFF))


def _rows_tree(x, op, slab):
    parts = [x[r:r + slab, :] for r in range(0, x.shape[0], slab)]
    while len(parts) > 1:
        nxt = [op(parts[k], parts[k + 1]) for k in range(0, len(parts) - 1, 2)]
        parts = nxt + ([parts[-1]] if len(parts) % 2 else [])
    return parts[0]


def _rows8(x, op):
    return _rows_tree(x, op, SUBLANES)


BF16_ROWS = 16


def _bit_search(count_ge, target, v, above, top_bit):
    def step(t, carry):
        v, above = carry
        cand = v | (jnp.int32(1) << (top_bit - t))
        c = count_ge(cand)
        keep = c >= target
        return jnp.where(keep, cand, v), jnp.where(keep, above, c)

    return lax.fori_loop(0, top_bit + 1, step, (v, above))


def _attn_b_kernel(qb_ref, x_ref, kiw_ref, ckv_ref, gain_ref, wuk_ref, wuv_ref, bias_ref, tri_ref,
                   o_ref, ckvn, kd, ql, qi_all, sk, dg, acc_s, p_s):
    i = pl.program_id(1)
    nkb = (i + 1) * B_QK
    mlo, mhi = _lane_masks(jnp.bfloat16)
    f32 = jnp.float32

    @pl.when(i == 0)
    def _():
        ckvn[...] = _rms(ckv_ref[...].astype(f32), gain_ref[...]).astype(ckvn.dtype)
        kiw = kiw_ref[...].astype(f32)
        lane = lax.broadcasted_iota(jnp.int32, kiw.shape, 1)
        kd[...] = jnp.where(lane < HEAD_DIM, kiw, pltpu.roll(kiw, HEAD_DIM, 1)).astype(kd.dtype)

    for h in range(B_HEADS):
        rows = slice(h * BQ, (h + 1) * BQ)
        qlat = jnp.dot(qb_ref[:, (h // 2) * LANES:(h // 2 + 1) * LANES], wuk_ref[h], preferred_element_type=f32)
        ql[rows, :] = (qlat * (HEAD_DIM ** -0.5 * LOG2E)).astype(ql.dtype)
        qcol = O_QI - X_COL0 + (h // 2) * LANES
        qi_all[rows, :] = x_ref[:, qcol:qcol + LANES] * (mhi if h % 2 else mlo)
    wcol = O_KI - X_COL0
    wi_t = x_ref[:, wcol:wcol + LANES].astype(f32).T * (IDX_HEADS ** -0.5 * IDX_DIM ** -0.5)
    wi_rows = [wi_t[O_WI - O_KI + h:O_WI - O_KI + h + 1, :] for h in range(IDX_HEADS)]

    kpos = lax.broadcasted_iota(jnp.int32, (KB, BQ), 0)
    key_limit = ((i * BQ + lax.broadcasted_iota(jnp.int32, (1, BQ), 1)) // CHUNK + 1) * CHUNK

    def admissible(kb):
        return kb * KB + kpos < key_limit

    def score_block(kb, carry):
        kblk = kd[pl.ds(pl.multiple_of(kb * KB, KB), KB), :]
        dots = lax.dot_general(kblk, qi_all[...], _NT, preferred_element_type=f32)
        score = jnp.zeros((KB, BQ), f32)
        for h in range(IDX_HEADS):
            score = score + jnp.maximum(dots[:, h * BQ:(h + 1) * BQ], 0.0) * wi_rows[h]
        sk[kb] = _sort_key(jnp.where(admissible(kb), score, NEG))
        return carry

    lax.fori_loop(0, nkb, score_block, 0)

    neg_key = _sort_key(jnp.full((1, 1), NEG, f32))
    n_rest = ((NKB - nkb) * KB).astype(f32)
    one, zero = jnp.ones((), jnp.bfloat16), jnp.zeros((), jnp.bfloat16)

    def byte_of(key, byte):
        return ((key >> 24) + 128) if byte == 3 else ((key >> (8 * byte)) & 255)

    target = jnp.full((1, BQ), TOPK, f32)
    prefix = jnp.zeros((1, BQ), jnp.int32)
    above = jnp.zeros((1, BQ), f32)
    for byte in (3, 2, 1, 0):
        def in_class(key, byte=byte, prefix=prefix):
            return (key >> (8 * byte + 8)) == prefix

        def prepare(kb, carry, byte=byte, in_class=in_class):
            key = sk[kb]
            digit = byte_of(key, byte).astype(f32)
            if byte < 3:
                digit = jnp.where(in_class(key), digit, -1.0)
            dg[kb] = digit.astype(dg.dtype)
            return carry

        lax.fori_loop(0, nkb, prepare, 0)
        rest_digit = byte_of(neg_key, byte)
        rest_on = in_class(neg_key) if byte < 3 else (neg_key == neg_key)

        def count_ge(cand, rest_digit=rest_digit, rest_on=rest_on):
            cand_b = cand.astype(f32).astype(jnp.bfloat16)

            def body(kb, acc):
                hit = jnp.where(dg[kb] >= cand_b, one, zero)
                return acc + _rows_tree(hit, jnp.add, BF16_ROWS).astype(f32)

            acc = lax.fori_loop(0, nkb, body, jnp.zeros((BF16_ROWS, BQ), f32))
            rest = jnp.where(rest_on & (rest_digit >= cand), n_rest, 0.0)
            return jnp.sum(acc, axis=0, keepdims=True) + rest

        digit_thr, above_here = _bit_search(count_ge, target, jnp.zeros((1, BQ), jnp.int32),
                                            jnp.zeros((1, BQ), f32), 7)
        prefix = (digit_thr - 128) if byte == 3 else (prefix * 256 + digit_thr)
        above = above + above_here
        target = target - above_here
    thr = prefix
    need = TOPK - above

    acc_s[...] = jnp.zeros(acc_s.shape, f32)

    far_bias = bias_ref[B_NEAR, 0:1, :]

    def attend(far, edge, kb, carry):
        eq_seen, m_old, l_old = carry
        keys = sk[kb]
        eq = keys == thr
        eq_f = jnp.where(eq, 1.0, 0.0)
        before = jnp.dot(tri_ref[...], eq_f.astype(jnp.bfloat16), preferred_element_type=f32)
        tie_add = jnp.where(eq_seen + before < need, 0.0, NEG)
        mask_add = jnp.where(keys > thr, 0.0, jnp.where(eq, tie_add, NEG))
        if edge:
            mask_add = jnp.where(admissible(kb), mask_add, NEG)
        cblk = ckvn[pl.ds(pl.multiple_of(kb * KB, KB), KB), :]
        dots = lax.dot_general(cblk, ql[...], _NT, preferred_element_type=f32)
        tile = nkb - 1 - kb
        m_parts, l_parts, a_parts = [], [], []
        for g in range(HQ // LANES):
            cols = slice(g * LANES, (g + 1) * LANES)
            qcols = slice((g % (BQ // LANES)) * LANES, (g % (BQ // LANES) + 1) * LANES)
            mo = m_old[:, cols]
            if far:
                s = dots[:, cols] + mask_add[:, qcols]
                shift = far_bias[:, cols]
                mn = jnp.maximum(mo, jnp.max(_rows8(s, jnp.maximum), axis=0, keepdims=True) + shift)
                e = jnp.exp2(s - (mn - shift))
            else:
                s = dots[:, cols] + bias_ref[tile, :, cols] + mask_add[:, qcols]
                mn = jnp.maximum(mo, jnp.max(_rows8(s, jnp.maximum), axis=0, keepdims=True))
                e = jnp.exp2(s - mn)
            a = jnp.exp2(mo - mn)
            l_parts.append(a * l_old[:, cols] + jnp.sum(_rows8(e, jnp.add), axis=0, keepdims=True))
            m_parts.append(mn)
            a_parts.append(a)
            p_s[:, cols] = e.astype(p_s.dtype)
        alpha = jnp.concatenate(a_parts, axis=1)
        acc_s[...] = acc_s[...] * alpha + lax.dot_general(cblk, p_s[...], _TN, preferred_element_type=f32)
        eq_seen = eq_seen + jnp.sum(_rows8(eq_f, jnp.add), axis=0, keepdims=True)
        return eq_seen, jnp.concatenate(m_parts, axis=1), jnp.concatenate(l_parts, axis=1)

    init = (jnp.zeros((1, BQ), f32), jnp.full((1, HQ), 4 * NEG, f32), jnp.zeros((1, HQ), f32))
    n_far = jnp.maximum(nkb - B_NEAR, 0)
    n_full = nkb - B_QK
    carry = lax.fori_loop(0, n_far, functools.partial(attend, True, False), init)
    carry = lax.fori_loop(n_far, n_full, functools.partial(attend, False, False), carry)
    _, _, l_fin = lax.fori_loop(n_full, nkb, functools.partial(attend, False, True), carry)

    o_lat_t = acc_s[...] / l_fin
    for p in range(B_HEADS // 2):
        out = jnp.zeros((BQ, LANES), f32)
        for h in (2 * p, 2 * p + 1):
            o_lat = o_lat_t[:, h * BQ:(h + 1) * BQ].T.astype(jnp.bfloat16)
            out = out + jnp.dot(o_lat, wuv_ref[h], preferred_element_type=f32)
        o_ref[:, p * LANES:(p + 1) * LANES] = out.astype(o_ref.dtype)


def _attn_b(proj3, gain, wuk, wuv, bias, tri, nbatch=BATCH):
    nq = SEQ // BQ
    xw = IN_A - X_COL0
    return pl.pallas_call(
        _attn_b_kernel,
        grid=(nbatch, nq),
        in_specs=[
            pl.BlockSpec((None, BQ, B_W), lambda b, i: (b, i, O_QB // B_W)),
            pl.BlockSpec((None, BQ, xw), lambda b, i: (b, i, X_COL0 // xw)),
            pl.BlockSpec((None, SEQ, LANES), lambda b, i: (b, 0, O_KI // LANES)),
            pl.BlockSpec((None, SEQ, LANES), lambda b, i: (b, 0, O_CKV // LANES)),
            pl.BlockSpec((1, B_LATENT), lambda b, i: (0, 0)),
            pl.BlockSpec((B_HEADS, LANES, B_LATENT), lambda b, i: (0, 0, 0)),
            pl.BlockSpec((B_HEADS, B_LATENT, LANES), lambda b, i: (0, 0, 0)),
            pl.BlockSpec((B_NEAR + 1, KB, HQ), lambda b, i: (0, 0, 0), pipeline_mode=pl.Buffered(1)),
            pl.BlockSpec((KB, KB), lambda b, i: (0, 0)),
        ],
        out_specs=pl.BlockSpec((None, BQ, B_W), lambda b, i: (b, i, 0)),
        out_shape=jax.ShapeDtypeStruct((nbatch, SEQ, B_W), jnp.bfloat16),
        scratch_shapes=[
            pltpu.VMEM((SEQ, B_LATENT), jnp.bfloat16),
            pltpu.VMEM((SEQ, LANES), jnp.bfloat16),
            pltpu.VMEM((HQ, B_LATENT), jnp.bfloat16),
            pltpu.VMEM((HQ, LANES), jnp.bfloat16),
            pltpu.VMEM((NKB, KB, BQ), jnp.int32),
            pltpu.VMEM((NKB, KB, BQ), jnp.bfloat16),
            pltpu.VMEM((B_LATENT, HQ), jnp.float32),
            pltpu.VMEM((KB, HQ), jnp.bfloat16),
        ],
        compiler_params=_cparams(("arbitrary", "arbitrary")),
        name="attn_b",
    )(proj3, proj3, proj3, proj3, gain, wuk, wuv, bias, tri)


MERGE_BN = 1024
MERGE_BM = 1024


def _merge_kernel(oa_ref, ob_ref, oc_ref, ga_ref, gb_ref, gc_ref, wa_ref, wb_ref, wc_ref,
                  wo_in, wq_in, wmo_in, o_ref, wo_out, wq_out, wmo_out, wa, wb, wc):
    f32 = jnp.float32

    @pl.when(pl.program_id(1) == 0)
    def _():
        wa[...] = wa_ref[...].astype(wa.dtype)
        wb[...] = wb_ref[...].astype(wb.dtype)
        wc[...] = wc_ref[...].astype(wc.dtype)

    for src, dst in ((wo_in, wo_out), (wq_in, wq_out), (wmo_in, wmo_out)):
        dst[...] = src[...].astype(dst.dtype)

    m = ga_ref[...].astype(f32) * jnp.dot(oa_ref[...], wa[...], preferred_element_type=f32)
    m = m + gb_ref[...].astype(f32) * jnp.dot(ob_ref[...], wb[...], preferred_element_type=f32)
    m = m + gc_ref[...].astype(f32) * jnp.dot(oc_ref[...], wc[...], preferred_element_type=f32)
    o_ref[...] = m.astype(o_ref.dtype)


def _merge(oa, ob, oc, gates, w_branch, layer, next_weights):
    t = oa.shape[0]
    bn, bm = MERGE_BN, MERGE_BM
    gstep = D_MODEL // bn
    ni = t // bm
    nsteps = (D_MODEL // bn) * ni
    slabs = [w.shape[1] // nsteps for w in next_weights]
    assert all(s * nsteps == w.shape[1] and s % BF16_ROWS == 0 for s, w in zip(slabs, next_weights))
    side_in = [pl.BlockSpec((None, s, w.shape[2]), lambda j, i: (layer, j * ni + i, 0))
               for s, w in zip(slabs, next_weights)]
    side_out = [pl.BlockSpec((s, w.shape[2]), lambda j, i: (j * ni + i, 0)) for s, w in zip(slabs, next_weights)]
    res = pl.pallas_call(
        _merge_kernel,
        grid=(D_MODEL // bn, ni),
        in_specs=[
            pl.BlockSpec((bm, A_W), lambda j, i: (i, 0)),
            pl.BlockSpec((bm, B_W), lambda j, i: (i, 0)),
            pl.BlockSpec((bm, C_W), lambda j, i: (i, 0)),
            pl.BlockSpec((bm, bn), lambda j, i: (i, j)),
            pl.BlockSpec((bm, bn), lambda j, i: (i, gstep + j)),
            pl.BlockSpec((bm, bn), lambda j, i: (i, 2 * gstep + j)),
            pl.BlockSpec((None, A_W, bn), lambda j, i: (layer, 0, j)),
            pl.BlockSpec((None, B_W, bn), lambda j, i: (layer, A_W // B_W, j)),
            pl.BlockSpec((None, C_W, bn), lambda j, i: (layer, (A_W + B_W) // C_W, j)),
        ] + side_in,
        out_specs=[pl.BlockSpec((bm, bn), lambda j, i: (i, j))] + side_out,
        out_shape=[jax.ShapeDtypeStruct((t, D_MODEL), jnp.bfloat16)]
        + [jax.ShapeDtypeStruct(w.shape[1:], jnp.bfloat16) for w in next_weights],
        scratch_shapes=[pltpu.VMEM((A_W, bn), jnp.bfloat16), pltpu.VMEM((B_W, bn), jnp.bfloat16),
                        pltpu.VMEM((C_W, bn), jnp.bfloat16)],
        compiler_params=_cparams(("arbitrary", "arbitrary")),
        name="merge",
    )(oa, ob, oc, gates, gates, gates, w_branch, w_branch, w_branch, *next_weights)
    return res[0], res[1:]


FINISH_CHUNKS = 4


def _finish_chunked(y_of_rows, nrows, x_ref, gp_ref, gn_ref, xo_ref, ho_ref):
    step = nrows // FINISH_CHUNKS
    for r in range(0, nrows, step):
        rows = slice(r, r + step)
        xn = x_ref[rows, :] + _rms(y_of_rows(rows), gp_ref[...])
        xo_ref[rows, :] = xn
        if ho_ref is not None:
            ho_ref[rows, :] = _rms(xn, gn_ref[...]).astype(ho_ref.dtype)


def _gemm_res_kernel(nk, with_next, a_ref, w_ref, x_ref, gp_ref, gn_ref, xo_ref, *rest):
    ho_ref = rest[0] if with_next else None
    f32 = jnp.float32
    bm = a_ref.shape[0]
    acc = rest[-1]
    k = pl.program_id(1)

    step = bm // FINISH_CHUNKS

    @pl.when(k == 0)
    def _():
        for r in range(0, bm, step):
            acc[r:r + step, :] = jnp.dot(a_ref[r:r + step, :], w_ref[...], preferred_element_type=f32)

    @pl.when((k > 0) & (k < nk - 1))
    def _():
        for r in range(0, bm, step):
            acc[r:r + step, :] += jnp.dot(a_ref[r:r + step, :], w_ref[...], preferred_element_type=f32)

    @pl.when(k == nk - 1)
    def _():
        _finish_chunked(lambda rows: acc[rows, :] + jnp.dot(a_ref[rows, :], w_ref[...], preferred_element_type=f32),
                        bm, x_ref, gp_ref, gn_ref, xo_ref, ho_ref)


def _gemm_res(a, w, layer, x, g_post, g_next, bk, bm=512, name="gemm_res"):
    t, kdim = a.shape
    n = w.shape[2]
    nk = kdim // bk
    assert nk > 1
    with_next = g_next is not None
    if g_next is None:
        g_next = g_post
    out_shape = [jax.ShapeDtypeStruct((t, n), jnp.float32)]
    out_specs = [pl.BlockSpec((bm, n), lambda i, k: (i, 0))]
    if with_next:
        out_shape.append(jax.ShapeDtypeStruct((t, n), jnp.bfloat16))
        out_specs.append(pl.BlockSpec((bm, n), lambda i, k: (i, 0)))
    res = pl.pallas_call(
        functools.partial(_gemm_res_kernel, nk, with_next),
        grid=(t // bm, nk),
        in_specs=[
            pl.BlockSpec((bm, bk), lambda i, k: (i, k)),
            pl.BlockSpec((None, bk, n), lambda i, k: (layer, k, 0)),
            pl.BlockSpec((bm, n), lambda i, k: (i, 0)),
            pl.BlockSpec((1, n), lambda i, k: (0, 0)),
            pl.BlockSpec((1, n), lambda i, k: (0, 0)),
        ],
        out_specs=out_specs,
        out_shape=out_shape,
        scratch_shapes=[pltpu.VMEM((bm, n), jnp.float32)],
        compiler_params=_cparams(("parallel", "arbitrary")),
        name=name,
    )(a, w, x, g_post, g_next)
    return (res[0], res[1]) if with_next else (res[0], None)


def _cast_once(pairs):
    @pl.when(pl.program_id(0) == 0)
    def _():
        for src, dst in pairs:
            dst[...] = src[...].astype(dst.dtype)


def _memkv_kernel(m_ref, g_ref, w_ref, o_ref, wb):
    _cast_once([(w_ref, wb)])
    mn = _rms(m_ref[...], g_ref[...]).astype(jnp.bfloat16)
    o_ref[...] = jnp.dot(mn, wb[...], preferred_element_type=jnp.float32).astype(o_ref.dtype)


def _memkv(mem2, g, w, layer):
    t, d = mem2.shape
    n = w.shape[2]
    bm = 512
    return pl.pallas_call(
        _memkv_kernel,
        grid=(t // bm,),
        in_specs=[pl.BlockSpec((bm, d), lambda i: (i, 0)), pl.BlockSpec((1, d), lambda i: (0, 0)),
                  pl.BlockSpec((None, d, n), lambda i: (layer, 0, 0))],
        out_specs=pl.BlockSpec((bm, n), lambda i: (i, 0)),
        out_shape=jax.ShapeDtypeStruct((t, n), jnp.bfloat16),
        scratch_shapes=[pltpu.VMEM((d, n), jnp.bfloat16)],
        compiler_params=_cparams(("arbitrary",)),
        name="mem_kv",
    )(mem2, g, w)


XA_BM = 512


def _mix_out_xattn_kernel(m_ref, wo_ref, wq_ref, kv_ref, wmo_ref, x_ref, g1_ref, g2_ref, g3_ref, g4_ref,
                          xo_ref, ho_ref, x1_s, h1_s):
    f32 = jnp.float32
    bm = m_ref.shape[0]
    _finish_chunked(lambda rows: jnp.dot(m_ref[rows, :], wo_ref[...], preferred_element_type=f32),
                    bm, x_ref, g1_ref, g2_ref, x1_s, h1_s)
    q = jnp.dot(h1_s[...], wq_ref[...], preferred_element_type=f32).astype(jnp.bfloat16)
    scale = MEM_HEAD_DIM ** -0.5 * LOG2E
    hcols = [slice(h * LANES, (h + 1) * LANES) for h in range(MEM_HEADS)]
    ss = [lax.dot_general(q[:, c], kv_ref[:, c], _NT, preferred_element_type=f32) * scale for c in hcols]
    ms = [jnp.max(s, axis=-1, keepdims=True) for s in ss]
    es = [jnp.exp2(s - m) for s, m in zip(ss, ms)]
    ls = [jnp.sum(e, axis=-1, keepdims=True) for e in es]
    pv = [jnp.dot(e.astype(jnp.bfloat16), kv_ref[:, MEM_W + h * LANES:MEM_W + (h + 1) * LANES],
                  preferred_element_type=f32) for h, e in enumerate(es)]
    o = jnp.concatenate([(o_h / l).astype(jnp.bfloat16) for o_h, l in zip(pv, ls)], axis=-1)
    _finish_chunked(lambda rows: jnp.dot(o[rows, :], wmo_ref[...], preferred_element_type=f32),
                    bm, x1_s, g3_ref, g4_ref, xo_ref, ho_ref)


def _mix_out_xattn(merged, w_o, w_mq, kv, w_mo, layer, x, g1, g2, g3, g4):
    t, d = merged.shape
    bm = XA_BM
    per_batch = SEQ // bm
    once = pl.Buffered(1)
    row = pl.BlockSpec((bm, d), lambda i: (i, 0))
    gain = pl.BlockSpec((1, d), lambda i: (0, 0))
    return pl.pallas_call(
        _mix_out_xattn_kernel,
        grid=(t // bm,),
        in_specs=[
            row,
            pl.BlockSpec((None, d, d), lambda i: (layer, 0, 0), pipeline_mode=once),
            pl.BlockSpec((None, d, MEM_W), lambda i: (layer, 0, 0), pipeline_mode=once),
            pl.BlockSpec((MEM_LEN, 2 * MEM_W), lambda i: (i // per_batch, 0)),
            pl.BlockSpec((None, MEM_W, d), lambda i: (layer, 0, 0), pipeline_mode=once),
            row, gain, gain, gain, gain,
        ],
        out_specs=[row, row],
        out_shape=[jax.ShapeDtypeStruct((t, d), jnp.float32), jax.ShapeDtypeStruct((t, d), jnp.bfloat16)],
        scratch_shapes=[pltpu.VMEM((bm, d), jnp.float32), pltpu.VMEM((bm, d), jnp.bfloat16)],
        compiler_params=_cparams(("parallel",)),
        name="mix_out_xattn",
    )(merged, w_o, w_mq, kv, w_mo, x, g1, g2, g3, g4)


def _ffn_up_kernel(h_ref, wgf_ref, wvf_ref, cwg_ref, cwv_ref, cbg_ref, cbv_ref, wd_ref, o_ref, wdo_ref,
                   wg_ref, wv_ref, ug, uv):
    j = pl.program_id(0)
    i = pl.program_id(1)

    slab = j * pl.num_programs(1) + i

    @pl.when(slab < D_FF // wd_ref.shape[0])
    def _():
        wdo_ref[...] = wd_ref[...].astype(wdo_ref.dtype)

    @pl.when(slab >= D_FF // wd_ref.shape[0])
    def _():
        wdo_ref[...] = jnp.zeros(wdo_ref.shape, wdo_ref.dtype)

    tiles_per_seq = SEQ // FF_BM
    last = FF_P // FF_BN - 1
    valid = D_FF - last * FF_BN
    shift = FF_BN - valid

    @pl.when((i == 0) & (j < last))
    def _():
        wg_ref[...] = wgf_ref[...].astype(wg_ref.dtype)
        wv_ref[...] = wvf_ref[...].astype(wv_ref.dtype)

    @pl.when((i == 0) & (j == last))
    def _():
        zeros = jnp.zeros((wg_ref.shape[0], FF_BN - valid), wg_ref.dtype)
        wg_ref[:, :valid] = wgf_ref[:, :valid].astype(wg_ref.dtype)
        wv_ref[:, :valid] = wvf_ref[:, shift:].astype(wv_ref.dtype)
        wg_ref[:, valid:] = zeros
        wv_ref[:, valid:] = zeros

    for u in (ug, uv):
        @pl.when(i % tiles_per_seq == 0)
        def _():
            u[0:SUBLANES, :] = jnp.zeros((SUBLANES, FF_BN), jnp.float32)

        @pl.when(i % tiles_per_seq != 0)
        def _():
            u[0:SUBLANES, :] = u[FF_BM:FF_BM + SUBLANES, :]

    def conv(u, cw_ref, cb_ref, r0):
        base = SUBLANES + r0
        acc = cb_ref[...] + u[base - 2:base - 2 + FF_CH, :] * cw_ref[0:1, :]
        acc = acc + u[base - 1:base - 1 + FF_CH, :] * cw_ref[1:2, :]
        return acc + u[base:base + FF_CH, :] * cw_ref[2:3, :]

    for c in range(FF_BM // FF_CH):
        r0 = c * FF_CH
        hb = h_ref[r0:r0 + FF_CH, :]
        ug[SUBLANES + r0:SUBLANES + r0 + FF_CH, :] = jnp.dot(hb, wg_ref[...], preferred_element_type=jnp.float32)
        uv[SUBLANES + r0:SUBLANES + r0 + FF_CH, :] = jnp.dot(hb, wv_ref[...], preferred_element_type=jnp.float32)
        gate = conv(ug, cwg_ref, cbg_ref, r0)
        val = conv(uv, cwv_ref, cbv_ref, r0)
        o_ref[r0:r0 + FF_CH, :] = (jax.nn.gelu(gate) * val).astype(o_ref.dtype)


def _ffn_up(h, w_up, w_down, layer, conv_w, conv_b):
    t, d = h.shape
    nj = FF_P // FF_BN
    ni = t // FF_BM
    slab = FF_P // (nj * ni)
    assert slab * nj * ni == FF_P and D_FF % slab == 0 and slab % BF16_ROWS == 0
    n_real = D_FF // slab
    w_block = (None, pl.Element(d), pl.Element(FF_BN))
    return pl.pallas_call(
        _ffn_up_kernel,
        grid=(nj, ni),
        in_specs=[
            pl.BlockSpec((FF_BM, d), lambda j, i: (i, 0)),
            pl.BlockSpec(w_block, lambda j, i: (layer, 0, pl.multiple_of(j * FF_BN, LANES))),
            pl.BlockSpec(w_block, lambda j, i: (
                layer, 0, pl.multiple_of(jnp.minimum(D_FF + j * FF_BN, 2 * D_FF - FF_BN), LANES))),
            pl.BlockSpec((CONV_W, FF_BN), lambda j, i: (0, j)),
            pl.BlockSpec((CONV_W, FF_BN), lambda j, i: (0, nj + j)),
            pl.BlockSpec((1, FF_BN), lambda j, i: (0, j)),
            pl.BlockSpec((1, FF_BN), lambda j, i: (0, nj + j)),
            pl.BlockSpec((None, slab, d), lambda j, i: (layer, jnp.minimum(j * ni + i, n_real - 1), 0)),
        ],
        out_specs=[pl.BlockSpec((FF_BM, FF_BN), lambda j, i: (i, j)),
                   pl.BlockSpec((slab, d), lambda j, i: (j * ni + i, 0))],
        out_shape=[jax.ShapeDtypeStruct((t, FF_P), jnp.bfloat16),
                   jax.ShapeDtypeStruct((FF_P, w_down.shape[2]), jnp.bfloat16)],
        scratch_shapes=[pltpu.VMEM((d, FF_BN), jnp.bfloat16), pltpu.VMEM((d, FF_BN), jnp.bfloat16),
                        pltpu.VMEM((FF_BM + SUBLANES, FF_BN), jnp.float32),
                        pltpu.VMEM((FF_BM + SUBLANES, FF_BN), jnp.float32)],
        compiler_params=_cparams(("arbitrary", "arbitrary")),
        name="ffn_up",
    )(h, w_up, w_up, conv_w, conv_w, conv_b, conv_b, w_down)


def _pad_heads(w, axis):
    h = w.shape[0]
    zero = jnp.zeros_like(w)
    even = jnp.concatenate([w, zero], axis=axis)
    odd = jnp.concatenate([zero, w], axis=axis)
    sel = (jnp.arange(h) % 2 == 0).reshape((h, 1, 1))
    return jnp.where(sel, even, odd)


def _toeplitz(fn, rows, cols):
    ks = np.concatenate([np.arange(0, cols), np.arange(-(rows - 1), 0)])
    w = fn(ks)
    h, period = w.shape
    x = jnp.tile(w, (1, rows))[:, :rows * (period - 1)].reshape(h, rows, period - 1)
    return x[:, :, :cols].astype(jnp.float32)


def _band(rows, cols, left):
    diff = left + np.arange(rows)[:, None] // CHUNK - np.arange(cols)[None, :] // CHUNK
    return (diff >= 0) & (diff <= left)


def _bias_a(rel_bias):
    fn = lambda ks: rel_bias[np.clip(A_PAD - ks, -A_MAX_REL, A_MAX_REL) + A_MAX_REL].T
    bias = jnp.where(_band(QB, A_WIN, A_LEFT_CHUNKS)[None], _toeplitz(fn, QB, A_WIN) * LOG2E, NEG)
    return bias.reshape(A_HEADS * QB, A_WIN)


def _bias_c(t5_c):
    fn = lambda ks: t5_c[_t5_bucket(jnp.asarray(ks - C_PAD, jnp.int32))].T
    bias = jnp.where(_band(QB, C_WIN, C_LEFT_CHUNKS)[None], _toeplitz(fn, QB, C_WIN) * LOG2E, NEG)
    return bias.reshape(C_Q_HEADS * QB, C_WIN)


def _bias_b(t5_b):
    tiles = []
    for n in range(B_NEAR + 1):
        off = KB * (n - (B_QK - 1)) if n < B_NEAR else SEQ
        fn = lambda ks, off=off: t5_b[_t5_bucket(jnp.asarray(-ks - off, jnp.int32))].T
        tile = _toeplitz(fn, KB, BQ)
        tiles.append(jnp.transpose(tile * LOG2E, (1, 0, 2)).reshape(KB, HQ))
    return jnp.stack(tiles)


def _pad_ff(a, dtype):
    z = jnp.zeros((a.shape[0], FF_P - D_FF), dtype)
    return jnp.concatenate([a[:, :D_FF].astype(dtype), z, a[:, D_FF:].astype(dtype), z], axis=1)


def kernel(x, mem, t5_table, norm_gains, w_in, a_rel_bias, ckv_gain, w_uk, w_uv, sinks, w_branch, w_o,
           mem_gain, w_mq, w_mkv, w_mo, w_up, conv_w, conv_b, w_down):
    bf16 = jnp.bfloat16
    xs = x.reshape(TOKENS, D_MODEL)
    mem2 = mem.reshape(BATCH * MEM_LEN, D_MODEL)
    tri = jnp.asarray(np.tril(np.ones((KB, KB), np.float32), -1), bf16)
    bias_b = _bias_b(t5_table[:, :B_HEADS])
    bias_c = _bias_c(t5_table[:, B_HEADS:])
    gains = norm_gains.reshape(DEPTH, 6, 1, D_MODEL)
    w_in_t = jnp.swapaxes(w_in, 1, 2)

    h = _norm(xs, gains[0, 0])
    for l in range(DEPTH):
        g = gains[l]
        proja = _proj(h, w_in_t, l, 0, IN_A, PROJ_BN, gate=False, name="in_proj_a")
        projc = _proj(h, w_in_t, l, O_QC, C_COLS, C_COLS, gate=False, name="in_proj_c")
        gates = _proj(h, w_in_t, l, O_GL, 3 * D_MODEL, PROJ_BN, gate=True, name="in_proj_g")
        proja3 = proja.reshape(BATCH, SEQ, IN_A)
        oa = _attn_a(proja3, _bias_a(a_rel_bias[l]))
        wuk = jnp.transpose(_pad_heads(w_uk[l], axis=2), (0, 2, 1)).astype(bf16)
        wuv = _pad_heads(w_uv[l], axis=2).astype(bf16)
        ob = _attn_b(proja3, ckv_gain[l].reshape(1, B_LATENT), wuk, wuv, bias_b, tri)
        oc = _attn_c(sinks[l], projc.reshape(BATCH, SEQ, C_COLS), bias_c)
        merged, (w_o_b, w_mq_b, w_mo_b) = _merge(oa.reshape(TOKENS, A_W), ob.reshape(TOKENS, B_W),
                                                 oc.reshape(TOKENS, C_W), gates, w_branch, l, (w_o, w_mq, w_mo))
        kv = _memkv(mem2, mem_gain[l].reshape(1, D_MODEL), w_mkv, l)
        xs, h = _mix_out_xattn(merged, w_o_b[None], w_mq_b[None], kv, w_mo_b[None], 0, xs, g[1], g[2], g[3], g[4])
        hidden, w_dn_b = _ffn_up(h, w_up, w_down, l, _pad_ff(conv_w[l], jnp.float32),
                                 _pad_ff(conv_b[l].reshape(1, -1), jnp.float32))
        g_next = gains[l + 1, 0] if l + 1 < DEPTH else None
        xs, h = _gemm_res(hidden, w_dn_b[None], 0, xs, g[5], g_next, bk=DOWN_BK, name="ffn_down")
    return xs.reshape(BATCH, SEQ, D_MODEL)
```

```python
import functools
import math

import numpy as np
import jax
import jax.numpy as jnp
from jax import lax
from jax.experimental import pallas as pl
from jax.experimental.pallas import tpu as pltpu

D_MODEL = 2048
BATCH = 4
SEQ = 2048
DEPTH = 2
TOKENS = BATCH * SEQ
CHUNK = 64
EPS = 1e-6
NEG = -1e30
LOG2E = math.log2(math.e)
A_HEADS = 8
A_LEFT_CHUNKS = 8
A_MAX_REL = 128
A_W = 512
B_HEADS = 8
B_W = 512
B_LATENT = 128
IDX_HEADS = 8
IDX_DIM = 64
TOPK = 256
C_Q_HEADS = 16
C_GROUP = 8
C_W = 1024
C_LEFT_CHUNKS = 2
T5_BUCKETS = 32
T5_MAX_DIST = 256
MEM_LEN = 256
MEM_HEADS = 4
MEM_HEAD_DIM = 128
MEM_W = 512
D_FF = 5504
CONV_W = 3

LANES = 128
SUBLANES = 8
HEAD_DIM = 64
QB = 128
KB = 256
VMEM_LIMIT = 56 * 1024 * 1024

O_QA, O_KA, O_VA, O_QB = 0, 512, 1024, 1536
O_CKV = 2048
O_QI = 2176
O_KI = 2688
O_WI = 2752
O_QC = 2760
O_GL = 4040
IN_W = O_GL + 3 * D_MODEL
IN_A = 3072
X_COL0 = 2048
C_COLS = O_GL - O_QC
PROJ_BM = 1024
PROJ_BN = 1536
PROJ_TR = 256

FF_P = 5632
FF_BN = 512
FF_BM = 2048
FF_CH = 256
DOWN_BK = 2816

A_WIN = (A_LEFT_CHUNKS + 2) * CHUNK
C_WIN = (C_LEFT_CHUNKS + 2) * CHUNK
A_PAD = A_LEFT_CHUNKS * CHUNK
C_PAD = C_LEFT_CHUNKS * CHUNK
BQ = 256
B_QK = BQ // KB
B_NEAR = B_QK + 1
NKB = SEQ // KB
HQ = B_HEADS * BQ

_NT = (((1,), (1,)), ((), ()))
_TN = (((0,), (0,)), ((), ()))


def _cparams(sem):
    return pltpu.CompilerParams(dimension_semantics=sem, vmem_limit_bytes=VMEM_LIMIT)


def _t5_bucket(rel):
    half = T5_BUCKETS // 2
    max_exact = half // 2
    sign = jnp.where(rel > 0, half, 0)
    d = jnp.abs(rel)
    d_f = jnp.maximum(d, 1).astype(jnp.float32)
    large = max_exact + (jnp.log(d_f / max_exact) / math.log(T5_MAX_DIST / max_exact) * (half - max_exact)).astype(jnp.int32)
    large = jnp.minimum(large, half - 1)
    return sign + jnp.where(d < max_exact, d, large)


def _far_bucket_is_constant():
    assert BQ % KB == 0
    d = np.arange(KB + 1, SEQ, dtype=np.float32)
    assert d[0] > T5_MAX_DIST
    large = 8 + (np.log(d / 8) / math.log(T5_MAX_DIST / 8) * 8).astype(np.int32)
    return bool(np.all(np.minimum(large, 15) == 15))


assert _far_bucket_is_constant()


def _rms(v, g):
    return v * lax.rsqrt(jnp.mean(v * v, axis=-1, keepdims=True) + EPS) * g


def _proj_kernel(gate, h_ref, wt_ref, o_ref, wb):
    @pl.when(pl.program_id(1) == 0)
    def _():
        for c in range(wt_ref.shape[0] // PROJ_TR):
            rows = slice(c * PROJ_TR, (c + 1) * PROJ_TR)
            wb[:, rows] = wt_ref[rows, :].T.astype(wb.dtype)

    step = h_ref.shape[0] // FINISH_CHUNKS
    for r in range(0, h_ref.shape[0], step):
        acc = jnp.dot(h_ref[r:r + step, :], wb[...], preferred_element_type=jnp.float32)
        o_ref[r:r + step, :] = (jax.nn.sigmoid(acc) if gate else acc).astype(o_ref.dtype)


def _proj(h, w_t, layer, col0, n, bn, gate, name):
    t, k = h.shape
    assert col0 % SUBLANES == 0 and bn % SUBLANES == 0
    w_spec = pl.BlockSpec((None, pl.Element(bn), pl.Element(k)),
                          lambda j, i: (layer, pl.multiple_of(col0 + j * bn, SUBLANES), 0))
    return pl.pallas_call(
        functools.partial(_proj_kernel, gate),
        grid=(n // bn, t // PROJ_BM),
        in_specs=[pl.BlockSpec((PROJ_BM, k), lambda j, i: (i, 0)), w_spec],
        out_specs=pl.BlockSpec((PROJ_BM, bn), lambda j, i: (i, j)),
        out_shape=jax.ShapeDtypeStruct((t, n), jnp.bfloat16),
        scratch_shapes=[pltpu.VMEM((k, bn), jnp.bfloat16)],
        compiler_params=_cparams(("arbitrary", "arbitrary")),
        name=name,
    )(h, w_t)


NORM_PROJ_BM = 512


def _norm_proj_kernel(x_ref, g_ref, wt_ref, o_ref, h_ref, wb):
    @pl.when(pl.program_id(0) == 0)
    def _():
        for c in range(wt_ref.shape[0] // PROJ_TR):
            rows = slice(c * PROJ_TR, (c + 1) * PROJ_TR)
            wb[:, rows] = wt_ref[rows, :].T.astype(wb.dtype)

    h_ref[...] = _rms(x_ref[...], g_ref[...]).astype(h_ref.dtype)
    step = PROJ_TR
    for r in range(0, x_ref.shape[0], step):
        acc = jnp.dot(h_ref[r:r + step, :], wb[...], preferred_element_type=jnp.float32)
        o_ref[r:r + step, :] = acc.astype(o_ref.dtype)


def _norm_proj(x, g, w_t, layer, col0, n):
    t, k = x.shape
    bm = NORM_PROJ_BM
    assert col0 % SUBLANES == 0 and n % PROJ_TR == 0
    return pl.pallas_call(
        _norm_proj_kernel,
        grid=(t // bm,),
        in_specs=[pl.BlockSpec((bm, k), lambda i: (i, 0)),
                  pl.BlockSpec((1, k), lambda i: (0, 0)),
                  pl.BlockSpec((None, pl.Element(n), pl.Element(k)), lambda i: (layer, col0, 0))],
        out_specs=[pl.BlockSpec((bm, n), lambda i: (i, 0)), pl.BlockSpec((bm, k), lambda i: (i, 0))],
        out_shape=[jax.ShapeDtypeStruct((t, n), jnp.bfloat16), jax.ShapeDtypeStruct((t, k), jnp.bfloat16)],
        scratch_shapes=[pltpu.VMEM((k, n), jnp.bfloat16)],
        compiler_params=_cparams(("arbitrary",)),
        name="norm_proj_c",
    )(x, g, w_t)


def _lane_masks(dtype):
    lane = lax.broadcasted_iota(jnp.int32, (1, LANES), 1)
    lo = (lane < HEAD_DIM).astype(jnp.float32)
    return lo.astype(dtype), (1.0 - lo).astype(dtype)


def _swap_halves(x):
    return pltpu.roll(x.astype(jnp.float32), HEAD_DIM, 1).astype(x.dtype)


def _fill_padded(src, dst_ref, pad):
    dst_ref[0:pad, :] = jnp.zeros((pad, dst_ref.shape[1]), dst_ref.dtype)
    dst_ref[pad:pad + SEQ, :] = src


def _attn_a_kernel(q_ref, k_ref, v_ref, bias_ref, o_ref, kpad, vpad):
    i = pl.program_id(1)

    @pl.when(i == 0)
    def _():
        _fill_padded(k_ref[...], kpad, A_PAD)
        _fill_padded(v_ref[...], vpad, A_PAD)

    start = pl.multiple_of(i * QB, QB)
    kw = kpad[pl.ds(start, A_WIN), :]
    vw = vpad[pl.ds(start, A_WIN), :]
    mlo, mhi = _lane_masks(jnp.bfloat16)
    lane = lax.broadcasted_iota(jnp.int32, (QB, LANES), 1)
    kchunk = lax.broadcasted_iota(jnp.int32, (QB, A_WIN), 1) // CHUNK
    pad_mask = jnp.where(kchunk >= A_LEFT_CHUNKS - 2 * i, 0.0, NEG)
    scale = HEAD_DIM ** -0.5 * LOG2E
    pad2 = jnp.concatenate([pad_mask, pad_mask], axis=0)
    npair = A_HEADS // 2
    pcols = [slice(p * LANES, (p + 1) * LANES) for p in range(npair)]
    ss = []
    for p in range(npair):
        qp = q_ref[:, pcols[p]]
        q2 = jnp.concatenate([qp * mlo, qp * mhi], axis=0)
        s = lax.dot_general(q2, kw[:, pcols[p]], _NT, preferred_element_type=jnp.float32)
        ss.append(s * scale + bias_ref[2 * p * QB:(2 * p + 2) * QB, :] + pad2)
    ms = [jnp.max(s, axis=-1, keepdims=True) for s in ss]
    es = [jnp.exp2(s - m) for s, m in zip(ss, ms)]
    ls = [jnp.sum(e, axis=-1, keepdims=True) for e in es]
    outs = [jnp.dot(e.astype(jnp.bfloat16), vw[:, pcols[p]], preferred_element_type=jnp.float32)
            for p, e in enumerate(es)]
    for p in range(npair):
        o = outs[p] / ls[p]
        o_ref[:, pcols[p]] = jnp.where(lane < HEAD_DIM, o[:QB], o[QB:]).astype(o_ref.dtype)


def _attn_a(proj3, bias):
    nq = SEQ // QB
    return pl.pallas_call(
        _attn_a_kernel,
        grid=(BATCH, nq),
        in_specs=[
            pl.BlockSpec((None, QB, A_W), lambda b, i: (b, i, O_QA // A_W)),
            pl.BlockSpec((None, SEQ, A_W), lambda b, i: (b, 0, O_KA // A_W)),
            pl.BlockSpec((None, SEQ, A_W), lambda b, i: (b, 0, O_VA // A_W)),
            pl.BlockSpec((A_HEADS * QB, A_WIN), lambda b, i: (0, 0)),
        ],
        out_specs=pl.BlockSpec((None, QB, A_W), lambda b, i: (b, i, 0)),
        out_shape=jax.ShapeDtypeStruct((BATCH, SEQ, A_W), jnp.bfloat16),
        scratch_shapes=[pltpu.VMEM((SEQ + A_PAD, A_W), jnp.bfloat16), pltpu.VMEM((SEQ + A_PAD, A_W), jnp.bfloat16)],
        compiler_params=_cparams(("arbitrary", "arbitrary")),
        name="attn_a",
    )(proj3, proj3, proj3, bias)


def _attn_c_kernel(sink_ref, q_ref, k_ref, v_ref, bias_ref, o_ref, kpad, kswp, vpad, vswp):
    i = pl.program_id(1)

    @pl.when(i == 0)
    def _():
        k = k_ref[...]
        v = v_ref[...]
        _fill_padded(k, kpad, C_PAD)
        _fill_padded(_swap_halves(k), kswp, C_PAD)
        _fill_padded(v, vpad, C_PAD)
        _fill_padded(_swap_halves(v), vswp, C_PAD)

    start = pl.multiple_of(i * QB, QB)
    mlo, mhi = _lane_masks(jnp.bfloat16)
    lane = lax.broadcasted_iota(jnp.int32, (QB, LANES), 1)
    kchunk = lax.broadcasted_iota(jnp.int32, (QB, C_WIN), 1) // CHUNK
    pad_mask = jnp.where(kchunk >= C_LEFT_CHUNKS - 2 * i, 0.0, NEG)
    scale = HEAD_DIM ** -0.5 * LOG2E
    qs = [q_ref[:, p * LANES:(p + 1) * LANES] for p in range(C_GROUP)]
    npair = C_GROUP // 2
    stacks = []
    for straight in (True, False):
        kref, vref = (kpad, vpad) if straight else (kswp, vswp)
        kw = kref[pl.ds(start, C_WIN), :]
        vw = vref[pl.ds(start, C_WIN), :]
        halves = [int((p >= npair) == straight) for p in range(C_GROUP)]
        qg = jnp.concatenate([qs[p] * (mhi if halves[p] else mlo) for p in range(C_GROUP)], axis=0)
        s_all = lax.dot_general(qg, kw, _NT, preferred_element_type=jnp.float32)
        ps, ls = [], []
        for p in range(C_GROUP):
            h = 2 * p + halves[p]
            s = s_all[p * QB:(p + 1) * QB] * scale + bias_ref[h * QB:(h + 1) * QB, :] + pad_mask
            sink = sink_ref[h] * LOG2E
            m = jnp.maximum(jnp.max(s, axis=-1, keepdims=True), sink)
            e = jnp.exp2(s - m)
            ls.append(jnp.sum(e, axis=-1, keepdims=True) + jnp.exp2(sink - m))
            ps.append(e.astype(jnp.bfloat16))
        o_all = jnp.dot(jnp.concatenate(ps, axis=0), vw, preferred_element_type=jnp.float32)
        stacks.append(([o_all[p * QB:(p + 1) * QB] / ls[p] for p in range(C_GROUP)], halves))
    for p in range(C_GROUP):
        (o1, h1), (o2, _) = stacks
        lo, hi = (o2[p], o1[p]) if h1[p] else (o1[p], o2[p])
        o_ref[:, p * LANES:(p + 1) * LANES] = jnp.where(lane < HEAD_DIM, lo, hi).astype(o_ref.dtype)


def _attn_c(sinks, projc3, bias):
    nq = SEQ // QB
    pad_buf = pltpu.VMEM((SEQ + C_PAD, LANES), jnp.bfloat16)
    return pl.pallas_call(
        _attn_c_kernel,
        grid=(BATCH, nq),
        in_specs=[
            pl.BlockSpec(memory_space=pltpu.SMEM),
            pl.BlockSpec((None, QB, C_W), lambda b, i: (b, i, 0)),
            pl.BlockSpec((None, SEQ, LANES), lambda b, i: (b, 0, C_W // LANES)),
            pl.BlockSpec((None, SEQ, LANES), lambda b, i: (b, 0, C_W // LANES + 1)),
            pl.BlockSpec((C_Q_HEADS * QB, C_WIN), lambda b, i: (0, 0)),
        ],
        out_specs=pl.BlockSpec((None, QB, C_W), lambda b, i: (b, i, 0)),
        out_shape=jax.ShapeDtypeStruct((BATCH, SEQ, C_W), jnp.bfloat16),
        scratch_shapes=[pad_buf, pad_buf, pad_buf, pad_buf],
        compiler_params=_cparams(("arbitrary", "arbitrary")),
        name="attn_c",
    )(sinks, projc3, projc3, projc3, bias)


def _sort_key(x):
    bits = lax.bitcast_convert_type(x + 0.0, jnp.int32)
    return bits ^ ((bits >> 31) & jnp.int32(0x7FFFFFFF))


def _rows_tree(x, op, slab):
    parts = [x[r:r + slab, :] for r in range(0, x.shape[0], slab)]
    while len(parts) > 1:
        nxt = [op(parts[k], parts[k + 1]) for k in range(0, len(parts) - 1, 2)]
        parts = nxt + ([parts[-1]] if len(parts) % 2 else [])
    return parts[0]


def _rows8(x, op):
    return _rows_tree(x, op, SUBLANES)


BF16_ROWS = 16


def _bit_search(count_ge, target, v, above, top_bit):
    def step(t, carry):
        v, above = carry
        cand = v | (jnp.int32(1) << (top_bit - t))
        c = count_ge(cand)
        keep = c >= target
        return jnp.where(keep, cand, v), jnp.where(keep, above, c)

    return lax.fori_loop(0, top_bit + 1, step, (v, above))


def _attn_b_kernel(qb_ref, x_ref, kiw_ref, ckv_ref, gain_ref, wuk_ref, wuv_ref, bias_ref, tri_ref,
                   o_ref, ckvn, kd, ql, qi_all, sk, dg, acc_s, p_s):
    i = pl.program_id(1)
    nkb = (i + 1) * B_QK
    mlo, mhi = _lane_masks(jnp.bfloat16)
    f32 = jnp.float32

    @pl.when(i == 0)
    def _():
        ckvn[...] = _rms(ckv_ref[...].astype(f32), gain_ref[...]).astype(ckvn.dtype)
        kiw = kiw_ref[...].astype(f32)
        lane = lax.broadcasted_iota(jnp.int32, kiw.shape, 1)
        kd[...] = jnp.where(lane < HEAD_DIM, kiw, pltpu.roll(kiw, HEAD_DIM, 1)).astype(kd.dtype)

    for h in range(B_HEADS):
        rows = slice(h * BQ, (h + 1) * BQ)
        qlat = jnp.dot(qb_ref[:, (h // 2) * LANES:(h // 2 + 1) * LANES], wuk_ref[h], preferred_element_type=f32)
        ql[rows, :] = (qlat * (HEAD_DIM ** -0.5 * LOG2E)).astype(ql.dtype)
        qcol = O_QI - X_COL0 + (h // 2) * LANES
        qi_all[rows, :] = x_ref[:, qcol:qcol + LANES] * (mhi if h % 2 else mlo)
    wcol = O_KI - X_COL0
    wi_t = x_ref[:, wcol:wcol + LANES].astype(f32).T * (IDX_HEADS ** -0.5 * IDX_DIM ** -0.5)
    wi_rows = [wi_t[O_WI - O_KI + h:O_WI - O_KI + h + 1, :] for h in range(IDX_HEADS)]

    kpos = lax.broadcasted_iota(jnp.int32, (KB, BQ), 0)
    key_limit = ((i * BQ + lax.broadcasted_iota(jnp.int32, (1, BQ), 1)) // CHUNK + 1) * CHUNK

    def admissible(kb):
        return kb * KB + kpos < key_limit

    def score_block(kb, carry):
        kblk = kd[pl.ds(pl.multiple_of(kb * KB, KB), KB), :]
        dots = lax.dot_general(kblk, qi_all[...], _NT, preferred_element_type=f32)
        score = jnp.zeros((KB, BQ), f32)
        for h in range(IDX_HEADS):
            score = score + jnp.maximum(dots[:, h * BQ:(h + 1) * BQ], 0.0) * wi_rows[h]
        sk[kb] = _sort_key(jnp.where(admissible(kb), score, NEG))
        return carry

    lax.fori_loop(0, nkb, score_block, 0)

    neg_key = _sort_key(jnp.full((1, 1), NEG, f32))
    n_rest = ((NKB - nkb) * KB).astype(f32)
    one, zero = jnp.ones((), jnp.bfloat16), jnp.zeros((), jnp.bfloat16)

    def byte_of(key, byte):
        return ((key >> 24) + 128) if byte == 3 else ((key >> (8 * byte)) & 255)

    target = jnp.full((1, BQ), TOPK, f32)
    prefix = jnp.zeros((1, BQ), jnp.int32)
    above = jnp.zeros((1, BQ), f32)
    for byte in (3, 2, 1, 0):
        def in_class(key, byte=byte, prefix=prefix):
            return (key >> (8 * byte + 8)) == prefix

        def prepare(kb, carry, byte=byte, in_class=in_class):
            key = sk[kb]
            digit = byte_of(key, byte).astype(f32)
            if byte < 3:
                digit = jnp.where(in_class(key), digit, -1.0)
            dg[kb] = digit.astype(dg.dtype)
            return carry

        lax.fori_loop(0, nkb, prepare, 0)
        rest_digit = byte_of(neg_key, byte)
        rest_on = in_class(neg_key) if byte < 3 else (neg_key == neg_key)

        def count_ge(cand, rest_digit=rest_digit, rest_on=rest_on):
            cand_b = cand.astype(f32).astype(jnp.bfloat16)

            def body(kb, acc):
                hit = jnp.where(dg[kb] >= cand_b, one, zero)
                return acc + _rows_tree(hit, jnp.add, BF16_ROWS).astype(f32)

            acc = lax.fori_loop(0, nkb, body, jnp.zeros((BF16_ROWS, BQ), f32))
            rest = jnp.where(rest_on & (rest_digit >= cand), n_rest, 0.0)
            return jnp.sum(acc, axis=0, keepdims=True) + rest

        digit_thr, above_here = _bit_search(count_ge, target, jnp.zeros((1, BQ), jnp.int32),
                                            jnp.zeros((1, BQ), f32), 7)
        prefix = (digit_thr - 128) if byte == 3 else (prefix * 256 + digit_thr)
        above = above + above_here
        target = target - above_here
    thr = prefix
    need = TOPK - above

    acc_s[...] = jnp.zeros(acc_s.shape, f32)

    far_bias = bias_ref[B_NEAR, 0:1, :]

    def attend(far, edge, kb, carry):
        eq_seen, m_old, l_old = carry
        keys = sk[kb]
        eq = keys == thr
        eq_f = jnp.where(eq, 1.0, 0.0)
        before = jnp.dot(tri_ref[...], eq_f.astype(jnp.bfloat16), preferred_element_type=f32)
        tie_add = jnp.where(eq_seen + before < need, 0.0, NEG)
        mask_add = jnp.where(keys > thr, 0.0, jnp.where(eq, tie_add, NEG))
        if edge:
            mask_add = jnp.where(admissible(kb), mask_add, NEG)
        cblk = ckvn[pl.ds(pl.multiple_of(kb * KB, KB), KB), :]
        dots = lax.dot_general(cblk, ql[...], _NT, preferred_element_type=f32)
        tile = nkb - 1 - kb
        m_parts, l_parts, a_parts = [], [], []
        for g in range(HQ // LANES):
            cols = slice(g * LANES, (g + 1) * LANES)
            qcols = slice((g % (BQ // LANES)) * LANES, (g % (BQ // LANES) + 1) * LANES)
            mo = m_old[:, cols]
            if far:
                s = dots[:, cols] + mask_add[:, qcols]
                shift = far_bias[:, cols]
                mn = jnp.maximum(mo, jnp.max(_rows8(s, jnp.maximum), axis=0, keepdims=True) + shift)
                e = jnp.exp2(s - (mn - shift))
            else:
                s = dots[:, cols] + bias_ref[tile, :, cols] + mask_add[:, qcols]
                mn = jnp.maximum(mo, jnp.max(_rows8(s, jnp.maximum), axis=0, keepdims=True))
                e = jnp.exp2(s - mn)
            a = jnp.exp2(mo - mn)
            l_parts.append(a * l_old[:, cols] + jnp.sum(_rows8(e, jnp.add), axis=0, keepdims=True))
            m_parts.append(mn)
            a_parts.append(a)
            p_s[:, cols] = e.astype(p_s.dtype)
        alpha = jnp.concatenate(a_parts, axis=1)
        acc_s[...] = acc_s[...] * alpha + lax.dot_general(cblk, p_s[...], _TN, preferred_element_type=f32)
        eq_seen = eq_seen + jnp.sum(_rows8(eq_f, jnp.add), axis=0, keepdims=True)
        return eq_seen, jnp.concatenate(m_parts, axis=1), jnp.concatenate(l_parts, axis=1)

    init = (jnp.zeros((1, BQ), f32), jnp.full((1, HQ), 4 * NEG, f32), jnp.zeros((1, HQ), f32))
    n_far = jnp.maximum(nkb - B_NEAR, 0)
    n_full = nkb - B_QK
    carry = lax.fori_loop(0, n_far, functools.partial(attend, True, False), init)
    carry = lax.fori_loop(n_far, n_full, functools.partial(attend, False, False), carry)
    _, _, l_fin = lax.fori_loop(n_full, nkb, functools.partial(attend, False, True), carry)

    o_lat_t = acc_s[...] / l_fin
    for p in range(B_HEADS // 2):
        out = jnp.zeros((BQ, LANES), f32)
        for h in (2 * p, 2 * p + 1):
            o_lat = o_lat_t[:, h * BQ:(h + 1) * BQ].T.astype(jnp.bfloat16)
            out = out + jnp.dot(o_lat, wuv_ref[h], preferred_element_type=f32)
        o_ref[:, p * LANES:(p + 1) * LANES] = out.astype(o_ref.dtype)


def _attn_b(proj3, gain, wuk, wuv, bias, tri, nbatch=BATCH):
    nq = SEQ // BQ
    xw = IN_A - X_COL0
    return pl.pallas_call(
        _attn_b_kernel,
        grid=(nbatch, nq),
        in_specs=[
            pl.BlockSpec((None, BQ, B_W), lambda b, i: (b, i, O_QB // B_W)),
            pl.BlockSpec((None, BQ, xw), lambda b, i: (b, i, X_COL0 // xw)),
            pl.BlockSpec((None, SEQ, LANES), lambda b, i: (b, 0, O_KI // LANES)),
            pl.BlockSpec((None, SEQ, LANES), lambda b, i: (b, 0, O_CKV // LANES)),
            pl.BlockSpec((1, B_LATENT), lambda b, i: (0, 0)),
            pl.BlockSpec((B_HEADS, LANES, B_LATENT), lambda b, i: (0, 0, 0)),
            pl.BlockSpec((B_HEADS, B_LATENT, LANES), lambda b, i: (0, 0, 0)),
            pl.BlockSpec((B_NEAR + 1, KB, HQ), lambda b, i: (0, 0, 0), pipeline_mode=pl.Buffered(1)),
            pl.BlockSpec((KB, KB), lambda b, i: (0, 0)),
        ],
        out_specs=pl.BlockSpec((None, BQ, B_W), lambda b, i: (b, i, 0)),
        out_shape=jax.ShapeDtypeStruct((nbatch, SEQ, B_W), jnp.bfloat16),
        scratch_shapes=[
            pltpu.VMEM((SEQ, B_LATENT), jnp.bfloat16),
            pltpu.VMEM((SEQ, LANES), jnp.bfloat16),
            pltpu.VMEM((HQ, B_LATENT), jnp.bfloat16),
            pltpu.VMEM((HQ, LANES), jnp.bfloat16),
            pltpu.VMEM((NKB, KB, BQ), jnp.int32),
            pltpu.VMEM((NKB, KB, BQ), jnp.bfloat16),
            pltpu.VMEM((B_LATENT, HQ), jnp.float32),
            pltpu.VMEM((KB, HQ), jnp.bfloat16),
        ],
        compiler_params=_cparams(("arbitrary", "arbitrary")),
        name="attn_b",
    )(proj3, proj3, proj3, proj3, gain, wuk, wuv, bias, tri)


MERGE_BN = 1024
MERGE_BM = 1024


def _merge_kernel(oa_ref, ob_ref, oc_ref, ga_ref, gb_ref, gc_ref, wa_ref, wb_ref, wc_ref,
                  wo_in, wq_in, wmo_in, o_ref, wo_out, wq_out, wmo_out, wa, wb, wc):
    f32 = jnp.float32

    @pl.when(pl.program_id(1) == 0)
    def _():
        wa[...] = wa_ref[...].astype(wa.dtype)
        wb[...] = wb_ref[...].astype(wb.dtype)
        wc[...] = wc_ref[...].astype(wc.dtype)

    for src, dst in ((wo_in, wo_out), (wq_in, wq_out), (wmo_in, wmo_out)):
        dst[...] = src[...].astype(dst.dtype)

    m = ga_ref[...].astype(f32) * jnp.dot(oa_ref[...], wa[...], preferred_element_type=f32)
    m = m + gb_ref[...].astype(f32) * jnp.dot(ob_ref[...], wb[...], preferred_element_type=f32)
    m = m + gc_ref[...].astype(f32) * jnp.dot(oc_ref[...], wc[...], preferred_element_type=f32)
    o_ref[...] = m.astype(o_ref.dtype)


def _merge(oa, ob, oc, gates, w_branch, layer, next_weights):
    t = oa.shape[0]
    bn, bm = MERGE_BN, MERGE_BM
    gstep = D_MODEL // bn
    ni = t // bm
    nsteps = (D_MODEL // bn) * ni
    slabs = [w.shape[1] // nsteps for w in next_weights]
    assert all(s * nsteps == w.shape[1] and s % BF16_ROWS == 0 for s, w in zip(slabs, next_weights))
    side_in = [pl.BlockSpec((None, s, w.shape[2]), lambda j, i: (layer, j * ni + i, 0))
               for s, w in zip(slabs, next_weights)]
    side_out = [pl.BlockSpec((s, w.shape[2]), lambda j, i: (j * ni + i, 0)) for s, w in zip(slabs, next_weights)]
    res = pl.pallas_call(
        _merge_kernel,
        grid=(D_MODEL // bn, ni),
        in_specs=[
            pl.BlockSpec((bm, A_W), lambda j, i: (i, 0)),
            pl.BlockSpec((bm, B_W), lambda j, i: (i, 0)),
            pl.BlockSpec((bm, C_W), lambda j, i: (i, 0)),
            pl.BlockSpec((bm, bn), lambda j, i: (i, j)),
            pl.BlockSpec((bm, bn), lambda j, i: (i, gstep + j)),
            pl.BlockSpec((bm, bn), lambda j, i: (i, 2 * gstep + j)),
            pl.BlockSpec((None, A_W, bn), lambda j, i: (layer, 0, j)),
            pl.BlockSpec((None, B_W, bn), lambda j, i: (layer, A_W // B_W, j)),
            pl.BlockSpec((None, C_W, bn), lambda j, i: (layer, (A_W + B_W) // C_W, j)),
        ] + side_in,
        out_specs=[pl.BlockSpec((bm, bn), lambda j, i: (i, j))] + side_out,
        out_shape=[jax.ShapeDtypeStruct((t, D_MODEL), jnp.bfloat16)]
        + [jax.ShapeDtypeStruct(w.shape[1:], jnp.bfloat16) for w in next_weights],
        scratch_shapes=[pltpu.VMEM((A_W, bn), jnp.bfloat16), pltpu.VMEM((B_W, bn), jnp.bfloat16),
                        pltpu.VMEM((C_W, bn), jnp.bfloat16)],
        compiler_params=_cparams(("arbitrary", "arbitrary")),
        name="merge",
    )(oa, ob, oc, gates, gates, gates, w_branch, w_branch, w_branch, *next_weights)
    return res[0], res[1:]


FINISH_CHUNKS = 4


def _finish_chunked(y_of_rows, nrows, x_ref, gp_ref, gn_ref, xo_ref, ho_ref):
    step = nrows // FINISH_CHUNKS
    for r in range(0, nrows, step):
        rows = slice(r, r + step)
        xn = x_ref[rows, :] + _rms(y_of_rows(rows), gp_ref[...])
        xo_ref[rows, :] = xn
        if ho_ref is not None:
            ho_ref[rows, :] = _rms(xn, gn_ref[...]).astype(ho_ref.dtype)


def _gemm_res_kernel(nk, with_next, a_ref, w_ref, x_ref, gp_ref, gn_ref, xo_ref, *rest):
    ho_ref = rest[0] if with_next else None
    f32 = jnp.float32
    bm = a_ref.shape[0]
    acc = rest[-1]
    k = pl.program_id(1)

    step = bm // FINISH_CHUNKS

    @pl.when(k == 0)
    def _():
        for r in range(0, bm, step):
            acc[r:r + step, :] = jnp.dot(a_ref[r:r + step, :], w_ref[...], preferred_element_type=f32)

    @pl.when((k > 0) & (k < nk - 1))
    def _():
        for r in range(0, bm, step):
            acc[r:r + step, :] += jnp.dot(a_ref[r:r + step, :], w_ref[...], preferred_element_type=f32)

    @pl.when(k == nk - 1)
    def _():
        _finish_chunked(lambda rows: acc[rows, :] + jnp.dot(a_ref[rows, :], w_ref[...], preferred_element_type=f32),
                        bm, x_ref, gp_ref, gn_ref, xo_ref, ho_ref)


def _gemm_res(a, w, layer, x, g_post, g_next, bk, bm=512, name="gemm_res"):
    t, kdim = a.shape
    n = w.shape[2]
    nk = kdim // bk
    assert nk > 1
    with_next = g_next is not None
    if g_next is None:
        g_next = g_post
    out_shape = [jax.ShapeDtypeStruct((t, n), jnp.float32)]
    out_specs = [pl.BlockSpec((bm, n), lambda i, k: (i, 0))]
    if with_next:
        out_shape.append(jax.ShapeDtypeStruct((t, n), jnp.bfloat16))
        out_specs.append(pl.BlockSpec((bm, n), lambda i, k: (i, 0)))
    res = pl.pallas_call(
        functools.partial(_gemm_res_kernel, nk, with_next),
        grid=(t // bm, nk),
        in_specs=[
            pl.BlockSpec((bm, bk), lambda i, k: (i, k)),
            pl.BlockSpec((None, bk, n), lambda i, k: (layer, k, 0)),
            pl.BlockSpec((bm, n), lambda i, k: (i, 0)),
            pl.BlockSpec((1, n), lambda i, k: (0, 0)),
            pl.BlockSpec((1, n), lambda i, k: (0, 0)),
        ],
        out_specs=out_specs,
        out_shape=out_shape,
        scratch_shapes=[pltpu.VMEM((bm, n), jnp.float32)],
        compiler_params=_cparams(("parallel", "arbitrary")),
        name=name,
    )(a, w, x, g_post, g_next)
    return (res[0], res[1]) if with_next else (res[0], None)


def _cast_once(pairs):
    @pl.when(pl.program_id(0) == 0)
    def _():
        for src, dst in pairs:
            dst[...] = src[...].astype(dst.dtype)


def _memkv_kernel(m_ref, g_ref, w_ref, o_ref, wb):
    _cast_once([(w_ref, wb)])
    mn = _rms(m_ref[...], g_ref[...]).astype(jnp.bfloat16)
    o_ref[...] = jnp.dot(mn, wb[...], preferred_element_type=jnp.float32).astype(o_ref.dtype)


def _memkv(mem2, g, w, layer):
    t, d = mem2.shape
    n = w.shape[2]
    bm = 512
    return pl.pallas_call(
        _memkv_kernel,
        grid=(t // bm,),
        in_specs=[pl.BlockSpec((bm, d), lambda i: (i, 0)), pl.BlockSpec((1, d), lambda i: (0, 0)),
                  pl.BlockSpec((None, d, n), lambda i: (layer, 0, 0))],
        out_specs=pl.BlockSpec((bm, n), lambda i: (i, 0)),
        out_shape=jax.ShapeDtypeStruct((t, n), jnp.bfloat16),
        scratch_shapes=[pltpu.VMEM((d, n), jnp.bfloat16)],
        compiler_params=_cparams(("arbitrary",)),
        name="mem_kv",
    )(mem2, g, w)


XA_BM = 512


def _mix_out_xattn_kernel(m_ref, wo_ref, wq_ref, kv_ref, wmo_ref, x_ref, g1_ref, g2_ref, g3_ref, g4_ref,
                          xo_ref, ho_ref, x1_s, h1_s):
    f32 = jnp.float32
    bm = m_ref.shape[0]
    _finish_chunked(lambda rows: jnp.dot(m_ref[rows, :], wo_ref[...], preferred_element_type=f32),
                    bm, x_ref, g1_ref, g2_ref, x1_s, h1_s)
    q = jnp.dot(h1_s[...], wq_ref[...], preferred_element_type=f32).astype(jnp.bfloat16)
    scale = MEM_HEAD_DIM ** -0.5 * LOG2E
    hcols = [slice(h * LANES, (h + 1) * LANES) for h in range(MEM_HEADS)]
    ss = [lax.dot_general(q[:, c], kv_ref[:, c], _NT, preferred_element_type=f32) * scale for c in hcols]
    ms = [jnp.max(s, axis=-1, keepdims=True) for s in ss]
    es = [jnp.exp2(s - m) for s, m in zip(ss, ms)]
    ls = [jnp.sum(e, axis=-1, keepdims=True) for e in es]
    pv = [jnp.dot(e.astype(jnp.bfloat16), kv_ref[:, MEM_W + h * LANES:MEM_W + (h + 1) * LANES],
                  preferred_element_type=f32) for h, e in enumerate(es)]
    o = jnp.concatenate([(o_h / l).astype(jnp.bfloat16) for o_h, l in zip(pv, ls)], axis=-1)
    _finish_chunked(lambda rows: jnp.dot(o[rows, :], wmo_ref[...], preferred_element_type=f32),
                    bm, x1_s, g3_ref, g4_ref, xo_ref, ho_ref)


def _mix_out_xattn(merged, w_o, w_mq, kv, w_mo, layer, x, g1, g2, g3, g4):
    t, d = merged.shape
    bm = XA_BM
    per_batch = SEQ // bm
    once = pl.Buffered(1)
    row = pl.BlockSpec((bm, d), lambda i: (i, 0))
    gain = pl.BlockSpec((1, d), lambda i: (0, 0))
    return pl.pallas_call(
        _mix_out_xattn_kernel,
        grid=(t // bm,),
        in_specs=[
            row,
            pl.BlockSpec((None, d, d), lambda i: (layer, 0, 0), pipeline_mode=once),
            pl.BlockSpec((None, d, MEM_W), lambda i: (layer, 0, 0), pipeline_mode=once),
            pl.BlockSpec((MEM_LEN, 2 * MEM_W), lambda i: (i // per_batch, 0)),
            pl.BlockSpec((None, MEM_W, d), lambda i: (layer, 0, 0), pipeline_mode=once),
            row, gain, gain, gain, gain,
        ],
        out_specs=[row, row],
        out_shape=[jax.ShapeDtypeStruct((t, d), jnp.float32), jax.ShapeDtypeStruct((t, d), jnp.bfloat16)],
        scratch_shapes=[pltpu.VMEM((bm, d), jnp.float32), pltpu.VMEM((bm, d), jnp.bfloat16)],
        compiler_params=_cparams(("parallel",)),
        name="mix_out_xattn",
    )(merged, w_o, w_mq, kv, w_mo, x, g1, g2, g3, g4)


def _ffn_up_kernel(h_ref, wgf_ref, wvf_ref, cwg_ref, cwv_ref, cbg_ref, cbv_ref, wd_ref, o_ref, wdo_ref,
                   wg_ref, wv_ref, ug, uv):
    j = pl.program_id(0)
    i = pl.program_id(1)

    slab = j * pl.num_programs(1) + i

    @pl.when(slab < D_FF // wd_ref.shape[0])
    def _():
        wdo_ref[...] = wd_ref[...].astype(wdo_ref.dtype)

    @pl.when(slab >= D_FF // wd_ref.shape[0])
    def _():
        wdo_ref[...] = jnp.zeros(wdo_ref.shape, wdo_ref.dtype)

    tiles_per_seq = SEQ // FF_BM
    last = FF_P // FF_BN - 1
    valid = D_FF - last * FF_BN
    shift = FF_BN - valid

    @pl.when((i == 0) & (j < last))
    def _():
        wg_ref[...] = wgf_ref[...].astype(wg_ref.dtype)
        wv_ref[...] = wvf_ref[...].astype(wv_ref.dtype)

    @pl.when((i == 0) & (j == last))
    def _():
        zeros = jnp.zeros((wg_ref.shape[0], FF_BN - valid), wg_ref.dtype)
        wg_ref[:, :valid] = wgf_ref[:, :valid].astype(wg_ref.dtype)
        wv_ref[:, :valid] = wvf_ref[:, shift:].astype(wv_ref.dtype)
        wg_ref[:, valid:] = zeros
        wv_ref[:, valid:] = zeros

    for u in (ug, uv):
        @pl.when(i % tiles_per_seq == 0)
        def _():
            u[0:SUBLANES, :] = jnp.zeros((SUBLANES, FF_BN), jnp.float32)

        @pl.when(i % tiles_per_seq != 0)
        def _():
            u[0:SUBLANES, :] = u[FF_BM:FF_BM + SUBLANES, :]

    def conv(u, cw_ref, cb_ref, r0):
        base = SUBLANES + r0
        acc = cb_ref[...] + u[base - 2:base - 2 + FF_CH, :] * cw_ref[0:1, :]
        acc = acc + u[base - 1:base - 1 + FF_CH, :] * cw_ref[1:2, :]
        return acc + u[base:base + FF_CH, :] * cw_ref[2:3, :]

    for c in range(FF_BM // FF_CH):
        r0 = c * FF_CH
        hb = h_ref[r0:r0 + FF_CH, :]
        ug[SUBLANES + r0:SUBLANES + r0 + FF_CH, :] = jnp.dot(hb, wg_ref[...], preferred_element_type=jnp.float32)
        uv[SUBLANES + r0:SUBLANES + r0 + FF_CH, :] = jnp.dot(hb, wv_ref[...], preferred_element_type=jnp.float32)
        gate = conv(ug, cwg_ref, cbg_ref, r0)
        val = conv(uv, cwv_ref, cbv_ref, r0)
        o_ref[r0:r0 + FF_CH, :] = (jax.nn.gelu(gate) * val).astype(o_ref.dtype)


def _ffn_up(h, w_up, w_down, layer, conv_w, conv_b):
    t, d = h.shape
    nj = FF_P // FF_BN
    ni = t // FF_BM
    slab = FF_P // (nj * ni)
    assert slab * nj * ni == FF_P and D_FF % slab == 0 and slab % BF16_ROWS == 0
    n_real = D_FF // slab
    w_block = (None, pl.Element(d), pl.Element(FF_BN))
    return pl.pallas_call(
        _ffn_up_kernel,
        grid=(nj, ni),
        in_specs=[
            pl.BlockSpec((FF_BM, d), lambda j, i: (i, 0)),
            pl.BlockSpec(w_block, lambda j, i: (layer, 0, pl.multiple_of(j * FF_BN, LANES))),
            pl.BlockSpec(w_block, lambda j, i: (
                layer, 0, pl.multiple_of(jnp.minimum(D_FF + j * FF_BN, 2 * D_FF - FF_BN), LANES))),
            pl.BlockSpec((CONV_W, FF_BN), lambda j, i: (0, j)),
            pl.BlockSpec((CONV_W, FF_BN), lambda j, i: (0, nj + j)),
            pl.BlockSpec((1, FF_BN), lambda j, i: (0, j)),
            pl.BlockSpec((1, FF_BN), lambda j, i: (0, nj + j)),
            pl.BlockSpec((None, slab, d), lambda j, i: (layer, jnp.minimum(j * ni + i, n_real - 1), 0)),
        ],
        out_specs=[pl.BlockSpec((FF_BM, FF_BN), lambda j, i: (i, j)),
                   pl.BlockSpec((slab, d), lambda j, i: (j * ni + i, 0))],
        out_shape=[jax.ShapeDtypeStruct((t, FF_P), jnp.bfloat16),
                   jax.ShapeDtypeStruct((FF_P, w_down.shape[2]), jnp.bfloat16)],
        scratch_shapes=[pltpu.VMEM((d, FF_BN), jnp.bfloat16), pltpu.VMEM((d, FF_BN), jnp.bfloat16),
                        pltpu.VMEM((FF_BM + SUBLANES, FF_BN), jnp.float32),
                        pltpu.VMEM((FF_BM + SUBLANES, FF_BN), jnp.float32)],
        compiler_params=_cparams(("arbitrary", "arbitrary")),
        name="ffn_up",
    )(h, w_up, w_up, conv_w, conv_w, conv_b, conv_b, w_down)


def _pad_heads(w, axis):
    h = w.shape[0]
    zero = jnp.zeros_like(w)
    even = jnp.concatenate([w, zero], axis=axis)
    odd = jnp.concatenate([zero, w], axis=axis)
    sel = (jnp.arange(h) % 2 == 0).reshape((h, 1, 1))
    return jnp.where(sel, even, odd)


def _toeplitz(fn, rows, cols):
    ks = np.concatenate([np.arange(0, cols), np.arange(-(rows - 1), 0)])
    w = fn(ks)
    h, period = w.shape
    x = jnp.tile(w, (1, rows))[:, :rows * (period - 1)].reshape(h, rows, period - 1)
    return x[:, :, :cols].astype(jnp.float32)


def _band(rows, cols, left):
    diff = left + np.arange(rows)[:, None] // CHUNK - np.arange(cols)[None, :] // CHUNK
    return (diff >= 0) & (diff <= left)


def _bias_a(rel_bias):
    fn = lambda ks: rel_bias[np.clip(A_PAD - ks, -A_MAX_REL, A_MAX_REL) + A_MAX_REL].T
    bias = jnp.where(_band(QB, A_WIN, A_LEFT_CHUNKS)[None], _toeplitz(fn, QB, A_WIN) * LOG2E, NEG)
    return bias.reshape(A_HEADS * QB, A_WIN)


def _bias_c(t5_c):
    fn = lambda ks: t5_c[_t5_bucket(jnp.asarray(ks - C_PAD, jnp.int32))].T
    bias = jnp.where(_band(QB, C_WIN, C_LEFT_CHUNKS)[None], _toeplitz(fn, QB, C_WIN) * LOG2E, NEG)
    return bias.reshape(C_Q_HEADS * QB, C_WIN)


def _bias_b(t5_b):
    tiles = []
    for n in range(B_NEAR + 1):
        off = KB * (n - (B_QK - 1)) if n < B_NEAR else SEQ
        fn = lambda ks, off=off: t5_b[_t5_bucket(jnp.asarray(-ks - off, jnp.int32))].T
        tile = _toeplitz(fn, KB, BQ)
        tiles.append(jnp.transpose(tile * LOG2E, (1, 0, 2)).reshape(KB, HQ))
    return jnp.stack(tiles)


def _pad_ff(a, dtype):
    z = jnp.zeros((a.shape[0], FF_P - D_FF), dtype)
    return jnp.concatenate([a[:, :D_FF].astype(dtype), z, a[:, D_FF:].astype(dtype), z], axis=1)


def kernel(x, mem, t5_table, norm_gains, w_in, a_rel_bias, ckv_gain, w_uk, w_uv, sinks, w_branch, w_o,
           mem_gain, w_mq, w_mkv, w_mo, w_up, conv_w, conv_b, w_down):
    bf16 = jnp.bfloat16
    xs = x.reshape(TOKENS, D_MODEL)
    mem2 = mem.reshape(BATCH * MEM_LEN, D_MODEL)
    tri = jnp.asarray(np.tril(np.ones((KB, KB), np.float32), -1), bf16)
    bias_b = _bias_b(t5_table[:, :B_HEADS])
    bias_c = _bias_c(t5_table[:, B_HEADS:])
    gains = norm_gains.reshape(DEPTH, 6, 1, D_MODEL)
    w_in_t = jnp.swapaxes(w_in, 1, 2)

    h = None
    for l in range(DEPTH):
        g = gains[l]
        if h is None:
            projc, h = _norm_proj(xs, g[0], w_in_t, l, O_QC, C_COLS)
        else:
            projc = _proj(h, w_in_t, l, O_QC, C_COLS, C_COLS, gate=False, name="in_proj_c")
        proja = _proj(h, w_in_t, l, 0, IN_A, PROJ_BN, gate=False, name="in_proj_a")
        gates = _proj(h, w_in_t, l, O_GL, 3 * D_MODEL, PROJ_BN, gate=True, name="in_proj_g")
        proja3 = proja.reshape(BATCH, SEQ, IN_A)
        oa = _attn_a(proja3, _bias_a(a_rel_bias[l]))
        wuk = jnp.transpose(_pad_heads(w_uk[l], axis=2), (0, 2, 1)).astype(bf16)
        wuv = _pad_heads(w_uv[l], axis=2).astype(bf16)
        ob = _attn_b(proja3, ckv_gain[l].reshape(1, B_LATENT), wuk, wuv, bias_b, tri)
        oc = _attn_c(sinks[l], projc.reshape(BATCH, SEQ, C_COLS), bias_c)
        merged, (w_o_b, w_mq_b, w_mo_b) = _merge(oa.reshape(TOKENS, A_W), ob.reshape(TOKENS, B_W),
                                                 oc.reshape(TOKENS, C_W), gates, w_branch, l, (w_o, w_mq, w_mo))
        kv = _memkv(mem2, mem_gain[l].reshape(1, D_MODEL), w_mkv, l)
        xs, h = _mix_out_xattn(merged, w_o_b[None], w_mq_b[None], kv, w_mo_b[None], 0, xs, g[1], g[2], g[3], g[4])
        hidden, w_dn_b = _ffn_up(h, w_up, w_down, l, _pad_ff(conv_w[l], jnp.float32),
                                 _pad_ff(conv_b[l].reshape(1, -1), jnp.float32))
        g_next = gains[l + 1, 0] if l + 1 < DEPTH else None
        xs, h = _gemm_res(hidden, w_dn_b[None], 0, xs, g[5], g_next, bk=DOWN_BK, name="ffn_down")
    return xs.reshape(BATCH, SEQ, D_MODEL)
```
